```python
import math
import jax
import jax.numpy as jnp
from jax import lax
import numpy as np

D_MODEL = 1024
BATCH = 8
SEQ = 2048
DEPTH = 1

CHUNK = 64
RMS_EPS = 1e-6
N_MOD = 6

FOX_HEADS = 8
FOX_HEAD_DIM = 64
FOX_WIDTH = FOX_HEADS * FOX_HEAD_DIM
Q_BLOCK = 128

SSM_WIDTH = 512
SSM_GROUP = 16
SSM_GROUPS = SSM_WIDTH // SSM_GROUP
SSM_STATE = 64
DT_MIN = 1e-3
DT_MAX = 1e-1
LAMBDA_RE_MAX = -1e-4

IN_SIZES = (FOX_WIDTH, FOX_WIDTH, FOX_WIDTH, FOX_HEADS, SSM_WIDTH, D_MODEL, D_MODEL)
IN_COLS = sum(IN_SIZES)

N_GROUPS = 4
EXPERTS_PER_GROUP = 8
N_EXPERTS = N_GROUPS * EXPERTS_PER_GROUP
TOP_K = 2
D_EXPERT = 512
MOE_BLOCK = 128

kernel_name = 'hybrid_fox_s5_hmoe_block'


def _in_splits():
    idx, acc = [], 0
    for s in IN_SIZES[:-1]:
        acc += s
        idx.append(acc)
    return idx


def _rmsnorm(x, g):
    xf = x.astype(jnp.float32)
    xf = xf * lax.rsqrt(jnp.mean(xf * xf, axis=-1, keepdims=True) + RMS_EPS)
    return xf.astype(x.dtype) * g


def _modulate(h, shift, scale):
    return h * (1.0 + scale[:, None, :]) + shift[:, None, :]


def _forgetting_attention(q, k, v, log_f):
    bsz, seq, nh, hd = q.shape
    nqb = seq // Q_BLOCK
    q = q.transpose(0, 2, 1, 3)
    k = k.transpose(0, 2, 1, 3)
    v = v.transpose(0, 2, 1, 3)
    cum = jnp.cumsum(log_f, axis=1).transpose(0, 2, 1)
    q_blocks = q.reshape(bsz, nh, nqb, Q_BLOCK, hd).transpose(2, 0, 1, 3, 4)
    cum_blocks = cum.reshape(bsz, nh, nqb, Q_BLOCK).transpose(2, 0, 1, 3)
    k_pos = jnp.arange(seq)
    scale = hd ** -0.5

    def one_block(args):
        qb, cb, i = args
        s = jnp.einsum('bhqd,bhkd->bhqk', qb, k).astype(jnp.float32) * scale
        s = s + cb[..., :, None] - cum[..., None, :]
        q_pos = i * Q_BLOCK + jnp.arange(Q_BLOCK)
        s = jnp.where(k_pos[None, :] <= q_pos[:, None], s, -jnp.inf)
        p = jax.nn.softmax(s, axis=-1).astype(v.dtype)
        return jnp.einsum('bhqk,bhkd->bhqd', p, v)

    out = lax.map(one_block, (q_blocks, cum_blocks, jnp.arange(nqb)))
    return out.transpose(1, 0, 3, 2, 4).reshape(bsz, seq, nh * hd)


def _complex_affine_combine(e1, e2):
    a1r, a1i, b1r, b1i = e1
    a2r, a2i, b2r, b2i = e2
    ar = a2r * a1r - a2i * a1i
    ai = a2r * a1i + a2i * a1r
    br = a2r * b1r - a2i * b1i + b2r
    bi = a2r * b1i + a2i * b1r + b2i
    return ar, ai, br, bi


def _s5_glu(u, lam_re, lam_im, log_dt, b_re, b_im, c_re, c_im, d_skip, w_glu):
    f32 = jnp.float32
    bsz, seq, _ = u.shape
    uf = u.astype(f32)
    ug = uf.reshape(bsz, seq, SSM_GROUPS, SSM_GROUP)
    lam_re = jnp.minimum(lam_re.astype(f32), LAMBDA_RE_MAX)
    lam_im = lam_im.astype(f32)
    dt = jnp.exp(log_dt.astype(f32))[:, None]
    mag = jnp.exp(lam_re * dt)
    a_re = mag * jnp.cos(lam_im * dt)
    a_im = mag * jnp.sin(lam_im * dt)
    den = lam_re * lam_re + lam_im * lam_im
    nr = a_re - 1.0
    co_re = (nr * lam_re + a_im * lam_im) / den
    co_im = (a_im * lam_re - nr * lam_im) / den
    b_re = b_re.astype(f32)
    b_im = b_im.astype(f32)
    bb_re = co_re[..., None] * b_re - co_im[..., None] * b_im
    bb_im = co_re[..., None] * b_im + co_im[..., None] * b_re
    bu_re = jnp.einsum('bsgc,gpc->bsgp', ug, bb_re)
    bu_im = jnp.einsum('bsgc,gpc->bsgp', ug, bb_im)
    a_re_t = jnp.broadcast_to(a_re, bu_re.shape)
    a_im_t = jnp.broadcast_to(a_im, bu_re.shape)
    _, _, x_re, x_im = lax.associative_scan(
        _complex_affine_combine, (a_re_t, a_im_t, bu_re, bu_im), axis=1)
    y = (jnp.einsum('gcp,bsgp->bsgc', c_re.astype(f32), x_re)
         - jnp.einsum('gcp,bsgp->bsgc', c_im.astype(f32), x_im))
    y = y.reshape(bsz, seq, SSM_WIDTH) + d_skip.astype(f32) * uf
    y = jax.nn.gelu(y).astype(u.dtype)
    lin, gate = jnp.split(y @ w_glu, 2, axis=-1)
    return lin * jax.nn.sigmoid(gate)


def _hier_moe(h, w_rg, b_rg, w_rexp, b_rexp, w_gate, w_up, w_down):
    bsz, seq, d = h.shape
    n_tok = bsz * seq
    hf = h.reshape(n_tok, d)
    lg = (hf @ w_rg).astype(jnp.float32) + b_rg.astype(jnp.float32)
    pg = jax.nn.softmax(lg, axis=-1)
    gi = jnp.argmax(lg, axis=-1).astype(jnp.int32)
    pg_sel = jnp.take_along_axis(pg, gi[:, None], axis=-1)[:, 0]
    le = (hf @ w_rexp).astype(jnp.float32) + b_rexp.astype(jnp.float32)
    le = le.reshape(n_tok, N_GROUPS, EXPERTS_PER_GROUP)
    le_sel = jnp.take_along_axis(le, gi[:, None, None], axis=1)[:, 0]
    top_v, top_i = lax.top_k(le_sel, TOP_K)
    weights = pg_sel[:, None] * jax.nn.softmax(top_v, axis=-1)
    eid = gi[:, None] * EXPERTS_PER_GROUP + top_i.astype(jnp.int32)
    n_assign = n_tok * TOP_K
    flat_e = eid.reshape(-1)
    flat_t = jnp.repeat(jnp.arange(n_tok, dtype=jnp.int32), TOP_K)
    flat_w = weights.reshape(-1)
    order = jnp.argsort(flat_e)
    se = flat_e[order]
    counts = jnp.bincount(flat_e, length=N_EXPERTS)
    starts = jnp.cumsum(counts) - counts
    pcounts = ((counts + MOE_BLOCK - 1) // MOE_BLOCK) * MOE_BLOCK
    pends = jnp.cumsum(pcounts)
    pstarts = pends - pcounts
    dest = pstarts[se] + (jnp.arange(n_assign) - starts[se])
    rows = n_assign + N_EXPERTS * MOE_BLOCK
    n_blocks = rows // MOE_BLOCK
    row_tok = jnp.full((rows,), n_tok, jnp.int32).at[dest].set(flat_t[order])
    row_w = jnp.zeros((rows,), jnp.float32).at[dest].set(flat_w[order])
    blk_e = jnp.minimum(jnp.searchsorted(pends, jnp.arange(n_blocks) * MOE_BLOCK, side='right'),
                        N_EXPERTS - 1)
    x_pad = jnp.concatenate([hf, jnp.zeros((1, d), hf.dtype)], axis=0)
    x_rows = x_pad[row_tok].reshape(n_blocks, MOE_BLOCK, d)

    def run_block(args):
        xb, e = args
        hid = jax.nn.silu(xb @ w_gate[e]) * (xb @ w_up[e])
        return hid @ w_down[e]

    y_rows = lax.map(run_block, (x_rows, blk_e)).reshape(rows, d)
    y = jnp.zeros((n_tok + 1, d), h.dtype).at[row_tok].add(y_rows * row_w[:, None].astype(h.dtype))
    return y[:n_tok].reshape(bsz, seq, d)


def setup_inputs(seed: int = 0) -> dict:
    key = jax.random.key(seed)
    ks = jax.random.split(key, 32)
    L, D, f32 = DEPTH, D_MODEL, jnp.float32

    def nrm(k, shape, scale):
        return jax.random.normal(k, shape, f32) * scale

    x = nrm(ks[0], (BATCH, SEQ, D), 1.0)
    c = nrm(ks[1], (BATCH, D), 1.0)
    w_ada = nrm(ks[2], (L, D, N_MOD * D), 0.02)
    b_ada = (jnp.repeat(jnp.array([0.0, 0.0, 1.0, 0.0, 0.0, 1.0], f32), D)[None]
             + nrm(ks[3], (L, N_MOD * D), 0.02))
    norm_mix_g = 1.0 + nrm(ks[4], (L, D), 0.02)
    w_in = nrm(ks[5], (L, D, IN_COLS), D ** -0.5)
    b_forget = 3.0 + nrm(ks[6], (L, FOX_HEADS), 0.5)
    w_out_fox = nrm(ks[7], (L, FOX_WIDTH, D), FOX_WIDTH ** -0.5)
    n_idx = jnp.arange(SSM_STATE, dtype=f32)
    lambda_re = -0.5 + nrm(ks[8], (L, SSM_GROUPS, SSM_STATE), 0.01)
    lambda_im = math.pi * n_idx + nrm(ks[9], (L, SSM_GROUPS, SSM_STATE), 0.01)
    log_dt = jax.random.uniform(ks[10], (L, SSM_GROUPS), f32, math.log(DT_MIN), math.log(DT_MAX))
    ssm_b_re = nrm(ks[11], (L, SSM_GROUPS, SSM_STATE, SSM_GROUP), (2 * SSM_GROUP) ** -0.5)
    ssm_b_im = nrm(ks[12], (L, SSM_GROUPS, SSM_STATE, SSM_GROUP), (2 * SSM_GROUP) ** -0.5)
    ssm_c_re = nrm(ks[13], (L, SSM_GROUPS, SSM_GROUP, SSM_STATE), SSM_STATE ** -0.5)
    ssm_c_im = nrm(ks[14], (L, SSM_GROUPS, SSM_GROUP, SSM_STATE), SSM_STATE ** -0.5)
    d_skip = nrm(ks[15], (L, SSM_WIDTH), 1.0)
    w_glu = nrm(ks[16], (L, SSM_WIDTH, 2 * SSM_WIDTH), SSM_WIDTH ** -0.5)
    w_out_ssm = nrm(ks[17], (L, SSM_WIDTH, D), SSM_WIDTH ** -0.5)
    w_o = nrm(ks[18], (L, D, D), D ** -0.5)
    norm_ffn_g = 1.0 + nrm(ks[19], (L, D), 0.02)
    w_router_group = nrm(ks[20], (L, D, N_GROUPS), D ** -0.5)
    b_router_group = nrm(ks[21], (L, N_GROUPS), 0.01)
    w_router_expert = nrm(ks[22], (L, D, N_EXPERTS), D ** -0.5)
    b_router_expert = nrm(ks[23], (L, N_EXPERTS), 0.01)
    w_gate_e = nrm(ks[24], (L, N_EXPERTS, D, D_EXPERT), D ** -0.5)
    w_up_e = nrm(ks[25], (L, N_EXPERTS, D, D_EXPERT), D ** -0.5)
    w_down_e = nrm(ks[26], (L, N_EXPERTS, D_EXPERT, D), D_EXPERT ** -0.5)
    final_g = 1.0 + nrm(ks[27], (D,), 0.02)
    return {'x': x, 'c': c, 'w_ada': w_ada, 'b_ada': b_ada, 'norm_mix_g': norm_mix_g,
            'w_in': w_in, 'b_forget': b_forget, 'w_out_fox': w_out_fox,
            'lambda_re': lambda_re, 'lambda_im': lambda_im, 'log_dt': log_dt,
            'ssm_b_re': ssm_b_re, 'ssm_b_im': ssm_b_im, 'ssm_c_re': ssm_c_re, 'ssm_c_im': ssm_c_im,
            'd_skip': d_skip, 'w_glu': w_glu, 'w_out_ssm': w_out_ssm, 'w_o': w_o,
            'norm_ffn_g': norm_ffn_g, 'w_router_group': w_router_group,
            'b_router_group': b_router_group, 'w_router_expert': w_router_expert,
            'b_router_expert': b_router_expert, 'w_gate_e': w_gate_e, 'w_up_e': w_up_e,
            'w_down_e': w_down_e, 'final_g': final_g}


def reference(x, c, w_ada, b_ada, norm_mix_g, w_in, b_forget, w_out_fox,
              lambda_re, lambda_im, log_dt, ssm_b_re, ssm_b_im, ssm_c_re, ssm_c_im,
              d_skip, w_glu, w_out_ssm, w_o, norm_ffn_g, w_router_group,
              b_router_group, w_router_expert, b_router_expert, w_gate_e, w_up_e,
              w_down_e, final_g):
    bsz, seq = x.shape[0], x.shape[1]
    c_act = jax.nn.silu(c)
    for l in range(DEPTH):
        mod = c_act @ w_ada[l] + b_ada[l]
        sh1, sc1, g1, sh2, sc2, g2 = jnp.split(mod, N_MOD, axis=-1)
        h = _modulate(_rmsnorm(x, norm_mix_g[l]), sh1, sc1)
        q, k, v, f_logit, u, ga, gb = jnp.split(h @ w_in[l], _in_splits(), axis=-1)
        log_f = jax.nn.log_sigmoid((f_logit + b_forget[l]).astype(jnp.float32))
        hs = (bsz, seq, FOX_HEADS, FOX_HEAD_DIM)
        o_fox = _forgetting_attention(q.reshape(hs), k.reshape(hs), v.reshape(hs), log_f)
        o_ssm = _s5_glu(u, lambda_re[l], lambda_im[l], log_dt[l], ssm_b_re[l], ssm_b_im[l],
                        ssm_c_re[l], ssm_c_im[l], d_skip[l], w_glu[l])
        merged = (jax.nn.sigmoid(ga) * (o_fox @ w_out_fox[l])
                  + jax.nn.sigmoid(gb) * (o_ssm @ w_out_ssm[l]))
        x = x + g1[:, None, :] * (merged @ w_o[l])
        h2 = _modulate(_rmsnorm(x, norm_ffn_g[l]), sh2, sc2)
        x = x + g2[:, None, :] * _hier_moe(h2, w_router_group[l], b_router_group[l],
                                            w_router_expert[l], b_router_expert[l],
                                            w_gate_e[l], w_up_e[l], w_down_e[l])
    return _rmsnorm(x, final_g)
```

```python
import functools
import math

import jax
import jax.numpy as jnp
import numpy as np
from jax import lax
from jax.experimental import pallas as pl
from jax.experimental.pallas import tpu as pltpu

F32 = jnp.float32
BF16 = jnp.bfloat16

D_MODEL = 1024
N_MOD = 6
RMS_EPS = 1e-6
FOX_HEADS = 8
FOX_HEAD_DIM = 64
FOX_WIDTH = FOX_HEADS * FOX_HEAD_DIM
HEAD_PAIRS = FOX_HEADS // 2
SSM_WIDTH = 512
SSM_GROUP = 16
SSM_GROUPS = SSM_WIDTH // SSM_GROUP
SSM_STATE = 64
LAMBDA_RE_MAX = -1e-4
N_GROUPS = 4
EXPERTS_PER_GROUP = 8
N_EXPERTS = N_GROUPS * EXPERTS_PER_GROUP
D_EXPERT = 512

LANES = 128
SUBLANES = 8
VMEM_LIMIT = 56 * 1024 * 1024

SSM_CHUNK = 16
TOK_TILE = 512
MIX_TILE = 256
ATT_TILE = 256
ROW_BLOCK = 256
MOVE_TILE = 256
NEG_BIG = -1e30

HIGHEST = lax.Precision.HIGHEST


def _params(sem):
    return pltpu.CompilerParams(dimension_semantics=sem, vmem_limit_bytes=VMEM_LIMIT)


def _rms_modulate(x, gain, shift, scale):
    ms = jnp.mean(x * x, axis=-1, keepdims=True)
    return (x * lax.rsqrt(ms + RMS_EPS)) * gain * (1.0 + scale) + shift


def _mod_kernel(c_ref, w_ref, b_ref, o_ref):
    c = c_ref[...]
    ca = (c * jax.nn.sigmoid(c)).astype(BF16)
    o_ref[...] = jnp.dot(ca, w_ref[...].astype(BF16), preferred_element_type=F32) + b_ref[...]


def _mod(c, w_ada, b_ada):
    bsz, d = c.shape
    cols = w_ada.shape[1]
    tn = 1536
    return pl.pallas_call(
        _mod_kernel,
        grid=(cols // tn,),
        in_specs=[pl.BlockSpec((bsz, d), lambda j: (0, 0)),
                  pl.BlockSpec((d, tn), lambda j: (0, j)),
                  pl.BlockSpec((1, tn), lambda j: (0, j))],
        out_specs=pl.BlockSpec((bsz, tn), lambda j: (0, j)),
        out_shape=jax.ShapeDtypeStruct((bsz, cols), F32),
        compiler_params=_params(("arbitrary",)),
        name="mod",
    )(c, w_ada, b_ada.reshape(1, cols))


_C_Q, _C_K, _C_V, _C_U, _C_GA, _C_GB, _C_F, _C_END = 0, 512, 1024, 1536, 2048, 3072, 4096, 4224


def _inproj_kernel(tiles_per_batch, x_ref, mod_ref, g_ref, w_ref, bf_ref, tri_ref,
                   q_ref, k_ref, v_ref, u_ref, ga_ref, gb_ref, cum_ref, carry_ref):
    i = pl.program_id(0)
    h = _rms_modulate(x_ref[...], g_ref[...], mod_ref[0:1, :], mod_ref[1:2, :])
    hb = h.astype(BF16)

    def proj(a, b):
        return jnp.dot(hb, w_ref[:, a:b], preferred_element_type=F32)

    q_ref[...] = proj(_C_Q, _C_K).astype(BF16)
    k_ref[...] = proj(_C_K, _C_V).astype(BF16)
    v_ref[...] = proj(_C_V, _C_U).astype(BF16)
    u_ref[...] = proj(_C_U, _C_GA)
    ga_ref[...] = jax.nn.sigmoid(proj(_C_GA, _C_GB)).astype(BF16)
    gb_ref[...] = jax.nn.sigmoid(proj(_C_GB, _C_F)).astype(BF16)

    f = proj(_C_F, _C_END) + bf_ref[...]
    logf = jnp.minimum(f, 0.0) - jnp.log(1.0 + jnp.exp(-jnp.abs(f)))

    @pl.when(i % tiles_per_batch == 0)
    def _():
        carry_ref[...] = jnp.zeros_like(carry_ref)

    cs = jnp.dot(tri_ref[...], logf, precision=HIGHEST, preferred_element_type=F32) + carry_ref[0:1, :]
    cum_ref[...] = cs
    carry_ref[...] = jnp.broadcast_to(cs[-1:, :], carry_ref.shape)


def _inproj(x2, mod3, gain, w_all, bf_pad, seq):
    n, d = x2.shape
    tm = TOK_TILE
    tpb = seq // tm
    tri = jnp.tril(jnp.ones((tm, tm), F32))
    tok = lambda w: pl.BlockSpec((tm, w), lambda i: (i, 0))
    const = lambda shape: pl.BlockSpec(shape, lambda i: (0,) * len(shape))
    return pl.pallas_call(
        functools.partial(_inproj_kernel, tpb),
        grid=(n // tm,),
        in_specs=[tok(d),
                  pl.BlockSpec((None, N_MOD, d), lambda i: (i // tpb, 0, 0)),
                  const((1, d)), const((d, _C_END)), const((1, LANES)), const((tm, tm))],
        out_specs=[tok(FOX_WIDTH), tok(FOX_WIDTH), tok(FOX_WIDTH), tok(SSM_WIDTH),
                   tok(d), tok(d), tok(LANES)],
        out_shape=[jax.ShapeDtypeStruct((n, FOX_WIDTH), BF16)] * 3
        + [jax.ShapeDtypeStruct((n, SSM_WIDTH), F32)]
        + [jax.ShapeDtypeStruct((n, d), BF16)] * 2
        + [jax.ShapeDtypeStruct((n, LANES), F32)],
        scratch_shapes=[pltpu.VMEM((SUBLANES, LANES), F32)],
        compiler_params=_params(("arbitrary",)),
        name="inproj",
    )(x2, mod3, gain, w_all, bf_pad, tri)


def _attn_kernel(q_ref, aq_ref, k_ref, ak_ref, v_ref, o_ref, m_ref, l_ref, acc_ref):
    i = pl.program_id(2)
    t = ATT_TILE
    q = q_ref[...]
    aq = aq_ref[...]
    lane = lax.broadcasted_iota(jnp.int32, (1, LANES), 1)
    zq = jnp.zeros_like(q)
    half = FOX_HEAD_DIM
    q_heads = (
        jnp.concatenate([jnp.where(lane < half, q, zq), jnp.where(lane < 8, aq, zq)], axis=1),
        jnp.concatenate([jnp.where(lane >= half, q, zq),
                         jnp.where((lane >= 8) & (lane < 16), aq, zq)], axis=1),
    )
    m_ref[...] = jnp.full(m_ref.shape, NEG_BIG, F32)
    l_ref[...] = jnp.zeros(l_ref.shape, F32)
    acc_ref[...] = jnp.zeros(acc_ref.shape, F32)

    def step(j, masked):
        start = pl.multiple_of(j * t, t)
        kk = jnp.concatenate([k_ref[pl.ds(start, t), :], ak_ref[pl.ds(start, t), :]], axis=1)
        vv = v_ref[pl.ds(start, t), :]
        for h in range(2):
            s = lax.dot_general(q_heads[h], kk, (((1,), (1,)), ((), ())),
                                preferred_element_type=F32)
            if masked:
                row = lax.broadcasted_iota(jnp.int32, (t, t), 0)
                col = lax.broadcasted_iota(jnp.int32, (t, t), 1)
                s = jnp.where(col <= row, s, NEG_BIG)
            m_old = m_ref[h]
            m_new = jnp.maximum(m_old, jnp.max(s, axis=1, keepdims=True))
            alpha = jnp.exp(m_old - m_new)
            p = jnp.exp(s - m_new)
            l_ref[h] = alpha * l_ref[h] + jnp.sum(p, axis=1, keepdims=True)
            acc_ref[h] = alpha * acc_ref[h] + jnp.dot(p.astype(BF16), vv,
                                                      preferred_element_type=F32)
            m_ref[h] = m_new

    def body(j, c):
        step(j, False)
        return c

    lax.fori_loop(0, i, body, 0)
    step(i, True)
    o0 = acc_ref[0] / l_ref[0]
    o1 = acc_ref[1] / l_ref[1]
    o_ref[...] = jnp.where(lane < half, o0, o1).astype(o_ref.dtype)


def _attention(q, k, v, aug_q, aug_k, bsz, seq):
    n = q.shape[0]
    t = ATT_TILE
    nq = seq // t
    return pl.pallas_call(
        _attn_kernel,
        grid=(bsz, HEAD_PAIRS, nq),
        in_specs=[pl.BlockSpec((t, LANES), lambda b, p, i: (b * nq + i, p)),
                  pl.BlockSpec((None, None, t, LANES), lambda b, p, i: (b, p, i, 0)),
                  pl.BlockSpec((seq, LANES), lambda b, p, i: (b, p)),
                  pl.BlockSpec((None, None, seq, LANES), lambda b, p, i: (b, p, 0, 0)),
                  pl.BlockSpec((seq, LANES), lambda b, p, i: (b, p))],
        out_specs=pl.BlockSpec((t, LANES), lambda b, p, i: (b * nq + i, p)),
        out_shape=jax.ShapeDtypeStruct((n, FOX_WIDTH), BF16),
        scratch_shapes=[pltpu.VMEM((2, t, 1), F32), pltpu.VMEM((2, t, 1), F32),
                        pltpu.VMEM((2, t, LANES), F32)],
        compiler_params=_params(("arbitrary", "arbitrary", "arbitrary")),
        name="attn",
    )(q, aug_q, k, aug_k, v)


def _forget_bias_lanes(cum, bsz, seq):
    def top_bits(a):
        bits = lax.bitcast_convert_type(a, jnp.uint32) & jnp.uint32(0xFFFF0000)
        return lax.bitcast_convert_type(bits, F32)

    hi32 = top_bits(cum)
    r1 = cum - hi32
    mid32 = top_bits(r1)
    hi, mid, lo = hi32.astype(BF16), mid32.astype(BF16), top_bits(r1 - mid32).astype(BF16)
    one = jnp.ones_like(hi)
    zero = jnp.zeros_like(hi)
    qa = jnp.stack([hi, mid, lo, one, one, one, zero, zero], axis=-1)
    ka = jnp.stack([one, one, one, -hi, -mid, -lo, zero, zero], axis=-1)

    def lay(a):
        a = a.reshape(bsz, seq, HEAD_PAIRS, 16).transpose(0, 2, 1, 3)
        return jnp.pad(a, ((0, 0), (0, 0), (0, 0), (0, LANES - 16)))

    return lay(qa), lay(ka)


def _ssm_matrices(lambda_re, lambda_im, log_dt, b_re, b_im, c_re, c_im):
    t_len = SSM_CHUNK
    lam_re = jnp.minimum(lambda_re.astype(F32), LAMBDA_RE_MAX)
    lam_im = lambda_im.astype(F32)
    dt = jnp.exp(log_dt.astype(F32))[:, None]
    mag = jnp.exp(lam_re * dt)
    a_re = mag * jnp.cos(lam_im * dt)
    a_im = mag * jnp.sin(lam_im * dt)
    den = lam_re * lam_re + lam_im * lam_im
    nr = a_re - 1.0
    co_re = (nr * lam_re + a_im * lam_im) / den
    co_im = (a_im * lam_re - nr * lam_im) / den
    b_re = b_re.astype(F32)
    b_im = b_im.astype(F32)
    bb_re = co_re[..., None] * b_re - co_im[..., None] * b_im
    bb_im = co_re[..., None] * b_im + co_im[..., None] * b_re
    ls = jnp.arange(t_len + 1, dtype=F32)[:, None, None]
    pmag = jnp.exp(ls * (lam_re * dt)[None])
    pang = ls * (lam_im * dt)[None]
    pw_re = pmag * jnp.cos(pang)
    pw_im = pmag * jnp.sin(pang)
    c_re = c_re.astype(F32)
    c_im = c_im.astype(F32)
    w_re = c_re[None] * pw_re[:, :, None, :] - c_im[None] * pw_im[:, :, None, :]
    w_im = c_re[None] * pw_im[:, :, None, :] + c_im[None] * pw_re[:, :, None, :]
    kern = (jnp.einsum('lgdp,gpc->gldc', w_re[:t_len], bb_re, precision=HIGHEST)
            - jnp.einsum('lgdp,gpc->gldc', w_im[:t_len], bb_im, precision=HIGHEST))
    s_idx = np.arange(t_len)[:, None]
    t_idx = np.arange(t_len)[None, :]
    lag = np.clip(t_idx - s_idx, 0, t_len - 1)
    causal = jnp.asarray(t_idx >= s_idx)
    toep = kern[:, lag]
    toep = jnp.where(causal[None, :, :, None, None], toep, 0.0)
    toep = toep.transpose(0, 1, 4, 2, 3).reshape(SSM_GROUPS, t_len * SSM_GROUP, t_len * SSM_GROUP)
    e_re = pw_re[t_len - 1 - np.arange(t_len)]
    e_im = pw_im[t_len - 1 - np.arange(t_len)]
    bs_re = e_re[..., None] * bb_re[None] - e_im[..., None] * bb_im[None]
    bs_im = e_re[..., None] * bb_im[None] + e_im[..., None] * bb_re[None]
    b_state = jnp.concatenate([bs_re, bs_im], axis=2)
    b_state = b_state.transpose(1, 0, 3, 2).reshape(SSM_GROUPS, t_len * SSM_GROUP, 2 * SSM_STATE)
    cp = jnp.concatenate([w_re[1:], -w_im[1:]], axis=3)
    c_pow = cp.transpose(1, 3, 0, 2).reshape(SSM_GROUPS, 2 * SSM_STATE, t_len * SSM_GROUP)
    a1 = jnp.concatenate([pw_re[t_len], pw_re[t_len]], axis=-1)
    a2 = jnp.concatenate([-pw_im[t_len], pw_im[t_len]], axis=-1)
    a_step = jnp.stack([a1, a2], axis=1)
    return toep.astype(BF16), b_state.astype(BF16), c_pow.astype(BF16), a_step


def _ssm_kernel(n_chunks, bsz, u_ref, toep_ref, bst_ref, cpw_ref, a_ref, y_ref, contrib_ref, xprev_ref):
    u = u_ref[...]
    contrib_ref[...] = jnp.dot(u, bst_ref[...], preferred_element_type=F32)
    a1 = a_ref[0:1, :]
    a2 = a_ref[1:2, :]

    def step(n, x):
        r = pl.multiple_of(n * bsz, bsz)
        xprev_ref[pl.ds(r, bsz), :] = x
        xs = pltpu.roll(x, SSM_STATE, axis=1)
        return a1 * x + a2 * xs + contrib_ref[pl.ds(r, bsz), :]

    lax.fori_loop(0, n_chunks, step, jnp.zeros((bsz, 2 * SSM_STATE), F32))
    y_ref[...] = (jnp.dot(u, toep_ref[...], preferred_element_type=F32)
                  + jnp.dot(xprev_ref[...].astype(BF16), cpw_ref[...], preferred_element_type=F32))


def _ssm(u_flat, toep, b_state, c_pow, a_step, bsz):
    g, rows, w = u_flat.shape
    per = lambda a, b: pl.BlockSpec((None, a, b), lambda i: (i, 0, 0))
    return pl.pallas_call(
        functools.partial(_ssm_kernel, rows // bsz, bsz),
        grid=(g,),
        in_specs=[per(rows, w), per(w, w), per(w, 2 * SSM_STATE), per(2 * SSM_STATE, w),
                  per(2, 2 * SSM_STATE)],
        out_specs=per(rows, w),
        out_shape=jax.ShapeDtypeStruct((g, rows, w), F32),
        scratch_shapes=[pltpu.VMEM((rows, 2 * SSM_STATE), F32), pltpu.VMEM((rows, 2 * SSM_STATE), F32)],
        compiler_params=_params(("arbitrary",)),
        name="ssm",
    )(u_flat, toep, b_state, c_pow, a_step)


def _mix_kernel(x_ref, of_ref, yp_ref, u_ref, ga_ref, gb_ref, mod_ref, dsk_ref, wglu_ref, wfox_ref,
                wssm_ref, wo_ref, g2_ref, wr_ref, br_ref, x1_ref, h2_ref, lg_ref):
    y = yp_ref[...] + dsk_ref[...] * u_ref[...]
    y = 0.5 * y * (1.0 + jnp.tanh(math.sqrt(2.0 / math.pi) * (y + 0.044715 * (y * y * y))))
    gl = jnp.dot(y.astype(BF16), wglu_ref[...], preferred_element_type=F32)
    o_ssm = gl[:, :SSM_WIDTH] * jax.nn.sigmoid(gl[:, SSM_WIDTH:])
    merged = (ga_ref[...].astype(F32) * jnp.dot(of_ref[...], wfox_ref[...], preferred_element_type=F32)
              + gb_ref[...].astype(F32) * jnp.dot(o_ssm.astype(BF16), wssm_ref[...],
                                                  preferred_element_type=F32))
    x1 = x_ref[...] + mod_ref[2:3, :] * jnp.dot(merged.astype(BF16), wo_ref[...],
                                                 preferred_element_type=F32)
    x1_ref[...] = x1
    h2 = _rms_modulate(x1, g2_ref[...], mod_ref[3:4, :], mod_ref[4:5, :])
    h2_ref[...] = h2
    lg_ref[...] = jnp.dot(h2, wr_ref[...], precision=HIGHEST, preferred_element_type=F32) + br_ref[...]


def _mix(x2, o_fox, y_pre, u, sga, sgb, mod3, d_skip, w_glu, w_fox, w_ssm, w_o, g2, w_r, b_r, seq):
    n, d = x2.shape
    tm = MIX_TILE
    tpb = seq // tm
    tok = lambda w: pl.BlockSpec((tm, w), lambda i: (i, 0))
    const = lambda a: pl.BlockSpec(a.shape, lambda i: (0,) * a.ndim)
    return pl.pallas_call(
        _mix_kernel,
        grid=(n // tm,),
        in_specs=[tok(d), tok(FOX_WIDTH), tok(SSM_WIDTH), tok(SSM_WIDTH), tok(d), tok(d),
                  pl.BlockSpec((None, N_MOD, d), lambda i: (i // tpb, 0, 0)),
                  const(d_skip), const(w_glu), const(w_fox), const(w_ssm), const(w_o), const(g2),
                  const(w_r), const(b_r)],
        out_specs=[tok(d), tok(d), tok(LANES)],
        out_shape=[jax.ShapeDtypeStruct((n, d), F32), jax.ShapeDtypeStruct((n, d), F32),
                   jax.ShapeDtypeStruct((n, LANES), F32)],
        compiler_params=_params(("arbitrary",)),
        name="mix",
    )(x2, o_fox, y_pre, u, sga, sgb, mod3, d_skip, w_glu, w_fox, w_ssm, w_o, g2, w_r, b_r)


def _route_kernel(lg_ref, tri_ref, idx_ref, wt_ref, cnt_ref, carry_ref):
    i = pl.program_id(0)

    @pl.when(i == 0)
    def _():
        carry_ref[...] = jnp.zeros_like(carry_ref)

    lg = lg_ref[...]
    tm = lg.shape[0]
    lane = lax.broadcasted_iota(jnp.int32, (tm, LANES), 1)
    neg = jnp.full_like(lg, -jnp.inf)

    def first_argmax(vals):
        mx = jnp.max(vals, axis=1, keepdims=True)
        ix = jnp.min(jnp.where(vals == mx, lane, LANES), axis=1, keepdims=True)
        return mx, ix

    is_group = lane < N_GROUPS
    g_max, gi = first_argmax(jnp.where(is_group, lg, neg))
    g_sum = jnp.sum(jnp.where(is_group, jnp.exp(lg - g_max), 0.0), axis=1, keepdims=True)
    p_group = 1.0 / g_sum
    lo = N_GROUPS + EXPERTS_PER_GROUP * gi
    in_group = (lane >= lo) & (lane < lo + EXPERTS_PER_GROUP)
    cand = jnp.where(in_group, lg, neg)
    v1, i1 = first_argmax(cand)
    v2, i2 = first_argmax(jnp.where(lane == i1, neg, cand))
    tt = jnp.exp(v2 - v1)
    w1 = p_group / (1.0 + tt)
    w2 = p_group * tt / (1.0 + tt)
    e1 = i1 - N_GROUPS
    e2 = i2 - N_GROUPS
    sel1 = lane == e1
    sel2 = lane == e2
    onehot = (sel1 | sel2).astype(F32)
    before = jnp.dot(tri_ref[...], onehot.astype(BF16), preferred_element_type=F32) + carry_ref[0:1, :]
    r1 = jnp.sum(jnp.where(sel1, before, 0.0), axis=1, keepdims=True).astype(jnp.int32)
    r2 = jnp.sum(jnp.where(sel2, before, 0.0), axis=1, keepdims=True).astype(jnp.int32)
    total = before[-1:, :] + onehot[-1:, :]
    carry_ref[...] = jnp.broadcast_to(total, carry_ref.shape)
    cnt_ref[...] = jnp.broadcast_to(total, cnt_ref.shape)
    idx_ref[...] = jnp.where(lane == 0, e1, jnp.where(lane == 1, e2, jnp.where(lane == 2, r1, r2)))
    wt_ref[...] = jnp.where(lane == 0, w1, w2)


def _route(logits):
    n = logits.shape[0]
    tm = TOK_TILE
    tri = jnp.tril(jnp.ones((tm, tm), BF16), k=-1)
    tok = pl.BlockSpec((tm, LANES), lambda i: (i, 0))
    return pl.pallas_call(
        _route_kernel,
        grid=(n // tm,),
        in_specs=[tok, pl.BlockSpec((tm, tm), lambda i: (0, 0))],
        out_specs=[tok, tok, pl.BlockSpec((SUBLANES, LANES), lambda i: (0, 0))],
        out_shape=[jax.ShapeDtypeStruct((n, LANES), jnp.int32), jax.ShapeDtypeStruct((n, LANES), F32),
                   jax.ShapeDtypeStruct((SUBLANES, LANES), F32)],
        scratch_shapes=[pltpu.VMEM((SUBLANES, LANES), F32)],
        compiler_params=_params(("arbitrary",)),
        name="route",
    )(logits, tri)


def _row_copy(src_ref, src_row, dst_ref, dst_row, sem):
    return pltpu.make_async_copy(src_ref.at[pl.ds(src_row, 1), :], dst_ref.at[pl.ds(dst_row, 1), :], sem)


def _dispatch_kernel(dest_ref, h_ref, rows_in_ref, rows_ref, sem):
    del rows_in_ref
    tm = h_ref.shape[0]

    def issue(t, c):
        _row_copy(h_ref, t, rows_ref, dest_ref[0, 0, 2 * t], sem).start()
        _row_copy(h_ref, t, rows_ref, dest_ref[0, 0, 2 * t + 1], sem).start()
        return c

    lax.fori_loop(0, tm, issue, 0)

    def drain(t, c):
        _row_copy(h_ref, 0, rows_ref, 0, sem).wait()
        return c

    lax.fori_loop(0, 2 * tm, drain, 0)


def _dispatch(dest3, h2, rows_zero):
    n, d = h2.shape
    tm = MOVE_TILE
    return pl.pallas_call(
        _dispatch_kernel,
        grid=(n // tm,),
        in_specs=[pl.BlockSpec((1, 1, 2 * tm), lambda i: (i, 0, 0), memory_space=pltpu.SMEM),
                  pl.BlockSpec((tm, d), lambda i: (i, 0)),
                  pl.BlockSpec(memory_space=pl.ANY)],
        out_specs=pl.BlockSpec(memory_space=pl.ANY),
        out_shape=jax.ShapeDtypeStruct(rows_zero.shape, rows_zero.dtype),
        scratch_shapes=[pltpu.SemaphoreType.DMA(())],
        input_output_aliases={2: 0},
        compiler_params=_params(("arbitrary",)),
        name="dispatch",
    )(dest3, h2, rows_zero)


def _combine_kernel(dest_ref, x1_ref, wt_ref, mod_ref, gf_ref, yr_ref, o_ref, buf_ref, sem):
    tm = x1_ref.shape[0]

    def issue(t, c):
        _row_copy(yr_ref, dest_ref[0, 0, 2 * t], buf_ref.at[0], t, sem).start()
        _row_copy(yr_ref, dest_ref[0, 0, 2 * t + 1], buf_ref.at[1], t, sem).start()
        return c

    lax.fori_loop(0, tm, issue, 0)

    def drain(t, c):
        _row_copy(yr_ref, 0, buf_ref.at[0], 0, sem).wait()
        return c

    lax.fori_loop(0, 2 * tm, drain, 0)
    wt = wt_ref[...]
    moe = wt[:, 0:1] * buf_ref[0] + wt[:, 1:2] * buf_ref[1]
    x = x1_ref[...] + mod_ref[5:6, :] * moe
    ms = jnp.mean(x * x, axis=-1, keepdims=True)
    o_ref[...] = (x * lax.rsqrt(ms + RMS_EPS)) * gf_ref[...]


def _combine(dest3, x1, wts, mod3, final_g, y_rows, seq):
    n, d = x1.shape
    tm = MOVE_TILE
    tpb = seq // tm
    return pl.pallas_call(
        _combine_kernel,
        grid=(n // tm,),
        in_specs=[pl.BlockSpec((1, 1, 2 * tm), lambda i: (i, 0, 0), memory_space=pltpu.SMEM),
                  pl.BlockSpec((tm, d), lambda i: (i, 0)),
                  pl.BlockSpec((tm, LANES), lambda i: (i, 0)),
                  pl.BlockSpec((None, N_MOD, d), lambda i: (i // tpb, 0, 0)),
                  pl.BlockSpec((1, d), lambda i: (0, 0)),
                  pl.BlockSpec(memory_space=pl.ANY)],
        out_specs=pl.BlockSpec((tm, d), lambda i: (i, 0)),
        out_shape=jax.ShapeDtypeStruct((n, d), F32),
        scratch_shapes=[pltpu.VMEM((2, tm, d), F32), pltpu.SemaphoreType.DMA(())],
        compiler_params=_params(("arbitrary",)),
        name="combine",
    )(dest3, x1, wts, mod3, final_g, y_rows)


def _expert_kernel(be_ref, nv_ref, x_ref, wg_ref, wu_ref, wd_ref, y_ref):
    i = pl.program_id(0)

    @pl.when(i < nv_ref[0])
    def _():
        xb = x_ref[...].astype(BF16)
        a = jnp.dot(xb, wg_ref[...], preferred_element_type=F32)
        b = jnp.dot(xb, wu_ref[...], preferred_element_type=F32)
        hid = (a * jax.nn.sigmoid(a)) * b
        y_ref[...] = jnp.dot(hid.astype(BF16), wd_ref[...], preferred_element_type=F32)

    @pl.when(i >= nv_ref[0])
    def _():
        y_ref[...] = jnp.zeros_like(y_ref)


def _experts(blk_e, n_valid, x_rows, w_gate, w_up, w_down):
    rows, d = x_rows.shape
    tb = ROW_BLOCK
    grid_spec = pltpu.PrefetchScalarGridSpec(
        num_scalar_prefetch=2,
        grid=(rows // tb,),
        in_specs=[pl.BlockSpec((tb, d), lambda i, be, nv: (i, 0)),
                  pl.BlockSpec((None, d, D_EXPERT), lambda i, be, nv: (be[i], 0, 0)),
                  pl.BlockSpec((None, d, D_EXPERT), lambda i, be, nv: (be[i], 0, 0)),
                  pl.BlockSpec((None, D_EXPERT, d), lambda i, be, nv: (be[i], 0, 0))],
        out_specs=pl.BlockSpec((tb, d), lambda i, be, nv: (i, 0)),
    )
    return pl.pallas_call(
        _expert_kernel,
        grid_spec=grid_spec,
        out_shape=jax.ShapeDtypeStruct((rows, d), F32),
        compiler_params=_params(("arbitrary",)),
        name="experts",
    )(blk_e, n_valid, x_rows, w_gate, w_up, w_down)


def kernel(x, c, w_ada, b_ada, norm_mix_g, w_in, b_forget, w_out_fox, lambda_re, lambda_im, log_dt,
           ssm_b_re, ssm_b_im, ssm_c_re, ssm_c_im, d_skip, w_glu, w_out_ssm, w_o, norm_ffn_g,
           w_router_group, b_router_group, w_router_expert, b_router_expert, w_gate_e, w_up_e,
           w_down_e, final_g):
    bsz, seq, d = x.shape
    n = bsz * seq
    assert w_ada.shape[0] == 1, "the final RMSNorm is fused into the (single) layer's combine kernel"
    xc = x.reshape(n, d)
    for l in range(1):
        mod3 = _mod(c, w_ada[l], b_ada[l]).reshape(bsz, N_MOD, d)

        wi = w_in[l]
        s_q, s_k, s_v, s_f, s_u, s_ga = 512, 1024, 1536, 1544, 2056, 3080
        scale = FOX_HEAD_DIM ** -0.5
        w_all = jnp.concatenate(
            [wi[:, :s_q] * scale, wi[:, s_q:s_k], wi[:, s_k:s_v], wi[:, s_f:s_u], wi[:, s_u:s_ga],
             wi[:, s_ga:], jnp.pad(wi[:, s_v:s_f], ((0, 0), (0, LANES - FOX_HEADS)))],
            axis=1).astype(BF16)
        bf_pad = jnp.pad(b_forget[l], (0, LANES - FOX_HEADS)).reshape(1, LANES)
        q, k, v, u, sga, sgb, cum_pad = _inproj(xc, mod3, norm_mix_g[l].reshape(1, d), w_all, bf_pad, seq)

        aug_q, aug_k = _forget_bias_lanes(cum_pad[:, :FOX_HEADS].reshape(bsz, seq, FOX_HEADS), bsz, seq)
        o_fox = _attention(q, k, v, aug_q, aug_k, bsz, seq)

        toep, b_state, c_pow, a_step = _ssm_matrices(lambda_re[l], lambda_im[l], log_dt[l], ssm_b_re[l],
                                                     ssm_b_im[l], ssm_c_re[l], ssm_c_im[l])
        n_chunks = seq // SSM_CHUNK
        u_flat = (u.astype(BF16).reshape(bsz, n_chunks, SSM_CHUNK, SSM_GROUPS, SSM_GROUP)
                  .transpose(3, 1, 0, 2, 4).reshape(SSM_GROUPS, n_chunks * bsz, SSM_CHUNK * SSM_GROUP))
        y_flat = _ssm(u_flat, toep, b_state, c_pow, a_step, bsz)
        y_pre = (y_flat.reshape(SSM_GROUPS, n_chunks, bsz, SSM_CHUNK, SSM_GROUP)
                 .transpose(2, 1, 3, 0, 4).reshape(n, SSM_WIDTH))

        w_r = jnp.pad(jnp.concatenate([w_router_group[l], w_router_expert[l]], axis=1),
                      ((0, 0), (0, LANES - N_GROUPS - N_EXPERTS)))
        b_r = jnp.pad(jnp.concatenate([b_router_group[l], b_router_expert[l]]),
                      (0, LANES - N_GROUPS - N_EXPERTS)).reshape(1, LANES)
        x1, h2, logits = _mix(xc, o_fox, y_pre, u, sga, sgb, mod3, d_skip[l].reshape(1, SSM_WIDTH),
                              w_glu[l].astype(BF16), w_out_fox[l].astype(BF16),
                              w_out_ssm[l].astype(BF16), w_o[l].astype(BF16),
                              norm_ffn_g[l].reshape(1, d), w_r, b_r, seq)

        idx, wts, cnt = _route(logits)
        counts = cnt[0, :N_EXPERTS].astype(jnp.int32)
        pcounts = ((counts + ROW_BLOCK - 1) // ROW_BLOCK) * ROW_BLOCK
        pends = jnp.cumsum(pcounts)
        pstarts = pends - pcounts
        dest = pstarts[idx[:, 0:2]] + idx[:, 2:4]
        rows = 2 * n + N_EXPERTS * ROW_BLOCK
        n_blocks = rows // ROW_BLOCK
        blk_e = jnp.minimum(jnp.searchsorted(pends, jnp.arange(n_blocks) * ROW_BLOCK, side='right'),
                            N_EXPERTS - 1).astype(jnp.int32)
        n_valid = (pends[-1:] // ROW_BLOCK).astype(jnp.int32)
        dest3 = dest.astype(jnp.int32).reshape(n // MOVE_TILE, 1, 2 * MOVE_TILE)

        x_rows = _dispatch(dest3, h2, jnp.zeros((rows, d), F32))
        y_rows = _experts(blk_e, n_valid, x_rows, w_gate_e[l].astype(BF16), w_up_e[l].astype(BF16),
                          w_down_e[l].astype(BF16))
        xc = _combine(dest3, x1, wts, mod3, final_g.reshape(1, d), y_rows, seq)
    return xc.reshape(bsz, seq, d)
```

```python
import functools
import math

import jax
import jax.numpy as jnp
import numpy as np
from jax import lax
from jax.experimental import pallas as pl
from jax.experimental.pallas import tpu as pltpu

F32 = jnp.float32
BF16 = jnp.bfloat16

D_MODEL = 1024
N_MOD = 6
RMS_EPS = 1e-6
FOX_HEADS = 8
FOX_HEAD_DIM = 64
FOX_WIDTH = FOX_HEADS * FOX_HEAD_DIM
HEAD_PAIRS = FOX_HEADS // 2
SSM_WIDTH = 512
SSM_GROUP = 16
SSM_GROUPS = SSM_WIDTH // SSM_GROUP
SSM_STATE = 64
LAMBDA_RE_MAX = -1e-4
N_GROUPS = 4
EXPERTS_PER_GROUP = 8
N_EXPERTS = N_GROUPS * EXPERTS_PER_GROUP
D_EXPERT = 512

LANES = 128
SUBLANES = 8
VMEM_LIMIT = 56 * 1024 * 1024

SSM_CHUNK = 16
TOK_TILE = 512
MIX_TILE = 256
ATT_TILE = 256
ROW_BLOCK = 256
MOVE_TILE = 256
NEG_BIG = -1e30

HIGHEST = lax.Precision.HIGHEST


def _params(sem):
    return pltpu.CompilerParams(dimension_semantics=sem, vmem_limit_bytes=VMEM_LIMIT)


def _rms_modulate(x, gain, shift, scale):
    ms = jnp.mean(x * x, axis=-1, keepdims=True)
    return (x * lax.rsqrt(ms + RMS_EPS)) * gain * (1.0 + scale) + shift


def _mod_kernel(c_ref, w_ref, b_ref, o_ref):
    c = c_ref[...]
    ca = (c * jax.nn.sigmoid(c)).astype(BF16)
    o_ref[...] = jnp.dot(ca, w_ref[...].astype(BF16), preferred_element_type=F32) + b_ref[...]


def _mod(c, w_ada, b_ada):
    bsz, d = c.shape
    cols = w_ada.shape[1]
    tn = 1536
    return pl.pallas_call(
        _mod_kernel,
        grid=(cols // tn,),
        in_specs=[pl.BlockSpec((bsz, d), lambda j: (0, 0)),
                  pl.BlockSpec((d, tn), lambda j: (0, j)),
                  pl.BlockSpec((1, tn), lambda j: (0, j))],
        out_specs=pl.BlockSpec((bsz, tn), lambda j: (0, j)),
        out_shape=jax.ShapeDtypeStruct((bsz, cols), F32),
        compiler_params=_params(("arbitrary",)),
        name="mod",
    )(c, w_ada, b_ada.reshape(1, cols))


_C_Q, _C_K, _C_V, _C_U, _C_GA, _C_GB, _C_F, _C_END = 0, 512, 1024, 1536, 2048, 3072, 4096, 4224


def _inproj_kernel(tiles_per_batch, x_ref, mod_ref, g_ref, w_ref, bf_ref, tri_ref,
                   q_ref, k_ref, v_ref, u_ref, ga_ref, gb_ref, cum_ref, carry_ref):
    i = pl.program_id(0)
    h = _rms_modulate(x_ref[...], g_ref[...], mod_ref[0:1, :], mod_ref[1:2, :])
    hb = h.astype(BF16)

    def proj(a, b):
        return jnp.dot(hb, w_ref[:, a:b], preferred_element_type=F32)

    q_ref[...] = proj(_C_Q, _C_K).astype(BF16)
    k_ref[...] = proj(_C_K, _C_V).astype(BF16)
    v_ref[...] = proj(_C_V, _C_U).astype(BF16)
    u_ref[...] = proj(_C_U, _C_GA)
    ga_ref[...] = jax.nn.sigmoid(proj(_C_GA, _C_GB)).astype(BF16)
    gb_ref[...] = jax.nn.sigmoid(proj(_C_GB, _C_F)).astype(BF16)

    f = proj(_C_F, _C_END) + bf_ref[...]
    logf = jnp.minimum(f, 0.0) - jnp.log(1.0 + jnp.exp(-jnp.abs(f)))

    @pl.when(i % tiles_per_batch == 0)
    def _():
        carry_ref[...] = jnp.zeros_like(carry_ref)

    cs = jnp.dot(tri_ref[...], logf, precision=HIGHEST, preferred_element_type=F32) + carry_ref[0:1, :]
    cum_ref[...] = cs
    carry_ref[...] = jnp.broadcast_to(cs[-1:, :], carry_ref.shape)


def _inproj(x2, mod3, gain, w_all, bf_pad, seq):
    n, d = x2.shape
    tm = TOK_TILE
    tpb = seq // tm
    tri = jnp.tril(jnp.ones((tm, tm), F32))
    tok = lambda w: pl.BlockSpec((tm, w), lambda i: (i, 0))
    const = lambda shape: pl.BlockSpec(shape, lambda i: (0,) * len(shape))
    return pl.pallas_call(
        functools.partial(_inproj_kernel, tpb),
        grid=(n // tm,),
        in_specs=[tok(d),
                  pl.BlockSpec((None, N_MOD, d), lambda i: (i // tpb, 0, 0)),
                  const((1, d)), const((d, _C_END)), const((1, LANES)), const((tm, tm))],
        out_specs=[tok(FOX_WIDTH), tok(FOX_WIDTH), tok(FOX_WIDTH), tok(SSM_WIDTH),
                   tok(d), tok(d), tok(LANES)],
        out_shape=[jax.ShapeDtypeStruct((n, FOX_WIDTH), BF16)] * 3
        + [jax.ShapeDtypeStruct((n, SSM_WIDTH), F32)]
        + [jax.ShapeDtypeStruct((n, d), BF16)] * 2
        + [jax.ShapeDtypeStruct((n, LANES), F32)],
        scratch_shapes=[pltpu.VMEM((SUBLANES, LANES), F32)],
        compiler_params=_params(("arbitrary",)),
        name="inproj",
    )(x2, mod3, gain, w_all, bf_pad, tri)


def _attn_kernel(q_ref, aq_ref, k_ref, ak_ref, v_ref, o_ref, m_ref, l_ref, acc_ref):
    i = pl.program_id(2)
    t = ATT_TILE
    q = q_ref[...]
    aq = aq_ref[...]
    lane = lax.broadcasted_iota(jnp.int32, (1, LANES), 1)
    zq = jnp.zeros_like(q)
    half = FOX_HEAD_DIM
    q_heads = (
        jnp.concatenate([jnp.where(lane < half, q, zq), jnp.where(lane < 8, aq, zq)], axis=1),
        jnp.concatenate([jnp.where(lane >= half, q, zq),
                         jnp.where((lane >= 8) & (lane < 16), aq, zq)], axis=1),
    )
    m_ref[...] = jnp.full(m_ref.shape, NEG_BIG, F32)
    l_ref[...] = jnp.zeros(l_ref.shape, F32)
    acc_ref[...] = jnp.zeros(acc_ref.shape, F32)

    def step(j, masked):
        start = pl.multiple_of(j * t, t)
        kk = jnp.concatenate([k_ref[pl.ds(start, t), :], ak_ref[pl.ds(start, t), :]], axis=1)
        vv = v_ref[pl.ds(start, t), :]
        for h in range(2):
            s = lax.dot_general(q_heads[h], kk, (((1,), (1,)), ((), ())),
                                preferred_element_type=F32)
            if masked:
                row = lax.broadcasted_iota(jnp.int32, (t, t), 0)
                col = lax.broadcasted_iota(jnp.int32, (t, t), 1)
                s = jnp.where(col <= row, s, NEG_BIG)
            m_old = m_ref[h]
            m_new = jnp.maximum(m_old, jnp.max(s, axis=1, keepdims=True))
            alpha = jnp.exp(m_old - m_new)
            p = jnp.exp(s - m_new)
            l_ref[h] = alpha * l_ref[h] + jnp.sum(p, axis=1, keepdims=True)
            acc_ref[h] = alpha * acc_ref[h] + jnp.dot(p.astype(BF16), vv,
                                                      preferred_element_type=F32)
            m_ref[h] = m_new

    def body(j, c):
        step(j, False)
        return c

    lax.fori_loop(0, i, body, 0)
    step(i, True)
    o0 = acc_ref[0] / l_ref[0]
    o1 = acc_ref[1] / l_ref[1]
    o_ref[...] = jnp.where(lane < half, o0, o1).astype(o_ref.dtype)


def _attention(q, k, v, aug_q, aug_k, bsz, seq):
    n = q.shape[0]
    t = ATT_TILE
    nq = seq // t
    return pl.pallas_call(
        _attn_kernel,
        grid=(bsz, HEAD_PAIRS, nq),
        in_specs=[pl.BlockSpec((t, LANES), lambda b, p, i: (b * nq + i, p)),
                  pl.BlockSpec((None, None, t, LANES), lambda b, p, i: (b, p, i, 0)),
                  pl.BlockSpec((seq, LANES), lambda b, p, i: (b, p)),
                  pl.BlockSpec((None, None, seq, LANES), lambda b, p, i: (b, p, 0, 0)),
                  pl.BlockSpec((seq, LANES), lambda b, p, i: (b, p))],
        out_specs=pl.BlockSpec((t, LANES), lambda b, p, i: (b * nq + i, p)),
        out_shape=jax.ShapeDtypeStruct((n, FOX_WIDTH), BF16),
        scratch_shapes=[pltpu.VMEM((2, t, 1), F32), pltpu.VMEM((2, t, 1), F32),
                        pltpu.VMEM((2, t, LANES), F32)],
        compiler_params=_params(("arbitrary", "arbitrary", "arbitrary")),
        name="attn",
    )(q, aug_q, k, aug_k, v)


def _forget_bias_lanes(cum, bsz, seq):
    def top_bits(a):
        bits = lax.bitcast_convert_type(a, jnp.uint32) & jnp.uint32(0xFFFF0000)
        return lax.bitcast_convert_type(bits, F32)

    hi32 = top_bits(cum)
    r1 = cum - hi32
    mid32 = top_bits(r1)
    hi, mid, lo = hi32.astype(BF16), mid32.astype(BF16), top_bits(r1 - mid32).astype(BF16)
    one = jnp.ones_like(hi)
    zero = jnp.zeros_like(hi)
    qa = jnp.stack([hi, mid, lo, one, one, one, zero, zero], axis=-1)
    ka = jnp.stack([one, one, one, -hi, -mid, -lo, zero, zero], axis=-1)

    def lay(a):
        a = a.reshape(bsz, seq, HEAD_PAIRS, 16).transpose(0, 2, 1, 3)
        return jnp.pad(a, ((0, 0), (0, 0), (0, 0), (0, LANES - 16)))

    return lay(qa), lay(ka)


def _ssm_matrices(lambda_re, lambda_im, log_dt, b_re, b_im, c_re, c_im):
    t_len = SSM_CHUNK
    lam_re = jnp.minimum(lambda_re.astype(F32), LAMBDA_RE_MAX)
    lam_im = lambda_im.astype(F32)
    dt = jnp.exp(log_dt.astype(F32))[:, None]
    mag = jnp.exp(lam_re * dt)
    a_re = mag * jnp.cos(lam_im * dt)
    a_im = mag * jnp.sin(lam_im * dt)
    den = lam_re * lam_re + lam_im * lam_im
    nr = a_re - 1.0
    co_re = (nr * lam_re + a_im * lam_im) / den
    co_im = (a_im * lam_re - nr * lam_im) / den
    b_re = b_re.astype(F32)
    b_im = b_im.astype(F32)
    bb_re = co_re[..., None] * b_re - co_im[..., None] * b_im
    bb_im = co_re[..., None] * b_im + co_im[..., None] * b_re
    ls = jnp.arange(t_len + 1, dtype=F32)[:, None, None]
    pmag = jnp.exp(ls * (lam_re * dt)[None])
    pang = ls * (lam_im * dt)[None]
    pw_re = pmag * jnp.cos(pang)
    pw_im = pmag * jnp.sin(pang)
    c_re = c_re.astype(F32)
    c_im = c_im.astype(F32)
    w_re = c_re[None] * pw_re[:, :, None, :] - c_im[None] * pw_im[:, :, None, :]
    w_im = c_re[None] * pw_im[:, :, None, :] + c_im[None] * pw_re[:, :, None, :]
    kern = (jnp.einsum('lgdp,gpc->gldc', w_re[:t_len], bb_re, precision=HIGHEST)
            - jnp.einsum('lgdp,gpc->gldc', w_im[:t_len], bb_im, precision=HIGHEST))
    s_idx = np.arange(t_len)[:, None]
    t_idx = np.arange(t_len)[None, :]
    lag = np.clip(t_idx - s_idx, 0, t_len - 1)
    causal = jnp.asarray(t_idx >= s_idx)
    toep = kern[:, lag]
    toep = jnp.where(causal[None, :, :, None, None], toep, 0.0)
    toep = toep.transpose(0, 1, 4, 2, 3).reshape(SSM_GROUPS, t_len * SSM_GROUP, t_len * SSM_GROUP)
    e_re = pw_re[t_len - 1 - np.arange(t_len)]
    e_im = pw_im[t_len - 1 - np.arange(t_len)]
    bs_re = e_re[..., None] * bb_re[None] - e_im[..., None] * bb_im[None]
    bs_im = e_re[..., None] * bb_im[None] + e_im[..., None] * bb_re[None]
    def lay_state(first, second):
        m = jnp.concatenate([first, second], axis=2)
        return m.transpose(1, 0, 3, 2).reshape(SSM_GROUPS, t_len * SSM_GROUP, 2 * SSM_STATE)

    b_state = lay_state(bs_re, bs_im)
    b_swap = lay_state(bs_im, bs_re)
    cp = jnp.concatenate([w_re[1:], -w_im[1:]], axis=3)
    c_pow = cp.transpose(1, 3, 0, 2).reshape(SSM_GROUPS, 2 * SSM_STATE, t_len * SSM_GROUP)
    a1 = jnp.concatenate([pw_re[t_len], pw_re[t_len]], axis=-1)
    a2 = jnp.concatenate([-pw_im[t_len], pw_im[t_len]], axis=-1)
    a_step = jnp.stack([a1, a2], axis=1)
    return toep.astype(BF16), b_state.astype(BF16), b_swap.astype(BF16), c_pow.astype(BF16), a_step


def _ssm_kernel(n_chunks, bsz, u_ref, toep_ref, bst_ref, bsw_ref, cpw_ref, a_ref, y_ref,
                contrib_ref, cswap_ref, xprev_ref):
    u = u_ref[...]
    contrib_ref[...] = jnp.dot(u, bst_ref[...], preferred_element_type=F32)
    cswap_ref[...] = jnp.dot(u, bsw_ref[...], preferred_element_type=F32)
    a1 = a_ref[0:1, :]
    a2 = a_ref[1:2, :]

    def step(n, carry):
        x, xs = carry
        r = pl.multiple_of(n * bsz, bsz)
        xprev_ref[pl.ds(r, bsz), :] = x
        x_new = a1 * x + a2 * xs + contrib_ref[pl.ds(r, bsz), :]
        xs_new = a1 * xs - a2 * x + cswap_ref[pl.ds(r, bsz), :]
        return x_new, xs_new

    zero = jnp.zeros((bsz, 2 * SSM_STATE), F32)
    lax.fori_loop(0, n_chunks, step, (zero, zero), unroll=4)
    y_ref[...] = (jnp.dot(u, toep_ref[...], preferred_element_type=F32)
                  + jnp.dot(xprev_ref[...].astype(BF16), cpw_ref[...], preferred_element_type=F32))


def _ssm(u_flat, toep, b_state, b_swap, c_pow, a_step, bsz):
    g, rows, w = u_flat.shape
    per = lambda a, b: pl.BlockSpec((None, a, b), lambda i: (i, 0, 0))
    state = pltpu.VMEM((rows, 2 * SSM_STATE), F32)
    return pl.pallas_call(
        functools.partial(_ssm_kernel, rows // bsz, bsz),
        grid=(g,),
        in_specs=[per(rows, w), per(w, w), per(w, 2 * SSM_STATE), per(w, 2 * SSM_STATE),
                  per(2 * SSM_STATE, w), per(2, 2 * SSM_STATE)],
        out_specs=per(rows, w),
        out_shape=jax.ShapeDtypeStruct((g, rows, w), F32),
        scratch_shapes=[state, state, state],
        compiler_params=_params(("arbitrary",)),
        name="ssm",
    )(u_flat, toep, b_state, b_swap, c_pow, a_step)


def _mix_kernel(x_ref, of_ref, yp_ref, u_ref, ga_ref, gb_ref, mod_ref, dsk_ref, wglu_ref, wfox_ref,
                wssm_ref, wo_ref, g2_ref, wr_ref, br_ref, x1_ref, h2_ref, lg_ref):
    y = yp_ref[...] + dsk_ref[...] * u_ref[...]
    y = 0.5 * y * (1.0 + jnp.tanh(math.sqrt(2.0 / math.pi) * (y + 0.044715 * (y * y * y))))
    gl = jnp.dot(y.astype(BF16), wglu_ref[...], preferred_element_type=F32)
    o_ssm = gl[:, :SSM_WIDTH] * jax.nn.sigmoid(gl[:, SSM_WIDTH:])
    merged = (ga_ref[...].astype(F32) * jnp.dot(of_ref[...], wfox_ref[...], preferred_element_type=F32)
              + gb_ref[...].astype(F32) * jnp.dot(o_ssm.astype(BF16), wssm_ref[...],
                                                  preferred_element_type=F32))
    x1 = x_ref[...] + mod_ref[2:3, :] * jnp.dot(merged.astype(BF16), wo_ref[...],
                                                 preferred_element_type=F32)
    x1_ref[...] = x1
    h2 = _rms_modulate(x1, g2_ref[...], mod_ref[3:4, :], mod_ref[4:5, :])
    h2_ref[...] = h2
    lg_ref[...] = jnp.dot(h2, wr_ref[...], precision=HIGHEST, preferred_element_type=F32) + br_ref[...]


def _mix(x2, o_fox, y_pre, u, sga, sgb, mod3, d_skip, w_glu, w_fox, w_ssm, w_o, g2, w_r, b_r, seq):
    n, d = x2.shape
    tm = MIX_TILE
    tpb = seq // tm
    tok = lambda w: pl.BlockSpec((tm, w), lambda i: (i, 0))
    const = lambda a: pl.BlockSpec(a.shape, lambda i: (0,) * a.ndim)
    return pl.pallas_call(
        _mix_kernel,
        grid=(n // tm,),
        in_specs=[tok(d), tok(FOX_WIDTH), tok(SSM_WIDTH), tok(SSM_WIDTH), tok(d), tok(d),
                  pl.BlockSpec((None, N_MOD, d), lambda i: (i // tpb, 0, 0)),
                  const(d_skip), const(w_glu), const(w_fox), const(w_ssm), const(w_o), const(g2),
                  const(w_r), const(b_r)],
        out_specs=[tok(d), tok(d), tok(LANES)],
        out_shape=[jax.ShapeDtypeStruct((n, d), F32), jax.ShapeDtypeStruct((n, d), F32),
                   jax.ShapeDtypeStruct((n, LANES), F32)],
        compiler_params=_params(("arbitrary",)),
        name="mix",
    )(x2, o_fox, y_pre, u, sga, sgb, mod3, d_skip, w_glu, w_fox, w_ssm, w_o, g2, w_r, b_r)


def _route_kernel(lg_ref, tri_ref, idx_ref, wt_ref, cnt_ref, carry_ref):
    i = pl.program_id(0)

    @pl.when(i == 0)
    def _():
        carry_ref[...] = jnp.zeros_like(carry_ref)

    lg = lg_ref[...]
    tm = lg.shape[0]
    lane = lax.broadcasted_iota(jnp.int32, (tm, LANES), 1)
    neg = jnp.full_like(lg, -jnp.inf)

    def first_argmax(vals):
        mx = jnp.max(vals, axis=1, keepdims=True)
        ix = jnp.min(jnp.where(vals == mx, lane, LANES), axis=1, keepdims=True)
        return mx, ix

    is_group = lane < N_GROUPS
    g_max, gi = first_argmax(jnp.where(is_group, lg, neg))
    g_sum = jnp.sum(jnp.where(is_group, jnp.exp(lg - g_max), 0.0), axis=1, keepdims=True)
    p_group = 1.0 / g_sum
    lo = N_GROUPS + EXPERTS_PER_GROUP * gi
    in_group = (lane >= lo) & (lane < lo + EXPERTS_PER_GROUP)
    cand = jnp.where(in_group, lg, neg)
    v1, i1 = first_argmax(cand)
    v2, i2 = first_argmax(jnp.where(lane == i1, neg, cand))
    tt = jnp.exp(v2 - v1)
    w1 = p_group / (1.0 + tt)
    w2 = p_group * tt / (1.0 + tt)
    e1 = i1 - N_GROUPS
    e2 = i2 - N_GROUPS
    sel1 = lane == e1
    sel2 = lane == e2
    onehot = (sel1 | sel2).astype(F32)
    before = jnp.dot(tri_ref[...], onehot.astype(BF16), preferred_element_type=F32) + carry_ref[0:1, :]
    r1 = jnp.sum(jnp.where(sel1, before, 0.0), axis=1, keepdims=True).astype(jnp.int32)
    r2 = jnp.sum(jnp.where(sel2, before, 0.0), axis=1, keepdims=True).astype(jnp.int32)
    total = before[-1:, :] + onehot[-1:, :]
    carry_ref[...] = jnp.broadcast_to(total, carry_ref.shape)
    cnt_ref[...] = jnp.broadcast_to(total, cnt_ref.shape)
    idx_ref[...] = jnp.where(lane == 0, e1, jnp.where(lane == 1, e2, jnp.where(lane == 2, r1, r2)))
    wt_ref[...] = jnp.where(lane == 0, w1, w2)


def _route(logits):
    n = logits.shape[0]
    tm = TOK_TILE
    tri = jnp.tril(jnp.ones((tm, tm), BF16), k=-1)
    tok = pl.BlockSpec((tm, LANES), lambda i: (i, 0))
    return pl.pallas_call(
        _route_kernel,
        grid=(n // tm,),
        in_specs=[tok, pl.BlockSpec((tm, tm), lambda i: (0, 0))],
        out_specs=[tok, tok, pl.BlockSpec((SUBLANES, LANES), lambda i: (0, 0))],
        out_shape=[jax.ShapeDtypeStruct((n, LANES), jnp.int32), jax.ShapeDtypeStruct((n, LANES), F32),
                   jax.ShapeDtypeStruct((SUBLANES, LANES), F32)],
        scratch_shapes=[pltpu.VMEM((SUBLANES, LANES), F32)],
        compiler_params=_params(("arbitrary",)),
        name="route",
    )(logits, tri)


def _row_copy(src_ref, src_row, dst_ref, dst_row, sem):
    return pltpu.make_async_copy(src_ref.at[pl.ds(src_row, 1), :], dst_ref.at[pl.ds(dst_row, 1), :], sem)


ISSUE_UNROLL = 8


def _dispatch_kernel(dest_ref, h_ref, rows_in_ref, rows_ref, sem):
    del rows_in_ref
    tm = h_ref.shape[0]

    def issue(g, c):
        for j in range(ISSUE_UNROLL):
            t = g * ISSUE_UNROLL + j
            _row_copy(h_ref, t, rows_ref, dest_ref[0, 0, 2 * t], sem).start(priority=0)
            _row_copy(h_ref, t, rows_ref, dest_ref[0, 0, 2 * t + 1], sem).start(priority=1)
        return c

    lax.fori_loop(0, tm // ISSUE_UNROLL, issue, 0)
    for _ in range(2):
        pltpu.make_async_copy(h_ref, rows_ref.at[pl.ds(0, tm), :], sem).wait()


def _dispatch(dest3, h2, rows_zero):
    n, d = h2.shape
    tm = MOVE_TILE
    return pl.pallas_call(
        _dispatch_kernel,
        grid=(n // tm,),
        in_specs=[pl.BlockSpec((1, 1, 2 * tm), lambda i: (i, 0, 0), memory_space=pltpu.SMEM),
                  pl.BlockSpec((tm, d), lambda i: (i, 0)),
                  pl.BlockSpec(memory_space=pl.ANY)],
        out_specs=pl.BlockSpec(memory_space=pl.ANY),
        out_shape=jax.ShapeDtypeStruct(rows_zero.shape, rows_zero.dtype),
        scratch_shapes=[pltpu.SemaphoreType.DMA(())],
        input_output_aliases={2: 0},
        compiler_params=_params(("arbitrary",)),
        name="dispatch",
    )(dest3, h2, rows_zero)


def _combine_kernel(dest_ref, x1_ref, wt_ref, mod_ref, gf_ref, yr_ref, o_ref, buf_ref, sem):
    tm = x1_ref.shape[0]

    def issue(g, c):
        for j in range(ISSUE_UNROLL):
            t = g * ISSUE_UNROLL + j
            _row_copy(yr_ref, dest_ref[0, 0, 2 * t], buf_ref.at[0], t, sem).start(priority=0)
            _row_copy(yr_ref, dest_ref[0, 0, 2 * t + 1], buf_ref.at[1], t, sem).start(priority=1)
        return c

    lax.fori_loop(0, tm // ISSUE_UNROLL, issue, 0)
    for slot in range(2):
        pltpu.make_async_copy(yr_ref.at[pl.ds(0, tm), :], buf_ref.at[slot], sem).wait()
    wt = wt_ref[...]
    moe = wt[:, 0:1] * buf_ref[0] + wt[:, 1:2] * buf_ref[1]
    x = x1_ref[...] + mod_ref[5:6, :] * moe
    ms = jnp.mean(x * x, axis=-1, keepdims=True)
    o_ref[...] = (x * lax.rsqrt(ms + RMS_EPS)) * gf_ref[...]


def _combine(dest3, x1, wts, mod3, final_g, y_rows, seq):
    n, d = x1.shape
    tm = MOVE_TILE
    tpb = seq // tm
    return pl.pallas_call(
        _combine_kernel,
        grid=(n // tm,),
        in_specs=[pl.BlockSpec((1, 1, 2 * tm), lambda i: (i, 0, 0), memory_space=pltpu.SMEM),
                  pl.BlockSpec((tm, d), lambda i: (i, 0)),
                  pl.BlockSpec((tm, LANES), lambda i: (i, 0)),
                  pl.BlockSpec((None, N_MOD, d), lambda i: (i // tpb, 0, 0)),
                  pl.BlockSpec((1, d), lambda i: (0, 0)),
                  pl.BlockSpec(memory_space=pl.ANY)],
        out_specs=pl.BlockSpec((tm, d), lambda i: (i, 0)),
        out_shape=jax.ShapeDtypeStruct((n, d), F32),
        scratch_shapes=[pltpu.VMEM((2, tm, d), F32), pltpu.SemaphoreType.DMA(())],
        compiler_params=_params(("arbitrary",)),
        name="combine",
    )(dest3, x1, wts, mod3, final_g, y_rows)


def _expert_kernel(be_ref, nv_ref, x_ref, wg_ref, wu_ref, wd_ref, y_ref, wgb_ref, wub_ref, wdb_ref):
    i = pl.program_id(0)
    valid = i < nv_ref[0]
    new_expert = (i == 0) | (be_ref[i] != be_ref[jnp.maximum(i - 1, 0)])

    @pl.when(valid & new_expert)
    def _():
        wgb_ref[...] = wg_ref[...].astype(BF16)
        wub_ref[...] = wu_ref[...].astype(BF16)
        wdb_ref[...] = wd_ref[...].astype(BF16)

    @pl.when(valid)
    def _():
        xb = x_ref[...].astype(BF16)
        a = jnp.dot(xb, wgb_ref[...], preferred_element_type=F32)
        b = jnp.dot(xb, wub_ref[...], preferred_element_type=F32)
        hid = (a * jax.nn.sigmoid(a)) * b
        y_ref[...] = jnp.dot(hid.astype(BF16), wdb_ref[...], preferred_element_type=F32)

    @pl.when(jnp.logical_not(valid))
    def _():
        y_ref[...] = jnp.zeros_like(y_ref)


def _experts(blk_e, n_valid, x_rows, w_gate, w_up, w_down):
    rows, d = x_rows.shape
    tb = ROW_BLOCK
    grid_spec = pltpu.PrefetchScalarGridSpec(
        num_scalar_prefetch=2,
        grid=(rows // tb,),
        in_specs=[pl.BlockSpec((tb, d), lambda i, be, nv: (i, 0)),
                  pl.BlockSpec((None, d, D_EXPERT), lambda i, be, nv: (be[i], 0, 0)),
                  pl.BlockSpec((None, d, D_EXPERT), lambda i, be, nv: (be[i], 0, 0)),
                  pl.BlockSpec((None, D_EXPERT, d), lambda i, be, nv: (be[i], 0, 0))],
        out_specs=pl.BlockSpec((tb, d), lambda i, be, nv: (i, 0)),
        scratch_shapes=[pltpu.VMEM((d, D_EXPERT), BF16), pltpu.VMEM((d, D_EXPERT), BF16),
                        pltpu.VMEM((D_EXPERT, d), BF16)],
    )
    return pl.pallas_call(
        _expert_kernel,
        grid_spec=grid_spec,
        out_shape=jax.ShapeDtypeStruct((rows, d), F32),
        compiler_params=_params(("arbitrary",)),
        name="experts",
    )(blk_e, n_valid, x_rows, w_gate, w_up, w_down)


def kernel(x, c, w_ada, b_ada, norm_mix_g, w_in, b_forget, w_out_fox, lambda_re, lambda_im, log_dt,
           ssm_b_re, ssm_b_im, ssm_c_re, ssm_c_im, d_skip, w_glu, w_out_ssm, w_o, norm_ffn_g,
           w_router_group, b_router_group, w_router_expert, b_router_expert, w_gate_e, w_up_e,
           w_down_e, final_g):
    bsz, seq, d = x.shape
    n = bsz * seq
    assert w_ada.shape[0] == 1, "the final RMSNorm is fused into the (single) layer's combine kernel"
    xc = x.reshape(n, d)
    for l in range(1):
        mod3 = _mod(c, w_ada[l], b_ada[l]).reshape(bsz, N_MOD, d)

        wi = w_in[l]
        s_q, s_k, s_v, s_f, s_u, s_ga = 512, 1024, 1536, 1544, 2056, 3080
        scale = FOX_HEAD_DIM ** -0.5
        w_all = jnp.concatenate(
            [wi[:, :s_q] * scale, wi[:, s_q:s_k], wi[:, s_k:s_v], wi[:, s_f:s_u], wi[:, s_u:s_ga],
             wi[:, s_ga:], jnp.pad(wi[:, s_v:s_f], ((0, 0), (0, LANES - FOX_HEADS)))],
            axis=1).astype(BF16)
        bf_pad = jnp.pad(b_forget[l], (0, LANES - FOX_HEADS)).reshape(1, LANES)
        q, k, v, u, sga, sgb, cum_pad = _inproj(xc, mod3, norm_mix_g[l].reshape(1, d), w_all, bf_pad, seq)

        aug_q, aug_k = _forget_bias_lanes(cum_pad[:, :FOX_HEADS].reshape(bsz, seq, FOX_HEADS), bsz, seq)
        o_fox = _attention(q, k, v, aug_q, aug_k, bsz, seq)

        toep, b_state, b_swap, c_pow, a_step = _ssm_matrices(
            lambda_re[l], lambda_im[l], log_dt[l], ssm_b_re[l], ssm_b_im[l], ssm_c_re[l], ssm_c_im[l])
        n_chunks = seq // SSM_CHUNK
        u_flat = (u.astype(BF16).reshape(bsz, n_chunks, SSM_CHUNK, SSM_GROUPS, SSM_GROUP)
                  .transpose(3, 1, 0, 2, 4).reshape(SSM_GROUPS, n_chunks * bsz, SSM_CHUNK * SSM_GROUP))
        y_flat = _ssm(u_flat, toep, b_state, b_swap, c_pow, a_step, bsz)
        y_pre = (y_flat.reshape(SSM_GROUPS, n_chunks, bsz, SSM_CHUNK, SSM_GROUP)
                 .transpose(2, 1, 3, 0, 4).reshape(n, SSM_WIDTH))

        w_r = jnp.pad(jnp.concatenate([w_router_group[l], w_router_expert[l]], axis=1),
                      ((0, 0), (0, LANES - N_GROUPS - N_EXPERTS)))
        b_r = jnp.pad(jnp.concatenate([b_router_group[l], b_router_expert[l]]),
                      (0, LANES - N_GROUPS - N_EXPERTS)).reshape(1, LANES)
        x1, h2, logits = _mix(xc, o_fox, y_pre, u, sga, sgb, mod3, d_skip[l].reshape(1, SSM_WIDTH),
                              w_glu[l].astype(BF16), w_out_fox[l].astype(BF16),
                              w_out_ssm[l].astype(BF16), w_o[l].astype(BF16),
                              norm_ffn_g[l].reshape(1, d), w_r, b_r, seq)

        idx, wts, cnt = _route(logits)
        counts = cnt[0, :N_EXPERTS].astype(jnp.int32)
        pcounts = ((counts + ROW_BLOCK - 1) // ROW_BLOCK) * ROW_BLOCK
        pends = jnp.cumsum(pcounts)
        pstarts = pends - pcounts
        dest = pstarts[idx[:, 0:2]] + idx[:, 2:4]
        rows = 2 * n + N_EXPERTS * ROW_BLOCK
        n_blocks = rows // ROW_BLOCK
        blk_start = jnp.arange(n_blocks, dtype=jnp.int32) * ROW_BLOCK
        blk_e = jnp.minimum(jnp.sum((pends[None, :] <= blk_start[:, None]).astype(jnp.int32), axis=1),
                            N_EXPERTS - 1)
        n_valid = (pends[-1:] // ROW_BLOCK).astype(jnp.int32)
        dest3 = dest.astype(jnp.int32).reshape(n // MOVE_TILE, 1, 2 * MOVE_TILE)

        x_rows = _dispatch(dest3, h2, jnp.zeros((rows, d), F32))
        y_rows = _experts(blk_e, n_valid, x_rows, w_gate_e[l], w_up_e[l], w_down_e[l])
        xc = _combine(dest3, x1, wts, mod3, final_g.reshape(1, d), y_rows, seq)
    return xc.reshape(bsz, seq, d)
```

```python
import functools
import math

import jax
import jax.numpy as jnp
import numpy as np
from jax import lax
from jax.experimental import pallas as pl
from jax.experimental.pallas import tpu as pltpu

F32 = jnp.float32
BF16 = jnp.bfloat16

D_MODEL = 1024
N_MOD = 6
RMS_EPS = 1e-6
FOX_HEADS = 8
FOX_HEAD_DIM = 64
FOX_WIDTH = FOX_HEADS * FOX_HEAD_DIM
HEAD_PAIRS = FOX_HEADS // 2
SSM_WIDTH = 512
SSM_GROUP = 16
SSM_GROUPS = SSM_WIDTH // SSM_GROUP
SSM_STATE = 64
LAMBDA_RE_MAX = -1e-4
N_GROUPS = 4
EXPERTS_PER_GROUP = 8
N_EXPERTS = N_GROUPS * EXPERTS_PER_GROUP
D_EXPERT = 512

LANES = 128
SUBLANES = 8
VMEM_LIMIT = 56 * 1024 * 1024

SSM_CHUNK = 16
TOK_TILE = 512
MIX_TILE = 256
ATT_TILE = 256
ROW_BLOCK = 256
MOVE_TILE = 256
NEG_BIG = -1e30

HIGHEST = lax.Precision.HIGHEST


def _params(sem):
    return pltpu.CompilerParams(dimension_semantics=sem, vmem_limit_bytes=VMEM_LIMIT)


def _rms_modulate(x, gain, shift, scale):
    ms = jnp.mean(x * x, axis=-1, keepdims=True)
    return (x * lax.rsqrt(ms + RMS_EPS)) * gain * (1.0 + scale) + shift


def _mod_kernel(c_ref, w_ref, b_ref, o_ref):
    c = c_ref[...]
    ca = (c * jax.nn.sigmoid(c)).astype(BF16)
    o_ref[...] = jnp.dot(ca, w_ref[...].astype(BF16), preferred_element_type=F32) + b_ref[...]


def _mod(c, w_ada, b_ada):
    bsz, d = c.shape
    cols = w_ada.shape[1]
    tn = 1536
    return pl.pallas_call(
        _mod_kernel,
        grid=(cols // tn,),
        in_specs=[pl.BlockSpec((bsz, d), lambda j: (0, 0)),
                  pl.BlockSpec((d, tn), lambda j: (0, j)),
                  pl.BlockSpec((1, tn), lambda j: (0, j))],
        out_specs=pl.BlockSpec((bsz, tn), lambda j: (0, j)),
        out_shape=jax.ShapeDtypeStruct((bsz, cols), F32),
        compiler_params=_params(("arbitrary",)),
        name="mod",
    )(c, w_ada, b_ada.reshape(1, cols))


_C_Q, _C_K, _C_U, _C_GA, _C_GB, _C_F, _C_END = 0, 512, 1024, 1536, 2560, 3584, 3712


def _inproj_kernel(tiles_per_batch, x_ref, mod_ref, g_ref, w_ref, wvt_ref, bf_ref, tri_ref,
                   q_ref, k_ref, vt_ref, u_ref, ga_ref, gb_ref, cum_ref, carry_ref):
    i = pl.program_id(0)
    h = _rms_modulate(x_ref[...], g_ref[...], mod_ref[0:1, :], mod_ref[1:2, :])
    hb = h.astype(BF16)

    def proj(a, b):
        return jnp.dot(hb, w_ref[:, a:b], preferred_element_type=F32)

    q_ref[...] = proj(_C_Q, _C_K).astype(BF16)
    k_ref[...] = proj(_C_K, _C_U).astype(BF16)
    vt_ref[...] = lax.dot_general(wvt_ref[...], hb, (((1,), (1,)), ((), ())),
                                  preferred_element_type=F32).astype(BF16)
    u_ref[...] = proj(_C_U, _C_GA)
    ga_ref[...] = jax.nn.sigmoid(proj(_C_GA, _C_GB)).astype(BF16)
    gb_ref[...] = jax.nn.sigmoid(proj(_C_GB, _C_F)).astype(BF16)

    f = proj(_C_F, _C_END) + bf_ref[...]
    logf = jnp.minimum(f, 0.0) - jnp.log(1.0 + jnp.exp(-jnp.abs(f)))

    @pl.when(i % tiles_per_batch == 0)
    def _():
        carry_ref[...] = jnp.zeros_like(carry_ref)

    cs = jnp.dot(tri_ref[...], logf, precision=HIGHEST, preferred_element_type=F32) + carry_ref[0:1, :]
    cum_ref[...] = cs
    carry_ref[...] = jnp.broadcast_to(cs[-1:, :], carry_ref.shape)


def _inproj(x2, mod3, gain, w_all, w_vt, bf_pad, seq):
    n, d = x2.shape
    tm = TOK_TILE
    tpb = seq // tm
    tri = jnp.tril(jnp.ones((tm, tm), F32))
    tok = lambda w: pl.BlockSpec((tm, w), lambda i: (i, 0))
    const = lambda shape: pl.BlockSpec(shape, lambda i: (0,) * len(shape))
    return pl.pallas_call(
        functools.partial(_inproj_kernel, tpb),
        grid=(n // tm,),
        in_specs=[tok(d),
                  pl.BlockSpec((None, N_MOD, d), lambda i: (i // tpb, 0, 0)),
                  const((1, d)), const((d, _C_END)), const((FOX_WIDTH, d)), const((1, LANES)),
                  const((tm, tm))],
        out_specs=[tok(FOX_WIDTH), tok(FOX_WIDTH), pl.BlockSpec((FOX_WIDTH, tm), lambda i: (0, i)),
                   tok(SSM_WIDTH), tok(d), tok(d), tok(LANES)],
        out_shape=[jax.ShapeDtypeStruct((n, FOX_WIDTH), BF16)] * 2
        + [jax.ShapeDtypeStruct((FOX_WIDTH, n), BF16)]
        + [jax.ShapeDtypeStruct((n, SSM_WIDTH), F32)]
        + [jax.ShapeDtypeStruct((n, d), BF16)] * 2
        + [jax.ShapeDtypeStruct((n, LANES), F32)],
        scratch_shapes=[pltpu.VMEM((SUBLANES, LANES), F32)],
        compiler_params=_params(("arbitrary",)),
        name="inproj",
    )(x2, mod3, gain, w_all, w_vt, bf_pad, tri)


def _attn_kernel(q_ref, aq_ref, k_ref, ak_ref, vt_ref, o_ref, m_ref, acc_ref):
    i = pl.program_id(2)
    t = ATT_TILE
    q = q_ref[...]
    aq = aq_ref[...]
    lane = lax.broadcasted_iota(jnp.int32, (1, LANES), 1)
    zq = jnp.zeros_like(q)
    half = FOX_HEAD_DIM
    q_both = jnp.concatenate([
        jnp.concatenate([jnp.where(lane < half, q, zq), jnp.where(lane < 8, aq, zq)], axis=1),
        jnp.concatenate([jnp.where(lane >= half, q, zq),
                         jnp.where((lane >= 8) & (lane < 16), aq, zq)], axis=1)], axis=0)
    m_ref[...] = jnp.full(m_ref.shape, NEG_BIG, F32)
    acc_ref[...] = jnp.zeros(acc_ref.shape, F32)
    ones_rows = jnp.ones((2 * SUBLANES, t), BF16)

    def scores(j):
        start = pl.multiple_of(j * t, t)
        kk = jnp.concatenate([k_ref[pl.ds(start, t), :], ak_ref[pl.ds(start, t), :]], axis=1)
        return lax.dot_general(kk, q_both, (((1,), (1,)), ((), ())), preferred_element_type=F32)

    def causal(s, j):
        key = j * t + lax.broadcasted_iota(jnp.int32, (t, 2 * t), 0)
        qry = i * t + (lax.broadcasted_iota(jnp.int32, (t, 2 * t), 1) & (t - 1))
        return jnp.where(key <= qry, s, NEG_BIG)

    def accumulate(j, s):
        start = pl.multiple_of(j * t, t)
        va = jnp.concatenate([vt_ref[:, pl.ds(start, t)], ones_rows], axis=0)
        m_old = m_ref[...]
        m_new = jnp.maximum(m_old, jnp.max(s, axis=0, keepdims=True))
        alpha = jnp.exp(m_old - m_new)
        p = jnp.exp(s - m_new).astype(BF16)
        acc_ref[...] = alpha * acc_ref[...] + jnp.dot(va, p, preferred_element_type=F32)
        m_ref[...] = m_new

    def body(j, s_cur):
        s_next = scores(j + 1)
        accumulate(j, s_cur)
        return s_next

    s_cur = lax.fori_loop(0, jnp.maximum(i - 1, 0), body, causal(scores(0), 0))

    @pl.when(i == 0)
    def _():
        accumulate(0, s_cur)

    @pl.when(i > 0)
    def _():
        s_diag = causal(scores(i), i)
        accumulate(i - 1, s_cur)
        accumulate(i, s_diag)

    acc = acc_ref[...]
    o_t = jnp.concatenate([acc[0:half, 0:t] / acc[LANES:LANES + 1, 0:t],
                           acc[half:LANES, t:2 * t] / acc[LANES:LANES + 1, t:2 * t]], axis=0)
    o_ref[...] = o_t.T.astype(o_ref.dtype)


def _attention(q, k, v_t, aug_q, aug_k, bsz, seq):
    n = q.shape[0]
    t = ATT_TILE
    nq = seq // t
    return pl.pallas_call(
        _attn_kernel,
        grid=(bsz, HEAD_PAIRS, nq),
        in_specs=[pl.BlockSpec((t, LANES), lambda b, p, i: (b * nq + i, p)),
                  pl.BlockSpec((None, None, t, LANES), lambda b, p, i: (b, p, i, 0)),
                  pl.BlockSpec((seq, LANES), lambda b, p, i: (b, p)),
                  pl.BlockSpec((None, None, seq, LANES), lambda b, p, i: (b, p, 0, 0)),
                  pl.BlockSpec((LANES, seq), lambda b, p, i: (p, b))],
        out_specs=pl.BlockSpec((t, LANES), lambda b, p, i: (b * nq + i, p)),
        out_shape=jax.ShapeDtypeStruct((n, FOX_WIDTH), BF16),
        scratch_shapes=[pltpu.VMEM((1, 2 * t), F32), pltpu.VMEM((LANES + 2 * SUBLANES, 2 * t), F32)],
        compiler_params=_params(("arbitrary", "arbitrary", "arbitrary")),
        name="attn",
    )(q, aug_q, k, aug_k, v_t)


def _forget_bias_lanes(cum, bsz, seq):
    def top_bits(a):
        bits = lax.bitcast_convert_type(a, jnp.uint32) & jnp.uint32(0xFFFF0000)
        return lax.bitcast_convert_type(bits, F32)

    hi32 = top_bits(cum)
    r1 = cum - hi32
    mid32 = top_bits(r1)
    hi, mid, lo = hi32.astype(BF16), mid32.astype(BF16), top_bits(r1 - mid32).astype(BF16)
    one = jnp.ones_like(hi)
    zero = jnp.zeros_like(hi)
    qa = jnp.stack([hi, mid, lo, one, one, one, zero, zero], axis=-1)
    ka = jnp.stack([one, one, one, -hi, -mid, -lo, zero, zero], axis=-1)

    def lay(a):
        a = a.reshape(bsz, seq, HEAD_PAIRS, 16).transpose(0, 2, 1, 3)
        return jnp.pad(a, ((0, 0), (0, 0), (0, 0), (0, LANES - 16)))

    return lay(qa), lay(ka)


def _ssm_matrices(lambda_re, lambda_im, log_dt, b_re, b_im, c_re, c_im):
    t_len = SSM_CHUNK
    lam_re = jnp.minimum(lambda_re.astype(F32), LAMBDA_RE_MAX)
    lam_im = lambda_im.astype(F32)
    dt = jnp.exp(log_dt.astype(F32))[:, None]
    mag = jnp.exp(lam_re * dt)
    a_re = mag * jnp.cos(lam_im * dt)
    a_im = mag * jnp.sin(lam_im * dt)
    den = lam_re * lam_re + lam_im * lam_im
    nr = a_re - 1.0
    co_re = (nr * lam_re + a_im * lam_im) / den
    co_im = (a_im * lam_re - nr * lam_im) / den
    b_re = b_re.astype(F32)
    b_im = b_im.astype(F32)
    bb_re = co_re[..., None] * b_re - co_im[..., None] * b_im
    bb_im = co_re[..., None] * b_im + co_im[..., None] * b_re
    ls = jnp.arange(t_len + 1, dtype=F32)[:, None, None]
    pmag = jnp.exp(ls * (lam_re * dt)[None])
    pang = ls * (lam_im * dt)[None]
    pw_re = pmag * jnp.cos(pang)
    pw_im = pmag * jnp.sin(pang)
    c_re = c_re.astype(F32)
    c_im = c_im.astype(F32)
    w_re = c_re[None] * pw_re[:, :, None, :] - c_im[None] * pw_im[:, :, None, :]
    w_im = c_re[None] * pw_im[:, :, None, :] + c_im[None] * pw_re[:, :, None, :]
    kern = (jnp.einsum('lgdp,gpc->gldc', w_re[:t_len], bb_re, precision=HIGHEST)
            - jnp.einsum('lgdp,gpc->gldc', w_im[:t_len], bb_im, precision=HIGHEST))
    s_idx = np.arange(t_len)[:, None]
    t_idx = np.arange(t_len)[None, :]
    lag = np.clip(t_idx - s_idx, 0, t_len - 1)
    causal = jnp.asarray(t_idx >= s_idx)
    toep = kern[:, lag]
    toep = jnp.where(causal[None, :, :, None, None], toep, 0.0)
    toep = toep.transpose(0, 1, 4, 2, 3).reshape(SSM_GROUPS, t_len * SSM_GROUP, t_len * SSM_GROUP)
    e_re = pw_re[t_len - 1 - np.arange(t_len)]
    e_im = pw_im[t_len - 1 - np.arange(t_len)]
    bs_re = e_re[..., None] * bb_re[None] - e_im[..., None] * bb_im[None]
    bs_im = e_re[..., None] * bb_im[None] + e_im[..., None] * bb_re[None]
    def lay_state(first, second):
        m = jnp.concatenate([first, second], axis=2)
        return m.transpose(1, 0, 3, 2).reshape(SSM_GROUPS, t_len * SSM_GROUP, 2 * SSM_STATE)

    b_state = lay_state(bs_re, bs_im)
    b_swap = lay_state(bs_im, bs_re)
    cp = jnp.concatenate([w_re[1:], -w_im[1:]], axis=3)
    c_pow = cp.transpose(1, 3, 0, 2).reshape(SSM_GROUPS, 2 * SSM_STATE, t_len * SSM_GROUP)
    a1 = jnp.concatenate([pw_re[t_len], pw_re[t_len]], axis=-1)
    a2 = jnp.concatenate([-pw_im[t_len], pw_im[t_len]], axis=-1)
    a_step = jnp.stack([a1, a2], axis=1)
    return toep.astype(BF16), b_state.astype(BF16), b_swap.astype(BF16), c_pow.astype(BF16), a_step


def _ssm_kernel(n_chunks, bsz, u_ref, toep_ref, bst_ref, bsw_ref, cpw_ref, a_ref, y_ref,
                contrib_ref, cswap_ref, xprev_ref):
    u = u_ref[...]
    contrib_ref[...] = jnp.dot(u, bst_ref[...], preferred_element_type=F32)
    cswap_ref[...] = jnp.dot(u, bsw_ref[...], preferred_element_type=F32)
    a1 = a_ref[0:1, :]
    a2 = a_ref[1:2, :]

    def step(n, carry):
        x, xs = carry
        r = pl.multiple_of(n * bsz, bsz)
        xprev_ref[pl.ds(r, bsz), :] = x
        x_new = a1 * x + a2 * xs + contrib_ref[pl.ds(r, bsz), :]
        xs_new = a1 * xs - a2 * x + cswap_ref[pl.ds(r, bsz), :]
        return x_new, xs_new

    zero = jnp.zeros((bsz, 2 * SSM_STATE), F32)
    lax.fori_loop(0, n_chunks, step, (zero, zero), unroll=4)
    y_ref[...] = (jnp.dot(u, toep_ref[...], preferred_element_type=F32)
                  + jnp.dot(xprev_ref[...].astype(BF16), cpw_ref[...], preferred_element_type=F32))


def _ssm(u_flat, toep, b_state, b_swap, c_pow, a_step, bsz):
    g, rows, w = u_flat.shape
    per = lambda a, b: pl.BlockSpec((None, a, b), lambda i: (i, 0, 0))
    state = pltpu.VMEM((rows, 2 * SSM_STATE), F32)
    return pl.pallas_call(
        functools.partial(_ssm_kernel, rows // bsz, bsz),
        grid=(g,),
        in_specs=[per(rows, w), per(w, w), per(w, 2 * SSM_STATE), per(w, 2 * SSM_STATE),
                  per(2 * SSM_STATE, w), per(2, 2 * SSM_STATE)],
        out_specs=per(rows, w),
        out_shape=jax.ShapeDtypeStruct((g, rows, w), F32),
        scratch_shapes=[state, state, state],
        compiler_params=_params(("arbitrary",)),
        name="ssm",
    )(u_flat, toep, b_state, b_swap, c_pow, a_step)


def _mix_kernel(x_ref, of_ref, yp_ref, u_ref, ga_ref, gb_ref, mod_ref, dsk_ref, wglu_ref, wfox_ref,
                wssm_ref, wo_ref, g2_ref, wr_ref, br_ref, x1_ref, h2_ref, lg_ref):
    y = yp_ref[...] + dsk_ref[...] * u_ref[...]
    y = 0.5 * y * (1.0 + jnp.tanh(math.sqrt(2.0 / math.pi) * (y + 0.044715 * (y * y * y))))
    gl = jnp.dot(y.astype(BF16), wglu_ref[...], preferred_element_type=F32)
    o_ssm = gl[:, :SSM_WIDTH] * jax.nn.sigmoid(gl[:, SSM_WIDTH:])
    merged = (ga_ref[...].astype(F32) * jnp.dot(of_ref[...], wfox_ref[...], preferred_element_type=F32)
              + gb_ref[...].astype(F32) * jnp.dot(o_ssm.astype(BF16), wssm_ref[...],
                                                  preferred_element_type=F32))
    x1 = x_ref[...] + mod_ref[2:3, :] * jnp.dot(merged.astype(BF16), wo_ref[...],
                                                 preferred_element_type=F32)
    x1_ref[...] = x1
    h2 = _rms_modulate(x1, g2_ref[...], mod_ref[3:4, :], mod_ref[4:5, :])
    h2_ref[...] = h2
    lg_ref[...] = jnp.dot(h2, wr_ref[...], precision=HIGHEST, preferred_element_type=F32) + br_ref[...]


def _mix(x2, o_fox, y_pre, u, sga, sgb, mod3, d_skip, w_glu, w_fox, w_ssm, w_o, g2, w_r, b_r, seq):
    n, d = x2.shape
    tm = MIX_TILE
    tpb = seq // tm
    tok = lambda w: pl.BlockSpec((tm, w), lambda i: (i, 0))
    const = lambda a: pl.BlockSpec(a.shape, lambda i: (0,) * a.ndim)
    return pl.pallas_call(
        _mix_kernel,
        grid=(n // tm,),
        in_specs=[tok(d), tok(FOX_WIDTH), tok(SSM_WIDTH), tok(SSM_WIDTH), tok(d), tok(d),
                  pl.BlockSpec((None, N_MOD, d), lambda i: (i // tpb, 0, 0)),
                  const(d_skip), const(w_glu), const(w_fox), const(w_ssm), const(w_o), const(g2),
                  const(w_r), const(b_r)],
        out_specs=[tok(d), tok(d), tok(LANES)],
        out_shape=[jax.ShapeDtypeStruct((n, d), F32), jax.ShapeDtypeStruct((n, d), F32),
                   jax.ShapeDtypeStruct((n, LANES), F32)],
        compiler_params=_params(("arbitrary",)),
        name="mix",
    )(x2, o_fox, y_pre, u, sga, sgb, mod3, d_skip, w_glu, w_fox, w_ssm, w_o, g2, w_r, b_r)


def _route_kernel(lg_ref, tri_ref, idx_ref, wt_ref, cnt_ref, carry_ref):
    i = pl.program_id(0)

    @pl.when(i == 0)
    def _():
        carry_ref[...] = jnp.zeros_like(carry_ref)

    lg = lg_ref[...]
    tm = lg.shape[0]
    lane = lax.broadcasted_iota(jnp.int32, (tm, LANES), 1)
    neg = jnp.full_like(lg, -jnp.inf)

    def first_argmax(vals):
        mx = jnp.max(vals, axis=1, keepdims=True)
        ix = jnp.min(jnp.where(vals == mx, lane, LANES), axis=1, keepdims=True)
        return mx, ix

    is_group = lane < N_GROUPS
    g_max, gi = first_argmax(jnp.where(is_group, lg, neg))
    g_sum = jnp.sum(jnp.where(is_group, jnp.exp(lg - g_max), 0.0), axis=1, keepdims=True)
    p_group = 1.0 / g_sum
    lo = N_GROUPS + EXPERTS_PER_GROUP * gi
    in_group = (lane >= lo) & (lane < lo + EXPERTS_PER_GROUP)
    cand = jnp.where(in_group, lg, neg)
    v1, i1 = first_argmax(cand)
    v2, i2 = first_argmax(jnp.where(lane == i1, neg, cand))
    tt = jnp.exp(v2 - v1)
    w1 = p_group / (1.0 + tt)
    w2 = p_group * tt / (1.0 + tt)
    e1 = i1 - N_GROUPS
    e2 = i2 - N_GROUPS
    sel1 = lane == e1
    sel2 = lane == e2
    onehot = (sel1 | sel2).astype(F32)
    before = jnp.dot(tri_ref[...], onehot.astype(BF16), preferred_element_type=F32) + carry_ref[0:1, :]
    r1 = jnp.sum(jnp.where(sel1, before, 0.0), axis=1, keepdims=True).astype(jnp.int32)
    r2 = jnp.sum(jnp.where(sel2, before, 0.0), axis=1, keepdims=True).astype(jnp.int32)
    total = before[-1:, :] + onehot[-1:, :]
    carry_ref[...] = jnp.broadcast_to(total, carry_ref.shape)
    cnt_ref[...] = jnp.broadcast_to(total, cnt_ref.shape)
    idx_ref[...] = jnp.where(lane == 0, e1, jnp.where(lane == 1, e2, jnp.where(lane == 2, r1, r2)))
    wt_ref[...] = jnp.where(lane == 0, w1, w2)


def _route(logits):
    n = logits.shape[0]
    tm = TOK_TILE
    tri = jnp.tril(jnp.ones((tm, tm), BF16), k=-1)
    tok = pl.BlockSpec((tm, LANES), lambda i: (i, 0))
    return pl.pallas_call(
        _route_kernel,
        grid=(n // tm,),
        in_specs=[tok, pl.BlockSpec((tm, tm), lambda i: (0, 0))],
        out_specs=[tok, tok, pl.BlockSpec((SUBLANES, LANES), lambda i: (0, 0))],
        out_shape=[jax.ShapeDtypeStruct((n, LANES), jnp.int32), jax.ShapeDtypeStruct((n, LANES), F32),
                   jax.ShapeDtypeStruct((SUBLANES, LANES), F32)],
        scratch_shapes=[pltpu.VMEM((SUBLANES, LANES), F32)],
        compiler_params=_params(("arbitrary",)),
        name="route",
    )(logits, tri)


def _row_copy(src_ref, src_row, dst_ref, dst_row, sem):
    return pltpu.make_async_copy(src_ref.at[pl.ds(src_row, 1), :], dst_ref.at[pl.ds(dst_row, 1), :], sem)


ISSUE_UNROLL = 8


def _dispatch_kernel(dest_ref, h_ref, rows_in_ref, rows_ref, sem):
    del rows_in_ref
    tm = h_ref.shape[0]

    def issue(g, c):
        for j in range(ISSUE_UNROLL):
            t = g * ISSUE_UNROLL + j
            _row_copy(h_ref, t, rows_ref, dest_ref[0, 0, 2 * t], sem).start(priority=0)
            _row_copy(h_ref, t, rows_ref, dest_ref[0, 0, 2 * t + 1], sem).start(priority=1)
        return c

    lax.fori_loop(0, tm // ISSUE_UNROLL, issue, 0)
    for _ in range(2):
        pltpu.make_async_copy(h_ref, rows_ref.at[pl.ds(0, tm), :], sem).wait()


def _dispatch(dest3, h2, rows_zero):
    n, d = h2.shape
    tm = MOVE_TILE
    return pl.pallas_call(
        _dispatch_kernel,
        grid=(n // tm,),
        in_specs=[pl.BlockSpec((1, 1, 2 * tm), lambda i: (i, 0, 0), memory_space=pltpu.SMEM),
                  pl.BlockSpec((tm, d), lambda i: (i, 0)),
                  pl.BlockSpec(memory_space=pl.ANY)],
        out_specs=pl.BlockSpec(memory_space=pl.ANY),
        out_shape=jax.ShapeDtypeStruct(rows_zero.shape, rows_zero.dtype),
        scratch_shapes=[pltpu.SemaphoreType.DMA(())],
        input_output_aliases={2: 0},
        compiler_params=_params(("arbitrary",)),
        name="dispatch",
    )(dest3, h2, rows_zero)


def _combine_kernel(dest_ref, x1_ref, wt_ref, mod_ref, gf_ref, yr_ref, o_ref, buf_ref, sem):
    tm = x1_ref.shape[0]

    def issue(g, c):
        for j in range(ISSUE_UNROLL):
            t = g * ISSUE_UNROLL + j
            _row_copy(yr_ref, dest_ref[0, 0, 2 * t], buf_ref.at[0], t, sem).start(priority=0)
            _row_copy(yr_ref, dest_ref[0, 0, 2 * t + 1], buf_ref.at[1], t, sem).start(priority=1)
        return c

    lax.fori_loop(0, tm // ISSUE_UNROLL, issue, 0)
    for slot in range(2):
        pltpu.make_async_copy(yr_ref.at[pl.ds(0, tm), :], buf_ref.at[slot], sem).wait()
    wt = wt_ref[...]
    moe = wt[:, 0:1] * buf_ref[0] + wt[:, 1:2] * buf_ref[1]
    x = x1_ref[...] + mod_ref[5:6, :] * moe
    ms = jnp.mean(x * x, axis=-1, keepdims=True)
    o_ref[...] = (x * lax.rsqrt(ms + RMS_EPS)) * gf_ref[...]


def _combine(dest3, x1, wts, mod3, final_g, y_rows, seq):
    n, d = x1.shape
    tm = MOVE_TILE
    tpb = seq // tm
    return pl.pallas_call(
        _combine_kernel,
        grid=(n // tm,),
        in_specs=[pl.BlockSpec((1, 1, 2 * tm), lambda i: (i, 0, 0), memory_space=pltpu.SMEM),
                  pl.BlockSpec((tm, d), lambda i: (i, 0)),
                  pl.BlockSpec((tm, LANES), lambda i: (i, 0)),
                  pl.BlockSpec((None, N_MOD, d), lambda i: (i // tpb, 0, 0)),
                  pl.BlockSpec((1, d), lambda i: (0, 0)),
                  pl.BlockSpec(memory_space=pl.ANY)],
        out_specs=pl.BlockSpec((tm, d), lambda i: (i, 0)),
        out_shape=jax.ShapeDtypeStruct((n, d), F32),
        scratch_shapes=[pltpu.VMEM((2, tm, d), F32), pltpu.SemaphoreType.DMA(())],
        compiler_params=_params(("arbitrary",)),
        name="combine",
    )(dest3, x1, wts, mod3, final_g, y_rows)


def _expert_kernel(be_ref, nv_ref, x_ref, wg_ref, wu_ref, wd_ref, y_ref, wgb_ref, wub_ref, wdb_ref):
    i = pl.program_id(0)
    valid = i < nv_ref[0]
    new_expert = (i == 0) | (be_ref[i] != be_ref[jnp.maximum(i - 1, 0)])

    @pl.when(valid & new_expert)
    def _():
        wgb_ref[...] = wg_ref[...].astype(BF16)
        wub_ref[...] = wu_ref[...].astype(BF16)
        wdb_ref[...] = wd_ref[...].astype(BF16)

    @pl.when(valid)
    def _():
        xb = x_ref[...].astype(BF16)
        a = jnp.dot(xb, wgb_ref[...], preferred_element_type=F32)
        b = jnp.dot(xb, wub_ref[...], preferred_element_type=F32)
        hid = (a * jax.nn.sigmoid(a)) * b
        y_ref[...] = jnp.dot(hid.astype(BF16), wdb_ref[...], preferred_element_type=F32)

    @pl.when(jnp.logical_not(valid))
    def _():
        y_ref[...] = jnp.zeros_like(y_ref)


def _experts(blk_e, n_valid, x_rows, w_gate, w_up, w_down):
    rows, d = x_rows.shape
    tb = ROW_BLOCK
    grid_spec = pltpu.PrefetchScalarGridSpec(
        num_scalar_prefetch=2,
        grid=(rows // tb,),
        in_specs=[pl.BlockSpec((tb, d), lambda i, be, nv: (i, 0)),
                  pl.BlockSpec((None, d, D_EXPERT), lambda i, be, nv: (be[i], 0, 0)),
                  pl.BlockSpec((None, d, D_EXPERT), lambda i, be, nv: (be[i], 0, 0)),
                  pl.BlockSpec((None, D_EXPERT, d), lambda i, be, nv: (be[i], 0, 0))],
        out_specs=pl.BlockSpec((tb, d), lambda i, be, nv: (i, 0)),
        scratch_shapes=[pltpu.VMEM((d, D_EXPERT), BF16), pltpu.VMEM((d, D_EXPERT), BF16),
                        pltpu.VMEM((D_EXPERT, d), BF16)],
    )
    return pl.pallas_call(
        _expert_kernel,
        grid_spec=grid_spec,
        out_shape=jax.ShapeDtypeStruct((rows, d), F32),
        compiler_params=_params(("arbitrary",)),
        name="experts",
    )(blk_e, n_valid, x_rows, w_gate, w_up, w_down)


def kernel(x, c, w_ada, b_ada, norm_mix_g, w_in, b_forget, w_out_fox, lambda_re, lambda_im, log_dt,
           ssm_b_re, ssm_b_im, ssm_c_re, ssm_c_im, d_skip, w_glu, w_out_ssm, w_o, norm_ffn_g,
           w_router_group, b_router_group, w_router_expert, b_router_expert, w_gate_e, w_up_e,
           w_down_e, final_g):
    bsz, seq, d = x.shape
    n = bsz * seq
    assert w_ada.shape[0] == 1, "the final RMSNorm is fused into the (single) layer's combine kernel"
    xc = x.reshape(n, d)
    for l in range(1):
        mod3 = _mod(c, w_ada[l], b_ada[l]).reshape(bsz, N_MOD, d)

        wi = w_in[l]
        s_q, s_k, s_v, s_f, s_u, s_ga = 512, 1024, 1536, 1544, 2056, 3080
        scale = FOX_HEAD_DIM ** -0.5
        w_all = jnp.concatenate(
            [wi[:, :s_q] * scale, wi[:, s_q:s_k], wi[:, s_f:s_u], wi[:, s_u:s_ga],
             wi[:, s_ga:], jnp.pad(wi[:, s_v:s_f], ((0, 0), (0, LANES - FOX_HEADS)))],
            axis=1).astype(BF16)
        w_vt = wi[:, s_k:s_v].T.astype(BF16)
        bf_pad = jnp.pad(b_forget[l], (0, LANES - FOX_HEADS)).reshape(1, LANES)
        q, k, v_t, u, sga, sgb, cum_pad = _inproj(xc, mod3, norm_mix_g[l].reshape(1, d), w_all, w_vt,
                                                  bf_pad, seq)

        aug_q, aug_k = _forget_bias_lanes(cum_pad[:, :FOX_HEADS].reshape(bsz, seq, FOX_HEADS), bsz, seq)
        o_fox = _attention(q, k, v_t, aug_q, aug_k, bsz, seq)

        toep, b_state, b_swap, c_pow, a_step = _ssm_matrices(
            lambda_re[l], lambda_im[l], log_dt[l], ssm_b_re[l], ssm_b_im[l], ssm_c_re[l], ssm_c_im[l])
        n_chunks = seq // SSM_CHUNK
        u_flat = (u.astype(BF16).reshape(bsz, n_chunks, SSM_CHUNK, SSM_GROUPS, SSM_GROUP)
                  .transpose(3, 1, 0, 2, 4).reshape(SSM_GROUPS, n_chunks * bsz, SSM_CHUNK * SSM_GROUP))
        y_flat = _ssm(u_flat, toep, b_state, b_swap, c_pow, a_step, bsz)
        y_pre = (y_flat.reshape(SSM_GROUPS, n_chunks, bsz, SSM_CHUNK, SSM_GROUP)
                 .transpose(2, 1, 3, 0, 4).reshape(n, SSM_WIDTH))

        w_r = jnp.pad(jnp.concatenate([w_router_group[l], w_router_expert[l]], axis=1),
                      ((0, 0), (0, LANES - N_GROUPS - N_EXPERTS)))
        b_r = jnp.pad(jnp.concatenate([b_router_group[l], b_router_expert[l]]),
                      (0, LANES - N_GROUPS - N_EXPERTS)).reshape(1, LANES)
        x1, h2, logits = _mix(xc, o_fox, y_pre, u, sga, sgb, mod3, d_skip[l].reshape(1, SSM_WIDTH),
                              w_glu[l].astype(BF16), w_out_fox[l].astype(BF16),
                              w_out_ssm[l].astype(BF16), w_o[l].astype(BF16),
                              norm_ffn_g[l].reshape(1, d), w_r, b_r, seq)

        idx, wts, cnt = _route(logits)
        counts = cnt[0, :N_EXPERTS].astype(jnp.int32)
        pcounts = ((counts + ROW_BLOCK - 1) // ROW_BLOCK) * ROW_BLOCK
        pends = jnp.cumsum(pcounts)
        pstarts = pends - pcounts
        dest = pstarts[idx[:, 0:2]] + idx[:, 2:4]
        rows = 2 * n + N_EXPERTS * ROW_BLOCK
        n_blocks = rows // ROW_BLOCK
        blk_start = jnp.arange(n_blocks, dtype=jnp.int32) * ROW_BLOCK
        blk_e = jnp.minimum(jnp.sum((pends[None, :] <= blk_start[:, None]).astype(jnp.int32), axis=1),
                            N_EXPERTS - 1)
        n_valid = (pends[-1:] // ROW_BLOCK).astype(jnp.int32)
        dest3 = dest.astype(jnp.int32).reshape(n // MOVE_TILE, 1, 2 * MOVE_TILE)

        x_rows = _dispatch(dest3, h2, jnp.zeros((rows, d), F32))
        y_rows = _experts(blk_e, n_valid, x_rows, w_gate_e[l], w_up_e[l], w_down_e[l])
        xc = _combine(dest3, x1, wts, mod3, final_g.reshape(1, d), y_rows, seq)
    return xc.reshape(bsz, seq, d)
```

```python
import functools
import math

import jax
import jax.numpy as jnp
import numpy as np
from jax import lax
from jax.experimental import pallas as pl
from jax.experimental.pallas import tpu as pltpu

F32 = jnp.float32
BF16 = jnp.bfloat16

D_MODEL = 1024
N_MOD = 6
RMS_EPS = 1e-6
FOX_HEADS = 8
FOX_HEAD_DIM = 64
FOX_WIDTH = FOX_HEADS * FOX_HEAD_DIM
HEAD_PAIRS = FOX_HEADS // 2
SSM_WIDTH = 512
SSM_GROUP = 16
SSM_GROUPS = SSM_WIDTH // SSM_GROUP
SSM_STATE = 64
LAMBDA_RE_MAX = -1e-4
N_GROUPS = 4
EXPERTS_PER_GROUP = 8
N_EXPERTS = N_GROUPS * EXPERTS_PER_GROUP
D_EXPERT = 512

LANES = 128
SUBLANES = 8
VMEM_LIMIT = 56 * 1024 * 1024

SSM_CHUNK = 16
TOK_TILE = 512
MIX_TILE = 256
ATT_TILE = 256
ROW_BLOCK = 256
MOVE_TILE = 256
NEG_BIG = -1e30

HIGHEST = lax.Precision.HIGHEST


def _params(sem):
    return pltpu.CompilerParams(dimension_semantics=sem, vmem_limit_bytes=VMEM_LIMIT)


def _rms_modulate(x, gain, shift, scale):
    ms = jnp.mean(x * x, axis=-1, keepdims=True)
    return (x * lax.rsqrt(ms + RMS_EPS)) * gain * (1.0 + scale) + shift


def _mod_kernel(c_ref, w_ref, b_ref, o_ref):
    c = c_ref[...]
    ca = (c * jax.nn.sigmoid(c)).astype(BF16)
    o_ref[...] = jnp.dot(ca, w_ref[...].astype(BF16), preferred_element_type=F32) + b_ref[...]


def _mod(c, w_ada, b_ada):
    bsz, d = c.shape
    cols = w_ada.shape[1]
    tn = 1536
    return pl.pallas_call(
        _mod_kernel,
        grid=(cols // tn,),
        in_specs=[pl.BlockSpec((bsz, d), lambda j: (0, 0)),
                  pl.BlockSpec((d, tn), lambda j: (0, j)),
                  pl.BlockSpec((1, tn), lambda j: (0, j))],
        out_specs=pl.BlockSpec((bsz, tn), lambda j: (0, j)),
        out_shape=jax.ShapeDtypeStruct((bsz, cols), F32),
        compiler_params=_params(("arbitrary",)),
        name="mod",
    )(c, w_ada, b_ada.reshape(1, cols))


_C_Q, _C_K, _C_U, _C_GA, _C_GB, _C_F, _C_END = 0, 512, 1024, 1536, 2560, 3584, 3712


def _lane_block():
    return lax.broadcasted_iota(jnp.int32, (1, LANES), 1) // SSM_GROUP


def _to_group_major(tok_ref, flat_ref, rows):
    blk = _lane_block()
    for half in range(2):
        for j in range(SSM_WIDTH // LANES):
            w = []
            for s8 in range(8):
                v = tok_ref[j, pl.ds(8 * half + s8, rows, stride=SSM_CHUNK), :]
                w.append(pltpu.roll(v, s8 * SSM_GROUP, axis=1) if s8 else v)
            for p in range(8):
                acc = w[0]
                for s8 in range(1, 8):
                    acc = jnp.where(blk == (p + s8) % 8, w[s8], acc)
                flat_ref[8 * j + p, :, half * LANES:(half + 1) * LANES] = acc.astype(flat_ref.dtype)


def _to_token_major(flat_ref, tok_ref, rows):
    blk = _lane_block()
    for half in range(2):
        for j in range(SSM_WIDTH // LANES):
            ys = [flat_ref[8 * j + p, :, half * LANES:(half + 1) * LANES] for p in range(8)]
            for s8 in range(8):
                acc = ys[0]
                for p in range(1, 8):
                    acc = jnp.where(blk == (p + s8) % 8, ys[p], acc)
                if s8:
                    acc = pltpu.roll(acc, LANES - s8 * SSM_GROUP, axis=1)
                tok_ref[j, pl.ds(8 * half + s8, rows, stride=SSM_CHUNK), :] = acc


def _chunk_lane_order():
    pos = np.arange(SSM_CHUNK * SSM_GROUP)
    half, q, c = pos // LANES, (pos % LANES) // SSM_GROUP, pos % SSM_GROUP
    p = (np.arange(SSM_GROUPS) % 8)[:, None]
    return (8 * half[None, :] + (q[None, :] - p) % 8) * SSM_GROUP + c[None, :]


def _inproj_kernel(tiles_per_batch, x_ref, mod_ref, g_ref, w_ref, wvt_ref, bf_ref, tri_ref,
                   q_ref, k_ref, vt_ref, u_ref, uflat_ref, ga_ref, gb_ref, cum_ref, carry_ref, uslab_ref):
    i = pl.program_id(0)
    h = _rms_modulate(x_ref[...], g_ref[...], mod_ref[0:1, :], mod_ref[1:2, :])
    hb = h.astype(BF16)

    def proj(a, b):
        return jnp.dot(hb, w_ref[:, a:b], preferred_element_type=F32)

    q_ref[...] = proj(_C_Q, _C_K).astype(BF16)
    k_ref[...] = proj(_C_K, _C_U).astype(BF16)
    vt_ref[...] = lax.dot_general(wvt_ref[...], hb, (((1,), (1,)), ((), ())),
                                  preferred_element_type=F32).astype(BF16)
    u = proj(_C_U, _C_GA)
    u_ref[...] = u
    for j in range(SSM_WIDTH // LANES):
        uslab_ref[j] = u[:, j * LANES:(j + 1) * LANES]
    _to_group_major(uslab_ref, uflat_ref, u.shape[0] // SSM_CHUNK)
    ga_ref[...] = jax.nn.sigmoid(proj(_C_GA, _C_GB)).astype(BF16)
    gb_ref[...] = jax.nn.sigmoid(proj(_C_GB, _C_F)).astype(BF16)

    f = proj(_C_F, _C_END) + bf_ref[...]
    logf = jnp.minimum(f, 0.0) - jnp.log(1.0 + jnp.exp(-jnp.abs(f)))

    @pl.when(i % tiles_per_batch == 0)
    def _():
        carry_ref[...] = jnp.zeros_like(carry_ref)

    cs = jnp.dot(tri_ref[...], logf, precision=HIGHEST, preferred_element_type=F32) + carry_ref[0:1, :]
    cum_ref[...] = cs
    carry_ref[...] = jnp.broadcast_to(cs[-1:, :], carry_ref.shape)


def _inproj(x2, mod3, gain, w_all, w_vt, bf_pad, seq):
    n, d = x2.shape
    tm = TOK_TILE
    tpb = seq // tm
    tri = jnp.tril(jnp.ones((tm, tm), F32))
    tok = lambda w: pl.BlockSpec((tm, w), lambda i: (i, 0))
    const = lambda shape: pl.BlockSpec(shape, lambda i: (0,) * len(shape))
    return pl.pallas_call(
        functools.partial(_inproj_kernel, tpb),
        grid=(n // tm,),
        in_specs=[tok(d),
                  pl.BlockSpec((None, N_MOD, d), lambda i: (i // tpb, 0, 0)),
                  const((1, d)), const((d, _C_END)), const((FOX_WIDTH, d)), const((1, LANES)),
                  const((tm, tm))],
        out_specs=[tok(FOX_WIDTH), tok(FOX_WIDTH), pl.BlockSpec((FOX_WIDTH, tm), lambda i: (0, i)),
                   tok(SSM_WIDTH),
                   pl.BlockSpec((SSM_GROUPS, tm // SSM_CHUNK, SSM_CHUNK * SSM_GROUP), lambda i: (0, i, 0)),
                   tok(d), tok(d), tok(LANES)],
        out_shape=[jax.ShapeDtypeStruct((n, FOX_WIDTH), BF16)] * 2
        + [jax.ShapeDtypeStruct((FOX_WIDTH, n), BF16)]
        + [jax.ShapeDtypeStruct((n, SSM_WIDTH), F32)]
        + [jax.ShapeDtypeStruct((SSM_GROUPS, n // SSM_CHUNK, SSM_CHUNK * SSM_GROUP), BF16)]
        + [jax.ShapeDtypeStruct((n, d), BF16)] * 2
        + [jax.ShapeDtypeStruct((n, LANES), F32)],
        scratch_shapes=[pltpu.VMEM((SUBLANES, LANES), F32),
                        pltpu.VMEM((SSM_WIDTH // LANES, tm, LANES), F32)],
        compiler_params=_params(("arbitrary",)),
        name="inproj",
    )(x2, mod3, gain, w_all, w_vt, bf_pad, tri)


def _attn_kernel(q_ref, aq_ref, k_ref, ak_ref, vt_ref, o_ref, m_ref, acc_ref):
    i = pl.program_id(2)
    t = ATT_TILE
    q = q_ref[...]
    aq = aq_ref[...]
    lane = lax.broadcasted_iota(jnp.int32, (1, LANES), 1)
    zq = jnp.zeros_like(q)
    half = FOX_HEAD_DIM
    q_both = jnp.concatenate([
        jnp.concatenate([jnp.where(lane < half, q, zq), jnp.where(lane < 8, aq, zq)], axis=1),
        jnp.concatenate([jnp.where(lane >= half, q, zq),
                         jnp.where((lane >= 8) & (lane < 16), aq, zq)], axis=1)], axis=0)
    m_ref[...] = jnp.full(m_ref.shape, NEG_BIG, F32)
    acc_ref[...] = jnp.zeros(acc_ref.shape, F32)
    ones_rows = jnp.ones((2 * SUBLANES, t), BF16)

    def scores(j):
        start = pl.multiple_of(j * t, t)
        kk = jnp.concatenate([k_ref[pl.ds(start, t), :], ak_ref[pl.ds(start, t), :]], axis=1)
        return lax.dot_general(kk, q_both, (((1,), (1,)), ((), ())), preferred_element_type=F32)

    def causal(s, j):
        key = j * t + lax.broadcasted_iota(jnp.int32, (t, 2 * t), 0)
        qry = i * t + (lax.broadcasted_iota(jnp.int32, (t, 2 * t), 1) & (t - 1))
        return jnp.where(key <= qry, s, NEG_BIG)

    def accumulate(j, s):
        start = pl.multiple_of(j * t, t)
        va = jnp.concatenate([vt_ref[:, pl.ds(start, t)], ones_rows], axis=0)
        m_old = m_ref[...]
        m_new = jnp.maximum(m_old, jnp.max(s, axis=0, keepdims=True))
        alpha = jnp.exp(m_old - m_new)
        p = jnp.exp(s - m_new).astype(BF16)
        acc_ref[...] = alpha * acc_ref[...] + jnp.dot(va, p, preferred_element_type=F32)
        m_ref[...] = m_new

    def body(j, s_cur):
        s_next = scores(j + 1)
        accumulate(j, s_cur)
        return s_next

    s_cur = lax.fori_loop(0, jnp.maximum(i - 1, 0), body, causal(scores(0), 0))

    @pl.when(i == 0)
    def _():
        accumulate(0, s_cur)

    @pl.when(i > 0)
    def _():
        s_diag = causal(scores(i), i)
        accumulate(i - 1, s_cur)
        accumulate(i, s_diag)

    acc = acc_ref[...]
    o_t = jnp.concatenate([acc[0:half, 0:t] / acc[LANES:LANES + 1, 0:t],
                           acc[half:LANES, t:2 * t] / acc[LANES:LANES + 1, t:2 * t]], axis=0)
    o_ref[...] = o_t.T.astype(o_ref.dtype)


def _attention(q, k, v_t, aug_q, aug_k, bsz, seq):
    n = q.shape[0]
    t = ATT_TILE
    nq = seq // t
    return pl.pallas_call(
        _attn_kernel,
        grid=(bsz, HEAD_PAIRS, nq),
        in_specs=[pl.BlockSpec((t, LANES), lambda b, p, i: (b * nq + i, p)),
                  pl.BlockSpec((None, None, t, LANES), lambda b, p, i: (b, p, i, 0)),
                  pl.BlockSpec((seq, LANES), lambda b, p, i: (b, p)),
                  pl.BlockSpec((None, None, seq, LANES), lambda b, p, i: (b, p, 0, 0)),
                  pl.BlockSpec((LANES, seq), lambda b, p, i: (p, b))],
        out_specs=pl.BlockSpec((t, LANES), lambda b, p, i: (b * nq + i, p)),
        out_shape=jax.ShapeDtypeStruct((n, FOX_WIDTH), BF16),
        scratch_shapes=[pltpu.VMEM((1, 2 * t), F32), pltpu.VMEM((LANES + 2 * SUBLANES, 2 * t), F32)],
        compiler_params=_params(("arbitrary", "arbitrary", "arbitrary")),
        name="attn",
    )(q, aug_q, k, aug_k, v_t)


def _forget_bias_lanes(cum, bsz, seq):
    def top_bits(a):
        bits = lax.bitcast_convert_type(a, jnp.uint32) & jnp.uint32(0xFFFF0000)
        return lax.bitcast_convert_type(bits, F32)

    hi32 = top_bits(cum)
    r1 = cum - hi32
    mid32 = top_bits(r1)
    hi, mid, lo = hi32.astype(BF16), mid32.astype(BF16), top_bits(r1 - mid32).astype(BF16)
    one = jnp.ones_like(hi)
    zero = jnp.zeros_like(hi)
    qa = jnp.stack([hi, mid, lo, one, one, one, zero, zero], axis=-1)
    ka = jnp.stack([one, one, one, -hi, -mid, -lo, zero, zero], axis=-1)

    def lay(a):
        a = a.reshape(bsz, seq, HEAD_PAIRS, 16).transpose(0, 2, 1, 3)
        return jnp.pad(a, ((0, 0), (0, 0), (0, 0), (0, LANES - 16)))

    return lay(qa), lay(ka)


def _ssm_matrices(lambda_re, lambda_im, log_dt, b_re, b_im, c_re, c_im):
    t_len = SSM_CHUNK
    lam_re = jnp.minimum(lambda_re.astype(F32), LAMBDA_RE_MAX)
    lam_im = lambda_im.astype(F32)
    dt = jnp.exp(log_dt.astype(F32))[:, None]
    mag = jnp.exp(lam_re * dt)
    a_re = mag * jnp.cos(lam_im * dt)
    a_im = mag * jnp.sin(lam_im * dt)
    den = lam_re * lam_re + lam_im * lam_im
    nr = a_re - 1.0
    co_re = (nr * lam_re + a_im * lam_im) / den
    co_im = (a_im * lam_re - nr * lam_im) / den
    b_re = b_re.astype(F32)
    b_im = b_im.astype(F32)
    bb_re = co_re[..., None] * b_re - co_im[..., None] * b_im
    bb_im = co_re[..., None] * b_im + co_im[..., None] * b_re
    ls = jnp.arange(t_len + 1, dtype=F32)[:, None, None]
    pmag = jnp.exp(ls * (lam_re * dt)[None])
    pang = ls * (lam_im * dt)[None]
    pw_re = pmag * jnp.cos(pang)
    pw_im = pmag * jnp.sin(pang)
    c_re = c_re.astype(F32)
    c_im = c_im.astype(F32)
    w_re = c_re[None] * pw_re[:, :, None, :] - c_im[None] * pw_im[:, :, None, :]
    w_im = c_re[None] * pw_im[:, :, None, :] + c_im[None] * pw_re[:, :, None, :]
    kern = (jnp.einsum('lgdp,gpc->gldc', w_re[:t_len], bb_re, precision=HIGHEST)
            - jnp.einsum('lgdp,gpc->gldc', w_im[:t_len], bb_im, precision=HIGHEST))
    s_idx = np.arange(t_len)[:, None]
    t_idx = np.arange(t_len)[None, :]
    lag = np.clip(t_idx - s_idx, 0, t_len - 1)
    causal = jnp.asarray(t_idx >= s_idx)
    toep = kern[:, lag]
    toep = jnp.where(causal[None, :, :, None, None], toep, 0.0)
    toep = toep.transpose(0, 1, 4, 2, 3).reshape(SSM_GROUPS, t_len * SSM_GROUP, t_len * SSM_GROUP)
    e_re = pw_re[t_len - 1 - np.arange(t_len)]
    e_im = pw_im[t_len - 1 - np.arange(t_len)]
    bs_re = e_re[..., None] * bb_re[None] - e_im[..., None] * bb_im[None]
    bs_im = e_re[..., None] * bb_im[None] + e_im[..., None] * bb_re[None]
    def lay_state(first, second):
        m = jnp.concatenate([first, second], axis=2)
        return m.transpose(1, 0, 3, 2).reshape(SSM_GROUPS, t_len * SSM_GROUP, 2 * SSM_STATE)

    b_state = lay_state(bs_re, bs_im)
    b_swap = lay_state(bs_im, bs_re)
    cp = jnp.concatenate([w_re[1:], -w_im[1:]], axis=3)
    c_pow = cp.transpose(1, 3, 0, 2).reshape(SSM_GROUPS, 2 * SSM_STATE, t_len * SSM_GROUP)
    a1 = jnp.concatenate([pw_re[t_len], pw_re[t_len]], axis=-1)
    a2 = jnp.concatenate([-pw_im[t_len], pw_im[t_len]], axis=-1)
    a_step = jnp.stack([a1, a2], axis=1)
    order = jnp.asarray(_chunk_lane_order())
    in_rows = lambda m: jnp.take_along_axis(m, order[:, :, None], axis=1)
    out_cols = lambda m: jnp.take_along_axis(m, order[:, None, :], axis=2)
    toep, b_state, b_swap, c_pow = out_cols(in_rows(toep)), in_rows(b_state), in_rows(b_swap), out_cols(c_pow)
    return toep.astype(BF16), b_state.astype(BF16), b_swap.astype(BF16), c_pow.astype(BF16), a_step


def _ssm_kernel(n_chunks, bsz, u_ref, toep_ref, bst_ref, bsw_ref, cpw_ref, a_ref, y_ref,
                contrib_ref, cswap_ref, xprev_ref):
    u = u_ref[...]
    contrib_ref[...] = jnp.dot(u, bst_ref[...], preferred_element_type=F32)
    cswap_ref[...] = jnp.dot(u, bsw_ref[...], preferred_element_type=F32)
    a1 = a_ref[0:1, :]
    a2 = a_ref[1:2, :]

    def step(n, carry):
        x, xs = carry
        rows = pl.ds(n, bsz, stride=n_chunks)
        xprev_ref[rows, :] = x
        x_new = a1 * x + a2 * xs + contrib_ref[rows, :]
        xs_new = a1 * xs - a2 * x + cswap_ref[rows, :]
        return x_new, xs_new

    zero = jnp.zeros((bsz, 2 * SSM_STATE), F32)
    lax.fori_loop(0, n_chunks, step, (zero, zero), unroll=4)
    y_ref[...] = (jnp.dot(u, toep_ref[...], preferred_element_type=F32)
                  + jnp.dot(xprev_ref[...].astype(BF16), cpw_ref[...], preferred_element_type=F32))


def _ssm(u_flat, toep, b_state, b_swap, c_pow, a_step, bsz):
    g, rows, w = u_flat.shape
    per = lambda a, b: pl.BlockSpec((None, a, b), lambda i: (i, 0, 0))
    state = pltpu.VMEM((rows, 2 * SSM_STATE), F32)
    return pl.pallas_call(
        functools.partial(_ssm_kernel, rows // bsz, bsz),
        grid=(g,),
        in_specs=[per(rows, w), per(w, w), per(w, 2 * SSM_STATE), per(w, 2 * SSM_STATE),
                  per(2 * SSM_STATE, w), per(2, 2 * SSM_STATE)],
        out_specs=per(rows, w),
        out_shape=jax.ShapeDtypeStruct((g, rows, w), F32),
        scratch_shapes=[state, state, state],
        compiler_params=_params(("arbitrary",)),
        name="ssm",
    )(u_flat, toep, b_state, b_swap, c_pow, a_step)


def _mix_kernel(x_ref, of_ref, yf_ref, u_ref, ga_ref, gb_ref, mod_ref, dsk_ref, wglu_ref, wfox_ref,
                wssm_ref, wo_ref, g2_ref, wr_ref, br_ref, x1_ref, h2_ref, lg_ref, ytok_ref):
    _to_token_major(yf_ref, ytok_ref, yf_ref.shape[1])
    y_ssm = jnp.concatenate([ytok_ref[j] for j in range(SSM_WIDTH // LANES)], axis=1)
    y = y_ssm + dsk_ref[...] * u_ref[...]
    y = 0.5 * y * (1.0 + jnp.tanh(math.sqrt(2.0 / math.pi) * (y + 0.044715 * (y * y * y))))
    gl = jnp.dot(y.astype(BF16), wglu_ref[...], preferred_element_type=F32)
    o_ssm = gl[:, :SSM_WIDTH] * jax.nn.sigmoid(gl[:, SSM_WIDTH:])
    merged = (ga_ref[...].astype(F32) * jnp.dot(of_ref[...], wfox_ref[...], preferred_element_type=F32)
              + gb_ref[...].astype(F32) * jnp.dot(o_ssm.astype(BF16), wssm_ref[...],
                                                  preferred_element_type=F32))
    x1 = x_ref[...] + mod_ref[2:3, :] * jnp.dot(merged.astype(BF16), wo_ref[...],
                                                 preferred_element_type=F32)
    x1_ref[...] = x1
    h2 = _rms_modulate(x1, g2_ref[...], mod_ref[3:4, :], mod_ref[4:5, :])
    h2_ref[...] = h2
    lg_ref[...] = jnp.dot(h2, wr_ref[...], precision=HIGHEST, preferred_element_type=F32) + br_ref[...]


def _mix(x2, o_fox, y_flat, u, sga, sgb, mod3, d_skip, w_glu, w_fox, w_ssm, w_o, g2, w_r, b_r, seq):
    n, d = x2.shape
    tm = MIX_TILE
    tpb = seq // tm
    tok = lambda w: pl.BlockSpec((tm, w), lambda i: (i, 0))
    const = lambda a: pl.BlockSpec(a.shape, lambda i: (0,) * a.ndim)
    flat = pl.BlockSpec((SSM_GROUPS, tm // SSM_CHUNK, SSM_CHUNK * SSM_GROUP), lambda i: (0, i, 0))
    return pl.pallas_call(
        _mix_kernel,
        grid=(n // tm,),
        in_specs=[tok(d), tok(FOX_WIDTH), flat, tok(SSM_WIDTH), tok(d), tok(d),
                  pl.BlockSpec((None, N_MOD, d), lambda i: (i // tpb, 0, 0)),
                  const(d_skip), const(w_glu), const(w_fox), const(w_ssm), const(w_o), const(g2),
                  const(w_r), const(b_r)],
        out_specs=[tok(d), tok(d), tok(LANES)],
        out_shape=[jax.ShapeDtypeStruct((n, d), F32), jax.ShapeDtypeStruct((n, d), F32),
                   jax.ShapeDtypeStruct((n, LANES), F32)],
        scratch_shapes=[pltpu.VMEM((SSM_WIDTH // LANES, tm, LANES), F32)],
        compiler_params=_params(("arbitrary",)),
        name="mix",
    )(x2, o_fox, y_flat, u, sga, sgb, mod3, d_skip, w_glu, w_fox, w_ssm, w_o, g2, w_r, b_r)


def _route_kernel(lg_ref, tri_ref, idx_ref, wt_ref, cnt_ref, carry_ref):
    i = pl.program_id(0)

    @pl.when(i == 0)
    def _():
        carry_ref[...] = jnp.zeros_like(carry_ref)

    lg = lg_ref[...]
    tm = lg.shape[0]
    lane = lax.broadcasted_iota(jnp.int32, (tm, LANES), 1)
    neg = jnp.full_like(lg, -jnp.inf)

    def first_argmax(vals):
        mx = jnp.max(vals, axis=1, keepdims=True)
        ix = jnp.min(jnp.where(vals == mx, lane, LANES), axis=1, keepdims=True)
        return mx, ix

    is_group = lane < N_GROUPS
    g_max, gi = first_argmax(jnp.where(is_group, lg, neg))
    g_sum = jnp.sum(jnp.where(is_group, jnp.exp(lg - g_max), 0.0), axis=1, keepdims=True)
    p_group = 1.0 / g_sum
    lo = N_GROUPS + EXPERTS_PER_GROUP * gi
    in_group = (lane >= lo) & (lane < lo + EXPERTS_PER_GROUP)
    cand = jnp.where(in_group, lg, neg)
    v1, i1 = first_argmax(cand)
    v2, i2 = first_argmax(jnp.where(lane == i1, neg, cand))
    tt = jnp.exp(v2 - v1)
    w1 = p_group / (1.0 + tt)
    w2 = p_group * tt / (1.0 + tt)
    e1 = i1 - N_GROUPS
    e2 = i2 - N_GROUPS
    sel1 = lane == e1
    sel2 = lane == e2
    onehot = (sel1 | sel2).astype(F32)
    before = jnp.dot(tri_ref[...], onehot.astype(BF16), preferred_element_type=F32) + carry_ref[0:1, :]
    r1 = jnp.sum(jnp.where(sel1, before, 0.0), axis=1, keepdims=True).astype(jnp.int32)
    r2 = jnp.sum(jnp.where(sel2, before, 0.0), axis=1, keepdims=True).astype(jnp.int32)
    total = before[-1:, :] + onehot[-1:, :]
    carry_ref[...] = jnp.broadcast_to(total, carry_ref.shape)
    cnt_ref[...] = jnp.broadcast_to(total, cnt_ref.shape)
    idx_ref[...] = jnp.where(lane == 0, e1, jnp.where(lane == 1, e2, jnp.where(lane == 2, r1, r2)))
    wt_ref[...] = jnp.where(lane == 0, w1, w2)


def _route(logits):
    n = logits.shape[0]
    tm = TOK_TILE
    tri = jnp.tril(jnp.ones((tm, tm), BF16), k=-1)
    tok = pl.BlockSpec((tm, LANES), lambda i: (i, 0))
    return pl.pallas_call(
        _route_kernel,
        grid=(n // tm,),
        in_specs=[tok, pl.BlockSpec((tm, tm), lambda i: (0, 0))],
        out_specs=[tok, tok, pl.BlockSpec((SUBLANES, LANES), lambda i: (0, 0))],
        out_shape=[jax.ShapeDtypeStruct((n, LANES), jnp.int32), jax.ShapeDtypeStruct((n, LANES), F32),
                   jax.ShapeDtypeStruct((SUBLANES, LANES), F32)],
        scratch_shapes=[pltpu.VMEM((SUBLANES, LANES), F32)],
        compiler_params=_params(("arbitrary",)),
        name="route",
    )(logits, tri)


def _row_copy(src_ref, src_row, dst_ref, dst_row, sem):
    return pltpu.make_async_copy(src_ref.at[pl.ds(src_row, 1), :], dst_ref.at[pl.ds(dst_row, 1), :], sem)


ISSUE_UNROLL = 8


def _dispatch_kernel(dest_ref, h_ref, rows_in_ref, rows_ref, sem):
    del rows_in_ref
    tm = h_ref.shape[0]

    def issue(g, c):
        for j in range(ISSUE_UNROLL):
            t = g * ISSUE_UNROLL + j
            _row_copy(h_ref, t, rows_ref, dest_ref[0, 0, 2 * t], sem).start(priority=0)
            _row_copy(h_ref, t, rows_ref, dest_ref[0, 0, 2 * t + 1], sem).start(priority=1)
        return c

    lax.fori_loop(0, tm // ISSUE_UNROLL, issue, 0)
    for _ in range(2):
        pltpu.make_async_copy(h_ref, rows_ref.at[pl.ds(0, tm), :], sem).wait()


def _dispatch(dest3, h2, rows_zero):
    n, d = h2.shape
    tm = MOVE_TILE
    return pl.pallas_call(
        _dispatch_kernel,
        grid=(n // tm,),
        in_specs=[pl.BlockSpec((1, 1, 2 * tm), lambda i: (i, 0, 0), memory_space=pltpu.SMEM),
                  pl.BlockSpec((tm, d), lambda i: (i, 0)),
                  pl.BlockSpec(memory_space=pl.ANY)],
        out_specs=pl.BlockSpec(memory_space=pl.ANY),
        out_shape=jax.ShapeDtypeStruct(rows_zero.shape, rows_zero.dtype),
        scratch_shapes=[pltpu.SemaphoreType.DMA(())],
        input_output_aliases={2: 0},
        compiler_params=_params(("arbitrary",)),
        name="dispatch",
    )(dest3, h2, rows_zero)


def _combine_kernel(dest_ref, x1_ref, wt_ref, mod_ref, gf_ref, yr_ref, o_ref, buf_ref, sem):
    tm = x1_ref.shape[0]

    def issue(g, c):
        for j in range(ISSUE_UNROLL):
            t = g * ISSUE_UNROLL + j
            _row_copy(yr_ref, dest_ref[0, 0, 2 * t], buf_ref.at[0], t, sem).start(priority=0)
            _row_copy(yr_ref, dest_ref[0, 0, 2 * t + 1], buf_ref.at[1], t, sem).start(priority=1)
        return c

    lax.fori_loop(0, tm // ISSUE_UNROLL, issue, 0)
    for slot in range(2):
        pltpu.make_async_copy(yr_ref.at[pl.ds(0, tm), :], buf_ref.at[slot], sem).wait()
    wt = wt_ref[...]
    moe = wt[:, 0:1] * buf_ref[0] + wt[:, 1:2] * buf_ref[1]
    x = x1_ref[...] + mod_ref[5:6, :] * moe
    ms = jnp.mean(x * x, axis=-1, keepdims=True)
    o_ref[...] = (x * lax.rsqrt(ms + RMS_EPS)) * gf_ref[...]


def _combine(dest3, x1, wts, mod3, final_g, y_rows, seq):
    n, d = x1.shape
    tm = MOVE_TILE
    tpb = seq // tm
    return pl.pallas_call(
        _combine_kernel,
        grid=(n // tm,),
        in_specs=[pl.BlockSpec((1, 1, 2 * tm), lambda i: (i, 0, 0), memory_space=pltpu.SMEM),
                  pl.BlockSpec((tm, d), lambda i: (i, 0)),
                  pl.BlockSpec((tm, LANES), lambda i: (i, 0)),
                  pl.BlockSpec((None, N_MOD, d), lambda i: (i // tpb, 0, 0)),
                  pl.BlockSpec((1, d), lambda i: (0, 0)),
                  pl.BlockSpec(memory_space=pl.ANY)],
        out_specs=pl.BlockSpec((tm, d), lambda i: (i, 0)),
        out_shape=jax.ShapeDtypeStruct((n, d), F32),
        scratch_shapes=[pltpu.VMEM((2, tm, d), F32), pltpu.SemaphoreType.DMA(())],
        compiler_params=_params(("arbitrary",)),
        name="combine",
    )(dest3, x1, wts, mod3, final_g, y_rows)


def _expert_kernel(be_ref, nv_ref, x_ref, wg_ref, wu_ref, wd_ref, y_ref, wgb_ref, wub_ref, wdb_ref):
    i = pl.program_id(0)
    valid = i < nv_ref[0]
    new_expert = (i == 0) | (be_ref[i] != be_ref[jnp.maximum(i - 1, 0)])

    @pl.when(valid & new_expert)
    def _():
        wgb_ref[...] = wg_ref[...].astype(BF16)
        wub_ref[...] = wu_ref[...].astype(BF16)
        wdb_ref[...] = wd_ref[...].astype(BF16)

    @pl.when(valid)
    def _():
        xb = x_ref[...].astype(BF16)
        a = jnp.dot(xb, wgb_ref[...], preferred_element_type=F32)
        b = jnp.dot(xb, wub_ref[...], preferred_element_type=F32)
        hid = (a * jax.nn.sigmoid(a)) * b
        y_ref[...] = jnp.dot(hid.astype(BF16), wdb_ref[...], preferred_element_type=F32)

    @pl.when(jnp.logical_not(valid))
    def _():
        y_ref[...] = jnp.zeros_like(y_ref)


def _experts(blk_e, n_valid, x_rows, w_gate, w_up, w_down):
    rows, d = x_rows.shape
    tb = ROW_BLOCK
    grid_spec = pltpu.PrefetchScalarGridSpec(
        num_scalar_prefetch=2,
        grid=(rows // tb,),
        in_specs=[pl.BlockSpec((tb, d), lambda i, be, nv: (i, 0)),
                  pl.BlockSpec((None, d, D_EXPERT), lambda i, be, nv: (be[i], 0, 0)),
                  pl.BlockSpec((None, d, D_EXPERT), lambda i, be, nv: (be[i], 0, 0)),
                  pl.BlockSpec((None, D_EXPERT, d), lambda i, be, nv: (be[i], 0, 0))],
        out_specs=pl.BlockSpec((tb, d), lambda i, be, nv: (i, 0)),
        scratch_shapes=[pltpu.VMEM((d, D_EXPERT), BF16), pltpu.VMEM((d, D_EXPERT), BF16),
                        pltpu.VMEM((D_EXPERT, d), BF16)],
    )
    return pl.pallas_call(
        _expert_kernel,
        grid_spec=grid_spec,
        out_shape=jax.ShapeDtypeStruct((rows, d), F32),
        compiler_params=_params(("arbitrary",)),
        name="experts",
    )(blk_e, n_valid, x_rows, w_gate, w_up, w_down)


def kernel(x, c, w_ada, b_ada, norm_mix_g, w_in, b_forget, w_out_fox, lambda_re, lambda_im, log_dt,
           ssm_b_re, ssm_b_im, ssm_c_re, ssm_c_im, d_skip, w_glu, w_out_ssm, w_o, norm_ffn_g,
           w_router_group, b_router_group, w_router_expert, b_router_expert, w_gate_e, w_up_e,
           w_down_e, final_g):
    bsz, seq, d = x.shape
    n = bsz * seq
    assert w_ada.shape[0] == 1, "the final RMSNorm is fused into the (single) layer's combine kernel"
    xc = x.reshape(n, d)
    for l in range(1):
        mod3 = _mod(c, w_ada[l], b_ada[l]).reshape(bsz, N_MOD, d)

        wi = w_in[l]
        s_q, s_k, s_v, s_f, s_u, s_ga = 512, 1024, 1536, 1544, 2056, 3080
        scale = FOX_HEAD_DIM ** -0.5
        w_all = jnp.concatenate(
            [wi[:, :s_q] * scale, wi[:, s_q:s_k], wi[:, s_f:s_u], wi[:, s_u:s_ga],
             wi[:, s_ga:], jnp.pad(wi[:, s_v:s_f], ((0, 0), (0, LANES - FOX_HEADS)))],
            axis=1).astype(BF16)
        w_vt = wi[:, s_k:s_v].T.astype(BF16)
        bf_pad = jnp.pad(b_forget[l], (0, LANES - FOX_HEADS)).reshape(1, LANES)
        q, k, v_t, u, u_flat, sga, sgb, cum_pad = _inproj(xc, mod3, norm_mix_g[l].reshape(1, d), w_all,
                                                          w_vt, bf_pad, seq)

        aug_q, aug_k = _forget_bias_lanes(cum_pad[:, :FOX_HEADS].reshape(bsz, seq, FOX_HEADS), bsz, seq)
        o_fox = _attention(q, k, v_t, aug_q, aug_k, bsz, seq)

        toep, b_state, b_swap, c_pow, a_step = _ssm_matrices(
            lambda_re[l], lambda_im[l], log_dt[l], ssm_b_re[l], ssm_b_im[l], ssm_c_re[l], ssm_c_im[l])
        y_flat = _ssm(u_flat, toep, b_state, b_swap, c_pow, a_step, bsz)

        w_r = jnp.pad(jnp.concatenate([w_router_group[l], w_router_expert[l]], axis=1),
                      ((0, 0), (0, LANES - N_GROUPS - N_EXPERTS)))
        b_r = jnp.pad(jnp.concatenate([b_router_group[l], b_router_expert[l]]),
                      (0, LANES - N_GROUPS - N_EXPERTS)).reshape(1, LANES)
        x1, h2, logits = _mix(xc, o_fox, y_flat, u, sga, sgb, mod3, d_skip[l].reshape(1, SSM_WIDTH),
                              w_glu[l].astype(BF16), w_out_fox[l].astype(BF16),
                              w_out_ssm[l].astype(BF16), w_o[l].astype(BF16),
                              norm_ffn_g[l].reshape(1, d), w_r, b_r, seq)

        idx, wts, cnt = _route(logits)
        counts = cnt[0, :N_EXPERTS].astype(jnp.int32)
        pcounts = ((counts + ROW_BLOCK - 1) // ROW_BLOCK) * ROW_BLOCK
        pends = jnp.cumsum(pcounts)
        pstarts = pends - pcounts
        dest = pstarts[idx[:, 0:2]] + idx[:, 2:4]
        rows = 2 * n + N_EXPERTS * ROW_BLOCK
        n_blocks = rows // ROW_BLOCK
        blk_start = jnp.arange(n_blocks, dtype=jnp.int32) * ROW_BLOCK
        blk_e = jnp.minimum(jnp.sum((pends[None, :] <= blk_start[:, None]).astype(jnp.int32), axis=1),
                            N_EXPERTS - 1)
        n_valid = (pends[-1:] // ROW_BLOCK).astype(jnp.int32)
        dest3 = dest.astype(jnp.int32).reshape(n // MOVE_TILE, 1, 2 * MOVE_TILE)

        x_rows = _dispatch(dest3, h2, jnp.zeros((rows, d), F32))
        y_rows = _experts(blk_e, n_valid, x_rows, w_gate_e[l], w_up_e[l], w_down_e[l])
        xc = _combine(dest3, x1, wts, mod3, final_g.reshape(1, d), y_rows, seq)
    return xc.reshape(bsz, seq, d)
```

```python
import functools
import math

import jax
import jax.numpy as jnp
import numpy as np
from jax import lax
from jax.experimental import pallas as pl
from jax.experimental.pallas import tpu as pltpu

F32 = jnp.float32
BF16 = jnp.bfloat16

D_MODEL = 1024
N_MOD = 6
RMS_EPS = 1e-6
FOX_HEADS = 8
FOX_HEAD_DIM = 64
FOX_WIDTH = FOX_HEADS * FOX_HEAD_DIM
HEAD_PAIRS = FOX_HEADS // 2
SSM_WIDTH = 512
SSM_GROUP = 16
SSM_GROUPS = SSM_WIDTH // SSM_GROUP
SSM_STATE = 64
LAMBDA_RE_MAX = -1e-4
N_GROUPS = 4
EXPERTS_PER_GROUP = 8
N_EXPERTS = N_GROUPS * EXPERTS_PER_GROUP
D_EXPERT = 512

LANES = 128
SUBLANES = 8
VMEM_LIMIT = 56 * 1024 * 1024

SSM_CHUNK = 16
TOK_TILE = 512
MIX_TILE = 256
ATT_TILE = 256
ROW_BLOCK = 256
MOVE_TILE = 256
NEG_BIG = -1e30

HIGHEST = lax.Precision.HIGHEST


def _params(sem):
    return pltpu.CompilerParams(dimension_semantics=sem, vmem_limit_bytes=VMEM_LIMIT)


def _rms_modulate(x, gain, shift, scale):
    ms = jnp.mean(x * x, axis=-1, keepdims=True)
    return (x * lax.rsqrt(ms + RMS_EPS)) * gain * (1.0 + scale) + shift


def _mod_kernel(c_ref, w_ref, b_ref, o_ref):
    c = c_ref[...]
    ca = (c * jax.nn.sigmoid(c)).astype(BF16)
    o_ref[...] = jnp.dot(ca, w_ref[...].astype(BF16), preferred_element_type=F32) + b_ref[...]


def _mod(c, w_ada, b_ada):
    bsz, d = c.shape
    cols = w_ada.shape[1]
    tn = 1536
    return pl.pallas_call(
        _mod_kernel,
        grid=(cols // tn,),
        in_specs=[pl.BlockSpec((bsz, d), lambda j: (0, 0)),
                  pl.BlockSpec((d, tn), lambda j: (0, j)),
                  pl.BlockSpec((1, tn), lambda j: (0, j))],
        out_specs=pl.BlockSpec((bsz, tn), lambda j: (0, j)),
        out_shape=jax.ShapeDtypeStruct((bsz, cols), F32),
        compiler_params=_params(("arbitrary",)),
        name="mod",
    )(c, w_ada, b_ada.reshape(1, cols))


_C_Q, _C_K, _C_U, _C_GA, _C_GB, _C_F, _C_END = 0, 512, 1024, 1536, 2560, 3584, 3712


def _lane_block():
    return lax.broadcasted_iota(jnp.int32, (1, LANES), 1) // SSM_GROUP


def _to_group_major(tok_ref, flat_ref, rows):
    blk = _lane_block()
    for half in range(2):
        for j in range(SSM_WIDTH // LANES):
            w = []
            for s8 in range(8):
                v = tok_ref[j, pl.ds(8 * half + s8, rows, stride=SSM_CHUNK), :]
                w.append(pltpu.roll(v, s8 * SSM_GROUP, axis=1) if s8 else v)
            for p in range(8):
                acc = w[0]
                for s8 in range(1, 8):
                    acc = jnp.where(blk == (p + s8) % 8, w[s8], acc)
                flat_ref[8 * j + p, :, half * LANES:(half + 1) * LANES] = acc.astype(flat_ref.dtype)


def _to_token_major(flat_ref, tok_ref, rows):
    blk = _lane_block()
    for half in range(2):
        for j in range(SSM_WIDTH // LANES):
            ys = [flat_ref[8 * j + p, :, half * LANES:(half + 1) * LANES] for p in range(8)]
            for s8 in range(8):
                acc = ys[0]
                for p in range(1, 8):
                    acc = jnp.where(blk == (p + s8) % 8, ys[p], acc)
                if s8:
                    acc = pltpu.roll(acc, LANES - s8 * SSM_GROUP, axis=1)
                tok_ref[j, pl.ds(8 * half + s8, rows, stride=SSM_CHUNK), :] = acc


def _rotate_steps(m, axis):
    shape = m.shape
    m = m.reshape((SSM_GROUPS // 8, 8) + shape[1:])
    ax = axis + 1
    m = m.reshape(m.shape[:ax] + (2, 8) + m.shape[ax + 1:])
    m = jnp.stack([jnp.roll(m[:, p], p, axis=ax) for p in range(8)], axis=1)
    return m.reshape(shape)


def _inproj_kernel(tiles_per_batch, x_ref, mod_ref, g_ref, w_ref, wvt_ref, bf_ref, tri_ref,
                   q_ref, k_ref, vt_ref, u_ref, uflat_ref, ga_ref, gb_ref, cum_ref, carry_ref, uslab_ref):
    i = pl.program_id(0)
    h = _rms_modulate(x_ref[...], g_ref[...], mod_ref[0:1, :], mod_ref[1:2, :])
    hb = h.astype(BF16)

    def proj(a, b):
        return jnp.dot(hb, w_ref[:, a:b], preferred_element_type=F32)

    q_ref[...] = proj(_C_Q, _C_K).astype(BF16)
    k_ref[...] = proj(_C_K, _C_U).astype(BF16)
    vt_ref[...] = lax.dot_general(wvt_ref[...], hb, (((1,), (1,)), ((), ())),
                                  preferred_element_type=F32).astype(BF16)
    u = proj(_C_U, _C_GA)
    u_ref[...] = u
    for j in range(SSM_WIDTH // LANES):
        uslab_ref[j] = u[:, j * LANES:(j + 1) * LANES]
    _to_group_major(uslab_ref, uflat_ref, u.shape[0] // SSM_CHUNK)
    ga_ref[...] = jax.nn.sigmoid(proj(_C_GA, _C_GB)).astype(BF16)
    gb_ref[...] = jax.nn.sigmoid(proj(_C_GB, _C_F)).astype(BF16)

    f = proj(_C_F, _C_END) + bf_ref[...]
    logf = jnp.minimum(f, 0.0) - jnp.log(1.0 + jnp.exp(-jnp.abs(f)))

    @pl.when(i % tiles_per_batch == 0)
    def _():
        carry_ref[...] = jnp.zeros_like(carry_ref)

    cs = jnp.dot(tri_ref[...], logf, precision=HIGHEST, preferred_element_type=F32) + carry_ref[0:1, :]
    cum_ref[...] = cs
    carry_ref[...] = jnp.broadcast_to(cs[-1:, :], carry_ref.shape)


def _inproj(x2, mod3, gain, w_all, w_vt, bf_pad, seq):
    n, d = x2.shape
    tm = TOK_TILE
    tpb = seq // tm
    tri = jnp.tril(jnp.ones((tm, tm), F32))
    tok = lambda w: pl.BlockSpec((tm, w), lambda i: (i, 0))
    const = lambda shape: pl.BlockSpec(shape, lambda i: (0,) * len(shape))
    return pl.pallas_call(
        functools.partial(_inproj_kernel, tpb),
        grid=(n // tm,),
        in_specs=[tok(d),
                  pl.BlockSpec((None, N_MOD, d), lambda i: (i // tpb, 0, 0)),
                  const((1, d)), const((d, _C_END)), const((FOX_WIDTH, d)), const((1, LANES)),
                  const((tm, tm))],
        out_specs=[tok(FOX_WIDTH), tok(FOX_WIDTH), pl.BlockSpec((FOX_WIDTH, tm), lambda i: (0, i)),
                   tok(SSM_WIDTH),
                   pl.BlockSpec((SSM_GROUPS, tm // SSM_CHUNK, SSM_CHUNK * SSM_GROUP), lambda i: (0, i, 0)),
                   tok(d), tok(d), tok(LANES)],
        out_shape=[jax.ShapeDtypeStruct((n, FOX_WIDTH), BF16)] * 2
        + [jax.ShapeDtypeStruct((FOX_WIDTH, n), BF16)]
        + [jax.ShapeDtypeStruct((n, SSM_WIDTH), F32)]
        + [jax.ShapeDtypeStruct((SSM_GROUPS, n // SSM_CHUNK, SSM_CHUNK * SSM_GROUP), BF16)]
        + [jax.ShapeDtypeStruct((n, d), BF16)] * 2
        + [jax.ShapeDtypeStruct((n, LANES), F32)],
        scratch_shapes=[pltpu.VMEM((SUBLANES, LANES), F32),
                        pltpu.VMEM((SSM_WIDTH // LANES, tm, LANES), F32)],
        compiler_params=_params(("arbitrary",)),
        name="inproj",
    )(x2, mod3, gain, w_all, w_vt, bf_pad, tri)


def _attn_kernel(q_ref, aq_ref, k_ref, ak_ref, vt_ref, o_ref, m_ref, acc_ref):
    i = pl.program_id(2)
    t = ATT_TILE
    q = q_ref[...]
    aq = aq_ref[...]
    lane = lax.broadcasted_iota(jnp.int32, (1, LANES), 1)
    zq = jnp.zeros_like(q)
    half = FOX_HEAD_DIM
    q_both = jnp.concatenate([
        jnp.concatenate([jnp.where(lane < half, q, zq), jnp.where(lane < 8, aq, zq)], axis=1),
        jnp.concatenate([jnp.where(lane >= half, q, zq),
                         jnp.where((lane >= 8) & (lane < 16), aq, zq)], axis=1)], axis=0)
    m_ref[...] = jnp.full(m_ref.shape, NEG_BIG, F32)
    acc_ref[...] = jnp.zeros(acc_ref.shape, F32)
    ones_rows = jnp.ones((2 * SUBLANES, t), BF16)

    def scores(j):
        start = pl.multiple_of(j * t, t)
        kk = jnp.concatenate([k_ref[pl.ds(start, t), :], ak_ref[pl.ds(start, t), :]], axis=1)
        return lax.dot_general(kk, q_both, (((1,), (1,)), ((), ())), preferred_element_type=F32)

    def causal(s, j):
        key = j * t + lax.broadcasted_iota(jnp.int32, (t, 2 * t), 0)
        qry = i * t + (lax.broadcasted_iota(jnp.int32, (t, 2 * t), 1) & (t - 1))
        return jnp.where(key <= qry, s, NEG_BIG)

    def accumulate(j, s):
        start = pl.multiple_of(j * t, t)
        va = jnp.concatenate([vt_ref[:, pl.ds(start, t)], ones_rows], axis=0)
        m_old = m_ref[...]
        m_new = jnp.maximum(m_old, jnp.max(s, axis=0, keepdims=True))
        alpha = jnp.exp(m_old - m_new)
        p = jnp.exp(s - m_new).astype(BF16)
        acc_ref[...] = alpha * acc_ref[...] + jnp.dot(va, p, preferred_element_type=F32)
        m_ref[...] = m_new

    def body(j, s_cur):
        s_next = scores(j + 1)
        accumulate(j, s_cur)
        return s_next

    s_cur = lax.fori_loop(0, jnp.maximum(i - 1, 0), body, causal(scores(0), 0))

    @pl.when(i == 0)
    def _():
        accumulate(0, s_cur)

    @pl.when(i > 0)
    def _():
        s_diag = causal(scores(i), i)
        accumulate(i - 1, s_cur)
        accumulate(i, s_diag)

    acc = acc_ref[...]
    o_t = jnp.concatenate([acc[0:half, 0:t] / acc[LANES:LANES + 1, 0:t],
                           acc[half:LANES, t:2 * t] / acc[LANES:LANES + 1, t:2 * t]], axis=0)
    o_ref[...] = o_t.T.astype(o_ref.dtype)


def _attention(q, k, v_t, aug_q, aug_k, bsz, seq):
    n = q.shape[0]
    t = ATT_TILE
    nq = seq // t
    return pl.pallas_call(
        _attn_kernel,
        grid=(bsz, HEAD_PAIRS, nq),
        in_specs=[pl.BlockSpec((t, LANES), lambda b, p, i: (b * nq + i, p)),
                  pl.BlockSpec((None, None, t, LANES), lambda b, p, i: (b, p, i, 0)),
                  pl.BlockSpec((seq, LANES), lambda b, p, i: (b, p)),
                  pl.BlockSpec((None, None, seq, LANES), lambda b, p, i: (b, p, 0, 0)),
                  pl.BlockSpec((LANES, seq), lambda b, p, i: (p, b))],
        out_specs=pl.BlockSpec((t, LANES), lambda b, p, i: (b * nq + i, p)),
        out_shape=jax.ShapeDtypeStruct((n, FOX_WIDTH), BF16),
        scratch_shapes=[pltpu.VMEM((1, 2 * t), F32), pltpu.VMEM((LANES + 2 * SUBLANES, 2 * t), F32)],
        compiler_params=_params(("arbitrary", "arbitrary", "arbitrary")),
        name="attn",
    )(q, aug_q, k, aug_k, v_t)


def _forget_bias_lanes(cum, bsz, seq):
    def top_bits(a):
        bits = lax.bitcast_convert_type(a, jnp.uint32) & jnp.uint32(0xFFFF0000)
        return lax.bitcast_convert_type(bits, F32)

    hi32 = top_bits(cum)
    r1 = cum - hi32
    mid32 = top_bits(r1)
    hi, mid, lo = hi32.astype(BF16), mid32.astype(BF16), top_bits(r1 - mid32).astype(BF16)
    one = jnp.ones_like(hi)
    zero = jnp.zeros_like(hi)
    qa = jnp.stack([hi, mid, lo, one, one, one, zero, zero], axis=-1)
    ka = jnp.stack([one, one, one, -hi, -mid, -lo, zero, zero], axis=-1)

    def lay(a):
        a = a.reshape(bsz, seq, HEAD_PAIRS, 16).transpose(0, 2, 1, 3)
        return jnp.pad(a, ((0, 0), (0, 0), (0, 0), (0, LANES - 16)))

    return lay(qa), lay(ka)


def _ssm_matrices(lambda_re, lambda_im, log_dt, b_re, b_im, c_re, c_im):
    t_len = SSM_CHUNK
    lam_re = jnp.minimum(lambda_re.astype(F32), LAMBDA_RE_MAX)
    lam_im = lambda_im.astype(F32)
    dt = jnp.exp(log_dt.astype(F32))[:, None]
    mag = jnp.exp(lam_re * dt)
    a_re = mag * jnp.cos(lam_im * dt)
    a_im = mag * jnp.sin(lam_im * dt)
    den = lam_re * lam_re + lam_im * lam_im
    nr = a_re - 1.0
    co_re = (nr * lam_re + a_im * lam_im) / den
    co_im = (a_im * lam_re - nr * lam_im) / den
    b_re = b_re.astype(F32)
    b_im = b_im.astype(F32)
    bb_re = co_re[..., None] * b_re - co_im[..., None] * b_im
    bb_im = co_re[..., None] * b_im + co_im[..., None] * b_re
    ls = jnp.arange(t_len + 1, dtype=F32)[:, None, None]
    pmag = jnp.exp(ls * (lam_re * dt)[None])
    pang = ls * (lam_im * dt)[None]
    pw_re = pmag * jnp.cos(pang)
    pw_im = pmag * jnp.sin(pang)
    c_re = c_re.astype(F32)
    c_im = c_im.astype(F32)
    w_re = c_re[None] * pw_re[:, :, None, :] - c_im[None] * pw_im[:, :, None, :]
    w_im = c_re[None] * pw_im[:, :, None, :] + c_im[None] * pw_re[:, :, None, :]
    kern = (jnp.einsum('lgdp,gpc->gldc', w_re[:t_len], bb_re, precision=HIGHEST)
            - jnp.einsum('lgdp,gpc->gldc', w_im[:t_len], bb_im, precision=HIGHEST))
    toep = jnp.stack([jnp.pad(kern[:, :t_len - s], ((0, 0), (s, 0), (0, 0), (0, 0)))
                      for s in range(t_len)], axis=1)
    toep = _rotate_steps(_rotate_steps(toep, 1), 2)
    toep = toep.transpose(0, 1, 4, 2, 3).reshape(SSM_GROUPS, t_len * SSM_GROUP, t_len * SSM_GROUP)
    e_re = pw_re[:t_len][::-1]
    e_im = pw_im[:t_len][::-1]
    bs_re = e_re[..., None] * bb_re[None] - e_im[..., None] * bb_im[None]
    bs_im = e_re[..., None] * bb_im[None] + e_im[..., None] * bb_re[None]

    def lay_state(first, second):
        m = jnp.concatenate([first, second], axis=2).transpose(1, 0, 3, 2)
        return _rotate_steps(m, 1).reshape(SSM_GROUPS, t_len * SSM_GROUP, 2 * SSM_STATE)

    b_state = lay_state(bs_re, bs_im)
    b_swap = lay_state(bs_im, bs_re)
    cp = jnp.concatenate([w_re[1:], -w_im[1:]], axis=3).transpose(1, 3, 0, 2)
    c_pow = _rotate_steps(cp, 2).reshape(SSM_GROUPS, 2 * SSM_STATE, t_len * SSM_GROUP)
    a1 = jnp.concatenate([pw_re[t_len], pw_re[t_len]], axis=-1)
    a2 = jnp.concatenate([-pw_im[t_len], pw_im[t_len]], axis=-1)
    a_step = jnp.stack([a1, a2], axis=1)
    return toep.astype(BF16), b_state.astype(BF16), b_swap.astype(BF16), c_pow.astype(BF16), a_step


def _ssm_kernel(n_chunks, bsz, u_ref, toep_ref, bst_ref, bsw_ref, cpw_ref, a_ref, y_ref,
                contrib_ref, cswap_ref, xprev_ref):
    u = u_ref[...]
    contrib_ref[...] = jnp.dot(u, bst_ref[...], preferred_element_type=F32)
    cswap_ref[...] = jnp.dot(u, bsw_ref[...], preferred_element_type=F32)
    a1 = a_ref[0:1, :]
    a2 = a_ref[1:2, :]

    def step(n, carry):
        x, xs = carry
        rows = pl.ds(n, bsz, stride=n_chunks)
        xprev_ref[rows, :] = x
        x_new = a1 * x + a2 * xs + contrib_ref[rows, :]
        xs_new = a1 * xs - a2 * x + cswap_ref[rows, :]
        return x_new, xs_new

    zero = jnp.zeros((bsz, 2 * SSM_STATE), F32)
    lax.fori_loop(0, n_chunks, step, (zero, zero), unroll=4)
    y_ref[...] = (jnp.dot(u, toep_ref[...], preferred_element_type=F32)
                  + jnp.dot(xprev_ref[...].astype(BF16), cpw_ref[...], preferred_element_type=F32))


def _ssm(u_flat, toep, b_state, b_swap, c_pow, a_step, bsz):
    g, rows, w = u_flat.shape
    per = lambda a, b: pl.BlockSpec((None, a, b), lambda i: (i, 0, 0))
    state = pltpu.VMEM((rows, 2 * SSM_STATE), F32)
    return pl.pallas_call(
        functools.partial(_ssm_kernel, rows // bsz, bsz),
        grid=(g,),
        in_specs=[per(rows, w), per(w, w), per(w, 2 * SSM_STATE), per(w, 2 * SSM_STATE),
                  per(2 * SSM_STATE, w), per(2, 2 * SSM_STATE)],
        out_specs=per(rows, w),
        out_shape=jax.ShapeDtypeStruct((g, rows, w), F32),
        scratch_shapes=[state, state, state],
        compiler_params=_params(("arbitrary",)),
        name="ssm",
    )(u_flat, toep, b_state, b_swap, c_pow, a_step)


def _mix_kernel(x_ref, of_ref, yf_ref, u_ref, ga_ref, gb_ref, mod_ref, dsk_ref, wglu_ref, wfox_ref,
                wssm_ref, wo_ref, g2_ref, wr_ref, br_ref, x1_ref, h2_ref, lg_ref, ytok_ref):
    _to_token_major(yf_ref, ytok_ref, yf_ref.shape[1])
    y_ssm = jnp.concatenate([ytok_ref[j] for j in range(SSM_WIDTH // LANES)], axis=1)
    y = y_ssm + dsk_ref[...] * u_ref[...]
    y = 0.5 * y * (1.0 + jnp.tanh(math.sqrt(2.0 / math.pi) * (y + 0.044715 * (y * y * y))))
    gl = jnp.dot(y.astype(BF16), wglu_ref[...], preferred_element_type=F32)
    o_ssm = gl[:, :SSM_WIDTH] * jax.nn.sigmoid(gl[:, SSM_WIDTH:])
    merged = (ga_ref[...].astype(F32) * jnp.dot(of_ref[...], wfox_ref[...], preferred_element_type=F32)
              + gb_ref[...].astype(F32) * jnp.dot(o_ssm.astype(BF16), wssm_ref[...],
                                                  preferred_element_type=F32))
    x1 = x_ref[...] + mod_ref[2:3, :] * jnp.dot(merged.astype(BF16), wo_ref[...],
                                                 preferred_element_type=F32)
    x1_ref[...] = x1
    h2 = _rms_modulate(x1, g2_ref[...], mod_ref[3:4, :], mod_ref[4:5, :])
    h2_ref[...] = h2
    lg_ref[...] = jnp.dot(h2, wr_ref[...], precision=HIGHEST, preferred_element_type=F32) + br_ref[...]


def _mix(x2, o_fox, y_flat, u, sga, sgb, mod3, d_skip, w_glu, w_fox, w_ssm, w_o, g2, w_r, b_r, seq):
    n, d = x2.shape
    tm = MIX_TILE
    tpb = seq // tm
    tok = lambda w: pl.BlockSpec((tm, w), lambda i: (i, 0))
    const = lambda a: pl.BlockSpec(a.shape, lambda i: (0,) * a.ndim)
    flat = pl.BlockSpec((SSM_GROUPS, tm // SSM_CHUNK, SSM_CHUNK * SSM_GROUP), lambda i: (0, i, 0))
    return pl.pallas_call(
        _mix_kernel,
        grid=(n // tm,),
        in_specs=[tok(d), tok(FOX_WIDTH), flat, tok(SSM_WIDTH), tok(d), tok(d),
                  pl.BlockSpec((None, N_MOD, d), lambda i: (i // tpb, 0, 0)),
                  const(d_skip), const(w_glu), const(w_fox), const(w_ssm), const(w_o), const(g2),
                  const(w_r), const(b_r)],
        out_specs=[tok(d), tok(d), tok(LANES)],
        out_shape=[jax.ShapeDtypeStruct((n, d), F32), jax.ShapeDtypeStruct((n, d), F32),
                   jax.ShapeDtypeStruct((n, LANES), F32)],
        scratch_shapes=[pltpu.VMEM((SSM_WIDTH // LANES, tm, LANES), F32)],
        compiler_params=_params(("arbitrary",)),
        name="mix",
    )(x2, o_fox, y_flat, u, sga, sgb, mod3, d_skip, w_glu, w_fox, w_ssm, w_o, g2, w_r, b_r)


def _route_kernel(lg_ref, tri_ref, idx_ref, wt_ref, cnt_ref, carry_ref):
    i = pl.program_id(0)

    @pl.when(i == 0)
    def _():
        carry_ref[...] = jnp.zeros_like(carry_ref)

    lg = lg_ref[...]
    tm = lg.shape[0]
    lane = lax.broadcasted_iota(jnp.int32, (tm, LANES), 1)
    neg = jnp.full_like(lg, -jnp.inf)

    def first_argmax(vals):
        mx = jnp.max(vals, axis=1, keepdims=True)
        ix = jnp.min(jnp.where(vals == mx, lane, LANES), axis=1, keepdims=True)
        return mx, ix

    is_group = lane < N_GROUPS
    g_max, gi = first_argmax(jnp.where(is_group, lg, neg))
    g_sum = jnp.sum(jnp.where(is_group, jnp.exp(lg - g_max), 0.0), axis=1, keepdims=True)
    p_group = 1.0 / g_sum
    lo = N_GROUPS + EXPERTS_PER_GROUP * gi
    in_group = (lane >= lo) & (lane < lo + EXPERTS_PER_GROUP)
    cand = jnp.where(in_group, lg, neg)
    v1, i1 = first_argmax(cand)
    v2, i2 = first_argmax(jnp.where(lane == i1, neg, cand))
    tt = jnp.exp(v2 - v1)
    w1 = p_group / (1.0 + tt)
    w2 = p_group * tt / (1.0 + tt)
    e1 = i1 - N_GROUPS
    e2 = i2 - N_GROUPS
    sel1 = lane == e1
    sel2 = lane == e2
    onehot = (sel1 | sel2).astype(F32)
    before = jnp.dot(tri_ref[...], onehot.astype(BF16), preferred_element_type=F32) + carry_ref[0:1, :]
    r1 = jnp.sum(jnp.where(sel1, before, 0.0), axis=1, keepdims=True).astype(jnp.int32)
    r2 = jnp.sum(jnp.where(sel2, before, 0.0), axis=1, keepdims=True).astype(jnp.int32)
    total = before[-1:, :] + onehot[-1:, :]
    carry_ref[...] = jnp.broadcast_to(total, carry_ref.shape)
    cnt_ref[...] = jnp.broadcast_to(total, cnt_ref.shape)
    idx_ref[...] = jnp.where(lane == 0, e1, jnp.where(lane == 1, e2, jnp.where(lane == 2, r1, r2)))
    wt_ref[...] = jnp.where(lane == 0, w1, w2)


def _route(logits):
    n = logits.shape[0]
    tm = TOK_TILE
    tri = jnp.tril(jnp.ones((tm, tm), BF16), k=-1)
    tok = pl.BlockSpec((tm, LANES), lambda i: (i, 0))
    return pl.pallas_call(
        _route_kernel,
        grid=(n // tm,),
        in_specs=[tok, pl.BlockSpec((tm, tm), lambda i: (0, 0))],
        out_specs=[tok, tok, pl.BlockSpec((SUBLANES, LANES), lambda i: (0, 0))],
        out_shape=[jax.ShapeDtypeStruct((n, LANES), jnp.int32), jax.ShapeDtypeStruct((n, LANES), F32),
                   jax.ShapeDtypeStruct((SUBLANES, LANES), F32)],
        scratch_shapes=[pltpu.VMEM((SUBLANES, LANES), F32)],
        compiler_params=_params(("arbitrary",)),
        name="route",
    )(logits, tri)


def _row_copy(src_ref, src_row, dst_ref, dst_row, sem):
    return pltpu.make_async_copy(src_ref.at[pl.ds(src_row, 1), :], dst_ref.at[pl.ds(dst_row, 1), :], sem)


ISSUE_UNROLL = 8


def _dispatch_kernel(dest_ref, h_ref, rows_in_ref, rows_ref, sem):
    del rows_in_ref
    tm = h_ref.shape[0]

    def issue(g, c):
        for j in range(ISSUE_UNROLL):
            t = g * ISSUE_UNROLL + j
            _row_copy(h_ref, t, rows_ref, dest_ref[0, 0, t], sem).start(priority=0)
            _row_copy(h_ref, t, rows_ref, dest_ref[0, 0, tm + t], sem).start(priority=1)
        return c

    lax.fori_loop(0, tm // ISSUE_UNROLL, issue, 0)
    for _ in range(2):
        pltpu.make_async_copy(h_ref, rows_ref.at[pl.ds(0, tm), :], sem).wait()


def _dispatch(dest3, h2, rows_zero):
    n, d = h2.shape
    tm = MOVE_TILE
    return pl.pallas_call(
        _dispatch_kernel,
        grid=(n // tm,),
        in_specs=[pl.BlockSpec((1, 1, 2 * tm), lambda i: (i, 0, 0), memory_space=pltpu.SMEM),
                  pl.BlockSpec((tm, d), lambda i: (i, 0)),
                  pl.BlockSpec(memory_space=pl.ANY)],
        out_specs=pl.BlockSpec(memory_space=pl.ANY),
        out_shape=jax.ShapeDtypeStruct(rows_zero.shape, rows_zero.dtype),
        scratch_shapes=[pltpu.SemaphoreType.DMA(())],
        input_output_aliases={2: 0},
        compiler_params=_params(("arbitrary",)),
        name="dispatch",
    )(dest3, h2, rows_zero)


def _combine_kernel(dest_ref, x1_ref, wt_ref, mod_ref, gf_ref, yr_ref, o_ref, buf_ref, sem):
    tm = x1_ref.shape[0]

    def issue(g, c):
        for j in range(ISSUE_UNROLL):
            t = g * ISSUE_UNROLL + j
            _row_copy(yr_ref, dest_ref[0, 0, t], buf_ref.at[0], t, sem).start(priority=0)
            _row_copy(yr_ref, dest_ref[0, 0, tm + t], buf_ref.at[1], t, sem).start(priority=1)
        return c

    lax.fori_loop(0, tm // ISSUE_UNROLL, issue, 0)
    for slot in range(2):
        pltpu.make_async_copy(yr_ref.at[pl.ds(0, tm), :], buf_ref.at[slot], sem).wait()
    wt = wt_ref[...]
    moe = wt[:, 0:1] * buf_ref[0] + wt[:, 1:2] * buf_ref[1]
    x = x1_ref[...] + mod_ref[5:6, :] * moe
    ms = jnp.mean(x * x, axis=-1, keepdims=True)
    o_ref[...] = (x * lax.rsqrt(ms + RMS_EPS)) * gf_ref[...]


def _combine(dest3, x1, wts, mod3, final_g, y_rows, seq):
    n, d = x1.shape
    tm = MOVE_TILE
    tpb = seq // tm
    return pl.pallas_call(
        _combine_kernel,
        grid=(n // tm,),
        in_specs=[pl.BlockSpec((1, 1, 2 * tm), lambda i: (i, 0, 0), memory_space=pltpu.SMEM),
                  pl.BlockSpec((tm, d), lambda i: (i, 0)),
                  pl.BlockSpec((tm, LANES), lambda i: (i, 0)),
                  pl.BlockSpec((None, N_MOD, d), lambda i: (i // tpb, 0, 0)),
                  pl.BlockSpec((1, d), lambda i: (0, 0)),
                  pl.BlockSpec(memory_space=pl.ANY)],
        out_specs=pl.BlockSpec((tm, d), lambda i: (i, 0)),
        out_shape=jax.ShapeDtypeStruct((n, d), F32),
        scratch_shapes=[pltpu.VMEM((2, tm, d), F32), pltpu.SemaphoreType.DMA(())],
        compiler_params=_params(("arbitrary",)),
        name="combine",
    )(dest3, x1, wts, mod3, final_g, y_rows)


def _expert_kernel(be_ref, nv_ref, x_ref, wg_ref, wu_ref, wd_ref, y_ref, wgb_ref, wub_ref, wdb_ref):
    i = pl.program_id(0)
    valid = i < nv_ref[0]
    new_expert = (i == 0) | (be_ref[i] != be_ref[jnp.maximum(i - 1, 0)])

    @pl.when(valid & new_expert)
    def _():
        wgb_ref[...] = wg_ref[...].astype(BF16)
        wub_ref[...] = wu_ref[...].astype(BF16)
        wdb_ref[...] = wd_ref[...].astype(BF16)

    @pl.when(valid)
    def _():
        xb = x_ref[...].astype(BF16)
        a = jnp.dot(xb, wgb_ref[...], preferred_element_type=F32)
        b = jnp.dot(xb, wub_ref[...], preferred_element_type=F32)
        hid = (a * jax.nn.sigmoid(a)) * b
        y_ref[...] = jnp.dot(hid.astype(BF16), wdb_ref[...], preferred_element_type=F32)

    @pl.when(jnp.logical_not(valid))
    def _():
        y_ref[...] = jnp.zeros_like(y_ref)


def _experts(blk_e, n_valid, x_rows, w_gate, w_up, w_down):
    rows, d = x_rows.shape
    tb = ROW_BLOCK
    grid_spec = pltpu.PrefetchScalarGridSpec(
        num_scalar_prefetch=2,
        grid=(rows // tb,),
        in_specs=[pl.BlockSpec((tb, d), lambda i, be, nv: (i, 0)),
                  pl.BlockSpec((None, d, D_EXPERT), lambda i, be, nv: (be[i], 0, 0)),
                  pl.BlockSpec((None, d, D_EXPERT), lambda i, be, nv: (be[i], 0, 0)),
                  pl.BlockSpec((None, D_EXPERT, d), lambda i, be, nv: (be[i], 0, 0))],
        out_specs=pl.BlockSpec((tb, d), lambda i, be, nv: (i, 0)),
        scratch_shapes=[pltpu.VMEM((d, D_EXPERT), BF16), pltpu.VMEM((d, D_EXPERT), BF16),
                        pltpu.VMEM((D_EXPERT, d), BF16)],
    )
    return pl.pallas_call(
        _expert_kernel,
        grid_spec=grid_spec,
        out_shape=jax.ShapeDtypeStruct((rows, d), F32),
        compiler_params=_params(("arbitrary",)),
        name="experts",
    )(blk_e, n_valid, x_rows, w_gate, w_up, w_down)


def kernel(x, c, w_ada, b_ada, norm_mix_g, w_in, b_forget, w_out_fox, lambda_re, lambda_im, log_dt,
           ssm_b_re, ssm_b_im, ssm_c_re, ssm_c_im, d_skip, w_glu, w_out_ssm, w_o, norm_ffn_g,
           w_router_group, b_router_group, w_router_expert, b_router_expert, w_gate_e, w_up_e,
           w_down_e, final_g):
    bsz, seq, d = x.shape
    n = bsz * seq
    assert w_ada.shape[0] == 1, "the final RMSNorm is fused into the (single) layer's combine kernel"
    xc = x.reshape(n, d)
    for l in range(1):
        mod3 = _mod(c, w_ada[l], b_ada[l]).reshape(bsz, N_MOD, d)

        wi = w_in[l]
        s_q, s_k, s_v, s_f, s_u, s_ga = 512, 1024, 1536, 1544, 2056, 3080
        scale = FOX_HEAD_DIM ** -0.5
        w_all = jnp.concatenate(
            [wi[:, :s_q] * scale, wi[:, s_q:s_k], wi[:, s_f:s_u], wi[:, s_u:s_ga],
             wi[:, s_ga:], jnp.pad(wi[:, s_v:s_f], ((0, 0), (0, LANES - FOX_HEADS)))],
            axis=1).astype(BF16)
        w_vt = wi[:, s_k:s_v].T.astype(BF16)
        bf_pad = jnp.pad(b_forget[l], (0, LANES - FOX_HEADS)).reshape(1, LANES)
        q, k, v_t, u, u_flat, sga, sgb, cum_pad = _inproj(xc, mod3, norm_mix_g[l].reshape(1, d), w_all,
                                                          w_vt, bf_pad, seq)

        aug_q, aug_k = _forget_bias_lanes(cum_pad[:, :FOX_HEADS].reshape(bsz, seq, FOX_HEADS), bsz, seq)
        o_fox = _attention(q, k, v_t, aug_q, aug_k, bsz, seq)

        toep, b_state, b_swap, c_pow, a_step = _ssm_matrices(
            lambda_re[l], lambda_im[l], log_dt[l], ssm_b_re[l], ssm_b_im[l], ssm_c_re[l], ssm_c_im[l])
        y_flat = _ssm(u_flat, toep, b_state, b_swap, c_pow, a_step, bsz)

        w_r = jnp.pad(jnp.concatenate([w_router_group[l], w_router_expert[l]], axis=1),
                      ((0, 0), (0, LANES - N_GROUPS - N_EXPERTS)))
        b_r = jnp.pad(jnp.concatenate([b_router_group[l], b_router_expert[l]]),
                      (0, LANES - N_GROUPS - N_EXPERTS)).reshape(1, LANES)
        x1, h2, logits = _mix(xc, o_fox, y_flat, u, sga, sgb, mod3, d_skip[l].reshape(1, SSM_WIDTH),
                              w_glu[l].astype(BF16), w_out_fox[l].astype(BF16),
                              w_out_ssm[l].astype(BF16), w_o[l].astype(BF16),
                              norm_ffn_g[l].reshape(1, d), w_r, b_r, seq)

        idx, wts, cnt = _route(logits)
        counts = cnt[0, :N_EXPERTS].astype(jnp.int32)
        pcounts = ((counts + ROW_BLOCK - 1) // ROW_BLOCK) * ROW_BLOCK
        pends = jnp.cumsum(pcounts)
        pstarts = pends - pcounts
        er = idx[:, 0:4].T
        hit = er[0:2, None, :] == jnp.arange(N_EXPERTS, dtype=jnp.int32)[None, :, None]
        dest = jnp.sum(jnp.where(hit, pstarts[None, :, None], 0), axis=1) + er[2:4]
        rows = 2 * n + N_EXPERTS * ROW_BLOCK
        n_blocks = rows // ROW_BLOCK
        blk_start = jnp.arange(n_blocks, dtype=jnp.int32) * ROW_BLOCK
        blk_e = jnp.minimum(jnp.sum((pends[None, :] <= blk_start[:, None]).astype(jnp.int32), axis=1),
                            N_EXPERTS - 1)
        n_valid = (pends[-1:] // ROW_BLOCK).astype(jnp.int32)
        dest3 = (dest.astype(jnp.int32).reshape(2, n // MOVE_TILE, MOVE_TILE).transpose(1, 0, 2)
                 .reshape(n // MOVE_TILE, 1, 2 * MOVE_TILE))

        x_rows = _dispatch(dest3, h2, jnp.zeros((rows, d), F32))
        y_rows = _experts(blk_e, n_valid, x_rows, w_gate_e[l], w_up_e[l], w_down_e[l])
        xc = _combine(dest3, x1, wts, mod3, final_g.reshape(1, d), y_rows, seq)
    return xc.reshape(bsz, seq, d)
```

```python
import functools
import math

import jax
import jax.numpy as jnp
import numpy as np
from jax import lax
from jax.experimental import pallas as pl
from jax.experimental.pallas import tpu as pltpu

F32 = jnp.float32
BF16 = jnp.bfloat16

D_MODEL = 1024
N_MOD = 6
RMS_EPS = 1e-6
FOX_HEADS = 8
FOX_HEAD_DIM = 64
FOX_WIDTH = FOX_HEADS * FOX_HEAD_DIM
HEAD_PAIRS = FOX_HEADS // 2
SSM_WIDTH = 512
SSM_GROUP = 16
SSM_GROUPS = SSM_WIDTH // SSM_GROUP
SSM_STATE = 64
LAMBDA_RE_MAX = -1e-4
N_GROUPS = 4
EXPERTS_PER_GROUP = 8
N_EXPERTS = N_GROUPS * EXPERTS_PER_GROUP
D_EXPERT = 512

LANES = 128
SUBLANES = 8
VMEM_LIMIT = 56 * 1024 * 1024

SSM_CHUNK = 16
TOK_TILE = 512
MIX_TILE = 256
ATT_TILE = 256
ROW_BLOCK = 256
MOVE_TILE = 256
NEG_BIG = -1e30

HIGHEST = lax.Precision.HIGHEST


def _params(sem):
    return pltpu.CompilerParams(dimension_semantics=sem, vmem_limit_bytes=VMEM_LIMIT)


def _rms_modulate(x, gain, shift, scale):
    ms = jnp.mean(x * x, axis=-1, keepdims=True)
    return (x * lax.rsqrt(ms + RMS_EPS)) * gain * (1.0 + scale) + shift


def _mod_kernel(c_ref, w_ref, b_ref, o_ref):
    c = c_ref[...]
    ca = (c * jax.nn.sigmoid(c)).astype(BF16)
    o_ref[...] = jnp.dot(ca, w_ref[...].astype(BF16), preferred_element_type=F32) + b_ref[...]


def _mod(c, w_ada, b_ada):
    bsz, d = c.shape
    cols = w_ada.shape[1]
    tn = 1536
    return pl.pallas_call(
        _mod_kernel,
        grid=(cols // tn,),
        in_specs=[pl.BlockSpec((bsz, d), lambda j: (0, 0)),
                  pl.BlockSpec((d, tn), lambda j: (0, j)),
                  pl.BlockSpec((1, tn), lambda j: (0, j))],
        out_specs=pl.BlockSpec((bsz, tn), lambda j: (0, j)),
        out_shape=jax.ShapeDtypeStruct((bsz, cols), F32),
        compiler_params=_params(("arbitrary",)),
        name="mod",
    )(c, w_ada, b_ada.reshape(1, cols))


_C_Q, _C_K, _C_U, _C_GA, _C_GB, _C_F, _C_END = 0, 512, 1024, 1536, 2560, 3584, 3712


def _lane_block():
    return lax.broadcasted_iota(jnp.int32, (1, LANES), 1) // SSM_GROUP


def _to_group_major(tok_ref, flat_ref, rows):
    blk = _lane_block()
    for half in range(2):
        for j in range(SSM_WIDTH // LANES):
            w = []
            for s8 in range(8):
                v = tok_ref[j, pl.ds(8 * half + s8, rows, stride=SSM_CHUNK), :]
                w.append(pltpu.roll(v, s8 * SSM_GROUP, axis=1) if s8 else v)
            for p in range(8):
                acc = w[0]
                for s8 in range(1, 8):
                    acc = jnp.where(blk == (p + s8) % 8, w[s8], acc)
                flat_ref[8 * j + p, :, half * LANES:(half + 1) * LANES] = acc.astype(flat_ref.dtype)


def _to_token_major(flat_ref, tok_ref, rows):
    blk = _lane_block()
    for half in range(2):
        for j in range(SSM_WIDTH // LANES):
            ys = [flat_ref[8 * j + p, :, half * LANES:(half + 1) * LANES] for p in range(8)]
            for s8 in range(8):
                acc = ys[0]
                for p in range(1, 8):
                    acc = jnp.where(blk == (p + s8) % 8, ys[p], acc)
                if s8:
                    acc = pltpu.roll(acc, LANES - s8 * SSM_GROUP, axis=1)
                tok_ref[j, pl.ds(8 * half + s8, rows, stride=SSM_CHUNK), :] = acc


def _rotate_steps(m, axis):
    shape = m.shape
    m = m.reshape((SSM_GROUPS // 8, 8) + shape[1:])
    ax = axis + 1
    m = m.reshape(m.shape[:ax] + (2, 8) + m.shape[ax + 1:])
    m = jnp.stack([jnp.roll(m[:, p], p, axis=ax) for p in range(8)], axis=1)
    return m.reshape(shape)


def _roll_lane_steps(m):
    shape = m.shape
    m = m.reshape((SSM_GROUPS // 8, 8) + shape[1:-1] + (2, LANES))
    m = jnp.stack([jnp.roll(m[:, p], p * SSM_GROUP, axis=-1) for p in range(8)], axis=1)
    return m.reshape(shape)


def _bias_lane_placement():
    pq = np.zeros((3 * LANES, HEAD_PAIRS * LANES), np.float32)
    pk = np.zeros((3 * LANES, HEAD_PAIRS * LANES), np.float32)
    bq = np.zeros((1, HEAD_PAIRS * LANES), np.float32)
    bk = np.zeros((1, HEAD_PAIRS * LANES), np.float32)
    for head in range(FOX_HEADS):
        base = (head // 2) * LANES + (head % 2) * 8
        for term in range(3):
            pq[term * LANES + head, base + term] = 1.0
            pk[term * LANES + head, base + 3 + term] = -1.0
            bq[0, base + 3 + term] = 1.0
            bk[0, base + term] = 1.0
    return pq, pk, bq, bk


def _top_bits(a):
    bits = lax.bitcast_convert_type(a, jnp.uint32) & jnp.uint32(0xFFFF0000)
    return lax.bitcast_convert_type(bits, F32)


def _inproj_kernel(tiles_per_batch, x_ref, mod_ref, g_ref, w_ref, wvt_ref, bf_ref, tri_ref,
                   pq_ref, pk_ref, bq_ref, bk_ref,
                   q_ref, k_ref, vt_ref, u_ref, uflat_ref, ga_ref, gb_ref, carry_ref, uslab_ref):
    i = pl.program_id(0)
    h = _rms_modulate(x_ref[...], g_ref[...], mod_ref[0:1, :], mod_ref[1:2, :])
    hb = h.astype(BF16)

    def proj(a, b):
        return jnp.dot(hb, w_ref[:, a:b], preferred_element_type=F32)

    q = proj(_C_Q, _C_K).astype(BF16)
    k = proj(_C_K, _C_U).astype(BF16)
    vt_ref[...] = lax.dot_general(wvt_ref[...], hb, (((1,), (1,)), ((), ())),
                                  preferred_element_type=F32).astype(BF16)
    u = proj(_C_U, _C_GA)
    u_ref[...] = u
    for j in range(SSM_WIDTH // LANES):
        uslab_ref[j] = u[:, j * LANES:(j + 1) * LANES]
    _to_group_major(uslab_ref, uflat_ref, u.shape[0] // SSM_CHUNK)
    ga_ref[...] = jax.nn.sigmoid(proj(_C_GA, _C_GB)).astype(BF16)
    gb_ref[...] = jax.nn.sigmoid(proj(_C_GB, _C_F)).astype(BF16)

    f = proj(_C_F, _C_END) + bf_ref[...]
    logf = jnp.minimum(f, 0.0) - jnp.log(1.0 + jnp.exp(-jnp.abs(f)))

    @pl.when(i % tiles_per_batch == 0)
    def _():
        carry_ref[...] = jnp.zeros_like(carry_ref)

    cs = jnp.dot(tri_ref[...], logf, precision=HIGHEST, preferred_element_type=F32) + carry_ref[0:1, :]
    carry_ref[...] = jnp.broadcast_to(cs[-1:, :], carry_ref.shape)

    hi = _top_bits(cs)
    r1 = cs - hi
    mid = _top_bits(r1)
    lo = _top_bits(r1 - mid)
    terms = jnp.concatenate([hi, mid, lo], axis=1).astype(BF16)
    bias_q = (jnp.dot(terms, pq_ref[...], preferred_element_type=F32) + bq_ref[...]).astype(BF16)
    bias_k = (jnp.dot(terms, pk_ref[...], preferred_element_type=F32) + bk_ref[...]).astype(BF16)
    for p in range(HEAD_PAIRS):
        lanes = slice(p * LANES, (p + 1) * LANES)
        q_ref[:, 2 * p * LANES:(2 * p + 1) * LANES] = q[:, lanes]
        q_ref[:, (2 * p + 1) * LANES:(2 * p + 2) * LANES] = bias_q[:, lanes]
        k_ref[:, 2 * p * LANES:(2 * p + 1) * LANES] = k[:, lanes]
        k_ref[:, (2 * p + 1) * LANES:(2 * p + 2) * LANES] = bias_k[:, lanes]


def _inproj(x2, mod3, gain, w_all, w_vt, bf_pad, seq):
    n, d = x2.shape
    tm = TOK_TILE
    tpb = seq // tm
    tri = jnp.tril(jnp.ones((tm, tm), F32))
    pq, pk, bq, bk = _bias_lane_placement()
    tok = lambda w: pl.BlockSpec((tm, w), lambda i: (i, 0))
    const = lambda shape: pl.BlockSpec(shape, lambda i: (0,) * len(shape))
    qk_width = 2 * FOX_WIDTH
    return pl.pallas_call(
        functools.partial(_inproj_kernel, tpb),
        grid=(n // tm,),
        in_specs=[tok(d),
                  pl.BlockSpec((None, N_MOD, d), lambda i: (i // tpb, 0, 0)),
                  const((1, d)), const((d, _C_END)), const((FOX_WIDTH, d)), const((1, LANES)),
                  const((tm, tm)), const(pq.shape), const(pk.shape), const(bq.shape), const(bk.shape)],
        out_specs=[tok(qk_width), tok(qk_width), pl.BlockSpec((FOX_WIDTH, tm), lambda i: (0, i)),
                   tok(SSM_WIDTH),
                   pl.BlockSpec((SSM_GROUPS, tm // SSM_CHUNK, SSM_CHUNK * SSM_GROUP), lambda i: (0, i, 0)),
                   tok(d), tok(d)],
        out_shape=[jax.ShapeDtypeStruct((n, qk_width), BF16)] * 2
        + [jax.ShapeDtypeStruct((FOX_WIDTH, n), BF16)]
        + [jax.ShapeDtypeStruct((n, SSM_WIDTH), F32)]
        + [jax.ShapeDtypeStruct((SSM_GROUPS, n // SSM_CHUNK, SSM_CHUNK * SSM_GROUP), BF16)]
        + [jax.ShapeDtypeStruct((n, d), BF16)] * 2,
        scratch_shapes=[pltpu.VMEM((SUBLANES, LANES), F32),
                        pltpu.VMEM((SSM_WIDTH // LANES, tm, LANES), F32)],
        compiler_params=_params(("arbitrary",)),
        name="inproj",
    )(x2, mod3, gain, w_all, w_vt, bf_pad, tri, jnp.asarray(pq, BF16), jnp.asarray(pk, BF16),
      jnp.asarray(bq), jnp.asarray(bk))


def _attn_kernel(q_ref, k_ref, vt_ref, o_ref, m_ref, acc_ref):
    i = pl.program_id(2)
    t = ATT_TILE
    q = q_ref[...]
    lane = lax.broadcasted_iota(jnp.int32, (1, 2 * LANES), 1)
    zq = jnp.zeros_like(q)
    half = FOX_HEAD_DIM
    own0 = (lane < half) | ((lane >= LANES) & (lane < LANES + 8))
    own1 = ((lane >= half) & (lane < LANES)) | ((lane >= LANES + 8) & (lane < LANES + 16))
    q_both = jnp.concatenate([jnp.where(own0, q, zq), jnp.where(own1, q, zq)], axis=0)
    m_ref[...] = jnp.full(m_ref.shape, NEG_BIG, F32)
    acc_ref[...] = jnp.zeros(acc_ref.shape, F32)
    ones_rows = jnp.ones((2 * SUBLANES, t), BF16)

    def scores(j):
        start = pl.multiple_of(j * t, t)
        return lax.dot_general(k_ref[pl.ds(start, t), :], q_both, (((1,), (1,)), ((), ())),
                               preferred_element_type=F32)

    def causal(s, j):
        key = j * t + lax.broadcasted_iota(jnp.int32, (t, 2 * t), 0)
        qry = i * t + (lax.broadcasted_iota(jnp.int32, (t, 2 * t), 1) & (t - 1))
        return jnp.where(key <= qry, s, NEG_BIG)

    def accumulate(j, s):
        start = pl.multiple_of(j * t, t)
        va = jnp.concatenate([vt_ref[:, pl.ds(start, t)], ones_rows], axis=0)
        m_old = m_ref[...]
        m_new = jnp.maximum(m_old, jnp.max(s, axis=0, keepdims=True))
        alpha = jnp.exp(m_old - m_new)
        p = jnp.exp(s - m_new).astype(BF16)
        acc_ref[...] = alpha * acc_ref[...] + jnp.dot(va, p, preferred_element_type=F32)
        m_ref[...] = m_new

    def body(j, s_cur):
        s_next = scores(j + 1)
        accumulate(j, s_cur)
        return s_next

    s_cur = lax.fori_loop(0, jnp.maximum(i - 1, 0), body, causal(scores(0), 0))

    @pl.when(i == 0)
    def _():
        accumulate(0, s_cur)

    @pl.when(i > 0)
    def _():
        s_diag = causal(scores(i), i)
        accumulate(i - 1, s_cur)
        accumulate(i, s_diag)

    acc = acc_ref[...]
    o_t = jnp.concatenate([acc[0:half, 0:t] / acc[LANES:LANES + 1, 0:t],
                           acc[half:LANES, t:2 * t] / acc[LANES:LANES + 1, t:2 * t]], axis=0)
    o_ref[...] = o_t.T.astype(o_ref.dtype)


def _attention(q, k, v_t, bsz, seq):
    n = q.shape[0]
    t = ATT_TILE
    nq = seq // t
    return pl.pallas_call(
        _attn_kernel,
        grid=(bsz, HEAD_PAIRS, nq),
        in_specs=[pl.BlockSpec((t, 2 * LANES), lambda b, p, i: (b * nq + i, p)),
                  pl.BlockSpec((seq, 2 * LANES), lambda b, p, i: (b, p)),
                  pl.BlockSpec((LANES, seq), lambda b, p, i: (p, b))],
        out_specs=pl.BlockSpec((t, LANES), lambda b, p, i: (b * nq + i, p)),
        out_shape=jax.ShapeDtypeStruct((n, FOX_WIDTH), BF16),
        scratch_shapes=[pltpu.VMEM((1, 2 * t), F32), pltpu.VMEM((LANES + 2 * SUBLANES, 2 * t), F32)],
        compiler_params=_params(("arbitrary", "arbitrary", "arbitrary")),
        name="attn",
    )(q, k, v_t)


def _ssm_matrices(lambda_re, lambda_im, log_dt, b_re, b_im, c_re, c_im):
    t_len = SSM_CHUNK
    lam_re = jnp.minimum(lambda_re.astype(F32), LAMBDA_RE_MAX)
    lam_im = lambda_im.astype(F32)
    dt = jnp.exp(log_dt.astype(F32))[:, None]
    mag = jnp.exp(lam_re * dt)
    a_re = mag * jnp.cos(lam_im * dt)
    a_im = mag * jnp.sin(lam_im * dt)
    den = lam_re * lam_re + lam_im * lam_im
    nr = a_re - 1.0
    co_re = (nr * lam_re + a_im * lam_im) / den
    co_im = (a_im * lam_re - nr * lam_im) / den
    b_re = b_re.astype(F32)
    b_im = b_im.astype(F32)
    bb_re = co_re[..., None] * b_re - co_im[..., None] * b_im
    bb_im = co_re[..., None] * b_im + co_im[..., None] * b_re
    ls = jnp.arange(t_len + 1, dtype=F32)[:, None, None]
    pmag = jnp.exp(ls * (lam_re * dt)[None])
    pang = ls * (lam_im * dt)[None]
    pw_re = pmag * jnp.cos(pang)
    pw_im = pmag * jnp.sin(pang)
    c_re = c_re.astype(F32)
    c_im = c_im.astype(F32)
    width = t_len * SSM_GROUP
    w_re = c_re[None] * pw_re[:t_len, :, None, :] - c_im[None] * pw_im[:t_len, :, None, :]
    w_im = c_re[None] * pw_im[:t_len, :, None, :] + c_im[None] * pw_re[:t_len, :, None, :]
    kern = (jnp.einsum('lgdp,gpc->gcld', w_re, bb_re, precision=HIGHEST)
            - jnp.einsum('lgdp,gpc->gcld', w_im, bb_im, precision=HIGHEST))
    kern = kern.reshape(SSM_GROUPS, SSM_GROUP, width)
    toep = jnp.stack([jnp.pad(kern[:, :, :width - s * SSM_GROUP], ((0, 0), (0, 0), (s * SSM_GROUP, 0)))
                      for s in range(t_len)], axis=1)
    toep = _roll_lane_steps(_rotate_steps(toep, 1)).reshape(SSM_GROUPS, width, width)
    e_re = pw_re[:t_len][::-1].transpose(1, 0, 2)[:, :, None, :]
    e_im = pw_im[:t_len][::-1].transpose(1, 0, 2)[:, :, None, :]
    bt_re = bb_re.transpose(0, 2, 1)[:, None]
    bt_im = bb_im.transpose(0, 2, 1)[:, None]
    bs_re = e_re * bt_re - e_im * bt_im
    bs_im = e_re * bt_im + e_im * bt_re

    def lay_state(first, second):
        m = jnp.concatenate([first, second], axis=-1)
        return _rotate_steps(m, 1).reshape(SSM_GROUPS, width, 2 * SSM_STATE)

    b_state = lay_state(bs_re, bs_im)
    b_swap = lay_state(bs_im, bs_re)
    ct_re = jnp.tile(c_re.transpose(0, 2, 1), (1, 1, t_len))
    ct_im = jnp.tile(c_im.transpose(0, 2, 1), (1, 1, t_len))
    pt_re = jnp.repeat(pw_re[1:].transpose(1, 2, 0), SSM_GROUP, axis=-1)
    pt_im = jnp.repeat(pw_im[1:].transpose(1, 2, 0), SSM_GROUP, axis=-1)
    c_pow = _roll_lane_steps(jnp.concatenate([ct_re * pt_re - ct_im * pt_im,
                                              -(ct_re * pt_im + ct_im * pt_re)], axis=1))
    a1 = jnp.concatenate([pw_re[t_len], pw_re[t_len]], axis=-1)
    a2 = jnp.concatenate([-pw_im[t_len], pw_im[t_len]], axis=-1)
    a_step = jnp.stack([a1, a2], axis=1)
    return toep.astype(BF16), b_state.astype(BF16), b_swap.astype(BF16), c_pow.astype(BF16), a_step


def _ssm_kernel(n_chunks, bsz, u_ref, toep_ref, bst_ref, bsw_ref, cpw_ref, a_ref, y_ref,
                contrib_ref, cswap_ref, xprev_ref):
    u = u_ref[...]
    contrib_ref[...] = jnp.dot(u, bst_ref[...], preferred_element_type=F32)
    cswap_ref[...] = jnp.dot(u, bsw_ref[...], preferred_element_type=F32)
    a1 = a_ref[0:1, :]
    a2 = a_ref[1:2, :]

    def step(n, carry):
        x, xs = carry
        rows = pl.ds(n, bsz, stride=n_chunks)
        xprev_ref[rows, :] = x
        x_new = a1 * x + a2 * xs + contrib_ref[rows, :]
        xs_new = a1 * xs - a2 * x + cswap_ref[rows, :]
        return x_new, xs_new

    zero = jnp.zeros((bsz, 2 * SSM_STATE), F32)
    lax.fori_loop(0, n_chunks, step, (zero, zero), unroll=4)
    y_ref[...] = (jnp.dot(u, toep_ref[...], preferred_element_type=F32)
                  + jnp.dot(xprev_ref[...].astype(BF16), cpw_ref[...], preferred_element_type=F32))


def _ssm(u_flat, toep, b_state, b_swap, c_pow, a_step, bsz):
    g, rows, w = u_flat.shape
    per = lambda a, b: pl.BlockSpec((None, a, b), lambda i: (i, 0, 0))
    state = pltpu.VMEM((rows, 2 * SSM_STATE), F32)
    return pl.pallas_call(
        functools.partial(_ssm_kernel, rows // bsz, bsz),
        grid=(g,),
        in_specs=[per(rows, w), per(w, w), per(w, 2 * SSM_STATE), per(w, 2 * SSM_STATE),
                  per(2 * SSM_STATE, w), per(2, 2 * SSM_STATE)],
        out_specs=per(rows, w),
        out_shape=jax.ShapeDtypeStruct((g, rows, w), F32),
        scratch_shapes=[state, state, state],
        compiler_params=_params(("arbitrary",)),
        name="ssm",
    )(u_flat, toep, b_state, b_swap, c_pow, a_step)


def _mix_kernel(x_ref, of_ref, yf_ref, u_ref, ga_ref, gb_ref, mod_ref, dsk_ref, wglu_ref, wfox_ref,
                wssm_ref, wo_ref, g2_ref, wr_ref, br_ref, x1_ref, h2_ref, lg_ref, ytok_ref):
    _to_token_major(yf_ref, ytok_ref, yf_ref.shape[1])
    y_ssm = jnp.concatenate([ytok_ref[j] for j in range(SSM_WIDTH // LANES)], axis=1)
    y = y_ssm + dsk_ref[...] * u_ref[...]
    y = 0.5 * y * (1.0 + jnp.tanh(math.sqrt(2.0 / math.pi) * (y + 0.044715 * (y * y * y))))
    gl = jnp.dot(y.astype(BF16), wglu_ref[...], preferred_element_type=F32)
    o_ssm = gl[:, :SSM_WIDTH] * jax.nn.sigmoid(gl[:, SSM_WIDTH:])
    merged = (ga_ref[...].astype(F32) * jnp.dot(of_ref[...], wfox_ref[...], preferred_element_type=F32)
              + gb_ref[...].astype(F32) * jnp.dot(o_ssm.astype(BF16), wssm_ref[...],
                                                  preferred_element_type=F32))
    x1 = x_ref[...] + mod_ref[2:3, :] * jnp.dot(merged.astype(BF16), wo_ref[...],
                                                 preferred_element_type=F32)
    x1_ref[...] = x1
    h2 = _rms_modulate(x1, g2_ref[...], mod_ref[3:4, :], mod_ref[4:5, :])
    h2_ref[...] = h2
    lg_ref[...] = jnp.dot(h2, wr_ref[...], precision=HIGHEST, preferred_element_type=F32) + br_ref[...]


def _mix(x2, o_fox, y_flat, u, sga, sgb, mod3, d_skip, w_glu, w_fox, w_ssm, w_o, g2, w_r, b_r, seq):
    n, d = x2.shape
    tm = MIX_TILE
    tpb = seq // tm
    tok = lambda w: pl.BlockSpec((tm, w), lambda i: (i, 0))
    const = lambda a: pl.BlockSpec(a.shape, lambda i: (0,) * a.ndim)
    flat = pl.BlockSpec((SSM_GROUPS, tm // SSM_CHUNK, SSM_CHUNK * SSM_GROUP), lambda i: (0, i, 0))
    return pl.pallas_call(
        _mix_kernel,
        grid=(n // tm,),
        in_specs=[tok(d), tok(FOX_WIDTH), flat, tok(SSM_WIDTH), tok(d), tok(d),
                  pl.BlockSpec((None, N_MOD, d), lambda i: (i // tpb, 0, 0)),
                  const(d_skip), const(w_glu), const(w_fox), const(w_ssm), const(w_o), const(g2),
                  const(w_r), const(b_r)],
        out_specs=[tok(d), tok(d), tok(LANES)],
        out_shape=[jax.ShapeDtypeStruct((n, d), F32), jax.ShapeDtypeStruct((n, d), F32),
                   jax.ShapeDtypeStruct((n, LANES), F32)],
        scratch_shapes=[pltpu.VMEM((SSM_WIDTH // LANES, tm, LANES), F32)],
        compiler_params=_params(("arbitrary",)),
        name="mix",
    )(x2, o_fox, y_flat, u, sga, sgb, mod3, d_skip, w_glu, w_fox, w_ssm, w_o, g2, w_r, b_r)


def _route_kernel(lg_ref, tri_ref, idx_ref, wt_ref, cnt_ref, carry_ref):
    i = pl.program_id(0)

    @pl.when(i == 0)
    def _():
        carry_ref[...] = jnp.zeros_like(carry_ref)

    lg = lg_ref[...]
    tm = lg.shape[0]
    lane = lax.broadcasted_iota(jnp.int32, (tm, LANES), 1)
    neg = jnp.full_like(lg, -jnp.inf)

    def first_argmax(vals):
        mx = jnp.max(vals, axis=1, keepdims=True)
        ix = jnp.min(jnp.where(vals == mx, lane, LANES), axis=1, keepdims=True)
        return mx, ix

    is_group = lane < N_GROUPS
    g_max, gi = first_argmax(jnp.where(is_group, lg, neg))
    g_sum = jnp.sum(jnp.where(is_group, jnp.exp(lg - g_max), 0.0), axis=1, keepdims=True)
    p_group = 1.0 / g_sum
    lo = N_GROUPS + EXPERTS_PER_GROUP * gi
    in_group = (lane >= lo) & (lane < lo + EXPERTS_PER_GROUP)
    cand = jnp.where(in_group, lg, neg)
    v1, i1 = first_argmax(cand)
    v2, i2 = first_argmax(jnp.where(lane == i1, neg, cand))
    tt = jnp.exp(v2 - v1)
    w1 = p_group / (1.0 + tt)
    w2 = p_group * tt / (1.0 + tt)
    e1 = i1 - N_GROUPS
    e2 = i2 - N_GROUPS
    sel1 = lane == e1
    sel2 = lane == e2
    onehot = (sel1 | sel2).astype(F32)
    before = jnp.dot(tri_ref[...], onehot.astype(BF16), preferred_element_type=F32) + carry_ref[0:1, :]
    r1 = jnp.sum(jnp.where(sel1, before, 0.0), axis=1, keepdims=True).astype(jnp.int32)
    r2 = jnp.sum(jnp.where(sel2, before, 0.0), axis=1, keepdims=True).astype(jnp.int32)
    total = before[-1:, :] + onehot[-1:, :]
    carry_ref[...] = jnp.broadcast_to(total, carry_ref.shape)
    cnt_ref[...] = jnp.broadcast_to(total, cnt_ref.shape)
    idx_ref[...] = jnp.where(lane == 0, e1, jnp.where(lane == 1, e2, jnp.where(lane == 2, r1, r2)))
    wt_ref[...] = jnp.where(lane == 0, w1, w2)


def _route(logits):
    n = logits.shape[0]
    tm = TOK_TILE
    tri = jnp.tril(jnp.ones((tm, tm), BF16), k=-1)
    tok = pl.BlockSpec((tm, LANES), lambda i: (i, 0))
    return pl.pallas_call(
        _route_kernel,
        grid=(n // tm,),
        in_specs=[tok, pl.BlockSpec((tm, tm), lambda i: (0, 0))],
        out_specs=[tok, tok, pl.BlockSpec((SUBLANES, LANES), lambda i: (0, 0))],
        out_shape=[jax.ShapeDtypeStruct((n, LANES), jnp.int32), jax.ShapeDtypeStruct((n, LANES), F32),
                   jax.ShapeDtypeStruct((SUBLANES, LANES), F32)],
        scratch_shapes=[pltpu.VMEM((SUBLANES, LANES), F32)],
        compiler_params=_params(("arbitrary",)),
        name="route",
    )(logits, tri)


def _row_copy(src_ref, src_row, dst_ref, dst_row, sem):
    return pltpu.make_async_copy(src_ref.at[pl.ds(src_row, 1), :], dst_ref.at[pl.ds(dst_row, 1), :], sem)


ISSUE_UNROLL = 8


def _dispatch_kernel(dest_ref, h_ref, rows_in_ref, rows_ref, sem):
    del rows_in_ref
    tm = h_ref.shape[0]

    def issue(g, c):
        for j in range(ISSUE_UNROLL):
            t = g * ISSUE_UNROLL + j
            _row_copy(h_ref, t, rows_ref, dest_ref[0, 0, t], sem).start(priority=0)
            _row_copy(h_ref, t, rows_ref, dest_ref[0, 0, tm + t], sem).start(priority=1)
        return c

    lax.fori_loop(0, tm // ISSUE_UNROLL, issue, 0)
    for _ in range(2):
        pltpu.make_async_copy(h_ref, rows_ref.at[pl.ds(0, tm), :], sem).wait()


def _dispatch(dest3, h2, rows_zero):
    n, d = h2.shape
    tm = MOVE_TILE
    return pl.pallas_call(
        _dispatch_kernel,
        grid=(n // tm,),
        in_specs=[pl.BlockSpec((1, 1, 2 * tm), lambda i: (i, 0, 0), memory_space=pltpu.SMEM),
                  pl.BlockSpec((tm, d), lambda i: (i, 0)),
                  pl.BlockSpec(memory_space=pl.ANY)],
        out_specs=pl.BlockSpec(memory_space=pl.ANY),
        out_shape=jax.ShapeDtypeStruct(rows_zero.shape, rows_zero.dtype),
        scratch_shapes=[pltpu.SemaphoreType.DMA(())],
        input_output_aliases={2: 0},
        compiler_params=_params(("arbitrary",)),
        name="dispatch",
    )(dest3, h2, rows_zero)


def _combine_kernel(dest_ref, x1_ref, wt_ref, mod_ref, gf_ref, yr_ref, o_ref, buf_ref, sem):
    tm = x1_ref.shape[0]

    def issue(g, c):
        for j in range(ISSUE_UNROLL):
            t = g * ISSUE_UNROLL + j
            _row_copy(yr_ref, dest_ref[0, 0, t], buf_ref.at[0], t, sem).start(priority=0)
            _row_copy(yr_ref, dest_ref[0, 0, tm + t], buf_ref.at[1], t, sem).start(priority=1)
        return c

    lax.fori_loop(0, tm // ISSUE_UNROLL, issue, 0)
    for slot in range(2):
        pltpu.make_async_copy(yr_ref.at[pl.ds(0, tm), :], buf_ref.at[slot], sem).wait()
    wt = wt_ref[...]
    moe = wt[:, 0:1] * buf_ref[0] + wt[:, 1:2] * buf_ref[1]
    x = x1_ref[...] + mod_ref[5:6, :] * moe
    ms = jnp.mean(x * x, axis=-1, keepdims=True)
    o_ref[...] = (x * lax.rsqrt(ms + RMS_EPS)) * gf_ref[...]


def _combine(dest3, x1, wts, mod3, final_g, y_rows, seq):
    n, d = x1.shape
    tm = MOVE_TILE
    tpb = seq // tm
    return pl.pallas_call(
        _combine_kernel,
        grid=(n // tm,),
        in_specs=[pl.BlockSpec((1, 1, 2 * tm), lambda i: (i, 0, 0), memory_space=pltpu.SMEM),
                  pl.BlockSpec((tm, d), lambda i: (i, 0)),
                  pl.BlockSpec((tm, LANES), lambda i: (i, 0)),
                  pl.BlockSpec((None, N_MOD, d), lambda i: (i // tpb, 0, 0)),
                  pl.BlockSpec((1, d), lambda i: (0, 0)),
                  pl.BlockSpec(memory_space=pl.ANY)],
        out_specs=pl.BlockSpec((tm, d), lambda i: (i, 0)),
        out_shape=jax.ShapeDtypeStruct((n, d), F32),
        scratch_shapes=[pltpu.VMEM((2, tm, d), F32), pltpu.SemaphoreType.DMA(())],
        compiler_params=_params(("arbitrary",)),
        name="combine",
    )(dest3, x1, wts, mod3, final_g, y_rows)


def _expert_kernel(be_ref, nv_ref, x_ref, wg_ref, wu_ref, wd_ref, y_ref, wgb_ref, wub_ref, wdb_ref):
    i = pl.program_id(0)
    valid = i < nv_ref[0]
    new_expert = (i == 0) | (be_ref[i] != be_ref[jnp.maximum(i - 1, 0)])

    @pl.when(valid & new_expert)
    def _():
        wgb_ref[...] = wg_ref[...].astype(BF16)
        wub_ref[...] = wu_ref[...].astype(BF16)
        wdb_ref[...] = wd_ref[...].astype(BF16)

    @pl.when(valid)
    def _():
        xb = x_ref[...].astype(BF16)
        a = jnp.dot(xb, wgb_ref[...], preferred_element_type=F32)
        b = jnp.dot(xb, wub_ref[...], preferred_element_type=F32)
        hid = (a * jax.nn.sigmoid(a)) * b
        y_ref[...] = jnp.dot(hid.astype(BF16), wdb_ref[...], preferred_element_type=F32)

    @pl.when(jnp.logical_not(valid))
    def _():
        y_ref[...] = jnp.zeros_like(y_ref)


def _experts(blk_e, n_valid, x_rows, w_gate, w_up, w_down):
    rows, d = x_rows.shape
    tb = ROW_BLOCK
    grid_spec = pltpu.PrefetchScalarGridSpec(
        num_scalar_prefetch=2,
        grid=(rows // tb,),
        in_specs=[pl.BlockSpec((tb, d), lambda i, be, nv: (i, 0)),
                  pl.BlockSpec((None, d, D_EXPERT), lambda i, be, nv: (be[i], 0, 0)),
                  pl.BlockSpec((None, d, D_EXPERT), lambda i, be, nv: (be[i], 0, 0)),
                  pl.BlockSpec((None, D_EXPERT, d), lambda i, be, nv: (be[i], 0, 0))],
        out_specs=pl.BlockSpec((tb, d), lambda i, be, nv: (i, 0)),
        scratch_shapes=[pltpu.VMEM((d, D_EXPERT), BF16), pltpu.VMEM((d, D_EXPERT), BF16),
                        pltpu.VMEM((D_EXPERT, d), BF16)],
    )
    return pl.pallas_call(
        _expert_kernel,
        grid_spec=grid_spec,
        out_shape=jax.ShapeDtypeStruct((rows, d), F32),
        compiler_params=_params(("arbitrary",)),
        name="experts",
    )(blk_e, n_valid, x_rows, w_gate, w_up, w_down)


def kernel(x, c, w_ada, b_ada, norm_mix_g, w_in, b_forget, w_out_fox, lambda_re, lambda_im, log_dt,
           ssm_b_re, ssm_b_im, ssm_c_re, ssm_c_im, d_skip, w_glu, w_out_ssm, w_o, norm_ffn_g,
           w_router_group, b_router_group, w_router_expert, b_router_expert, w_gate_e, w_up_e,
           w_down_e, final_g):
    bsz, seq, d = x.shape
    n = bsz * seq
    assert w_ada.shape[0] == 1, "the final RMSNorm is fused into the (single) layer's combine kernel"
    xc = x.reshape(n, d)
    for l in range(1):
        mod3 = _mod(c, w_ada[l], b_ada[l]).reshape(bsz, N_MOD, d)

        wi = w_in[l]
        s_q, s_k, s_v, s_f, s_u, s_ga = 512, 1024, 1536, 1544, 2056, 3080
        scale = FOX_HEAD_DIM ** -0.5
        w_all = jnp.concatenate(
            [wi[:, :s_q] * scale, wi[:, s_q:s_k], wi[:, s_f:s_u], wi[:, s_u:s_ga],
             wi[:, s_ga:], jnp.pad(wi[:, s_v:s_f], ((0, 0), (0, LANES - FOX_HEADS)))],
            axis=1).astype(BF16)
        w_vt = wi[:, s_k:s_v].T.astype(BF16)
        bf_pad = jnp.pad(b_forget[l], (0, LANES - FOX_HEADS)).reshape(1, LANES)
        q, k, v_t, u, u_flat, sga, sgb = _inproj(xc, mod3, norm_mix_g[l].reshape(1, d), w_all, w_vt,
                                                 bf_pad, seq)

        o_fox = _attention(q, k, v_t, bsz, seq)

        toep, b_state, b_swap, c_pow, a_step = _ssm_matrices(
            lambda_re[l], lambda_im[l], log_dt[l], ssm_b_re[l], ssm_b_im[l], ssm_c_re[l], ssm_c_im[l])
        y_flat = _ssm(u_flat, toep, b_state, b_swap, c_pow, a_step, bsz)

        w_r = jnp.pad(jnp.concatenate([w_router_group[l], w_router_expert[l]], axis=1),
                      ((0, 0), (0, LANES - N_GROUPS - N_EXPERTS)))
        b_r = jnp.pad(jnp.concatenate([b_router_group[l], b_router_expert[l]]),
                      (0, LANES - N_GROUPS - N_EXPERTS)).reshape(1, LANES)
        x1, h2, logits = _mix(xc, o_fox, y_flat, u, sga, sgb, mod3, d_skip[l].reshape(1, SSM_WIDTH),
                              w_glu[l].astype(BF16), w_out_fox[l].astype(BF16),
                              w_out_ssm[l].astype(BF16), w_o[l].astype(BF16),
                              norm_ffn_g[l].reshape(1, d), w_r, b_r, seq)

        idx, wts, cnt = _route(logits)
        counts = cnt[0, :N_EXPERTS].astype(jnp.int32)
        pcounts = ((counts + ROW_BLOCK - 1) // ROW_BLOCK) * ROW_BLOCK
        pends = jnp.cumsum(pcounts)
        pstarts = pends - pcounts
        er = idx[:, 0:4].T
        hit = er[0:2, None, :] == jnp.arange(N_EXPERTS, dtype=jnp.int32)[None, :, None]
        dest = jnp.sum(jnp.where(hit, pstarts[None, :, None], 0), axis=1) + er[2:4]
        rows = 2 * n + N_EXPERTS * ROW_BLOCK
        n_blocks = rows // ROW_BLOCK
        blk_start = jnp.arange(n_blocks, dtype=jnp.int32) * ROW_BLOCK
        blk_e = jnp.minimum(jnp.sum((pends[None, :] <= blk_start[:, None]).astype(jnp.int32), axis=1),
                            N_EXPERTS - 1)
        n_valid = (pends[-1:] // ROW_BLOCK).astype(jnp.int32)
        dest3 = (dest.astype(jnp.int32).reshape(2, n // MOVE_TILE, MOVE_TILE).transpose(1, 0, 2)
                 .reshape(n // MOVE_TILE, 1, 2 * MOVE_TILE))

        x_rows = _dispatch(dest3, h2, jnp.zeros((rows, d), F32))
        y_rows = _experts(blk_e, n_valid, x_rows, w_gate_e[l], w_up_e[l], w_down_e[l])
        xc = _combine(dest3, x1, wts, mod3, final_g.reshape(1, d), y_rows, seq)
    return xc.reshape(bsz, seq, d)
```

```python
import functools
import math

import jax
import jax.numpy as jnp
import numpy as np
from jax import lax
from jax.experimental import pallas as pl
from jax.experimental.pallas import tpu as pltpu

F32 = jnp.float32
BF16 = jnp.bfloat16

D_MODEL = 1024
N_MOD = 6
RMS_EPS = 1e-6
FOX_HEADS = 8
FOX_HEAD_DIM = 64
FOX_WIDTH = FOX_HEADS * FOX_HEAD_DIM
HEAD_PAIRS = FOX_HEADS // 2
SSM_WIDTH = 512
SSM_GROUP = 16
SSM_GROUPS = SSM_WIDTH // SSM_GROUP
SSM_STATE = 64
LAMBDA_RE_MAX = -1e-4
N_GROUPS = 4
EXPERTS_PER_GROUP = 8
N_EXPERTS = N_GROUPS * EXPERTS_PER_GROUP
D_EXPERT = 512

LANES = 128
SUBLANES = 8
VMEM_LIMIT = 56 * 1024 * 1024

SSM_CHUNK = 16
TOK_TILE = 512
MIX_TILE = 256
ATT_Q_TILE = 512
ATT_K_TILE = 256
ROW_BLOCK = 256
MOVE_TILE = 256
NEG_BIG = -1e30

HIGHEST = lax.Precision.HIGHEST


def _params(sem):
    return pltpu.CompilerParams(dimension_semantics=sem, vmem_limit_bytes=VMEM_LIMIT)


def _rms_modulate(x, gain, shift, scale):
    ms = jnp.mean(x * x, axis=-1, keepdims=True)
    return (x * lax.rsqrt(ms + RMS_EPS)) * gain * (1.0 + scale) + shift


def _mod_kernel(c_ref, w_ref, b_ref, o_ref):
    c = c_ref[...]
    ca = (c * jax.nn.sigmoid(c)).astype(BF16)
    o_ref[...] = jnp.dot(ca, w_ref[...].astype(BF16), preferred_element_type=F32) + b_ref[...]


def _mod(c, w_ada, b_ada):
    bsz, d = c.shape
    cols = w_ada.shape[1]
    tn = 1536
    return pl.pallas_call(
        _mod_kernel,
        grid=(cols // tn,),
        in_specs=[pl.BlockSpec((bsz, d), lambda j: (0, 0)),
                  pl.BlockSpec((d, tn), lambda j: (0, j)),
                  pl.BlockSpec((1, tn), lambda j: (0, j))],
        out_specs=pl.BlockSpec((bsz, tn), lambda j: (0, j)),
        out_shape=jax.ShapeDtypeStruct((bsz, cols), F32),
        compiler_params=_params(("arbitrary",)),
        name="mod",
    )(c, w_ada, b_ada.reshape(1, cols))


_C_Q, _C_K, _C_U, _C_GA, _C_GB, _C_F, _C_END = 0, 512, 1024, 1536, 2560, 3584, 3712


def _lane_block():
    return lax.broadcasted_iota(jnp.int32, (1, LANES), 1) // SSM_GROUP


def _to_group_major(tok_ref, flat_ref, rows):
    blk = _lane_block()
    for half in range(2):
        for j in range(SSM_WIDTH // LANES):
            w = []
            for s8 in range(8):
                v = tok_ref[j, pl.ds(8 * half + s8, rows, stride=SSM_CHUNK), :]
                w.append(pltpu.roll(v, s8 * SSM_GROUP, axis=1) if s8 else v)
            for p in range(8):
                acc = w[0]
                for s8 in range(1, 8):
                    acc = jnp.where(blk == (p + s8) % 8, w[s8], acc)
                flat_ref[8 * j + p, :, half * LANES:(half + 1) * LANES] = acc.astype(flat_ref.dtype)


def _to_token_major(flat_ref, tok_ref, rows):
    blk = _lane_block()
    for half in range(2):
        for j in range(SSM_WIDTH // LANES):
            ys = [flat_ref[8 * j + p, :, half * LANES:(half + 1) * LANES] for p in range(8)]
            for s8 in range(8):
                acc = ys[0]
                for p in range(1, 8):
                    acc = jnp.where(blk == (p + s8) % 8, ys[p], acc)
                if s8:
                    acc = pltpu.roll(acc, LANES - s8 * SSM_GROUP, axis=1)
                tok_ref[j, pl.ds(8 * half + s8, rows, stride=SSM_CHUNK), :] = acc


def _rotate_steps(m, axis):
    shape = m.shape
    m = m.reshape((SSM_GROUPS // 8, 8) + shape[1:])
    ax = axis + 1
    m = m.reshape(m.shape[:ax] + (2, 8) + m.shape[ax + 1:])
    m = jnp.stack([jnp.roll(m[:, p], p, axis=ax) for p in range(8)], axis=1)
    return m.reshape(shape)


def _roll_lane_steps(m):
    shape = m.shape
    m = m.reshape((SSM_GROUPS // 8, 8) + shape[1:-1] + (2, LANES))
    m = jnp.stack([jnp.roll(m[:, p], p * SSM_GROUP, axis=-1) for p in range(8)], axis=1)
    return m.reshape(shape)


def _bias_lane_placement():
    pq = np.zeros((3 * LANES, HEAD_PAIRS * LANES), np.float32)
    pk = np.zeros((3 * LANES, HEAD_PAIRS * LANES), np.float32)
    bq = np.zeros((1, HEAD_PAIRS * LANES), np.float32)
    bk = np.zeros((1, HEAD_PAIRS * LANES), np.float32)
    for head in range(FOX_HEADS):
        base = (head // 2) * LANES + (head % 2) * 8
        for term in range(3):
            pq[term * LANES + head, base + term] = 1.0
            pk[term * LANES + head, base + 3 + term] = -1.0
            bq[0, base + 3 + term] = 1.0
            bk[0, base + term] = 1.0
    return pq, pk, bq, bk


def _top_bits(a):
    bits = lax.bitcast_convert_type(a, jnp.uint32) & jnp.uint32(0xFFFF0000)
    return lax.bitcast_convert_type(bits, F32)


def _inproj_kernel(tiles_per_batch, x_ref, mod_ref, g_ref, w_ref, wvt_ref, bf_ref, tri_ref,
                   pq_ref, pk_ref, bq_ref, bk_ref,
                   q_ref, k_ref, vt_ref, u_ref, uflat_ref, ga_ref, gb_ref, carry_ref, uslab_ref):
    i = pl.program_id(0)
    h = _rms_modulate(x_ref[...], g_ref[...], mod_ref[0:1, :], mod_ref[1:2, :])
    hb = h.astype(BF16)

    def proj(a, b):
        return jnp.dot(hb, w_ref[:, a:b], preferred_element_type=F32)

    q = proj(_C_Q, _C_K).astype(BF16)
    k = proj(_C_K, _C_U).astype(BF16)
    vt_ref[...] = lax.dot_general(wvt_ref[...], hb, (((1,), (1,)), ((), ())),
                                  preferred_element_type=F32).astype(BF16)
    u = proj(_C_U, _C_GA)
    u_ref[...] = u
    for j in range(SSM_WIDTH // LANES):
        uslab_ref[j] = u[:, j * LANES:(j + 1) * LANES]
    _to_group_major(uslab_ref, uflat_ref, u.shape[0] // SSM_CHUNK)
    ga_ref[...] = jax.nn.sigmoid(proj(_C_GA, _C_GB)).astype(BF16)
    gb_ref[...] = jax.nn.sigmoid(proj(_C_GB, _C_F)).astype(BF16)

    f = proj(_C_F, _C_END) + bf_ref[...]
    logf = jnp.minimum(f, 0.0) - jnp.log(1.0 + jnp.exp(-jnp.abs(f)))

    @pl.when(i % tiles_per_batch == 0)
    def _():
        carry_ref[...] = jnp.zeros_like(carry_ref)

    def split3(a):
        hi = _top_bits(a)
        r1 = a - hi
        mid = _top_bits(r1)
        return jnp.concatenate([hi, mid, _top_bits(r1 - mid)], axis=1).astype(BF16)

    part = jnp.dot(tri_ref[...], split3(logf), preferred_element_type=F32)
    cs = (part[:, :LANES] + part[:, LANES:2 * LANES] + part[:, 2 * LANES:]) + carry_ref[0:1, :]
    carry_ref[...] = jnp.broadcast_to(cs[-1:, :], carry_ref.shape)

    terms = split3(cs)
    bias_q = (jnp.dot(terms, pq_ref[...], preferred_element_type=F32) + bq_ref[...]).astype(BF16)
    bias_k = (jnp.dot(terms, pk_ref[...], preferred_element_type=F32) + bk_ref[...]).astype(BF16)
    for p in range(HEAD_PAIRS):
        lanes = slice(p * LANES, (p + 1) * LANES)
        q_ref[:, 2 * p * LANES:(2 * p + 1) * LANES] = q[:, lanes]
        q_ref[:, (2 * p + 1) * LANES:(2 * p + 2) * LANES] = bias_q[:, lanes]
        k_ref[:, 2 * p * LANES:(2 * p + 1) * LANES] = k[:, lanes]
        k_ref[:, (2 * p + 1) * LANES:(2 * p + 2) * LANES] = bias_k[:, lanes]


def _inproj(x2, mod3, gain, w_all, w_vt, bf_pad, seq):
    n, d = x2.shape
    tm = TOK_TILE
    tpb = seq // tm
    tri = jnp.tril(jnp.ones((tm, tm), BF16))
    pq, pk, bq, bk = _bias_lane_placement()
    tok = lambda w: pl.BlockSpec((tm, w), lambda i: (i, 0))
    const = lambda shape: pl.BlockSpec(shape, lambda i: (0,) * len(shape))
    qk_width = 2 * FOX_WIDTH
    return pl.pallas_call(
        functools.partial(_inproj_kernel, tpb),
        grid=(n // tm,),
        in_specs=[tok(d),
                  pl.BlockSpec((None, N_MOD, d), lambda i: (i // tpb, 0, 0)),
                  const((1, d)), const((d, _C_END)), const((FOX_WIDTH, d)), const((1, LANES)),
                  const((tm, tm)), const(pq.shape), const(pk.shape), const(bq.shape), const(bk.shape)],
        out_specs=[tok(qk_width), tok(qk_width), pl.BlockSpec((FOX_WIDTH, tm), lambda i: (0, i)),
                   tok(SSM_WIDTH),
                   pl.BlockSpec((SSM_GROUPS, tm // SSM_CHUNK, SSM_CHUNK * SSM_GROUP), lambda i: (0, i, 0)),
                   tok(d), tok(d)],
        out_shape=[jax.ShapeDtypeStruct((n, qk_width), BF16)] * 2
        + [jax.ShapeDtypeStruct((FOX_WIDTH, n), BF16)]
        + [jax.ShapeDtypeStruct((n, SSM_WIDTH), F32)]
        + [jax.ShapeDtypeStruct((SSM_GROUPS, n // SSM_CHUNK, SSM_CHUNK * SSM_GROUP), BF16)]
        + [jax.ShapeDtypeStruct((n, d), BF16)] * 2,
        scratch_shapes=[pltpu.VMEM((SUBLANES, LANES), F32),
                        pltpu.VMEM((SSM_WIDTH // LANES, tm, LANES), F32)],
        compiler_params=_params(("arbitrary",)),
        name="inproj",
    )(x2, mod3, gain, w_all, w_vt, bf_pad, tri, jnp.asarray(pq, BF16), jnp.asarray(pk, BF16),
      jnp.asarray(bq), jnp.asarray(bk))


def _attn_kernel(q_ref, k_ref, vt_ref, o_ref, m_ref, acc_ref):
    i = pl.program_id(2)
    tq, tk = ATT_Q_TILE, ATT_K_TILE
    q = q_ref[...]
    lane = lax.broadcasted_iota(jnp.int32, (1, 2 * LANES), 1)
    zq = jnp.zeros_like(q)
    half = FOX_HEAD_DIM
    own0 = (lane < half) | ((lane >= LANES) & (lane < LANES + 8))
    own1 = ((lane >= half) & (lane < LANES)) | ((lane >= LANES + 8) & (lane < LANES + 16))
    q_both = jnp.concatenate([jnp.where(own0, q, zq), jnp.where(own1, q, zq)], axis=0)
    m_ref[...] = jnp.full(m_ref.shape, NEG_BIG, F32)
    acc_ref[...] = jnp.zeros(acc_ref.shape, F32)
    ones_rows = jnp.ones((2 * SUBLANES, tk), BF16)
    key_in_tile = lax.broadcasted_iota(jnp.int32, (tk, 2 * tq), 0)
    qry_pos = i * tq + (lax.broadcasted_iota(jnp.int32, (tk, 2 * tq), 1) & (tq - 1))

    def scores(j):
        start = pl.multiple_of(j * tk, tk)
        s = lax.dot_general(k_ref[pl.ds(start, tk), :], q_both, (((1,), (1,)), ((), ())),
                            preferred_element_type=F32)
        return jnp.where(key_in_tile + j * tk <= qry_pos, s, NEG_BIG)

    def accumulate(j, s):
        start = pl.multiple_of(j * tk, tk)
        va = jnp.concatenate([vt_ref[:, pl.ds(start, tk)], ones_rows], axis=0)
        m_old = m_ref[...]
        m_new = jnp.maximum(m_old, jnp.max(s, axis=0, keepdims=True))
        alpha = jnp.exp(m_old - m_new)
        p = jnp.exp(s - m_new).astype(BF16)
        acc_ref[...] = alpha * acc_ref[...] + jnp.dot(va, p, preferred_element_type=F32)
        m_ref[...] = m_new

    def body(j, s_cur):
        s_next = scores(j + 1)
        accumulate(j, s_cur)
        return s_next

    last = (i + 1) * (tq // tk) - 1
    s_last = lax.fori_loop(0, last, body, scores(0))
    accumulate(last, s_last)

    acc = acc_ref[...]
    o_t = jnp.concatenate([acc[0:half, 0:tq] / acc[LANES:LANES + 1, 0:tq],
                           acc[half:LANES, tq:2 * tq] / acc[LANES:LANES + 1, tq:2 * tq]], axis=0)
    o_ref[...] = o_t.T.astype(o_ref.dtype)


def _attention(q, k, v_t, bsz, seq):
    n = q.shape[0]
    t = ATT_Q_TILE
    nq = seq // t
    return pl.pallas_call(
        _attn_kernel,
        grid=(bsz, HEAD_PAIRS, nq),
        in_specs=[pl.BlockSpec((t, 2 * LANES), lambda b, p, i: (b * nq + i, p)),
                  pl.BlockSpec((seq, 2 * LANES), lambda b, p, i: (b, p)),
                  pl.BlockSpec((LANES, seq), lambda b, p, i: (p, b))],
        out_specs=pl.BlockSpec((t, LANES), lambda b, p, i: (b * nq + i, p)),
        out_shape=jax.ShapeDtypeStruct((n, FOX_WIDTH), BF16),
        scratch_shapes=[pltpu.VMEM((1, 2 * t), F32), pltpu.VMEM((LANES + 2 * SUBLANES, 2 * t), F32)],
        compiler_params=_params(("arbitrary", "arbitrary", "arbitrary")),
        name="attn",
    )(q, k, v_t)


def _ssm_matrices(lambda_re, lambda_im, log_dt, b_re, b_im, c_re, c_im):
    t_len = SSM_CHUNK
    lam_re = jnp.minimum(lambda_re.astype(F32), LAMBDA_RE_MAX)
    lam_im = lambda_im.astype(F32)
    dt = jnp.exp(log_dt.astype(F32))[:, None]
    mag = jnp.exp(lam_re * dt)
    a_re = mag * jnp.cos(lam_im * dt)
    a_im = mag * jnp.sin(lam_im * dt)
    den = lam_re * lam_re + lam_im * lam_im
    nr = a_re - 1.0
    co_re = (nr * lam_re + a_im * lam_im) / den
    co_im = (a_im * lam_re - nr * lam_im) / den
    b_re = b_re.astype(F32)
    b_im = b_im.astype(F32)
    bb_re = co_re[..., None] * b_re - co_im[..., None] * b_im
    bb_im = co_re[..., None] * b_im + co_im[..., None] * b_re
    ls = jnp.arange(t_len + 1, dtype=F32)[:, None, None]
    pmag = jnp.exp(ls * (lam_re * dt)[None])
    pang = ls * (lam_im * dt)[None]
    pw_re = pmag * jnp.cos(pang)
    pw_im = pmag * jnp.sin(pang)
    c_re = c_re.astype(F32)
    c_im = c_im.astype(F32)
    width = t_len * SSM_GROUP
    w_re = c_re[None] * pw_re[:t_len, :, None, :] - c_im[None] * pw_im[:t_len, :, None, :]
    w_im = c_re[None] * pw_im[:t_len, :, None, :] + c_im[None] * pw_re[:t_len, :, None, :]
    kern = (jnp.einsum('lgdp,gpc->gcld', w_re, bb_re, precision=HIGHEST)
            - jnp.einsum('lgdp,gpc->gcld', w_im, bb_im, precision=HIGHEST))
    kern = kern.reshape(SSM_GROUPS, SSM_GROUP, width)
    toep = jnp.stack([jnp.pad(kern[:, :, :width - s * SSM_GROUP], ((0, 0), (0, 0), (s * SSM_GROUP, 0)))
                      for s in range(t_len)], axis=1)
    toep = _roll_lane_steps(_rotate_steps(toep, 1)).reshape(SSM_GROUPS, width, width)
    e_re = pw_re[:t_len][::-1].transpose(1, 0, 2)[:, :, None, :]
    e_im = pw_im[:t_len][::-1].transpose(1, 0, 2)[:, :, None, :]
    bt_re = bb_re.transpose(0, 2, 1)[:, None]
    bt_im = bb_im.transpose(0, 2, 1)[:, None]
    bs_re = e_re * bt_re - e_im * bt_im
    bs_im = e_re * bt_im + e_im * bt_re

    def lay_state(first, second):
        m = jnp.concatenate([first, second], axis=-1)
        return _rotate_steps(m, 1).reshape(SSM_GROUPS, width, 2 * SSM_STATE)

    b_state = lay_state(bs_re, bs_im)
    b_swap = lay_state(bs_im, bs_re)
    ct_re = jnp.tile(c_re.transpose(0, 2, 1), (1, 1, t_len))
    ct_im = jnp.tile(c_im.transpose(0, 2, 1), (1, 1, t_len))
    pt_re = jnp.repeat(pw_re[1:].transpose(1, 2, 0), SSM_GROUP, axis=-1)
    pt_im = jnp.repeat(pw_im[1:].transpose(1, 2, 0), SSM_GROUP, axis=-1)
    c_pow = _roll_lane_steps(jnp.concatenate([ct_re * pt_re - ct_im * pt_im,
                                              -(ct_re * pt_im + ct_im * pt_re)], axis=1))
    a1 = jnp.concatenate([pw_re[t_len], pw_re[t_len]], axis=-1)
    a2 = jnp.concatenate([-pw_im[t_len], pw_im[t_len]], axis=-1)
    a_step = jnp.stack([a1, a2], axis=1)
    return toep.astype(BF16), b_state.astype(BF16), b_swap.astype(BF16), c_pow.astype(BF16), a_step


def _ssm_kernel(n_chunks, bsz, u_ref, toep_ref, bst_ref, bsw_ref, cpw_ref, a_ref, y_ref,
                contrib_ref, cswap_ref, xprev_ref):
    u = u_ref[...]
    contrib_ref[...] = jnp.dot(u, bst_ref[...], preferred_element_type=F32)
    cswap_ref[...] = jnp.dot(u, bsw_ref[...], preferred_element_type=F32)
    a1 = a_ref[0:1, :]
    a2 = a_ref[1:2, :]

    def step(n, carry):
        x, xs = carry
        rows = pl.ds(n, bsz, stride=n_chunks)
        xprev_ref[rows, :] = x
        x_new = a1 * x + a2 * xs + contrib_ref[rows, :]
        xs_new = a1 * xs - a2 * x + cswap_ref[rows, :]
        return x_new, xs_new

    zero = jnp.zeros((bsz, 2 * SSM_STATE), F32)
    lax.fori_loop(0, n_chunks, step, (zero, zero), unroll=4)
    y_ref[...] = (jnp.dot(u, toep_ref[...], preferred_element_type=F32)
                  + jnp.dot(xprev_ref[...].astype(BF16), cpw_ref[...], preferred_element_type=F32))


def _ssm(u_flat, toep, b_state, b_swap, c_pow, a_step, bsz):
    g, rows, w = u_flat.shape
    per = lambda a, b: pl.BlockSpec((None, a, b), lambda i: (i, 0, 0))
    state = pltpu.VMEM((rows, 2 * SSM_STATE), F32)
    return pl.pallas_call(
        functools.partial(_ssm_kernel, rows // bsz, bsz),
        grid=(g,),
        in_specs=[per(rows, w), per(w, w), per(w, 2 * SSM_STATE), per(w, 2 * SSM_STATE),
                  per(2 * SSM_STATE, w), per(2, 2 * SSM_STATE)],
        out_specs=per(rows, w),
        out_shape=jax.ShapeDtypeStruct((g, rows, w), F32),
        scratch_shapes=[state, state, state],
        compiler_params=_params(("arbitrary",)),
        name="ssm",
    )(u_flat, toep, b_state, b_swap, c_pow, a_step)


def _mix_kernel(x_ref, of_ref, yf_ref, u_ref, ga_ref, gb_ref, mod_ref, dsk_ref, wglu_ref, wfox_ref,
                wssm_ref, wo_ref, g2_ref, wr_ref, br_ref, x1_ref, h2_ref, lg_ref, ytok_ref):
    _to_token_major(yf_ref, ytok_ref, yf_ref.shape[1])
    y_ssm = jnp.concatenate([ytok_ref[j] for j in range(SSM_WIDTH // LANES)], axis=1)
    y = y_ssm + dsk_ref[...] * u_ref[...]
    y = 0.5 * y * (1.0 + jnp.tanh(math.sqrt(2.0 / math.pi) * (y + 0.044715 * (y * y * y))))
    gl = jnp.dot(y.astype(BF16), wglu_ref[...], preferred_element_type=F32)
    o_ssm = gl[:, :SSM_WIDTH] * jax.nn.sigmoid(gl[:, SSM_WIDTH:])
    merged = (ga_ref[...].astype(F32) * jnp.dot(of_ref[...], wfox_ref[...], preferred_element_type=F32)
              + gb_ref[...].astype(F32) * jnp.dot(o_ssm.astype(BF16), wssm_ref[...],
                                                  preferred_element_type=F32))
    x1 = x_ref[...] + mod_ref[2:3, :] * jnp.dot(merged.astype(BF16), wo_ref[...],
                                                 preferred_element_type=F32)
    x1_ref[...] = x1
    h2 = _rms_modulate(x1, g2_ref[...], mod_ref[3:4, :], mod_ref[4:5, :])
    h2_ref[...] = h2
    a1 = _top_bits(h2)
    a2 = _top_bits(h2 - a1)
    lhs = jnp.concatenate([a1, a1, a2], axis=1).astype(BF16)
    lg_ref[...] = jnp.dot(lhs, wr_ref[...], preferred_element_type=F32) + br_ref[...]


def _mix(x2, o_fox, y_flat, u, sga, sgb, mod3, d_skip, w_glu, w_fox, w_ssm, w_o, g2, w_r, b_r, seq):
    n, d = x2.shape
    tm = MIX_TILE
    tpb = seq // tm
    tok = lambda w: pl.BlockSpec((tm, w), lambda i: (i, 0))
    const = lambda a: pl.BlockSpec(a.shape, lambda i: (0,) * a.ndim)
    flat = pl.BlockSpec((SSM_GROUPS, tm // SSM_CHUNK, SSM_CHUNK * SSM_GROUP), lambda i: (0, i, 0))
    return pl.pallas_call(
        _mix_kernel,
        grid=(n // tm,),
        in_specs=[tok(d), tok(FOX_WIDTH), flat, tok(SSM_WIDTH), tok(d), tok(d),
                  pl.BlockSpec((None, N_MOD, d), lambda i: (i // tpb, 0, 0)),
                  const(d_skip), const(w_glu), const(w_fox), const(w_ssm), const(w_o), const(g2),
                  const(w_r), const(b_r)],
        out_specs=[tok(d), tok(d), tok(LANES)],
        out_shape=[jax.ShapeDtypeStruct((n, d), F32), jax.ShapeDtypeStruct((n, d), F32),
                   jax.ShapeDtypeStruct((n, LANES), F32)],
        scratch_shapes=[pltpu.VMEM((SSM_WIDTH // LANES, tm, LANES), F32)],
        compiler_params=_params(("arbitrary",)),
        name="mix",
    )(x2, o_fox, y_flat, u, sga, sgb, mod3, d_skip, w_glu, w_fox, w_ssm, w_o, g2, w_r, b_r)


def _route_kernel(lg_ref, tri_ref, idx_ref, wt_ref, cnt_ref, carry_ref):
    i = pl.program_id(0)

    @pl.when(i == 0)
    def _():
        carry_ref[...] = jnp.zeros_like(carry_ref)

    lg = lg_ref[...]
    tm = lg.shape[0]
    lane = lax.broadcasted_iota(jnp.int32, (tm, LANES), 1)
    neg = jnp.full_like(lg, -jnp.inf)

    def first_argmax(vals):
        mx = jnp.max(vals, axis=1, keepdims=True)
        ix = jnp.min(jnp.where(vals == mx, lane, LANES), axis=1, keepdims=True)
        return mx, ix

    is_group = lane < N_GROUPS
    g_max, gi = first_argmax(jnp.where(is_group, lg, neg))
    g_sum = jnp.sum(jnp.where(is_group, jnp.exp(lg - g_max), 0.0), axis=1, keepdims=True)
    p_group = 1.0 / g_sum
    lo = N_GROUPS + EXPERTS_PER_GROUP * gi
    in_group = (lane >= lo) & (lane < lo + EXPERTS_PER_GROUP)
    cand = jnp.where(in_group, lg, neg)
    v1, i1 = first_argmax(cand)
    v2, i2 = first_argmax(jnp.where(lane == i1, neg, cand))
    tt = jnp.exp(v2 - v1)
    w1 = p_group / (1.0 + tt)
    w2 = p_group * tt / (1.0 + tt)
    e1 = i1 - N_GROUPS
    e2 = i2 - N_GROUPS
    sel1 = lane == e1
    sel2 = lane == e2
    onehot = (sel1 | sel2).astype(F32)
    before = jnp.dot(tri_ref[...], onehot.astype(BF16), preferred_element_type=F32) + carry_ref[0:1, :]
    r1 = jnp.sum(jnp.where(sel1, before, 0.0), axis=1, keepdims=True).astype(jnp.int32)
    r2 = jnp.sum(jnp.where(sel2, before, 0.0), axis=1, keepdims=True).astype(jnp.int32)
    total = before[-1:, :] + onehot[-1:, :]
    carry_ref[...] = jnp.broadcast_to(total, carry_ref.shape)
    cnt_ref[...] = jnp.broadcast_to(total, cnt_ref.shape)
    idx_ref[...] = jnp.where(lane == 0, e1, jnp.where(lane == 1, e2, jnp.where(lane == 2, r1, r2)))
    wt_ref[...] = jnp.where(lane == 0, w1, w2)


def _route(logits):
    n = logits.shape[0]
    tm = TOK_TILE
    tri = jnp.tril(jnp.ones((tm, tm), BF16), k=-1)
    tok = pl.BlockSpec((tm, LANES), lambda i: (i, 0))
    return pl.pallas_call(
        _route_kernel,
        grid=(n // tm,),
        in_specs=[tok, pl.BlockSpec((tm, tm), lambda i: (0, 0))],
        out_specs=[tok, tok, pl.BlockSpec((SUBLANES, LANES), lambda i: (0, 0))],
        out_shape=[jax.ShapeDtypeStruct((n, LANES), jnp.int32), jax.ShapeDtypeStruct((n, LANES), F32),
                   jax.ShapeDtypeStruct((SUBLANES, LANES), F32)],
        scratch_shapes=[pltpu.VMEM((SUBLANES, LANES), F32)],
        compiler_params=_params(("arbitrary",)),
        name="route",
    )(logits, tri)


def _row_copy(src_ref, src_row, dst_ref, dst_row, sem):
    return pltpu.make_async_copy(src_ref.at[pl.ds(src_row, 1), :], dst_ref.at[pl.ds(dst_row, 1), :], sem)


ISSUE_UNROLL = 8


def _dispatch_kernel(dest_ref, h_ref, rows_in_ref, rows_ref, sem):
    del rows_in_ref
    tm = h_ref.shape[0]

    def issue(g, c):
        for j in range(ISSUE_UNROLL):
            t = g * ISSUE_UNROLL + j
            _row_copy(h_ref, t, rows_ref, dest_ref[0, 0, t], sem).start(priority=0)
            _row_copy(h_ref, t, rows_ref, dest_ref[0, 0, tm + t], sem).start(priority=1)
        return c

    lax.fori_loop(0, tm // ISSUE_UNROLL, issue, 0)
    for _ in range(2):
        pltpu.make_async_copy(h_ref, rows_ref.at[pl.ds(0, tm), :], sem).wait()


def _dispatch(dest3, h2, rows_zero):
    n, d = h2.shape
    tm = MOVE_TILE
    return pl.pallas_call(
        _dispatch_kernel,
        grid=(n // tm,),
        in_specs=[pl.BlockSpec((1, 1, 2 * tm), lambda i: (i, 0, 0), memory_space=pltpu.SMEM),
                  pl.BlockSpec((tm, d), lambda i: (i, 0)),
                  pl.BlockSpec(memory_space=pl.ANY)],
        out_specs=pl.BlockSpec(memory_space=pl.ANY),
        out_shape=jax.ShapeDtypeStruct(rows_zero.shape, rows_zero.dtype),
        scratch_shapes=[pltpu.SemaphoreType.DMA(())],
        input_output_aliases={2: 0},
        compiler_params=_params(("arbitrary",)),
        name="dispatch",
    )(dest3, h2, rows_zero)


def _combine_kernel(dest_ref, x1_ref, wt_ref, mod_ref, gf_ref, yr_ref, o_ref, buf_ref, sem):
    tm = x1_ref.shape[0]

    def issue(g, c):
        for j in range(ISSUE_UNROLL):
            t = g * ISSUE_UNROLL + j
            _row_copy(yr_ref, dest_ref[0, 0, t], buf_ref.at[0], t, sem).start(priority=0)
            _row_copy(yr_ref, dest_ref[0, 0, tm + t], buf_ref.at[1], t, sem).start(priority=1)
        return c

    lax.fori_loop(0, tm // ISSUE_UNROLL, issue, 0)
    for slot in range(2):
        pltpu.make_async_copy(yr_ref.at[pl.ds(0, tm), :], buf_ref.at[slot], sem).wait()
    wt = wt_ref[...]
    moe = wt[:, 0:1] * buf_ref[0] + wt[:, 1:2] * buf_ref[1]
    x = x1_ref[...] + mod_ref[5:6, :] * moe
    ms = jnp.mean(x * x, axis=-1, keepdims=True)
    o_ref[...] = (x * lax.rsqrt(ms + RMS_EPS)) * gf_ref[...]


def _combine(dest3, x1, wts, mod3, final_g, y_rows, seq):
    n, d = x1.shape
    tm = MOVE_TILE
    tpb = seq // tm
    return pl.pallas_call(
        _combine_kernel,
        grid=(n // tm,),
        in_specs=[pl.BlockSpec((1, 1, 2 * tm), lambda i: (i, 0, 0), memory_space=pltpu.SMEM),
                  pl.BlockSpec((tm, d), lambda i: (i, 0)),
                  pl.BlockSpec((tm, LANES), lambda i: (i, 0)),
                  pl.BlockSpec((None, N_MOD, d), lambda i: (i // tpb, 0, 0)),
                  pl.BlockSpec((1, d), lambda i: (0, 0)),
                  pl.BlockSpec(memory_space=pl.ANY)],
        out_specs=pl.BlockSpec((tm, d), lambda i: (i, 0)),
        out_shape=jax.ShapeDtypeStruct((n, d), F32),
        scratch_shapes=[pltpu.VMEM((2, tm, d), F32), pltpu.SemaphoreType.DMA(())],
        compiler_params=_params(("arbitrary",)),
        name="combine",
    )(dest3, x1, wts, mod3, final_g, y_rows)


def _expert_kernel(be_ref, nv_ref, x_ref, wg_ref, wu_ref, wd_ref, y_ref, wgb_ref, wub_ref, wdb_ref):
    i = pl.program_id(0)
    valid = i < nv_ref[0]
    new_expert = (i == 0) | (be_ref[i] != be_ref[jnp.maximum(i - 1, 0)])

    @pl.when(valid & new_expert)
    def _():
        wgb_ref[...] = wg_ref[...].astype(BF16)
        wub_ref[...] = wu_ref[...].astype(BF16)
        wdb_ref[...] = wd_ref[...].astype(BF16)

    @pl.when(valid)
    def _():
        xb = x_ref[...].astype(BF16)
        a = jnp.dot(xb, wgb_ref[...], preferred_element_type=F32)
        b = jnp.dot(xb, wub_ref[...], preferred_element_type=F32)
        hid = (a * jax.nn.sigmoid(a)) * b
        y_ref[...] = jnp.dot(hid.astype(BF16), wdb_ref[...], preferred_element_type=F32)

    @pl.when(jnp.logical_not(valid))
    def _():
        y_ref[...] = jnp.zeros_like(y_ref)


def _experts(blk_e, n_valid, x_rows, w_gate, w_up, w_down):
    rows, d = x_rows.shape
    tb = ROW_BLOCK
    grid_spec = pltpu.PrefetchScalarGridSpec(
        num_scalar_prefetch=2,
        grid=(rows // tb,),
        in_specs=[pl.BlockSpec((tb, d), lambda i, be, nv: (i, 0)),
                  pl.BlockSpec((None, d, D_EXPERT), lambda i, be, nv: (be[i], 0, 0)),
                  pl.BlockSpec((None, d, D_EXPERT), lambda i, be, nv: (be[i], 0, 0)),
                  pl.BlockSpec((None, D_EXPERT, d), lambda i, be, nv: (be[i], 0, 0))],
        out_specs=pl.BlockSpec((tb, d), lambda i, be, nv: (i, 0)),
        scratch_shapes=[pltpu.VMEM((d, D_EXPERT), BF16), pltpu.VMEM((d, D_EXPERT), BF16),
                        pltpu.VMEM((D_EXPERT, d), BF16)],
    )
    return pl.pallas_call(
        _expert_kernel,
        grid_spec=grid_spec,
        out_shape=jax.ShapeDtypeStruct((rows, d), F32),
        compiler_params=_params(("arbitrary",)),
        name="experts",
    )(blk_e, n_valid, x_rows, w_gate, w_up, w_down)


def kernel(x, c, w_ada, b_ada, norm_mix_g, w_in, b_forget, w_out_fox, lambda_re, lambda_im, log_dt,
           ssm_b_re, ssm_b_im, ssm_c_re, ssm_c_im, d_skip, w_glu, w_out_ssm, w_o, norm_ffn_g,
           w_router_group, b_router_group, w_router_expert, b_router_expert, w_gate_e, w_up_e,
           w_down_e, final_g):
    bsz, seq, d = x.shape
    n = bsz * seq
    assert w_ada.shape[0] == 1, "the final RMSNorm is fused into the (single) layer's combine kernel"
    xc = x.reshape(n, d)
    for l in range(1):
        mod3 = _mod(c, w_ada[l], b_ada[l]).reshape(bsz, N_MOD, d)

        wi = w_in[l]
        s_q, s_k, s_v, s_f, s_u, s_ga = 512, 1024, 1536, 1544, 2056, 3080
        scale = FOX_HEAD_DIM ** -0.5
        w_all = jnp.concatenate(
            [wi[:, :s_q] * scale, wi[:, s_q:s_k], wi[:, s_f:s_u], wi[:, s_u:s_ga],
             wi[:, s_ga:], jnp.pad(wi[:, s_v:s_f], ((0, 0), (0, LANES - FOX_HEADS)))],
            axis=1).astype(BF16)
        w_vt = wi[:, s_k:s_v].T.astype(BF16)
        bf_pad = jnp.pad(b_forget[l], (0, LANES - FOX_HEADS)).reshape(1, LANES)
        q, k, v_t, u, u_flat, sga, sgb = _inproj(xc, mod3, norm_mix_g[l].reshape(1, d), w_all, w_vt,
                                                 bf_pad, seq)

        o_fox = _attention(q, k, v_t, bsz, seq)

        toep, b_state, b_swap, c_pow, a_step = _ssm_matrices(
            lambda_re[l], lambda_im[l], log_dt[l], ssm_b_re[l], ssm_b_im[l], ssm_c_re[l], ssm_c_im[l])
        y_flat = _ssm(u_flat, toep, b_state, b_swap, c_pow, a_step, bsz)

        w_r = jnp.pad(jnp.concatenate([w_router_group[l], w_router_expert[l]], axis=1),
                      ((0, 0), (0, LANES - N_GROUPS - N_EXPERTS)))
        w_r1 = _top_bits(w_r)
        w_r2 = _top_bits(w_r - w_r1)
        w_r = jnp.concatenate([w_r1, w_r2, w_r1], axis=0).astype(BF16)
        b_r = jnp.pad(jnp.concatenate([b_router_group[l], b_router_expert[l]]),
                      (0, LANES - N_GROUPS - N_EXPERTS)).reshape(1, LANES)
        x1, h2, logits = _mix(xc, o_fox, y_flat, u, sga, sgb, mod3, d_skip[l].reshape(1, SSM_WIDTH),
                              w_glu[l].astype(BF16), w_out_fox[l].astype(BF16),
                              w_out_ssm[l].astype(BF16), w_o[l].astype(BF16),
                              norm_ffn_g[l].reshape(1, d), w_r, b_r, seq)

        idx, wts, cnt = _route(logits)
        counts = cnt[0, :N_EXPERTS].astype(jnp.int32)
        pcounts = ((counts + ROW_BLOCK - 1) // ROW_BLOCK) * ROW_BLOCK
        pends = jnp.cumsum(pcounts)
        pstarts = pends - pcounts
        er = idx[:, 0:4].T
        hit = er[0:2, None, :] == jnp.arange(N_EXPERTS, dtype=jnp.int32)[None, :, None]
        dest = jnp.sum(jnp.where(hit, pstarts[None, :, None], 0), axis=1) + er[2:4]
        rows = 2 * n + N_EXPERTS * ROW_BLOCK
        n_blocks = rows // ROW_BLOCK
        blk_start = jnp.arange(n_blocks, dtype=jnp.int32) * ROW_BLOCK
        blk_e = jnp.minimum(jnp.sum((pends[None, :] <= blk_start[:, None]).astype(jnp.int32), axis=1),
                            N_EXPERTS - 1)
        n_valid = (pends[-1:] // ROW_BLOCK).astype(jnp.int32)
        dest3 = (dest.astype(jnp.int32).reshape(2, n // MOVE_TILE, MOVE_TILE).transpose(1, 0, 2)
                 .reshape(n // MOVE_TILE, 1, 2 * MOVE_TILE))

        x_rows = _dispatch(dest3, h2, jnp.zeros((rows, d), F32))
        y_rows = _experts(blk_e, n_valid, x_rows, w_gate_e[l], w_up_e[l], w_down_e[l])
        xc = _combine(dest3, x1, wts, mod3, final_g.reshape(1, d), y_rows, seq)
    return xc.reshape(bsz, seq, d)
```

```python
import functools
import math

import jax
import jax.numpy as jnp
import numpy as np
from jax import lax
from jax.experimental import pallas as pl
from jax.experimental.pallas import tpu as pltpu

F32 = jnp.float32
BF16 = jnp.bfloat16

D_MODEL = 1024
N_MOD = 6
RMS_EPS = 1e-6
FOX_HEADS = 8
FOX_HEAD_DIM = 64
FOX_WIDTH = FOX_HEADS * FOX_HEAD_DIM
HEAD_PAIRS = FOX_HEADS // 2
SSM_WIDTH = 512
SSM_GROUP = 16
SSM_GROUPS = SSM_WIDTH // SSM_GROUP
SSM_STATE = 64
LAMBDA_RE_MAX = -1e-4
N_GROUPS = 4
EXPERTS_PER_GROUP = 8
N_EXPERTS = N_GROUPS * EXPERTS_PER_GROUP
D_EXPERT = 512

LANES = 128
SUBLANES = 8
VMEM_LIMIT = 56 * 1024 * 1024

SSM_CHUNK = 16
TOK_TILE = 512
MIX_TILE = 256
ATT_Q_TILE = 512
ATT_K_TILE = 256
ROW_BLOCK = 256
MOVE_TILE = 256
NEG_BIG = -1e30

HIGHEST = lax.Precision.HIGHEST


def _params(sem):
    return pltpu.CompilerParams(dimension_semantics=sem, vmem_limit_bytes=VMEM_LIMIT)


def _rms_modulate(x, gain, shift, scale):
    ms = jnp.mean(x * x, axis=-1, keepdims=True)
    return (x * lax.rsqrt(ms + RMS_EPS)) * gain * (1.0 + scale) + shift


def _mod_kernel(c_ref, w_ref, b_ref, o_ref):
    c = c_ref[...]
    ca = (c * jax.nn.sigmoid(c)).astype(BF16)
    o_ref[...] = jnp.dot(ca, w_ref[...].astype(BF16), preferred_element_type=F32) + b_ref[...]


def _mod(c, w_ada, b_ada):
    bsz, d = c.shape
    cols = w_ada.shape[1]
    tn = 1536
    return pl.pallas_call(
        _mod_kernel,
        grid=(cols // tn,),
        in_specs=[pl.BlockSpec((bsz, d), lambda j: (0, 0)),
                  pl.BlockSpec((d, tn), lambda j: (0, j)),
                  pl.BlockSpec((1, tn), lambda j: (0, j))],
        out_specs=pl.BlockSpec((bsz, tn), lambda j: (0, j)),
        out_shape=jax.ShapeDtypeStruct((bsz, cols), F32),
        compiler_params=_params(("arbitrary",)),
        name="mod",
    )(c, w_ada, b_ada.reshape(1, cols))


_C_Q, _C_K, _C_U, _C_GA, _C_GB, _C_F, _C_END = 0, 512, 1024, 1536, 2560, 3584, 3712


def _lane_block():
    return lax.broadcasted_iota(jnp.int32, (1, LANES), 1) // SSM_GROUP


def _to_group_major(tok_ref, flat_ref, rows):
    blk = _lane_block()
    for half in range(2):
        for j in range(SSM_WIDTH // LANES):
            w = []
            for s8 in range(8):
                v = tok_ref[j, pl.ds(8 * half + s8, rows, stride=SSM_CHUNK), :]
                w.append(pltpu.roll(v, s8 * SSM_GROUP, axis=1) if s8 else v)
            for p in range(8):
                acc = w[0]
                for s8 in range(1, 8):
                    acc = jnp.where(blk == (p + s8) % 8, w[s8], acc)
                flat_ref[8 * j + p, :, half * LANES:(half + 1) * LANES] = acc.astype(flat_ref.dtype)


def _to_token_major(flat_ref, tok_ref, rows):
    blk = _lane_block()
    for half in range(2):
        for j in range(SSM_WIDTH // LANES):
            ys = [flat_ref[8 * j + p, :, half * LANES:(half + 1) * LANES] for p in range(8)]
            for s8 in range(8):
                acc = ys[0]
                for p in range(1, 8):
                    acc = jnp.where(blk == (p + s8) % 8, ys[p], acc)
                if s8:
                    acc = pltpu.roll(acc, LANES - s8 * SSM_GROUP, axis=1)
                tok_ref[j, pl.ds(8 * half + s8, rows, stride=SSM_CHUNK), :] = acc


def _rotate_steps(m, axis):
    shape = m.shape
    m = m.reshape((SSM_GROUPS // 8, 8) + shape[1:])
    ax = axis + 1
    m = m.reshape(m.shape[:ax] + (2, 8) + m.shape[ax + 1:])
    m = jnp.stack([jnp.roll(m[:, p], p, axis=ax) for p in range(8)], axis=1)
    return m.reshape(shape)


def _roll_lane_steps(m):
    shape = m.shape
    m = m.reshape((SSM_GROUPS // 8, 8) + shape[1:-1] + (2, LANES))
    m = jnp.stack([jnp.roll(m[:, p], p * SSM_GROUP, axis=-1) for p in range(8)], axis=1)
    return m.reshape(shape)


def _bias_lane_placement():
    pq = np.zeros((3 * LANES, HEAD_PAIRS * LANES), np.float32)
    pk = np.zeros((3 * LANES, HEAD_PAIRS * LANES), np.float32)
    bq = np.zeros((1, HEAD_PAIRS * LANES), np.float32)
    bk = np.zeros((1, HEAD_PAIRS * LANES), np.float32)
    for head in range(FOX_HEADS):
        base = (head // 2) * LANES + (head % 2) * 8
        for term in range(3):
            pq[term * LANES + head, base + term] = 1.0
            pk[term * LANES + head, base + 3 + term] = -1.0
            bq[0, base + 3 + term] = 1.0
            bk[0, base + term] = 1.0
    return pq, pk, bq, bk


def _top_bits(a):
    bits = lax.bitcast_convert_type(a, jnp.uint32) & jnp.uint32(0xFFFF0000)
    return lax.bitcast_convert_type(bits, F32)


def _inproj_kernel(tiles_per_batch, x_ref, mod_ref, g_ref, w_ref, wvt_ref, bf_ref, tri_ref,
                   pq_ref, pk_ref, bq_ref, bk_ref,
                   q_ref, k_ref, vt_ref, u_ref, uflat_ref, ga_ref, gb_ref, carry_ref, uslab_ref):
    i = pl.program_id(0)
    h = _rms_modulate(x_ref[...], g_ref[...], mod_ref[0:1, :], mod_ref[1:2, :])
    hb = h.astype(BF16)

    def proj(a, b):
        return jnp.dot(hb, w_ref[:, a:b], preferred_element_type=F32)

    q = proj(_C_Q, _C_K).astype(BF16)
    k = proj(_C_K, _C_U).astype(BF16)
    vt_ref[...] = lax.dot_general(wvt_ref[...], hb, (((1,), (1,)), ((), ())),
                                  preferred_element_type=F32).astype(BF16)
    u = proj(_C_U, _C_GA)
    u_ref[...] = u
    for j in range(SSM_WIDTH // LANES):
        uslab_ref[j] = u[:, j * LANES:(j + 1) * LANES]
    _to_group_major(uslab_ref, uflat_ref, u.shape[0] // SSM_CHUNK)
    ga_ref[...] = jax.nn.sigmoid(proj(_C_GA, _C_GB)).astype(BF16)
    gb_ref[...] = jax.nn.sigmoid(proj(_C_GB, _C_F)).astype(BF16)

    f = proj(_C_F, _C_END) + bf_ref[...]
    logf = jnp.minimum(f, 0.0) - jnp.log(1.0 + jnp.exp(-jnp.abs(f)))

    @pl.when(i % tiles_per_batch == 0)
    def _():
        carry_ref[...] = jnp.zeros_like(carry_ref)

    def split3(a):
        hi = _top_bits(a)
        r1 = a - hi
        mid = _top_bits(r1)
        return jnp.concatenate([hi, mid, _top_bits(r1 - mid)], axis=1).astype(BF16)

    part = jnp.dot(tri_ref[...], split3(logf), preferred_element_type=F32)
    cs = (part[:, :LANES] + part[:, LANES:2 * LANES] + part[:, 2 * LANES:]) + carry_ref[0:1, :]
    carry_ref[...] = jnp.broadcast_to(cs[-1:, :], carry_ref.shape)

    terms = split3(cs)
    bias_q = (jnp.dot(terms, pq_ref[...], preferred_element_type=F32) + bq_ref[...]).astype(BF16)
    bias_k = (jnp.dot(terms, pk_ref[...], preferred_element_type=F32) + bk_ref[...]).astype(BF16)
    for p in range(HEAD_PAIRS):
        lanes = slice(p * LANES, (p + 1) * LANES)
        q_ref[:, 2 * p * LANES:(2 * p + 1) * LANES] = q[:, lanes]
        q_ref[:, (2 * p + 1) * LANES:(2 * p + 2) * LANES] = bias_q[:, lanes]
        k_ref[:, 2 * p * LANES:(2 * p + 1) * LANES] = k[:, lanes]
        k_ref[:, (2 * p + 1) * LANES:(2 * p + 2) * LANES] = bias_k[:, lanes]


def _inproj(x2, mod3, gain, w_all, w_vt, bf_pad, seq):
    n, d = x2.shape
    tm = TOK_TILE
    tpb = seq // tm
    tri = jnp.tril(jnp.ones((tm, tm), BF16))
    pq, pk, bq, bk = _bias_lane_placement()
    tok = lambda w: pl.BlockSpec((tm, w), lambda i: (i, 0))
    const = lambda shape: pl.BlockSpec(shape, lambda i: (0,) * len(shape))
    qk_width = 2 * FOX_WIDTH
    return pl.pallas_call(
        functools.partial(_inproj_kernel, tpb),
        grid=(n // tm,),
        in_specs=[tok(d),
                  pl.BlockSpec((None, N_MOD, d), lambda i: (i // tpb, 0, 0)),
                  const((1, d)), const((d, _C_END)), const((FOX_WIDTH, d)), const((1, LANES)),
                  const((tm, tm)), const(pq.shape), const(pk.shape), const(bq.shape), const(bk.shape)],
        out_specs=[tok(qk_width), tok(qk_width), pl.BlockSpec((FOX_WIDTH, tm), lambda i: (0, i)),
                   tok(SSM_WIDTH),
                   pl.BlockSpec((SSM_GROUPS, tm // SSM_CHUNK, SSM_CHUNK * SSM_GROUP), lambda i: (0, i, 0)),
                   tok(d), tok(d)],
        out_shape=[jax.ShapeDtypeStruct((n, qk_width), BF16)] * 2
        + [jax.ShapeDtypeStruct((FOX_WIDTH, n), BF16)]
        + [jax.ShapeDtypeStruct((n, SSM_WIDTH), F32)]
        + [jax.ShapeDtypeStruct((SSM_GROUPS, n // SSM_CHUNK, SSM_CHUNK * SSM_GROUP), BF16)]
        + [jax.ShapeDtypeStruct((n, d), BF16)] * 2,
        scratch_shapes=[pltpu.VMEM((SUBLANES, LANES), F32),
                        pltpu.VMEM((SSM_WIDTH // LANES, tm, LANES), F32)],
        compiler_params=_params(("arbitrary",)),
        name="inproj",
    )(x2, mod3, gain, w_all, w_vt, bf_pad, tri, jnp.asarray(pq, BF16), jnp.asarray(pk, BF16),
      jnp.asarray(bq), jnp.asarray(bk))


def _attn_kernel(q_ref, k_ref, vt_ref, o_ref, m_ref, acc_ref):
    i = pl.program_id(2)
    tq, tk = ATT_Q_TILE, ATT_K_TILE
    q = q_ref[...]
    lane = lax.broadcasted_iota(jnp.int32, (1, 2 * LANES), 1)
    zq = jnp.zeros_like(q)
    half = FOX_HEAD_DIM
    own0 = (lane < half) | ((lane >= LANES) & (lane < LANES + 8))
    own1 = ((lane >= half) & (lane < LANES)) | ((lane >= LANES + 8) & (lane < LANES + 16))
    q_both = jnp.concatenate([jnp.where(own0, q, zq), jnp.where(own1, q, zq)], axis=0)
    m_ref[...] = jnp.full(m_ref.shape, NEG_BIG, F32)
    acc_ref[...] = jnp.zeros(acc_ref.shape, F32)
    ones_rows = jnp.ones((2 * SUBLANES, tk), BF16)
    key_in_tile = lax.broadcasted_iota(jnp.int32, (tk, 2 * tq), 0)
    qry_pos = i * tq + (lax.broadcasted_iota(jnp.int32, (tk, 2 * tq), 1) & (tq - 1))

    def scores(j):
        start = pl.multiple_of(j * tk, tk)
        s = lax.dot_general(k_ref[pl.ds(start, tk), :], q_both, (((1,), (1,)), ((), ())),
                            preferred_element_type=F32)
        return jnp.where(key_in_tile + j * tk <= qry_pos, s, NEG_BIG)

    def accumulate(j, s):
        start = pl.multiple_of(j * tk, tk)
        va = jnp.concatenate([vt_ref[:, pl.ds(start, tk)], ones_rows], axis=0)
        m_old = m_ref[...]
        m_new = jnp.maximum(m_old, jnp.max(s, axis=0, keepdims=True))
        alpha = jnp.exp(m_old - m_new)
        p = jnp.exp(s - m_new).astype(BF16)
        acc_ref[...] = alpha * acc_ref[...] + jnp.dot(va, p, preferred_element_type=F32)
        m_ref[...] = m_new

    def body(j, s_cur):
        s_next = scores(j + 1)
        accumulate(j, s_cur)
        return s_next

    last = (i + 1) * (tq // tk) - 1
    s_last = lax.fori_loop(0, last, body, scores(0))
    accumulate(last, s_last)

    acc = acc_ref[...]
    o_t = jnp.concatenate([acc[0:half, 0:tq] / acc[LANES:LANES + 1, 0:tq],
                           acc[half:LANES, tq:2 * tq] / acc[LANES:LANES + 1, tq:2 * tq]], axis=0)
    o_ref[...] = o_t.T.astype(o_ref.dtype)


def _attention(q, k, v_t, bsz, seq):
    n = q.shape[0]
    t = ATT_Q_TILE
    nq = seq // t
    return pl.pallas_call(
        _attn_kernel,
        grid=(bsz, HEAD_PAIRS, nq),
        in_specs=[pl.BlockSpec((t, 2 * LANES), lambda b, p, i: (b * nq + i, p)),
                  pl.BlockSpec((seq, 2 * LANES), lambda b, p, i: (b, p)),
                  pl.BlockSpec((LANES, seq), lambda b, p, i: (p, b))],
        out_specs=pl.BlockSpec((t, LANES), lambda b, p, i: (b * nq + i, p)),
        out_shape=jax.ShapeDtypeStruct((n, FOX_WIDTH), BF16),
        scratch_shapes=[pltpu.VMEM((1, 2 * t), F32), pltpu.VMEM((LANES + 2 * SUBLANES, 2 * t), F32)],
        compiler_params=_params(("arbitrary", "arbitrary", "arbitrary")),
        name="attn",
    )(q, k, v_t)


def _ssm_matrices(lambda_re, lambda_im, log_dt, b_re, b_im, c_re, c_im):
    t_len = SSM_CHUNK
    lam_re = jnp.minimum(lambda_re.astype(F32), LAMBDA_RE_MAX)
    lam_im = lambda_im.astype(F32)
    dt = jnp.exp(log_dt.astype(F32))[:, None]
    mag = jnp.exp(lam_re * dt)
    a_re = mag * jnp.cos(lam_im * dt)
    a_im = mag * jnp.sin(lam_im * dt)
    den = lam_re * lam_re + lam_im * lam_im
    nr = a_re - 1.0
    co_re = (nr * lam_re + a_im * lam_im) / den
    co_im = (a_im * lam_re - nr * lam_im) / den
    b_re = b_re.astype(F32)
    b_im = b_im.astype(F32)
    bb_re = co_re[..., None] * b_re - co_im[..., None] * b_im
    bb_im = co_re[..., None] * b_im + co_im[..., None] * b_re
    ls = jnp.arange(t_len + 1, dtype=F32)[:, None, None]
    pmag = jnp.exp(ls * (lam_re * dt)[None])
    pang = ls * (lam_im * dt)[None]
    pw_re = pmag * jnp.cos(pang)
    pw_im = pmag * jnp.sin(pang)
    c_re = c_re.astype(F32)
    c_im = c_im.astype(F32)
    width = t_len * SSM_GROUP
    w_re = c_re[None] * pw_re[:t_len, :, None, :] - c_im[None] * pw_im[:t_len, :, None, :]
    w_im = c_re[None] * pw_im[:t_len, :, None, :] + c_im[None] * pw_re[:t_len, :, None, :]
    kern = (jnp.einsum('lgdp,gpc->gcld', w_re, bb_re, precision=HIGHEST)
            - jnp.einsum('lgdp,gpc->gcld', w_im, bb_im, precision=HIGHEST))
    kern = kern.reshape(SSM_GROUPS, SSM_GROUP, width)
    toep = jnp.stack([jnp.pad(kern[:, :, :width - s * SSM_GROUP], ((0, 0), (0, 0), (s * SSM_GROUP, 0)))
                      for s in range(t_len)], axis=1)
    toep = _roll_lane_steps(_rotate_steps(toep, 1)).reshape(SSM_GROUPS, width, width)
    e_re = pw_re[:t_len][::-1].transpose(1, 0, 2)[:, :, None, :]
    e_im = pw_im[:t_len][::-1].transpose(1, 0, 2)[:, :, None, :]
    bt_re = bb_re.transpose(0, 2, 1)[:, None]
    bt_im = bb_im.transpose(0, 2, 1)[:, None]
    bs_re = e_re * bt_re - e_im * bt_im
    bs_im = e_re * bt_im + e_im * bt_re

    def lay_state(first, second):
        m = jnp.concatenate([first, second], axis=-1)
        return _rotate_steps(m, 1).reshape(SSM_GROUPS, width, 2 * SSM_STATE)

    b_state = lay_state(bs_re, bs_im)
    b_swap = lay_state(bs_im, bs_re)
    ct_re = jnp.tile(c_re.transpose(0, 2, 1), (1, 1, t_len))
    ct_im = jnp.tile(c_im.transpose(0, 2, 1), (1, 1, t_len))
    pt_re = jnp.repeat(pw_re[1:].transpose(1, 2, 0), SSM_GROUP, axis=-1)
    pt_im = jnp.repeat(pw_im[1:].transpose(1, 2, 0), SSM_GROUP, axis=-1)
    c_pow = _roll_lane_steps(jnp.concatenate([ct_re * pt_re - ct_im * pt_im,
                                              -(ct_re * pt_im + ct_im * pt_re)], axis=1))
    a1 = jnp.concatenate([pw_re[t_len], pw_re[t_len]], axis=-1)
    a2 = jnp.concatenate([-pw_im[t_len], pw_im[t_len]], axis=-1)
    a_step = jnp.stack([a1, a2], axis=1)
    return toep.astype(BF16), b_state.astype(BF16), b_swap.astype(BF16), c_pow.astype(BF16), a_step


def _ssm_kernel(n_chunks, bsz, u_ref, toep_ref, bst_ref, bsw_ref, cpw_ref, a_ref, y_ref,
                contrib_ref, cswap_ref, xprev_ref):
    u = u_ref[...]
    contrib_ref[...] = jnp.dot(u, bst_ref[...], preferred_element_type=F32)
    cswap_ref[...] = jnp.dot(u, bsw_ref[...], preferred_element_type=F32)
    a1 = a_ref[0:1, :]
    a2 = a_ref[1:2, :]

    def step(n, carry):
        x, xs = carry
        rows = pl.ds(n, bsz, stride=n_chunks)
        xprev_ref[rows, :] = x
        x_new = a1 * x + a2 * xs + contrib_ref[rows, :]
        xs_new = a1 * xs - a2 * x + cswap_ref[rows, :]
        return x_new, xs_new

    zero = jnp.zeros((bsz, 2 * SSM_STATE), F32)
    lax.fori_loop(0, n_chunks, step, (zero, zero), unroll=4)
    y_ref[...] = (jnp.dot(u, toep_ref[...], preferred_element_type=F32)
                  + jnp.dot(xprev_ref[...].astype(BF16), cpw_ref[...], preferred_element_type=F32))


def _ssm(u_flat, toep, b_state, b_swap, c_pow, a_step, bsz):
    g, rows, w = u_flat.shape
    per = lambda a, b: pl.BlockSpec((None, a, b), lambda i: (i, 0, 0))
    state = pltpu.VMEM((rows, 2 * SSM_STATE), F32)
    return pl.pallas_call(
        functools.partial(_ssm_kernel, rows // bsz, bsz),
        grid=(g,),
        in_specs=[per(rows, w), per(w, w), per(w, 2 * SSM_STATE), per(w, 2 * SSM_STATE),
                  per(2 * SSM_STATE, w), per(2, 2 * SSM_STATE)],
        out_specs=per(rows, w),
        out_shape=jax.ShapeDtypeStruct((g, rows, w), F32),
        scratch_shapes=[state, state, state],
        compiler_params=_params(("arbitrary",)),
        name="ssm",
    )(u_flat, toep, b_state, b_swap, c_pow, a_step)


ROW_SLABS = D_MODEL // LANES


def _store_row_tiles(ref, value):
    rows = value.shape[0]
    for k in range(ROW_SLABS):
        ref[pl.ds(k, rows, stride=ROW_SLABS), :] = value[:, k * LANES:(k + 1) * LANES]


def _load_row_tiles(ref, rows):
    return jnp.concatenate([ref[pl.ds(k, rows, stride=ROW_SLABS), :] for k in range(ROW_SLABS)], axis=1)


def _row_tile_copy(src_ref, src_row, dst_ref, dst_row, sem):
    src = src_ref.at[pl.ds(pl.multiple_of(src_row * ROW_SLABS, ROW_SLABS), ROW_SLABS), :]
    dst = dst_ref.at[pl.ds(pl.multiple_of(dst_row * ROW_SLABS, ROW_SLABS), ROW_SLABS), :]
    return pltpu.make_async_copy(src, dst, sem)


def _mix_kernel(x_ref, of_ref, yf_ref, u_ref, ga_ref, gb_ref, mod_ref, dsk_ref, wglu_ref, wfox_ref,
                wssm_ref, wo_ref, g2_ref, wr_ref, br_ref, x1_ref, h2_ref, lg_ref, ytok_ref):
    _to_token_major(yf_ref, ytok_ref, yf_ref.shape[1])
    y_ssm = jnp.concatenate([ytok_ref[j] for j in range(SSM_WIDTH // LANES)], axis=1)
    y = y_ssm + dsk_ref[...] * u_ref[...]
    y = 0.5 * y * (1.0 + jnp.tanh(math.sqrt(2.0 / math.pi) * (y + 0.044715 * (y * y * y))))
    gl = jnp.dot(y.astype(BF16), wglu_ref[...], preferred_element_type=F32)
    o_ssm = gl[:, :SSM_WIDTH] * jax.nn.sigmoid(gl[:, SSM_WIDTH:])
    merged = (ga_ref[...].astype(F32) * jnp.dot(of_ref[...], wfox_ref[...], preferred_element_type=F32)
              + gb_ref[...].astype(F32) * jnp.dot(o_ssm.astype(BF16), wssm_ref[...],
                                                  preferred_element_type=F32))
    x1 = x_ref[...] + mod_ref[2:3, :] * jnp.dot(merged.astype(BF16), wo_ref[...],
                                                 preferred_element_type=F32)
    x1_ref[...] = x1
    h2 = _rms_modulate(x1, g2_ref[...], mod_ref[3:4, :], mod_ref[4:5, :])
    _store_row_tiles(h2_ref, h2)
    a1 = _top_bits(h2)
    a2 = _top_bits(h2 - a1)
    lhs = jnp.concatenate([a1, a1, a2], axis=1).astype(BF16)
    lg_ref[...] = jnp.dot(lhs, wr_ref[...], preferred_element_type=F32) + br_ref[...]


def _mix(x2, o_fox, y_flat, u, sga, sgb, mod3, d_skip, w_glu, w_fox, w_ssm, w_o, g2, w_r, b_r, seq):
    n, d = x2.shape
    tm = MIX_TILE
    tpb = seq // tm
    tok = lambda w: pl.BlockSpec((tm, w), lambda i: (i, 0))
    const = lambda a: pl.BlockSpec(a.shape, lambda i: (0,) * a.ndim)
    flat = pl.BlockSpec((SSM_GROUPS, tm // SSM_CHUNK, SSM_CHUNK * SSM_GROUP), lambda i: (0, i, 0))
    return pl.pallas_call(
        _mix_kernel,
        grid=(n // tm,),
        in_specs=[tok(d), tok(FOX_WIDTH), flat, tok(SSM_WIDTH), tok(d), tok(d),
                  pl.BlockSpec((None, N_MOD, d), lambda i: (i // tpb, 0, 0)),
                  const(d_skip), const(w_glu), const(w_fox), const(w_ssm), const(w_o), const(g2),
                  const(w_r), const(b_r)],
        out_specs=[tok(d), pl.BlockSpec((tm * ROW_SLABS, LANES), lambda i: (i, 0)), tok(LANES)],
        out_shape=[jax.ShapeDtypeStruct((n, d), F32), jax.ShapeDtypeStruct((n * ROW_SLABS, LANES), F32),
                   jax.ShapeDtypeStruct((n, LANES), F32)],
        scratch_shapes=[pltpu.VMEM((SSM_WIDTH // LANES, tm, LANES), F32)],
        compiler_params=_params(("arbitrary",)),
        name="mix",
    )(x2, o_fox, y_flat, u, sga, sgb, mod3, d_skip, w_glu, w_fox, w_ssm, w_o, g2, w_r, b_r)


def _route_kernel(lg_ref, tri_ref, idx_ref, wt_ref, cnt_ref, carry_ref):
    i = pl.program_id(0)

    @pl.when(i == 0)
    def _():
        carry_ref[...] = jnp.zeros_like(carry_ref)

    lg = lg_ref[...]
    tm = lg.shape[0]
    lane = lax.broadcasted_iota(jnp.int32, (tm, LANES), 1)
    neg = jnp.full_like(lg, -jnp.inf)

    def first_argmax(vals):
        mx = jnp.max(vals, axis=1, keepdims=True)
        ix = jnp.min(jnp.where(vals == mx, lane, LANES), axis=1, keepdims=True)
        return mx, ix

    is_group = lane < N_GROUPS
    g_max, gi = first_argmax(jnp.where(is_group, lg, neg))
    g_sum = jnp.sum(jnp.where(is_group, jnp.exp(lg - g_max), 0.0), axis=1, keepdims=True)
    p_group = 1.0 / g_sum
    lo = N_GROUPS + EXPERTS_PER_GROUP * gi
    in_group = (lane >= lo) & (lane < lo + EXPERTS_PER_GROUP)
    cand = jnp.where(in_group, lg, neg)
    v1, i1 = first_argmax(cand)
    v2, i2 = first_argmax(jnp.where(lane == i1, neg, cand))
    tt = jnp.exp(v2 - v1)
    w1 = p_group / (1.0 + tt)
    w2 = p_group * tt / (1.0 + tt)
    e1 = i1 - N_GROUPS
    e2 = i2 - N_GROUPS
    sel1 = lane == e1
    sel2 = lane == e2
    onehot = (sel1 | sel2).astype(F32)
    before = jnp.dot(tri_ref[...], onehot.astype(BF16), preferred_element_type=F32) + carry_ref[0:1, :]
    r1 = jnp.sum(jnp.where(sel1, before, 0.0), axis=1, keepdims=True).astype(jnp.int32)
    r2 = jnp.sum(jnp.where(sel2, before, 0.0), axis=1, keepdims=True).astype(jnp.int32)
    total = before[-1:, :] + onehot[-1:, :]
    carry_ref[...] = jnp.broadcast_to(total, carry_ref.shape)
    cnt_ref[...] = jnp.broadcast_to(total, cnt_ref.shape)
    idx_ref[...] = jnp.where(lane == 0, e1, jnp.where(lane == 1, e2, jnp.where(lane == 2, r1, r2)))
    wt_ref[...] = jnp.where(lane == 0, w1, w2)


def _route(logits):
    n = logits.shape[0]
    tm = TOK_TILE
    tri = jnp.tril(jnp.ones((tm, tm), BF16), k=-1)
    tok = pl.BlockSpec((tm, LANES), lambda i: (i, 0))
    return pl.pallas_call(
        _route_kernel,
        grid=(n // tm,),
        in_specs=[tok, pl.BlockSpec((tm, tm), lambda i: (0, 0))],
        out_specs=[tok, tok, pl.BlockSpec((SUBLANES, LANES), lambda i: (0, 0))],
        out_shape=[jax.ShapeDtypeStruct((n, LANES), jnp.int32), jax.ShapeDtypeStruct((n, LANES), F32),
                   jax.ShapeDtypeStruct((SUBLANES, LANES), F32)],
        scratch_shapes=[pltpu.VMEM((SUBLANES, LANES), F32)],
        compiler_params=_params(("arbitrary",)),
        name="route",
    )(logits, tri)


ISSUE_UNROLL = 8


def _dispatch_kernel(dest_ref, h_ref, rows_in_ref, rows_ref, sem):
    del rows_in_ref
    tm = h_ref.shape[0] // ROW_SLABS

    def issue(g, c):
        for j in range(ISSUE_UNROLL):
            t = g * ISSUE_UNROLL + j
            _row_tile_copy(h_ref, t, rows_ref, dest_ref[0, 0, t], sem).start(priority=0)
            _row_tile_copy(h_ref, t, rows_ref, dest_ref[0, 0, tm + t], sem).start(priority=1)
        return c

    lax.fori_loop(0, tm // ISSUE_UNROLL, issue, 0)
    for _ in range(2):
        pltpu.make_async_copy(h_ref, rows_ref.at[pl.ds(0, tm * ROW_SLABS), :], sem).wait()


def _dispatch(dest3, h2_tiles, rows_zero):
    tm = MOVE_TILE
    n = h2_tiles.shape[0] // ROW_SLABS
    return pl.pallas_call(
        _dispatch_kernel,
        grid=(n // tm,),
        in_specs=[pl.BlockSpec((1, 1, 2 * tm), lambda i: (i, 0, 0), memory_space=pltpu.SMEM),
                  pl.BlockSpec((tm * ROW_SLABS, LANES), lambda i: (i, 0)),
                  pl.BlockSpec(memory_space=pl.ANY)],
        out_specs=pl.BlockSpec(memory_space=pl.ANY),
        out_shape=jax.ShapeDtypeStruct(rows_zero.shape, rows_zero.dtype),
        scratch_shapes=[pltpu.SemaphoreType.DMA(())],
        input_output_aliases={2: 0},
        compiler_params=_params(("arbitrary",)),
        name="dispatch",
    )(dest3, h2_tiles, rows_zero)


def _combine_kernel(n_steps, dest_ref, dnext_ref, x1_ref, wt_ref, mod_ref, gf_ref, yr_ref, o_ref,
                    buf_ref, sem):
    i = pl.program_id(0)
    tm = x1_ref.shape[0]

    def gather(idx_ref, which):
        def issue(g, c):
            for j in range(ISSUE_UNROLL):
                t = g * ISSUE_UNROLL + j
                _row_tile_copy(yr_ref, idx_ref[0, 0, t], buf_ref.at[which, 0], t,
                               sem.at[which]).start(priority=0)
                _row_tile_copy(yr_ref, idx_ref[0, 0, tm + t], buf_ref.at[which, 1], t,
                               sem.at[which]).start(priority=1)
            return c

        lax.fori_loop(0, tm // ISSUE_UNROLL, issue, 0)

    cur = i % 2

    @pl.when(i == 0)
    def _():
        gather(dest_ref, 0)

    @pl.when(i + 1 < n_steps)
    def _():
        gather(dnext_ref, 1 - cur)

    for slot in range(2):
        pltpu.make_async_copy(yr_ref.at[pl.ds(0, tm * ROW_SLABS), :], buf_ref.at[cur, slot],
                              sem.at[cur]).wait()
    wt = wt_ref[...]
    moe = (wt[:, 0:1] * _load_row_tiles(buf_ref.at[cur, 0], tm)
           + wt[:, 1:2] * _load_row_tiles(buf_ref.at[cur, 1], tm))
    x = x1_ref[...] + mod_ref[5:6, :] * moe
    ms = jnp.mean(x * x, axis=-1, keepdims=True)
    o_ref[...] = (x * lax.rsqrt(ms + RMS_EPS)) * gf_ref[...]


def _combine(dest3, x1, wts, mod3, final_g, y_rows, seq):
    n, d = x1.shape
    tm = MOVE_TILE
    tpb = seq // tm
    n_steps = n // tm
    idx_spec = lambda f: pl.BlockSpec((1, 1, 2 * tm), f, memory_space=pltpu.SMEM)
    return pl.pallas_call(
        functools.partial(_combine_kernel, n_steps),
        grid=(n_steps,),
        in_specs=[idx_spec(lambda i: (i, 0, 0)),
                  idx_spec(lambda i: (jnp.minimum(i + 1, n_steps - 1), 0, 0)),
                  pl.BlockSpec((tm, d), lambda i: (i, 0)),
                  pl.BlockSpec((tm, LANES), lambda i: (i, 0)),
                  pl.BlockSpec((None, N_MOD, d), lambda i: (i // tpb, 0, 0)),
                  pl.BlockSpec((1, d), lambda i: (0, 0)),
                  pl.BlockSpec(memory_space=pl.ANY)],
        out_specs=pl.BlockSpec((tm, d), lambda i: (i, 0)),
        out_shape=jax.ShapeDtypeStruct((n, d), F32),
        scratch_shapes=[pltpu.VMEM((2, 2, tm * ROW_SLABS, LANES), F32), pltpu.SemaphoreType.DMA((2,))],
        compiler_params=_params(("arbitrary",)),
        name="combine",
    )(dest3, dest3, x1, wts, mod3, final_g, y_rows)


def _expert_kernel(be_ref, nv_ref, x_ref, wg_ref, wu_ref, wd_ref, y_ref, wgb_ref, wub_ref, wdb_ref):
    i = pl.program_id(0)
    valid = i < nv_ref[0]
    new_expert = (i == 0) | (be_ref[i] != be_ref[jnp.maximum(i - 1, 0)])

    @pl.when(valid & new_expert)
    def _():
        wgb_ref[...] = wg_ref[...].astype(BF16)
        wub_ref[...] = wu_ref[...].astype(BF16)
        wdb_ref[...] = wd_ref[...].astype(BF16)

    @pl.when(valid)
    def _():
        xb = _load_row_tiles(x_ref, x_ref.shape[0] // ROW_SLABS).astype(BF16)
        a = jnp.dot(xb, wgb_ref[...], preferred_element_type=F32)
        b = jnp.dot(xb, wub_ref[...], preferred_element_type=F32)
        hid = (a * jax.nn.sigmoid(a)) * b
        _store_row_tiles(y_ref, jnp.dot(hid.astype(BF16), wdb_ref[...], preferred_element_type=F32))

    @pl.when(jnp.logical_not(valid))
    def _():
        y_ref[...] = jnp.zeros_like(y_ref)


def _experts(blk_e, n_valid, x_rows, w_gate, w_up, w_down):
    d = D_MODEL
    rows = x_rows.shape[0] // ROW_SLABS
    tb = ROW_BLOCK
    grid_spec = pltpu.PrefetchScalarGridSpec(
        num_scalar_prefetch=2,
        grid=(rows // tb,),
        in_specs=[pl.BlockSpec((tb * ROW_SLABS, LANES), lambda i, be, nv: (i, 0)),
                  pl.BlockSpec((None, d, D_EXPERT), lambda i, be, nv: (be[i], 0, 0)),
                  pl.BlockSpec((None, d, D_EXPERT), lambda i, be, nv: (be[i], 0, 0)),
                  pl.BlockSpec((None, D_EXPERT, d), lambda i, be, nv: (be[i], 0, 0))],
        out_specs=pl.BlockSpec((tb * ROW_SLABS, LANES), lambda i, be, nv: (i, 0)),
        scratch_shapes=[pltpu.VMEM((d, D_EXPERT), BF16), pltpu.VMEM((d, D_EXPERT), BF16),
                        pltpu.VMEM((D_EXPERT, d), BF16)],
    )
    return pl.pallas_call(
        _expert_kernel,
        grid_spec=grid_spec,
        out_shape=jax.ShapeDtypeStruct(x_rows.shape, F32),
        compiler_params=_params(("arbitrary",)),
        name="experts",
    )(blk_e, n_valid, x_rows, w_gate, w_up, w_down)


def kernel(x, c, w_ada, b_ada, norm_mix_g, w_in, b_forget, w_out_fox, lambda_re, lambda_im, log_dt,
           ssm_b_re, ssm_b_im, ssm_c_re, ssm_c_im, d_skip, w_glu, w_out_ssm, w_o, norm_ffn_g,
           w_router_group, b_router_group, w_router_expert, b_router_expert, w_gate_e, w_up_e,
           w_down_e, final_g):
    bsz, seq, d = x.shape
    n = bsz * seq
    assert w_ada.shape[0] == 1, "the final RMSNorm is fused into the (single) layer's combine kernel"
    xc = x.reshape(n, d)
    for l in range(1):
        mod3 = _mod(c, w_ada[l], b_ada[l]).reshape(bsz, N_MOD, d)

        wi = w_in[l]
        s_q, s_k, s_v, s_f, s_u, s_ga = 512, 1024, 1536, 1544, 2056, 3080
        scale = FOX_HEAD_DIM ** -0.5
        w_all = jnp.concatenate(
            [wi[:, :s_q] * scale, wi[:, s_q:s_k], wi[:, s_f:s_u], wi[:, s_u:s_ga],
             wi[:, s_ga:], jnp.pad(wi[:, s_v:s_f], ((0, 0), (0, LANES - FOX_HEADS)))],
            axis=1).astype(BF16)
        w_vt = wi[:, s_k:s_v].T.astype(BF16)
        bf_pad = jnp.pad(b_forget[l], (0, LANES - FOX_HEADS)).reshape(1, LANES)
        q, k, v_t, u, u_flat, sga, sgb = _inproj(xc, mod3, norm_mix_g[l].reshape(1, d), w_all, w_vt,
                                                 bf_pad, seq)

        o_fox = _attention(q, k, v_t, bsz, seq)

        toep, b_state, b_swap, c_pow, a_step = _ssm_matrices(
            lambda_re[l], lambda_im[l], log_dt[l], ssm_b_re[l], ssm_b_im[l], ssm_c_re[l], ssm_c_im[l])
        y_flat = _ssm(u_flat, toep, b_state, b_swap, c_pow, a_step, bsz)

        w_r = jnp.pad(jnp.concatenate([w_router_group[l], w_router_expert[l]], axis=1),
                      ((0, 0), (0, LANES - N_GROUPS - N_EXPERTS)))
        w_r1 = _top_bits(w_r)
        w_r2 = _top_bits(w_r - w_r1)
        w_r = jnp.concatenate([w_r1, w_r2, w_r1], axis=0).astype(BF16)
        b_r = jnp.pad(jnp.concatenate([b_router_group[l], b_router_expert[l]]),
                      (0, LANES - N_GROUPS - N_EXPERTS)).reshape(1, LANES)
        x1, h2, logits = _mix(xc, o_fox, y_flat, u, sga, sgb, mod3, d_skip[l].reshape(1, SSM_WIDTH),
                              w_glu[l].astype(BF16), w_out_fox[l].astype(BF16),
                              w_out_ssm[l].astype(BF16), w_o[l].astype(BF16),
                              norm_ffn_g[l].reshape(1, d), w_r, b_r, seq)

        idx, wts, cnt = _route(logits)
        counts = cnt[0, :N_EXPERTS].astype(jnp.int32)
        pcounts = ((counts + ROW_BLOCK - 1) // ROW_BLOCK) * ROW_BLOCK
        pends = jnp.cumsum(pcounts)
        pstarts = pends - pcounts
        er = idx[:, 0:4].T
        hit = er[0:2, None, :] == jnp.arange(N_EXPERTS, dtype=jnp.int32)[None, :, None]
        dest = jnp.sum(jnp.where(hit, pstarts[None, :, None], 0), axis=1) + er[2:4]
        rows = 2 * n + N_EXPERTS * ROW_BLOCK
        n_blocks = rows // ROW_BLOCK
        blk_start = jnp.arange(n_blocks, dtype=jnp.int32) * ROW_BLOCK
        blk_e = jnp.minimum(jnp.sum((pends[None, :] <= blk_start[:, None]).astype(jnp.int32), axis=1),
                            N_EXPERTS - 1)
        n_valid = (pends[-1:] // ROW_BLOCK).astype(jnp.int32)
        dest3 = (dest.astype(jnp.int32).reshape(2, n // MOVE_TILE, MOVE_TILE).transpose(1, 0, 2)
                 .reshape(n // MOVE_TILE, 1, 2 * MOVE_TILE))

        x_rows = _dispatch(dest3, h2, jnp.zeros((rows * ROW_SLABS, LANES), F32))
        y_rows = _experts(blk_e, n_valid, x_rows, w_gate_e[l], w_up_e[l], w_down_e[l])
        xc = _combine(dest3, x1, wts, mod3, final_g.reshape(1, d), y_rows, seq)
    return xc.reshape(bsz, seq, d)
```

```python
import functools
import math

import jax
import jax.numpy as jnp
import numpy as np
from jax import lax
from jax.experimental import pallas as pl
from jax.experimental.pallas import tpu as pltpu

F32 = jnp.float32
BF16 = jnp.bfloat16

D_MODEL = 1024
N_MOD = 6
RMS_EPS = 1e-6
FOX_HEADS = 8
FOX_HEAD_DIM = 64
FOX_WIDTH = FOX_HEADS * FOX_HEAD_DIM
HEAD_PAIRS = FOX_HEADS // 2
SSM_WIDTH = 512
SSM_GROUP = 16
SSM_GROUPS = SSM_WIDTH // SSM_GROUP
SSM_STATE = 64
LAMBDA_RE_MAX = -1e-4
N_GROUPS = 4
EXPERTS_PER_GROUP = 8
N_EXPERTS = N_GROUPS * EXPERTS_PER_GROUP
D_EXPERT = 512

LANES = 128
SUBLANES = 8
VMEM_LIMIT = 56 * 1024 * 1024

SSM_CHUNK = 16
TOK_TILE = 512
MIX_TILE = 256
ATT_Q_TILE = 512
ATT_K_TILE = 256
ROW_BLOCK = 256
MOVE_TILE = 256
NEG_BIG = -1e30

HIGHEST = lax.Precision.HIGHEST


def _params(sem):
    return pltpu.CompilerParams(dimension_semantics=sem, vmem_limit_bytes=VMEM_LIMIT)


def _rms_modulate(x, gain, shift, scale):
    ms = jnp.mean(x * x, axis=-1, keepdims=True)
    return (x * lax.rsqrt(ms + RMS_EPS)) * gain * (1.0 + scale) + shift


def _mod_kernel(c_ref, w_ref, b_ref, o_ref):
    c = c_ref[...]
    ca = (c * jax.nn.sigmoid(c)).astype(BF16)
    o_ref[...] = jnp.dot(ca, w_ref[...].astype(BF16), preferred_element_type=F32) + b_ref[...]


def _mod(c, w_ada, b_ada):
    bsz, d = c.shape
    cols = w_ada.shape[1]
    tn = 1536
    return pl.pallas_call(
        _mod_kernel,
        grid=(cols // tn,),
        in_specs=[pl.BlockSpec((bsz, d), lambda j: (0, 0)),
                  pl.BlockSpec((d, tn), lambda j: (0, j)),
                  pl.BlockSpec((1, tn), lambda j: (0, j))],
        out_specs=pl.BlockSpec((bsz, tn), lambda j: (0, j)),
        out_shape=jax.ShapeDtypeStruct((bsz, cols), F32),
        compiler_params=_params(("arbitrary",)),
        name="mod",
    )(c, w_ada, b_ada.reshape(1, cols))


_C_Q, _C_K, _C_U, _C_GA, _C_GB, _C_F, _C_END = 0, 512, 1024, 1536, 2560, 3584, 3712


def _lane_block():
    return lax.broadcasted_iota(jnp.int32, (1, LANES), 1) // SSM_GROUP


def _to_group_major(tok_ref, flat_ref, rows):
    blk = _lane_block()
    for half in range(2):
        for j in range(SSM_WIDTH // LANES):
            w = []
            for s8 in range(8):
                v = tok_ref[j, pl.ds(8 * half + s8, rows, stride=SSM_CHUNK), :]
                w.append(pltpu.roll(v, s8 * SSM_GROUP, axis=1) if s8 else v)
            for p in range(8):
                acc = w[0]
                for s8 in range(1, 8):
                    acc = jnp.where(blk == (p + s8) % 8, w[s8], acc)
                flat_ref[8 * j + p, :, half * LANES:(half + 1) * LANES] = acc.astype(flat_ref.dtype)


def _to_token_major(flat_ref, tok_ref, rows):
    blk = _lane_block()
    for half in range(2):
        for j in range(SSM_WIDTH // LANES):
            ys = [flat_ref[8 * j + p, :, half * LANES:(half + 1) * LANES] for p in range(8)]
            for s8 in range(8):
                acc = ys[0]
                for p in range(1, 8):
                    acc = jnp.where(blk == (p + s8) % 8, ys[p], acc)
                if s8:
                    acc = pltpu.roll(acc, LANES - s8 * SSM_GROUP, axis=1)
                tok_ref[j, pl.ds(8 * half + s8, rows, stride=SSM_CHUNK), :] = acc


def _bias_lane_placement():
    pq = np.zeros((3 * LANES, HEAD_PAIRS * LANES), np.float32)
    pk = np.zeros((3 * LANES, HEAD_PAIRS * LANES), np.float32)
    bq = np.zeros((1, HEAD_PAIRS * LANES), np.float32)
    bk = np.zeros((1, HEAD_PAIRS * LANES), np.float32)
    for head in range(FOX_HEADS):
        base = (head // 2) * LANES + (head % 2) * 8
        for term in range(3):
            pq[term * LANES + head, base + term] = 1.0
            pk[term * LANES + head, base + 3 + term] = -1.0
            bq[0, base + 3 + term] = 1.0
            bk[0, base + term] = 1.0
    return pq, pk, bq, bk


def _top_bits(a):
    bits = lax.bitcast_convert_type(a, jnp.uint32) & jnp.uint32(0xFFFF0000)
    return lax.bitcast_convert_type(bits, F32)


def _inproj_kernel(tiles_per_batch, x_ref, mod_ref, g_ref, w_ref, wvt_ref, bf_ref, tri_ref,
                   pq_ref, pk_ref, bq_ref, bk_ref,
                   q_ref, k_ref, vt_ref, u_ref, uflat_ref, ga_ref, gb_ref, carry_ref, uslab_ref):
    i = pl.program_id(0)
    h = _rms_modulate(x_ref[...], g_ref[...], mod_ref[0:1, :], mod_ref[1:2, :])
    hb = h.astype(BF16)

    def proj(a, b):
        return jnp.dot(hb, w_ref[:, a:b], preferred_element_type=F32)

    q = proj(_C_Q, _C_K).astype(BF16)
    k = proj(_C_K, _C_U).astype(BF16)
    vt_ref[...] = lax.dot_general(wvt_ref[...], hb, (((1,), (1,)), ((), ())),
                                  preferred_element_type=F32).astype(BF16)
    u = proj(_C_U, _C_GA)
    u_ref[...] = u
    for j in range(SSM_WIDTH // LANES):
        uslab_ref[j] = u[:, j * LANES:(j + 1) * LANES]
    _to_group_major(uslab_ref, uflat_ref, u.shape[0] // SSM_CHUNK)
    ga_ref[...] = jax.nn.sigmoid(proj(_C_GA, _C_GB)).astype(BF16)
    gb_ref[...] = jax.nn.sigmoid(proj(_C_GB, _C_F)).astype(BF16)

    f = proj(_C_F, _C_END) + bf_ref[...]
    logf = jnp.minimum(f, 0.0) - jnp.log(1.0 + jnp.exp(-jnp.abs(f)))

    @pl.when(i % tiles_per_batch == 0)
    def _():
        carry_ref[...] = jnp.zeros_like(carry_ref)

    def split3(a):
        hi = _top_bits(a)
        r1 = a - hi
        mid = _top_bits(r1)
        return jnp.concatenate([hi, mid, _top_bits(r1 - mid)], axis=1).astype(BF16)

    part = jnp.dot(tri_ref[...], split3(logf), preferred_element_type=F32)
    cs = (part[:, :LANES] + part[:, LANES:2 * LANES] + part[:, 2 * LANES:]) + carry_ref[0:1, :]
    carry_ref[...] = jnp.broadcast_to(cs[-1:, :], carry_ref.shape)

    terms = split3(cs)
    bias_q = (jnp.dot(terms, pq_ref[...], preferred_element_type=F32) + bq_ref[...]).astype(BF16)
    bias_k = (jnp.dot(terms, pk_ref[...], preferred_element_type=F32) + bk_ref[...]).astype(BF16)
    for p in range(HEAD_PAIRS):
        lanes = slice(p * LANES, (p + 1) * LANES)
        q_ref[:, 2 * p * LANES:(2 * p + 1) * LANES] = q[:, lanes]
        q_ref[:, (2 * p + 1) * LANES:(2 * p + 2) * LANES] = bias_q[:, lanes]
        k_ref[:, 2 * p * LANES:(2 * p + 1) * LANES] = k[:, lanes]
        k_ref[:, (2 * p + 1) * LANES:(2 * p + 2) * LANES] = bias_k[:, lanes]


def _inproj(x2, mod3, gain, w_all, w_vt, bf_pad, seq):
    n, d = x2.shape
    tm = TOK_TILE
    tpb = seq // tm
    tri = jnp.tril(jnp.ones((tm, tm), BF16))
    pq, pk, bq, bk = _bias_lane_placement()
    tok = lambda w: pl.BlockSpec((tm, w), lambda i: (i, 0))
    const = lambda shape: pl.BlockSpec(shape, lambda i: (0,) * len(shape))
    qk_width = 2 * FOX_WIDTH
    return pl.pallas_call(
        functools.partial(_inproj_kernel, tpb),
        grid=(n // tm,),
        in_specs=[tok(d),
                  pl.BlockSpec((None, N_MOD, d), lambda i: (i // tpb, 0, 0)),
                  const((1, d)), const((d, _C_END)), const((FOX_WIDTH, d)), const((1, LANES)),
                  const((tm, tm)), const(pq.shape), const(pk.shape), const(bq.shape), const(bk.shape)],
        out_specs=[tok(qk_width), tok(qk_width), pl.BlockSpec((FOX_WIDTH, tm), lambda i: (0, i)),
                   tok(SSM_WIDTH),
                   pl.BlockSpec((SSM_GROUPS, tm // SSM_CHUNK, SSM_CHUNK * SSM_GROUP), lambda i: (0, i, 0)),
                   tok(d), tok(d)],
        out_shape=[jax.ShapeDtypeStruct((n, qk_width), BF16)] * 2
        + [jax.ShapeDtypeStruct((FOX_WIDTH, n), BF16)]
        + [jax.ShapeDtypeStruct((n, SSM_WIDTH), F32)]
        + [jax.ShapeDtypeStruct((SSM_GROUPS, n // SSM_CHUNK, SSM_CHUNK * SSM_GROUP), BF16)]
        + [jax.ShapeDtypeStruct((n, d), BF16)] * 2,
        scratch_shapes=[pltpu.VMEM((SUBLANES, LANES), F32),
                        pltpu.VMEM((SSM_WIDTH // LANES, tm, LANES), F32)],
        compiler_params=_params(("arbitrary",)),
        name="inproj",
    )(x2, mod3, gain, w_all, w_vt, bf_pad, tri, jnp.asarray(pq, BF16), jnp.asarray(pk, BF16),
      jnp.asarray(bq), jnp.asarray(bk))


def _attn_kernel(q_ref, k_ref, vt_ref, o_ref, m_ref, acc_ref):
    i = pl.program_id(2)
    tq, tk = ATT_Q_TILE, ATT_K_TILE
    q = q_ref[...]
    lane = lax.broadcasted_iota(jnp.int32, (1, 2 * LANES), 1)
    zq = jnp.zeros_like(q)
    half = FOX_HEAD_DIM
    own0 = (lane < half) | ((lane >= LANES) & (lane < LANES + 8))
    own1 = ((lane >= half) & (lane < LANES)) | ((lane >= LANES + 8) & (lane < LANES + 16))
    q_both = jnp.concatenate([jnp.where(own0, q, zq), jnp.where(own1, q, zq)], axis=0)
    m_ref[...] = jnp.full(m_ref.shape, NEG_BIG, F32)
    acc_ref[...] = jnp.zeros(acc_ref.shape, F32)
    ones_rows = jnp.ones((2 * SUBLANES, tk), BF16)
    key_in_tile = lax.broadcasted_iota(jnp.int32, (tk, 2 * tq), 0)
    qry_pos = i * tq + (lax.broadcasted_iota(jnp.int32, (tk, 2 * tq), 1) & (tq - 1))

    def scores(j):
        start = pl.multiple_of(j * tk, tk)
        s = lax.dot_general(k_ref[pl.ds(start, tk), :], q_both, (((1,), (1,)), ((), ())),
                            preferred_element_type=F32)
        return jnp.where(key_in_tile + j * tk <= qry_pos, s, NEG_BIG)

    def accumulate(j, s):
        start = pl.multiple_of(j * tk, tk)
        va = jnp.concatenate([vt_ref[:, pl.ds(start, tk)], ones_rows], axis=0)
        m_old = m_ref[...]
        m_new = jnp.maximum(m_old, jnp.max(s, axis=0, keepdims=True))
        alpha = jnp.exp(m_old - m_new)
        p = jnp.exp(s - m_new).astype(BF16)
        acc_ref[...] = alpha * acc_ref[...] + jnp.dot(va, p, preferred_element_type=F32)
        m_ref[...] = m_new

    def body(j, s_cur):
        s_next = scores(j + 1)
        accumulate(j, s_cur)
        return s_next

    last = (i + 1) * (tq // tk) - 1
    s_last = lax.fori_loop(0, last, body, scores(0))
    accumulate(last, s_last)

    acc = acc_ref[...]
    o_t = jnp.concatenate([acc[0:half, 0:tq] / acc[LANES:LANES + 1, 0:tq],
                           acc[half:LANES, tq:2 * tq] / acc[LANES:LANES + 1, tq:2 * tq]], axis=0)
    o_ref[...] = o_t.T.astype(o_ref.dtype)


def _attention(q, k, v_t, bsz, seq):
    n = q.shape[0]
    t = ATT_Q_TILE
    nq = seq // t
    return pl.pallas_call(
        _attn_kernel,
        grid=(bsz, HEAD_PAIRS, nq),
        in_specs=[pl.BlockSpec((t, 2 * LANES), lambda b, p, i: (b * nq + i, p)),
                  pl.BlockSpec((seq, 2 * LANES), lambda b, p, i: (b, p)),
                  pl.BlockSpec((LANES, seq), lambda b, p, i: (p, b))],
        out_specs=pl.BlockSpec((t, LANES), lambda b, p, i: (b * nq + i, p)),
        out_shape=jax.ShapeDtypeStruct((n, FOX_WIDTH), BF16),
        scratch_shapes=[pltpu.VMEM((1, 2 * t), F32), pltpu.VMEM((LANES + 2 * SUBLANES, 2 * t), F32)],
        compiler_params=_params(("arbitrary", "arbitrary", "arbitrary")),
        name="attn",
    )(q, k, v_t)


def _ssm_prep_kernel(lrow_ref, lcol_ref, ldt_ref, btr_ref, bti_ref, ctr_ref, cti_ref,
                     toep_ref, bst_ref, bsw_ref, cpw_ref, a_ref):
    p8 = pl.program_id(0) % 8
    t_len, grp = SSM_CHUNK, SSM_GROUP
    dt = jnp.exp(ldt_ref[...])
    lr, li = jnp.minimum(lrow_ref[0:1, :], LAMBDA_RE_MAX), lrow_ref[1:2, :]

    def powers(steps, re, im):
        mag = jnp.exp(steps * (re * dt))
        return mag * jnp.cos(steps * (im * dt)), mag * jnp.sin(steps * (im * dt))

    a_re, a_im = powers(1.0, lr, li)
    den = lr * lr + li * li
    nr = a_re - 1.0
    co_re = (nr * lr + a_im * li) / den
    co_im = (a_im * lr - nr * li) / den
    bbt_re = co_re * btr_ref[...] - co_im * bti_ref[...]
    bbt_im = co_re * bti_ref[...] + co_im * btr_ref[...]

    lag = (lax.broadcasted_iota(jnp.int32, (1, t_len * grp), 1) // grp).astype(F32)
    lcr, lci = jnp.minimum(lcol_ref[:, 0:1], LAMBDA_RE_MAX), lcol_ref[:, 1:2]

    def c_times_power(steps):
        p_re, p_im = powers(steps, lcr, lci)
        return (ctr_ref[...] * p_re - cti_ref[...] * p_im, ctr_ref[...] * p_im + cti_ref[...] * p_re)

    wt_re, wt_im = c_times_power(lag)
    kern = (jnp.dot(bbt_re, wt_re, precision=HIGHEST, preferred_element_type=F32)
            - jnp.dot(bbt_im, wt_im, precision=HIGHEST, preferred_element_type=F32))

    lane = lax.broadcasted_iota(jnp.int32, (1, LANES), 1)
    col_shift = p8 * grp

    def store_cols(ref, rows, lo_half, hi_half):
        ref[rows, 0:LANES] = pltpu.roll(lo_half, col_shift, axis=1).astype(ref.dtype)
        ref[rows, LANES:2 * LANES] = pltpu.roll(hi_half, col_shift, axis=1).astype(ref.dtype)

    def slot_rows(s):
        half, s8 = divmod(s, 8)
        return pl.ds(pl.multiple_of((8 * half + (s8 + p8) % 8) * grp, grp), grp)

    back = (t_len - 1 - lax.broadcasted_iota(jnp.int32, (t_len, 1), 0)).astype(F32)
    e_re, e_im = powers(back, lr, li)
    zero = jnp.zeros((grp, LANES), F32)
    k_lo, k_hi = kern[:, 0:LANES], kern[:, LANES:2 * LANES]
    for s in range(t_len):
        half, s8 = divmod(s, 8)
        keep = lane >= s8 * grp
        r_lo = pltpu.roll(k_lo, s8 * grp, axis=1) if s8 else k_lo
        r_hi = pltpu.roll(k_hi, s8 * grp, axis=1) if s8 else k_hi
        if half == 0:
            lo, hi = jnp.where(keep, r_lo, 0.0), jnp.where(keep, r_hi, r_lo)
        else:
            lo, hi = zero, jnp.where(keep, r_lo, 0.0)
        store_cols(toep_ref, slot_rows(s), lo, hi)
        es_re, es_im = e_re[s:s + 1, :], e_im[s:s + 1, :]
        bs_re = es_re * bbt_re - es_im * bbt_im
        bs_im = es_re * bbt_im + es_im * bbt_re
        bst_ref[slot_rows(s), :] = jnp.concatenate([bs_re, bs_im], axis=1).astype(bst_ref.dtype)
        bsw_ref[slot_rows(s), :] = jnp.concatenate([bs_im, bs_re], axis=1).astype(bsw_ref.dtype)

    w1_re, w1_im = c_times_power(lag + 1.0)
    store_cols(cpw_ref, pl.ds(0, SSM_STATE), w1_re[:, 0:LANES], w1_re[:, LANES:2 * LANES])
    store_cols(cpw_ref, pl.ds(SSM_STATE, SSM_STATE), -w1_im[:, 0:LANES], -w1_im[:, LANES:2 * LANES])
    s_re, s_im = powers(float(t_len), lr, li)
    a_ref[0:1, :] = jnp.concatenate([s_re, s_re], axis=1)
    a_ref[1:2, :] = jnp.concatenate([-s_im, s_im], axis=1)


def _ssm_prep(lambda_re, lambda_im, log_dt, b_re, b_im, c_re, c_im):
    width = SSM_CHUNK * SSM_GROUP
    lam_row = jnp.stack([lambda_re, lambda_im], axis=1)
    tiled = lambda c: jnp.tile(c.transpose(0, 2, 1), (1, 1, SSM_CHUNK))
    per = lambda a, b: pl.BlockSpec((None, a, b), lambda g: (g, 0, 0))
    return pl.pallas_call(
        _ssm_prep_kernel,
        grid=(SSM_GROUPS,),
        in_specs=[per(2, SSM_STATE), per(SSM_STATE, 2), per(1, 1), per(SSM_GROUP, SSM_STATE),
                  per(SSM_GROUP, SSM_STATE), per(SSM_STATE, width), per(SSM_STATE, width)],
        out_specs=[per(width, width), per(width, 2 * SSM_STATE), per(width, 2 * SSM_STATE),
                   per(2 * SSM_STATE, width), per(2, 2 * SSM_STATE)],
        out_shape=[jax.ShapeDtypeStruct((SSM_GROUPS, width, width), BF16),
                   jax.ShapeDtypeStruct((SSM_GROUPS, width, 2 * SSM_STATE), BF16),
                   jax.ShapeDtypeStruct((SSM_GROUPS, width, 2 * SSM_STATE), BF16),
                   jax.ShapeDtypeStruct((SSM_GROUPS, 2 * SSM_STATE, width), BF16),
                   jax.ShapeDtypeStruct((SSM_GROUPS, 2, 2 * SSM_STATE), F32)],
        compiler_params=_params(("arbitrary",)),
        name="ssm_prep",
    )(lam_row, lam_row.transpose(0, 2, 1), log_dt.reshape(SSM_GROUPS, 1, 1),
      b_re.transpose(0, 2, 1), b_im.transpose(0, 2, 1), tiled(c_re), tiled(c_im))


def _ssm_kernel(n_chunks, bsz, u_ref, toep_ref, bst_ref, bsw_ref, cpw_ref, a_ref, y_ref,
                contrib_ref, cswap_ref, xprev_ref):
    u = u_ref[...]
    contrib_ref[...] = jnp.dot(u, bst_ref[...], preferred_element_type=F32)
    cswap_ref[...] = jnp.dot(u, bsw_ref[...], preferred_element_type=F32)
    a1 = a_ref[0:1, :]
    a2 = a_ref[1:2, :]

    def step(n, carry):
        x, xs = carry
        rows = pl.ds(n, bsz, stride=n_chunks)
        xprev_ref[rows, :] = x
        x_new = a1 * x + a2 * xs + contrib_ref[rows, :]
        xs_new = a1 * xs - a2 * x + cswap_ref[rows, :]
        return x_new, xs_new

    zero = jnp.zeros((bsz, 2 * SSM_STATE), F32)
    lax.fori_loop(0, n_chunks, step, (zero, zero), unroll=4)
    y_ref[...] = (jnp.dot(u, toep_ref[...], preferred_element_type=F32)
                  + jnp.dot(xprev_ref[...].astype(BF16), cpw_ref[...], preferred_element_type=F32))


def _ssm(u_flat, toep, b_state, b_swap, c_pow, a_step, bsz):
    g, rows, w = u_flat.shape
    per = lambda a, b: pl.BlockSpec((None, a, b), lambda i: (i, 0, 0))
    state = pltpu.VMEM((rows, 2 * SSM_STATE), F32)
    return pl.pallas_call(
        functools.partial(_ssm_kernel, rows // bsz, bsz),
        grid=(g,),
        in_specs=[per(rows, w), per(w, w), per(w, 2 * SSM_STATE), per(w, 2 * SSM_STATE),
                  per(2 * SSM_STATE, w), per(2, 2 * SSM_STATE)],
        out_specs=per(rows, w),
        out_shape=jax.ShapeDtypeStruct((g, rows, w), F32),
        scratch_shapes=[state, state, state],
        compiler_params=_params(("arbitrary",)),
        name="ssm",
    )(u_flat, toep, b_state, b_swap, c_pow, a_step)


ROW_SLABS = D_MODEL // LANES


def _store_row_tiles(ref, value):
    rows = value.shape[0]
    for k in range(ROW_SLABS):
        ref[pl.ds(k, rows, stride=ROW_SLABS), :] = value[:, k * LANES:(k + 1) * LANES]


def _load_row_tiles(ref, rows):
    return jnp.concatenate([ref[pl.ds(k, rows, stride=ROW_SLABS), :] for k in range(ROW_SLABS)], axis=1)


def _row_tile_copy(src_ref, src_row, dst_ref, dst_row, sem):
    src = src_ref.at[pl.ds(pl.multiple_of(src_row * ROW_SLABS, ROW_SLABS), ROW_SLABS), :]
    dst = dst_ref.at[pl.ds(pl.multiple_of(dst_row * ROW_SLABS, ROW_SLABS), ROW_SLABS), :]
    return pltpu.make_async_copy(src, dst, sem)


def _mix_kernel(x_ref, of_ref, yf_ref, u_ref, ga_ref, gb_ref, mod_ref, dsk_ref, wglu_ref, wfox_ref,
                wssm_ref, wo_ref, g2_ref, wr_ref, br_ref, x1_ref, h2_ref, lg_ref, ytok_ref):
    _to_token_major(yf_ref, ytok_ref, yf_ref.shape[1])
    y_ssm = jnp.concatenate([ytok_ref[j] for j in range(SSM_WIDTH // LANES)], axis=1)
    y = y_ssm + dsk_ref[...] * u_ref[...]
    y = 0.5 * y * (1.0 + jnp.tanh(math.sqrt(2.0 / math.pi) * (y + 0.044715 * (y * y * y))))
    gl = jnp.dot(y.astype(BF16), wglu_ref[...], preferred_element_type=F32)
    o_ssm = gl[:, :SSM_WIDTH] * jax.nn.sigmoid(gl[:, SSM_WIDTH:])
    merged = (ga_ref[...].astype(F32) * jnp.dot(of_ref[...], wfox_ref[...], preferred_element_type=F32)
              + gb_ref[...].astype(F32) * jnp.dot(o_ssm.astype(BF16), wssm_ref[...],
                                                  preferred_element_type=F32))
    x1 = x_ref[...] + mod_ref[2:3, :] * jnp.dot(merged.astype(BF16), wo_ref[...],
                                                 preferred_element_type=F32)
    x1_ref[...] = x1
    h2 = _rms_modulate(x1, g2_ref[...], mod_ref[3:4, :], mod_ref[4:5, :])
    _store_row_tiles(h2_ref, h2)
    a1 = _top_bits(h2)
    a2 = _top_bits(h2 - a1)
    lhs = jnp.concatenate([a1, a1, a2], axis=1).astype(BF16)
    lg_ref[...] = jnp.dot(lhs, wr_ref[...], preferred_element_type=F32) + br_ref[...]


def _mix(x2, o_fox, y_flat, u, sga, sgb, mod3, d_skip, w_glu, w_fox, w_ssm, w_o, g2, w_r, b_r, seq):
    n, d = x2.shape
    tm = MIX_TILE
    tpb = seq // tm
    tok = lambda w: pl.BlockSpec((tm, w), lambda i: (i, 0))
    const = lambda a: pl.BlockSpec(a.shape, lambda i: (0,) * a.ndim)
    flat = pl.BlockSpec((SSM_GROUPS, tm // SSM_CHUNK, SSM_CHUNK * SSM_GROUP), lambda i: (0, i, 0))
    return pl.pallas_call(
        _mix_kernel,
        grid=(n // tm,),
        in_specs=[tok(d), tok(FOX_WIDTH), flat, tok(SSM_WIDTH), tok(d), tok(d),
                  pl.BlockSpec((None, N_MOD, d), lambda i: (i // tpb, 0, 0)),
                  const(d_skip), const(w_glu), const(w_fox), const(w_ssm), const(w_o), const(g2),
                  const(w_r), const(b_r)],
        out_specs=[tok(d), pl.BlockSpec((tm * ROW_SLABS, LANES), lambda i: (i, 0)), tok(LANES)],
        out_shape=[jax.ShapeDtypeStruct((n, d), F32), jax.ShapeDtypeStruct((n * ROW_SLABS, LANES), F32),
                   jax.ShapeDtypeStruct((n, LANES), F32)],
        scratch_shapes=[pltpu.VMEM((SSM_WIDTH // LANES, tm, LANES), F32)],
        compiler_params=_params(("arbitrary",)),
        name="mix",
    )(x2, o_fox, y_flat, u, sga, sgb, mod3, d_skip, w_glu, w_fox, w_ssm, w_o, g2, w_r, b_r)


def _route_kernel(lg_ref, tri_ref, idx_ref, wt_ref, cnt_ref, carry_ref):
    i = pl.program_id(0)

    @pl.when(i == 0)
    def _():
        carry_ref[...] = jnp.zeros_like(carry_ref)

    lg = lg_ref[...]
    tm = lg.shape[0]
    lane = lax.broadcasted_iota(jnp.int32, (tm, LANES), 1)
    neg = jnp.full_like(lg, -jnp.inf)

    def first_argmax(vals):
        mx = jnp.max(vals, axis=1, keepdims=True)
        ix = jnp.min(jnp.where(vals == mx, lane, LANES), axis=1, keepdims=True)
        return mx, ix

    is_group = lane < N_GROUPS
    g_max, gi = first_argmax(jnp.where(is_group, lg, neg))
    g_sum = jnp.sum(jnp.where(is_group, jnp.exp(lg - g_max), 0.0), axis=1, keepdims=True)
    p_group = 1.0 / g_sum
    lo = N_GROUPS + EXPERTS_PER_GROUP * gi
    in_group = (lane >= lo) & (lane < lo + EXPERTS_PER_GROUP)
    cand = jnp.where(in_group, lg, neg)
    v1, i1 = first_argmax(cand)
    v2, i2 = first_argmax(jnp.where(lane == i1, neg, cand))
    tt = jnp.exp(v2 - v1)
    w1 = p_group / (1.0 + tt)
    w2 = p_group * tt / (1.0 + tt)
    e1 = i1 - N_GROUPS
    e2 = i2 - N_GROUPS
    sel1 = lane == e1
    sel2 = lane == e2
    onehot = (sel1 | sel2).astype(F32)
    before = jnp.dot(tri_ref[...], onehot.astype(BF16), preferred_element_type=F32) + carry_ref[0:1, :]
    r1 = jnp.sum(jnp.where(sel1, before, 0.0), axis=1, keepdims=True).astype(jnp.int32)
    r2 = jnp.sum(jnp.where(sel2, before, 0.0), axis=1, keepdims=True).astype(jnp.int32)
    total = before[-1:, :] + onehot[-1:, :]
    carry_ref[...] = jnp.broadcast_to(total, carry_ref.shape)
    cnt_ref[...] = jnp.broadcast_to(total, cnt_ref.shape)
    idx_ref[...] = jnp.where(lane == 0, e1, jnp.where(lane == 1, e2, jnp.where(lane == 2, r1, r2)))
    wt_ref[...] = jnp.where(lane == 0, w1, w2)


def _route(logits):
    n = logits.shape[0]
    tm = TOK_TILE
    tri = jnp.tril(jnp.ones((tm, tm), BF16), k=-1)
    tok = pl.BlockSpec((tm, LANES), lambda i: (i, 0))
    return pl.pallas_call(
        _route_kernel,
        grid=(n // tm,),
        in_specs=[tok, pl.BlockSpec((tm, tm), lambda i: (0, 0))],
        out_specs=[tok, tok, pl.BlockSpec((SUBLANES, LANES), lambda i: (0, 0))],
        out_shape=[jax.ShapeDtypeStruct((n, LANES), jnp.int32), jax.ShapeDtypeStruct((n, LANES), F32),
                   jax.ShapeDtypeStruct((SUBLANES, LANES), F32)],
        scratch_shapes=[pltpu.VMEM((SUBLANES, LANES), F32)],
        compiler_params=_params(("arbitrary",)),
        name="route",
    )(logits, tri)


ISSUE_UNROLL = 8


def _dispatch_kernel(dest_ref, h_ref, rows_in_ref, rows_ref, sem):
    del rows_in_ref
    tm = h_ref.shape[0] // ROW_SLABS

    def issue(g, c):
        for j in range(ISSUE_UNROLL):
            t = g * ISSUE_UNROLL + j
            _row_tile_copy(h_ref, t, rows_ref, dest_ref[0, 0, t], sem).start(priority=0)
            _row_tile_copy(h_ref, t, rows_ref, dest_ref[0, 0, tm + t], sem).start(priority=1)
        return c

    lax.fori_loop(0, tm // ISSUE_UNROLL, issue, 0)
    for _ in range(2):
        pltpu.make_async_copy(h_ref, rows_ref.at[pl.ds(0, tm * ROW_SLABS), :], sem).wait()


def _dispatch(dest3, h2_tiles, rows_zero):
    tm = MOVE_TILE
    n = h2_tiles.shape[0] // ROW_SLABS
    return pl.pallas_call(
        _dispatch_kernel,
        grid=(n // tm,),
        in_specs=[pl.BlockSpec((1, 1, 2 * tm), lambda i: (i, 0, 0), memory_space=pltpu.SMEM),
                  pl.BlockSpec((tm * ROW_SLABS, LANES), lambda i: (i, 0)),
                  pl.BlockSpec(memory_space=pl.ANY)],
        out_specs=pl.BlockSpec(memory_space=pl.ANY),
        out_shape=jax.ShapeDtypeStruct(rows_zero.shape, rows_zero.dtype),
        scratch_shapes=[pltpu.SemaphoreType.DMA(())],
        input_output_aliases={2: 0},
        compiler_params=_params(("arbitrary",)),
        name="dispatch",
    )(dest3, h2_tiles, rows_zero)


def _combine_kernel(n_steps, dest_ref, dnext_ref, x1_ref, wt_ref, mod_ref, gf_ref, yr_ref, o_ref,
                    buf_ref, sem):
    i = pl.program_id(0)
    tm = x1_ref.shape[0]

    def gather(idx_ref, which):
        def issue(g, c):
            for j in range(ISSUE_UNROLL):
                t = g * ISSUE_UNROLL + j
                _row_tile_copy(yr_ref, idx_ref[0, 0, t], buf_ref.at[which, 0], t,
                               sem.at[which]).start(priority=0)
                _row_tile_copy(yr_ref, idx_ref[0, 0, tm + t], buf_ref.at[which, 1], t,
                               sem.at[which]).start(priority=1)
            return c

        lax.fori_loop(0, tm // ISSUE_UNROLL, issue, 0)

    cur = i % 2

    @pl.when(i == 0)
    def _():
        gather(dest_ref, 0)

    @pl.when(i + 1 < n_steps)
    def _():
        gather(dnext_ref, 1 - cur)

    for slot in range(2):
        pltpu.make_async_copy(yr_ref.at[pl.ds(0, tm * ROW_SLABS), :], buf_ref.at[cur, slot],
                              sem.at[cur]).wait()
    wt = wt_ref[...]
    moe = (wt[:, 0:1] * _load_row_tiles(buf_ref.at[cur, 0], tm)
           + wt[:, 1:2] * _load_row_tiles(buf_ref.at[cur, 1], tm))
    x = x1_ref[...] + mod_ref[5:6, :] * moe
    ms = jnp.mean(x * x, axis=-1, keepdims=True)
    o_ref[...] = (x * lax.rsqrt(ms + RMS_EPS)) * gf_ref[...]


def _combine(dest3, x1, wts, mod3, final_g, y_rows, seq):
    n, d = x1.shape
    tm = MOVE_TILE
    tpb = seq // tm
    n_steps = n // tm
    idx_spec = lambda f: pl.BlockSpec((1, 1, 2 * tm), f, memory_space=pltpu.SMEM)
    return pl.pallas_call(
        functools.partial(_combine_kernel, n_steps),
        grid=(n_steps,),
        in_specs=[idx_spec(lambda i: (i, 0, 0)),
                  idx_spec(lambda i: (jnp.minimum(i + 1, n_steps - 1), 0, 0)),
                  pl.BlockSpec((tm, d), lambda i: (i, 0)),
                  pl.BlockSpec((tm, LANES), lambda i: (i, 0)),
                  pl.BlockSpec((None, N_MOD, d), lambda i: (i // tpb, 0, 0)),
                  pl.BlockSpec((1, d), lambda i: (0, 0)),
                  pl.BlockSpec(memory_space=pl.ANY)],
        out_specs=pl.BlockSpec((tm, d), lambda i: (i, 0)),
        out_shape=jax.ShapeDtypeStruct((n, d), F32),
        scratch_shapes=[pltpu.VMEM((2, 2, tm * ROW_SLABS, LANES), F32), pltpu.SemaphoreType.DMA((2,))],
        compiler_params=_params(("arbitrary",)),
        name="combine",
    )(dest3, dest3, x1, wts, mod3, final_g, y_rows)


def _expert_kernel(be_ref, nv_ref, x_ref, wg_ref, wu_ref, wd_ref, y_ref, wgb_ref, wub_ref, wdb_ref):
    i = pl.program_id(0)
    valid = i < nv_ref[0]
    new_expert = (i == 0) | (be_ref[i] != be_ref[jnp.maximum(i - 1, 0)])

    @pl.when(valid & new_expert)
    def _():
        wgb_ref[...] = wg_ref[...].astype(BF16)
        wub_ref[...] = wu_ref[...].astype(BF16)
        wdb_ref[...] = wd_ref[...].astype(BF16)

    @pl.when(valid)
    def _():
        xb = _load_row_tiles(x_ref, x_ref.shape[0] // ROW_SLABS).astype(BF16)
        a = jnp.dot(xb, wgb_ref[...], preferred_element_type=F32)
        b = jnp.dot(xb, wub_ref[...], preferred_element_type=F32)
        hid = (a * jax.nn.sigmoid(a)) * b
        _store_row_tiles(y_ref, jnp.dot(hid.astype(BF16), wdb_ref[...], preferred_element_type=F32))

    @pl.when(jnp.logical_not(valid))
    def _():
        y_ref[...] = jnp.zeros_like(y_ref)


def _experts(blk_e, n_valid, x_rows, w_gate, w_up, w_down):
    d = D_MODEL
    rows = x_rows.shape[0] // ROW_SLABS
    tb = ROW_BLOCK
    grid_spec = pltpu.PrefetchScalarGridSpec(
        num_scalar_prefetch=2,
        grid=(rows // tb,),
        in_specs=[pl.BlockSpec((tb * ROW_SLABS, LANES), lambda i, be, nv: (i, 0)),
                  pl.BlockSpec((None, d, D_EXPERT), lambda i, be, nv: (be[i], 0, 0)),
                  pl.BlockSpec((None, d, D_EXPERT), lambda i, be, nv: (be[i], 0, 0)),
                  pl.BlockSpec((None, D_EXPERT, d), lambda i, be, nv: (be[i], 0, 0))],
        out_specs=pl.BlockSpec((tb * ROW_SLABS, LANES), lambda i, be, nv: (i, 0)),
        scratch_shapes=[pltpu.VMEM((d, D_EXPERT), BF16), pltpu.VMEM((d, D_EXPERT), BF16),
                        pltpu.VMEM((D_EXPERT, d), BF16)],
    )
    return pl.pallas_call(
        _expert_kernel,
        grid_spec=grid_spec,
        out_shape=jax.ShapeDtypeStruct(x_rows.shape, F32),
        compiler_params=_params(("arbitrary",)),
        name="experts",
    )(blk_e, n_valid, x_rows, w_gate, w_up, w_down)


def kernel(x, c, w_ada, b_ada, norm_mix_g, w_in, b_forget, w_out_fox, lambda_re, lambda_im, log_dt,
           ssm_b_re, ssm_b_im, ssm_c_re, ssm_c_im, d_skip, w_glu, w_out_ssm, w_o, norm_ffn_g,
           w_router_group, b_router_group, w_router_expert, b_router_expert, w_gate_e, w_up_e,
           w_down_e, final_g):
    bsz, seq, d = x.shape
    n = bsz * seq
    assert w_ada.shape[0] == 1, "the final RMSNorm is fused into the (single) layer's combine kernel"
    xc = x.reshape(n, d)
    for l in range(1):
        mod3 = _mod(c, w_ada[l], b_ada[l]).reshape(bsz, N_MOD, d)

        wi = w_in[l]
        s_q, s_k, s_v, s_f, s_u, s_ga = 512, 1024, 1536, 1544, 2056, 3080
        scale = FOX_HEAD_DIM ** -0.5
        w_all = jnp.concatenate(
            [wi[:, :s_q] * scale, wi[:, s_q:s_k], wi[:, s_f:s_u], wi[:, s_u:s_ga],
             wi[:, s_ga:], jnp.pad(wi[:, s_v:s_f], ((0, 0), (0, LANES - FOX_HEADS)))],
            axis=1).astype(BF16)
        w_vt = wi[:, s_k:s_v].T.astype(BF16)
        bf_pad = jnp.pad(b_forget[l], (0, LANES - FOX_HEADS)).reshape(1, LANES)
        q, k, v_t, u, u_flat, sga, sgb = _inproj(xc, mod3, norm_mix_g[l].reshape(1, d), w_all, w_vt,
                                                 bf_pad, seq)

        o_fox = _attention(q, k, v_t, bsz, seq)

        toep, b_state, b_swap, c_pow, a_step = _ssm_prep(
            lambda_re[l], lambda_im[l], log_dt[l], ssm_b_re[l], ssm_b_im[l], ssm_c_re[l], ssm_c_im[l])
        y_flat = _ssm(u_flat, toep, b_state, b_swap, c_pow, a_step, bsz)

        w_r = jnp.pad(jnp.concatenate([w_router_group[l], w_router_expert[l]], axis=1),
                      ((0, 0), (0, LANES - N_GROUPS - N_EXPERTS)))
        w_r1 = _top_bits(w_r)
        w_r2 = _top_bits(w_r - w_r1)
        w_r = jnp.concatenate([w_r1, w_r2, w_r1], axis=0).astype(BF16)
        b_r = jnp.pad(jnp.concatenate([b_router_group[l], b_router_expert[l]]),
                      (0, LANES - N_GROUPS - N_EXPERTS)).reshape(1, LANES)
        x1, h2, logits = _mix(xc, o_fox, y_flat, u, sga, sgb, mod3, d_skip[l].reshape(1, SSM_WIDTH),
                              w_glu[l].astype(BF16), w_out_fox[l].astype(BF16),
                              w_out_ssm[l].astype(BF16), w_o[l].astype(BF16),
                              norm_ffn_g[l].reshape(1, d), w_r, b_r, seq)

        idx, wts, cnt = _route(logits)
        counts = cnt[0, :N_EXPERTS].astype(jnp.int32)
        pcounts = ((counts + ROW_BLOCK - 1) // ROW_BLOCK) * ROW_BLOCK
        pends = jnp.cumsum(pcounts)
        pstarts = pends - pcounts
        er = idx[:, 0:4].T
        hit = er[0:2, None, :] == jnp.arange(N_EXPERTS, dtype=jnp.int32)[None, :, None]
        dest = jnp.sum(jnp.where(hit, pstarts[None, :, None], 0), axis=1) + er[2:4]
        rows = 2 * n + N_EXPERTS * ROW_BLOCK
        n_blocks = rows // ROW_BLOCK
        blk_start = jnp.arange(n_blocks, dtype=jnp.int32) * ROW_BLOCK
        blk_e = jnp.minimum(jnp.sum((pends[None, :] <= blk_start[:, None]).astype(jnp.int32), axis=1),
                            N_EXPERTS - 1)
        n_valid = (pends[-1:] // ROW_BLOCK).astype(jnp.int32)
        dest3 = (dest.astype(jnp.int32).reshape(2, n // MOVE_TILE, MOVE_TILE).transpose(1, 0, 2)
                 .reshape(n // MOVE_TILE, 1, 2 * MOVE_TILE))

        x_rows = _dispatch(dest3, h2, jnp.zeros((rows * ROW_SLABS, LANES), F32))
        y_rows = _experts(blk_e, n_valid, x_rows, w_gate_e[l], w_up_e[l], w_down_e[l])
        xc = _combine(dest3, x1, wts, mod3, final_g.reshape(1, d), y_rows, seq)
    return xc.reshape(bsz, seq, d)
```

```python
import functools
import math

import jax
import jax.numpy as jnp
import numpy as np
from jax import lax
from jax.experimental import pallas as pl
from jax.experimental.pallas import tpu as pltpu

F32 = jnp.float32
BF16 = jnp.bfloat16

D_MODEL = 1024
N_MOD = 6
RMS_EPS = 1e-6
FOX_HEADS = 8
FOX_HEAD_DIM = 64
FOX_WIDTH = FOX_HEADS * FOX_HEAD_DIM
HEAD_PAIRS = FOX_HEADS // 2
SSM_WIDTH = 512
SSM_GROUP = 16
SSM_GROUPS = SSM_WIDTH // SSM_GROUP
SSM_STATE = 64
LAMBDA_RE_MAX = -1e-4
N_GROUPS = 4
EXPERTS_PER_GROUP = 8
N_EXPERTS = N_GROUPS * EXPERTS_PER_GROUP
D_EXPERT = 512

LANES = 128
SUBLANES = 8
VMEM_LIMIT = 56 * 1024 * 1024

SSM_CHUNK = 16
TOK_TILE = 512
MIX_TILE = 256
ATT_Q_TILE = 512
ATT_K_TILE = 256
ROW_BLOCK = 256
MOVE_TILE = 256
NEG_BIG = -1e30

HIGHEST = lax.Precision.HIGHEST


def _params(sem):
    return pltpu.CompilerParams(dimension_semantics=sem, vmem_limit_bytes=VMEM_LIMIT)


def _rms_modulate(x, gain, shift, scale):
    ms = jnp.mean(x * x, axis=-1, keepdims=True)
    return (x * lax.rsqrt(ms + RMS_EPS)) * gain * (1.0 + scale) + shift


def _mod_kernel(c_ref, w_ref, b_ref, o_ref):
    c = c_ref[...]
    ca = (c * jax.nn.sigmoid(c)).astype(BF16)
    o_ref[...] = jnp.dot(ca, w_ref[...].astype(BF16), preferred_element_type=F32) + b_ref[...]


def _mod(c, w_ada, b_ada):
    bsz, d = c.shape
    cols = w_ada.shape[1]
    tn = 1536
    return pl.pallas_call(
        _mod_kernel,
        grid=(cols // tn,),
        in_specs=[pl.BlockSpec((bsz, d), lambda j: (0, 0)),
                  pl.BlockSpec((d, tn), lambda j: (0, j)),
                  pl.BlockSpec((1, tn), lambda j: (0, j))],
        out_specs=pl.BlockSpec((bsz, tn), lambda j: (0, j)),
        out_shape=jax.ShapeDtypeStruct((bsz, cols), F32),
        compiler_params=_params(("arbitrary",)),
        name="mod",
    )(c, w_ada, b_ada.reshape(1, cols))


_C_Q, _C_K, _C_U, _C_GA, _C_GB, _C_F, _C_END = 0, 512, 1024, 1536, 2560, 3584, 3712


def _lane_block():
    return lax.broadcasted_iota(jnp.int32, (1, LANES), 1) // SSM_GROUP


def _to_group_major(tok_ref, flat_ref, rows):
    blk = _lane_block()
    for half in range(2):
        for j in range(SSM_WIDTH // LANES):
            w = []
            for s8 in range(8):
                v = tok_ref[j, pl.ds(8 * half + s8, rows, stride=SSM_CHUNK), :]
                w.append(pltpu.roll(v, s8 * SSM_GROUP, axis=1) if s8 else v)
            for p in range(8):
                acc = w[0]
                for s8 in range(1, 8):
                    acc = jnp.where(blk == (p + s8) % 8, w[s8], acc)
                flat_ref[8 * j + p, :, half * LANES:(half + 1) * LANES] = acc.astype(flat_ref.dtype)


def _to_token_major(flat_ref, tok_ref, rows):
    blk = _lane_block()
    for half in range(2):
        for j in range(SSM_WIDTH // LANES):
            ys = [flat_ref[8 * j + p, :, half * LANES:(half + 1) * LANES] for p in range(8)]
            for s8 in range(8):
                acc = ys[0]
                for p in range(1, 8):
                    acc = jnp.where(blk == (p + s8) % 8, ys[p], acc)
                if s8:
                    acc = pltpu.roll(acc, LANES - s8 * SSM_GROUP, axis=1)
                tok_ref[j, pl.ds(8 * half + s8, rows, stride=SSM_CHUNK), :] = acc


def _bias_lane_placement():
    pq = np.zeros((3 * LANES, HEAD_PAIRS * LANES), np.float32)
    pk = np.zeros((3 * LANES, HEAD_PAIRS * LANES), np.float32)
    bq = np.zeros((1, HEAD_PAIRS * LANES), np.float32)
    bk = np.zeros((1, HEAD_PAIRS * LANES), np.float32)
    for head in range(FOX_HEADS):
        base = (head // 2) * LANES + (head % 2) * 8
        for term in range(3):
            pq[term * LANES + head, base + term] = 1.0
            pk[term * LANES + head, base + 3 + term] = -1.0
            bq[0, base + 3 + term] = 1.0
            bk[0, base + term] = 1.0
    return pq, pk, bq, bk


def _top_bits(a):
    bits = lax.bitcast_convert_type(a, jnp.uint32) & jnp.uint32(0xFFFF0000)
    return lax.bitcast_convert_type(bits, F32)


def _inproj_kernel(tiles_per_batch, x_ref, mod_ref, g_ref, w_ref, wvt_ref, bf_ref, tri_ref,
                   pq_ref, pk_ref, bq_ref, bk_ref,
                   q_ref, k_ref, vt_ref, u_ref, uflat_ref, ga_ref, gb_ref, carry_ref, uslab_ref):
    i = pl.program_id(0)
    h = _rms_modulate(x_ref[...], g_ref[...], mod_ref[0:1, :], mod_ref[1:2, :])
    hb = h.astype(BF16)

    def proj(a, b):
        return jnp.dot(hb, w_ref[:, a:b], preferred_element_type=F32)

    q = proj(_C_Q, _C_K).astype(BF16)
    k = proj(_C_K, _C_U).astype(BF16)
    vt_ref[...] = lax.dot_general(wvt_ref[...], hb, (((1,), (1,)), ((), ())),
                                  preferred_element_type=F32).astype(BF16)
    u = proj(_C_U, _C_GA)
    u_ref[...] = u
    for j in range(SSM_WIDTH // LANES):
        uslab_ref[j] = u[:, j * LANES:(j + 1) * LANES]
    _to_group_major(uslab_ref, uflat_ref, u.shape[0] // SSM_CHUNK)
    ga_ref[...] = jax.nn.sigmoid(proj(_C_GA, _C_GB)).astype(BF16)
    gb_ref[...] = jax.nn.sigmoid(proj(_C_GB, _C_F)).astype(BF16)

    f = proj(_C_F, _C_END) + bf_ref[...]
    logf = jnp.minimum(f, 0.0) - jnp.log(1.0 + jnp.exp(-jnp.abs(f)))

    @pl.when(i % tiles_per_batch == 0)
    def _():
        carry_ref[...] = jnp.zeros_like(carry_ref)

    def split3(a):
        hi = _top_bits(a)
        r1 = a - hi
        mid = _top_bits(r1)
        return jnp.concatenate([hi, mid, _top_bits(r1 - mid)], axis=1).astype(BF16)

    part = jnp.dot(tri_ref[...], split3(logf), preferred_element_type=F32)
    cs = (part[:, :LANES] + part[:, LANES:2 * LANES] + part[:, 2 * LANES:]) + carry_ref[0:1, :]
    carry_ref[...] = jnp.broadcast_to(cs[-1:, :], carry_ref.shape)

    terms = split3(cs)
    bias_q = (jnp.dot(terms, pq_ref[...], preferred_element_type=F32) + bq_ref[...]).astype(BF16)
    bias_k = (jnp.dot(terms, pk_ref[...], preferred_element_type=F32) + bk_ref[...]).astype(BF16)
    for p in range(HEAD_PAIRS):
        lanes = slice(p * LANES, (p + 1) * LANES)
        q_ref[:, 2 * p * LANES:(2 * p + 1) * LANES] = q[:, lanes]
        q_ref[:, (2 * p + 1) * LANES:(2 * p + 2) * LANES] = bias_q[:, lanes]
        k_ref[:, 2 * p * LANES:(2 * p + 1) * LANES] = k[:, lanes]
        k_ref[:, (2 * p + 1) * LANES:(2 * p + 2) * LANES] = bias_k[:, lanes]


def _inproj(x2, mod3, gain, w_all, w_vt, bf_pad, seq):
    n, d = x2.shape
    tm = TOK_TILE
    tpb = seq // tm
    tri = jnp.tril(jnp.ones((tm, tm), BF16))
    pq, pk, bq, bk = _bias_lane_placement()
    tok = lambda w: pl.BlockSpec((tm, w), lambda i: (i, 0))
    const = lambda shape: pl.BlockSpec(shape, lambda i: (0,) * len(shape))
    qk_width = 2 * FOX_WIDTH
    return pl.pallas_call(
        functools.partial(_inproj_kernel, tpb),
        grid=(n // tm,),
        in_specs=[tok(d),
                  pl.BlockSpec((None, N_MOD, d), lambda i: (i // tpb, 0, 0)),
                  const((1, d)), const((d, _C_END)), const((FOX_WIDTH, d)), const((1, LANES)),
                  const((tm, tm)), const(pq.shape), const(pk.shape), const(bq.shape), const(bk.shape)],
        out_specs=[tok(qk_width), tok(qk_width), pl.BlockSpec((FOX_WIDTH, tm), lambda i: (0, i)),
                   tok(SSM_WIDTH),
                   pl.BlockSpec((SSM_GROUPS, tm // SSM_CHUNK, SSM_CHUNK * SSM_GROUP), lambda i: (0, i, 0)),
                   tok(d), tok(d)],
        out_shape=[jax.ShapeDtypeStruct((n, qk_width), BF16)] * 2
        + [jax.ShapeDtypeStruct((FOX_WIDTH, n), BF16)]
        + [jax.ShapeDtypeStruct((n, SSM_WIDTH), F32)]
        + [jax.ShapeDtypeStruct((SSM_GROUPS, n // SSM_CHUNK, SSM_CHUNK * SSM_GROUP), BF16)]
        + [jax.ShapeDtypeStruct((n, d), BF16)] * 2,
        scratch_shapes=[pltpu.VMEM((SUBLANES, LANES), F32),
                        pltpu.VMEM((SSM_WIDTH // LANES, tm, LANES), F32)],
        compiler_params=_params(("arbitrary",)),
        name="inproj",
    )(x2, mod3, gain, w_all, w_vt, bf_pad, tri, jnp.asarray(pq, BF16), jnp.asarray(pk, BF16),
      jnp.asarray(bq), jnp.asarray(bk))


def _attn_kernel(q_ref, k_ref, vt_ref, o_ref, m_ref, acc_ref):
    i = pl.program_id(2)
    tq, tk = ATT_Q_TILE, ATT_K_TILE
    q = q_ref[...]
    lane = lax.broadcasted_iota(jnp.int32, (1, 2 * LANES), 1)
    zq = jnp.zeros_like(q)
    half = FOX_HEAD_DIM
    own0 = (lane < half) | ((lane >= LANES) & (lane < LANES + 8))
    own1 = ((lane >= half) & (lane < LANES)) | ((lane >= LANES + 8) & (lane < LANES + 16))
    q_both = jnp.concatenate([jnp.where(own0, q, zq), jnp.where(own1, q, zq)], axis=0)
    m_ref[...] = jnp.full(m_ref.shape, NEG_BIG, F32)
    acc_ref[...] = jnp.zeros(acc_ref.shape, F32)
    ones_rows = jnp.ones((2 * SUBLANES, tk), BF16)
    key_in_tile = lax.broadcasted_iota(jnp.int32, (tk, 2 * tq), 0)
    qry_pos = i * tq + (lax.broadcasted_iota(jnp.int32, (tk, 2 * tq), 1) & (tq - 1))

    def scores(j, masked):
        start = pl.multiple_of(j * tk, tk)
        s = lax.dot_general(k_ref[pl.ds(start, tk), :], q_both, (((1,), (1,)), ((), ())),
                            preferred_element_type=F32)
        if masked:
            s = jnp.where(key_in_tile + j * tk <= qry_pos, s, NEG_BIG)
        return s

    def accumulate(j, s):
        start = pl.multiple_of(j * tk, tk)
        va = jnp.concatenate([vt_ref[:, pl.ds(start, tk)], ones_rows], axis=0)
        m_old = m_ref[...]
        m_new = jnp.maximum(m_old, jnp.max(s, axis=0, keepdims=True))
        alpha = jnp.exp(m_old - m_new)
        p = jnp.exp(s - m_new).astype(BF16)
        acc_ref[...] = alpha * acc_ref[...] + jnp.dot(va, p, preferred_element_type=F32)
        m_ref[...] = m_new

    def body(masked, j, s_cur):
        s_next = scores(j + 1, masked)
        accumulate(j, s_cur)
        return s_next

    n_full = i * (tq // tk)
    last = n_full + tq // tk - 1
    split = jnp.maximum(n_full - 1, 0)
    s_cur = scores(0, True)
    s_cur = lax.fori_loop(0, split, functools.partial(body, False), s_cur)
    s_cur = lax.fori_loop(split, last, functools.partial(body, True), s_cur)
    accumulate(last, s_cur)

    acc = acc_ref[...]
    o_t = jnp.concatenate([acc[0:half, 0:tq] / acc[LANES:LANES + 1, 0:tq],
                           acc[half:LANES, tq:2 * tq] / acc[LANES:LANES + 1, tq:2 * tq]], axis=0)
    o_ref[...] = o_t.T.astype(o_ref.dtype)


def _attention(q, k, v_t, bsz, seq):
    n = q.shape[0]
    t = ATT_Q_TILE
    nq = seq // t
    return pl.pallas_call(
        _attn_kernel,
        grid=(bsz, HEAD_PAIRS, nq),
        in_specs=[pl.BlockSpec((t, 2 * LANES), lambda b, p, i: (b * nq + i, p)),
                  pl.BlockSpec((seq, 2 * LANES), lambda b, p, i: (b, p)),
                  pl.BlockSpec((LANES, seq), lambda b, p, i: (p, b))],
        out_specs=pl.BlockSpec((t, LANES), lambda b, p, i: (b * nq + i, p)),
        out_shape=jax.ShapeDtypeStruct((n, FOX_WIDTH), BF16),
        scratch_shapes=[pltpu.VMEM((1, 2 * t), F32), pltpu.VMEM((LANES + 2 * SUBLANES, 2 * t), F32)],
        compiler_params=_params(("arbitrary", "arbitrary", "arbitrary")),
        name="attn",
    )(q, k, v_t)


def _ssm_prep_kernel(lrow_ref, lcol_ref, ldt_ref, btr_ref, bti_ref, ctr_ref, cti_ref,
                     toep_ref, bst_ref, bsw_ref, cpw_ref, a_ref):
    p8 = pl.program_id(0) % 8
    t_len, grp = SSM_CHUNK, SSM_GROUP
    dt = jnp.exp(ldt_ref[...])
    lr, li = jnp.minimum(lrow_ref[0:1, :], LAMBDA_RE_MAX), lrow_ref[1:2, :]

    def powers(steps, re, im):
        mag = jnp.exp(steps * (re * dt))
        return mag * jnp.cos(steps * (im * dt)), mag * jnp.sin(steps * (im * dt))

    a_re, a_im = powers(1.0, lr, li)
    den = lr * lr + li * li
    nr = a_re - 1.0
    co_re = (nr * lr + a_im * li) / den
    co_im = (a_im * lr - nr * li) / den
    bbt_re = co_re * btr_ref[...] - co_im * bti_ref[...]
    bbt_im = co_re * bti_ref[...] + co_im * btr_ref[...]

    lag = (lax.broadcasted_iota(jnp.int32, (1, t_len * grp), 1) // grp).astype(F32)
    lcr, lci = jnp.minimum(lcol_ref[:, 0:1], LAMBDA_RE_MAX), lcol_ref[:, 1:2]

    def c_times_power(steps):
        p_re, p_im = powers(steps, lcr, lci)
        return (ctr_ref[...] * p_re - cti_ref[...] * p_im, ctr_ref[...] * p_im + cti_ref[...] * p_re)

    wt_re, wt_im = c_times_power(lag)
    kern = (jnp.dot(bbt_re, wt_re, precision=HIGHEST, preferred_element_type=F32)
            - jnp.dot(bbt_im, wt_im, precision=HIGHEST, preferred_element_type=F32))

    lane = lax.broadcasted_iota(jnp.int32, (1, LANES), 1)
    col_shift = p8 * grp

    def store_cols(ref, rows, lo_half, hi_half):
        ref[rows, 0:LANES] = pltpu.roll(lo_half, col_shift, axis=1).astype(ref.dtype)
        ref[rows, LANES:2 * LANES] = pltpu.roll(hi_half, col_shift, axis=1).astype(ref.dtype)

    def slot_rows(s):
        half, s8 = divmod(s, 8)
        return pl.ds(pl.multiple_of((8 * half + (s8 + p8) % 8) * grp, grp), grp)

    back = (t_len - 1 - lax.broadcasted_iota(jnp.int32, (t_len, 1), 0)).astype(F32)
    e_re, e_im = powers(back, lr, li)
    zero = jnp.zeros((grp, LANES), F32)
    k_lo, k_hi = kern[:, 0:LANES], kern[:, LANES:2 * LANES]
    for s in range(t_len):
        half, s8 = divmod(s, 8)
        keep = lane >= s8 * grp
        r_lo = pltpu.roll(k_lo, s8 * grp, axis=1) if s8 else k_lo
        r_hi = pltpu.roll(k_hi, s8 * grp, axis=1) if s8 else k_hi
        if half == 0:
            lo, hi = jnp.where(keep, r_lo, 0.0), jnp.where(keep, r_hi, r_lo)
        else:
            lo, hi = zero, jnp.where(keep, r_lo, 0.0)
        store_cols(toep_ref, slot_rows(s), lo, hi)
        es_re, es_im = e_re[s:s + 1, :], e_im[s:s + 1, :]
        bs_re = es_re * bbt_re - es_im * bbt_im
        bs_im = es_re * bbt_im + es_im * bbt_re
        bst_ref[slot_rows(s), :] = jnp.concatenate([bs_re, bs_im], axis=1).astype(bst_ref.dtype)
        bsw_ref[slot_rows(s), :] = jnp.concatenate([bs_im, bs_re], axis=1).astype(bsw_ref.dtype)

    w1_re, w1_im = c_times_power(lag + 1.0)
    store_cols(cpw_ref, pl.ds(0, SSM_STATE), w1_re[:, 0:LANES], w1_re[:, LANES:2 * LANES])
    store_cols(cpw_ref, pl.ds(SSM_STATE, SSM_STATE), -w1_im[:, 0:LANES], -w1_im[:, LANES:2 * LANES])
    s_re, s_im = powers(float(t_len), lr, li)
    a_ref[0:1, :] = jnp.concatenate([s_re, s_re], axis=1)
    a_ref[1:2, :] = jnp.concatenate([-s_im, s_im], axis=1)


def _ssm_prep(lambda_re, lambda_im, log_dt, b_re, b_im, c_re, c_im):
    width = SSM_CHUNK * SSM_GROUP
    lam_row = jnp.stack([lambda_re, lambda_im], axis=1)
    tiled = lambda c: jnp.tile(c.transpose(0, 2, 1), (1, 1, SSM_CHUNK))
    per = lambda a, b: pl.BlockSpec((None, a, b), lambda g: (g, 0, 0))
    return pl.pallas_call(
        _ssm_prep_kernel,
        grid=(SSM_GROUPS,),
        in_specs=[per(2, SSM_STATE), per(SSM_STATE, 2), per(1, 1), per(SSM_GROUP, SSM_STATE),
                  per(SSM_GROUP, SSM_STATE), per(SSM_STATE, width), per(SSM_STATE, width)],
        out_specs=[per(width, width), per(width, 2 * SSM_STATE), per(width, 2 * SSM_STATE),
                   per(2 * SSM_STATE, width), per(2, 2 * SSM_STATE)],
        out_shape=[jax.ShapeDtypeStruct((SSM_GROUPS, width, width), BF16),
                   jax.ShapeDtypeStruct((SSM_GROUPS, width, 2 * SSM_STATE), BF16),
                   jax.ShapeDtypeStruct((SSM_GROUPS, width, 2 * SSM_STATE), BF16),
                   jax.ShapeDtypeStruct((SSM_GROUPS, 2 * SSM_STATE, width), BF16),
                   jax.ShapeDtypeStruct((SSM_GROUPS, 2, 2 * SSM_STATE), F32)],
        compiler_params=_params(("arbitrary",)),
        name="ssm_prep",
    )(lam_row, lam_row.transpose(0, 2, 1), log_dt.reshape(SSM_GROUPS, 1, 1),
      b_re.transpose(0, 2, 1), b_im.transpose(0, 2, 1), tiled(c_re), tiled(c_im))


def _ssm_kernel(n_chunks, bsz, u_ref, toep_ref, bst_ref, bsw_ref, cpw_ref, a_ref, y_ref,
                contrib_ref, cswap_ref, xprev_ref):
    u = u_ref[...]
    contrib_ref[...] = jnp.dot(u, bst_ref[...], preferred_element_type=F32)
    cswap_ref[...] = jnp.dot(u, bsw_ref[...], preferred_element_type=F32)
    a1 = a_ref[0:1, :]
    a2 = a_ref[1:2, :]

    def step(n, carry):
        x, xs = carry
        rows = pl.ds(n, bsz, stride=n_chunks)
        xprev_ref[rows, :] = x
        x_new = a1 * x + a2 * xs + contrib_ref[rows, :]
        xs_new = a1 * xs - a2 * x + cswap_ref[rows, :]
        return x_new, xs_new

    zero = jnp.zeros((bsz, 2 * SSM_STATE), F32)
    lax.fori_loop(0, n_chunks, step, (zero, zero), unroll=4)
    y_ref[...] = (jnp.dot(u, toep_ref[...], preferred_element_type=F32)
                  + jnp.dot(xprev_ref[...].astype(BF16), cpw_ref[...], preferred_element_type=F32))


def _ssm(u_flat, toep, b_state, b_swap, c_pow, a_step, bsz):
    g, rows, w = u_flat.shape
    per = lambda a, b: pl.BlockSpec((None, a, b), lambda i: (i, 0, 0))
    state = pltpu.VMEM((rows, 2 * SSM_STATE), F32)
    return pl.pallas_call(
        functools.partial(_ssm_kernel, rows // bsz, bsz),
        grid=(g,),
        in_specs=[per(rows, w), per(w, w), per(w, 2 * SSM_STATE), per(w, 2 * SSM_STATE),
                  per(2 * SSM_STATE, w), per(2, 2 * SSM_STATE)],
        out_specs=per(rows, w),
        out_shape=jax.ShapeDtypeStruct((g, rows, w), F32),
        scratch_shapes=[state, state, state],
        compiler_params=_params(("arbitrary",)),
        name="ssm",
    )(u_flat, toep, b_state, b_swap, c_pow, a_step)


ROW_SLABS = D_MODEL // LANES // 2
_HIGH_HALF = 0xFFFF0000


def _store_row_tiles(ref, value):
    rows = value.shape[0]
    bits = lax.bitcast_convert_type(value.astype(BF16).astype(F32), jnp.uint32)
    for j in range(ROW_SLABS):
        low = bits[:, j * LANES:(j + 1) * LANES] >> 16
        high = bits[:, (j + ROW_SLABS) * LANES:(j + ROW_SLABS + 1) * LANES] & jnp.uint32(_HIGH_HALF)
        ref[pl.ds(j, rows, stride=ROW_SLABS), :] = high | low


def _load_row_tiles(ref, rows):
    words = [ref[pl.ds(j, rows, stride=ROW_SLABS), :] for j in range(ROW_SLABS)]
    low = [lax.bitcast_convert_type(w << 16, F32) for w in words]
    high = [lax.bitcast_convert_type(w & jnp.uint32(_HIGH_HALF), F32) for w in words]
    return jnp.concatenate(low + high, axis=1)


def _row_tile_copy(src_ref, src_row, dst_ref, dst_row, sem):
    src = src_ref.at[pl.ds(pl.multiple_of(src_row * ROW_SLABS, ROW_SLABS), ROW_SLABS), :]
    dst = dst_ref.at[pl.ds(pl.multiple_of(dst_row * ROW_SLABS, ROW_SLABS), ROW_SLABS), :]
    return pltpu.make_async_copy(src, dst, sem)


def _mix_kernel(x_ref, of_ref, yf_ref, u_ref, ga_ref, gb_ref, mod_ref, dsk_ref, wglu_ref, wfox_ref,
                wssm_ref, wo_ref, g2_ref, wr_ref, br_ref, x1_ref, h2_ref, lg_ref, ytok_ref):
    _to_token_major(yf_ref, ytok_ref, yf_ref.shape[1])
    y_ssm = jnp.concatenate([ytok_ref[j] for j in range(SSM_WIDTH // LANES)], axis=1)
    y = y_ssm + dsk_ref[...] * u_ref[...]
    y = 0.5 * y * (1.0 + jnp.tanh(math.sqrt(2.0 / math.pi) * (y + 0.044715 * (y * y * y))))
    gl = jnp.dot(y.astype(BF16), wglu_ref[...], preferred_element_type=F32)
    o_ssm = gl[:, :SSM_WIDTH] * jax.nn.sigmoid(gl[:, SSM_WIDTH:])
    merged = (ga_ref[...].astype(F32) * jnp.dot(of_ref[...], wfox_ref[...], preferred_element_type=F32)
              + gb_ref[...].astype(F32) * jnp.dot(o_ssm.astype(BF16), wssm_ref[...],
                                                  preferred_element_type=F32))
    x1 = x_ref[...] + mod_ref[2:3, :] * jnp.dot(merged.astype(BF16), wo_ref[...],
                                                 preferred_element_type=F32)
    x1_ref[...] = x1
    h2 = _rms_modulate(x1, g2_ref[...], mod_ref[3:4, :], mod_ref[4:5, :])
    _store_row_tiles(h2_ref, h2)
    a1 = _top_bits(h2)
    a2 = _top_bits(h2 - a1)
    lhs = jnp.concatenate([a1, a1, a2], axis=1).astype(BF16)
    lg_ref[...] = jnp.dot(lhs, wr_ref[...], preferred_element_type=F32) + br_ref[...]


def _mix(x2, o_fox, y_flat, u, sga, sgb, mod3, d_skip, w_glu, w_fox, w_ssm, w_o, g2, w_r, b_r, seq):
    n, d = x2.shape
    tm = MIX_TILE
    tpb = seq // tm
    tok = lambda w: pl.BlockSpec((tm, w), lambda i: (i, 0))
    const = lambda a: pl.BlockSpec(a.shape, lambda i: (0,) * a.ndim)
    flat = pl.BlockSpec((SSM_GROUPS, tm // SSM_CHUNK, SSM_CHUNK * SSM_GROUP), lambda i: (0, i, 0))
    return pl.pallas_call(
        _mix_kernel,
        grid=(n // tm,),
        in_specs=[tok(d), tok(FOX_WIDTH), flat, tok(SSM_WIDTH), tok(d), tok(d),
                  pl.BlockSpec((None, N_MOD, d), lambda i: (i // tpb, 0, 0)),
                  const(d_skip), const(w_glu), const(w_fox), const(w_ssm), const(w_o), const(g2),
                  const(w_r), const(b_r)],
        out_specs=[tok(d), pl.BlockSpec((tm * ROW_SLABS, LANES), lambda i: (i, 0)), tok(LANES)],
        out_shape=[jax.ShapeDtypeStruct((n, d), F32), jax.ShapeDtypeStruct((n * ROW_SLABS, LANES), jnp.uint32),
                   jax.ShapeDtypeStruct((n, LANES), F32)],
        scratch_shapes=[pltpu.VMEM((SSM_WIDTH // LANES, tm, LANES), F32)],
        compiler_params=_params(("arbitrary",)),
        name="mix",
    )(x2, o_fox, y_flat, u, sga, sgb, mod3, d_skip, w_glu, w_fox, w_ssm, w_o, g2, w_r, b_r)


def _route_kernel(lg_ref, tri_ref, idx_ref, wt_ref, cnt_ref, carry_ref):
    i = pl.program_id(0)

    @pl.when(i == 0)
    def _():
        carry_ref[...] = jnp.zeros_like(carry_ref)

    lg = lg_ref[...]
    tm = lg.shape[0]
    lane = lax.broadcasted_iota(jnp.int32, (tm, LANES), 1)
    neg = jnp.full_like(lg, -jnp.inf)

    def first_argmax(vals):
        mx = jnp.max(vals, axis=1, keepdims=True)
        ix = jnp.min(jnp.where(vals == mx, lane, LANES), axis=1, keepdims=True)
        return mx, ix

    is_group = lane < N_GROUPS
    g_max, gi = first_argmax(jnp.where(is_group, lg, neg))
    g_sum = jnp.sum(jnp.where(is_group, jnp.exp(lg - g_max), 0.0), axis=1, keepdims=True)
    p_group = 1.0 / g_sum
    lo = N_GROUPS + EXPERTS_PER_GROUP * gi
    in_group = (lane >= lo) & (lane < lo + EXPERTS_PER_GROUP)
    cand = jnp.where(in_group, lg, neg)
    v1, i1 = first_argmax(cand)
    v2, i2 = first_argmax(jnp.where(lane == i1, neg, cand))
    tt = jnp.exp(v2 - v1)
    w1 = p_group / (1.0 + tt)
    w2 = p_group * tt / (1.0 + tt)
    e1 = i1 - N_GROUPS
    e2 = i2 - N_GROUPS
    sel1 = lane == e1
    sel2 = lane == e2
    onehot = (sel1 | sel2).astype(F32)
    before = jnp.dot(tri_ref[...], onehot.astype(BF16), preferred_element_type=F32) + carry_ref[0:1, :]
    r1 = jnp.sum(jnp.where(sel1, before, 0.0), axis=1, keepdims=True).astype(jnp.int32)
    r2 = jnp.sum(jnp.where(sel2, before, 0.0), axis=1, keepdims=True).astype(jnp.int32)
    total = before[-1:, :] + onehot[-1:, :]
    carry_ref[...] = jnp.broadcast_to(total, carry_ref.shape)
    cnt_ref[...] = jnp.broadcast_to(total, cnt_ref.shape)
    idx_ref[...] = jnp.where(lane == 0, e1, jnp.where(lane == 1, e2, jnp.where(lane == 2, r1, r2)))
    wt_ref[...] = jnp.where(lane == 0, w1, w2)


def _route(logits):
    n = logits.shape[0]
    tm = TOK_TILE
    tri = jnp.tril(jnp.ones((tm, tm), BF16), k=-1)
    tok = pl.BlockSpec((tm, LANES), lambda i: (i, 0))
    return pl.pallas_call(
        _route_kernel,
        grid=(n // tm,),
        in_specs=[tok, pl.BlockSpec((tm, tm), lambda i: (0, 0))],
        out_specs=[tok, tok, pl.BlockSpec((SUBLANES, LANES), lambda i: (0, 0))],
        out_shape=[jax.ShapeDtypeStruct((n, LANES), jnp.int32), jax.ShapeDtypeStruct((n, LANES), F32),
                   jax.ShapeDtypeStruct((SUBLANES, LANES), F32)],
        scratch_shapes=[pltpu.VMEM((SUBLANES, LANES), F32)],
        compiler_params=_params(("arbitrary",)),
        name="route",
    )(logits, tri)


ISSUE_UNROLL = 8


def _dispatch_kernel(dest_ref, h_ref, rows_in_ref, rows_ref, sem):
    del rows_in_ref
    tm = h_ref.shape[0] // ROW_SLABS

    def issue(g, c):
        for j in range(ISSUE_UNROLL):
            t = g * ISSUE_UNROLL + j
            _row_tile_copy(h_ref, t, rows_ref, dest_ref[0, 0, t], sem).start(priority=0)
            _row_tile_copy(h_ref, t, rows_ref, dest_ref[0, 0, tm + t], sem).start(priority=1)
        return c

    lax.fori_loop(0, tm // ISSUE_UNROLL, issue, 0)
    for _ in range(2):
        pltpu.make_async_copy(h_ref, rows_ref.at[pl.ds(0, tm * ROW_SLABS), :], sem).wait()


def _dispatch(dest3, h2_tiles, rows_zero):
    tm = MOVE_TILE
    n = h2_tiles.shape[0] // ROW_SLABS
    return pl.pallas_call(
        _dispatch_kernel,
        grid=(n // tm,),
        in_specs=[pl.BlockSpec((1, 1, 2 * tm), lambda i: (i, 0, 0), memory_space=pltpu.SMEM),
                  pl.BlockSpec((tm * ROW_SLABS, LANES), lambda i: (i, 0)),
                  pl.BlockSpec(memory_space=pl.ANY)],
        out_specs=pl.BlockSpec(memory_space=pl.ANY),
        out_shape=jax.ShapeDtypeStruct(rows_zero.shape, rows_zero.dtype),
        scratch_shapes=[pltpu.SemaphoreType.DMA(())],
        input_output_aliases={2: 0},
        compiler_params=_params(("arbitrary",)),
        name="dispatch",
    )(dest3, h2_tiles, rows_zero)


def _combine_kernel(n_steps, dest_ref, dnext_ref, x1_ref, wt_ref, mod_ref, gf_ref, yr_ref, o_ref,
                    buf_ref, sem):
    i = pl.program_id(0)
    tm = x1_ref.shape[0]

    def gather(idx_ref, which):
        def issue(g, c):
            for j in range(ISSUE_UNROLL):
                t = g * ISSUE_UNROLL + j
                _row_tile_copy(yr_ref, idx_ref[0, 0, t], buf_ref.at[which, 0], t,
                               sem.at[which]).start(priority=0)
                _row_tile_copy(yr_ref, idx_ref[0, 0, tm + t], buf_ref.at[which, 1], t,
                               sem.at[which]).start(priority=1)
            return c

        lax.fori_loop(0, tm // ISSUE_UNROLL, issue, 0)

    cur = i % 2

    @pl.when(i == 0)
    def _():
        gather(dest_ref, 0)

    @pl.when(i + 1 < n_steps)
    def _():
        gather(dnext_ref, 1 - cur)

    for slot in range(2):
        pltpu.make_async_copy(yr_ref.at[pl.ds(0, tm * ROW_SLABS), :], buf_ref.at[cur, slot],
                              sem.at[cur]).wait()
    wt = wt_ref[...]
    moe = (wt[:, 0:1] * _load_row_tiles(buf_ref.at[cur, 0], tm)
           + wt[:, 1:2] * _load_row_tiles(buf_ref.at[cur, 1], tm))
    x = x1_ref[...] + mod_ref[5:6, :] * moe
    ms = jnp.mean(x * x, axis=-1, keepdims=True)
    o_ref[...] = (x * lax.rsqrt(ms + RMS_EPS)) * gf_ref[...]


def _combine(dest3, x1, wts, mod3, final_g, y_rows, seq):
    n, d = x1.shape
    tm = MOVE_TILE
    tpb = seq // tm
    n_steps = n // tm
    idx_spec = lambda f: pl.BlockSpec((1, 1, 2 * tm), f, memory_space=pltpu.SMEM)
    return pl.pallas_call(
        functools.partial(_combine_kernel, n_steps),
        grid=(n_steps,),
        in_specs=[idx_spec(lambda i: (i, 0, 0)),
                  idx_spec(lambda i: (jnp.minimum(i + 1, n_steps - 1), 0, 0)),
                  pl.BlockSpec((tm, d), lambda i: (i, 0)),
                  pl.BlockSpec((tm, LANES), lambda i: (i, 0)),
                  pl.BlockSpec((None, N_MOD, d), lambda i: (i // tpb, 0, 0)),
                  pl.BlockSpec((1, d), lambda i: (0, 0)),
                  pl.BlockSpec(memory_space=pl.ANY)],
        out_specs=pl.BlockSpec((tm, d), lambda i: (i, 0)),
        out_shape=jax.ShapeDtypeStruct((n, d), F32),
        scratch_shapes=[pltpu.VMEM((2, 2, tm * ROW_SLABS, LANES), jnp.uint32),
                        pltpu.SemaphoreType.DMA((2,))],
        compiler_params=_params(("arbitrary",)),
        name="combine",
    )(dest3, dest3, x1, wts, mod3, final_g, y_rows)


def _expert_kernel(be_ref, nv_ref, x_ref, wg_ref, wu_ref, wd_ref, y_ref, wgb_ref, wub_ref, wdb_ref):
    i = pl.program_id(0)
    valid = i < nv_ref[0]
    new_expert = (i == 0) | (be_ref[i] != be_ref[jnp.maximum(i - 1, 0)])

    @pl.when(valid & new_expert)
    def _():
        wgb_ref[...] = wg_ref[...].astype(BF16)
        wub_ref[...] = wu_ref[...].astype(BF16)
        wdb_ref[...] = wd_ref[...].astype(BF16)

    @pl.when(valid)
    def _():
        xb = _load_row_tiles(x_ref, x_ref.shape[0] // ROW_SLABS).astype(BF16)
        a = jnp.dot(xb, wgb_ref[...], preferred_element_type=F32)
        b = jnp.dot(xb, wub_ref[...], preferred_element_type=F32)
        hid = (a * jax.nn.sigmoid(a)) * b
        _store_row_tiles(y_ref, jnp.dot(hid.astype(BF16), wdb_ref[...], preferred_element_type=F32))

    @pl.when(jnp.logical_not(valid))
    def _():
        y_ref[...] = jnp.zeros_like(y_ref)


def _experts(blk_e, n_valid, x_rows, w_gate, w_up, w_down):
    d = D_MODEL
    rows = x_rows.shape[0] // ROW_SLABS
    tb = ROW_BLOCK
    grid_spec = pltpu.PrefetchScalarGridSpec(
        num_scalar_prefetch=2,
        grid=(rows // tb,),
        in_specs=[pl.BlockSpec((tb * ROW_SLABS, LANES), lambda i, be, nv: (i, 0)),
                  pl.BlockSpec((None, d, D_EXPERT), lambda i, be, nv: (be[i], 0, 0)),
                  pl.BlockSpec((None, d, D_EXPERT), lambda i, be, nv: (be[i], 0, 0)),
                  pl.BlockSpec((None, D_EXPERT, d), lambda i, be, nv: (be[i], 0, 0))],
        out_specs=pl.BlockSpec((tb * ROW_SLABS, LANES), lambda i, be, nv: (i, 0)),
        scratch_shapes=[pltpu.VMEM((d, D_EXPERT), BF16), pltpu.VMEM((d, D_EXPERT), BF16),
                        pltpu.VMEM((D_EXPERT, d), BF16)],
    )
    return pl.pallas_call(
        _expert_kernel,
        grid_spec=grid_spec,
        out_shape=jax.ShapeDtypeStruct(x_rows.shape, x_rows.dtype),
        compiler_params=_params(("arbitrary",)),
        name="experts",
    )(blk_e, n_valid, x_rows, w_gate, w_up, w_down)


def kernel(x, c, w_ada, b_ada, norm_mix_g, w_in, b_forget, w_out_fox, lambda_re, lambda_im, log_dt,
           ssm_b_re, ssm_b_im, ssm_c_re, ssm_c_im, d_skip, w_glu, w_out_ssm, w_o, norm_ffn_g,
           w_router_group, b_router_group, w_router_expert, b_router_expert, w_gate_e, w_up_e,
           w_down_e, final_g):
    bsz, seq, d = x.shape
    n = bsz * seq
    assert w_ada.shape[0] == 1, "the final RMSNorm is fused into the (single) layer's combine kernel"
    xc = x.reshape(n, d)
    for l in range(1):
        mod3 = _mod(c, w_ada[l], b_ada[l]).reshape(bsz, N_MOD, d)

        wi = w_in[l]
        s_q, s_k, s_v, s_f, s_u, s_ga = 512, 1024, 1536, 1544, 2056, 3080
        scale = FOX_HEAD_DIM ** -0.5
        w_all = jnp.concatenate(
            [wi[:, :s_q] * scale, wi[:, s_q:s_k], wi[:, s_f:s_u], wi[:, s_u:s_ga],
             wi[:, s_ga:], jnp.pad(wi[:, s_v:s_f], ((0, 0), (0, LANES - FOX_HEADS)))],
            axis=1).astype(BF16)
        w_vt = wi[:, s_k:s_v].T.astype(BF16)
        bf_pad = jnp.pad(b_forget[l], (0, LANES - FOX_HEADS)).reshape(1, LANES)
        q, k, v_t, u, u_flat, sga, sgb = _inproj(xc, mod3, norm_mix_g[l].reshape(1, d), w_all, w_vt,
                                                 bf_pad, seq)

        o_fox = _attention(q, k, v_t, bsz, seq)

        toep, b_state, b_swap, c_pow, a_step = _ssm_prep(
            lambda_re[l], lambda_im[l], log_dt[l], ssm_b_re[l], ssm_b_im[l], ssm_c_re[l], ssm_c_im[l])
        y_flat = _ssm(u_flat, toep, b_state, b_swap, c_pow, a_step, bsz)

        w_r = jnp.pad(jnp.concatenate([w_router_group[l], w_router_expert[l]], axis=1),
                      ((0, 0), (0, LANES - N_GROUPS - N_EXPERTS)))
        w_r1 = _top_bits(w_r)
        w_r2 = _top_bits(w_r - w_r1)
        w_r = jnp.concatenate([w_r1, w_r2, w_r1], axis=0).astype(BF16)
        b_r = jnp.pad(jnp.concatenate([b_router_group[l], b_router_expert[l]]),
                      (0, LANES - N_GROUPS - N_EXPERTS)).reshape(1, LANES)
        x1, h2, logits = _mix(xc, o_fox, y_flat, u, sga, sgb, mod3, d_skip[l].reshape(1, SSM_WIDTH),
                              w_glu[l].astype(BF16), w_out_fox[l].astype(BF16),
                              w_out_ssm[l].astype(BF16), w_o[l].astype(BF16),
                              norm_ffn_g[l].reshape(1, d), w_r, b_r, seq)

        idx, wts, cnt = _route(logits)
        counts = cnt[0, :N_EXPERTS].astype(jnp.int32)
        pcounts = ((counts + ROW_BLOCK - 1) // ROW_BLOCK) * ROW_BLOCK
        pends = jnp.cumsum(pcounts)
        pstarts = pends - pcounts
        er = idx[:, 0:4].T
        hit = er[0:2, None, :] == jnp.arange(N_EXPERTS, dtype=jnp.int32)[None, :, None]
        dest = jnp.sum(jnp.where(hit, pstarts[None, :, None], 0), axis=1) + er[2:4]
        rows = 2 * n + N_EXPERTS * ROW_BLOCK
        n_blocks = rows // ROW_BLOCK
        blk_start = jnp.arange(n_blocks, dtype=jnp.int32) * ROW_BLOCK
        blk_e = jnp.minimum(jnp.sum((pends[None, :] <= blk_start[:, None]).astype(jnp.int32), axis=1),
                            N_EXPERTS - 1)
        n_valid = (pends[-1:] // ROW_BLOCK).astype(jnp.int32)
        dest3 = (dest.astype(jnp.int32).reshape(2, n // MOVE_TILE, MOVE_TILE).transpose(1, 0, 2)
                 .reshape(n // MOVE_TILE, 1, 2 * MOVE_TILE))

        x_rows = _dispatch(dest3, h2, jnp.zeros((rows * ROW_SLABS, LANES), jnp.uint32))
        y_rows = _experts(blk_e, n_valid, x_rows, w_gate_e[l], w_up_e[l], w_down_e[l])
        xc = _combine(dest3, x1, wts, mod3, final_g.reshape(1, d), y_rows, seq)
    return xc.reshape(bsz, seq, d)
```

```python
import functools
import math

import jax
import jax.numpy as jnp
import numpy as np
from jax import lax
from jax.experimental import pallas as pl
from jax.experimental.pallas import tpu as pltpu

F32 = jnp.float32
BF16 = jnp.bfloat16

D_MODEL = 1024
N_MOD = 6
RMS_EPS = 1e-6
FOX_HEADS = 8
FOX_HEAD_DIM = 64
FOX_WIDTH = FOX_HEADS * FOX_HEAD_DIM
HEAD_PAIRS = FOX_HEADS // 2
SSM_WIDTH = 512
SSM_GROUP = 16
SSM_GROUPS = SSM_WIDTH // SSM_GROUP
SSM_STATE = 64
LAMBDA_RE_MAX = -1e-4
N_GROUPS = 4
EXPERTS_PER_GROUP = 8
N_EXPERTS = N_GROUPS * EXPERTS_PER_GROUP
D_EXPERT = 512

LANES = 128
SUBLANES = 8
VMEM_LIMIT = 56 * 1024 * 1024

SSM_CHUNK = 16
TOK_TILE = 512
MIX_TILE = 256
ATT_Q_TILE = 512
ATT_K_TILE = 256
ROW_BLOCK = 256
MOVE_TILE = 256
NEG_BIG = -1e30

HIGHEST = lax.Precision.HIGHEST


def _params(sem):
    return pltpu.CompilerParams(dimension_semantics=sem, vmem_limit_bytes=VMEM_LIMIT)


def _rms_modulate(x, gain, shift, scale):
    ms = jnp.mean(x * x, axis=-1, keepdims=True)
    return (x * lax.rsqrt(ms + RMS_EPS)) * gain * (1.0 + scale) + shift


def _mod_kernel(c_ref, w_ref, b_ref, o_ref):
    c = c_ref[...]
    ca = (c * jax.nn.sigmoid(c)).astype(BF16)
    o_ref[...] = jnp.dot(ca, w_ref[...].astype(BF16), preferred_element_type=F32) + b_ref[...]


def _mod(c, w_ada, b_ada):
    bsz, d = c.shape
    cols = w_ada.shape[1]
    tn = 1536
    return pl.pallas_call(
        _mod_kernel,
        grid=(cols // tn,),
        in_specs=[pl.BlockSpec((bsz, d), lambda j: (0, 0)),
                  pl.BlockSpec((d, tn), lambda j: (0, j)),
                  pl.BlockSpec((1, tn), lambda j: (0, j))],
        out_specs=pl.BlockSpec((bsz, tn), lambda j: (0, j)),
        out_shape=jax.ShapeDtypeStruct((bsz, cols), F32),
        compiler_params=_params(("arbitrary",)),
        name="mod",
    )(c, w_ada, b_ada.reshape(1, cols))


_C_Q, _C_K, _C_U, _C_GA, _C_GB, _C_F, _C_END = 0, 512, 1024, 1536, 2560, 3584, 3712


def _lane_block():
    return lax.broadcasted_iota(jnp.int32, (1, LANES), 1) // SSM_GROUP


def _to_group_major(tok_ref, flat_ref, rows):
    blk = _lane_block()
    for half in range(2):
        for j in range(SSM_WIDTH // LANES):
            w = []
            for s8 in range(8):
                v = tok_ref[j, pl.ds(8 * half + s8, rows, stride=SSM_CHUNK), :]
                w.append(pltpu.roll(v, s8 * SSM_GROUP, axis=1) if s8 else v)
            for p in range(8):
                acc = w[0]
                for s8 in range(1, 8):
                    acc = jnp.where(blk == (p + s8) % 8, w[s8], acc)
                flat_ref[8 * j + p, :, half * LANES:(half + 1) * LANES] = acc.astype(flat_ref.dtype)


def _to_token_major(flat_ref, tok_ref, rows):
    blk = _lane_block()
    for half in range(2):
        for j in range(SSM_WIDTH // LANES):
            ys = [flat_ref[8 * j + p, :, half * LANES:(half + 1) * LANES] for p in range(8)]
            for s8 in range(8):
                acc = ys[0]
                for p in range(1, 8):
                    acc = jnp.where(blk == (p + s8) % 8, ys[p], acc)
                if s8:
                    acc = pltpu.roll(acc, LANES - s8 * SSM_GROUP, axis=1)
                tok_ref[j, pl.ds(8 * half + s8, rows, stride=SSM_CHUNK), :] = acc


def _bias_lane_placement():
    pq = np.zeros((3 * LANES, HEAD_PAIRS * LANES), np.float32)
    pk = np.zeros((3 * LANES, HEAD_PAIRS * LANES), np.float32)
    bq = np.zeros((1, HEAD_PAIRS * LANES), np.float32)
    bk = np.zeros((1, HEAD_PAIRS * LANES), np.float32)
    for head in range(FOX_HEADS):
        base = (head // 2) * LANES + (head % 2) * 8
        for term in range(3):
            pq[term * LANES + head, base + term] = 1.0
            pk[term * LANES + head, base + 3 + term] = -1.0
            bq[0, base + 3 + term] = 1.0
            bk[0, base + term] = 1.0
    return pq, pk, bq, bk


def _top_bits(a):
    bits = lax.bitcast_convert_type(a, jnp.uint32) & jnp.uint32(0xFFFF0000)
    return lax.bitcast_convert_type(bits, F32)


def _inproj_kernel(tiles_per_batch, x_ref, mod_ref, g_ref, w_ref, wvt_ref, bf_ref, tri_ref,
                   pq_ref, pk_ref, bq_ref, bk_ref,
                   q_ref, k_ref, vt_ref, u_ref, uflat_ref, ga_ref, gb_ref, carry_ref, uslab_ref):
    i = pl.program_id(0)
    h = _rms_modulate(x_ref[...], g_ref[...], mod_ref[0:1, :], mod_ref[1:2, :])
    hb = h.astype(BF16)

    def proj(a, b):
        return jnp.dot(hb, w_ref[:, a:b], preferred_element_type=F32)

    q = proj(_C_Q, _C_K).astype(BF16)
    k = proj(_C_K, _C_U).astype(BF16)
    vt_ref[...] = lax.dot_general(wvt_ref[...], hb, (((1,), (1,)), ((), ())),
                                  preferred_element_type=F32).astype(BF16)
    u = proj(_C_U, _C_GA)
    u_ref[...] = u
    for j in range(SSM_WIDTH // LANES):
        uslab_ref[j] = u[:, j * LANES:(j + 1) * LANES]
    _to_group_major(uslab_ref, uflat_ref, u.shape[0] // SSM_CHUNK)
    ga_ref[...] = jax.nn.sigmoid(proj(_C_GA, _C_GB)).astype(BF16)
    gb_ref[...] = jax.nn.sigmoid(proj(_C_GB, _C_F)).astype(BF16)

    f = proj(_C_F, _C_END) + bf_ref[...]
    logf = jnp.minimum(f, 0.0) - jnp.log(1.0 + jnp.exp(-jnp.abs(f)))

    @pl.when(i % tiles_per_batch == 0)
    def _():
        carry_ref[...] = jnp.zeros_like(carry_ref)

    def split3(a):
        hi = _top_bits(a)
        r1 = a - hi
        mid = _top_bits(r1)
        return jnp.concatenate([hi, mid, _top_bits(r1 - mid)], axis=1).astype(BF16)

    part = jnp.dot(tri_ref[...], split3(logf), preferred_element_type=F32)
    cs = (part[:, :LANES] + part[:, LANES:2 * LANES] + part[:, 2 * LANES:]) + carry_ref[0:1, :]
    carry_ref[...] = jnp.broadcast_to(cs[-1:, :], carry_ref.shape)

    terms = split3(cs)
    bias_q = (jnp.dot(terms, pq_ref[...], preferred_element_type=F32) + bq_ref[...]).astype(BF16)
    bias_k = (jnp.dot(terms, pk_ref[...], preferred_element_type=F32) + bk_ref[...]).astype(BF16)
    for p in range(HEAD_PAIRS):
        lanes = slice(p * LANES, (p + 1) * LANES)
        q_ref[:, 2 * p * LANES:(2 * p + 1) * LANES] = q[:, lanes]
        q_ref[:, (2 * p + 1) * LANES:(2 * p + 2) * LANES] = bias_q[:, lanes]
        k_ref[:, 2 * p * LANES:(2 * p + 1) * LANES] = k[:, lanes]
        k_ref[:, (2 * p + 1) * LANES:(2 * p + 2) * LANES] = bias_k[:, lanes]


def _inproj(x2, mod3, gain, w_all, w_vt, bf_pad, seq):
    n, d = x2.shape
    tm = TOK_TILE
    tpb = seq // tm
    tri = jnp.tril(jnp.ones((tm, tm), BF16))
    pq, pk, bq, bk = _bias_lane_placement()
    tok = lambda w: pl.BlockSpec((tm, w), lambda i: (i, 0))
    const = lambda shape: pl.BlockSpec(shape, lambda i: (0,) * len(shape))
    qk_width = 2 * FOX_WIDTH
    return pl.pallas_call(
        functools.partial(_inproj_kernel, tpb),
        grid=(n // tm,),
        in_specs=[tok(d),
                  pl.BlockSpec((None, N_MOD, d), lambda i: (i // tpb, 0, 0)),
                  const((1, d)), const((d, _C_END)), const((FOX_WIDTH, d)), const((1, LANES)),
                  const((tm, tm)), const(pq.shape), const(pk.shape), const(bq.shape), const(bk.shape)],
        out_specs=[tok(qk_width), tok(qk_width), pl.BlockSpec((FOX_WIDTH, tm), lambda i: (0, i)),
                   tok(SSM_WIDTH),
                   pl.BlockSpec((SSM_GROUPS, tm // SSM_CHUNK, SSM_CHUNK * SSM_GROUP), lambda i: (0, i, 0)),
                   tok(d), tok(d)],
        out_shape=[jax.ShapeDtypeStruct((n, qk_width), BF16)] * 2
        + [jax.ShapeDtypeStruct((FOX_WIDTH, n), BF16)]
        + [jax.ShapeDtypeStruct((n, SSM_WIDTH), F32)]
        + [jax.ShapeDtypeStruct((SSM_GROUPS, n // SSM_CHUNK, SSM_CHUNK * SSM_GROUP), BF16)]
        + [jax.ShapeDtypeStruct((n, d), BF16)] * 2,
        scratch_shapes=[pltpu.VMEM((SUBLANES, LANES), F32),
                        pltpu.VMEM((SSM_WIDTH // LANES, tm, LANES), F32)],
        compiler_params=_params(("arbitrary",)),
        name="inproj",
    )(x2, mod3, gain, w_all, w_vt, bf_pad, tri, jnp.asarray(pq, BF16), jnp.asarray(pk, BF16),
      jnp.asarray(bq), jnp.asarray(bk))


def _attn_kernel(q_ref, k_ref, vt_ref, o_ref, m_ref, acc_ref):
    i = pl.program_id(2)
    tq, tk = ATT_Q_TILE, ATT_K_TILE
    q = q_ref[...]
    lane = lax.broadcasted_iota(jnp.int32, (1, 2 * LANES), 1)
    zq = jnp.zeros_like(q)
    half = FOX_HEAD_DIM
    own0 = (lane < half) | ((lane >= LANES) & (lane < LANES + 8))
    own1 = ((lane >= half) & (lane < LANES)) | ((lane >= LANES + 8) & (lane < LANES + 16))
    q_both = jnp.concatenate([jnp.where(own0, q, zq), jnp.where(own1, q, zq)], axis=0)
    m_ref[...] = jnp.full(m_ref.shape, NEG_BIG, F32)
    acc_ref[...] = jnp.zeros(acc_ref.shape, F32)
    ones_rows = jnp.ones((2 * SUBLANES, tk), BF16)
    key_in_tile = lax.broadcasted_iota(jnp.int32, (tk, 2 * tq), 0)
    qry_pos = i * tq + (lax.broadcasted_iota(jnp.int32, (tk, 2 * tq), 1) & (tq - 1))

    def scores(j):
        start = pl.multiple_of(j * tk, tk)
        s = lax.dot_general(k_ref[pl.ds(start, tk), :], q_both, (((1,), (1,)), ((), ())),
                            preferred_element_type=F32)
        return jnp.where(key_in_tile + j * tk <= qry_pos, s, NEG_BIG)

    def accumulate(j, s):
        start = pl.multiple_of(j * tk, tk)
        va = jnp.concatenate([vt_ref[:, pl.ds(start, tk)], ones_rows], axis=0)
        m_old = m_ref[...]
        m_new = jnp.maximum(m_old, jnp.max(s, axis=0, keepdims=True))
        alpha = jnp.exp(m_old - m_new)
        p = jnp.exp(s - m_new).astype(BF16)
        acc_ref[...] = alpha * acc_ref[...] + jnp.dot(va, p, preferred_element_type=F32)
        m_ref[...] = m_new

    def body(j, s_cur):
        s_next = scores(j + 1)
        accumulate(j, s_cur)
        return s_next

    last = (i + 1) * (tq // tk) - 1
    s_last = lax.fori_loop(0, last, body, scores(0))
    accumulate(last, s_last)

    acc = acc_ref[...]
    o_t = jnp.concatenate([acc[0:half, 0:tq] / acc[LANES:LANES + 1, 0:tq],
                           acc[half:LANES, tq:2 * tq] / acc[LANES:LANES + 1, tq:2 * tq]], axis=0)
    o_ref[...] = o_t.T.astype(o_ref.dtype)


def _attention(q, k, v_t, bsz, seq):
    n = q.shape[0]
    t = ATT_Q_TILE
    nq = seq // t
    return pl.pallas_call(
        _attn_kernel,
        grid=(bsz, HEAD_PAIRS, nq),
        in_specs=[pl.BlockSpec((t, 2 * LANES), lambda b, p, i: (b * nq + i, p)),
                  pl.BlockSpec((seq, 2 * LANES), lambda b, p, i: (b, p)),
                  pl.BlockSpec((LANES, seq), lambda b, p, i: (p, b))],
        out_specs=pl.BlockSpec((t, LANES), lambda b, p, i: (b * nq + i, p)),
        out_shape=jax.ShapeDtypeStruct((n, FOX_WIDTH), BF16),
        scratch_shapes=[pltpu.VMEM((1, 2 * t), F32), pltpu.VMEM((LANES + 2 * SUBLANES, 2 * t), F32)],
        compiler_params=_params(("arbitrary", "arbitrary", "arbitrary")),
        name="attn",
    )(q, k, v_t)


def _ssm_prep_kernel(lrow_ref, lcol_ref, ldt_ref, btr_ref, bti_ref, ctr_ref, cti_ref,
                     toep_ref, bst_ref, bsw_ref, cpw_ref, a_ref):
    p8 = pl.program_id(0) % 8
    t_len, grp = SSM_CHUNK, SSM_GROUP
    dt = jnp.exp(ldt_ref[...])
    lr, li = jnp.minimum(lrow_ref[0:1, :], LAMBDA_RE_MAX), lrow_ref[1:2, :]

    def powers(steps, re, im):
        mag = jnp.exp(steps * (re * dt))
        return mag * jnp.cos(steps * (im * dt)), mag * jnp.sin(steps * (im * dt))

    a_re, a_im = powers(1.0, lr, li)
    den = lr * lr + li * li
    nr = a_re - 1.0
    co_re = (nr * lr + a_im * li) / den
    co_im = (a_im * lr - nr * li) / den
    bbt_re = co_re * btr_ref[...] - co_im * bti_ref[...]
    bbt_im = co_re * bti_ref[...] + co_im * btr_ref[...]

    lag = (lax.broadcasted_iota(jnp.int32, (1, t_len * grp), 1) // grp).astype(F32)
    lcr, lci = jnp.minimum(lcol_ref[:, 0:1], LAMBDA_RE_MAX), lcol_ref[:, 1:2]

    def c_times_power(steps):
        p_re, p_im = powers(steps, lcr, lci)
        return (ctr_ref[...] * p_re - cti_ref[...] * p_im, ctr_ref[...] * p_im + cti_ref[...] * p_re)

    wt_re, wt_im = c_times_power(lag)
    kern = (jnp.dot(bbt_re, wt_re, precision=HIGHEST, preferred_element_type=F32)
            - jnp.dot(bbt_im, wt_im, precision=HIGHEST, preferred_element_type=F32))

    lane = lax.broadcasted_iota(jnp.int32, (1, LANES), 1)
    col_shift = p8 * grp

    def store_cols(ref, rows, lo_half, hi_half):
        ref[rows, 0:LANES] = pltpu.roll(lo_half, col_shift, axis=1).astype(ref.dtype)
        ref[rows, LANES:2 * LANES] = pltpu.roll(hi_half, col_shift, axis=1).astype(ref.dtype)

    def slot_rows(s):
        half, s8 = divmod(s, 8)
        return pl.ds(pl.multiple_of((8 * half + (s8 + p8) % 8) * grp, grp), grp)

    back = (t_len - 1 - lax.broadcasted_iota(jnp.int32, (t_len, 1), 0)).astype(F32)
    e_re, e_im = powers(back, lr, li)
    zero = jnp.zeros((grp, LANES), F32)
    k_lo, k_hi = kern[:, 0:LANES], kern[:, LANES:2 * LANES]
    for s in range(t_len):
        half, s8 = divmod(s, 8)
        keep = lane >= s8 * grp
        r_lo = pltpu.roll(k_lo, s8 * grp, axis=1) if s8 else k_lo
        r_hi = pltpu.roll(k_hi, s8 * grp, axis=1) if s8 else k_hi
        if half == 0:
            lo, hi = jnp.where(keep, r_lo, 0.0), jnp.where(keep, r_hi, r_lo)
        else:
            lo, hi = zero, jnp.where(keep, r_lo, 0.0)
        store_cols(toep_ref, slot_rows(s), lo, hi)
        es_re, es_im = e_re[s:s + 1, :], e_im[s:s + 1, :]
        bs_re = es_re * bbt_re - es_im * bbt_im
        bs_im = es_re * bbt_im + es_im * bbt_re
        bst_ref[slot_rows(s), :] = jnp.concatenate([bs_re, bs_im], axis=1).astype(bst_ref.dtype)
        bsw_ref[slot_rows(s), :] = jnp.concatenate([bs_im, bs_re], axis=1).astype(bsw_ref.dtype)

    w1_re, w1_im = c_times_power(lag + 1.0)
    store_cols(cpw_ref, pl.ds(0, SSM_STATE), w1_re[:, 0:LANES], w1_re[:, LANES:2 * LANES])
    store_cols(cpw_ref, pl.ds(SSM_STATE, SSM_STATE), -w1_im[:, 0:LANES], -w1_im[:, LANES:2 * LANES])
    s_re, s_im = powers(float(t_len), lr, li)
    a_ref[0:1, :] = jnp.concatenate([s_re, s_re], axis=1)
    a_ref[1:2, :] = jnp.concatenate([-s_im, s_im], axis=1)


def _ssm_prep(lambda_re, lambda_im, log_dt, b_re, b_im, c_re, c_im):
    width = SSM_CHUNK * SSM_GROUP
    lam_row = jnp.stack([lambda_re, lambda_im], axis=1)
    tiled = lambda c: jnp.tile(c.transpose(0, 2, 1), (1, 1, SSM_CHUNK))
    per = lambda a, b: pl.BlockSpec((None, a, b), lambda g: (g, 0, 0))
    return pl.pallas_call(
        _ssm_prep_kernel,
        grid=(SSM_GROUPS,),
        in_specs=[per(2, SSM_STATE), per(SSM_STATE, 2), per(1, 1), per(SSM_GROUP, SSM_STATE),
                  per(SSM_GROUP, SSM_STATE), per(SSM_STATE, width), per(SSM_STATE, width)],
        out_specs=[per(width, width), per(width, 2 * SSM_STATE), per(width, 2 * SSM_STATE),
                   per(2 * SSM_STATE, width), per(2, 2 * SSM_STATE)],
        out_shape=[jax.ShapeDtypeStruct((SSM_GROUPS, width, width), BF16),
                   jax.ShapeDtypeStruct((SSM_GROUPS, width, 2 * SSM_STATE), BF16),
                   jax.ShapeDtypeStruct((SSM_GROUPS, width, 2 * SSM_STATE), BF16),
                   jax.ShapeDtypeStruct((SSM_GROUPS, 2 * SSM_STATE, width), BF16),
                   jax.ShapeDtypeStruct((SSM_GROUPS, 2, 2 * SSM_STATE), F32)],
        compiler_params=_params(("arbitrary",)),
        name="ssm_prep",
    )(lam_row, lam_row.transpose(0, 2, 1), log_dt.reshape(SSM_GROUPS, 1, 1),
      b_re.transpose(0, 2, 1), b_im.transpose(0, 2, 1), tiled(c_re), tiled(c_im))


def _ssm_kernel(n_chunks, bsz, u_ref, toep_ref, bst_ref, bsw_ref, cpw_ref, a_ref, y_ref,
                contrib_ref, cswap_ref, xprev_ref):
    u = u_ref[...]
    contrib_ref[...] = jnp.dot(u, bst_ref[...], preferred_element_type=F32)
    cswap_ref[...] = jnp.dot(u, bsw_ref[...], preferred_element_type=F32)
    a1 = a_ref[0:1, :]
    a2 = a_ref[1:2, :]

    def step(n, carry):
        x, xs = carry
        rows = pl.ds(n, bsz, stride=n_chunks)
        xprev_ref[rows, :] = x
        x_new = a1 * x + a2 * xs + contrib_ref[rows, :]
        xs_new = a1 * xs - a2 * x + cswap_ref[rows, :]
        return x_new, xs_new

    zero = jnp.zeros((bsz, 2 * SSM_STATE), F32)
    lax.fori_loop(0, n_chunks, step, (zero, zero), unroll=4)
    y_ref[...] = (jnp.dot(u, toep_ref[...], preferred_element_type=F32)
                  + jnp.dot(xprev_ref[...].astype(BF16), cpw_ref[...], preferred_element_type=F32))


def _ssm(u_flat, toep, b_state, b_swap, c_pow, a_step, bsz):
    g, rows, w = u_flat.shape
    per = lambda a, b: pl.BlockSpec((None, a, b), lambda i: (i, 0, 0))
    state = pltpu.VMEM((rows, 2 * SSM_STATE), F32)
    return pl.pallas_call(
        functools.partial(_ssm_kernel, rows // bsz, bsz),
        grid=(g,),
        in_specs=[per(rows, w), per(w, w), per(w, 2 * SSM_STATE), per(w, 2 * SSM_STATE),
                  per(2 * SSM_STATE, w), per(2, 2 * SSM_STATE)],
        out_specs=per(rows, w),
        out_shape=jax.ShapeDtypeStruct((g, rows, w), F32),
        scratch_shapes=[state, state, state],
        compiler_params=_params(("arbitrary",)),
        name="ssm",
    )(u_flat, toep, b_state, b_swap, c_pow, a_step)


ROW_SLABS = D_MODEL // LANES // 2
_HIGH_HALF = 0xFFFF0000


def _store_row_tiles(ref, value):
    rows = value.shape[0]
    bits = lax.bitcast_convert_type(value.astype(BF16).astype(F32), jnp.uint32)
    for j in range(ROW_SLABS):
        low = bits[:, j * LANES:(j + 1) * LANES] >> 16
        high = bits[:, (j + ROW_SLABS) * LANES:(j + ROW_SLABS + 1) * LANES] & jnp.uint32(_HIGH_HALF)
        ref[pl.ds(j, rows, stride=ROW_SLABS), :] = high | low


def _load_row_tiles(ref, rows):
    words = [ref[pl.ds(j, rows, stride=ROW_SLABS), :] for j in range(ROW_SLABS)]
    low = [lax.bitcast_convert_type(w << 16, F32) for w in words]
    high = [lax.bitcast_convert_type(w & jnp.uint32(_HIGH_HALF), F32) for w in words]
    return jnp.concatenate(low + high, axis=1)


def _row_tile_copy(src_ref, src_row, dst_ref, dst_row, sem):
    src = src_ref.at[pl.ds(pl.multiple_of(src_row * ROW_SLABS, ROW_SLABS), ROW_SLABS), :]
    dst = dst_ref.at[pl.ds(pl.multiple_of(dst_row * ROW_SLABS, ROW_SLABS), ROW_SLABS), :]
    return pltpu.make_async_copy(src, dst, sem)


def _mix_kernel(x_ref, of_ref, yf_ref, u_ref, ga_ref, gb_ref, mod_ref, dsk_ref, wglu_ref, wfox_ref,
                wssm_ref, wo_ref, g2_ref, wr_ref, br_ref, x1_ref, h2_ref, lg_ref, ytok_ref):
    _to_token_major(yf_ref, ytok_ref, yf_ref.shape[1])
    y_ssm = jnp.concatenate([ytok_ref[j] for j in range(SSM_WIDTH // LANES)], axis=1)
    y = y_ssm + dsk_ref[...] * u_ref[...]
    y = 0.5 * y * (1.0 + jnp.tanh(math.sqrt(2.0 / math.pi) * (y + 0.044715 * (y * y * y))))
    gl = jnp.dot(y.astype(BF16), wglu_ref[...], preferred_element_type=F32)
    o_ssm = gl[:, :SSM_WIDTH] * jax.nn.sigmoid(gl[:, SSM_WIDTH:])
    merged = (ga_ref[...].astype(F32) * jnp.dot(of_ref[...], wfox_ref[...], preferred_element_type=F32)
              + gb_ref[...].astype(F32) * jnp.dot(o_ssm.astype(BF16), wssm_ref[...],
                                                  preferred_element_type=F32))
    x1 = x_ref[...] + mod_ref[2:3, :] * jnp.dot(merged.astype(BF16), wo_ref[...],
                                                 preferred_element_type=F32)
    x1_ref[...] = x1
    h2 = _rms_modulate(x1, g2_ref[...], mod_ref[3:4, :], mod_ref[4:5, :])
    _store_row_tiles(h2_ref, h2)
    a1 = _top_bits(h2)
    a2 = _top_bits(h2 - a1)
    lhs = jnp.concatenate([a1, a1, a2], axis=1).astype(BF16)
    lg_ref[...] = jnp.dot(lhs, wr_ref[...], preferred_element_type=F32) + br_ref[...]


def _mix(x2, o_fox, y_flat, u, sga, sgb, mod3, d_skip, w_glu, w_fox, w_ssm, w_o, g2, w_r, b_r, seq):
    n, d = x2.shape
    tm = MIX_TILE
    tpb = seq // tm
    tok = lambda w: pl.BlockSpec((tm, w), lambda i: (i, 0))
    const = lambda a: pl.BlockSpec(a.shape, lambda i: (0,) * a.ndim)
    flat = pl.BlockSpec((SSM_GROUPS, tm // SSM_CHUNK, SSM_CHUNK * SSM_GROUP), lambda i: (0, i, 0))
    return pl.pallas_call(
        _mix_kernel,
        grid=(n // tm,),
        in_specs=[tok(d), tok(FOX_WIDTH), flat, tok(SSM_WIDTH), tok(d), tok(d),
                  pl.BlockSpec((None, N_MOD, d), lambda i: (i // tpb, 0, 0)),
                  const(d_skip), const(w_glu), const(w_fox), const(w_ssm), const(w_o), const(g2),
                  const(w_r), const(b_r)],
        out_specs=[tok(d), pl.BlockSpec((tm * ROW_SLABS, LANES), lambda i: (i, 0)), tok(LANES)],
        out_shape=[jax.ShapeDtypeStruct((n, d), F32), jax.ShapeDtypeStruct((n * ROW_SLABS, LANES), jnp.uint32),
                   jax.ShapeDtypeStruct((n, LANES), F32)],
        scratch_shapes=[pltpu.VMEM((SSM_WIDTH // LANES, tm, LANES), F32)],
        compiler_params=_params(("arbitrary",)),
        name="mix",
    )(x2, o_fox, y_flat, u, sga, sgb, mod3, d_skip, w_glu, w_fox, w_ssm, w_o, g2, w_r, b_r)


def _route_kernel(lg_ref, tri_ref, idx_ref, wt_ref, cnt_ref, carry_ref):
    i = pl.program_id(0)

    @pl.when(i == 0)
    def _():
        carry_ref[...] = jnp.zeros_like(carry_ref)

    lg = lg_ref[...]
    tm = lg.shape[0]
    lane = lax.broadcasted_iota(jnp.int32, (tm, LANES), 1)
    neg = jnp.full_like(lg, -jnp.inf)

    def first_argmax(vals):
        mx = jnp.max(vals, axis=1, keepdims=True)
        ix = jnp.min(jnp.where(vals == mx, lane, LANES), axis=1, keepdims=True)
        return mx, ix

    is_group = lane < N_GROUPS
    g_max, gi = first_argmax(jnp.where(is_group, lg, neg))
    g_sum = jnp.sum(jnp.where(is_group, jnp.exp(lg - g_max), 0.0), axis=1, keepdims=True)
    p_group = 1.0 / g_sum
    lo = N_GROUPS + EXPERTS_PER_GROUP * gi
    in_group = (lane >= lo) & (lane < lo + EXPERTS_PER_GROUP)
    cand = jnp.where(in_group, lg, neg)
    v1, i1 = first_argmax(cand)
    v2, i2 = first_argmax(jnp.where(lane == i1, neg, cand))
    tt = jnp.exp(v2 - v1)
    w1 = p_group / (1.0 + tt)
    w2 = p_group * tt / (1.0 + tt)
    e1 = i1 - N_GROUPS
    e2 = i2 - N_GROUPS
    sel1 = lane == e1
    sel2 = lane == e2
    onehot = (sel1 | sel2).astype(F32)
    before = jnp.dot(tri_ref[...], onehot.astype(BF16), preferred_element_type=F32) + carry_ref[0:1, :]
    r1 = jnp.sum(jnp.where(sel1, before, 0.0), axis=1, keepdims=True).astype(jnp.int32)
    r2 = jnp.sum(jnp.where(sel2, before, 0.0), axis=1, keepdims=True).astype(jnp.int32)
    total = before[-1:, :] + onehot[-1:, :]
    carry_ref[...] = jnp.broadcast_to(total, carry_ref.shape)
    cnt_ref[...] = jnp.broadcast_to(total, cnt_ref.shape)
    idx_ref[...] = jnp.where(lane == 0, e1, jnp.where(lane == 1, e2, jnp.where(lane == 2, r1, r2)))
    wt_ref[...] = jnp.where(lane == 0, w1, w2)


def _route(logits):
    n = logits.shape[0]
    tm = TOK_TILE
    tri = jnp.tril(jnp.ones((tm, tm), BF16), k=-1)
    tok = pl.BlockSpec((tm, LANES), lambda i: (i, 0))
    return pl.pallas_call(
        _route_kernel,
        grid=(n // tm,),
        in_specs=[tok, pl.BlockSpec((tm, tm), lambda i: (0, 0))],
        out_specs=[tok, tok, pl.BlockSpec((SUBLANES, LANES), lambda i: (0, 0))],
        out_shape=[jax.ShapeDtypeStruct((n, LANES), jnp.int32), jax.ShapeDtypeStruct((n, LANES), F32),
                   jax.ShapeDtypeStruct((SUBLANES, LANES), F32)],
        scratch_shapes=[pltpu.VMEM((SUBLANES, LANES), F32)],
        compiler_params=_params(("arbitrary",)),
        name="route",
    )(logits, tri)


ISSUE_UNROLL = 8


def _dispatch_kernel(dest_ref, h_ref, rows_in_ref, rows_ref, sem):
    del rows_in_ref
    tm = h_ref.shape[0] // ROW_SLABS

    def issue(g, c):
        for j in range(ISSUE_UNROLL):
            t = g * ISSUE_UNROLL + j
            _row_tile_copy(h_ref, t, rows_ref, dest_ref[0, 0, t], sem).start(priority=0)
            _row_tile_copy(h_ref, t, rows_ref, dest_ref[0, 0, tm + t], sem).start(priority=1)
        return c

    lax.fori_loop(0, tm // ISSUE_UNROLL, issue, 0)
    for _ in range(2):
        pltpu.make_async_copy(h_ref, rows_ref.at[pl.ds(0, tm * ROW_SLABS), :], sem).wait()


def _dispatch(dest3, h2_tiles, rows_zero):
    tm = MOVE_TILE
    n = h2_tiles.shape[0] // ROW_SLABS
    return pl.pallas_call(
        _dispatch_kernel,
        grid=(n // tm,),
        in_specs=[pl.BlockSpec((1, 1, 2 * tm), lambda i: (i, 0, 0), memory_space=pltpu.SMEM),
                  pl.BlockSpec((tm * ROW_SLABS, LANES), lambda i: (i, 0)),
                  pl.BlockSpec(memory_space=pl.ANY)],
        out_specs=pl.BlockSpec(memory_space=pl.ANY),
        out_shape=jax.ShapeDtypeStruct(rows_zero.shape, rows_zero.dtype),
        scratch_shapes=[pltpu.SemaphoreType.DMA(())],
        input_output_aliases={2: 0},
        compiler_params=_params(("arbitrary",)),
        name="dispatch",
    )(dest3, h2_tiles, rows_zero)


def _combine_kernel(n_steps, dest_ref, dnext_ref, x1_ref, wt_ref, mod_ref, gf_ref, yr_ref, o_ref,
                    buf_ref, sem):
    i = pl.program_id(0)
    tm = x1_ref.shape[0]

    def gather(idx_ref, which):
        def issue(g, c):
            for j in range(ISSUE_UNROLL):
                t = g * ISSUE_UNROLL + j
                _row_tile_copy(yr_ref, idx_ref[0, 0, t], buf_ref.at[which, 0], t,
                               sem.at[which]).start(priority=0)
                _row_tile_copy(yr_ref, idx_ref[0, 0, tm + t], buf_ref.at[which, 1], t,
                               sem.at[which]).start(priority=1)
            return c

        lax.fori_loop(0, tm // ISSUE_UNROLL, issue, 0)

    cur = i % 2

    @pl.when(i == 0)
    def _():
        gather(dest_ref, 0)

    @pl.when(i + 1 < n_steps)
    def _():
        gather(dnext_ref, 1 - cur)

    for slot in range(2):
        pltpu.make_async_copy(yr_ref.at[pl.ds(0, tm * ROW_SLABS), :], buf_ref.at[cur, slot],
                              sem.at[cur]).wait()
    wt = wt_ref[...]
    moe = (wt[:, 0:1] * _load_row_tiles(buf_ref.at[cur, 0], tm)
           + wt[:, 1:2] * _load_row_tiles(buf_ref.at[cur, 1], tm))
    x = x1_ref[...] + mod_ref[5:6, :] * moe
    ms = jnp.mean(x * x, axis=-1, keepdims=True)
    o_ref[...] = (x * lax.rsqrt(ms + RMS_EPS)) * gf_ref[...]


def _combine(dest3, x1, wts, mod3, final_g, y_rows, seq):
    n, d = x1.shape
    tm = MOVE_TILE
    tpb = seq // tm
    n_steps = n // tm
    idx_spec = lambda f: pl.BlockSpec((1, 1, 2 * tm), f, memory_space=pltpu.SMEM)
    return pl.pallas_call(
        functools.partial(_combine_kernel, n_steps),
        grid=(n_steps,),
        in_specs=[idx_spec(lambda i: (i, 0, 0)),
                  idx_spec(lambda i: (jnp.minimum(i + 1, n_steps - 1), 0, 0)),
                  pl.BlockSpec((tm, d), lambda i: (i, 0)),
                  pl.BlockSpec((tm, LANES), lambda i: (i, 0)),
                  pl.BlockSpec((None, N_MOD, d), lambda i: (i // tpb, 0, 0)),
                  pl.BlockSpec((1, d), lambda i: (0, 0)),
                  pl.BlockSpec(memory_space=pl.ANY)],
        out_specs=pl.BlockSpec((tm, d), lambda i: (i, 0)),
        out_shape=jax.ShapeDtypeStruct((n, d), F32),
        scratch_shapes=[pltpu.VMEM((2, 2, tm * ROW_SLABS, LANES), jnp.uint32),
                        pltpu.SemaphoreType.DMA((2,))],
        compiler_params=_params(("arbitrary",)),
        name="combine",
    )(dest3, dest3, x1, wts, mod3, final_g, y_rows)


def _expert_kernel(be_ref, seg_ref, nxt_ref, nv_ref, x_ref, wg_hbm, wu_hbm, wd_hbm, y_ref,
                   wg_buf, wu_buf, wd_buf, wgb_ref, wub_ref, wdb_ref, sem):
    i = pl.program_id(0)
    valid = i < nv_ref[0]
    first = (i == 0) | (be_ref[i] != be_ref[jnp.maximum(i - 1, 0)])
    slot = seg_ref[i] % 2

    def weight_copies(e, s):
        return [pltpu.make_async_copy(hbm.at[e], buf.at[s], sem.at[s])
                for hbm, buf in ((wg_hbm, wg_buf), (wu_hbm, wu_buf), (wd_hbm, wd_buf))]

    @pl.when(valid & (i == 0))
    def _():
        for c in weight_copies(be_ref[0], 0):
            c.start()

    @pl.when(valid & first)
    def _():
        for c in weight_copies(be_ref[i], slot):
            c.wait()

        @pl.when(nxt_ref[i] >= 0)
        def _():
            for c in weight_copies(nxt_ref[i], 1 - slot):
                c.start()

        wgb_ref[...] = wg_buf[slot].astype(BF16)
        wub_ref[...] = wu_buf[slot].astype(BF16)
        wdb_ref[...] = wd_buf[slot].astype(BF16)

    @pl.when(valid)
    def _():
        xb = _load_row_tiles(x_ref, x_ref.shape[0] // ROW_SLABS).astype(BF16)
        a = jnp.dot(xb, wgb_ref[...], preferred_element_type=F32)
        b = jnp.dot(xb, wub_ref[...], preferred_element_type=F32)
        hid = (a * jax.nn.sigmoid(a)) * b
        _store_row_tiles(y_ref, jnp.dot(hid.astype(BF16), wdb_ref[...], preferred_element_type=F32))

    @pl.when(jnp.logical_not(valid))
    def _():
        y_ref[...] = jnp.zeros_like(y_ref)


def _experts(blk_e, n_valid, x_rows, w_gate, w_up, w_down):
    d = D_MODEL
    rows = x_rows.shape[0] // ROW_SLABS
    tb = ROW_BLOCK
    n_blocks = rows // tb
    idx = jnp.arange(n_blocks, dtype=jnp.int32)
    change = (idx == 0) | (blk_e != jnp.roll(blk_e, 1))
    seg = jnp.cumsum(change.astype(jnp.int32)) - 1
    later_start = (idx[None, :] > idx[:, None]) & change[None, :] & (idx[None, :] < n_valid[0])
    none = jnp.int32(N_EXPERTS)
    nxt = jnp.min(jnp.where(later_start, blk_e[None, :], none), axis=1)
    nxt = jnp.where(nxt == none, -1, nxt)
    row_spec = pl.BlockSpec((tb * ROW_SLABS, LANES), lambda i, *_: (i, 0))
    grid_spec = pltpu.PrefetchScalarGridSpec(
        num_scalar_prefetch=4,
        grid=(n_blocks,),
        in_specs=[row_spec, pl.BlockSpec(memory_space=pl.ANY), pl.BlockSpec(memory_space=pl.ANY),
                  pl.BlockSpec(memory_space=pl.ANY)],
        out_specs=row_spec,
        scratch_shapes=[pltpu.VMEM((2, d, D_EXPERT), F32), pltpu.VMEM((2, d, D_EXPERT), F32),
                        pltpu.VMEM((2, D_EXPERT, d), F32),
                        pltpu.VMEM((d, D_EXPERT), BF16), pltpu.VMEM((d, D_EXPERT), BF16),
                        pltpu.VMEM((D_EXPERT, d), BF16), pltpu.SemaphoreType.DMA((2,))],
    )
    return pl.pallas_call(
        _expert_kernel,
        grid_spec=grid_spec,
        out_shape=jax.ShapeDtypeStruct(x_rows.shape, x_rows.dtype),
        compiler_params=_params(("arbitrary",)),
        name="experts",
    )(blk_e, seg, nxt.astype(jnp.int32), n_valid, x_rows, w_gate, w_up, w_down)


def kernel(x, c, w_ada, b_ada, norm_mix_g, w_in, b_forget, w_out_fox, lambda_re, lambda_im, log_dt,
           ssm_b_re, ssm_b_im, ssm_c_re, ssm_c_im, d_skip, w_glu, w_out_ssm, w_o, norm_ffn_g,
           w_router_group, b_router_group, w_router_expert, b_router_expert, w_gate_e, w_up_e,
           w_down_e, final_g):
    bsz, seq, d = x.shape
    n = bsz * seq
    assert w_ada.shape[0] == 1, "the final RMSNorm is fused into the (single) layer's combine kernel"
    xc = x.reshape(n, d)
    for l in range(1):
        mod3 = _mod(c, w_ada[l], b_ada[l]).reshape(bsz, N_MOD, d)

        wi = w_in[l]
        s_q, s_k, s_v, s_f, s_u, s_ga = 512, 1024, 1536, 1544, 2056, 3080
        scale = FOX_HEAD_DIM ** -0.5
        w_all = jnp.concatenate(
            [wi[:, :s_q] * scale, wi[:, s_q:s_k], wi[:, s_f:s_u], wi[:, s_u:s_ga],
             wi[:, s_ga:], jnp.pad(wi[:, s_v:s_f], ((0, 0), (0, LANES - FOX_HEADS)))],
            axis=1).astype(BF16)
        w_vt = wi[:, s_k:s_v].T.astype(BF16)
        bf_pad = jnp.pad(b_forget[l], (0, LANES - FOX_HEADS)).reshape(1, LANES)
        q, k, v_t, u, u_flat, sga, sgb = _inproj(xc, mod3, norm_mix_g[l].reshape(1, d), w_all, w_vt,
                                                 bf_pad, seq)

        o_fox = _attention(q, k, v_t, bsz, seq)

        toep, b_state, b_swap, c_pow, a_step = _ssm_prep(
            lambda_re[l], lambda_im[l], log_dt[l], ssm_b_re[l], ssm_b_im[l], ssm_c_re[l], ssm_c_im[l])
        y_flat = _ssm(u_flat, toep, b_state, b_swap, c_pow, a_step, bsz)

        w_r = jnp.pad(jnp.concatenate([w_router_group[l], w_router_expert[l]], axis=1),
                      ((0, 0), (0, LANES - N_GROUPS - N_EXPERTS)))
        w_r1 = _top_bits(w_r)
        w_r2 = _top_bits(w_r - w_r1)
        w_r = jnp.concatenate([w_r1, w_r2, w_r1], axis=0).astype(BF16)
        b_r = jnp.pad(jnp.concatenate([b_router_group[l], b_router_expert[l]]),
                      (0, LANES - N_GROUPS - N_EXPERTS)).reshape(1, LANES)
        x1, h2, logits = _mix(xc, o_fox, y_flat, u, sga, sgb, mod3, d_skip[l].reshape(1, SSM_WIDTH),
                              w_glu[l].astype(BF16), w_out_fox[l].astype(BF16),
                              w_out_ssm[l].astype(BF16), w_o[l].astype(BF16),
                              norm_ffn_g[l].reshape(1, d), w_r, b_r, seq)

        idx, wts, cnt = _route(logits)
        counts = cnt[0, :N_EXPERTS].astype(jnp.int32)
        pcounts = ((counts + ROW_BLOCK - 1) // ROW_BLOCK) * ROW_BLOCK
        pends = jnp.cumsum(pcounts)
        pstarts = pends - pcounts
        er = idx[:, 0:4].T
        hit = er[0:2, None, :] == jnp.arange(N_EXPERTS, dtype=jnp.int32)[None, :, None]
        dest = jnp.sum(jnp.where(hit, pstarts[None, :, None], 0), axis=1) + er[2:4]
        rows = 2 * n + N_EXPERTS * ROW_BLOCK
        n_blocks = rows // ROW_BLOCK
        blk_start = jnp.arange(n_blocks, dtype=jnp.int32) * ROW_BLOCK
        blk_e = jnp.minimum(jnp.sum((pends[None, :] <= blk_start[:, None]).astype(jnp.int32), axis=1),
                            N_EXPERTS - 1)
        n_valid = (pends[-1:] // ROW_BLOCK).astype(jnp.int32)
        dest3 = (dest.astype(jnp.int32).reshape(2, n // MOVE_TILE, MOVE_TILE).transpose(1, 0, 2)
                 .reshape(n // MOVE_TILE, 1, 2 * MOVE_TILE))

        x_rows = _dispatch(dest3, h2, jnp.zeros((rows * ROW_SLABS, LANES), jnp.uint32))
        y_rows = _experts(blk_e, n_valid, x_rows, w_gate_e[l], w_up_e[l], w_down_e[l])
        xc = _combine(dest3, x1, wts, mod3, final_g.reshape(1, d), y_rows, seq)
    return xc.reshape(bsz, seq, d)
```

```python
import functools
import math

import jax
import jax.numpy as jnp
import numpy as np
from jax import lax
from jax.experimental import pallas as pl
from jax.experimental.pallas import tpu as pltpu

F32 = jnp.float32
BF16 = jnp.bfloat16

D_MODEL = 1024
N_MOD = 6
RMS_EPS = 1e-6
FOX_HEADS = 8
FOX_HEAD_DIM = 64
FOX_WIDTH = FOX_HEADS * FOX_HEAD_DIM
HEAD_PAIRS = FOX_HEADS // 2
SSM_WIDTH = 512
SSM_GROUP = 16
SSM_GROUPS = SSM_WIDTH // SSM_GROUP
SSM_STATE = 64
LAMBDA_RE_MAX = -1e-4
N_GROUPS = 4
EXPERTS_PER_GROUP = 8
N_EXPERTS = N_GROUPS * EXPERTS_PER_GROUP
D_EXPERT = 512

LANES = 128
SUBLANES = 8
VMEM_LIMIT = 56 * 1024 * 1024

SSM_CHUNK = 16
TOK_TILE = 512
MIX_TILE = 256
ATT_Q_TILE = 512
ATT_K_TILE = 256
ROW_BLOCK = 256
MOVE_TILE = 256
NEG_BIG = -1e30

HIGHEST = lax.Precision.HIGHEST


def _params(sem):
    return pltpu.CompilerParams(dimension_semantics=sem, vmem_limit_bytes=VMEM_LIMIT)


def _rms_modulate(x, gain, shift, scale):
    ms = jnp.mean(x * x, axis=-1, keepdims=True)
    return (x * lax.rsqrt(ms + RMS_EPS)) * gain * (1.0 + scale) + shift


def _mod_kernel(c_ref, w_ref, b_ref, o_ref):
    c = c_ref[...]
    ca = (c * jax.nn.sigmoid(c)).astype(BF16)
    o_ref[...] = jnp.dot(ca, w_ref[...].astype(BF16), preferred_element_type=F32) + b_ref[...]


def _mod(c, w_ada, b_ada):
    bsz, d = c.shape
    cols = w_ada.shape[1]
    tn = 1536
    return pl.pallas_call(
        _mod_kernel,
        grid=(cols // tn,),
        in_specs=[pl.BlockSpec((bsz, d), lambda j: (0, 0)),
                  pl.BlockSpec((d, tn), lambda j: (0, j)),
                  pl.BlockSpec((1, tn), lambda j: (0, j))],
        out_specs=pl.BlockSpec((bsz, tn), lambda j: (0, j)),
        out_shape=jax.ShapeDtypeStruct((bsz, cols), F32),
        compiler_params=_params(("arbitrary",)),
        name="mod",
    )(c, w_ada, b_ada.reshape(1, cols))


_C_Q, _C_K, _C_U, _C_GA, _C_GB, _C_F, _C_END = 0, 512, 1024, 1536, 2560, 3584, 3712


def _lane_block():
    return lax.broadcasted_iota(jnp.int32, (1, LANES), 1) // SSM_GROUP


def _to_group_major(tok_ref, flat_ref, rows):
    blk = _lane_block()
    for half in range(2):
        for j in range(SSM_WIDTH // LANES):
            w = []
            for s8 in range(8):
                v = tok_ref[j, pl.ds(8 * half + s8, rows, stride=SSM_CHUNK), :]
                w.append(pltpu.roll(v, s8 * SSM_GROUP, axis=1) if s8 else v)
            for p in range(8):
                acc = w[0]
                for s8 in range(1, 8):
                    acc = jnp.where(blk == (p + s8) % 8, w[s8], acc)
                flat_ref[8 * j + p, :, half * LANES:(half + 1) * LANES] = acc.astype(flat_ref.dtype)


def _to_token_major(flat_ref, tok_ref, rows):
    blk = _lane_block()
    for half in range(2):
        for j in range(SSM_WIDTH // LANES):
            ys = [flat_ref[8 * j + p, :, half * LANES:(half + 1) * LANES] for p in range(8)]
            for s8 in range(8):
                acc = ys[0]
                for p in range(1, 8):
                    acc = jnp.where(blk == (p + s8) % 8, ys[p], acc)
                if s8:
                    acc = pltpu.roll(acc, LANES - s8 * SSM_GROUP, axis=1)
                tok_ref[j, pl.ds(8 * half + s8, rows, stride=SSM_CHUNK), :] = acc


def _bias_lane_placement():
    pq = np.zeros((3 * LANES, HEAD_PAIRS * LANES), np.float32)
    pk = np.zeros((3 * LANES, HEAD_PAIRS * LANES), np.float32)
    bq = np.zeros((1, HEAD_PAIRS * LANES), np.float32)
    bk = np.zeros((1, HEAD_PAIRS * LANES), np.float32)
    for head in range(FOX_HEADS):
        base = (head // 2) * LANES + (head % 2) * 8
        for term in range(3):
            pq[term * LANES + head, base + term] = 1.0
            pk[term * LANES + head, base + 3 + term] = -1.0
            bq[0, base + 3 + term] = 1.0
            bk[0, base + term] = 1.0
    return pq, pk, bq, bk


def _top_bits(a):
    bits = lax.bitcast_convert_type(a, jnp.uint32) & jnp.uint32(0xFFFF0000)
    return lax.bitcast_convert_type(bits, F32)


def _inproj_kernel(tiles_per_batch, x_ref, mod_ref, g_ref, w_ref, wvt_ref, bf_ref, tri_ref,
                   pq_ref, pk_ref, bq_ref, bk_ref,
                   q_ref, k_ref, vt_ref, u_ref, uflat_ref, ga_ref, gb_ref, carry_ref, uslab_ref):
    i = pl.program_id(0)
    h = _rms_modulate(x_ref[...], g_ref[...], mod_ref[0:1, :], mod_ref[1:2, :])
    hb = h.astype(BF16)

    def proj(a, b):
        return jnp.dot(hb, w_ref[:, a:b], preferred_element_type=F32)

    q = proj(_C_Q, _C_K).astype(BF16)
    k = proj(_C_K, _C_U).astype(BF16)
    vt_ref[...] = lax.dot_general(wvt_ref[...], hb, (((1,), (1,)), ((), ())),
                                  preferred_element_type=F32).astype(BF16)
    u = proj(_C_U, _C_GA)
    u_ref[...] = u
    for j in range(SSM_WIDTH // LANES):
        uslab_ref[j] = u[:, j * LANES:(j + 1) * LANES]
    _to_group_major(uslab_ref, uflat_ref, u.shape[0] // SSM_CHUNK)
    ga_ref[...] = jax.nn.sigmoid(proj(_C_GA, _C_GB)).astype(BF16)
    gb_ref[...] = jax.nn.sigmoid(proj(_C_GB, _C_F)).astype(BF16)

    f = proj(_C_F, _C_END) + bf_ref[...]
    logf = jnp.minimum(f, 0.0) - jnp.log(1.0 + jnp.exp(-jnp.abs(f)))

    @pl.when(i % tiles_per_batch == 0)
    def _():
        carry_ref[...] = jnp.zeros_like(carry_ref)

    def split3(a):
        hi = _top_bits(a)
        r1 = a - hi
        mid = _top_bits(r1)
        return jnp.concatenate([hi, mid, _top_bits(r1 - mid)], axis=1).astype(BF16)

    part = jnp.dot(tri_ref[...], split3(logf), preferred_element_type=F32)
    cs = (part[:, :LANES] + part[:, LANES:2 * LANES] + part[:, 2 * LANES:]) + carry_ref[0:1, :]
    carry_ref[...] = jnp.broadcast_to(cs[-1:, :], carry_ref.shape)

    terms = split3(cs)
    bias_q = (jnp.dot(terms, pq_ref[...], preferred_element_type=F32) + bq_ref[...]).astype(BF16)
    bias_k = (jnp.dot(terms, pk_ref[...], preferred_element_type=F32) + bk_ref[...]).astype(BF16)
    for p in range(HEAD_PAIRS):
        lanes = slice(p * LANES, (p + 1) * LANES)
        q_ref[:, 2 * p * LANES:(2 * p + 1) * LANES] = q[:, lanes]
        q_ref[:, (2 * p + 1) * LANES:(2 * p + 2) * LANES] = bias_q[:, lanes]
        k_ref[:, 2 * p * LANES:(2 * p + 1) * LANES] = k[:, lanes]
        k_ref[:, (2 * p + 1) * LANES:(2 * p + 2) * LANES] = bias_k[:, lanes]


def _inproj(x2, mod3, gain, w_all, w_vt, bf_pad, seq):
    n, d = x2.shape
    tm = TOK_TILE
    tpb = seq // tm
    tri = jnp.tril(jnp.ones((tm, tm), BF16))
    pq, pk, bq, bk = _bias_lane_placement()
    tok = lambda w: pl.BlockSpec((tm, w), lambda i: (i, 0))
    const = lambda shape: pl.BlockSpec(shape, lambda i: (0,) * len(shape))
    qk_width = 2 * FOX_WIDTH
    return pl.pallas_call(
        functools.partial(_inproj_kernel, tpb),
        grid=(n // tm,),
        in_specs=[tok(d),
                  pl.BlockSpec((None, N_MOD, d), lambda i: (i // tpb, 0, 0)),
                  const((1, d)), const((d, _C_END)), const((FOX_WIDTH, d)), const((1, LANES)),
                  const((tm, tm)), const(pq.shape), const(pk.shape), const(bq.shape), const(bk.shape)],
        out_specs=[tok(qk_width), tok(qk_width), pl.BlockSpec((FOX_WIDTH, tm), lambda i: (0, i)),
                   tok(SSM_WIDTH),
                   pl.BlockSpec((SSM_GROUPS, tm // SSM_CHUNK, SSM_CHUNK * SSM_GROUP), lambda i: (0, i, 0)),
                   tok(d), tok(d)],
        out_shape=[jax.ShapeDtypeStruct((n, qk_width), BF16)] * 2
        + [jax.ShapeDtypeStruct((FOX_WIDTH, n), BF16)]
        + [jax.ShapeDtypeStruct((n, SSM_WIDTH), F32)]
        + [jax.ShapeDtypeStruct((SSM_GROUPS, n // SSM_CHUNK, SSM_CHUNK * SSM_GROUP), BF16)]
        + [jax.ShapeDtypeStruct((n, d), BF16)] * 2,
        scratch_shapes=[pltpu.VMEM((SUBLANES, LANES), F32),
                        pltpu.VMEM((SSM_WIDTH // LANES, tm, LANES), F32)],
        compiler_params=_params(("arbitrary",)),
        name="inproj",
    )(x2, mod3, gain, w_all, w_vt, bf_pad, tri, jnp.asarray(pq, BF16), jnp.asarray(pk, BF16),
      jnp.asarray(bq), jnp.asarray(bk))


def _attn_kernel(q_ref, k_ref, vt_ref, o_ref, m_ref, acc_ref, sa_ref, sb_ref):
    i = pl.program_id(2)
    tq, tk = ATT_Q_TILE, ATT_K_TILE
    q = q_ref[...]
    lane = lax.broadcasted_iota(jnp.int32, (1, 2 * LANES), 1)
    zq = jnp.zeros_like(q)
    half = FOX_HEAD_DIM
    own0 = (lane < half) | ((lane >= LANES) & (lane < LANES + 8))
    own1 = ((lane >= half) & (lane < LANES)) | ((lane >= LANES + 8) & (lane < LANES + 16))
    q_both = jnp.concatenate([jnp.where(own0, q, zq), jnp.where(own1, q, zq)], axis=0)
    m_ref[...] = jnp.full(m_ref.shape, NEG_BIG, F32)
    acc_ref[...] = jnp.zeros(acc_ref.shape, F32)
    ones_rows = jnp.ones((2 * SUBLANES, tk), BF16)
    key_in_tile = lax.broadcasted_iota(jnp.int32, (tk, 2 * tq), 0)
    qry_pos = i * tq + (lax.broadcasted_iota(jnp.int32, (tk, 2 * tq), 1) & (tq - 1))

    def scores(j, s_ref):
        start = pl.multiple_of(j * tk, tk)
        s = lax.dot_general(k_ref[pl.ds(start, tk), :], q_both, (((1,), (1,)), ((), ())),
                            preferred_element_type=F32)
        s_ref[...] = jnp.where(key_in_tile + j * tk <= qry_pos, s, NEG_BIG)

    def accumulate(j, s_ref):
        start = pl.multiple_of(j * tk, tk)
        va = jnp.concatenate([vt_ref[:, pl.ds(start, tk)], ones_rows], axis=0)
        m_old = m_ref[...]
        m_new = jnp.maximum(m_old, jnp.max(s_ref[...], axis=0, keepdims=True))
        alpha = jnp.exp(m_old - m_new)
        p = jnp.exp(s_ref[...] - m_new).astype(BF16)
        acc_ref[...] = alpha * acc_ref[...] + jnp.dot(va, p, preferred_element_type=F32)
        m_ref[...] = m_new

    def body(jj, c):
        t0 = 2 * jj
        scores(t0 + 1, sb_ref)
        accumulate(t0, sa_ref)
        scores(t0 + 2, sa_ref)
        accumulate(t0 + 1, sb_ref)
        return c

    assert tq == 2 * tk
    scores(0, sa_ref)
    lax.fori_loop(0, i, body, 0)
    scores(2 * i + 1, sb_ref)
    accumulate(2 * i, sa_ref)
    accumulate(2 * i + 1, sb_ref)

    acc = acc_ref[...]
    o_t = jnp.concatenate([acc[0:half, 0:tq] / acc[LANES:LANES + 1, 0:tq],
                           acc[half:LANES, tq:2 * tq] / acc[LANES:LANES + 1, tq:2 * tq]], axis=0)
    o_ref[...] = o_t.T.astype(o_ref.dtype)


def _attention(q, k, v_t, bsz, seq):
    n = q.shape[0]
    t = ATT_Q_TILE
    nq = seq // t
    return pl.pallas_call(
        _attn_kernel,
        grid=(bsz, HEAD_PAIRS, nq),
        in_specs=[pl.BlockSpec((t, 2 * LANES), lambda b, p, i: (b * nq + i, p)),
                  pl.BlockSpec((seq, 2 * LANES), lambda b, p, i: (b, p)),
                  pl.BlockSpec((LANES, seq), lambda b, p, i: (p, b))],
        out_specs=pl.BlockSpec((t, LANES), lambda b, p, i: (b * nq + i, p)),
        out_shape=jax.ShapeDtypeStruct((n, FOX_WIDTH), BF16),
        scratch_shapes=[pltpu.VMEM((1, 2 * t), F32), pltpu.VMEM((LANES + 2 * SUBLANES, 2 * t), F32),
                        pltpu.VMEM((ATT_K_TILE, 2 * t), F32), pltpu.VMEM((ATT_K_TILE, 2 * t), F32)],
        compiler_params=_params(("arbitrary", "arbitrary", "arbitrary")),
        name="attn",
    )(q, k, v_t)


def _ssm_prep_kernel(lrow_ref, lcol_ref, ldt_ref, btr_ref, bti_ref, ctr_ref, cti_ref,
                     toep_ref, bst_ref, bsw_ref, cpw_ref, a_ref):
    p8 = pl.program_id(0) % 8
    t_len, grp = SSM_CHUNK, SSM_GROUP
    dt = jnp.exp(ldt_ref[...])
    lr, li = jnp.minimum(lrow_ref[0:1, :], LAMBDA_RE_MAX), lrow_ref[1:2, :]

    def powers(steps, re, im):
        mag = jnp.exp(steps * (re * dt))
        return mag * jnp.cos(steps * (im * dt)), mag * jnp.sin(steps * (im * dt))

    a_re, a_im = powers(1.0, lr, li)
    den = lr * lr + li * li
    nr = a_re - 1.0
    co_re = (nr * lr + a_im * li) / den
    co_im = (a_im * lr - nr * li) / den
    bbt_re = co_re * btr_ref[...] - co_im * bti_ref[...]
    bbt_im = co_re * bti_ref[...] + co_im * btr_ref[...]

    lag = (lax.broadcasted_iota(jnp.int32, (1, t_len * grp), 1) // grp).astype(F32)
    lcr, lci = jnp.minimum(lcol_ref[:, 0:1], LAMBDA_RE_MAX), lcol_ref[:, 1:2]

    def c_times_power(steps):
        p_re, p_im = powers(steps, lcr, lci)
        return (ctr_ref[...] * p_re - cti_ref[...] * p_im, ctr_ref[...] * p_im + cti_ref[...] * p_re)

    wt_re, wt_im = c_times_power(lag)
    kern = (jnp.dot(bbt_re, wt_re, precision=HIGHEST, preferred_element_type=F32)
            - jnp.dot(bbt_im, wt_im, precision=HIGHEST, preferred_element_type=F32))

    lane = lax.broadcasted_iota(jnp.int32, (1, LANES), 1)
    col_shift = p8 * grp

    def store_cols(ref, rows, lo_half, hi_half):
        ref[rows, 0:LANES] = pltpu.roll(lo_half, col_shift, axis=1).astype(ref.dtype)
        ref[rows, LANES:2 * LANES] = pltpu.roll(hi_half, col_shift, axis=1).astype(ref.dtype)

    def slot_rows(s):
        half, s8 = divmod(s, 8)
        return pl.ds(pl.multiple_of((8 * half + (s8 + p8) % 8) * grp, grp), grp)

    back = (t_len - 1 - lax.broadcasted_iota(jnp.int32, (t_len, 1), 0)).astype(F32)
    e_re, e_im = powers(back, lr, li)
    zero = jnp.zeros((grp, LANES), F32)
    k_lo, k_hi = kern[:, 0:LANES], kern[:, LANES:2 * LANES]
    for s in range(t_len):
        half, s8 = divmod(s, 8)
        keep = lane >= s8 * grp
        r_lo = pltpu.roll(k_lo, s8 * grp, axis=1) if s8 else k_lo
        r_hi = pltpu.roll(k_hi, s8 * grp, axis=1) if s8 else k_hi
        if half == 0:
            lo, hi = jnp.where(keep, r_lo, 0.0), jnp.where(keep, r_hi, r_lo)
        else:
            lo, hi = zero, jnp.where(keep, r_lo, 0.0)
        store_cols(toep_ref, slot_rows(s), lo, hi)
        es_re, es_im = e_re[s:s + 1, :], e_im[s:s + 1, :]
        bs_re = es_re * bbt_re - es_im * bbt_im
        bs_im = es_re * bbt_im + es_im * bbt_re
        bst_ref[slot_rows(s), :] = jnp.concatenate([bs_re, bs_im], axis=1).astype(bst_ref.dtype)
        bsw_ref[slot_rows(s), :] = jnp.concatenate([bs_im, bs_re], axis=1).astype(bsw_ref.dtype)

    w1_re, w1_im = c_times_power(lag + 1.0)
    store_cols(cpw_ref, pl.ds(0, SSM_STATE), w1_re[:, 0:LANES], w1_re[:, LANES:2 * LANES])
    store_cols(cpw_ref, pl.ds(SSM_STATE, SSM_STATE), -w1_im[:, 0:LANES], -w1_im[:, LANES:2 * LANES])
    s_re, s_im = powers(float(t_len), lr, li)
    a_ref[0:1, :] = jnp.concatenate([s_re, s_re], axis=1)
    a_ref[1:2, :] = jnp.concatenate([-s_im, s_im], axis=1)


def _ssm_prep(lambda_re, lambda_im, log_dt, b_re, b_im, c_re, c_im):
    width = SSM_CHUNK * SSM_GROUP
    lam_row = jnp.stack([lambda_re, lambda_im], axis=1)
    tiled = lambda c: jnp.tile(c.transpose(0, 2, 1), (1, 1, SSM_CHUNK))
    per = lambda a, b: pl.BlockSpec((None, a, b), lambda g: (g, 0, 0))
    return pl.pallas_call(
        _ssm_prep_kernel,
        grid=(SSM_GROUPS,),
        in_specs=[per(2, SSM_STATE), per(SSM_STATE, 2), per(1, 1), per(SSM_GROUP, SSM_STATE),
                  per(SSM_GROUP, SSM_STATE), per(SSM_STATE, width), per(SSM_STATE, width)],
        out_specs=[per(width, width), per(width, 2 * SSM_STATE), per(width, 2 * SSM_STATE),
                   per(2 * SSM_STATE, width), per(2, 2 * SSM_STATE)],
        out_shape=[jax.ShapeDtypeStruct((SSM_GROUPS, width, width), BF16),
                   jax.ShapeDtypeStruct((SSM_GROUPS, width, 2 * SSM_STATE), BF16),
                   jax.ShapeDtypeStruct((SSM_GROUPS, width, 2 * SSM_STATE), BF16),
                   jax.ShapeDtypeStruct((SSM_GROUPS, 2 * SSM_STATE, width), BF16),
                   jax.ShapeDtypeStruct((SSM_GROUPS, 2, 2 * SSM_STATE), F32)],
        compiler_params=_params(("arbitrary",)),
        name="ssm_prep",
    )(lam_row, lam_row.transpose(0, 2, 1), log_dt.reshape(SSM_GROUPS, 1, 1),
      b_re.transpose(0, 2, 1), b_im.transpose(0, 2, 1), tiled(c_re), tiled(c_im))


def _ssm_kernel(n_chunks, bsz, u_ref, toep_ref, bst_ref, bsw_ref, cpw_ref, a_ref, y_ref,
                contrib_ref, cswap_ref, xprev_ref):
    u = u_ref[...]
    contrib_ref[...] = jnp.dot(u, bst_ref[...], preferred_element_type=F32)
    cswap_ref[...] = jnp.dot(u, bsw_ref[...], preferred_element_type=F32)
    a1 = a_ref[0:1, :]
    a2 = a_ref[1:2, :]

    def step(n, carry):
        x, xs = carry
        rows = pl.ds(n, bsz, stride=n_chunks)
        xprev_ref[rows, :] = x
        x_new = a1 * x + a2 * xs + contrib_ref[rows, :]
        xs_new = a1 * xs - a2 * x + cswap_ref[rows, :]
        return x_new, xs_new

    zero = jnp.zeros((bsz, 2 * SSM_STATE), F32)
    lax.fori_loop(0, n_chunks, step, (zero, zero), unroll=4)
    y_ref[...] = (jnp.dot(u, toep_ref[...], preferred_element_type=F32)
                  + jnp.dot(xprev_ref[...].astype(BF16), cpw_ref[...], preferred_element_type=F32))


def _ssm(u_flat, toep, b_state, b_swap, c_pow, a_step, bsz):
    g, rows, w = u_flat.shape
    per = lambda a, b: pl.BlockSpec((None, a, b), lambda i: (i, 0, 0))
    state = pltpu.VMEM((rows, 2 * SSM_STATE), F32)
    return pl.pallas_call(
        functools.partial(_ssm_kernel, rows // bsz, bsz),
        grid=(g,),
        in_specs=[per(rows, w), per(w, w), per(w, 2 * SSM_STATE), per(w, 2 * SSM_STATE),
                  per(2 * SSM_STATE, w), per(2, 2 * SSM_STATE)],
        out_specs=per(rows, w),
        out_shape=jax.ShapeDtypeStruct((g, rows, w), F32),
        scratch_shapes=[state, state, state],
        compiler_params=_params(("arbitrary",)),
        name="ssm",
    )(u_flat, toep, b_state, b_swap, c_pow, a_step)


ROW_SLABS = D_MODEL // LANES // 2
_HIGH_HALF = 0xFFFF0000


def _store_row_tiles(ref, value):
    rows = value.shape[0]
    bits = lax.bitcast_convert_type(value.astype(BF16).astype(F32), jnp.uint32)
    for j in range(ROW_SLABS):
        low = bits[:, j * LANES:(j + 1) * LANES] >> 16
        high = bits[:, (j + ROW_SLABS) * LANES:(j + ROW_SLABS + 1) * LANES] & jnp.uint32(_HIGH_HALF)
        ref[pl.ds(j, rows, stride=ROW_SLABS), :] = high | low


def _load_row_tiles(ref, rows):
    words = [ref[pl.ds(j, rows, stride=ROW_SLABS), :] for j in range(ROW_SLABS)]
    low = [lax.bitcast_convert_type(w << 16, F32) for w in words]
    high = [lax.bitcast_convert_type(w & jnp.uint32(_HIGH_HALF), F32) for w in words]
    return jnp.concatenate(low + high, axis=1)


def _row_tile_copy(src_ref, src_row, dst_ref, dst_row, sem):
    src = src_ref.at[pl.ds(pl.multiple_of(src_row * ROW_SLABS, ROW_SLABS), ROW_SLABS), :]
    dst = dst_ref.at[pl.ds(pl.multiple_of(dst_row * ROW_SLABS, ROW_SLABS), ROW_SLABS), :]
    return pltpu.make_async_copy(src, dst, sem)


def _mix_kernel(x_ref, of_ref, yf_ref, u_ref, ga_ref, gb_ref, mod_ref, dsk_ref, wglu_ref, wfox_ref,
                wssm_ref, wo_ref, g2_ref, wr_ref, br_ref, x1_ref, h2_ref, lg_ref, ytok_ref):
    _to_token_major(yf_ref, ytok_ref, yf_ref.shape[1])
    y_ssm = jnp.concatenate([ytok_ref[j] for j in range(SSM_WIDTH // LANES)], axis=1)
    y = y_ssm + dsk_ref[...] * u_ref[...]
    y = 0.5 * y * (1.0 + jnp.tanh(math.sqrt(2.0 / math.pi) * (y + 0.044715 * (y * y * y))))
    gl = jnp.dot(y.astype(BF16), wglu_ref[...], preferred_element_type=F32)
    o_ssm = gl[:, :SSM_WIDTH] * jax.nn.sigmoid(gl[:, SSM_WIDTH:])
    merged = (ga_ref[...].astype(F32) * jnp.dot(of_ref[...], wfox_ref[...], preferred_element_type=F32)
              + gb_ref[...].astype(F32) * jnp.dot(o_ssm.astype(BF16), wssm_ref[...],
                                                  preferred_element_type=F32))
    x1 = x_ref[...] + mod_ref[2:3, :] * jnp.dot(merged.astype(BF16), wo_ref[...],
                                                 preferred_element_type=F32)
    x1_ref[...] = x1
    h2 = _rms_modulate(x1, g2_ref[...], mod_ref[3:4, :], mod_ref[4:5, :])
    _store_row_tiles(h2_ref, h2)
    a1 = _top_bits(h2)
    a2 = _top_bits(h2 - a1)
    lhs = jnp.concatenate([a1, a1, a2], axis=1).astype(BF16)
    lg_ref[...] = jnp.dot(lhs, wr_ref[...], preferred_element_type=F32) + br_ref[...]


def _mix(x2, o_fox, y_flat, u, sga, sgb, mod3, d_skip, w_glu, w_fox, w_ssm, w_o, g2, w_r, b_r, seq):
    n, d = x2.shape
    tm = MIX_TILE
    tpb = seq // tm
    tok = lambda w: pl.BlockSpec((tm, w), lambda i: (i, 0))
    const = lambda a: pl.BlockSpec(a.shape, lambda i: (0,) * a.ndim)
    flat = pl.BlockSpec((SSM_GROUPS, tm // SSM_CHUNK, SSM_CHUNK * SSM_GROUP), lambda i: (0, i, 0))
    return pl.pallas_call(
        _mix_kernel,
        grid=(n // tm,),
        in_specs=[tok(d), tok(FOX_WIDTH), flat, tok(SSM_WIDTH), tok(d), tok(d),
                  pl.BlockSpec((None, N_MOD, d), lambda i: (i // tpb, 0, 0)),
                  const(d_skip), const(w_glu), const(w_fox), const(w_ssm), const(w_o), const(g2),
                  const(w_r), const(b_r)],
        out_specs=[tok(d), pl.BlockSpec((tm * ROW_SLABS, LANES), lambda i: (i, 0)), tok(LANES)],
        out_shape=[jax.ShapeDtypeStruct((n, d), F32), jax.ShapeDtypeStruct((n * ROW_SLABS, LANES), jnp.uint32),
                   jax.ShapeDtypeStruct((n, LANES), F32)],
        scratch_shapes=[pltpu.VMEM((SSM_WIDTH // LANES, tm, LANES), F32)],
        compiler_params=_params(("arbitrary",)),
        name="mix",
    )(x2, o_fox, y_flat, u, sga, sgb, mod3, d_skip, w_glu, w_fox, w_ssm, w_o, g2, w_r, b_r)


def _route_kernel(lg_ref, tri_ref, idx_ref, wt_ref, cnt_ref, carry_ref):
    i = pl.program_id(0)

    @pl.when(i == 0)
    def _():
        carry_ref[...] = jnp.zeros_like(carry_ref)

    lg = lg_ref[...]
    tm = lg.shape[0]
    lane = lax.broadcasted_iota(jnp.int32, (tm, LANES), 1)
    neg = jnp.full_like(lg, -jnp.inf)

    def first_argmax(vals):
        mx = jnp.max(vals, axis=1, keepdims=True)
        ix = jnp.min(jnp.where(vals == mx, lane, LANES), axis=1, keepdims=True)
        return mx, ix

    is_group = lane < N_GROUPS
    g_max, gi = first_argmax(jnp.where(is_group, lg, neg))
    g_sum = jnp.sum(jnp.where(is_group, jnp.exp(lg - g_max), 0.0), axis=1, keepdims=True)
    p_group = 1.0 / g_sum
    lo = N_GROUPS + EXPERTS_PER_GROUP * gi
    in_group = (lane >= lo) & (lane < lo + EXPERTS_PER_GROUP)
    cand = jnp.where(in_group, lg, neg)
    v1, i1 = first_argmax(cand)
    v2, i2 = first_argmax(jnp.where(lane == i1, neg, cand))
    tt = jnp.exp(v2 - v1)
    w1 = p_group / (1.0 + tt)
    w2 = p_group * tt / (1.0 + tt)
    e1 = i1 - N_GROUPS
    e2 = i2 - N_GROUPS
    sel1 = lane == e1
    sel2 = lane == e2
    onehot = (sel1 | sel2).astype(F32)
    before = jnp.dot(tri_ref[...], onehot.astype(BF16), preferred_element_type=F32) + carry_ref[0:1, :]
    r1 = jnp.sum(jnp.where(sel1, before, 0.0), axis=1, keepdims=True).astype(jnp.int32)
    r2 = jnp.sum(jnp.where(sel2, before, 0.0), axis=1, keepdims=True).astype(jnp.int32)
    total = before[-1:, :] + onehot[-1:, :]
    carry_ref[...] = jnp.broadcast_to(total, carry_ref.shape)
    cnt_ref[...] = jnp.broadcast_to(total, cnt_ref.shape)
    idx_ref[...] = jnp.where(lane == 0, e1, jnp.where(lane == 1, e2, jnp.where(lane == 2, r1, r2)))
    wt_ref[...] = jnp.where(lane == 0, w1, w2)


def _route(logits):
    n = logits.shape[0]
    tm = TOK_TILE
    tri = jnp.tril(jnp.ones((tm, tm), BF16), k=-1)
    tok = pl.BlockSpec((tm, LANES), lambda i: (i, 0))
    return pl.pallas_call(
        _route_kernel,
        grid=(n // tm,),
        in_specs=[tok, pl.BlockSpec((tm, tm), lambda i: (0, 0))],
        out_specs=[tok, tok, pl.BlockSpec((SUBLANES, LANES), lambda i: (0, 0))],
        out_shape=[jax.ShapeDtypeStruct((n, LANES), jnp.int32), jax.ShapeDtypeStruct((n, LANES), F32),
                   jax.ShapeDtypeStruct((SUBLANES, LANES), F32)],
        scratch_shapes=[pltpu.VMEM((SUBLANES, LANES), F32)],
        compiler_params=_params(("arbitrary",)),
        name="route",
    )(logits, tri)


ISSUE_UNROLL = 8


def _dispatch_kernel(dest_ref, h_ref, rows_in_ref, rows_ref, sem):
    del rows_in_ref
    tm = h_ref.shape[0] // ROW_SLABS

    def issue(g, c):
        for j in range(ISSUE_UNROLL):
            t = g * ISSUE_UNROLL + j
            _row_tile_copy(h_ref, t, rows_ref, dest_ref[0, 0, t], sem).start(priority=0)
            _row_tile_copy(h_ref, t, rows_ref, dest_ref[0, 0, tm + t], sem).start(priority=1)
        return c

    lax.fori_loop(0, tm // ISSUE_UNROLL, issue, 0)
    for _ in range(2):
        pltpu.make_async_copy(h_ref, rows_ref.at[pl.ds(0, tm * ROW_SLABS), :], sem).wait()


def _dispatch(dest3, h2_tiles, rows_zero):
    tm = MOVE_TILE
    n = h2_tiles.shape[0] // ROW_SLABS
    return pl.pallas_call(
        _dispatch_kernel,
        grid=(n // tm,),
        in_specs=[pl.BlockSpec((1, 1, 2 * tm), lambda i: (i, 0, 0), memory_space=pltpu.SMEM),
                  pl.BlockSpec((tm * ROW_SLABS, LANES), lambda i: (i, 0)),
                  pl.BlockSpec(memory_space=pl.ANY)],
        out_specs=pl.BlockSpec(memory_space=pl.ANY),
        out_shape=jax.ShapeDtypeStruct(rows_zero.shape, rows_zero.dtype),
        scratch_shapes=[pltpu.SemaphoreType.DMA(())],
        input_output_aliases={2: 0},
        compiler_params=_params(("arbitrary",)),
        name="dispatch",
    )(dest3, h2_tiles, rows_zero)


def _combine_kernel(n_steps, dest_ref, dnext_ref, x1_ref, wt_ref, mod_ref, gf_ref, yr_ref, o_ref,
                    buf_ref, sem):
    i = pl.program_id(0)
    tm = x1_ref.shape[0]

    def gather(idx_ref, which):
        def issue(g, c):
            for j in range(ISSUE_UNROLL):
                t = g * ISSUE_UNROLL + j
                _row_tile_copy(yr_ref, idx_ref[0, 0, t], buf_ref.at[which, 0], t,
                               sem.at[which]).start(priority=0)
                _row_tile_copy(yr_ref, idx_ref[0, 0, tm + t], buf_ref.at[which, 1], t,
                               sem.at[which]).start(priority=1)
            return c

        lax.fori_loop(0, tm // ISSUE_UNROLL, issue, 0)

    cur = i % 2

    @pl.when(i == 0)
    def _():
        gather(dest_ref, 0)

    @pl.when(i + 1 < n_steps)
    def _():
        gather(dnext_ref, 1 - cur)

    for slot in range(2):
        pltpu.make_async_copy(yr_ref.at[pl.ds(0, tm * ROW_SLABS), :], buf_ref.at[cur, slot],
                              sem.at[cur]).wait()
    wt = wt_ref[...]
    moe = (wt[:, 0:1] * _load_row_tiles(buf_ref.at[cur, 0], tm)
           + wt[:, 1:2] * _load_row_tiles(buf_ref.at[cur, 1], tm))
    x = x1_ref[...] + mod_ref[5:6, :] * moe
    ms = jnp.mean(x * x, axis=-1, keepdims=True)
    o_ref[...] = (x * lax.rsqrt(ms + RMS_EPS)) * gf_ref[...]


def _combine(dest3, x1, wts, mod3, final_g, y_rows, seq):
    n, d = x1.shape
    tm = MOVE_TILE
    tpb = seq // tm
    n_steps = n // tm
    idx_spec = lambda f: pl.BlockSpec((1, 1, 2 * tm), f, memory_space=pltpu.SMEM)
    return pl.pallas_call(
        functools.partial(_combine_kernel, n_steps),
        grid=(n_steps,),
        in_specs=[idx_spec(lambda i: (i, 0, 0)),
                  idx_spec(lambda i: (jnp.minimum(i + 1, n_steps - 1), 0, 0)),
                  pl.BlockSpec((tm, d), lambda i: (i, 0)),
                  pl.BlockSpec((tm, LANES), lambda i: (i, 0)),
                  pl.BlockSpec((None, N_MOD, d), lambda i: (i // tpb, 0, 0)),
                  pl.BlockSpec((1, d), lambda i: (0, 0)),
                  pl.BlockSpec(memory_space=pl.ANY)],
        out_specs=pl.BlockSpec((tm, d), lambda i: (i, 0)),
        out_shape=jax.ShapeDtypeStruct((n, d), F32),
        scratch_shapes=[pltpu.VMEM((2, 2, tm * ROW_SLABS, LANES), jnp.uint32),
                        pltpu.SemaphoreType.DMA((2,))],
        compiler_params=_params(("arbitrary",)),
        name="combine",
    )(dest3, dest3, x1, wts, mod3, final_g, y_rows)


def _expert_kernel(be_ref, seg_ref, nxt_ref, nv_ref, x_ref, wg_hbm, wu_hbm, wd_hbm, y_ref,
                   wg_buf, wu_buf, wd_buf, wgb_ref, wub_ref, wdb_ref, sem):
    i = pl.program_id(0)
    valid = i < nv_ref[0]
    first = (i == 0) | (be_ref[i] != be_ref[jnp.maximum(i - 1, 0)])
    slot = seg_ref[i] % 2

    def weight_copies(e, s):
        return [pltpu.make_async_copy(hbm.at[e], buf.at[s], sem.at[s])
                for hbm, buf in ((wg_hbm, wg_buf), (wu_hbm, wu_buf), (wd_hbm, wd_buf))]

    @pl.when(valid & (i == 0))
    def _():
        for c in weight_copies(be_ref[0], 0):
            c.start()

    @pl.when(valid & first)
    def _():
        for c in weight_copies(be_ref[i], slot):
            c.wait()

        @pl.when(nxt_ref[i] >= 0)
        def _():
            for c in weight_copies(nxt_ref[i], 1 - slot):
                c.start()

        wgb_ref[...] = wg_buf[slot].astype(BF16)
        wub_ref[...] = wu_buf[slot].astype(BF16)
        wdb_ref[...] = wd_buf[slot].astype(BF16)

    @pl.when(valid)
    def _():
        xb = _load_row_tiles(x_ref, x_ref.shape[0] // ROW_SLABS).astype(BF16)
        a = jnp.dot(xb, wgb_ref[...], preferred_element_type=F32)
        b = jnp.dot(xb, wub_ref[...], preferred_element_type=F32)
        hid = (a * jax.nn.sigmoid(a)) * b
        _store_row_tiles(y_ref, jnp.dot(hid.astype(BF16), wdb_ref[...], preferred_element_type=F32))

    @pl.when(jnp.logical_not(valid))
    def _():
        y_ref[...] = jnp.zeros_like(y_ref)


def _experts(blk_e, n_valid, x_rows, w_gate, w_up, w_down):
    d = D_MODEL
    rows = x_rows.shape[0] // ROW_SLABS
    tb = ROW_BLOCK
    n_blocks = rows // tb
    idx = jnp.arange(n_blocks, dtype=jnp.int32)
    change = (idx == 0) | (blk_e != jnp.roll(blk_e, 1))
    seg = jnp.cumsum(change.astype(jnp.int32)) - 1
    later_start = (idx[None, :] > idx[:, None]) & change[None, :] & (idx[None, :] < n_valid[0])
    none = jnp.int32(N_EXPERTS)
    nxt = jnp.min(jnp.where(later_start, blk_e[None, :], none), axis=1)
    nxt = jnp.where(nxt == none, -1, nxt)
    row_spec = pl.BlockSpec((tb * ROW_SLABS, LANES), lambda i, *_: (i, 0))
    grid_spec = pltpu.PrefetchScalarGridSpec(
        num_scalar_prefetch=4,
        grid=(n_blocks,),
        in_specs=[row_spec, pl.BlockSpec(memory_space=pl.ANY), pl.BlockSpec(memory_space=pl.ANY),
                  pl.BlockSpec(memory_space=pl.ANY)],
        out_specs=row_spec,
        scratch_shapes=[pltpu.VMEM((2, d, D_EXPERT), F32), pltpu.VMEM((2, d, D_EXPERT), F32),
                        pltpu.VMEM((2, D_EXPERT, d), F32),
                        pltpu.VMEM((d, D_EXPERT), BF16), pltpu.VMEM((d, D_EXPERT), BF16),
                        pltpu.VMEM((D_EXPERT, d), BF16), pltpu.SemaphoreType.DMA((2,))],
    )
    return pl.pallas_call(
        _expert_kernel,
        grid_spec=grid_spec,
        out_shape=jax.ShapeDtypeStruct(x_rows.shape, x_rows.dtype),
        compiler_params=_params(("arbitrary",)),
        name="experts",
    )(blk_e, seg, nxt.astype(jnp.int32), n_valid, x_rows, w_gate, w_up, w_down)


def kernel(x, c, w_ada, b_ada, norm_mix_g, w_in, b_forget, w_out_fox, lambda_re, lambda_im, log_dt,
           ssm_b_re, ssm_b_im, ssm_c_re, ssm_c_im, d_skip, w_glu, w_out_ssm, w_o, norm_ffn_g,
           w_router_group, b_router_group, w_router_expert, b_router_expert, w_gate_e, w_up_e,
           w_down_e, final_g):
    bsz, seq, d = x.shape
    n = bsz * seq
    assert w_ada.shape[0] == 1, "the final RMSNorm is fused into the (single) layer's combine kernel"
    xc = x.reshape(n, d)
    for l in range(1):
        mod3 = _mod(c, w_ada[l], b_ada[l]).reshape(bsz, N_MOD, d)

        wi = w_in[l]
        s_q, s_k, s_v, s_f, s_u, s_ga = 512, 1024, 1536, 1544, 2056, 3080
        scale = FOX_HEAD_DIM ** -0.5
        w_all = jnp.concatenate(
            [wi[:, :s_q] * scale, wi[:, s_q:s_k], wi[:, s_f:s_u], wi[:, s_u:s_ga],
             wi[:, s_ga:], jnp.pad(wi[:, s_v:s_f], ((0, 0), (0, LANES - FOX_HEADS)))],
            axis=1).astype(BF16)
        w_vt = wi[:, s_k:s_v].T.astype(BF16)
        bf_pad = jnp.pad(b_forget[l], (0, LANES - FOX_HEADS)).reshape(1, LANES)
        q, k, v_t, u, u_flat, sga, sgb = _inproj(xc, mod3, norm_mix_g[l].reshape(1, d), w_all, w_vt,
                                                 bf_pad, seq)

        o_fox = _attention(q, k, v_t, bsz, seq)

        toep, b_state, b_swap, c_pow, a_step = _ssm_prep(
            lambda_re[l], lambda_im[l], log_dt[l], ssm_b_re[l], ssm_b_im[l], ssm_c_re[l], ssm_c_im[l])
        y_flat = _ssm(u_flat, toep, b_state, b_swap, c_pow, a_step, bsz)

        w_r = jnp.pad(jnp.concatenate([w_router_group[l], w_router_expert[l]], axis=1),
                      ((0, 0), (0, LANES - N_GROUPS - N_EXPERTS)))
        w_r1 = _top_bits(w_r)
        w_r2 = _top_bits(w_r - w_r1)
        w_r = jnp.concatenate([w_r1, w_r2, w_r1], axis=0).astype(BF16)
        b_r = jnp.pad(jnp.concatenate([b_router_group[l], b_router_expert[l]]),
                      (0, LANES - N_GROUPS - N_EXPERTS)).reshape(1, LANES)
        x1, h2, logits = _mix(xc, o_fox, y_flat, u, sga, sgb, mod3, d_skip[l].reshape(1, SSM_WIDTH),
                              w_glu[l].astype(BF16), w_out_fox[l].astype(BF16),
                              w_out_ssm[l].astype(BF16), w_o[l].astype(BF16),
                              norm_ffn_g[l].reshape(1, d), w_r, b_r, seq)

        idx, wts, cnt = _route(logits)
        counts = cnt[0, :N_EXPERTS].astype(jnp.int32)
        pcounts = ((counts + ROW_BLOCK - 1) // ROW_BLOCK) * ROW_BLOCK
        pends = jnp.cumsum(pcounts)
        pstarts = pends - pcounts
        er = idx[:, 0:4].T
        hit = er[0:2, None, :] == jnp.arange(N_EXPERTS, dtype=jnp.int32)[None, :, None]
        dest = jnp.sum(jnp.where(hit, pstarts[None, :, None], 0), axis=1) + er[2:4]
        rows = 2 * n + N_EXPERTS * ROW_BLOCK
        n_blocks = rows // ROW_BLOCK
        blk_start = jnp.arange(n_blocks, dtype=jnp.int32) * ROW_BLOCK
        blk_e = jnp.minimum(jnp.sum((pends[None, :] <= blk_start[:, None]).astype(jnp.int32), axis=1),
                            N_EXPERTS - 1)
        n_valid = (pends[-1:] // ROW_BLOCK).astype(jnp.int32)
        dest3 = (dest.astype(jnp.int32).reshape(2, n // MOVE_TILE, MOVE_TILE).transpose(1, 0, 2)
                 .reshape(n // MOVE_TILE, 1, 2 * MOVE_TILE))

        x_rows = _dispatch(dest3, h2, jnp.zeros((rows * ROW_SLABS, LANES), jnp.uint32))
        y_rows = _experts(blk_e, n_valid, x_rows, w_gate_e[l], w_up_e[l], w_down_e[l])
        xc = _combine(dest3, x1, wts, mod3, final_g.reshape(1, d), y_rows, seq)
    return xc.reshape(bsz, seq, d)
```

```python
import functools
import math

import jax
import jax.numpy as jnp
import numpy as np
from jax import lax
from jax.experimental import pallas as pl
from jax.experimental.pallas import tpu as pltpu

F32 = jnp.float32
BF16 = jnp.bfloat16

D_MODEL = 1024
N_MOD = 6
RMS_EPS = 1e-6
FOX_HEADS = 8
FOX_HEAD_DIM = 64
FOX_WIDTH = FOX_HEADS * FOX_HEAD_DIM
HEAD_PAIRS = FOX_HEADS // 2
SSM_WIDTH = 512
SSM_GROUP = 16
SSM_GROUPS = SSM_WIDTH // SSM_GROUP
SSM_STATE = 64
LAMBDA_RE_MAX = -1e-4
N_GROUPS = 4
EXPERTS_PER_GROUP = 8
N_EXPERTS = N_GROUPS * EXPERTS_PER_GROUP
D_EXPERT = 512

LANES = 128
SUBLANES = 8
VMEM_LIMIT = 56 * 1024 * 1024

SSM_CHUNK = 16
TOK_TILE = 512
MIX_TILE = 256
ATT_Q_TILE = 512
ATT_K_TILE = 256
ROW_BLOCK = 256
MOVE_TILE = 512
NEG_BIG = -1e30

HIGHEST = lax.Precision.HIGHEST


def _params(sem):
    return pltpu.CompilerParams(dimension_semantics=sem, vmem_limit_bytes=VMEM_LIMIT)


def _rms_modulate(x, gain, shift, scale):
    ms = jnp.mean(x * x, axis=-1, keepdims=True)
    return (x * lax.rsqrt(ms + RMS_EPS)) * gain * (1.0 + scale) + shift


def _mod_kernel(c_ref, w_ref, b_ref, o_ref):
    c = c_ref[...]
    ca = (c * jax.nn.sigmoid(c)).astype(BF16)
    o_ref[...] = jnp.dot(ca, w_ref[...].astype(BF16), preferred_element_type=F32) + b_ref[...]


def _mod(c, w_ada, b_ada):
    bsz, d = c.shape
    cols = w_ada.shape[1]
    tn = 1536
    return pl.pallas_call(
        _mod_kernel,
        grid=(cols // tn,),
        in_specs=[pl.BlockSpec((bsz, d), lambda j: (0, 0)),
                  pl.BlockSpec((d, tn), lambda j: (0, j)),
                  pl.BlockSpec((1, tn), lambda j: (0, j))],
        out_specs=pl.BlockSpec((bsz, tn), lambda j: (0, j)),
        out_shape=jax.ShapeDtypeStruct((bsz, cols), F32),
        compiler_params=_params(("arbitrary",)),
        name="mod",
    )(c, w_ada, b_ada.reshape(1, cols))


_C_Q, _C_K, _C_U, _C_GA, _C_GB, _C_F, _C_END = 0, 512, 1024, 1536, 2560, 3584, 3712


def _lane_block():
    return lax.broadcasted_iota(jnp.int32, (1, LANES), 1) // SSM_GROUP


def _to_group_major(tok_ref, flat_ref, rows):
    blk = _lane_block()
    for half in range(2):
        for j in range(SSM_WIDTH // LANES):
            w = []
            for s8 in range(8):
                v = tok_ref[j, pl.ds(8 * half + s8, rows, stride=SSM_CHUNK), :]
                w.append(pltpu.roll(v, s8 * SSM_GROUP, axis=1) if s8 else v)
            for p in range(8):
                acc = w[0]
                for s8 in range(1, 8):
                    acc = jnp.where(blk == (p + s8) % 8, w[s8], acc)
                flat_ref[8 * j + p, :, half * LANES:(half + 1) * LANES] = acc.astype(flat_ref.dtype)


def _to_token_major(flat_ref, tok_ref, rows):
    blk = _lane_block()
    for half in range(2):
        for j in range(SSM_WIDTH // LANES):
            ys = [flat_ref[8 * j + p, :, half * LANES:(half + 1) * LANES] for p in range(8)]
            for s8 in range(8):
                acc = ys[0]
                for p in range(1, 8):
                    acc = jnp.where(blk == (p + s8) % 8, ys[p], acc)
                if s8:
                    acc = pltpu.roll(acc, LANES - s8 * SSM_GROUP, axis=1)
                tok_ref[j, pl.ds(8 * half + s8, rows, stride=SSM_CHUNK), :] = acc


def _bias_lane_placement():
    pq = np.zeros((3 * LANES, HEAD_PAIRS * LANES), np.float32)
    pk = np.zeros((3 * LANES, HEAD_PAIRS * LANES), np.float32)
    bq = np.zeros((1, HEAD_PAIRS * LANES), np.float32)
    bk = np.zeros((1, HEAD_PAIRS * LANES), np.float32)
    for head in range(FOX_HEADS):
        base = (head // 2) * LANES + (head % 2) * 8
        for term in range(3):
            pq[term * LANES + head, base + term] = 1.0
            pk[term * LANES + head, base + 3 + term] = -1.0
            bq[0, base + 3 + term] = 1.0
            bk[0, base + term] = 1.0
    return pq, pk, bq, bk


def _top_bits(a):
    bits = lax.bitcast_convert_type(a, jnp.uint32) & jnp.uint32(0xFFFF0000)
    return lax.bitcast_convert_type(bits, F32)


def _inproj_kernel(tiles_per_batch, x_ref, mod_ref, g_ref, w_ref, wvt_ref, bf_ref, tri_ref,
                   pq_ref, pk_ref, bq_ref, bk_ref,
                   q_ref, k_ref, vt_ref, u_ref, uflat_ref, ga_ref, gb_ref, carry_ref, uslab_ref):
    i = pl.program_id(0)
    h = _rms_modulate(x_ref[...], g_ref[...], mod_ref[0:1, :], mod_ref[1:2, :])
    hb = h.astype(BF16)

    def proj(a, b):
        return jnp.dot(hb, w_ref[:, a:b], preferred_element_type=F32)

    q = proj(_C_Q, _C_K).astype(BF16)
    k = proj(_C_K, _C_U).astype(BF16)
    vt_ref[...] = lax.dot_general(wvt_ref[...], hb, (((1,), (1,)), ((), ())),
                                  preferred_element_type=F32).astype(BF16)
    u = proj(_C_U, _C_GA)
    u_ref[...] = u
    for j in range(SSM_WIDTH // LANES):
        uslab_ref[j] = u[:, j * LANES:(j + 1) * LANES]
    _to_group_major(uslab_ref, uflat_ref, u.shape[0] // SSM_CHUNK)
    ga_ref[...] = jax.nn.sigmoid(proj(_C_GA, _C_GB)).astype(BF16)
    gb_ref[...] = jax.nn.sigmoid(proj(_C_GB, _C_F)).astype(BF16)

    f = proj(_C_F, _C_END) + bf_ref[...]
    logf = jnp.minimum(f, 0.0) - jnp.log(1.0 + jnp.exp(-jnp.abs(f)))

    @pl.when(i % tiles_per_batch == 0)
    def _():
        carry_ref[...] = jnp.zeros_like(carry_ref)

    def split3(a):
        hi = _top_bits(a)
        r1 = a - hi
        mid = _top_bits(r1)
        return jnp.concatenate([hi, mid, _top_bits(r1 - mid)], axis=1).astype(BF16)

    part = jnp.dot(tri_ref[...], split3(logf), preferred_element_type=F32)
    cs = (part[:, :LANES] + part[:, LANES:2 * LANES] + part[:, 2 * LANES:]) + carry_ref[0:1, :]
    carry_ref[...] = jnp.broadcast_to(cs[-1:, :], carry_ref.shape)

    terms = split3(cs)
    bias_q = (jnp.dot(terms, pq_ref[...], preferred_element_type=F32) + bq_ref[...]).astype(BF16)
    bias_k = (jnp.dot(terms, pk_ref[...], preferred_element_type=F32) + bk_ref[...]).astype(BF16)
    for p in range(HEAD_PAIRS):
        lanes = slice(p * LANES, (p + 1) * LANES)
        q_ref[:, 2 * p * LANES:(2 * p + 1) * LANES] = q[:, lanes]
        q_ref[:, (2 * p + 1) * LANES:(2 * p + 2) * LANES] = bias_q[:, lanes]
        k_ref[:, 2 * p * LANES:(2 * p + 1) * LANES] = k[:, lanes]
        k_ref[:, (2 * p + 1) * LANES:(2 * p + 2) * LANES] = bias_k[:, lanes]


def _inproj(x2, mod3, gain, w_all, w_vt, bf_pad, seq):
    n, d = x2.shape
    tm = TOK_TILE
    tpb = seq // tm
    tri = jnp.tril(jnp.ones((tm, tm), BF16))
    pq, pk, bq, bk = _bias_lane_placement()
    tok = lambda w: pl.BlockSpec((tm, w), lambda i: (i, 0))
    const = lambda shape: pl.BlockSpec(shape, lambda i: (0,) * len(shape))
    qk_width = 2 * FOX_WIDTH
    return pl.pallas_call(
        functools.partial(_inproj_kernel, tpb),
        grid=(n // tm,),
        in_specs=[tok(d),
                  pl.BlockSpec((None, N_MOD, d), lambda i: (i // tpb, 0, 0)),
                  const((1, d)), const((d, _C_END)), const((FOX_WIDTH, d)), const((1, LANES)),
                  const((tm, tm)), const(pq.shape), const(pk.shape), const(bq.shape), const(bk.shape)],
        out_specs=[tok(qk_width), tok(qk_width), pl.BlockSpec((FOX_WIDTH, tm), lambda i: (0, i)),
                   tok(SSM_WIDTH),
                   pl.BlockSpec((SSM_GROUPS, tm // SSM_CHUNK, SSM_CHUNK * SSM_GROUP), lambda i: (0, i, 0)),
                   tok(d), tok(d)],
        out_shape=[jax.ShapeDtypeStruct((n, qk_width), BF16)] * 2
        + [jax.ShapeDtypeStruct((FOX_WIDTH, n), BF16)]
        + [jax.ShapeDtypeStruct((n, SSM_WIDTH), F32)]
        + [jax.ShapeDtypeStruct((SSM_GROUPS, n // SSM_CHUNK, SSM_CHUNK * SSM_GROUP), BF16)]
        + [jax.ShapeDtypeStruct((n, d), BF16)] * 2,
        scratch_shapes=[pltpu.VMEM((SUBLANES, LANES), F32),
                        pltpu.VMEM((SSM_WIDTH // LANES, tm, LANES), F32)],
        compiler_params=_params(("arbitrary",)),
        name="inproj",
    )(x2, mod3, gain, w_all, w_vt, bf_pad, tri, jnp.asarray(pq, BF16), jnp.asarray(pk, BF16),
      jnp.asarray(bq), jnp.asarray(bk))


def _attn_kernel(q_ref, k_ref, vt_ref, o_ref, m_ref, acc_ref, sa_ref, sb_ref):
    i = pl.program_id(2)
    tq, tk = ATT_Q_TILE, ATT_K_TILE
    q = q_ref[...]
    lane = lax.broadcasted_iota(jnp.int32, (1, 2 * LANES), 1)
    zq = jnp.zeros_like(q)
    half = FOX_HEAD_DIM
    own0 = (lane < half) | ((lane >= LANES) & (lane < LANES + 8))
    own1 = ((lane >= half) & (lane < LANES)) | ((lane >= LANES + 8) & (lane < LANES + 16))
    q_both = jnp.concatenate([jnp.where(own0, q, zq), jnp.where(own1, q, zq)], axis=0)
    m_ref[...] = jnp.full(m_ref.shape, NEG_BIG, F32)
    acc_ref[...] = jnp.zeros(acc_ref.shape, F32)
    ones_rows = jnp.ones((2 * SUBLANES, tk), BF16)
    key_in_tile = lax.broadcasted_iota(jnp.int32, (tk, 2 * tq), 0)
    qry_pos = i * tq + (lax.broadcasted_iota(jnp.int32, (tk, 2 * tq), 1) & (tq - 1))

    def scores(j, s_ref):
        start = pl.multiple_of(j * tk, tk)
        s = lax.dot_general(k_ref[pl.ds(start, tk), :], q_both, (((1,), (1,)), ((), ())),
                            preferred_element_type=F32)
        s_ref[...] = jnp.where(key_in_tile + j * tk <= qry_pos, s, NEG_BIG)

    def accumulate(j, s_ref):
        start = pl.multiple_of(j * tk, tk)
        va = jnp.concatenate([vt_ref[:, pl.ds(start, tk)], ones_rows], axis=0)
        m_old = m_ref[...]
        m_new = jnp.maximum(m_old, jnp.max(s_ref[...], axis=0, keepdims=True))
        alpha = jnp.exp(m_old - m_new)
        p = jnp.exp(s_ref[...] - m_new).astype(BF16)
        acc_ref[...] = alpha * acc_ref[...] + jnp.dot(va, p, preferred_element_type=F32)
        m_ref[...] = m_new

    def body(jj, c):
        t0 = 2 * jj
        scores(t0 + 1, sb_ref)
        accumulate(t0, sa_ref)
        scores(t0 + 2, sa_ref)
        accumulate(t0 + 1, sb_ref)
        return c

    assert tq == 2 * tk
    scores(0, sa_ref)
    lax.fori_loop(0, i, body, 0)
    scores(2 * i + 1, sb_ref)
    accumulate(2 * i, sa_ref)
    accumulate(2 * i + 1, sb_ref)

    acc = acc_ref[...]
    o_t = jnp.concatenate([acc[0:half, 0:tq] / acc[LANES:LANES + 1, 0:tq],
                           acc[half:LANES, tq:2 * tq] / acc[LANES:LANES + 1, tq:2 * tq]], axis=0)
    o_ref[...] = o_t.T.astype(o_ref.dtype)


def _attention(q, k, v_t, bsz, seq):
    n = q.shape[0]
    t = ATT_Q_TILE
    nq = seq // t
    return pl.pallas_call(
        _attn_kernel,
        grid=(bsz, HEAD_PAIRS, nq),
        in_specs=[pl.BlockSpec((t, 2 * LANES), lambda b, p, i: (b * nq + i, p)),
                  pl.BlockSpec((seq, 2 * LANES), lambda b, p, i: (b, p)),
                  pl.BlockSpec((LANES, seq), lambda b, p, i: (p, b))],
        out_specs=pl.BlockSpec((t, LANES), lambda b, p, i: (b * nq + i, p)),
        out_shape=jax.ShapeDtypeStruct((n, FOX_WIDTH), BF16),
        scratch_shapes=[pltpu.VMEM((1, 2 * t), F32), pltpu.VMEM((LANES + 2 * SUBLANES, 2 * t), F32),
                        pltpu.VMEM((ATT_K_TILE, 2 * t), F32), pltpu.VMEM((ATT_K_TILE, 2 * t), F32)],
        compiler_params=_params(("arbitrary", "arbitrary", "arbitrary")),
        name="attn",
    )(q, k, v_t)


def _ssm_prep_kernel(lrow_ref, lcol_ref, ldt_ref, btr_ref, bti_ref, ctr_ref, cti_ref,
                     toep_ref, bst_ref, bsw_ref, cpw_ref, a_ref):
    p8 = pl.program_id(0) % 8
    t_len, grp = SSM_CHUNK, SSM_GROUP
    dt = jnp.exp(ldt_ref[...])
    lr, li = jnp.minimum(lrow_ref[0:1, :], LAMBDA_RE_MAX), lrow_ref[1:2, :]

    def powers(steps, re, im):
        mag = jnp.exp(steps * (re * dt))
        return mag * jnp.cos(steps * (im * dt)), mag * jnp.sin(steps * (im * dt))

    a_re, a_im = powers(1.0, lr, li)
    den = lr * lr + li * li
    nr = a_re - 1.0
    co_re = (nr * lr + a_im * li) / den
    co_im = (a_im * lr - nr * li) / den
    bbt_re = co_re * btr_ref[...] - co_im * bti_ref[...]
    bbt_im = co_re * bti_ref[...] + co_im * btr_ref[...]

    lag = (lax.broadcasted_iota(jnp.int32, (1, t_len * grp), 1) // grp).astype(F32)
    lcr, lci = jnp.minimum(lcol_ref[:, 0:1], LAMBDA_RE_MAX), lcol_ref[:, 1:2]

    def c_times_power(steps):
        p_re, p_im = powers(steps, lcr, lci)
        return (ctr_ref[...] * p_re - cti_ref[...] * p_im, ctr_ref[...] * p_im + cti_ref[...] * p_re)

    wt_re, wt_im = c_times_power(lag)
    kern = (jnp.dot(bbt_re, wt_re, precision=HIGHEST, preferred_element_type=F32)
            - jnp.dot(bbt_im, wt_im, precision=HIGHEST, preferred_element_type=F32))

    lane = lax.broadcasted_iota(jnp.int32, (1, LANES), 1)
    col_shift = p8 * grp

    def store_cols(ref, rows, lo_half, hi_half):
        ref[rows, 0:LANES] = pltpu.roll(lo_half, col_shift, axis=1).astype(ref.dtype)
        ref[rows, LANES:2 * LANES] = pltpu.roll(hi_half, col_shift, axis=1).astype(ref.dtype)

    def slot_rows(s):
        half, s8 = divmod(s, 8)
        return pl.ds(pl.multiple_of((8 * half + (s8 + p8) % 8) * grp, grp), grp)

    back = (t_len - 1 - lax.broadcasted_iota(jnp.int32, (t_len, 1), 0)).astype(F32)
    e_re, e_im = powers(back, lr, li)
    zero = jnp.zeros((grp, LANES), F32)
    k_lo, k_hi = kern[:, 0:LANES], kern[:, LANES:2 * LANES]
    for s in range(t_len):
        half, s8 = divmod(s, 8)
        keep = lane >= s8 * grp
        r_lo = pltpu.roll(k_lo, s8 * grp, axis=1) if s8 else k_lo
        r_hi = pltpu.roll(k_hi, s8 * grp, axis=1) if s8 else k_hi
        if half == 0:
            lo, hi = jnp.where(keep, r_lo, 0.0), jnp.where(keep, r_hi, r_lo)
        else:
            lo, hi = zero, jnp.where(keep, r_lo, 0.0)
        store_cols(toep_ref, slot_rows(s), lo, hi)
        es_re, es_im = e_re[s:s + 1, :], e_im[s:s + 1, :]
        bs_re = es_re * bbt_re - es_im * bbt_im
        bs_im = es_re * bbt_im + es_im * bbt_re
        bst_ref[slot_rows(s), :] = jnp.concatenate([bs_re, bs_im], axis=1).astype(bst_ref.dtype)
        bsw_ref[slot_rows(s), :] = jnp.concatenate([bs_im, bs_re], axis=1).astype(bsw_ref.dtype)

    w1_re, w1_im = c_times_power(lag + 1.0)
    store_cols(cpw_ref, pl.ds(0, SSM_STATE), w1_re[:, 0:LANES], w1_re[:, LANES:2 * LANES])
    store_cols(cpw_ref, pl.ds(SSM_STATE, SSM_STATE), -w1_im[:, 0:LANES], -w1_im[:, LANES:2 * LANES])
    s_re, s_im = powers(float(t_len), lr, li)
    a_ref[0:1, :] = jnp.concatenate([s_re, s_re], axis=1)
    a_ref[1:2, :] = jnp.concatenate([-s_im, s_im], axis=1)


def _ssm_prep(lambda_re, lambda_im, log_dt, b_re, b_im, c_re, c_im):
    width = SSM_CHUNK * SSM_GROUP
    lam_row = jnp.stack([lambda_re, lambda_im], axis=1)
    tiled = lambda c: jnp.tile(c.transpose(0, 2, 1), (1, 1, SSM_CHUNK))
    per = lambda a, b: pl.BlockSpec((None, a, b), lambda g: (g, 0, 0))
    return pl.pallas_call(
        _ssm_prep_kernel,
        grid=(SSM_GROUPS,),
        in_specs=[per(2, SSM_STATE), per(SSM_STATE, 2), per(1, 1), per(SSM_GROUP, SSM_STATE),
                  per(SSM_GROUP, SSM_STATE), per(SSM_STATE, width), per(SSM_STATE, width)],
        out_specs=[per(width, width), per(width, 2 * SSM_STATE), per(width, 2 * SSM_STATE),
                   per(2 * SSM_STATE, width), per(2, 2 * SSM_STATE)],
        out_shape=[jax.ShapeDtypeStruct((SSM_GROUPS, width, width), BF16),
                   jax.ShapeDtypeStruct((SSM_GROUPS, width, 2 * SSM_STATE), BF16),
                   jax.ShapeDtypeStruct((SSM_GROUPS, width, 2 * SSM_STATE), BF16),
                   jax.ShapeDtypeStruct((SSM_GROUPS, 2 * SSM_STATE, width), BF16),
                   jax.ShapeDtypeStruct((SSM_GROUPS, 2, 2 * SSM_STATE), F32)],
        compiler_params=_params(("arbitrary",)),
        name="ssm_prep",
    )(lam_row, lam_row.transpose(0, 2, 1), log_dt.reshape(SSM_GROUPS, 1, 1),
      b_re.transpose(0, 2, 1), b_im.transpose(0, 2, 1), tiled(c_re), tiled(c_im))


SSM_GROUPS_PER_STEP = 4


def _ssm_kernel(n_chunks, bsz, u_ref, toep_ref, bst_ref, bsw_ref, cpw_ref, a_ref, y_ref,
                contrib_ref, cswap_ref, xprev_ref):
    groups = u_ref.shape[0]
    for k in range(groups):
        u = u_ref[k]
        contrib_ref[k] = jnp.dot(u, bst_ref[k], preferred_element_type=F32)
        cswap_ref[k] = jnp.dot(u, bsw_ref[k], preferred_element_type=F32)
    a1 = [a_ref[k, 0:1, :] for k in range(groups)]
    a2 = [a_ref[k, 1:2, :] for k in range(groups)]

    def step(n, carry):
        rows = pl.ds(n, bsz, stride=n_chunks)
        new = []
        for k in range(groups):
            x, xs = carry[2 * k], carry[2 * k + 1]
            xprev_ref[k, rows, :] = x
            new.append(a1[k] * x + a2[k] * xs + contrib_ref[k, rows, :])
            new.append(a1[k] * xs - a2[k] * x + cswap_ref[k, rows, :])
        return tuple(new)

    zero = jnp.zeros((bsz, 2 * SSM_STATE), F32)
    lax.fori_loop(0, n_chunks, step, (zero,) * (2 * groups), unroll=2)
    for k in range(groups):
        y_ref[k] = (jnp.dot(u_ref[k], toep_ref[k], preferred_element_type=F32)
                    + jnp.dot(xprev_ref[k].astype(BF16), cpw_ref[k], preferred_element_type=F32))


def _ssm(u_flat, toep, b_state, b_swap, c_pow, a_step, bsz):
    g, rows, w = u_flat.shape
    gb = SSM_GROUPS_PER_STEP
    per = lambda a, b: pl.BlockSpec((gb, a, b), lambda i: (i, 0, 0))
    state = pltpu.VMEM((gb, rows, 2 * SSM_STATE), F32)
    return pl.pallas_call(
        functools.partial(_ssm_kernel, rows // bsz, bsz),
        grid=(g // gb,),
        in_specs=[per(rows, w), per(w, w), per(w, 2 * SSM_STATE), per(w, 2 * SSM_STATE),
                  per(2 * SSM_STATE, w), per(2, 2 * SSM_STATE)],
        out_specs=per(rows, w),
        out_shape=jax.ShapeDtypeStruct((g, rows, w), F32),
        scratch_shapes=[state, state, state],
        compiler_params=_params(("arbitrary",)),
        name="ssm",
    )(u_flat, toep, b_state, b_swap, c_pow, a_step)


ROW_SLABS = D_MODEL // LANES // 2
_HIGH_HALF = 0xFFFF0000


def _store_row_tiles(ref, value):
    rows = value.shape[0]
    bits = lax.bitcast_convert_type(value.astype(BF16).astype(F32), jnp.uint32)
    for j in range(ROW_SLABS):
        low = bits[:, j * LANES:(j + 1) * LANES] >> 16
        high = bits[:, (j + ROW_SLABS) * LANES:(j + ROW_SLABS + 1) * LANES] & jnp.uint32(_HIGH_HALF)
        ref[pl.ds(j, rows, stride=ROW_SLABS), :] = high | low


def _load_row_tiles(ref, rows):
    words = [ref[pl.ds(j, rows, stride=ROW_SLABS), :] for j in range(ROW_SLABS)]
    low = [lax.bitcast_convert_type(w << 16, F32) for w in words]
    high = [lax.bitcast_convert_type(w & jnp.uint32(_HIGH_HALF), F32) for w in words]
    return jnp.concatenate(low + high, axis=1)


def _row_tile_copy(src_ref, src_row, dst_ref, dst_row, sem):
    src = src_ref.at[pl.ds(pl.multiple_of(src_row * ROW_SLABS, ROW_SLABS), ROW_SLABS), :]
    dst = dst_ref.at[pl.ds(pl.multiple_of(dst_row * ROW_SLABS, ROW_SLABS), ROW_SLABS), :]
    return pltpu.make_async_copy(src, dst, sem)


def _mix_kernel(x_ref, of_ref, yf_ref, u_ref, ga_ref, gb_ref, mod_ref, dsk_ref, wglu_ref, wfox_ref,
                wssm_ref, wo_ref, g2_ref, wr_ref, br_ref, x1_ref, h2_ref, lg_ref, ytok_ref):
    _to_token_major(yf_ref, ytok_ref, yf_ref.shape[1])
    y_ssm = jnp.concatenate([ytok_ref[j] for j in range(SSM_WIDTH // LANES)], axis=1)
    y = y_ssm + dsk_ref[...] * u_ref[...]
    y = 0.5 * y * (1.0 + jnp.tanh(math.sqrt(2.0 / math.pi) * (y + 0.044715 * (y * y * y))))
    gl = jnp.dot(y.astype(BF16), wglu_ref[...], preferred_element_type=F32)
    o_ssm = gl[:, :SSM_WIDTH] * jax.nn.sigmoid(gl[:, SSM_WIDTH:])
    merged = (ga_ref[...].astype(F32) * jnp.dot(of_ref[...], wfox_ref[...], preferred_element_type=F32)
              + gb_ref[...].astype(F32) * jnp.dot(o_ssm.astype(BF16), wssm_ref[...],
                                                  preferred_element_type=F32))
    x1 = x_ref[...] + mod_ref[2:3, :] * jnp.dot(merged.astype(BF16), wo_ref[...],
                                                 preferred_element_type=F32)
    x1_ref[...] = x1
    h2 = _rms_modulate(x1, g2_ref[...], mod_ref[3:4, :], mod_ref[4:5, :])
    _store_row_tiles(h2_ref, h2)
    a1 = _top_bits(h2)
    a2 = _top_bits(h2 - a1)
    lhs = jnp.concatenate([a1, a1, a2], axis=1).astype(BF16)
    lg_ref[...] = jnp.dot(lhs, wr_ref[...], preferred_element_type=F32) + br_ref[...]


def _mix(x2, o_fox, y_flat, u, sga, sgb, mod3, d_skip, w_glu, w_fox, w_ssm, w_o, g2, w_r, b_r, seq):
    n, d = x2.shape
    tm = MIX_TILE
    tpb = seq // tm
    tok = lambda w: pl.BlockSpec((tm, w), lambda i: (i, 0))
    const = lambda a: pl.BlockSpec(a.shape, lambda i: (0,) * a.ndim)
    flat = pl.BlockSpec((SSM_GROUPS, tm // SSM_CHUNK, SSM_CHUNK * SSM_GROUP), lambda i: (0, i, 0))
    return pl.pallas_call(
        _mix_kernel,
        grid=(n // tm,),
        in_specs=[tok(d), tok(FOX_WIDTH), flat, tok(SSM_WIDTH), tok(d), tok(d),
                  pl.BlockSpec((None, N_MOD, d), lambda i: (i // tpb, 0, 0)),
                  const(d_skip), const(w_glu), const(w_fox), const(w_ssm), const(w_o), const(g2),
                  const(w_r), const(b_r)],
        out_specs=[tok(d), pl.BlockSpec((tm * ROW_SLABS, LANES), lambda i: (i, 0)), tok(LANES)],
        out_shape=[jax.ShapeDtypeStruct((n, d), F32), jax.ShapeDtypeStruct((n * ROW_SLABS, LANES), jnp.uint32),
                   jax.ShapeDtypeStruct((n, LANES), F32)],
        scratch_shapes=[pltpu.VMEM((SSM_WIDTH // LANES, tm, LANES), F32)],
        compiler_params=_params(("arbitrary",)),
        name="mix",
    )(x2, o_fox, y_flat, u, sga, sgb, mod3, d_skip, w_glu, w_fox, w_ssm, w_o, g2, w_r, b_r)


def _route_kernel(lg_ref, tri_ref, idx_ref, wt_ref, cnt_ref, carry_ref):
    i = pl.program_id(0)

    @pl.when(i == 0)
    def _():
        carry_ref[...] = jnp.zeros_like(carry_ref)

    lg = lg_ref[...]
    tm = lg.shape[0]
    lane = lax.broadcasted_iota(jnp.int32, (tm, LANES), 1)
    neg = jnp.full_like(lg, -jnp.inf)

    def first_argmax(vals):
        mx = jnp.max(vals, axis=1, keepdims=True)
        ix = jnp.min(jnp.where(vals == mx, lane, LANES), axis=1, keepdims=True)
        return mx, ix

    is_group = lane < N_GROUPS
    g_max, gi = first_argmax(jnp.where(is_group, lg, neg))
    g_sum = jnp.sum(jnp.where(is_group, jnp.exp(lg - g_max), 0.0), axis=1, keepdims=True)
    p_group = 1.0 / g_sum
    lo = N_GROUPS + EXPERTS_PER_GROUP * gi
    in_group = (lane >= lo) & (lane < lo + EXPERTS_PER_GROUP)
    cand = jnp.where(in_group, lg, neg)
    v1, i1 = first_argmax(cand)
    v2, i2 = first_argmax(jnp.where(lane == i1, neg, cand))
    tt = jnp.exp(v2 - v1)
    w1 = p_group / (1.0 + tt)
    w2 = p_group * tt / (1.0 + tt)
    e1 = i1 - N_GROUPS
    e2 = i2 - N_GROUPS
    sel1 = lane == e1
    sel2 = lane == e2
    onehot = (sel1 | sel2).astype(F32)
    before = jnp.dot(tri_ref[...], onehot.astype(BF16), preferred_element_type=F32) + carry_ref[0:1, :]
    r1 = jnp.sum(jnp.where(sel1, before, 0.0), axis=1, keepdims=True).astype(jnp.int32)
    r2 = jnp.sum(jnp.where(sel2, before, 0.0), axis=1, keepdims=True).astype(jnp.int32)
    total = before[-1:, :] + onehot[-1:, :]
    carry_ref[...] = jnp.broadcast_to(total, carry_ref.shape)
    cnt_ref[...] = jnp.broadcast_to(total, cnt_ref.shape)
    idx_ref[...] = jnp.where(lane == 0, e1, jnp.where(lane == 1, e2, jnp.where(lane == 2, r1, r2)))
    wt_ref[...] = jnp.where(lane == 0, w1, w2)


def _route(logits):
    n = logits.shape[0]
    tm = TOK_TILE
    tri = jnp.tril(jnp.ones((tm, tm), BF16), k=-1)
    tok = pl.BlockSpec((tm, LANES), lambda i: (i, 0))
    return pl.pallas_call(
        _route_kernel,
        grid=(n // tm,),
        in_specs=[tok, pl.BlockSpec((tm, tm), lambda i: (0, 0))],
        out_specs=[tok, tok, pl.BlockSpec((SUBLANES, LANES), lambda i: (0, 0))],
        out_shape=[jax.ShapeDtypeStruct((n, LANES), jnp.int32), jax.ShapeDtypeStruct((n, LANES), F32),
                   jax.ShapeDtypeStruct((SUBLANES, LANES), F32)],
        scratch_shapes=[pltpu.VMEM((SUBLANES, LANES), F32)],
        compiler_params=_params(("arbitrary",)),
        name="route",
    )(logits, tri)


ISSUE_UNROLL = 8


def _dispatch_kernel(dest_ref, h_ref, rows_in_ref, rows_ref, sem):
    del rows_in_ref
    tm = h_ref.shape[0] // ROW_SLABS

    def issue(g, c):
        for j in range(ISSUE_UNROLL):
            t = g * ISSUE_UNROLL + j
            _row_tile_copy(h_ref, t, rows_ref, dest_ref[0, 0, t], sem).start(priority=0)
            _row_tile_copy(h_ref, t, rows_ref, dest_ref[0, 0, tm + t], sem).start(priority=1)
        return c

    lax.fori_loop(0, tm // ISSUE_UNROLL, issue, 0)
    for _ in range(2):
        pltpu.make_async_copy(h_ref, rows_ref.at[pl.ds(0, tm * ROW_SLABS), :], sem).wait()


def _dispatch(dest3, h2_tiles, rows_zero):
    tm = MOVE_TILE
    n = h2_tiles.shape[0] // ROW_SLABS
    return pl.pallas_call(
        _dispatch_kernel,
        grid=(n // tm,),
        in_specs=[pl.BlockSpec((1, 1, 2 * tm), lambda i: (i, 0, 0), memory_space=pltpu.SMEM),
                  pl.BlockSpec((tm * ROW_SLABS, LANES), lambda i: (i, 0)),
                  pl.BlockSpec(memory_space=pl.ANY)],
        out_specs=pl.BlockSpec(memory_space=pl.ANY),
        out_shape=jax.ShapeDtypeStruct(rows_zero.shape, rows_zero.dtype),
        scratch_shapes=[pltpu.SemaphoreType.DMA(())],
        input_output_aliases={2: 0},
        compiler_params=_params(("arbitrary",)),
        name="dispatch",
    )(dest3, h2_tiles, rows_zero)


def _combine_kernel(n_steps, dest_ref, dnext_ref, x1_ref, wt_ref, mod_ref, gf_ref, yr_ref, o_ref,
                    buf_ref, sem):
    i = pl.program_id(0)
    tm = x1_ref.shape[0]

    def gather(idx_ref, which):
        def issue(g, c):
            for j in range(ISSUE_UNROLL):
                t = g * ISSUE_UNROLL + j
                _row_tile_copy(yr_ref, idx_ref[0, 0, t], buf_ref.at[which, 0], t,
                               sem.at[which]).start(priority=0)
                _row_tile_copy(yr_ref, idx_ref[0, 0, tm + t], buf_ref.at[which, 1], t,
                               sem.at[which]).start(priority=1)
            return c

        lax.fori_loop(0, tm // ISSUE_UNROLL, issue, 0)

    cur = i % 2

    @pl.when(i == 0)
    def _():
        gather(dest_ref, 0)

    @pl.when(i + 1 < n_steps)
    def _():
        gather(dnext_ref, 1 - cur)

    for slot in range(2):
        pltpu.make_async_copy(yr_ref.at[pl.ds(0, tm * ROW_SLABS), :], buf_ref.at[cur, slot],
                              sem.at[cur]).wait()
    wt = wt_ref[...]
    moe = (wt[:, 0:1] * _load_row_tiles(buf_ref.at[cur, 0], tm)
           + wt[:, 1:2] * _load_row_tiles(buf_ref.at[cur, 1], tm))
    x = x1_ref[...] + mod_ref[5:6, :] * moe
    ms = jnp.mean(x * x, axis=-1, keepdims=True)
    o_ref[...] = (x * lax.rsqrt(ms + RMS_EPS)) * gf_ref[...]


def _combine(dest3, x1, wts, mod3, final_g, y_rows, seq):
    n, d = x1.shape
    tm = MOVE_TILE
    tpb = seq // tm
    n_steps = n // tm
    idx_spec = lambda f: pl.BlockSpec((1, 1, 2 * tm), f, memory_space=pltpu.SMEM)
    return pl.pallas_call(
        functools.partial(_combine_kernel, n_steps),
        grid=(n_steps,),
        in_specs=[idx_spec(lambda i: (i, 0, 0)),
                  idx_spec(lambda i: (jnp.minimum(i + 1, n_steps - 1), 0, 0)),
                  pl.BlockSpec((tm, d), lambda i: (i, 0)),
                  pl.BlockSpec((tm, LANES), lambda i: (i, 0)),
                  pl.BlockSpec((None, N_MOD, d), lambda i: (i // tpb, 0, 0)),
                  pl.BlockSpec((1, d), lambda i: (0, 0)),
                  pl.BlockSpec(memory_space=pl.ANY)],
        out_specs=pl.BlockSpec((tm, d), lambda i: (i, 0)),
        out_shape=jax.ShapeDtypeStruct((n, d), F32),
        scratch_shapes=[pltpu.VMEM((2, 2, tm * ROW_SLABS, LANES), jnp.uint32),
                        pltpu.SemaphoreType.DMA((2,))],
        compiler_params=_params(("arbitrary",)),
        name="combine",
    )(dest3, dest3, x1, wts, mod3, final_g, y_rows)


def _expert_kernel(be_ref, seg_ref, nxt_ref, nv_ref, x_ref, wg_hbm, wu_hbm, wd_hbm, y_ref,
                   wg_buf, wu_buf, wd_buf, wgb_ref, wub_ref, wdb_ref, sem):
    i = pl.program_id(0)
    valid = i < nv_ref[0]
    first = (i == 0) | (be_ref[i] != be_ref[jnp.maximum(i - 1, 0)])
    slot = seg_ref[i] % 2

    def weight_copies(e, s):
        return [pltpu.make_async_copy(hbm.at[e], buf.at[s], sem.at[s])
                for hbm, buf in ((wg_hbm, wg_buf), (wu_hbm, wu_buf), (wd_hbm, wd_buf))]

    @pl.when(valid & (i == 0))
    def _():
        for c in weight_copies(be_ref[0], 0):
            c.start()

    @pl.when(valid & first)
    def _():
        for c in weight_copies(be_ref[i], slot):
            c.wait()

        @pl.when(nxt_ref[i] >= 0)
        def _():
            for c in weight_copies(nxt_ref[i], 1 - slot):
                c.start()

        wgb_ref[...] = wg_buf[slot].astype(BF16)
        wub_ref[...] = wu_buf[slot].astype(BF16)
        wdb_ref[...] = wd_buf[slot].astype(BF16)

    @pl.when(valid)
    def _():
        xb = _load_row_tiles(x_ref, x_ref.shape[0] // ROW_SLABS).astype(BF16)
        a = jnp.dot(xb, wgb_ref[...], preferred_element_type=F32)
        b = jnp.dot(xb, wub_ref[...], preferred_element_type=F32)
        hid = (a * jax.nn.sigmoid(a)) * b
        _store_row_tiles(y_ref, jnp.dot(hid.astype(BF16), wdb_ref[...], preferred_element_type=F32))

    @pl.when(jnp.logical_not(valid))
    def _():
        y_ref[...] = jnp.zeros_like(y_ref)


def _experts(blk_e, n_valid, x_rows, w_gate, w_up, w_down):
    d = D_MODEL
    rows = x_rows.shape[0] // ROW_SLABS
    tb = ROW_BLOCK
    n_blocks = rows // tb
    idx = jnp.arange(n_blocks, dtype=jnp.int32)
    change = (idx == 0) | (blk_e != jnp.roll(blk_e, 1))
    seg = jnp.cumsum(change.astype(jnp.int32)) - 1
    later_start = (idx[None, :] > idx[:, None]) & change[None, :] & (idx[None, :] < n_valid[0])
    none = jnp.int32(N_EXPERTS)
    nxt = jnp.min(jnp.where(later_start, blk_e[None, :], none), axis=1)
    nxt = jnp.where(nxt == none, -1, nxt)
    row_spec = pl.BlockSpec((tb * ROW_SLABS, LANES), lambda i, *_: (i, 0))
    grid_spec = pltpu.PrefetchScalarGridSpec(
        num_scalar_prefetch=4,
        grid=(n_blocks,),
        in_specs=[row_spec, pl.BlockSpec(memory_space=pl.ANY), pl.BlockSpec(memory_space=pl.ANY),
                  pl.BlockSpec(memory_space=pl.ANY)],
        out_specs=row_spec,
        scratch_shapes=[pltpu.VMEM((2, d, D_EXPERT), F32), pltpu.VMEM((2, d, D_EXPERT), F32),
                        pltpu.VMEM((2, D_EXPERT, d), F32),
                        pltpu.VMEM((d, D_EXPERT), BF16), pltpu.VMEM((d, D_EXPERT), BF16),
                        pltpu.VMEM((D_EXPERT, d), BF16), pltpu.SemaphoreType.DMA((2,))],
    )
    return pl.pallas_call(
        _expert_kernel,
        grid_spec=grid_spec,
        out_shape=jax.ShapeDtypeStruct(x_rows.shape, x_rows.dtype),
        compiler_params=_params(("arbitrary",)),
        name="experts",
    )(blk_e, seg, nxt.astype(jnp.int32), n_valid, x_rows, w_gate, w_up, w_down)


def kernel(x, c, w_ada, b_ada, norm_mix_g, w_in, b_forget, w_out_fox, lambda_re, lambda_im, log_dt,
           ssm_b_re, ssm_b_im, ssm_c_re, ssm_c_im, d_skip, w_glu, w_out_ssm, w_o, norm_ffn_g,
           w_router_group, b_router_group, w_router_expert, b_router_expert, w_gate_e, w_up_e,
           w_down_e, final_g):
    bsz, seq, d = x.shape
    n = bsz * seq
    assert w_ada.shape[0] == 1, "the final RMSNorm is fused into the (single) layer's combine kernel"
    xc = x.reshape(n, d)
    for l in range(1):
        mod3 = _mod(c, w_ada[l], b_ada[l]).reshape(bsz, N_MOD, d)

        wi = w_in[l]
        s_q, s_k, s_v, s_f, s_u, s_ga = 512, 1024, 1536, 1544, 2056, 3080
        scale = FOX_HEAD_DIM ** -0.5
        w_all = jnp.concatenate(
            [wi[:, :s_q] * scale, wi[:, s_q:s_k], wi[:, s_f:s_u], wi[:, s_u:s_ga],
             wi[:, s_ga:], jnp.pad(wi[:, s_v:s_f], ((0, 0), (0, LANES - FOX_HEADS)))],
            axis=1).astype(BF16)
        w_vt = wi[:, s_k:s_v].T.astype(BF16)
        bf_pad = jnp.pad(b_forget[l], (0, LANES - FOX_HEADS)).reshape(1, LANES)
        q, k, v_t, u, u_flat, sga, sgb = _inproj(xc, mod3, norm_mix_g[l].reshape(1, d), w_all, w_vt,
                                                 bf_pad, seq)

        o_fox = _attention(q, k, v_t, bsz, seq)

        toep, b_state, b_swap, c_pow, a_step = _ssm_prep(
            lambda_re[l], lambda_im[l], log_dt[l], ssm_b_re[l], ssm_b_im[l], ssm_c_re[l], ssm_c_im[l])
        y_flat = _ssm(u_flat, toep, b_state, b_swap, c_pow, a_step, bsz)

        w_r = jnp.pad(jnp.concatenate([w_router_group[l], w_router_expert[l]], axis=1),
                      ((0, 0), (0, LANES - N_GROUPS - N_EXPERTS)))
        w_r1 = _top_bits(w_r)
        w_r2 = _top_bits(w_r - w_r1)
        w_r = jnp.concatenate([w_r1, w_r2, w_r1], axis=0).astype(BF16)
        b_r = jnp.pad(jnp.concatenate([b_router_group[l], b_router_expert[l]]),
                      (0, LANES - N_GROUPS - N_EXPERTS)).reshape(1, LANES)
        x1, h2, logits = _mix(xc, o_fox, y_flat, u, sga, sgb, mod3, d_skip[l].reshape(1, SSM_WIDTH),
                              w_glu[l].astype(BF16), w_out_fox[l].astype(BF16),
                              w_out_ssm[l].astype(BF16), w_o[l].astype(BF16),
                              norm_ffn_g[l].reshape(1, d), w_r, b_r, seq)

        idx, wts, cnt = _route(logits)
        counts = cnt[0, :N_EXPERTS].astype(jnp.int32)
        pcounts = ((counts + ROW_BLOCK - 1) // ROW_BLOCK) * ROW_BLOCK
        pends = jnp.cumsum(pcounts)
        pstarts = pends - pcounts
        er = idx[:, 0:4].T
        hit = er[0:2, None, :] == jnp.arange(N_EXPERTS, dtype=jnp.int32)[None, :, None]
        dest = jnp.sum(jnp.where(hit, pstarts[None, :, None], 0), axis=1) + er[2:4]
        rows = 2 * n + N_EXPERTS * ROW_BLOCK
        n_blocks = rows // ROW_BLOCK
        blk_start = jnp.arange(n_blocks, dtype=jnp.int32) * ROW_BLOCK
        blk_e = jnp.minimum(jnp.sum((pends[None, :] <= blk_start[:, None]).astype(jnp.int32), axis=1),
                            N_EXPERTS - 1)
        n_valid = (pends[-1:] // ROW_BLOCK).astype(jnp.int32)
        dest3 = (dest.astype(jnp.int32).reshape(2, n // MOVE_TILE, MOVE_TILE).transpose(1, 0, 2)
                 .reshape(n // MOVE_TILE, 1, 2 * MOVE_TILE))

        x_rows = _dispatch(dest3, h2, jnp.zeros((rows * ROW_SLABS, LANES), jnp.uint32))
        y_rows = _experts(blk_e, n_valid, x_rows, w_gate_e[l], w_up_e[l], w_down_e[l])
        xc = _combine(dest3, x1, wts, mod3, final_g.reshape(1, d), y_rows, seq)
    return xc.reshape(bsz, seq, d)
```

```python
import functools
import math

import jax
import jax.numpy as jnp
import numpy as np
from jax import lax
from jax.experimental import pallas as pl
from jax.experimental.pallas import tpu as pltpu

F32 = jnp.float32
BF16 = jnp.bfloat16

D_MODEL = 1024
N_MOD = 6
RMS_EPS = 1e-6
FOX_HEADS = 8
FOX_HEAD_DIM = 64
FOX_WIDTH = FOX_HEADS * FOX_HEAD_DIM
HEAD_PAIRS = FOX_HEADS // 2
SSM_WIDTH = 512
SSM_GROUP = 16
SSM_GROUPS = SSM_WIDTH // SSM_GROUP
SSM_STATE = 64
LAMBDA_RE_MAX = -1e-4
N_GROUPS = 4
EXPERTS_PER_GROUP = 8
N_EXPERTS = N_GROUPS * EXPERTS_PER_GROUP
D_EXPERT = 512

LANES = 128
SUBLANES = 8
VMEM_LIMIT = 56 * 1024 * 1024

SSM_CHUNK = 16
TOK_TILE = 512
MIX_TILE = 256
ATT_Q_TILE = 512
ATT_K_TILE = 256
ROW_BLOCK = 256
MOVE_TILE = 512
NEG_BIG = -1e30

HIGHEST = lax.Precision.HIGHEST


def _params(sem):
    return pltpu.CompilerParams(dimension_semantics=sem, vmem_limit_bytes=VMEM_LIMIT)


def _sigmoid(x):
    return 0.5 * jnp.tanh(0.5 * x) + 0.5


def _rms_modulate(x, gain, shift, scale):
    ms = jnp.mean(x * x, axis=-1, keepdims=True)
    return (x * lax.rsqrt(ms + RMS_EPS)) * gain * (1.0 + scale) + shift


def _mod_kernel(c_ref, w_ref, b_ref, o_ref):
    c = c_ref[...]
    ca = (c * jax.nn.sigmoid(c)).astype(BF16)
    o_ref[...] = jnp.dot(ca, w_ref[...].astype(BF16), preferred_element_type=F32) + b_ref[...]


def _mod(c, w_ada, b_ada):
    bsz, d = c.shape
    cols = w_ada.shape[1]
    tn = 1536
    return pl.pallas_call(
        _mod_kernel,
        grid=(cols // tn,),
        in_specs=[pl.BlockSpec((bsz, d), lambda j: (0, 0)),
                  pl.BlockSpec((d, tn), lambda j: (0, j)),
                  pl.BlockSpec((1, tn), lambda j: (0, j))],
        out_specs=pl.BlockSpec((bsz, tn), lambda j: (0, j)),
        out_shape=jax.ShapeDtypeStruct((bsz, cols), F32),
        compiler_params=_params(("arbitrary",)),
        name="mod",
    )(c, w_ada, b_ada.reshape(1, cols))


_C_Q, _C_K, _C_U, _C_GA, _C_GB, _C_F, _C_END = 0, 512, 1024, 1536, 2560, 3584, 3712


def _lane_block():
    return lax.broadcasted_iota(jnp.int32, (1, LANES), 1) // SSM_GROUP


def _to_group_major(tok_ref, flat_ref, rows):
    blk = _lane_block()
    for half in range(2):
        for j in range(SSM_WIDTH // LANES):
            w = []
            for s8 in range(8):
                v = tok_ref[j, pl.ds(8 * half + s8, rows, stride=SSM_CHUNK), :]
                w.append(pltpu.roll(v, s8 * SSM_GROUP, axis=1) if s8 else v)
            for p in range(8):
                acc = w[0]
                for s8 in range(1, 8):
                    acc = jnp.where(blk == (p + s8) % 8, w[s8], acc)
                flat_ref[8 * j + p, :, half * LANES:(half + 1) * LANES] = acc.astype(flat_ref.dtype)


def _to_token_major(flat_ref, tok_ref, rows):
    blk = _lane_block()
    for half in range(2):
        for j in range(SSM_WIDTH // LANES):
            ys = [flat_ref[8 * j + p, :, half * LANES:(half + 1) * LANES] for p in range(8)]
            for s8 in range(8):
                acc = ys[0]
                for p in range(1, 8):
                    acc = jnp.where(blk == (p + s8) % 8, ys[p], acc)
                if s8:
                    acc = pltpu.roll(acc, LANES - s8 * SSM_GROUP, axis=1)
                tok_ref[j, pl.ds(8 * half + s8, rows, stride=SSM_CHUNK), :] = acc


def _bias_lane_placement():
    pq = np.zeros((3 * LANES, LANES), np.float32)
    pk = np.zeros((3 * LANES, LANES), np.float32)
    bq = np.zeros((1, LANES), np.float32)
    bk = np.zeros((1, LANES), np.float32)
    for head in range(FOX_HEADS):
        base = head * 8
        for term in range(3):
            pq[term * LANES + head, base + term] = 1.0
            pk[term * LANES + head, base + 3 + term] = -1.0
            bq[0, base + 3 + term] = 1.0
            bk[0, base + term] = 1.0
    return pq, pk, bq, bk


def _top_bits(a):
    bits = lax.bitcast_convert_type(a, jnp.uint32) & jnp.uint32(0xFFFF0000)
    return lax.bitcast_convert_type(bits, F32)


def _inproj_kernel(tiles_per_batch, x_ref, mod_ref, g_ref, w_ref, wvt_ref, bf_ref, tri_ref,
                   pq_ref, pk_ref, bq_ref, bk_ref,
                   q_ref, k_ref, vt_ref, u_ref, uflat_ref, ga_ref, gb_ref, carry_ref, uslab_ref):
    i = pl.program_id(0)
    h = _rms_modulate(x_ref[...], g_ref[...], mod_ref[0:1, :], mod_ref[1:2, :])
    hb = h.astype(BF16)

    def proj(a, b):
        return jnp.dot(hb, w_ref[:, a:b], preferred_element_type=F32)

    q = proj(_C_Q, _C_K).astype(BF16)
    k = proj(_C_K, _C_U).astype(BF16)
    vt_ref[...] = lax.dot_general(wvt_ref[...], hb, (((1,), (1,)), ((), ())),
                                  preferred_element_type=F32).astype(BF16)
    u = proj(_C_U, _C_GA)
    u_ref[...] = u
    for j in range(SSM_WIDTH // LANES):
        uslab_ref[j] = u[:, j * LANES:(j + 1) * LANES]
    _to_group_major(uslab_ref, uflat_ref, u.shape[0] // SSM_CHUNK)
    ga_ref[...] = _sigmoid(proj(_C_GA, _C_GB)).astype(BF16)
    gb_ref[...] = _sigmoid(proj(_C_GB, _C_F)).astype(BF16)

    f = proj(_C_F, _C_END) + bf_ref[...]
    logf = jnp.minimum(f, 0.0) - jnp.log(1.0 + jnp.exp(-jnp.abs(f)))

    @pl.when(i % tiles_per_batch == 0)
    def _():
        carry_ref[...] = jnp.zeros_like(carry_ref)

    def split3(a):
        hi = _top_bits(a)
        r1 = a - hi
        mid = _top_bits(r1)
        return jnp.concatenate([hi, mid, _top_bits(r1 - mid)], axis=1).astype(BF16)

    part = jnp.dot(tri_ref[...], split3(logf), preferred_element_type=F32)
    cs = (part[:, :LANES] + part[:, LANES:2 * LANES] + part[:, 2 * LANES:]) + carry_ref[0:1, :]
    carry_ref[...] = jnp.broadcast_to(cs[-1:, :], carry_ref.shape)

    terms = split3(cs)
    bias_q = (jnp.dot(terms, pq_ref[...], preferred_element_type=F32) + bq_ref[...]).astype(BF16)
    bias_k = (jnp.dot(terms, pk_ref[...], preferred_element_type=F32) + bk_ref[...]).astype(BF16)
    for p in range(HEAD_PAIRS):
        lanes = slice(p * LANES, (p + 1) * LANES)
        q_ref[:, 2 * p * LANES:(2 * p + 1) * LANES] = q[:, lanes]
        q_ref[:, (2 * p + 1) * LANES:(2 * p + 2) * LANES] = bias_q
        k_ref[:, 2 * p * LANES:(2 * p + 1) * LANES] = k[:, lanes]
        k_ref[:, (2 * p + 1) * LANES:(2 * p + 2) * LANES] = bias_k


def _inproj(x2, mod3, gain, w_all, w_vt, bf_pad, seq):
    n, d = x2.shape
    tm = TOK_TILE
    tpb = seq // tm
    tri = jnp.tril(jnp.ones((tm, tm), BF16))
    pq, pk, bq, bk = _bias_lane_placement()
    tok = lambda w: pl.BlockSpec((tm, w), lambda i: (i, 0))
    const = lambda shape: pl.BlockSpec(shape, lambda i: (0,) * len(shape))
    qk_width = 2 * FOX_WIDTH
    return pl.pallas_call(
        functools.partial(_inproj_kernel, tpb),
        grid=(n // tm,),
        in_specs=[tok(d),
                  pl.BlockSpec((None, N_MOD, d), lambda i: (i // tpb, 0, 0)),
                  const((1, d)), const((d, _C_END)), const((FOX_WIDTH, d)), const((1, LANES)),
                  const((tm, tm)), const(pq.shape), const(pk.shape), const(bq.shape), const(bk.shape)],
        out_specs=[tok(qk_width), tok(qk_width), pl.BlockSpec((FOX_WIDTH, tm), lambda i: (0, i)),
                   tok(SSM_WIDTH),
                   pl.BlockSpec((SSM_GROUPS, tm // SSM_CHUNK, SSM_CHUNK * SSM_GROUP), lambda i: (0, i, 0)),
                   tok(d), tok(d)],
        out_shape=[jax.ShapeDtypeStruct((n, qk_width), BF16)] * 2
        + [jax.ShapeDtypeStruct((FOX_WIDTH, n), BF16)]
        + [jax.ShapeDtypeStruct((n, SSM_WIDTH), F32)]
        + [jax.ShapeDtypeStruct((SSM_GROUPS, n // SSM_CHUNK, SSM_CHUNK * SSM_GROUP), BF16)]
        + [jax.ShapeDtypeStruct((n, d), BF16)] * 2,
        scratch_shapes=[pltpu.VMEM((SUBLANES, LANES), F32),
                        pltpu.VMEM((SSM_WIDTH // LANES, tm, LANES), F32)],
        compiler_params=_params(("arbitrary",)),
        name="inproj",
    )(x2, mod3, gain, w_all, w_vt, bf_pad, tri, jnp.asarray(pq, BF16), jnp.asarray(pk, BF16),
      jnp.asarray(bq), jnp.asarray(bk))


def _attn_kernel(q_ref, k_ref, vt_ref, o_ref, m_ref, acc_ref, sa_ref, sb_ref):
    i = pl.program_id(2)
    tq, tk = ATT_Q_TILE, ATT_K_TILE
    q = q_ref[...]
    lane = lax.broadcasted_iota(jnp.int32, (1, 2 * LANES), 1)
    zq = jnp.zeros_like(q)
    half = FOX_HEAD_DIM
    bias0 = LANES + 16 * pl.program_id(1)
    own0 = (lane < half) | ((lane >= bias0) & (lane < bias0 + 8))
    own1 = ((lane >= half) & (lane < LANES)) | ((lane >= bias0 + 8) & (lane < bias0 + 16))
    q_both = jnp.concatenate([jnp.where(own0, q, zq), jnp.where(own1, q, zq)], axis=0)
    m_ref[...] = jnp.full(m_ref.shape, NEG_BIG, F32)
    acc_ref[...] = jnp.zeros(acc_ref.shape, F32)
    ones_rows = jnp.ones((2 * SUBLANES, tk), BF16)
    key_in_tile = lax.broadcasted_iota(jnp.int32, (tk, 2 * tq), 0)
    qry_pos = i * tq + (lax.broadcasted_iota(jnp.int32, (tk, 2 * tq), 1) & (tq - 1))

    def scores(j, s_ref):
        start = pl.multiple_of(j * tk, tk)
        s = lax.dot_general(k_ref[pl.ds(start, tk), :], q_both, (((1,), (1,)), ((), ())),
                            preferred_element_type=F32)
        s_ref[...] = jnp.where(key_in_tile + j * tk <= qry_pos, s, NEG_BIG)

    def accumulate(j, s_ref):
        start = pl.multiple_of(j * tk, tk)
        va = jnp.concatenate([vt_ref[:, pl.ds(start, tk)], ones_rows], axis=0)
        m_old = m_ref[...]
        m_new = jnp.maximum(m_old, jnp.max(s_ref[...], axis=0, keepdims=True))
        alpha = jnp.exp(m_old - m_new)
        p = jnp.exp(s_ref[...] - m_new).astype(BF16)
        acc_ref[...] = alpha * acc_ref[...] + jnp.dot(va, p, preferred_element_type=F32)
        m_ref[...] = m_new

    def body(jj, c):
        t0 = 2 * jj
        scores(t0 + 1, sb_ref)
        accumulate(t0, sa_ref)
        scores(t0 + 2, sa_ref)
        accumulate(t0 + 1, sb_ref)
        return c

    assert tq == 2 * tk
    scores(0, sa_ref)
    lax.fori_loop(0, i, body, 0)
    scores(2 * i + 1, sb_ref)
    accumulate(2 * i, sa_ref)
    accumulate(2 * i + 1, sb_ref)

    acc = acc_ref[...]
    o_t = jnp.concatenate([acc[0:half, 0:tq] / acc[LANES:LANES + 1, 0:tq],
                           acc[half:LANES, tq:2 * tq] / acc[LANES:LANES + 1, tq:2 * tq]], axis=0)
    o_ref[...] = o_t.T.astype(o_ref.dtype)


def _attention(q, k, v_t, bsz, seq):
    n = q.shape[0]
    t = ATT_Q_TILE
    nq = seq // t
    return pl.pallas_call(
        _attn_kernel,
        grid=(bsz, HEAD_PAIRS, nq),
        in_specs=[pl.BlockSpec((t, 2 * LANES), lambda b, p, i: (b * nq + i, p)),
                  pl.BlockSpec((seq, 2 * LANES), lambda b, p, i: (b, p)),
                  pl.BlockSpec((LANES, seq), lambda b, p, i: (p, b))],
        out_specs=pl.BlockSpec((t, LANES), lambda b, p, i: (b * nq + i, p)),
        out_shape=jax.ShapeDtypeStruct((n, FOX_WIDTH), BF16),
        scratch_shapes=[pltpu.VMEM((1, 2 * t), F32), pltpu.VMEM((LANES + 2 * SUBLANES, 2 * t), F32),
                        pltpu.VMEM((ATT_K_TILE, 2 * t), F32), pltpu.VMEM((ATT_K_TILE, 2 * t), F32)],
        compiler_params=_params(("arbitrary", "arbitrary", "arbitrary")),
        name="attn",
    )(q, k, v_t)


def _ssm_prep_kernel(lrow_ref, lcol_ref, ldt_ref, btr_ref, bti_ref, ctr_ref, cti_ref,
                     toep_ref, bst_ref, bsw_ref, cpw_ref, a_ref):
    p8 = pl.program_id(0) % 8
    t_len, grp = SSM_CHUNK, SSM_GROUP
    dt = jnp.exp(ldt_ref[...])
    lr, li = jnp.minimum(lrow_ref[0:1, :], LAMBDA_RE_MAX), lrow_ref[1:2, :]

    def powers(steps, re, im):
        mag = jnp.exp(steps * (re * dt))
        return mag * jnp.cos(steps * (im * dt)), mag * jnp.sin(steps * (im * dt))

    a_re, a_im = powers(1.0, lr, li)
    den = lr * lr + li * li
    nr = a_re - 1.0
    co_re = (nr * lr + a_im * li) / den
    co_im = (a_im * lr - nr * li) / den
    bbt_re = co_re * btr_ref[...] - co_im * bti_ref[...]
    bbt_im = co_re * bti_ref[...] + co_im * btr_ref[...]

    lag = (lax.broadcasted_iota(jnp.int32, (1, t_len * grp), 1) // grp).astype(F32)
    lcr, lci = jnp.minimum(lcol_ref[:, 0:1], LAMBDA_RE_MAX), lcol_ref[:, 1:2]

    def c_times_power(steps):
        p_re, p_im = powers(steps, lcr, lci)
        return (ctr_ref[...] * p_re - cti_ref[...] * p_im, ctr_ref[...] * p_im + cti_ref[...] * p_re)

    wt_re, wt_im = c_times_power(lag)
    kern = (jnp.dot(bbt_re, wt_re, precision=HIGHEST, preferred_element_type=F32)
            - jnp.dot(bbt_im, wt_im, precision=HIGHEST, preferred_element_type=F32))

    lane = lax.broadcasted_iota(jnp.int32, (1, LANES), 1)
    col_shift = p8 * grp

    def store_cols(ref, rows, lo_half, hi_half):
        ref[rows, 0:LANES] = pltpu.roll(lo_half, col_shift, axis=1).astype(ref.dtype)
        ref[rows, LANES:2 * LANES] = pltpu.roll(hi_half, col_shift, axis=1).astype(ref.dtype)

    def slot_rows(s):
        half, s8 = divmod(s, 8)
        return pl.ds(pl.multiple_of((8 * half + (s8 + p8) % 8) * grp, grp), grp)

    back = (t_len - 1 - lax.broadcasted_iota(jnp.int32, (t_len, 1), 0)).astype(F32)
    e_re, e_im = powers(back, lr, li)
    zero = jnp.zeros((grp, LANES), F32)
    k_lo, k_hi = kern[:, 0:LANES], kern[:, LANES:2 * LANES]
    for s in range(t_len):
        half, s8 = divmod(s, 8)
        keep = lane >= s8 * grp
        r_lo = pltpu.roll(k_lo, s8 * grp, axis=1) if s8 else k_lo
        r_hi = pltpu.roll(k_hi, s8 * grp, axis=1) if s8 else k_hi
        if half == 0:
            lo, hi = jnp.where(keep, r_lo, 0.0), jnp.where(keep, r_hi, r_lo)
        else:
            lo, hi = zero, jnp.where(keep, r_lo, 0.0)
        store_cols(toep_ref, slot_rows(s), lo, hi)
        es_re, es_im = e_re[s:s + 1, :], e_im[s:s + 1, :]
        bs_re = es_re * bbt_re - es_im * bbt_im
        bs_im = es_re * bbt_im + es_im * bbt_re
        bst_ref[slot_rows(s), :] = jnp.concatenate([bs_re, bs_im], axis=1).astype(bst_ref.dtype)
        bsw_ref[slot_rows(s), :] = jnp.concatenate([bs_im, bs_re], axis=1).astype(bsw_ref.dtype)

    w1_re, w1_im = c_times_power(lag + 1.0)
    store_cols(cpw_ref, pl.ds(0, SSM_STATE), w1_re[:, 0:LANES], w1_re[:, LANES:2 * LANES])
    store_cols(cpw_ref, pl.ds(SSM_STATE, SSM_STATE), -w1_im[:, 0:LANES], -w1_im[:, LANES:2 * LANES])
    s_re, s_im = powers(float(t_len), lr, li)
    a_ref[0:1, :] = jnp.concatenate([s_re, s_re], axis=1)
    a_ref[1:2, :] = jnp.concatenate([-s_im, s_im], axis=1)


def _ssm_prep(lambda_re, lambda_im, log_dt, b_re, b_im, c_re, c_im):
    width = SSM_CHUNK * SSM_GROUP
    lam_row = jnp.stack([lambda_re, lambda_im], axis=1)
    tiled = lambda c: jnp.tile(c.transpose(0, 2, 1), (1, 1, SSM_CHUNK))
    per = lambda a, b: pl.BlockSpec((None, a, b), lambda g: (g, 0, 0))
    return pl.pallas_call(
        _ssm_prep_kernel,
        grid=(SSM_GROUPS,),
        in_specs=[per(2, SSM_STATE), per(SSM_STATE, 2), per(1, 1), per(SSM_GROUP, SSM_STATE),
                  per(SSM_GROUP, SSM_STATE), per(SSM_STATE, width), per(SSM_STATE, width)],
        out_specs=[per(width, width), per(width, 2 * SSM_STATE), per(width, 2 * SSM_STATE),
                   per(2 * SSM_STATE, width), per(2, 2 * SSM_STATE)],
        out_shape=[jax.ShapeDtypeStruct((SSM_GROUPS, width, width), BF16),
                   jax.ShapeDtypeStruct((SSM_GROUPS, width, 2 * SSM_STATE), BF16),
                   jax.ShapeDtypeStruct((SSM_GROUPS, width, 2 * SSM_STATE), BF16),
                   jax.ShapeDtypeStruct((SSM_GROUPS, 2 * SSM_STATE, width), BF16),
                   jax.ShapeDtypeStruct((SSM_GROUPS, 2, 2 * SSM_STATE), F32)],
        compiler_params=_params(("arbitrary",)),
        name="ssm_prep",
    )(lam_row, lam_row.transpose(0, 2, 1), log_dt.reshape(SSM_GROUPS, 1, 1),
      b_re.transpose(0, 2, 1), b_im.transpose(0, 2, 1), tiled(c_re), tiled(c_im))


SSM_GROUPS_PER_STEP = 4


def _ssm_kernel(n_chunks, bsz, u_ref, toep_ref, bst_ref, bsw_ref, cpw_ref, a_ref, y_ref,
                contrib_ref, cswap_ref, xprev_ref):
    groups = u_ref.shape[0]
    for k in range(groups):
        u = u_ref[k]
        contrib_ref[k] = jnp.dot(u, bst_ref[k], preferred_element_type=F32)
        cswap_ref[k] = jnp.dot(u, bsw_ref[k], preferred_element_type=F32)
    a1 = [a_ref[k, 0:1, :] for k in range(groups)]
    a2 = [a_ref[k, 1:2, :] for k in range(groups)]

    def step(n, carry):
        rows = pl.ds(n, bsz, stride=n_chunks)
        new = []
        for k in range(groups):
            x, xs = carry[2 * k], carry[2 * k + 1]
            xprev_ref[k, rows, :] = x
            new.append(a1[k] * x + a2[k] * xs + contrib_ref[k, rows, :])
            new.append(a1[k] * xs - a2[k] * x + cswap_ref[k, rows, :])
        return tuple(new)

    zero = jnp.zeros((bsz, 2 * SSM_STATE), F32)
    lax.fori_loop(0, n_chunks, step, (zero,) * (2 * groups), unroll=2)
    for k in range(groups):
        y_ref[k] = (jnp.dot(u_ref[k], toep_ref[k], preferred_element_type=F32)
                    + jnp.dot(xprev_ref[k].astype(BF16), cpw_ref[k], preferred_element_type=F32))


def _ssm(u_flat, toep, b_state, b_swap, c_pow, a_step, bsz):
    g, rows, w = u_flat.shape
    gb = SSM_GROUPS_PER_STEP
    per = lambda a, b: pl.BlockSpec((gb, a, b), lambda i: (i, 0, 0))
    state = pltpu.VMEM((gb, rows, 2 * SSM_STATE), F32)
    return pl.pallas_call(
        functools.partial(_ssm_kernel, rows // bsz, bsz),
        grid=(g // gb,),
        in_specs=[per(rows, w), per(w, w), per(w, 2 * SSM_STATE), per(w, 2 * SSM_STATE),
                  per(2 * SSM_STATE, w), per(2, 2 * SSM_STATE)],
        out_specs=per(rows, w),
        out_shape=jax.ShapeDtypeStruct((g, rows, w), F32),
        scratch_shapes=[state, state, state],
        compiler_params=_params(("arbitrary",)),
        name="ssm",
    )(u_flat, toep, b_state, b_swap, c_pow, a_step)


ROW_SLABS = D_MODEL // LANES // 2
_HIGH_HALF = 0xFFFF0000


def _store_row_tiles(ref, value):
    rows = value.shape[0]
    bits = lax.bitcast_convert_type(value.astype(BF16).astype(F32), jnp.uint32)
    for j in range(ROW_SLABS):
        low = bits[:, j * LANES:(j + 1) * LANES] >> 16
        high = bits[:, (j + ROW_SLABS) * LANES:(j + ROW_SLABS + 1) * LANES] & jnp.uint32(_HIGH_HALF)
        ref[pl.ds(j, rows, stride=ROW_SLABS), :] = high | low


def _load_row_tiles(ref, rows):
    words = [ref[pl.ds(j, rows, stride=ROW_SLABS), :] for j in range(ROW_SLABS)]
    low = [lax.bitcast_convert_type(w << 16, F32) for w in words]
    high = [lax.bitcast_convert_type(w & jnp.uint32(_HIGH_HALF), F32) for w in words]
    return jnp.concatenate(low + high, axis=1)


def _row_tile_copy(src_ref, src_row, dst_ref, dst_row, sem):
    src = src_ref.at[pl.ds(pl.multiple_of(src_row * ROW_SLABS, ROW_SLABS), ROW_SLABS), :]
    dst = dst_ref.at[pl.ds(pl.multiple_of(dst_row * ROW_SLABS, ROW_SLABS), ROW_SLABS), :]
    return pltpu.make_async_copy(src, dst, sem)


def _mix_kernel(x_ref, of_ref, yf_ref, u_ref, ga_ref, gb_ref, mod_ref, dsk_ref, wglu_ref, wfox_ref,
                wssm_ref, wo_ref, g2_ref, wr_ref, br_ref, x1_ref, h2_ref, lg_ref, ytok_ref):
    _to_token_major(yf_ref, ytok_ref, yf_ref.shape[1])
    y_ssm = jnp.concatenate([ytok_ref[j] for j in range(SSM_WIDTH // LANES)], axis=1)
    y = y_ssm + dsk_ref[...] * u_ref[...]
    y = 0.5 * y * (1.0 + jnp.tanh(math.sqrt(2.0 / math.pi) * (y + 0.044715 * (y * y * y))))
    gl = jnp.dot(y.astype(BF16), wglu_ref[...], preferred_element_type=F32)
    o_ssm = gl[:, :SSM_WIDTH] * _sigmoid(gl[:, SSM_WIDTH:])
    merged = (ga_ref[...].astype(F32) * jnp.dot(of_ref[...], wfox_ref[...], preferred_element_type=F32)
              + gb_ref[...].astype(F32) * jnp.dot(o_ssm.astype(BF16), wssm_ref[...],
                                                  preferred_element_type=F32))
    x1 = x_ref[...] + mod_ref[2:3, :] * jnp.dot(merged.astype(BF16), wo_ref[...],
                                                 preferred_element_type=F32)
    x1_ref[...] = x1
    h2 = _rms_modulate(x1, g2_ref[...], mod_ref[3:4, :], mod_ref[4:5, :])
    _store_row_tiles(h2_ref, h2)
    a1 = _top_bits(h2)
    a2 = _top_bits(h2 - a1)
    lhs = jnp.concatenate([a1, a1, a2], axis=1).astype(BF16)
    lg_ref[...] = jnp.dot(lhs, wr_ref[...], preferred_element_type=F32) + br_ref[...]


def _mix(x2, o_fox, y_flat, u, sga, sgb, mod3, d_skip, w_glu, w_fox, w_ssm, w_o, g2, w_r, b_r, seq):
    n, d = x2.shape
    tm = MIX_TILE
    tpb = seq // tm
    tok = lambda w: pl.BlockSpec((tm, w), lambda i: (i, 0))
    const = lambda a: pl.BlockSpec(a.shape, lambda i: (0,) * a.ndim)
    flat = pl.BlockSpec((SSM_GROUPS, tm // SSM_CHUNK, SSM_CHUNK * SSM_GROUP), lambda i: (0, i, 0))
    return pl.pallas_call(
        _mix_kernel,
        grid=(n // tm,),
        in_specs=[tok(d), tok(FOX_WIDTH), flat, tok(SSM_WIDTH), tok(d), tok(d),
                  pl.BlockSpec((None, N_MOD, d), lambda i: (i // tpb, 0, 0)),
                  const(d_skip), const(w_glu), const(w_fox), const(w_ssm), const(w_o), const(g2),
                  const(w_r), const(b_r)],
        out_specs=[tok(d), pl.BlockSpec((tm * ROW_SLABS, LANES), lambda i: (i, 0)), tok(LANES)],
        out_shape=[jax.ShapeDtypeStruct((n, d), F32), jax.ShapeDtypeStruct((n * ROW_SLABS, LANES), jnp.uint32),
                   jax.ShapeDtypeStruct((n, LANES), F32)],
        scratch_shapes=[pltpu.VMEM((SSM_WIDTH // LANES, tm, LANES), F32)],
        compiler_params=_params(("arbitrary",)),
        name="mix",
    )(x2, o_fox, y_flat, u, sga, sgb, mod3, d_skip, w_glu, w_fox, w_ssm, w_o, g2, w_r, b_r)


def _route_kernel(lg_ref, tri_ref, idx_ref, wt_ref, cnt_ref, carry_ref):
    i = pl.program_id(0)

    @pl.when(i == 0)
    def _():
        carry_ref[...] = jnp.zeros_like(carry_ref)

    lg = lg_ref[...]
    tm = lg.shape[0]
    lane = lax.broadcasted_iota(jnp.int32, (tm, LANES), 1)
    neg = jnp.full_like(lg, -jnp.inf)

    def first_argmax(vals):
        mx = jnp.max(vals, axis=1, keepdims=True)
        ix = jnp.min(jnp.where(vals == mx, lane, LANES), axis=1, keepdims=True)
        return mx, ix

    is_group = lane < N_GROUPS
    g_max, gi = first_argmax(jnp.where(is_group, lg, neg))
    g_sum = jnp.sum(jnp.where(is_group, jnp.exp(lg - g_max), 0.0), axis=1, keepdims=True)
    p_group = 1.0 / g_sum
    lo = N_GROUPS + EXPERTS_PER_GROUP * gi
    in_group = (lane >= lo) & (lane < lo + EXPERTS_PER_GROUP)
    cand = jnp.where(in_group, lg, neg)
    v1, i1 = first_argmax(cand)
    v2, i2 = first_argmax(jnp.where(lane == i1, neg, cand))
    tt = jnp.exp(v2 - v1)
    w1 = p_group / (1.0 + tt)
    w2 = p_group * tt / (1.0 + tt)
    e1 = i1 - N_GROUPS
    e2 = i2 - N_GROUPS
    sel1 = lane == e1
    sel2 = lane == e2
    onehot = (sel1 | sel2).astype(F32)
    before = jnp.dot(tri_ref[...], onehot.astype(BF16), preferred_element_type=F32) + carry_ref[0:1, :]
    r1 = jnp.sum(jnp.where(sel1, before, 0.0), axis=1, keepdims=True).astype(jnp.int32)
    r2 = jnp.sum(jnp.where(sel2, before, 0.0), axis=1, keepdims=True).astype(jnp.int32)
    total = before[-1:, :] + onehot[-1:, :]
    carry_ref[...] = jnp.broadcast_to(total, carry_ref.shape)
    cnt_ref[...] = jnp.broadcast_to(total, cnt_ref.shape)
    idx_ref[...] = jnp.where(lane == 0, e1, jnp.where(lane == 1, e2, jnp.where(lane == 2, r1, r2)))
    wt_ref[...] = jnp.where(lane == 0, w1, w2)


def _route(logits):
    n = logits.shape[0]
    tm = TOK_TILE
    tri = jnp.tril(jnp.ones((tm, tm), BF16), k=-1)
    tok = pl.BlockSpec((tm, LANES), lambda i: (i, 0))
    return pl.pallas_call(
        _route_kernel,
        grid=(n // tm,),
        in_specs=[tok, pl.BlockSpec((tm, tm), lambda i: (0, 0))],
        out_specs=[tok, tok, pl.BlockSpec((SUBLANES, LANES), lambda i: (0, 0))],
        out_shape=[jax.ShapeDtypeStruct((n, LANES), jnp.int32), jax.ShapeDtypeStruct((n, LANES), F32),
                   jax.ShapeDtypeStruct((SUBLANES, LANES), F32)],
        scratch_shapes=[pltpu.VMEM((SUBLANES, LANES), F32)],
        compiler_params=_params(("arbitrary",)),
        name="route",
    )(logits, tri)


ISSUE_UNROLL = 8


def _dispatch_kernel(dest_ref, h_ref, rows_in_ref, rows_ref, sem):
    del rows_in_ref
    tm = h_ref.shape[0] // ROW_SLABS

    def issue(g, c):
        for j in range(ISSUE_UNROLL):
            t = g * ISSUE_UNROLL + j
            _row_tile_copy(h_ref, t, rows_ref, dest_ref[0, 0, t], sem).start(priority=0)
            _row_tile_copy(h_ref, t, rows_ref, dest_ref[0, 0, tm + t], sem).start(priority=1)
        return c

    lax.fori_loop(0, tm // ISSUE_UNROLL, issue, 0)
    for _ in range(2):
        pltpu.make_async_copy(h_ref, rows_ref.at[pl.ds(0, tm * ROW_SLABS), :], sem).wait()


def _dispatch(dest3, h2_tiles, rows_zero):
    tm = MOVE_TILE
    n = h2_tiles.shape[0] // ROW_SLABS
    return pl.pallas_call(
        _dispatch_kernel,
        grid=(n // tm,),
        in_specs=[pl.BlockSpec((1, 1, 2 * tm), lambda i: (i, 0, 0), memory_space=pltpu.SMEM),
                  pl.BlockSpec((tm * ROW_SLABS, LANES), lambda i: (i, 0)),
                  pl.BlockSpec(memory_space=pl.ANY)],
        out_specs=pl.BlockSpec(memory_space=pl.ANY),
        out_shape=jax.ShapeDtypeStruct(rows_zero.shape, rows_zero.dtype),
        scratch_shapes=[pltpu.SemaphoreType.DMA(())],
        input_output_aliases={2: 0},
        compiler_params=_params(("arbitrary",)),
        name="dispatch",
    )(dest3, h2_tiles, rows_zero)


def _combine_kernel(n_steps, dest_ref, dnext_ref, x1_ref, wt_ref, mod_ref, gf_ref, yr_ref, o_ref,
                    buf_ref, sem):
    i = pl.program_id(0)
    tm = x1_ref.shape[0]

    def gather(idx_ref, which):
        def issue(g, c):
            for j in range(ISSUE_UNROLL):
                t = g * ISSUE_UNROLL + j
                _row_tile_copy(yr_ref, idx_ref[0, 0, t], buf_ref.at[which, 0], t,
                               sem.at[which]).start(priority=0)
                _row_tile_copy(yr_ref, idx_ref[0, 0, tm + t], buf_ref.at[which, 1], t,
                               sem.at[which]).start(priority=1)
            return c

        lax.fori_loop(0, tm // ISSUE_UNROLL, issue, 0)

    cur = i % 2

    @pl.when(i == 0)
    def _():
        gather(dest_ref, 0)

    @pl.when(i + 1 < n_steps)
    def _():
        gather(dnext_ref, 1 - cur)

    for slot in range(2):
        pltpu.make_async_copy(yr_ref.at[pl.ds(0, tm * ROW_SLABS), :], buf_ref.at[cur, slot],
                              sem.at[cur]).wait()
    wt = wt_ref[...]
    moe = (wt[:, 0:1] * _load_row_tiles(buf_ref.at[cur, 0], tm)
           + wt[:, 1:2] * _load_row_tiles(buf_ref.at[cur, 1], tm))
    x = x1_ref[...] + mod_ref[5:6, :] * moe
    ms = jnp.mean(x * x, axis=-1, keepdims=True)
    o_ref[...] = (x * lax.rsqrt(ms + RMS_EPS)) * gf_ref[...]


def _combine(dest3, x1, wts, mod3, final_g, y_rows, seq):
    n, d = x1.shape
    tm = MOVE_TILE
    tpb = seq // tm
    n_steps = n // tm
    idx_spec = lambda f: pl.BlockSpec((1, 1, 2 * tm), f, memory_space=pltpu.SMEM)
    return pl.pallas_call(
        functools.partial(_combine_kernel, n_steps),
        grid=(n_steps,),
        in_specs=[idx_spec(lambda i: (i, 0, 0)),
                  idx_spec(lambda i: (jnp.minimum(i + 1, n_steps - 1), 0, 0)),
                  pl.BlockSpec((tm, d), lambda i: (i, 0)),
                  pl.BlockSpec((tm, LANES), lambda i: (i, 0)),
                  pl.BlockSpec((None, N_MOD, d), lambda i: (i // tpb, 0, 0)),
                  pl.BlockSpec((1, d), lambda i: (0, 0)),
                  pl.BlockSpec(memory_space=pl.ANY)],
        out_specs=pl.BlockSpec((tm, d), lambda i: (i, 0)),
        out_shape=jax.ShapeDtypeStruct((n, d), F32),
        scratch_shapes=[pltpu.VMEM((2, 2, tm * ROW_SLABS, LANES), jnp.uint32),
                        pltpu.SemaphoreType.DMA((2,))],
        compiler_params=_params(("arbitrary",)),
        name="combine",
    )(dest3, dest3, x1, wts, mod3, final_g, y_rows)


def _expert_kernel(be_ref, seg_ref, nxt_ref, nv_ref, x_ref, wg_hbm, wu_hbm, wd_hbm, y_ref,
                   wg_buf, wu_buf, wd_buf, wgb_ref, wub_ref, wdb_ref, sem):
    i = pl.program_id(0)
    valid = i < nv_ref[0]
    first = (i == 0) | (be_ref[i] != be_ref[jnp.maximum(i - 1, 0)])
    slot = seg_ref[i] % 2

    def weight_copies(e, s):
        return [pltpu.make_async_copy(hbm.at[e], buf.at[s], sem.at[s])
                for hbm, buf in ((wg_hbm, wg_buf), (wu_hbm, wu_buf), (wd_hbm, wd_buf))]

    @pl.when(valid & (i == 0))
    def _():
        for c in weight_copies(be_ref[0], 0):
            c.start()

    @pl.when(valid & first)
    def _():
        for c in weight_copies(be_ref[i], slot):
            c.wait()

        @pl.when(nxt_ref[i] >= 0)
        def _():
            for c in weight_copies(nxt_ref[i], 1 - slot):
                c.start()

        wgb_ref[...] = wg_buf[slot].astype(BF16)
        wub_ref[...] = wu_buf[slot].astype(BF16)
        wdb_ref[...] = wd_buf[slot].astype(BF16)

    @pl.when(valid)
    def _():
        xb = _load_row_tiles(x_ref, x_ref.shape[0] // ROW_SLABS).astype(BF16)
        a = jnp.dot(xb, wgb_ref[...], preferred_element_type=F32)
        b = jnp.dot(xb, wub_ref[...], preferred_element_type=F32)
        hid = (a * _sigmoid(a)) * b
        _store_row_tiles(y_ref, jnp.dot(hid.astype(BF16), wdb_ref[...], preferred_element_type=F32))

    @pl.when(jnp.logical_not(valid))
    def _():
        y_ref[...] = jnp.zeros_like(y_ref)


def _experts(blk_e, n_valid, x_rows, w_gate, w_up, w_down):
    d = D_MODEL
    rows = x_rows.shape[0] // ROW_SLABS
    tb = ROW_BLOCK
    n_blocks = rows // tb
    idx = jnp.arange(n_blocks, dtype=jnp.int32)
    change = (idx == 0) | (blk_e != jnp.roll(blk_e, 1))
    seg = jnp.cumsum(change.astype(jnp.int32)) - 1
    later_start = (idx[None, :] > idx[:, None]) & change[None, :] & (idx[None, :] < n_valid[0])
    none = jnp.int32(N_EXPERTS)
    nxt = jnp.min(jnp.where(later_start, blk_e[None, :], none), axis=1)
    nxt = jnp.where(nxt == none, -1, nxt)
    row_spec = pl.BlockSpec((tb * ROW_SLABS, LANES), lambda i, *_: (i, 0))
    grid_spec = pltpu.PrefetchScalarGridSpec(
        num_scalar_prefetch=4,
        grid=(n_blocks,),
        in_specs=[row_spec, pl.BlockSpec(memory_space=pl.ANY), pl.BlockSpec(memory_space=pl.ANY),
                  pl.BlockSpec(memory_space=pl.ANY)],
        out_specs=row_spec,
        scratch_shapes=[pltpu.VMEM((2, d, D_EXPERT), F32), pltpu.VMEM((2, d, D_EXPERT), F32),
                        pltpu.VMEM((2, D_EXPERT, d), F32),
                        pltpu.VMEM((d, D_EXPERT), BF16), pltpu.VMEM((d, D_EXPERT), BF16),
                        pltpu.VMEM((D_EXPERT, d), BF16), pltpu.SemaphoreType.DMA((2,))],
    )
    return pl.pallas_call(
        _expert_kernel,
        grid_spec=grid_spec,
        out_shape=jax.ShapeDtypeStruct(x_rows.shape, x_rows.dtype),
        compiler_params=_params(("arbitrary",)),
        name="experts",
    )(blk_e, seg, nxt.astype(jnp.int32), n_valid, x_rows, w_gate, w_up, w_down)


def kernel(x, c, w_ada, b_ada, norm_mix_g, w_in, b_forget, w_out_fox, lambda_re, lambda_im, log_dt,
           ssm_b_re, ssm_b_im, ssm_c_re, ssm_c_im, d_skip, w_glu, w_out_ssm, w_o, norm_ffn_g,
           w_router_group, b_router_group, w_router_expert, b_router_expert, w_gate_e, w_up_e,
           w_down_e, final_g):
    bsz, seq, d = x.shape
    n = bsz * seq
    assert w_ada.shape[0] == 1, "the final RMSNorm is fused into the (single) layer's combine kernel"
    xc = x.reshape(n, d)
    for l in range(1):
        mod3 = _mod(c, w_ada[l], b_ada[l]).reshape(bsz, N_MOD, d)

        wi = w_in[l]
        s_q, s_k, s_v, s_f, s_u, s_ga = 512, 1024, 1536, 1544, 2056, 3080
        scale = FOX_HEAD_DIM ** -0.5
        w_all = jnp.concatenate(
            [wi[:, :s_q] * scale, wi[:, s_q:s_k], wi[:, s_f:s_u], wi[:, s_u:s_ga],
             wi[:, s_ga:], jnp.pad(wi[:, s_v:s_f], ((0, 0), (0, LANES - FOX_HEADS)))],
            axis=1).astype(BF16)
        w_vt = wi[:, s_k:s_v].T.astype(BF16)
        bf_pad = jnp.pad(b_forget[l], (0, LANES - FOX_HEADS)).reshape(1, LANES)
        q, k, v_t, u, u_flat, sga, sgb = _inproj(xc, mod3, norm_mix_g[l].reshape(1, d), w_all, w_vt,
                                                 bf_pad, seq)

        o_fox = _attention(q, k, v_t, bsz, seq)

        toep, b_state, b_swap, c_pow, a_step = _ssm_prep(
            lambda_re[l], lambda_im[l], log_dt[l], ssm_b_re[l], ssm_b_im[l], ssm_c_re[l], ssm_c_im[l])
        y_flat = _ssm(u_flat, toep, b_state, b_swap, c_pow, a_step, bsz)

        w_r = jnp.pad(jnp.concatenate([w_router_group[l], w_router_expert[l]], axis=1),
                      ((0, 0), (0, LANES - N_GROUPS - N_EXPERTS)))
        w_r1 = _top_bits(w_r)
        w_r2 = _top_bits(w_r - w_r1)
        w_r = jnp.concatenate([w_r1, w_r2, w_r1], axis=0).astype(BF16)
        b_r = jnp.pad(jnp.concatenate([b_router_group[l], b_router_expert[l]]),
                      (0, LANES - N_GROUPS - N_EXPERTS)).reshape(1, LANES)
        x1, h2, logits = _mix(xc, o_fox, y_flat, u, sga, sgb, mod3, d_skip[l].reshape(1, SSM_WIDTH),
                              w_glu[l].astype(BF16), w_out_fox[l].astype(BF16),
                              w_out_ssm[l].astype(BF16), w_o[l].astype(BF16),
                              norm_ffn_g[l].reshape(1, d), w_r, b_r, seq)

        idx, wts, cnt = _route(logits)
        counts = cnt[0, :N_EXPERTS].astype(jnp.int32)
        pcounts = ((counts + ROW_BLOCK - 1) // ROW_BLOCK) * ROW_BLOCK
        pends = jnp.cumsum(pcounts)
        pstarts = pends - pcounts
        er = idx[:, 0:4].T
        hit = er[0:2, None, :] == jnp.arange(N_EXPERTS, dtype=jnp.int32)[None, :, None]
        dest = jnp.sum(jnp.where(hit, pstarts[None, :, None], 0), axis=1) + er[2:4]
        rows = 2 * n + N_EXPERTS * ROW_BLOCK
        n_blocks = rows // ROW_BLOCK
        blk_start = jnp.arange(n_blocks, dtype=jnp.int32) * ROW_BLOCK
        blk_e = jnp.minimum(jnp.sum((pends[None, :] <= blk_start[:, None]).astype(jnp.int32), axis=1),
                            N_EXPERTS - 1)
        n_valid = (pends[-1:] // ROW_BLOCK).astype(jnp.int32)
        dest3 = (dest.astype(jnp.int32).reshape(2, n // MOVE_TILE, MOVE_TILE).transpose(1, 0, 2)
                 .reshape(n // MOVE_TILE, 1, 2 * MOVE_TILE))

        x_rows = _dispatch(dest3, h2, jnp.zeros((rows * ROW_SLABS, LANES), jnp.uint32))
        y_rows = _experts(blk_e, n_valid, x_rows, w_gate_e[l], w_up_e[l], w_down_e[l])
        xc = _combine(dest3, x1, wts, mod3, final_g.reshape(1, d), y_rows, seq)
    return xc.reshape(bsz, seq, d)
```

```python
import functools
import math

import jax
import jax.numpy as jnp
import numpy as np
from jax import lax
from jax.experimental import pallas as pl
from jax.experimental.pallas import tpu as pltpu

F32 = jnp.float32
BF16 = jnp.bfloat16

D_MODEL = 1024
N_MOD = 6
RMS_EPS = 1e-6
FOX_HEADS = 8
FOX_HEAD_DIM = 64
FOX_WIDTH = FOX_HEADS * FOX_HEAD_DIM
HEAD_PAIRS = FOX_HEADS // 2
SSM_WIDTH = 512
SSM_GROUP = 16
SSM_GROUPS = SSM_WIDTH // SSM_GROUP
SSM_STATE = 64
LAMBDA_RE_MAX = -1e-4
N_GROUPS = 4
EXPERTS_PER_GROUP = 8
N_EXPERTS = N_GROUPS * EXPERTS_PER_GROUP
D_EXPERT = 512

LANES = 128
SUBLANES = 8
VMEM_LIMIT = 56 * 1024 * 1024

SSM_CHUNK = 16
TOK_TILE = 512
MIX_TILE = 256
ATT_Q_TILE = 512
ATT_K_TILE = 256
ROW_BLOCK = 512
MOVE_TILE = 512
NEG_BIG = -1e30

HIGHEST = lax.Precision.HIGHEST


def _params(sem):
    return pltpu.CompilerParams(dimension_semantics=sem, vmem_limit_bytes=VMEM_LIMIT)


def _sigmoid(x):
    return 0.5 * jnp.tanh(0.5 * x) + 0.5


def _rms_modulate(x, gain, shift, scale):
    ms = jnp.mean(x * x, axis=-1, keepdims=True)
    return (x * lax.rsqrt(ms + RMS_EPS)) * gain * (1.0 + scale) + shift


def _mod_kernel(c_ref, w_ref, b_ref, o_ref):
    c = c_ref[...]
    ca = (c * jax.nn.sigmoid(c)).astype(BF16)
    o_ref[...] = jnp.dot(ca, w_ref[...].astype(BF16), preferred_element_type=F32) + b_ref[...]


def _mod(c, w_ada, b_ada):
    bsz, d = c.shape
    cols = w_ada.shape[1]
    tn = 1536
    return pl.pallas_call(
        _mod_kernel,
        grid=(cols // tn,),
        in_specs=[pl.BlockSpec((bsz, d), lambda j: (0, 0)),
                  pl.BlockSpec((d, tn), lambda j: (0, j)),
                  pl.BlockSpec((1, tn), lambda j: (0, j))],
        out_specs=pl.BlockSpec((bsz, tn), lambda j: (0, j)),
        out_shape=jax.ShapeDtypeStruct((bsz, cols), F32),
        compiler_params=_params(("arbitrary",)),
        name="mod",
    )(c, w_ada, b_ada.reshape(1, cols))


_C_Q, _C_K, _C_U, _C_GA, _C_GB, _C_F, _C_END = 0, 512, 1024, 1536, 2560, 3584, 3712


def _lane_block():
    return lax.broadcasted_iota(jnp.int32, (1, LANES), 1) // SSM_GROUP


def _to_group_major(tok_ref, flat_ref, rows):
    blk = _lane_block()
    for half in range(2):
        for j in range(SSM_WIDTH // LANES):
            w = []
            for s8 in range(8):
                v = tok_ref[j, pl.ds(8 * half + s8, rows, stride=SSM_CHUNK), :]
                w.append(pltpu.roll(v, s8 * SSM_GROUP, axis=1) if s8 else v)
            for p in range(8):
                acc = w[0]
                for s8 in range(1, 8):
                    acc = jnp.where(blk == (p + s8) % 8, w[s8], acc)
                flat_ref[8 * j + p, :, half * LANES:(half + 1) * LANES] = acc.astype(flat_ref.dtype)


def _to_token_major(flat_ref, tok_ref, rows):
    blk = _lane_block()
    for half in range(2):
        for j in range(SSM_WIDTH // LANES):
            ys = [flat_ref[8 * j + p, :, half * LANES:(half + 1) * LANES] for p in range(8)]
            for s8 in range(8):
                acc = ys[0]
                for p in range(1, 8):
                    acc = jnp.where(blk == (p + s8) % 8, ys[p], acc)
                if s8:
                    acc = pltpu.roll(acc, LANES - s8 * SSM_GROUP, axis=1)
                tok_ref[j, pl.ds(8 * half + s8, rows, stride=SSM_CHUNK), :] = acc


def _bias_lane_placement():
    pq = np.zeros((3 * LANES, LANES), np.float32)
    pk = np.zeros((3 * LANES, LANES), np.float32)
    bq = np.zeros((1, LANES), np.float32)
    bk = np.zeros((1, LANES), np.float32)
    for head in range(FOX_HEADS):
        base = head * 8
        for term in range(3):
            pq[term * LANES + head, base + term] = 1.0
            pk[term * LANES + head, base + 3 + term] = -1.0
            bq[0, base + 3 + term] = 1.0
            bk[0, base + term] = 1.0
    return pq, pk, bq, bk


def _top_bits(a):
    bits = lax.bitcast_convert_type(a, jnp.uint32) & jnp.uint32(0xFFFF0000)
    return lax.bitcast_convert_type(bits, F32)


def _inproj_kernel(tiles_per_batch, x_ref, mod_ref, g_ref, w_ref, wvt_ref, bf_ref, tri_ref,
                   pq_ref, pk_ref, bq_ref, bk_ref,
                   q_ref, k_ref, vt_ref, u_ref, uflat_ref, ga_ref, gb_ref, carry_ref, uslab_ref):
    i = pl.program_id(0)
    h = _rms_modulate(x_ref[...], g_ref[...], mod_ref[0:1, :], mod_ref[1:2, :])
    hb = h.astype(BF16)

    def proj(a, b):
        return jnp.dot(hb, w_ref[:, a:b], preferred_element_type=F32)

    q = proj(_C_Q, _C_K).astype(BF16)
    k = proj(_C_K, _C_U).astype(BF16)
    vt_ref[...] = lax.dot_general(wvt_ref[...], hb, (((1,), (1,)), ((), ())),
                                  preferred_element_type=F32).astype(BF16)
    u = proj(_C_U, _C_GA)
    u_ref[...] = u
    for j in range(SSM_WIDTH // LANES):
        uslab_ref[j] = u[:, j * LANES:(j + 1) * LANES]
    _to_group_major(uslab_ref, uflat_ref, u.shape[0] // SSM_CHUNK)
    ga_ref[...] = _sigmoid(proj(_C_GA, _C_GB)).astype(BF16)
    gb_ref[...] = _sigmoid(proj(_C_GB, _C_F)).astype(BF16)

    f = proj(_C_F, _C_END) + bf_ref[...]
    logf = jnp.minimum(f, 0.0) - jnp.log(1.0 + jnp.exp(-jnp.abs(f)))

    @pl.when(i % tiles_per_batch == 0)
    def _():
        carry_ref[...] = jnp.zeros_like(carry_ref)

    def split3(a):
        hi = _top_bits(a)
        r1 = a - hi
        mid = _top_bits(r1)
        return jnp.concatenate([hi, mid, _top_bits(r1 - mid)], axis=1).astype(BF16)

    part = jnp.dot(tri_ref[...], split3(logf), preferred_element_type=F32)
    cs = (part[:, :LANES] + part[:, LANES:2 * LANES] + part[:, 2 * LANES:]) + carry_ref[0:1, :]
    carry_ref[...] = jnp.broadcast_to(cs[-1:, :], carry_ref.shape)

    terms = split3(cs)
    bias_q = (jnp.dot(terms, pq_ref[...], preferred_element_type=F32) + bq_ref[...]).astype(BF16)
    bias_k = (jnp.dot(terms, pk_ref[...], preferred_element_type=F32) + bk_ref[...]).astype(BF16)
    for p in range(HEAD_PAIRS):
        lanes = slice(p * LANES, (p + 1) * LANES)
        q_ref[:, 2 * p * LANES:(2 * p + 1) * LANES] = q[:, lanes]
        q_ref[:, (2 * p + 1) * LANES:(2 * p + 2) * LANES] = bias_q
        k_ref[:, 2 * p * LANES:(2 * p + 1) * LANES] = k[:, lanes]
        k_ref[:, (2 * p + 1) * LANES:(2 * p + 2) * LANES] = bias_k


def _inproj(x2, mod3, gain, w_all, w_vt, bf_pad, seq):
    n, d = x2.shape
    tm = TOK_TILE
    tpb = seq // tm
    tri = jnp.tril(jnp.ones((tm, tm), BF16))
    pq, pk, bq, bk = _bias_lane_placement()
    tok = lambda w: pl.BlockSpec((tm, w), lambda i: (i, 0))
    const = lambda shape: pl.BlockSpec(shape, lambda i: (0,) * len(shape))
    qk_width = 2 * FOX_WIDTH
    return pl.pallas_call(
        functools.partial(_inproj_kernel, tpb),
        grid=(n // tm,),
        in_specs=[tok(d),
                  pl.BlockSpec((None, N_MOD, d), lambda i: (i // tpb, 0, 0)),
                  const((1, d)), const((d, _C_END)), const((FOX_WIDTH, d)), const((1, LANES)),
                  const((tm, tm)), const(pq.shape), const(pk.shape), const(bq.shape), const(bk.shape)],
        out_specs=[tok(qk_width), tok(qk_width), pl.BlockSpec((FOX_WIDTH, tm), lambda i: (0, i)),
                   tok(SSM_WIDTH),
                   pl.BlockSpec((SSM_GROUPS, tm // SSM_CHUNK, SSM_CHUNK * SSM_GROUP), lambda i: (0, i, 0)),
                   tok(d), tok(d)],
        out_shape=[jax.ShapeDtypeStruct((n, qk_width), BF16)] * 2
        + [jax.ShapeDtypeStruct((FOX_WIDTH, n), BF16)]
        + [jax.ShapeDtypeStruct((n, SSM_WIDTH), F32)]
        + [jax.ShapeDtypeStruct((SSM_GROUPS, n // SSM_CHUNK, SSM_CHUNK * SSM_GROUP), BF16)]
        + [jax.ShapeDtypeStruct((n, d), BF16)] * 2,
        scratch_shapes=[pltpu.VMEM((SUBLANES, LANES), F32),
                        pltpu.VMEM((SSM_WIDTH // LANES, tm, LANES), F32)],
        compiler_params=_params(("arbitrary",)),
        name="inproj",
    )(x2, mod3, gain, w_all, w_vt, bf_pad, tri, jnp.asarray(pq, BF16), jnp.asarray(pk, BF16),
      jnp.asarray(bq), jnp.asarray(bk))


def _attn_kernel(q_ref, k_ref, vt_ref, o_ref, m_ref, acc_ref, sa_ref, sb_ref):
    i = pl.program_id(2)
    tq, tk = ATT_Q_TILE, ATT_K_TILE
    q = q_ref[...]
    lane = lax.broadcasted_iota(jnp.int32, (1, 2 * LANES), 1)
    zq = jnp.zeros_like(q)
    half = FOX_HEAD_DIM
    bias0 = LANES + 16 * pl.program_id(1)
    own0 = (lane < half) | ((lane >= bias0) & (lane < bias0 + 8))
    own1 = ((lane >= half) & (lane < LANES)) | ((lane >= bias0 + 8) & (lane < bias0 + 16))
    q_both = jnp.concatenate([jnp.where(own0, q, zq), jnp.where(own1, q, zq)], axis=0)
    m_ref[...] = jnp.full(m_ref.shape, NEG_BIG, F32)
    acc_ref[...] = jnp.zeros(acc_ref.shape, F32)
    ones_rows = jnp.ones((2 * SUBLANES, tk), BF16)
    key_in_tile = lax.broadcasted_iota(jnp.int32, (tk, 2 * tq), 0)
    qry_pos = i * tq + (lax.broadcasted_iota(jnp.int32, (tk, 2 * tq), 1) & (tq - 1))

    def scores(j, s_ref):
        start = pl.multiple_of(j * tk, tk)
        s = lax.dot_general(k_ref[pl.ds(start, tk), :], q_both, (((1,), (1,)), ((), ())),
                            preferred_element_type=F32)
        s_ref[...] = jnp.where(key_in_tile + j * tk <= qry_pos, s, NEG_BIG)

    def accumulate(j, s_ref):
        start = pl.multiple_of(j * tk, tk)
        va = jnp.concatenate([vt_ref[:, pl.ds(start, tk)], ones_rows], axis=0)
        m_old = m_ref[...]
        m_new = jnp.maximum(m_old, jnp.max(s_ref[...], axis=0, keepdims=True))
        alpha = jnp.exp(m_old - m_new)
        p = jnp.exp(s_ref[...] - m_new).astype(BF16)
        acc_ref[...] = alpha * acc_ref[...] + jnp.dot(va, p, preferred_element_type=F32)
        m_ref[...] = m_new

    def body(jj, c):
        t0 = 2 * jj
        scores(t0 + 1, sb_ref)
        accumulate(t0, sa_ref)
        scores(t0 + 2, sa_ref)
        accumulate(t0 + 1, sb_ref)
        return c

    assert tq == 2 * tk
    scores(0, sa_ref)
    lax.fori_loop(0, i, body, 0)
    scores(2 * i + 1, sb_ref)
    accumulate(2 * i, sa_ref)
    accumulate(2 * i + 1, sb_ref)

    acc = acc_ref[...]
    o_t = jnp.concatenate([acc[0:half, 0:tq] / acc[LANES:LANES + 1, 0:tq],
                           acc[half:LANES, tq:2 * tq] / acc[LANES:LANES + 1, tq:2 * tq]], axis=0)
    o_ref[...] = o_t.T.astype(o_ref.dtype)


def _attention(q, k, v_t, bsz, seq):
    n = q.shape[0]
    t = ATT_Q_TILE
    nq = seq // t
    return pl.pallas_call(
        _attn_kernel,
        grid=(bsz, HEAD_PAIRS, nq),
        in_specs=[pl.BlockSpec((t, 2 * LANES), lambda b, p, i: (b * nq + i, p)),
                  pl.BlockSpec((seq, 2 * LANES), lambda b, p, i: (b, p)),
                  pl.BlockSpec((LANES, seq), lambda b, p, i: (p, b))],
        out_specs=pl.BlockSpec((t, LANES), lambda b, p, i: (b * nq + i, p)),
        out_shape=jax.ShapeDtypeStruct((n, FOX_WIDTH), BF16),
        scratch_shapes=[pltpu.VMEM((1, 2 * t), F32), pltpu.VMEM((LANES + 2 * SUBLANES, 2 * t), F32),
                        pltpu.VMEM((ATT_K_TILE, 2 * t), F32), pltpu.VMEM((ATT_K_TILE, 2 * t), F32)],
        compiler_params=_params(("arbitrary", "arbitrary", "arbitrary")),
        name="attn",
    )(q, k, v_t)


def _ssm_prep_kernel(lrow_ref, lcol_ref, ldt_ref, btr_ref, bti_ref, ctr_ref, cti_ref,
                     toep_ref, bst_ref, bsw_ref, cpw_ref, a_ref):
    p8 = pl.program_id(0) % 8
    t_len, grp = SSM_CHUNK, SSM_GROUP
    dt = jnp.exp(ldt_ref[...])
    lr, li = jnp.minimum(lrow_ref[0:1, :], LAMBDA_RE_MAX), lrow_ref[1:2, :]

    def powers(steps, re, im):
        mag = jnp.exp(steps * (re * dt))
        return mag * jnp.cos(steps * (im * dt)), mag * jnp.sin(steps * (im * dt))

    a_re, a_im = powers(1.0, lr, li)
    den = lr * lr + li * li
    nr = a_re - 1.0
    co_re = (nr * lr + a_im * li) / den
    co_im = (a_im * lr - nr * li) / den
    bbt_re = co_re * btr_ref[...] - co_im * bti_ref[...]
    bbt_im = co_re * bti_ref[...] + co_im * btr_ref[...]

    lag = (lax.broadcasted_iota(jnp.int32, (1, t_len * grp), 1) // grp).astype(F32)
    lcr, lci = jnp.minimum(lcol_ref[:, 0:1], LAMBDA_RE_MAX), lcol_ref[:, 1:2]

    def c_times_power(steps):
        p_re, p_im = powers(steps, lcr, lci)
        return (ctr_ref[...] * p_re - cti_ref[...] * p_im, ctr_ref[...] * p_im + cti_ref[...] * p_re)

    wt_re, wt_im = c_times_power(lag)
    kern = (jnp.dot(bbt_re, wt_re, precision=HIGHEST, preferred_element_type=F32)
            - jnp.dot(bbt_im, wt_im, precision=HIGHEST, preferred_element_type=F32))

    lane = lax.broadcasted_iota(jnp.int32, (1, LANES), 1)
    col_shift = p8 * grp

    def store_cols(ref, rows, lo_half, hi_half):
        ref[rows, 0:LANES] = pltpu.roll(lo_half, col_shift, axis=1).astype(ref.dtype)
        ref[rows, LANES:2 * LANES] = pltpu.roll(hi_half, col_shift, axis=1).astype(ref.dtype)

    def slot_rows(s):
        half, s8 = divmod(s, 8)
        return pl.ds(pl.multiple_of((8 * half + (s8 + p8) % 8) * grp, grp), grp)

    back = (t_len - 1 - lax.broadcasted_iota(jnp.int32, (t_len, 1), 0)).astype(F32)
    e_re, e_im = powers(back, lr, li)
    zero = jnp.zeros((grp, LANES), F32)
    k_lo, k_hi = kern[:, 0:LANES], kern[:, LANES:2 * LANES]
    for s in range(t_len):
        half, s8 = divmod(s, 8)
        keep = lane >= s8 * grp
        r_lo = pltpu.roll(k_lo, s8 * grp, axis=1) if s8 else k_lo
        r_hi = pltpu.roll(k_hi, s8 * grp, axis=1) if s8 else k_hi
        if half == 0:
            lo, hi = jnp.where(keep, r_lo, 0.0), jnp.where(keep, r_hi, r_lo)
        else:
            lo, hi = zero, jnp.where(keep, r_lo, 0.0)
        store_cols(toep_ref, slot_rows(s), lo, hi)
        es_re, es_im = e_re[s:s + 1, :], e_im[s:s + 1, :]
        bs_re = es_re * bbt_re - es_im * bbt_im
        bs_im = es_re * bbt_im + es_im * bbt_re
        bst_ref[slot_rows(s), :] = jnp.concatenate([bs_re, bs_im], axis=1).astype(bst_ref.dtype)
        bsw_ref[slot_rows(s), :] = jnp.concatenate([bs_im, bs_re], axis=1).astype(bsw_ref.dtype)

    w1_re, w1_im = c_times_power(lag + 1.0)
    store_cols(cpw_ref, pl.ds(0, SSM_STATE), w1_re[:, 0:LANES], w1_re[:, LANES:2 * LANES])
    store_cols(cpw_ref, pl.ds(SSM_STATE, SSM_STATE), -w1_im[:, 0:LANES], -w1_im[:, LANES:2 * LANES])
    s_re, s_im = powers(float(t_len), lr, li)
    a_ref[0:1, :] = jnp.concatenate([s_re, s_re], axis=1)
    a_ref[1:2, :] = jnp.concatenate([-s_im, s_im], axis=1)


def _ssm_prep(lambda_re, lambda_im, log_dt, b_re, b_im, c_re, c_im):
    width = SSM_CHUNK * SSM_GROUP
    lam_row = jnp.stack([lambda_re, lambda_im], axis=1)
    tiled = lambda c: jnp.tile(c.transpose(0, 2, 1), (1, 1, SSM_CHUNK))
    per = lambda a, b: pl.BlockSpec((None, a, b), lambda g: (g, 0, 0))
    return pl.pallas_call(
        _ssm_prep_kernel,
        grid=(SSM_GROUPS,),
        in_specs=[per(2, SSM_STATE), per(SSM_STATE, 2), per(1, 1), per(SSM_GROUP, SSM_STATE),
                  per(SSM_GROUP, SSM_STATE), per(SSM_STATE, width), per(SSM_STATE, width)],
        out_specs=[per(width, width), per(width, 2 * SSM_STATE), per(width, 2 * SSM_STATE),
                   per(2 * SSM_STATE, width), per(2, 2 * SSM_STATE)],
        out_shape=[jax.ShapeDtypeStruct((SSM_GROUPS, width, width), BF16),
                   jax.ShapeDtypeStruct((SSM_GROUPS, width, 2 * SSM_STATE), BF16),
                   jax.ShapeDtypeStruct((SSM_GROUPS, width, 2 * SSM_STATE), BF16),
                   jax.ShapeDtypeStruct((SSM_GROUPS, 2 * SSM_STATE, width), BF16),
                   jax.ShapeDtypeStruct((SSM_GROUPS, 2, 2 * SSM_STATE), F32)],
        compiler_params=_params(("arbitrary",)),
        name="ssm_prep",
    )(lam_row, lam_row.transpose(0, 2, 1), log_dt.reshape(SSM_GROUPS, 1, 1),
      b_re.transpose(0, 2, 1), b_im.transpose(0, 2, 1), tiled(c_re), tiled(c_im))


SSM_GROUPS_PER_STEP = 4


def _ssm_kernel(n_chunks, bsz, u_ref, toep_ref, bst_ref, bsw_ref, cpw_ref, a_ref, y_ref,
                contrib_ref, cswap_ref, xprev_ref):
    groups = u_ref.shape[0]
    for k in range(groups):
        u = u_ref[k]
        contrib_ref[k] = jnp.dot(u, bst_ref[k], preferred_element_type=F32)
        cswap_ref[k] = jnp.dot(u, bsw_ref[k], preferred_element_type=F32)
    a1 = [a_ref[k, 0:1, :] for k in range(groups)]
    a2 = [a_ref[k, 1:2, :] for k in range(groups)]

    def step(n, carry):
        rows = pl.ds(n, bsz, stride=n_chunks)
        new = []
        for k in range(groups):
            x, xs = carry[2 * k], carry[2 * k + 1]
            xprev_ref[k, rows, :] = x
            new.append(a1[k] * x + a2[k] * xs + contrib_ref[k, rows, :])
            new.append(a1[k] * xs - a2[k] * x + cswap_ref[k, rows, :])
        return tuple(new)

    zero = jnp.zeros((bsz, 2 * SSM_STATE), F32)
    lax.fori_loop(0, n_chunks, step, (zero,) * (2 * groups), unroll=2)
    for k in range(groups):
        y_ref[k] = (jnp.dot(u_ref[k], toep_ref[k], preferred_element_type=F32)
                    + jnp.dot(xprev_ref[k].astype(BF16), cpw_ref[k], preferred_element_type=F32))


def _ssm(u_flat, toep, b_state, b_swap, c_pow, a_step, bsz):
    g, rows, w = u_flat.shape
    gb = SSM_GROUPS_PER_STEP
    per = lambda a, b: pl.BlockSpec((gb, a, b), lambda i: (i, 0, 0))
    state = pltpu.VMEM((gb, rows, 2 * SSM_STATE), F32)
    return pl.pallas_call(
        functools.partial(_ssm_kernel, rows // bsz, bsz),
        grid=(g // gb,),
        in_specs=[per(rows, w), per(w, w), per(w, 2 * SSM_STATE), per(w, 2 * SSM_STATE),
                  per(2 * SSM_STATE, w), per(2, 2 * SSM_STATE)],
        out_specs=per(rows, w),
        out_shape=jax.ShapeDtypeStruct((g, rows, w), F32),
        scratch_shapes=[state, state, state],
        compiler_params=_params(("arbitrary",)),
        name="ssm",
    )(u_flat, toep, b_state, b_swap, c_pow, a_step)


ROW_SLABS = D_MODEL // LANES // 2
_HIGH_HALF = 0xFFFF0000


def _store_row_tiles(ref, value):
    rows = value.shape[0]
    bits = lax.bitcast_convert_type(value.astype(BF16).astype(F32), jnp.uint32)
    for j in range(ROW_SLABS):
        low = bits[:, j * LANES:(j + 1) * LANES] >> 16
        high = bits[:, (j + ROW_SLABS) * LANES:(j + ROW_SLABS + 1) * LANES] & jnp.uint32(_HIGH_HALF)
        ref[pl.ds(j, rows, stride=ROW_SLABS), :] = high | low


def _load_row_tiles(ref, rows):
    words = [ref[pl.ds(j, rows, stride=ROW_SLABS), :] for j in range(ROW_SLABS)]
    low = [lax.bitcast_convert_type(w << 16, F32) for w in words]
    high = [lax.bitcast_convert_type(w & jnp.uint32(_HIGH_HALF), F32) for w in words]
    return jnp.concatenate(low + high, axis=1)


def _row_tile_copy(src_ref, src_row, dst_ref, dst_row, sem):
    src = src_ref.at[pl.ds(pl.multiple_of(src_row * ROW_SLABS, ROW_SLABS), ROW_SLABS), :]
    dst = dst_ref.at[pl.ds(pl.multiple_of(dst_row * ROW_SLABS, ROW_SLABS), ROW_SLABS), :]
    return pltpu.make_async_copy(src, dst, sem)


def _mix_kernel(x_ref, of_ref, yf_ref, u_ref, ga_ref, gb_ref, mod_ref, dsk_ref, wglu_ref, wfox_ref,
                wssm_ref, wo_ref, g2_ref, wr_ref, br_ref, x1_ref, h2_ref, lg_ref, ytok_ref):
    _to_token_major(yf_ref, ytok_ref, yf_ref.shape[1])
    y_ssm = jnp.concatenate([ytok_ref[j] for j in range(SSM_WIDTH // LANES)], axis=1)
    y = y_ssm + dsk_ref[...] * u_ref[...]
    y = 0.5 * y * (1.0 + jnp.tanh(math.sqrt(2.0 / math.pi) * (y + 0.044715 * (y * y * y))))
    gl = jnp.dot(y.astype(BF16), wglu_ref[...], preferred_element_type=F32)
    o_ssm = gl[:, :SSM_WIDTH] * _sigmoid(gl[:, SSM_WIDTH:])
    merged = (ga_ref[...].astype(F32) * jnp.dot(of_ref[...], wfox_ref[...], preferred_element_type=F32)
              + gb_ref[...].astype(F32) * jnp.dot(o_ssm.astype(BF16), wssm_ref[...],
                                                  preferred_element_type=F32))
    x1 = x_ref[...] + mod_ref[2:3, :] * jnp.dot(merged.astype(BF16), wo_ref[...],
                                                 preferred_element_type=F32)
    x1_ref[...] = x1
    h2 = _rms_modulate(x1, g2_ref[...], mod_ref[3:4, :], mod_ref[4:5, :])
    _store_row_tiles(h2_ref, h2)
    a1 = _top_bits(h2)
    a2 = _top_bits(h2 - a1)
    lhs = jnp.concatenate([a1, a1, a2], axis=1).astype(BF16)
    lg_ref[...] = jnp.dot(lhs, wr_ref[...], preferred_element_type=F32) + br_ref[...]


def _mix(x2, o_fox, y_flat, u, sga, sgb, mod3, d_skip, w_glu, w_fox, w_ssm, w_o, g2, w_r, b_r, seq):
    n, d = x2.shape
    tm = MIX_TILE
    tpb = seq // tm
    tok = lambda w: pl.BlockSpec((tm, w), lambda i: (i, 0))
    const = lambda a: pl.BlockSpec(a.shape, lambda i: (0,) * a.ndim)
    flat = pl.BlockSpec((SSM_GROUPS, tm // SSM_CHUNK, SSM_CHUNK * SSM_GROUP), lambda i: (0, i, 0))
    return pl.pallas_call(
        _mix_kernel,
        grid=(n // tm,),
        in_specs=[tok(d), tok(FOX_WIDTH), flat, tok(SSM_WIDTH), tok(d), tok(d),
                  pl.BlockSpec((None, N_MOD, d), lambda i: (i // tpb, 0, 0)),
                  const(d_skip), const(w_glu), const(w_fox), const(w_ssm), const(w_o), const(g2),
                  const(w_r), const(b_r)],
        out_specs=[tok(d), pl.BlockSpec((tm * ROW_SLABS, LANES), lambda i: (i, 0)), tok(LANES)],
        out_shape=[jax.ShapeDtypeStruct((n, d), F32), jax.ShapeDtypeStruct((n * ROW_SLABS, LANES), jnp.uint32),
                   jax.ShapeDtypeStruct((n, LANES), F32)],
        scratch_shapes=[pltpu.VMEM((SSM_WIDTH // LANES, tm, LANES), F32)],
        compiler_params=_params(("arbitrary",)),
        name="mix",
    )(x2, o_fox, y_flat, u, sga, sgb, mod3, d_skip, w_glu, w_fox, w_ssm, w_o, g2, w_r, b_r)


def _route_kernel(lg_ref, tri_ref, idx_ref, wt_ref, cnt_ref, carry_ref):
    i = pl.program_id(0)

    @pl.when(i == 0)
    def _():
        carry_ref[...] = jnp.zeros_like(carry_ref)

    lg = lg_ref[...]
    tm = lg.shape[0]
    lane = lax.broadcasted_iota(jnp.int32, (tm, LANES), 1)
    neg = jnp.full_like(lg, -jnp.inf)

    def first_argmax(vals):
        mx = jnp.max(vals, axis=1, keepdims=True)
        ix = jnp.min(jnp.where(vals == mx, lane, LANES), axis=1, keepdims=True)
        return mx, ix

    is_group = lane < N_GROUPS
    g_max, gi = first_argmax(jnp.where(is_group, lg, neg))
    g_sum = jnp.sum(jnp.where(is_group, jnp.exp(lg - g_max), 0.0), axis=1, keepdims=True)
    p_group = 1.0 / g_sum
    lo = N_GROUPS + EXPERTS_PER_GROUP * gi
    in_group = (lane >= lo) & (lane < lo + EXPERTS_PER_GROUP)
    cand = jnp.where(in_group, lg, neg)
    v1, i1 = first_argmax(cand)
    v2, i2 = first_argmax(jnp.where(lane == i1, neg, cand))
    tt = jnp.exp(v2 - v1)
    w1 = p_group / (1.0 + tt)
    w2 = p_group * tt / (1.0 + tt)
    e1 = i1 - N_GROUPS
    e2 = i2 - N_GROUPS
    sel1 = lane == e1
    sel2 = lane == e2
    onehot = (sel1 | sel2).astype(F32)
    before = jnp.dot(tri_ref[...], onehot.astype(BF16), preferred_element_type=F32) + carry_ref[0:1, :]
    r1 = jnp.sum(jnp.where(sel1, before, 0.0), axis=1, keepdims=True).astype(jnp.int32)
    r2 = jnp.sum(jnp.where(sel2, before, 0.0), axis=1, keepdims=True).astype(jnp.int32)
    total = before[-1:, :] + onehot[-1:, :]
    carry_ref[...] = jnp.broadcast_to(total, carry_ref.shape)
    cnt_ref[...] = jnp.broadcast_to(total, cnt_ref.shape)
    idx_ref[...] = jnp.where(lane == 0, e1, jnp.where(lane == 1, e2, jnp.where(lane == 2, r1, r2)))
    wt_ref[...] = jnp.where(lane == 0, w1, w2)


def _route(logits):
    n = logits.shape[0]
    tm = TOK_TILE
    tri = jnp.tril(jnp.ones((tm, tm), BF16), k=-1)
    tok = pl.BlockSpec((tm, LANES), lambda i: (i, 0))
    return pl.pallas_call(
        _route_kernel,
        grid=(n // tm,),
        in_specs=[tok, pl.BlockSpec((tm, tm), lambda i: (0, 0))],
        out_specs=[tok, tok, pl.BlockSpec((SUBLANES, LANES), lambda i: (0, 0))],
        out_shape=[jax.ShapeDtypeStruct((n, LANES), jnp.int32), jax.ShapeDtypeStruct((n, LANES), F32),
                   jax.ShapeDtypeStruct((SUBLANES, LANES), F32)],
        scratch_shapes=[pltpu.VMEM((SUBLANES, LANES), F32)],
        compiler_params=_params(("arbitrary",)),
        name="route",
    )(logits, tri)


ISSUE_UNROLL = 8


def _dispatch_kernel(dest_ref, h_ref, rows_in_ref, rows_ref, sem):
    del rows_in_ref
    tm = h_ref.shape[0] // ROW_SLABS

    def issue(g, c):
        for j in range(ISSUE_UNROLL):
            t = g * ISSUE_UNROLL + j
            _row_tile_copy(h_ref, t, rows_ref, dest_ref[0, 0, t], sem).start(priority=0)
            _row_tile_copy(h_ref, t, rows_ref, dest_ref[0, 0, tm + t], sem).start(priority=1)
        return c

    lax.fori_loop(0, tm // ISSUE_UNROLL, issue, 0)
    for _ in range(2):
        pltpu.make_async_copy(h_ref, rows_ref.at[pl.ds(0, tm * ROW_SLABS), :], sem).wait()


def _dispatch(dest3, h2_tiles, rows_zero):
    tm = MOVE_TILE
    n = h2_tiles.shape[0] // ROW_SLABS
    return pl.pallas_call(
        _dispatch_kernel,
        grid=(n // tm,),
        in_specs=[pl.BlockSpec((1, 1, 2 * tm), lambda i: (i, 0, 0), memory_space=pltpu.SMEM),
                  pl.BlockSpec((tm * ROW_SLABS, LANES), lambda i: (i, 0)),
                  pl.BlockSpec(memory_space=pl.ANY)],
        out_specs=pl.BlockSpec(memory_space=pl.ANY),
        out_shape=jax.ShapeDtypeStruct(rows_zero.shape, rows_zero.dtype),
        scratch_shapes=[pltpu.SemaphoreType.DMA(())],
        input_output_aliases={2: 0},
        compiler_params=_params(("arbitrary",)),
        name="dispatch",
    )(dest3, h2_tiles, rows_zero)


def _combine_kernel(n_steps, dest_ref, dnext_ref, x1_ref, wt_ref, mod_ref, gf_ref, yr_ref, o_ref,
                    buf_ref, sem):
    i = pl.program_id(0)
    tm = x1_ref.shape[0]

    def gather(idx_ref, which):
        def issue(g, c):
            for j in range(ISSUE_UNROLL):
                t = g * ISSUE_UNROLL + j
                _row_tile_copy(yr_ref, idx_ref[0, 0, t], buf_ref.at[which, 0], t,
                               sem.at[which]).start(priority=0)
                _row_tile_copy(yr_ref, idx_ref[0, 0, tm + t], buf_ref.at[which, 1], t,
                               sem.at[which]).start(priority=1)
            return c

        lax.fori_loop(0, tm // ISSUE_UNROLL, issue, 0)

    cur = i % 2

    @pl.when(i == 0)
    def _():
        gather(dest_ref, 0)

    @pl.when(i + 1 < n_steps)
    def _():
        gather(dnext_ref, 1 - cur)

    for slot in range(2):
        pltpu.make_async_copy(yr_ref.at[pl.ds(0, tm * ROW_SLABS), :], buf_ref.at[cur, slot],
                              sem.at[cur]).wait()
    wt = wt_ref[...]
    moe = (wt[:, 0:1] * _load_row_tiles(buf_ref.at[cur, 0], tm)
           + wt[:, 1:2] * _load_row_tiles(buf_ref.at[cur, 1], tm))
    x = x1_ref[...] + mod_ref[5:6, :] * moe
    ms = jnp.mean(x * x, axis=-1, keepdims=True)
    o_ref[...] = (x * lax.rsqrt(ms + RMS_EPS)) * gf_ref[...]


def _combine(dest3, x1, wts, mod3, final_g, y_rows, seq):
    n, d = x1.shape
    tm = MOVE_TILE
    tpb = seq // tm
    n_steps = n // tm
    idx_spec = lambda f: pl.BlockSpec((1, 1, 2 * tm), f, memory_space=pltpu.SMEM)
    return pl.pallas_call(
        functools.partial(_combine_kernel, n_steps),
        grid=(n_steps,),
        in_specs=[idx_spec(lambda i: (i, 0, 0)),
                  idx_spec(lambda i: (jnp.minimum(i + 1, n_steps - 1), 0, 0)),
                  pl.BlockSpec((tm, d), lambda i: (i, 0)),
                  pl.BlockSpec((tm, LANES), lambda i: (i, 0)),
                  pl.BlockSpec((None, N_MOD, d), lambda i: (i // tpb, 0, 0)),
                  pl.BlockSpec((1, d), lambda i: (0, 0)),
                  pl.BlockSpec(memory_space=pl.ANY)],
        out_specs=pl.BlockSpec((tm, d), lambda i: (i, 0)),
        out_shape=jax.ShapeDtypeStruct((n, d), F32),
        scratch_shapes=[pltpu.VMEM((2, 2, tm * ROW_SLABS, LANES), jnp.uint32),
                        pltpu.SemaphoreType.DMA((2,))],
        compiler_params=_params(("arbitrary",)),
        name="combine",
    )(dest3, dest3, x1, wts, mod3, final_g, y_rows)


def _expert_kernel(be_ref, seg_ref, nxt_ref, nv_ref, x_ref, wg_hbm, wu_hbm, wd_hbm, y_ref,
                   wg_buf, wu_buf, wd_buf, wgb_ref, wub_ref, wdb_ref, sem):
    i = pl.program_id(0)
    valid = i < nv_ref[0]
    first = (i == 0) | (be_ref[i] != be_ref[jnp.maximum(i - 1, 0)])
    slot = seg_ref[i] % 2

    def weight_copies(e, s):
        return [pltpu.make_async_copy(hbm.at[e], buf.at[s], sem.at[s])
                for hbm, buf in ((wg_hbm, wg_buf), (wu_hbm, wu_buf), (wd_hbm, wd_buf))]

    @pl.when(valid & (i == 0))
    def _():
        for c in weight_copies(be_ref[0], 0):
            c.start()

    @pl.when(valid & first)
    def _():
        for c in weight_copies(be_ref[i], slot):
            c.wait()

        @pl.when(nxt_ref[i] >= 0)
        def _():
            for c in weight_copies(nxt_ref[i], 1 - slot):
                c.start()

        wgb_ref[...] = wg_buf[slot].astype(BF16)
        wub_ref[...] = wu_buf[slot].astype(BF16)
        wdb_ref[...] = wd_buf[slot].astype(BF16)

    @pl.when(valid)
    def _():
        xb = _load_row_tiles(x_ref, x_ref.shape[0] // ROW_SLABS).astype(BF16)
        a = jnp.dot(xb, wgb_ref[...], preferred_element_type=F32)
        b = jnp.dot(xb, wub_ref[...], preferred_element_type=F32)
        hid = (a * _sigmoid(a)) * b
        _store_row_tiles(y_ref, jnp.dot(hid.astype(BF16), wdb_ref[...], preferred_element_type=F32))

    @pl.when(jnp.logical_not(valid))
    def _():
        y_ref[...] = jnp.zeros_like(y_ref)


def _experts(blk_e, n_valid, x_rows, w_gate, w_up, w_down):
    d = D_MODEL
    rows = x_rows.shape[0] // ROW_SLABS
    tb = ROW_BLOCK
    n_blocks = rows // tb
    idx = jnp.arange(n_blocks, dtype=jnp.int32)
    change = (idx == 0) | (blk_e != jnp.roll(blk_e, 1))
    seg = jnp.cumsum(change.astype(jnp.int32)) - 1
    later_start = (idx[None, :] > idx[:, None]) & change[None, :] & (idx[None, :] < n_valid[0])
    none = jnp.int32(N_EXPERTS)
    nxt = jnp.min(jnp.where(later_start, blk_e[None, :], none), axis=1)
    nxt = jnp.where(nxt == none, -1, nxt)
    row_spec = pl.BlockSpec((tb * ROW_SLABS, LANES), lambda i, *_: (i, 0))
    grid_spec = pltpu.PrefetchScalarGridSpec(
        num_scalar_prefetch=4,
        grid=(n_blocks,),
        in_specs=[row_spec, pl.BlockSpec(memory_space=pl.ANY), pl.BlockSpec(memory_space=pl.ANY),
                  pl.BlockSpec(memory_space=pl.ANY)],
        out_specs=row_spec,
        scratch_shapes=[pltpu.VMEM((2, d, D_EXPERT), F32), pltpu.VMEM((2, d, D_EXPERT), F32),
                        pltpu.VMEM((2, D_EXPERT, d), F32),
                        pltpu.VMEM((d, D_EXPERT), BF16), pltpu.VMEM((d, D_EXPERT), BF16),
                        pltpu.VMEM((D_EXPERT, d), BF16), pltpu.SemaphoreType.DMA((2,))],
    )
    return pl.pallas_call(
        _expert_kernel,
        grid_spec=grid_spec,
        out_shape=jax.ShapeDtypeStruct(x_rows.shape, x_rows.dtype),
        compiler_params=_params(("arbitrary",)),
        name="experts",
    )(blk_e, seg, nxt.astype(jnp.int32), n_valid, x_rows, w_gate, w_up, w_down)


def kernel(x, c, w_ada, b_ada, norm_mix_g, w_in, b_forget, w_out_fox, lambda_re, lambda_im, log_dt,
           ssm_b_re, ssm_b_im, ssm_c_re, ssm_c_im, d_skip, w_glu, w_out_ssm, w_o, norm_ffn_g,
           w_router_group, b_router_group, w_router_expert, b_router_expert, w_gate_e, w_up_e,
           w_down_e, final_g):
    bsz, seq, d = x.shape
    n = bsz * seq
    assert w_ada.shape[0] == 1, "the final RMSNorm is fused into the (single) layer's combine kernel"
    xc = x.reshape(n, d)
    for l in range(1):
        mod3 = _mod(c, w_ada[l], b_ada[l]).reshape(bsz, N_MOD, d)

        wi = w_in[l]
        s_q, s_k, s_v, s_f, s_u, s_ga = 512, 1024, 1536, 1544, 2056, 3080
        scale = FOX_HEAD_DIM ** -0.5
        w_all = jnp.concatenate(
            [wi[:, :s_q] * scale, wi[:, s_q:s_k], wi[:, s_f:s_u], wi[:, s_u:s_ga],
             wi[:, s_ga:], jnp.pad(wi[:, s_v:s_f], ((0, 0), (0, LANES - FOX_HEADS)))],
            axis=1).astype(BF16)
        w_vt = wi[:, s_k:s_v].T.astype(BF16)
        bf_pad = jnp.pad(b_forget[l], (0, LANES - FOX_HEADS)).reshape(1, LANES)
        q, k, v_t, u, u_flat, sga, sgb = _inproj(xc, mod3, norm_mix_g[l].reshape(1, d), w_all, w_vt,
                                                 bf_pad, seq)

        o_fox = _attention(q, k, v_t, bsz, seq)

        toep, b_state, b_swap, c_pow, a_step = _ssm_prep(
            lambda_re[l], lambda_im[l], log_dt[l], ssm_b_re[l], ssm_b_im[l], ssm_c_re[l], ssm_c_im[l])
        y_flat = _ssm(u_flat, toep, b_state, b_swap, c_pow, a_step, bsz)

        w_r = jnp.pad(jnp.concatenate([w_router_group[l], w_router_expert[l]], axis=1),
                      ((0, 0), (0, LANES - N_GROUPS - N_EXPERTS)))
        w_r1 = _top_bits(w_r)
        w_r2 = _top_bits(w_r - w_r1)
        w_r = jnp.concatenate([w_r1, w_r2, w_r1], axis=0).astype(BF16)
        b_r = jnp.pad(jnp.concatenate([b_router_group[l], b_router_expert[l]]),
                      (0, LANES - N_GROUPS - N_EXPERTS)).reshape(1, LANES)
        x1, h2, logits = _mix(xc, o_fox, y_flat, u, sga, sgb, mod3, d_skip[l].reshape(1, SSM_WIDTH),
                              w_glu[l].astype(BF16), w_out_fox[l].astype(BF16),
                              w_out_ssm[l].astype(BF16), w_o[l].astype(BF16),
                              norm_ffn_g[l].reshape(1, d), w_r, b_r, seq)

        idx, wts, cnt = _route(logits)
        counts = cnt[0, :N_EXPERTS].astype(jnp.int32)
        pcounts = ((counts + ROW_BLOCK - 1) // ROW_BLOCK) * ROW_BLOCK
        pends = jnp.cumsum(pcounts)
        pstarts = pends - pcounts
        er = idx[:, 0:4].T
        hit = er[0:2, None, :] == jnp.arange(N_EXPERTS, dtype=jnp.int32)[None, :, None]
        dest = jnp.sum(jnp.where(hit, pstarts[None, :, None], 0), axis=1) + er[2:4]
        rows = 2 * n + N_EXPERTS * ROW_BLOCK
        n_blocks = rows // ROW_BLOCK
        blk_start = jnp.arange(n_blocks, dtype=jnp.int32) * ROW_BLOCK
        blk_e = jnp.minimum(jnp.sum((pends[None, :] <= blk_start[:, None]).astype(jnp.int32), axis=1),
                            N_EXPERTS - 1)
        n_valid = (pends[-1:] // ROW_BLOCK).astype(jnp.int32)
        dest3 = (dest.astype(jnp.int32).reshape(2, n // MOVE_TILE, MOVE_TILE).transpose(1, 0, 2)
                 .reshape(n // MOVE_TILE, 1, 2 * MOVE_TILE))

        x_rows = _dispatch(dest3, h2, jnp.zeros((rows * ROW_SLABS, LANES), jnp.uint32))
        y_rows = _experts(blk_e, n_valid, x_rows, w_gate_e[l], w_up_e[l], w_down_e[l])
        xc = _combine(dest3, x1, wts, mod3, final_g.reshape(1, d), y_rows, seq)
    return xc.reshape(bsz, seq, d)
```

```python
import functools
import math

import jax
import jax.numpy as jnp
import numpy as np
from jax import lax
from jax.experimental import pallas as pl
from jax.experimental.pallas import tpu as pltpu

F32 = jnp.float32
BF16 = jnp.bfloat16

D_MODEL = 1024
N_MOD = 6
RMS_EPS = 1e-6
FOX_HEADS = 8
FOX_HEAD_DIM = 64
FOX_WIDTH = FOX_HEADS * FOX_HEAD_DIM
HEAD_PAIRS = FOX_HEADS // 2
SSM_WIDTH = 512
SSM_GROUP = 16
SSM_GROUPS = SSM_WIDTH // SSM_GROUP
SSM_STATE = 64
LAMBDA_RE_MAX = -1e-4
N_GROUPS = 4
EXPERTS_PER_GROUP = 8
N_EXPERTS = N_GROUPS * EXPERTS_PER_GROUP
D_EXPERT = 512

LANES = 128
SUBLANES = 8
VMEM_LIMIT = 56 * 1024 * 1024

SSM_CHUNK = 16
TOK_TILE = 512
MIX_TILE = 256
ATT_Q_TILE = 512
ATT_K_TILE = 256
ROW_BLOCK = 512
MOVE_TILE = 512
NEG_BIG = -1e30

HIGHEST = lax.Precision.HIGHEST


def _params(sem):
    return pltpu.CompilerParams(dimension_semantics=sem, vmem_limit_bytes=VMEM_LIMIT)


def _sigmoid(x):
    return 0.5 * jnp.tanh(0.5 * x) + 0.5


def _rms_modulate(x, gain, shift, scale):
    ms = jnp.mean(x * x, axis=-1, keepdims=True)
    return (x * lax.rsqrt(ms + RMS_EPS)) * gain * (1.0 + scale) + shift


def _mod_kernel(c_ref, w_ref, b_ref, o_ref):
    c = c_ref[...]
    ca = (c * jax.nn.sigmoid(c)).astype(BF16)
    o_ref[...] = jnp.dot(ca, w_ref[...].astype(BF16), preferred_element_type=F32) + b_ref[...]


def _mod(c, w_ada, b_ada):
    bsz, d = c.shape
    cols = w_ada.shape[1]
    tn = 1536
    return pl.pallas_call(
        _mod_kernel,
        grid=(cols // tn,),
        in_specs=[pl.BlockSpec((bsz, d), lambda j: (0, 0)),
                  pl.BlockSpec((d, tn), lambda j: (0, j)),
                  pl.BlockSpec((1, tn), lambda j: (0, j))],
        out_specs=pl.BlockSpec((bsz, tn), lambda j: (0, j)),
        out_shape=jax.ShapeDtypeStruct((bsz, cols), F32),
        compiler_params=_params(("arbitrary",)),
        name="mod",
    )(c, w_ada, b_ada.reshape(1, cols))


_C_Q, _C_K, _C_U, _C_GA, _C_GB, _C_F, _C_END = 0, 512, 1024, 1536, 2560, 3584, 3712


def _lane_block():
    return lax.broadcasted_iota(jnp.int32, (1, LANES), 1) // SSM_GROUP


def _to_group_major(tok_ref, flat_ref, rows):
    blk = _lane_block()
    for half in range(2):
        for j in range(SSM_WIDTH // LANES):
            w = []
            for s8 in range(8):
                v = tok_ref[j, pl.ds(8 * half + s8, rows, stride=SSM_CHUNK), :]
                w.append(pltpu.roll(v, s8 * SSM_GROUP, axis=1) if s8 else v)
            for p in range(8):
                acc = w[0]
                for s8 in range(1, 8):
                    acc = jnp.where(blk == (p + s8) % 8, w[s8], acc)
                flat_ref[8 * j + p, :, half * LANES:(half + 1) * LANES] = acc.astype(flat_ref.dtype)


def _to_token_major(flat_ref, tok_ref, rows):
    blk = _lane_block()
    for half in range(2):
        for j in range(SSM_WIDTH // LANES):
            ys = [flat_ref[8 * j + p, :, half * LANES:(half + 1) * LANES] for p in range(8)]
            for s8 in range(8):
                acc = ys[0]
                for p in range(1, 8):
                    acc = jnp.where(blk == (p + s8) % 8, ys[p], acc)
                if s8:
                    acc = pltpu.roll(acc, LANES - s8 * SSM_GROUP, axis=1)
                tok_ref[j, pl.ds(8 * half + s8, rows, stride=SSM_CHUNK), :] = acc


def _bias_lane_placement():
    pq = np.zeros((3 * LANES, LANES), np.float32)
    pk = np.zeros((3 * LANES, LANES), np.float32)
    bq = np.zeros((1, LANES), np.float32)
    bk = np.zeros((1, LANES), np.float32)
    for head in range(FOX_HEADS):
        base = head * 8
        for term in range(3):
            pq[term * LANES + head, base + term] = 1.0
            pk[term * LANES + head, base + 3 + term] = -1.0
            bq[0, base + 3 + term] = 1.0
            bk[0, base + term] = 1.0
    return pq, pk, bq, bk


def _top_bits(a):
    bits = lax.bitcast_convert_type(a, jnp.uint32) & jnp.uint32(0xFFFF0000)
    return lax.bitcast_convert_type(bits, F32)


def _inproj_kernel(tiles_per_batch, x_ref, mod_ref, g_ref, w_ref, wvt_ref, bf_ref, tri_ref,
                   pq_ref, pk_ref, bq_ref, bk_ref,
                   q_ref, k_ref, vt_ref, u_ref, uflat_ref, ga_ref, gb_ref, carry_ref, uslab_ref):
    i = pl.program_id(0)
    h = _rms_modulate(x_ref[...], g_ref[...], mod_ref[0:1, :], mod_ref[1:2, :])
    hb = h.astype(BF16)

    def proj(a, b):
        return jnp.dot(hb, w_ref[:, a:b], preferred_element_type=F32)

    q = proj(_C_Q, _C_K).astype(BF16)
    k = proj(_C_K, _C_U).astype(BF16)
    vt_ref[...] = lax.dot_general(wvt_ref[...], hb, (((1,), (1,)), ((), ())),
                                  preferred_element_type=F32).astype(BF16)
    u = proj(_C_U, _C_GA)
    u_ref[...] = u
    for j in range(SSM_WIDTH // LANES):
        uslab_ref[j] = u[:, j * LANES:(j + 1) * LANES]
    _to_group_major(uslab_ref, uflat_ref, u.shape[0] // SSM_CHUNK)
    ga_ref[...] = _sigmoid(proj(_C_GA, _C_GB)).astype(BF16)
    gb_ref[...] = _sigmoid(proj(_C_GB, _C_F)).astype(BF16)

    f = proj(_C_F, _C_END) + bf_ref[...]
    logf = jnp.minimum(f, 0.0) - jnp.log(1.0 + jnp.exp(-jnp.abs(f)))

    @pl.when(i % tiles_per_batch == 0)
    def _():
        carry_ref[...] = jnp.zeros_like(carry_ref)

    def split3(a):
        hi = _top_bits(a)
        r1 = a - hi
        mid = _top_bits(r1)
        return jnp.concatenate([hi, mid, _top_bits(r1 - mid)], axis=1).astype(BF16)

    part = jnp.dot(tri_ref[...], split3(logf), preferred_element_type=F32)
    cs = (part[:, :LANES] + part[:, LANES:2 * LANES] + part[:, 2 * LANES:]) + carry_ref[0:1, :]
    carry_ref[...] = jnp.broadcast_to(cs[-1:, :], carry_ref.shape)

    terms = split3(cs)
    bias_q = (jnp.dot(terms, pq_ref[...], preferred_element_type=F32) + bq_ref[...]).astype(BF16)
    bias_k = (jnp.dot(terms, pk_ref[...], preferred_element_type=F32) + bk_ref[...]).astype(BF16)
    for p in range(HEAD_PAIRS):
        lanes = slice(p * LANES, (p + 1) * LANES)
        q_ref[:, 2 * p * LANES:(2 * p + 1) * LANES] = q[:, lanes]
        q_ref[:, (2 * p + 1) * LANES:(2 * p + 2) * LANES] = bias_q
        k_ref[:, 2 * p * LANES:(2 * p + 1) * LANES] = k[:, lanes]
        k_ref[:, (2 * p + 1) * LANES:(2 * p + 2) * LANES] = bias_k


def _inproj(x2, mod3, gain, w_all, w_vt, bf_pad, seq):
    n, d = x2.shape
    tm = TOK_TILE
    tpb = seq // tm
    tri = jnp.tril(jnp.ones((tm, tm), BF16))
    pq, pk, bq, bk = _bias_lane_placement()
    tok = lambda w: pl.BlockSpec((tm, w), lambda i: (i, 0))
    const = lambda shape: pl.BlockSpec(shape, lambda i: (0,) * len(shape))
    qk_width = 2 * FOX_WIDTH
    return pl.pallas_call(
        functools.partial(_inproj_kernel, tpb),
        grid=(n // tm,),
        in_specs=[tok(d),
                  pl.BlockSpec((None, N_MOD, d), lambda i: (i // tpb, 0, 0)),
                  const((1, d)), const((d, _C_END)), const((FOX_WIDTH, d)), const((1, LANES)),
                  const((tm, tm)), const(pq.shape), const(pk.shape), const(bq.shape), const(bk.shape)],
        out_specs=[tok(qk_width), tok(qk_width), pl.BlockSpec((FOX_WIDTH, tm), lambda i: (0, i)),
                   tok(SSM_WIDTH),
                   pl.BlockSpec((SSM_GROUPS, tm // SSM_CHUNK, SSM_CHUNK * SSM_GROUP), lambda i: (0, i, 0)),
                   tok(d), tok(d)],
        out_shape=[jax.ShapeDtypeStruct((n, qk_width), BF16)] * 2
        + [jax.ShapeDtypeStruct((FOX_WIDTH, n), BF16)]
        + [jax.ShapeDtypeStruct((n, SSM_WIDTH), F32)]
        + [jax.ShapeDtypeStruct((SSM_GROUPS, n // SSM_CHUNK, SSM_CHUNK * SSM_GROUP), BF16)]
        + [jax.ShapeDtypeStruct((n, d), BF16)] * 2,
        scratch_shapes=[pltpu.VMEM((SUBLANES, LANES), F32),
                        pltpu.VMEM((SSM_WIDTH // LANES, tm, LANES), F32)],
        compiler_params=_params(("arbitrary",)),
        name="inproj",
    )(x2, mod3, gain, w_all, w_vt, bf_pad, tri, jnp.asarray(pq, BF16), jnp.asarray(pk, BF16),
      jnp.asarray(bq), jnp.asarray(bk))


def _attn_kernel(q_ref, k_ref, vt_ref, o_ref, m_ref, acc_ref, sa_ref, sb_ref):
    i = pl.program_id(2)
    tq, tk = ATT_Q_TILE, ATT_K_TILE
    q = q_ref[...]
    lane = lax.broadcasted_iota(jnp.int32, (1, 2 * LANES), 1)
    zq = jnp.zeros_like(q)
    half = FOX_HEAD_DIM
    bias0 = LANES + 16 * pl.program_id(1)
    own0 = (lane < half) | ((lane >= bias0) & (lane < bias0 + 8))
    own1 = ((lane >= half) & (lane < LANES)) | ((lane >= bias0 + 8) & (lane < bias0 + 16))
    q_both = jnp.concatenate([jnp.where(own0, q, zq), jnp.where(own1, q, zq)], axis=0)
    m_ref[...] = jnp.full(m_ref.shape, NEG_BIG, F32)
    acc_ref[...] = jnp.zeros(acc_ref.shape, F32)
    ones_rows = jnp.ones((2 * SUBLANES, tk), BF16)
    key_in_tile = lax.broadcasted_iota(jnp.int32, (tk, 2 * tq), 0)
    qry_pos = i * tq + (lax.broadcasted_iota(jnp.int32, (tk, 2 * tq), 1) & (tq - 1))

    def scores(j, s_ref):
        start = pl.multiple_of(j * tk, tk)
        s = lax.dot_general(k_ref[pl.ds(start, tk), :], q_both, (((1,), (1,)), ((), ())),
                            preferred_element_type=F32)
        s_ref[...] = jnp.where(key_in_tile + j * tk <= qry_pos, s, NEG_BIG)

    def accumulate(j, s_ref):
        start = pl.multiple_of(j * tk, tk)
        va = jnp.concatenate([vt_ref[:, pl.ds(start, tk)], ones_rows], axis=0)
        m_old = m_ref[...]
        m_new = jnp.maximum(m_old, jnp.max(s_ref[...], axis=0, keepdims=True))
        alpha = jnp.exp(m_old - m_new)
        p = jnp.exp(s_ref[...] - m_new).astype(BF16)
        acc_ref[...] = alpha * acc_ref[...] + jnp.dot(va, p, preferred_element_type=F32)
        m_ref[...] = m_new

    def body(jj, c):
        t0 = 2 * jj
        scores(t0 + 1, sb_ref)
        accumulate(t0, sa_ref)
        scores(t0 + 2, sa_ref)
        accumulate(t0 + 1, sb_ref)
        return c

    assert tq == 2 * tk
    scores(0, sa_ref)
    lax.fori_loop(0, i, body, 0)
    scores(2 * i + 1, sb_ref)
    accumulate(2 * i, sa_ref)
    accumulate(2 * i + 1, sb_ref)

    acc = acc_ref[...]
    o_t = jnp.concatenate([acc[0:half, 0:tq] / acc[LANES:LANES + 1, 0:tq],
                           acc[half:LANES, tq:2 * tq] / acc[LANES:LANES + 1, tq:2 * tq]], axis=0)
    o_ref[...] = o_t.T.astype(o_ref.dtype)


def _attention(q, k, v_t, bsz, seq):
    n = q.shape[0]
    t = ATT_Q_TILE
    nq = seq // t
    return pl.pallas_call(
        _attn_kernel,
        grid=(bsz, HEAD_PAIRS, nq),
        in_specs=[pl.BlockSpec((t, 2 * LANES), lambda b, p, i: (b * nq + i, p)),
                  pl.BlockSpec((seq, 2 * LANES), lambda b, p, i: (b, p)),
                  pl.BlockSpec((LANES, seq), lambda b, p, i: (p, b))],
        out_specs=pl.BlockSpec((t, LANES), lambda b, p, i: (b * nq + i, p)),
        out_shape=jax.ShapeDtypeStruct((n, FOX_WIDTH), BF16),
        scratch_shapes=[pltpu.VMEM((1, 2 * t), F32), pltpu.VMEM((LANES + 2 * SUBLANES, 2 * t), F32),
                        pltpu.VMEM((ATT_K_TILE, 2 * t), F32), pltpu.VMEM((ATT_K_TILE, 2 * t), F32)],
        compiler_params=_params(("arbitrary", "arbitrary", "arbitrary")),
        name="attn",
    )(q, k, v_t)


def _ssm_prep_kernel(lrow_ref, lcol_ref, ldt_ref, btr_ref, bti_ref, ctr_ref, cti_ref,
                     toep_ref, bst_ref, bsw_ref, cpw_ref, a_ref):
    p8 = pl.program_id(0) % 8
    t_len, grp = SSM_CHUNK, SSM_GROUP
    dt = jnp.exp(ldt_ref[...])
    lr, li = jnp.minimum(lrow_ref[0:1, :], LAMBDA_RE_MAX), lrow_ref[1:2, :]

    def powers(steps, re, im):
        mag = jnp.exp(steps * (re * dt))
        return mag * jnp.cos(steps * (im * dt)), mag * jnp.sin(steps * (im * dt))

    a_re, a_im = powers(1.0, lr, li)
    den = lr * lr + li * li
    nr = a_re - 1.0
    co_re = (nr * lr + a_im * li) / den
    co_im = (a_im * lr - nr * li) / den
    bbt_re = co_re * btr_ref[...] - co_im * bti_ref[...]
    bbt_im = co_re * bti_ref[...] + co_im * btr_ref[...]

    lag = (lax.broadcasted_iota(jnp.int32, (1, t_len * grp), 1) // grp).astype(F32)
    lcr, lci = jnp.minimum(lcol_ref[:, 0:1], LAMBDA_RE_MAX), lcol_ref[:, 1:2]

    def c_times_power(steps):
        p_re, p_im = powers(steps, lcr, lci)
        return (ctr_ref[...] * p_re - cti_ref[...] * p_im, ctr_ref[...] * p_im + cti_ref[...] * p_re)

    wt_re, wt_im = c_times_power(lag)
    kern = (jnp.dot(bbt_re, wt_re, precision=HIGHEST, preferred_element_type=F32)
            - jnp.dot(bbt_im, wt_im, precision=HIGHEST, preferred_element_type=F32))

    lane = lax.broadcasted_iota(jnp.int32, (1, LANES), 1)
    col_shift = p8 * grp

    def store_cols(ref, rows, lo_half, hi_half):
        ref[rows, 0:LANES] = pltpu.roll(lo_half, col_shift, axis=1).astype(ref.dtype)
        ref[rows, LANES:2 * LANES] = pltpu.roll(hi_half, col_shift, axis=1).astype(ref.dtype)

    def slot_rows(s):
        half, s8 = divmod(s, 8)
        return pl.ds(pl.multiple_of((8 * half + (s8 + p8) % 8) * grp, grp), grp)

    back = (t_len - 1 - lax.broadcasted_iota(jnp.int32, (t_len, 1), 0)).astype(F32)
    e_re, e_im = powers(back, lr, li)
    zero = jnp.zeros((grp, LANES), F32)
    k_lo, k_hi = kern[:, 0:LANES], kern[:, LANES:2 * LANES]
    for s in range(t_len):
        half, s8 = divmod(s, 8)
        keep = lane >= s8 * grp
        r_lo = pltpu.roll(k_lo, s8 * grp, axis=1) if s8 else k_lo
        r_hi = pltpu.roll(k_hi, s8 * grp, axis=1) if s8 else k_hi
        if half == 0:
            lo, hi = jnp.where(keep, r_lo, 0.0), jnp.where(keep, r_hi, r_lo)
        else:
            lo, hi = zero, jnp.where(keep, r_lo, 0.0)
        store_cols(toep_ref, slot_rows(s), lo, hi)
        es_re, es_im = e_re[s:s + 1, :], e_im[s:s + 1, :]
        bs_re = es_re * bbt_re - es_im * bbt_im
        bs_im = es_re * bbt_im + es_im * bbt_re
        bst_ref[slot_rows(s), :] = jnp.concatenate([bs_re, bs_im], axis=1).astype(bst_ref.dtype)
        bsw_ref[slot_rows(s), :] = jnp.concatenate([bs_im, bs_re], axis=1).astype(bsw_ref.dtype)

    w1_re, w1_im = c_times_power(lag + 1.0)
    store_cols(cpw_ref, pl.ds(0, SSM_STATE), w1_re[:, 0:LANES], w1_re[:, LANES:2 * LANES])
    store_cols(cpw_ref, pl.ds(SSM_STATE, SSM_STATE), -w1_im[:, 0:LANES], -w1_im[:, LANES:2 * LANES])
    s_re, s_im = powers(float(t_len), lr, li)
    a_ref[0:1, :] = jnp.concatenate([s_re, s_re], axis=1)
    a_ref[1:2, :] = jnp.concatenate([-s_im, s_im], axis=1)


def _ssm_prep(lambda_re, lambda_im, log_dt, b_re, b_im, c_re, c_im):
    width = SSM_CHUNK * SSM_GROUP
    lam_row = jnp.stack([lambda_re, lambda_im], axis=1)
    tiled = lambda c: jnp.tile(c.transpose(0, 2, 1), (1, 1, SSM_CHUNK))
    per = lambda a, b: pl.BlockSpec((None, a, b), lambda g: (g, 0, 0))
    return pl.pallas_call(
        _ssm_prep_kernel,
        grid=(SSM_GROUPS,),
        in_specs=[per(2, SSM_STATE), per(SSM_STATE, 2), per(1, 1), per(SSM_GROUP, SSM_STATE),
                  per(SSM_GROUP, SSM_STATE), per(SSM_STATE, width), per(SSM_STATE, width)],
        out_specs=[per(width, width), per(width, 2 * SSM_STATE), per(width, 2 * SSM_STATE),
                   per(2 * SSM_STATE, width), per(2, 2 * SSM_STATE)],
        out_shape=[jax.ShapeDtypeStruct((SSM_GROUPS, width, width), BF16),
                   jax.ShapeDtypeStruct((SSM_GROUPS, width, 2 * SSM_STATE), BF16),
                   jax.ShapeDtypeStruct((SSM_GROUPS, width, 2 * SSM_STATE), BF16),
                   jax.ShapeDtypeStruct((SSM_GROUPS, 2 * SSM_STATE, width), BF16),
                   jax.ShapeDtypeStruct((SSM_GROUPS, 2, 2 * SSM_STATE), F32)],
        compiler_params=_params(("arbitrary",)),
        name="ssm_prep",
    )(lam_row, lam_row.transpose(0, 2, 1), log_dt.reshape(SSM_GROUPS, 1, 1),
      b_re.transpose(0, 2, 1), b_im.transpose(0, 2, 1), tiled(c_re), tiled(c_im))


SSM_GROUPS_PER_STEP = 4


def _ssm_kernel(n_chunks, bsz, u_ref, toep_ref, bst_ref, bsw_ref, cpw_ref, a_ref, y_ref,
                contrib_ref, cswap_ref, xprev_ref):
    groups = u_ref.shape[0]
    for k in range(groups):
        u = u_ref[k]
        contrib_ref[k] = jnp.dot(u, bst_ref[k], preferred_element_type=F32)
        cswap_ref[k] = jnp.dot(u, bsw_ref[k], preferred_element_type=F32)
    a1 = [a_ref[k, 0:1, :] for k in range(groups)]
    a2 = [a_ref[k, 1:2, :] for k in range(groups)]

    def step(n, carry):
        rows = pl.ds(n, bsz, stride=n_chunks)
        new = []
        for k in range(groups):
            x, xs = carry[2 * k], carry[2 * k + 1]
            xprev_ref[k, rows, :] = x
            new.append(a1[k] * x + a2[k] * xs + contrib_ref[k, rows, :])
            new.append(a1[k] * xs - a2[k] * x + cswap_ref[k, rows, :])
        return tuple(new)

    zero = jnp.zeros((bsz, 2 * SSM_STATE), F32)
    lax.fori_loop(0, n_chunks, step, (zero,) * (2 * groups), unroll=2)
    for k in range(groups):
        y_ref[k] = (jnp.dot(u_ref[k], toep_ref[k], preferred_element_type=F32)
                    + jnp.dot(xprev_ref[k].astype(BF16), cpw_ref[k], preferred_element_type=F32))


def _ssm(u_flat, toep, b_state, b_swap, c_pow, a_step, bsz):
    g, rows, w = u_flat.shape
    gb = SSM_GROUPS_PER_STEP
    per = lambda a, b: pl.BlockSpec((gb, a, b), lambda i: (i, 0, 0))
    state = pltpu.VMEM((gb, rows, 2 * SSM_STATE), F32)
    return pl.pallas_call(
        functools.partial(_ssm_kernel, rows // bsz, bsz),
        grid=(g // gb,),
        in_specs=[per(rows, w), per(w, w), per(w, 2 * SSM_STATE), per(w, 2 * SSM_STATE),
                  per(2 * SSM_STATE, w), per(2, 2 * SSM_STATE)],
        out_specs=per(rows, w),
        out_shape=jax.ShapeDtypeStruct((g, rows, w), F32),
        scratch_shapes=[state, state, state],
        compiler_params=_params(("arbitrary",)),
        name="ssm",
    )(u_flat, toep, b_state, b_swap, c_pow, a_step)


ROW_SLABS = D_MODEL // LANES // 2
_HIGH_HALF = 0xFFFF0000


def _store_row_tiles(ref, value):
    rows = value.shape[0]
    bits = lax.bitcast_convert_type(value.astype(BF16).astype(F32), jnp.uint32)
    for j in range(ROW_SLABS):
        low = bits[:, j * LANES:(j + 1) * LANES] >> 16
        high = bits[:, (j + ROW_SLABS) * LANES:(j + ROW_SLABS + 1) * LANES] & jnp.uint32(_HIGH_HALF)
        ref[pl.ds(j, rows, stride=ROW_SLABS), :] = high | low


def _load_row_tiles(ref, rows):
    words = [ref[pl.ds(j, rows, stride=ROW_SLABS), :] for j in range(ROW_SLABS)]
    low = [lax.bitcast_convert_type(w << 16, F32) for w in words]
    high = [lax.bitcast_convert_type(w & jnp.uint32(_HIGH_HALF), F32) for w in words]
    return jnp.concatenate(low + high, axis=1)


def _row_tile_copy(src_ref, src_row, dst_ref, dst_row, sem):
    src = src_ref.at[pl.ds(pl.multiple_of(src_row * ROW_SLABS, ROW_SLABS), ROW_SLABS), :]
    dst = dst_ref.at[pl.ds(pl.multiple_of(dst_row * ROW_SLABS, ROW_SLABS), ROW_SLABS), :]
    return pltpu.make_async_copy(src, dst, sem)


def _mix_kernel(x_ref, of_ref, yf_ref, u_ref, ga_ref, gb_ref, mod_ref, dsk_ref, wglu_ref, wfox_ref,
                wssm_ref, wo_ref, g2_ref, wr_ref, br_ref, x1_ref, h2_ref, lg_ref, ytok_ref):
    _to_token_major(yf_ref, ytok_ref, yf_ref.shape[1])
    y_ssm = jnp.concatenate([ytok_ref[j] for j in range(SSM_WIDTH // LANES)], axis=1)
    y = y_ssm + dsk_ref[...] * u_ref[...]
    y = 0.5 * y * (1.0 + jnp.tanh(math.sqrt(2.0 / math.pi) * (y + 0.044715 * (y * y * y))))
    gl = jnp.dot(y.astype(BF16), wglu_ref[...], preferred_element_type=F32)
    o_ssm = gl[:, :SSM_WIDTH] * _sigmoid(gl[:, SSM_WIDTH:])
    merged = (ga_ref[...].astype(F32) * jnp.dot(of_ref[...], wfox_ref[...], preferred_element_type=F32)
              + gb_ref[...].astype(F32) * jnp.dot(o_ssm.astype(BF16), wssm_ref[...],
                                                  preferred_element_type=F32))
    x1 = x_ref[...] + mod_ref[2:3, :] * jnp.dot(merged.astype(BF16), wo_ref[...],
                                                 preferred_element_type=F32)
    x1_ref[...] = x1
    h2 = _rms_modulate(x1, g2_ref[...], mod_ref[3:4, :], mod_ref[4:5, :])
    _store_row_tiles(h2_ref, h2)
    a1 = _top_bits(h2)
    a2 = _top_bits(h2 - a1)
    lhs = jnp.concatenate([a1, a1, a2], axis=1).astype(BF16)
    lg_ref[...] = jnp.dot(lhs, wr_ref[...], preferred_element_type=F32) + br_ref[...]


def _mix(x2, o_fox, y_flat, u, sga, sgb, mod3, d_skip, w_glu, w_fox, w_ssm, w_o, g2, w_r, b_r, seq):
    n, d = x2.shape
    tm = MIX_TILE
    tpb = seq // tm
    tok = lambda w: pl.BlockSpec((tm, w), lambda i: (i, 0))
    const = lambda a: pl.BlockSpec(a.shape, lambda i: (0,) * a.ndim)
    flat = pl.BlockSpec((SSM_GROUPS, tm // SSM_CHUNK, SSM_CHUNK * SSM_GROUP), lambda i: (0, i, 0))
    return pl.pallas_call(
        _mix_kernel,
        grid=(n // tm,),
        in_specs=[tok(d), tok(FOX_WIDTH), flat, tok(SSM_WIDTH), tok(d), tok(d),
                  pl.BlockSpec((None, N_MOD, d), lambda i: (i // tpb, 0, 0)),
                  const(d_skip), const(w_glu), const(w_fox), const(w_ssm), const(w_o), const(g2),
                  const(w_r), const(b_r)],
        out_specs=[tok(d), pl.BlockSpec((tm * ROW_SLABS, LANES), lambda i: (i, 0)), tok(LANES)],
        out_shape=[jax.ShapeDtypeStruct((n, d), F32), jax.ShapeDtypeStruct((n * ROW_SLABS, LANES), jnp.uint32),
                   jax.ShapeDtypeStruct((n, LANES), F32)],
        scratch_shapes=[pltpu.VMEM((SSM_WIDTH // LANES, tm, LANES), F32)],
        compiler_params=_params(("arbitrary",)),
        name="mix",
    )(x2, o_fox, y_flat, u, sga, sgb, mod3, d_skip, w_glu, w_fox, w_ssm, w_o, g2, w_r, b_r)


def _route_kernel(lg_ref, tri_ref, er_ref, wt_ref, cnt_ref, carry_ref):
    i = pl.program_id(0)

    @pl.when(i == 0)
    def _():
        carry_ref[...] = jnp.zeros_like(carry_ref)

    lg = lg_ref[...].T
    tm = lg.shape[1]
    row = lax.broadcasted_iota(jnp.int32, (LANES, tm), 0)
    neg = jnp.full_like(lg, -jnp.inf)

    def first_argmax(vals):
        mx = jnp.max(vals, axis=0, keepdims=True)
        ix = jnp.min(jnp.where(vals == mx, row, LANES), axis=0, keepdims=True)
        return mx, ix

    is_group = row < N_GROUPS
    g_max, gi = first_argmax(jnp.where(is_group, lg, neg))
    g_sum = jnp.sum(jnp.where(is_group, jnp.exp(lg - g_max), 0.0), axis=0, keepdims=True)
    p_group = 1.0 / g_sum
    lo = N_GROUPS + EXPERTS_PER_GROUP * gi
    in_group = (row >= lo) & (row < lo + EXPERTS_PER_GROUP)
    cand = jnp.where(in_group, lg, neg)
    v1, i1 = first_argmax(cand)
    v2, i2 = first_argmax(jnp.where(row == i1, neg, cand))
    tt = jnp.exp(v2 - v1)
    w1 = p_group / (1.0 + tt)
    w2 = p_group * tt / (1.0 + tt)
    e1 = i1 - N_GROUPS
    e2 = i2 - N_GROUPS
    sel1 = row == e1
    sel2 = row == e2
    onehot = (sel1 | sel2).astype(F32)
    before = jnp.dot(onehot.astype(BF16), tri_ref[...], preferred_element_type=F32) + carry_ref[:, 0:1]
    r1 = jnp.sum(jnp.where(sel1, before, 0.0), axis=0, keepdims=True).astype(jnp.int32)
    r2 = jnp.sum(jnp.where(sel2, before, 0.0), axis=0, keepdims=True).astype(jnp.int32)
    total = before[:, tm - 1:tm] + onehot[:, tm - 1:tm]
    carry_ref[...] = jnp.broadcast_to(total, carry_ref.shape)
    cnt_ref[...] = jnp.broadcast_to(total, cnt_ref.shape)
    slot = lax.broadcasted_iota(jnp.int32, (SUBLANES, tm), 0)
    er_ref[...] = jnp.where(slot == 0, e1, jnp.where(slot == 1, e2, jnp.where(slot == 2, r1, r2)))
    wt_ref[...] = jnp.where(row == 0, w1, jnp.where(row == 1, w2, 0.0)).T


def _route(logits):
    n = logits.shape[0]
    tm = TOK_TILE
    tri = jnp.triu(jnp.ones((tm, tm), BF16), k=1)
    tok = pl.BlockSpec((tm, LANES), lambda i: (i, 0))
    return pl.pallas_call(
        _route_kernel,
        grid=(n // tm,),
        in_specs=[tok, pl.BlockSpec((tm, tm), lambda i: (0, 0))],
        out_specs=[pl.BlockSpec((SUBLANES, tm), lambda i: (0, i)), tok,
                   pl.BlockSpec((LANES, LANES), lambda i: (0, 0))],
        out_shape=[jax.ShapeDtypeStruct((SUBLANES, n), jnp.int32), jax.ShapeDtypeStruct((n, LANES), F32),
                   jax.ShapeDtypeStruct((LANES, LANES), F32)],
        scratch_shapes=[pltpu.VMEM((LANES, LANES), F32)],
        compiler_params=_params(("arbitrary",)),
        name="route",
    )(logits, tri)


ISSUE_UNROLL = 8


def _dispatch_kernel(dest_ref, h_ref, rows_in_ref, rows_ref, sem):
    del rows_in_ref
    tm = h_ref.shape[0] // ROW_SLABS

    def issue(g, c):
        for j in range(ISSUE_UNROLL):
            t = g * ISSUE_UNROLL + j
            _row_tile_copy(h_ref, t, rows_ref, dest_ref[0, 0, t], sem).start(priority=0)
            _row_tile_copy(h_ref, t, rows_ref, dest_ref[0, 0, tm + t], sem).start(priority=1)
        return c

    lax.fori_loop(0, tm // ISSUE_UNROLL, issue, 0)
    for _ in range(2):
        pltpu.make_async_copy(h_ref, rows_ref.at[pl.ds(0, tm * ROW_SLABS), :], sem).wait()


def _dispatch(dest3, h2_tiles, rows_zero):
    tm = MOVE_TILE
    n = h2_tiles.shape[0] // ROW_SLABS
    return pl.pallas_call(
        _dispatch_kernel,
        grid=(n // tm,),
        in_specs=[pl.BlockSpec((1, 1, 2 * tm), lambda i: (i, 0, 0), memory_space=pltpu.SMEM),
                  pl.BlockSpec((tm * ROW_SLABS, LANES), lambda i: (i, 0)),
                  pl.BlockSpec(memory_space=pl.ANY)],
        out_specs=pl.BlockSpec(memory_space=pl.ANY),
        out_shape=jax.ShapeDtypeStruct(rows_zero.shape, rows_zero.dtype),
        scratch_shapes=[pltpu.SemaphoreType.DMA(())],
        input_output_aliases={2: 0},
        compiler_params=_params(("arbitrary",)),
        name="dispatch",
    )(dest3, h2_tiles, rows_zero)


def _combine_kernel(n_steps, dest_ref, dnext_ref, x1_ref, wt_ref, mod_ref, gf_ref, yr_ref, o_ref,
                    buf_ref, sem):
    i = pl.program_id(0)
    tm = x1_ref.shape[0]

    def gather(idx_ref, which):
        def issue(g, c):
            for j in range(ISSUE_UNROLL):
                t = g * ISSUE_UNROLL + j
                _row_tile_copy(yr_ref, idx_ref[0, 0, t], buf_ref.at[which, 0], t,
                               sem.at[which]).start(priority=0)
                _row_tile_copy(yr_ref, idx_ref[0, 0, tm + t], buf_ref.at[which, 1], t,
                               sem.at[which]).start(priority=1)
            return c

        lax.fori_loop(0, tm // ISSUE_UNROLL, issue, 0)

    cur = i % 2

    @pl.when(i == 0)
    def _():
        gather(dest_ref, 0)

    @pl.when(i + 1 < n_steps)
    def _():
        gather(dnext_ref, 1 - cur)

    for slot in range(2):
        pltpu.make_async_copy(yr_ref.at[pl.ds(0, tm * ROW_SLABS), :], buf_ref.at[cur, slot],
                              sem.at[cur]).wait()
    wt = wt_ref[...]
    moe = (wt[:, 0:1] * _load_row_tiles(buf_ref.at[cur, 0], tm)
           + wt[:, 1:2] * _load_row_tiles(buf_ref.at[cur, 1], tm))
    x = x1_ref[...] + mod_ref[5:6, :] * moe
    ms = jnp.mean(x * x, axis=-1, keepdims=True)
    o_ref[...] = (x * lax.rsqrt(ms + RMS_EPS)) * gf_ref[...]


def _combine(dest3, x1, wts, mod3, final_g, y_rows, seq):
    n, d = x1.shape
    tm = MOVE_TILE
    tpb = seq // tm
    n_steps = n // tm
    idx_spec = lambda f: pl.BlockSpec((1, 1, 2 * tm), f, memory_space=pltpu.SMEM)
    return pl.pallas_call(
        functools.partial(_combine_kernel, n_steps),
        grid=(n_steps,),
        in_specs=[idx_spec(lambda i: (i, 0, 0)),
                  idx_spec(lambda i: (jnp.minimum(i + 1, n_steps - 1), 0, 0)),
                  pl.BlockSpec((tm, d), lambda i: (i, 0)),
                  pl.BlockSpec((tm, LANES), lambda i: (i, 0)),
                  pl.BlockSpec((None, N_MOD, d), lambda i: (i // tpb, 0, 0)),
                  pl.BlockSpec((1, d), lambda i: (0, 0)),
                  pl.BlockSpec(memory_space=pl.ANY)],
        out_specs=pl.BlockSpec((tm, d), lambda i: (i, 0)),
        out_shape=jax.ShapeDtypeStruct((n, d), F32),
        scratch_shapes=[pltpu.VMEM((2, 2, tm * ROW_SLABS, LANES), jnp.uint32),
                        pltpu.SemaphoreType.DMA((2,))],
        compiler_params=_params(("arbitrary",)),
        name="combine",
    )(dest3, dest3, x1, wts, mod3, final_g, y_rows)


def _expert_kernel(be_ref, seg_ref, nxt_ref, nv_ref, x_ref, wg_hbm, wu_hbm, wd_hbm, y_ref,
                   wg_buf, wu_buf, wd_buf, wgb_ref, wub_ref, wdb_ref, sem):
    i = pl.program_id(0)
    valid = i < nv_ref[0]
    first = (i == 0) | (be_ref[i] != be_ref[jnp.maximum(i - 1, 0)])
    slot = seg_ref[i] % 2

    def weight_copies(e, s):
        return [pltpu.make_async_copy(hbm.at[e], buf.at[s], sem.at[s])
                for hbm, buf in ((wg_hbm, wg_buf), (wu_hbm, wu_buf), (wd_hbm, wd_buf))]

    @pl.when(valid & (i == 0))
    def _():
        for c in weight_copies(be_ref[0], 0):
            c.start()

    @pl.when(valid & first)
    def _():
        for c in weight_copies(be_ref[i], slot):
            c.wait()

        @pl.when(nxt_ref[i] >= 0)
        def _():
            for c in weight_copies(nxt_ref[i], 1 - slot):
                c.start()

        wgb_ref[...] = wg_buf[slot].astype(BF16)
        wub_ref[...] = wu_buf[slot].astype(BF16)
        wdb_ref[...] = wd_buf[slot].astype(BF16)

    @pl.when(valid)
    def _():
        xb = _load_row_tiles(x_ref, x_ref.shape[0] // ROW_SLABS).astype(BF16)
        a = jnp.dot(xb, wgb_ref[...], preferred_element_type=F32)
        b = jnp.dot(xb, wub_ref[...], preferred_element_type=F32)
        hid = (a * _sigmoid(a)) * b
        _store_row_tiles(y_ref, jnp.dot(hid.astype(BF16), wdb_ref[...], preferred_element_type=F32))

    @pl.when(jnp.logical_not(valid))
    def _():
        y_ref[...] = jnp.zeros_like(y_ref)


def _experts(blk_e, n_valid, x_rows, w_gate, w_up, w_down):
    d = D_MODEL
    rows = x_rows.shape[0] // ROW_SLABS
    tb = ROW_BLOCK
    n_blocks = rows // tb
    idx = jnp.arange(n_blocks, dtype=jnp.int32)
    change = (idx == 0) | (blk_e != jnp.roll(blk_e, 1))
    seg = jnp.cumsum(change.astype(jnp.int32)) - 1
    later_start = (idx[None, :] > idx[:, None]) & change[None, :] & (idx[None, :] < n_valid[0])
    none = jnp.int32(N_EXPERTS)
    nxt = jnp.min(jnp.where(later_start, blk_e[None, :], none), axis=1)
    nxt = jnp.where(nxt == none, -1, nxt)
    row_spec = pl.BlockSpec((tb * ROW_SLABS, LANES), lambda i, *_: (i, 0))
    grid_spec = pltpu.PrefetchScalarGridSpec(
        num_scalar_prefetch=4,
        grid=(n_blocks,),
        in_specs=[row_spec, pl.BlockSpec(memory_space=pl.ANY), pl.BlockSpec(memory_space=pl.ANY),
                  pl.BlockSpec(memory_space=pl.ANY)],
        out_specs=row_spec,
        scratch_shapes=[pltpu.VMEM((2, d, D_EXPERT), F32), pltpu.VMEM((2, d, D_EXPERT), F32),
                        pltpu.VMEM((2, D_EXPERT, d), F32),
                        pltpu.VMEM((d, D_EXPERT), BF16), pltpu.VMEM((d, D_EXPERT), BF16),
                        pltpu.VMEM((D_EXPERT, d), BF16), pltpu.SemaphoreType.DMA((2,))],
    )
    return pl.pallas_call(
        _expert_kernel,
        grid_spec=grid_spec,
        out_shape=jax.ShapeDtypeStruct(x_rows.shape, x_rows.dtype),
        compiler_params=_params(("arbitrary",)),
        name="experts",
    )(blk_e, seg, nxt.astype(jnp.int32), n_valid, x_rows, w_gate, w_up, w_down)


def kernel(x, c, w_ada, b_ada, norm_mix_g, w_in, b_forget, w_out_fox, lambda_re, lambda_im, log_dt,
           ssm_b_re, ssm_b_im, ssm_c_re, ssm_c_im, d_skip, w_glu, w_out_ssm, w_o, norm_ffn_g,
           w_router_group, b_router_group, w_router_expert, b_router_expert, w_gate_e, w_up_e,
           w_down_e, final_g):
    bsz, seq, d = x.shape
    n = bsz * seq
    assert w_ada.shape[0] == 1, "the final RMSNorm is fused into the (single) layer's combine kernel"
    xc = x.reshape(n, d)
    for l in range(1):
        mod3 = _mod(c, w_ada[l], b_ada[l]).reshape(bsz, N_MOD, d)

        wi = w_in[l]
        s_q, s_k, s_v, s_f, s_u, s_ga = 512, 1024, 1536, 1544, 2056, 3080
        scale = FOX_HEAD_DIM ** -0.5
        w_all = jnp.concatenate(
            [wi[:, :s_q] * scale, wi[:, s_q:s_k], wi[:, s_f:s_u], wi[:, s_u:s_ga],
             wi[:, s_ga:], jnp.pad(wi[:, s_v:s_f], ((0, 0), (0, LANES - FOX_HEADS)))],
            axis=1).astype(BF16)
        w_vt = wi[:, s_k:s_v].T.astype(BF16)
        bf_pad = jnp.pad(b_forget[l], (0, LANES - FOX_HEADS)).reshape(1, LANES)
        q, k, v_t, u, u_flat, sga, sgb = _inproj(xc, mod3, norm_mix_g[l].reshape(1, d), w_all, w_vt,
                                                 bf_pad, seq)

        o_fox = _attention(q, k, v_t, bsz, seq)

        toep, b_state, b_swap, c_pow, a_step = _ssm_prep(
            lambda_re[l], lambda_im[l], log_dt[l], ssm_b_re[l], ssm_b_im[l], ssm_c_re[l], ssm_c_im[l])
        y_flat = _ssm(u_flat, toep, b_state, b_swap, c_pow, a_step, bsz)

        w_r = jnp.pad(jnp.concatenate([w_router_group[l], w_router_expert[l]], axis=1),
                      ((0, 0), (0, LANES - N_GROUPS - N_EXPERTS)))
        w_r1 = _top_bits(w_r)
        w_r2 = _top_bits(w_r - w_r1)
        w_r = jnp.concatenate([w_r1, w_r2, w_r1], axis=0).astype(BF16)
        b_r = jnp.pad(jnp.concatenate([b_router_group[l], b_router_expert[l]]),
                      (0, LANES - N_GROUPS - N_EXPERTS)).reshape(1, LANES)
        x1, h2, logits = _mix(xc, o_fox, y_flat, u, sga, sgb, mod3, d_skip[l].reshape(1, SSM_WIDTH),
                              w_glu[l].astype(BF16), w_out_fox[l].astype(BF16),
                              w_out_ssm[l].astype(BF16), w_o[l].astype(BF16),
                              norm_ffn_g[l].reshape(1, d), w_r, b_r, seq)

        er, wts, cnt = _route(logits)
        counts = cnt[:N_EXPERTS, 0].astype(jnp.int32)
        pcounts = ((counts + ROW_BLOCK - 1) // ROW_BLOCK) * ROW_BLOCK
        pends = jnp.cumsum(pcounts)
        pstarts = pends - pcounts
        hit = er[0:2, None, :] == jnp.arange(N_EXPERTS, dtype=jnp.int32)[None, :, None]
        dest = jnp.sum(jnp.where(hit, pstarts[None, :, None], 0), axis=1) + er[2:4]
        rows = 2 * n + N_EXPERTS * ROW_BLOCK
        n_blocks = rows // ROW_BLOCK
        blk_start = jnp.arange(n_blocks, dtype=jnp.int32) * ROW_BLOCK
        blk_e = jnp.minimum(jnp.sum((pends[None, :] <= blk_start[:, None]).astype(jnp.int32), axis=1),
                            N_EXPERTS - 1)
        n_valid = (pends[-1:] // ROW_BLOCK).astype(jnp.int32)
        dest3 = (dest.astype(jnp.int32).reshape(2, n // MOVE_TILE, MOVE_TILE).transpose(1, 0, 2)
                 .reshape(n // MOVE_TILE, 1, 2 * MOVE_TILE))

        x_rows = _dispatch(dest3, h2, jnp.zeros((rows * ROW_SLABS, LANES), jnp.uint32))
        y_rows = _experts(blk_e, n_valid, x_rows, w_gate_e[l], w_up_e[l], w_down_e[l])
        xc = _combine(dest3, x1, wts, mod3, final_g.reshape(1, d), y_rows, seq)
    return xc.reshape(bsz, seq, d)
```

```python
import functools
import math

import jax
import jax.numpy as jnp
import numpy as np
from jax import lax
from jax.experimental import pallas as pl
from jax.experimental.pallas import tpu as pltpu

F32 = jnp.float32
BF16 = jnp.bfloat16

D_MODEL = 1024
N_MOD = 6
RMS_EPS = 1e-6
FOX_HEADS = 8
FOX_HEAD_DIM = 64
FOX_WIDTH = FOX_HEADS * FOX_HEAD_DIM
HEAD_PAIRS = FOX_HEADS // 2
SSM_WIDTH = 512
SSM_GROUP = 16
SSM_GROUPS = SSM_WIDTH // SSM_GROUP
SSM_STATE = 64
LAMBDA_RE_MAX = -1e-4
N_GROUPS = 4
EXPERTS_PER_GROUP = 8
N_EXPERTS = N_GROUPS * EXPERTS_PER_GROUP
D_EXPERT = 512

LANES = 128
SUBLANES = 8
VMEM_LIMIT = 56 * 1024 * 1024

SSM_CHUNK = 16
TOK_TILE = 512
MIX_TILE = 256
ATT_Q_TILE = 512
ATT_K_TILE = 256
ROW_BLOCK = 512
MOVE_TILE = 512
NEG_BIG = -1e30

HIGHEST = lax.Precision.HIGHEST


def _params(sem):
    return pltpu.CompilerParams(dimension_semantics=sem, vmem_limit_bytes=VMEM_LIMIT)


def _sigmoid(x):
    return 0.5 * jnp.tanh(0.5 * x) + 0.5


def _rms_modulate(x, gain, shift, scale):
    ms = jnp.mean(x * x, axis=-1, keepdims=True)
    return (x * lax.rsqrt(ms + RMS_EPS)) * gain * (1.0 + scale) + shift


def _mod_kernel(c_ref, w_ref, b_ref, o_ref):
    c = c_ref[...]
    ca = (c * jax.nn.sigmoid(c)).astype(BF16)
    o_ref[...] = jnp.dot(ca, w_ref[...].astype(BF16), preferred_element_type=F32) + b_ref[...]


def _mod(c, w_ada, b_ada):
    bsz, d = c.shape
    cols = w_ada.shape[1]
    tn = 1536
    return pl.pallas_call(
        _mod_kernel,
        grid=(cols // tn,),
        in_specs=[pl.BlockSpec((bsz, d), lambda j: (0, 0)),
                  pl.BlockSpec((d, tn), lambda j: (0, j)),
                  pl.BlockSpec((1, tn), lambda j: (0, j))],
        out_specs=pl.BlockSpec((bsz, tn), lambda j: (0, j)),
        out_shape=jax.ShapeDtypeStruct((bsz, cols), F32),
        compiler_params=_params(("arbitrary",)),
        name="mod",
    )(c, w_ada, b_ada.reshape(1, cols))


_C_Q, _C_K, _C_U, _C_GA, _C_GB, _C_F, _C_END = 0, 512, 1024, 1536, 2560, 3584, 3712


def _lane_block():
    return lax.broadcasted_iota(jnp.int32, (1, LANES), 1) // SSM_GROUP


def _to_group_major(tok_ref, flat_ref, rows):
    blk = _lane_block()
    for half in range(2):
        for j in range(SSM_WIDTH // LANES):
            w = []
            for s8 in range(8):
                v = tok_ref[j, pl.ds(8 * half + s8, rows, stride=SSM_CHUNK), :]
                w.append(pltpu.roll(v, s8 * SSM_GROUP, axis=1) if s8 else v)
            for p in range(8):
                acc = w[0]
                for s8 in range(1, 8):
                    acc = jnp.where(blk == (p + s8) % 8, w[s8], acc)
                flat_ref[8 * j + p, :, half * LANES:(half + 1) * LANES] = acc.astype(flat_ref.dtype)


def _to_token_major(flat_ref, tok_ref, rows):
    blk = _lane_block()
    for half in range(2):
        for j in range(SSM_WIDTH // LANES):
            ys = [flat_ref[8 * j + p, :, half * LANES:(half + 1) * LANES] for p in range(8)]
            for s8 in range(8):
                acc = ys[0]
                for p in range(1, 8):
                    acc = jnp.where(blk == (p + s8) % 8, ys[p], acc)
                if s8:
                    acc = pltpu.roll(acc, LANES - s8 * SSM_GROUP, axis=1)
                tok_ref[j, pl.ds(8 * half + s8, rows, stride=SSM_CHUNK), :] = acc


def _bias_lane_placement():
    pq = np.zeros((3 * LANES, LANES), np.float32)
    pk = np.zeros((3 * LANES, LANES), np.float32)
    bq = np.zeros((1, LANES), np.float32)
    bk = np.zeros((1, LANES), np.float32)
    for head in range(FOX_HEADS):
        base = head * 8
        for term in range(3):
            pq[term * LANES + head, base + term] = 1.0
            pk[term * LANES + head, base + 3 + term] = -1.0
            bq[0, base + 3 + term] = 1.0
            bk[0, base + term] = 1.0
    return pq, pk, bq, bk


def _top_bits(a):
    bits = lax.bitcast_convert_type(a, jnp.uint32) & jnp.uint32(0xFFFF0000)
    return lax.bitcast_convert_type(bits, F32)


def _inproj_kernel(tiles_per_batch, x_ref, mod_ref, g_ref, w_ref, wvt_ref, bf_ref, tri_ref,
                   pq_ref, pk_ref, bq_ref, bk_ref,
                   q_ref, k_ref, vt_ref, u_ref, uflat_ref, ga_ref, gb_ref, carry_ref, uslab_ref):
    i = pl.program_id(0)
    h = _rms_modulate(x_ref[...], g_ref[...], mod_ref[0:1, :], mod_ref[1:2, :])
    hb = h.astype(BF16)

    def proj(a, b):
        return jnp.dot(hb, w_ref[:, a:b], preferred_element_type=F32)

    q = proj(_C_Q, _C_K).astype(BF16)
    k = proj(_C_K, _C_U).astype(BF16)
    vt_ref[...] = lax.dot_general(wvt_ref[...], hb, (((1,), (1,)), ((), ())),
                                  preferred_element_type=F32).astype(BF16)
    u = proj(_C_U, _C_GA)
    u_ref[...] = u
    for j in range(SSM_WIDTH // LANES):
        uslab_ref[j] = u[:, j * LANES:(j + 1) * LANES]
    _to_group_major(uslab_ref, uflat_ref, u.shape[0] // SSM_CHUNK)
    ga_ref[...] = _sigmoid(proj(_C_GA, _C_GB)).astype(BF16)
    gb_ref[...] = _sigmoid(proj(_C_GB, _C_F)).astype(BF16)

    f = proj(_C_F, _C_END) + bf_ref[...]
    logf = jnp.minimum(f, 0.0) - jnp.log(1.0 + jnp.exp(-jnp.abs(f)))

    @pl.when(i % tiles_per_batch == 0)
    def _():
        carry_ref[...] = jnp.zeros_like(carry_ref)

    def split3(a):
        hi = _top_bits(a)
        r1 = a - hi
        mid = _top_bits(r1)
        return jnp.concatenate([hi, mid, _top_bits(r1 - mid)], axis=1).astype(BF16)

    part = jnp.dot(tri_ref[...], split3(logf), preferred_element_type=F32)
    cs = (part[:, :LANES] + part[:, LANES:2 * LANES] + part[:, 2 * LANES:]) + carry_ref[0:1, :]
    carry_ref[...] = jnp.broadcast_to(cs[-1:, :], carry_ref.shape)

    terms = split3(cs)
    bias_q = (jnp.dot(terms, pq_ref[...], preferred_element_type=F32) + bq_ref[...]).astype(BF16)
    bias_k = (jnp.dot(terms, pk_ref[...], preferred_element_type=F32) + bk_ref[...]).astype(BF16)
    for p in range(HEAD_PAIRS):
        lanes = slice(p * LANES, (p + 1) * LANES)
        q_ref[:, 2 * p * LANES:(2 * p + 1) * LANES] = q[:, lanes]
        q_ref[:, (2 * p + 1) * LANES:(2 * p + 2) * LANES] = bias_q
        k_ref[:, 2 * p * LANES:(2 * p + 1) * LANES] = k[:, lanes]
        k_ref[:, (2 * p + 1) * LANES:(2 * p + 2) * LANES] = bias_k


def _inproj(x2, mod3, gain, w_all, w_vt, bf_pad, seq):
    n, d = x2.shape
    tm = TOK_TILE
    tpb = seq // tm
    tri = jnp.tril(jnp.ones((tm, tm), BF16))
    pq, pk, bq, bk = _bias_lane_placement()
    tok = lambda w: pl.BlockSpec((tm, w), lambda i: (i, 0))
    const = lambda shape: pl.BlockSpec(shape, lambda i: (0,) * len(shape))
    qk_width = 2 * FOX_WIDTH
    return pl.pallas_call(
        functools.partial(_inproj_kernel, tpb),
        grid=(n // tm,),
        in_specs=[tok(d),
                  pl.BlockSpec((None, N_MOD, d), lambda i: (i // tpb, 0, 0)),
                  const((1, d)), const((d, _C_END)), const((FOX_WIDTH, d)), const((1, LANES)),
                  const((tm, tm)), const(pq.shape), const(pk.shape), const(bq.shape), const(bk.shape)],
        out_specs=[tok(qk_width), tok(qk_width), pl.BlockSpec((FOX_WIDTH, tm), lambda i: (0, i)),
                   tok(SSM_WIDTH),
                   pl.BlockSpec((SSM_GROUPS, tm // SSM_CHUNK, SSM_CHUNK * SSM_GROUP), lambda i: (0, i, 0)),
                   tok(d), tok(d)],
        out_shape=[jax.ShapeDtypeStruct((n, qk_width), BF16)] * 2
        + [jax.ShapeDtypeStruct((FOX_WIDTH, n), BF16)]
        + [jax.ShapeDtypeStruct((n, SSM_WIDTH), F32)]
        + [jax.ShapeDtypeStruct((SSM_GROUPS, n // SSM_CHUNK, SSM_CHUNK * SSM_GROUP), BF16)]
        + [jax.ShapeDtypeStruct((n, d), BF16)] * 2,
        scratch_shapes=[pltpu.VMEM((SUBLANES, LANES), F32),
                        pltpu.VMEM((SSM_WIDTH // LANES, tm, LANES), F32)],
        compiler_params=_params(("arbitrary",)),
        name="inproj",
    )(x2, mod3, gain, w_all, w_vt, bf_pad, tri, jnp.asarray(pq, BF16), jnp.asarray(pk, BF16),
      jnp.asarray(bq), jnp.asarray(bk))


def _attn_static_kernel(q_ref, k_ref, vt_ref, o_ref, m_ref, acc_ref, sa_ref, sb_ref):
    tq, tk = ATT_Q_TILE, ATT_K_TILE
    seq = q_ref.shape[0]
    nq = seq // tq
    half = FOX_HEAD_DIM
    lane = lax.broadcasted_iota(jnp.int32, (1, 2 * LANES), 1)
    bias0 = LANES + 16 * pl.program_id(1)
    own0 = (lane < half) | ((lane >= bias0) & (lane < bias0 + 8))
    own1 = ((lane >= half) & (lane < LANES)) | ((lane >= bias0 + 8) & (lane < bias0 + 16))
    ones_rows = jnp.ones((2 * SUBLANES, tk), BF16)
    key_in_tile = lax.broadcasted_iota(jnp.int32, (tk, 2 * tq), 0)
    qry_in_tile = lax.broadcasted_iota(jnp.int32, (tk, 2 * tq), 1) & (tq - 1)
    bufs = (sa_ref, sb_ref)
    steps = [(i, j) for i in range(nq) for j in range((i + 1) * (tq // tk))]
    q_cache = {}

    def q_both(i):
        if i not in q_cache:
            q = q_ref[i * tq:(i + 1) * tq, :]
            zq = jnp.zeros_like(q)
            q_cache[i] = jnp.concatenate([jnp.where(own0, q, zq), jnp.where(own1, q, zq)], axis=0)
        return q_cache[i]

    def scores(n):
        i, j = steps[n]
        s = lax.dot_general(k_ref[j * tk:(j + 1) * tk, :], q_both(i), (((1,), (1,)), ((), ())),
                            preferred_element_type=F32)
        if j * tk + tk - 1 > i * tq:
            s = jnp.where(key_in_tile + (j * tk - i * tq) <= qry_in_tile, s, NEG_BIG)
        bufs[n % 2][...] = s

    scores(0)
    for n, (i, j) in enumerate(steps):
        if n + 1 < len(steps):
            scores(n + 1)
        s_ref = bufs[n % 2]
        va = jnp.concatenate([vt_ref[:, j * tk:(j + 1) * tk], ones_rows], axis=0)
        if j == 0:
            m_new = jnp.max(s_ref[...], axis=0, keepdims=True)
            p = jnp.exp(s_ref[...] - m_new).astype(BF16)
            acc_ref[...] = jnp.dot(va, p, preferred_element_type=F32)
        else:
            m_old = m_ref[...]
            m_new = jnp.maximum(m_old, jnp.max(s_ref[...], axis=0, keepdims=True))
            alpha = jnp.exp(m_old - m_new)
            p = jnp.exp(s_ref[...] - m_new).astype(BF16)
            acc_ref[...] = alpha * acc_ref[...] + jnp.dot(va, p, preferred_element_type=F32)
        m_ref[...] = m_new
        if j == (i + 1) * (tq // tk) - 1:
            acc = acc_ref[...]
            o_t = jnp.concatenate([acc[0:half, 0:tq] / acc[LANES:LANES + 1, 0:tq],
                                   acc[half:LANES, tq:2 * tq] / acc[LANES:LANES + 1, tq:2 * tq]], axis=0)
            o_ref[i * tq:(i + 1) * tq, :] = o_t.T.astype(o_ref.dtype)


def _attention_static(q, k, v_t, bsz, seq):
    n = q.shape[0]
    t = ATT_Q_TILE
    return pl.pallas_call(
        _attn_static_kernel,
        grid=(bsz, HEAD_PAIRS),
        in_specs=[pl.BlockSpec((seq, 2 * LANES), lambda b, p: (b, p)),
                  pl.BlockSpec((seq, 2 * LANES), lambda b, p: (b, p)),
                  pl.BlockSpec((LANES, seq), lambda b, p: (p, b))],
        out_specs=pl.BlockSpec((seq, LANES), lambda b, p: (b, p)),
        out_shape=jax.ShapeDtypeStruct((n, FOX_WIDTH), BF16),
        scratch_shapes=[pltpu.VMEM((1, 2 * t), F32), pltpu.VMEM((LANES + 2 * SUBLANES, 2 * t), F32),
                        pltpu.VMEM((ATT_K_TILE, 2 * t), F32), pltpu.VMEM((ATT_K_TILE, 2 * t), F32)],
        compiler_params=_params(("arbitrary", "arbitrary")),
        name="attn",
    )(q, k, v_t)


def _attn_kernel(q_ref, k_ref, vt_ref, o_ref, m_ref, acc_ref, sa_ref, sb_ref):
    i = pl.program_id(2)
    tq, tk = ATT_Q_TILE, ATT_K_TILE
    q = q_ref[...]
    lane = lax.broadcasted_iota(jnp.int32, (1, 2 * LANES), 1)
    zq = jnp.zeros_like(q)
    half = FOX_HEAD_DIM
    bias0 = LANES + 16 * pl.program_id(1)
    own0 = (lane < half) | ((lane >= bias0) & (lane < bias0 + 8))
    own1 = ((lane >= half) & (lane < LANES)) | ((lane >= bias0 + 8) & (lane < bias0 + 16))
    q_both = jnp.concatenate([jnp.where(own0, q, zq), jnp.where(own1, q, zq)], axis=0)
    m_ref[...] = jnp.full(m_ref.shape, NEG_BIG, F32)
    acc_ref[...] = jnp.zeros(acc_ref.shape, F32)
    ones_rows = jnp.ones((2 * SUBLANES, tk), BF16)
    key_in_tile = lax.broadcasted_iota(jnp.int32, (tk, 2 * tq), 0)
    qry_pos = i * tq + (lax.broadcasted_iota(jnp.int32, (tk, 2 * tq), 1) & (tq - 1))

    def scores(j, s_ref):
        start = pl.multiple_of(j * tk, tk)
        s = lax.dot_general(k_ref[pl.ds(start, tk), :], q_both, (((1,), (1,)), ((), ())),
                            preferred_element_type=F32)
        s_ref[...] = jnp.where(key_in_tile + j * tk <= qry_pos, s, NEG_BIG)

    def accumulate(j, s_ref):
        start = pl.multiple_of(j * tk, tk)
        va = jnp.concatenate([vt_ref[:, pl.ds(start, tk)], ones_rows], axis=0)
        m_old = m_ref[...]
        m_new = jnp.maximum(m_old, jnp.max(s_ref[...], axis=0, keepdims=True))
        alpha = jnp.exp(m_old - m_new)
        p = jnp.exp(s_ref[...] - m_new).astype(BF16)
        acc_ref[...] = alpha * acc_ref[...] + jnp.dot(va, p, preferred_element_type=F32)
        m_ref[...] = m_new

    def body(jj, c):
        t0 = 2 * jj
        scores(t0 + 1, sb_ref)
        accumulate(t0, sa_ref)
        scores(t0 + 2, sa_ref)
        accumulate(t0 + 1, sb_ref)
        return c

    assert tq == 2 * tk
    scores(0, sa_ref)
    lax.fori_loop(0, i, body, 0)
    scores(2 * i + 1, sb_ref)
    accumulate(2 * i, sa_ref)
    accumulate(2 * i + 1, sb_ref)

    acc = acc_ref[...]
    o_t = jnp.concatenate([acc[0:half, 0:tq] / acc[LANES:LANES + 1, 0:tq],
                           acc[half:LANES, tq:2 * tq] / acc[LANES:LANES + 1, tq:2 * tq]], axis=0)
    o_ref[...] = o_t.T.astype(o_ref.dtype)


def _attention(q, k, v_t, bsz, seq):
    n = q.shape[0]
    t = ATT_Q_TILE
    nq = seq // t
    return pl.pallas_call(
        _attn_kernel,
        grid=(bsz, HEAD_PAIRS, nq),
        in_specs=[pl.BlockSpec((t, 2 * LANES), lambda b, p, i: (b * nq + i, p)),
                  pl.BlockSpec((seq, 2 * LANES), lambda b, p, i: (b, p)),
                  pl.BlockSpec((LANES, seq), lambda b, p, i: (p, b))],
        out_specs=pl.BlockSpec((t, LANES), lambda b, p, i: (b * nq + i, p)),
        out_shape=jax.ShapeDtypeStruct((n, FOX_WIDTH), BF16),
        scratch_shapes=[pltpu.VMEM((1, 2 * t), F32), pltpu.VMEM((LANES + 2 * SUBLANES, 2 * t), F32),
                        pltpu.VMEM((ATT_K_TILE, 2 * t), F32), pltpu.VMEM((ATT_K_TILE, 2 * t), F32)],
        compiler_params=_params(("arbitrary", "arbitrary", "arbitrary")),
        name="attn",
    )(q, k, v_t)


def _ssm_prep_kernel(lrow_ref, lcol_ref, ldt_ref, btr_ref, bti_ref, ctr_ref, cti_ref,
                     toep_ref, bst_ref, bsw_ref, cpw_ref, a_ref):
    p8 = pl.program_id(0) % 8
    t_len, grp = SSM_CHUNK, SSM_GROUP
    dt = jnp.exp(ldt_ref[...])
    lr, li = jnp.minimum(lrow_ref[0:1, :], LAMBDA_RE_MAX), lrow_ref[1:2, :]

    def powers(steps, re, im):
        mag = jnp.exp(steps * (re * dt))
        return mag * jnp.cos(steps * (im * dt)), mag * jnp.sin(steps * (im * dt))

    a_re, a_im = powers(1.0, lr, li)
    den = lr * lr + li * li
    nr = a_re - 1.0
    co_re = (nr * lr + a_im * li) / den
    co_im = (a_im * lr - nr * li) / den
    bbt_re = co_re * btr_ref[...] - co_im * bti_ref[...]
    bbt_im = co_re * bti_ref[...] + co_im * btr_ref[...]

    lag = (lax.broadcasted_iota(jnp.int32, (1, t_len * grp), 1) // grp).astype(F32)
    lcr, lci = jnp.minimum(lcol_ref[:, 0:1], LAMBDA_RE_MAX), lcol_ref[:, 1:2]

    def c_times_power(steps):
        p_re, p_im = powers(steps, lcr, lci)
        return (ctr_ref[...] * p_re - cti_ref[...] * p_im, ctr_ref[...] * p_im + cti_ref[...] * p_re)

    wt_re, wt_im = c_times_power(lag)
    kern = (jnp.dot(bbt_re, wt_re, precision=HIGHEST, preferred_element_type=F32)
            - jnp.dot(bbt_im, wt_im, precision=HIGHEST, preferred_element_type=F32))

    lane = lax.broadcasted_iota(jnp.int32, (1, LANES), 1)
    col_shift = p8 * grp

    def store_cols(ref, rows, lo_half, hi_half):
        ref[rows, 0:LANES] = pltpu.roll(lo_half, col_shift, axis=1).astype(ref.dtype)
        ref[rows, LANES:2 * LANES] = pltpu.roll(hi_half, col_shift, axis=1).astype(ref.dtype)

    def slot_rows(s):
        half, s8 = divmod(s, 8)
        return pl.ds(pl.multiple_of((8 * half + (s8 + p8) % 8) * grp, grp), grp)

    back = (t_len - 1 - lax.broadcasted_iota(jnp.int32, (t_len, 1), 0)).astype(F32)
    e_re, e_im = powers(back, lr, li)
    zero = jnp.zeros((grp, LANES), F32)
    k_lo, k_hi = kern[:, 0:LANES], kern[:, LANES:2 * LANES]
    for s in range(t_len):
        half, s8 = divmod(s, 8)
        keep = lane >= s8 * grp
        r_lo = pltpu.roll(k_lo, s8 * grp, axis=1) if s8 else k_lo
        r_hi = pltpu.roll(k_hi, s8 * grp, axis=1) if s8 else k_hi
        if half == 0:
            lo, hi = jnp.where(keep, r_lo, 0.0), jnp.where(keep, r_hi, r_lo)
        else:
            lo, hi = zero, jnp.where(keep, r_lo, 0.0)
        store_cols(toep_ref, slot_rows(s), lo, hi)
        es_re, es_im = e_re[s:s + 1, :], e_im[s:s + 1, :]
        bs_re = es_re * bbt_re - es_im * bbt_im
        bs_im = es_re * bbt_im + es_im * bbt_re
        bst_ref[slot_rows(s), :] = jnp.concatenate([bs_re, bs_im], axis=1).astype(bst_ref.dtype)
        bsw_ref[slot_rows(s), :] = jnp.concatenate([bs_im, bs_re], axis=1).astype(bsw_ref.dtype)

    w1_re, w1_im = c_times_power(lag + 1.0)
    store_cols(cpw_ref, pl.ds(0, SSM_STATE), w1_re[:, 0:LANES], w1_re[:, LANES:2 * LANES])
    store_cols(cpw_ref, pl.ds(SSM_STATE, SSM_STATE), -w1_im[:, 0:LANES], -w1_im[:, LANES:2 * LANES])
    s_re, s_im = powers(float(t_len), lr, li)
    a_ref[0:1, :] = jnp.concatenate([s_re, s_re], axis=1)
    a_ref[1:2, :] = jnp.concatenate([-s_im, s_im], axis=1)


def _ssm_prep(lambda_re, lambda_im, log_dt, b_re, b_im, c_re, c_im):
    width = SSM_CHUNK * SSM_GROUP
    lam_row = jnp.stack([lambda_re, lambda_im], axis=1)
    tiled = lambda c: jnp.tile(c.transpose(0, 2, 1), (1, 1, SSM_CHUNK))
    per = lambda a, b: pl.BlockSpec((None, a, b), lambda g: (g, 0, 0))
    return pl.pallas_call(
        _ssm_prep_kernel,
        grid=(SSM_GROUPS,),
        in_specs=[per(2, SSM_STATE), per(SSM_STATE, 2), per(1, 1), per(SSM_GROUP, SSM_STATE),
                  per(SSM_GROUP, SSM_STATE), per(SSM_STATE, width), per(SSM_STATE, width)],
        out_specs=[per(width, width), per(width, 2 * SSM_STATE), per(width, 2 * SSM_STATE),
                   per(2 * SSM_STATE, width), per(2, 2 * SSM_STATE)],
        out_shape=[jax.ShapeDtypeStruct((SSM_GROUPS, width, width), BF16),
                   jax.ShapeDtypeStruct((SSM_GROUPS, width, 2 * SSM_STATE), BF16),
                   jax.ShapeDtypeStruct((SSM_GROUPS, width, 2 * SSM_STATE), BF16),
                   jax.ShapeDtypeStruct((SSM_GROUPS, 2 * SSM_STATE, width), BF16),
                   jax.ShapeDtypeStruct((SSM_GROUPS, 2, 2 * SSM_STATE), F32)],
        compiler_params=_params(("arbitrary",)),
        name="ssm_prep",
    )(lam_row, lam_row.transpose(0, 2, 1), log_dt.reshape(SSM_GROUPS, 1, 1),
      b_re.transpose(0, 2, 1), b_im.transpose(0, 2, 1), tiled(c_re), tiled(c_im))


SSM_GROUPS_PER_STEP = 4


def _ssm_kernel(n_chunks, bsz, u_ref, toep_ref, bst_ref, bsw_ref, cpw_ref, a_ref, y_ref,
                contrib_ref, cswap_ref, xprev_ref):
    groups = u_ref.shape[0]
    for k in range(groups):
        u = u_ref[k]
        contrib_ref[k] = jnp.dot(u, bst_ref[k], preferred_element_type=F32)
        cswap_ref[k] = jnp.dot(u, bsw_ref[k], preferred_element_type=F32)
    a1 = [a_ref[k, 0:1, :] for k in range(groups)]
    a2 = [a_ref[k, 1:2, :] for k in range(groups)]

    def step(n, carry):
        rows = pl.ds(n, bsz, stride=n_chunks)
        new = []
        for k in range(groups):
            x, xs = carry[2 * k], carry[2 * k + 1]
            xprev_ref[k, rows, :] = x
            new.append(a1[k] * x + a2[k] * xs + contrib_ref[k, rows, :])
            new.append(a1[k] * xs - a2[k] * x + cswap_ref[k, rows, :])
        return tuple(new)

    zero = jnp.zeros((bsz, 2 * SSM_STATE), F32)
    lax.fori_loop(0, n_chunks, step, (zero,) * (2 * groups), unroll=2)
    for k in range(groups):
        y_ref[k] = (jnp.dot(u_ref[k], toep_ref[k], preferred_element_type=F32)
                    + jnp.dot(xprev_ref[k].astype(BF16), cpw_ref[k], preferred_element_type=F32))


def _ssm(u_flat, toep, b_state, b_swap, c_pow, a_step, bsz):
    g, rows, w = u_flat.shape
    gb = SSM_GROUPS_PER_STEP
    per = lambda a, b: pl.BlockSpec((gb, a, b), lambda i: (i, 0, 0))
    state = pltpu.VMEM((gb, rows, 2 * SSM_STATE), F32)
    return pl.pallas_call(
        functools.partial(_ssm_kernel, rows // bsz, bsz),
        grid=(g // gb,),
        in_specs=[per(rows, w), per(w, w), per(w, 2 * SSM_STATE), per(w, 2 * SSM_STATE),
                  per(2 * SSM_STATE, w), per(2, 2 * SSM_STATE)],
        out_specs=per(rows, w),
        out_shape=jax.ShapeDtypeStruct((g, rows, w), F32),
        scratch_shapes=[state, state, state],
        compiler_params=_params(("arbitrary",)),
        name="ssm",
    )(u_flat, toep, b_state, b_swap, c_pow, a_step)


ROW_SLABS = D_MODEL // LANES // 2
_HIGH_HALF = 0xFFFF0000


def _store_row_tiles(ref, value):
    rows = value.shape[0]
    bits = lax.bitcast_convert_type(value.astype(BF16).astype(F32), jnp.uint32)
    for j in range(ROW_SLABS):
        low = bits[:, j * LANES:(j + 1) * LANES] >> 16
        high = bits[:, (j + ROW_SLABS) * LANES:(j + ROW_SLABS + 1) * LANES] & jnp.uint32(_HIGH_HALF)
        ref[pl.ds(j, rows, stride=ROW_SLABS), :] = high | low


def _load_row_tiles(ref, rows):
    words = [ref[pl.ds(j, rows, stride=ROW_SLABS), :] for j in range(ROW_SLABS)]
    low = [lax.bitcast_convert_type(w << 16, F32) for w in words]
    high = [lax.bitcast_convert_type(w & jnp.uint32(_HIGH_HALF), F32) for w in words]
    return jnp.concatenate(low + high, axis=1)


def _row_tile_copy(src_ref, src_row, dst_ref, dst_row, sem):
    src = src_ref.at[pl.ds(pl.multiple_of(src_row * ROW_SLABS, ROW_SLABS), ROW_SLABS), :]
    dst = dst_ref.at[pl.ds(pl.multiple_of(dst_row * ROW_SLABS, ROW_SLABS), ROW_SLABS), :]
    return pltpu.make_async_copy(src, dst, sem)


def _mix_kernel(x_ref, of_ref, yf_ref, u_ref, ga_ref, gb_ref, mod_ref, dsk_ref, wglu_ref, wfox_ref,
                wssm_ref, wo_ref, g2_ref, wr_ref, br_ref, x1_ref, h2_ref, lg_ref, ytok_ref):
    _to_token_major(yf_ref, ytok_ref, yf_ref.shape[1])
    y_ssm = jnp.concatenate([ytok_ref[j] for j in range(SSM_WIDTH // LANES)], axis=1)
    y = y_ssm + dsk_ref[...] * u_ref[...]
    y = 0.5 * y * (1.0 + jnp.tanh(math.sqrt(2.0 / math.pi) * (y + 0.044715 * (y * y * y))))
    gl = jnp.dot(y.astype(BF16), wglu_ref[...], preferred_element_type=F32)
    o_ssm = gl[:, :SSM_WIDTH] * _sigmoid(gl[:, SSM_WIDTH:])
    merged = (ga_ref[...].astype(F32) * jnp.dot(of_ref[...], wfox_ref[...], preferred_element_type=F32)
              + gb_ref[...].astype(F32) * jnp.dot(o_ssm.astype(BF16), wssm_ref[...],
                                                  preferred_element_type=F32))
    x1 = x_ref[...] + mod_ref[2:3, :] * jnp.dot(merged.astype(BF16), wo_ref[...],
                                                 preferred_element_type=F32)
    x1_ref[...] = x1
    h2 = _rms_modulate(x1, g2_ref[...], mod_ref[3:4, :], mod_ref[4:5, :])
    _store_row_tiles(h2_ref, h2)
    a1 = _top_bits(h2)
    a2 = _top_bits(h2 - a1)
    lhs = jnp.concatenate([a1, a1, a2], axis=1).astype(BF16)
    lg_ref[...] = jnp.dot(lhs, wr_ref[...], preferred_element_type=F32) + br_ref[...]


def _mix(x2, o_fox, y_flat, u, sga, sgb, mod3, d_skip, w_glu, w_fox, w_ssm, w_o, g2, w_r, b_r, seq):
    n, d = x2.shape
    tm = MIX_TILE
    tpb = seq // tm
    tok = lambda w: pl.BlockSpec((tm, w), lambda i: (i, 0))
    const = lambda a: pl.BlockSpec(a.shape, lambda i: (0,) * a.ndim)
    flat = pl.BlockSpec((SSM_GROUPS, tm // SSM_CHUNK, SSM_CHUNK * SSM_GROUP), lambda i: (0, i, 0))
    return pl.pallas_call(
        _mix_kernel,
        grid=(n // tm,),
        in_specs=[tok(d), tok(FOX_WIDTH), flat, tok(SSM_WIDTH), tok(d), tok(d),
                  pl.BlockSpec((None, N_MOD, d), lambda i: (i // tpb, 0, 0)),
                  const(d_skip), const(w_glu), const(w_fox), const(w_ssm), const(w_o), const(g2),
                  const(w_r), const(b_r)],
        out_specs=[tok(d), pl.BlockSpec((tm * ROW_SLABS, LANES), lambda i: (i, 0)), tok(LANES)],
        out_shape=[jax.ShapeDtypeStruct((n, d), F32), jax.ShapeDtypeStruct((n * ROW_SLABS, LANES), jnp.uint32),
                   jax.ShapeDtypeStruct((n, LANES), F32)],
        scratch_shapes=[pltpu.VMEM((SSM_WIDTH // LANES, tm, LANES), F32)],
        compiler_params=_params(("arbitrary",)),
        name="mix",
    )(x2, o_fox, y_flat, u, sga, sgb, mod3, d_skip, w_glu, w_fox, w_ssm, w_o, g2, w_r, b_r)


def _route_kernel(lg_ref, tri_ref, er_ref, wt_ref, cnt_ref, carry_ref):
    i = pl.program_id(0)

    @pl.when(i == 0)
    def _():
        carry_ref[...] = jnp.zeros_like(carry_ref)

    lg = lg_ref[...].T
    tm = lg.shape[1]
    row = lax.broadcasted_iota(jnp.int32, (LANES, tm), 0)
    neg = jnp.full_like(lg, -jnp.inf)

    def first_argmax(vals):
        mx = jnp.max(vals, axis=0, keepdims=True)
        ix = jnp.min(jnp.where(vals == mx, row, LANES), axis=0, keepdims=True)
        return mx, ix

    is_group = row < N_GROUPS
    g_max, gi = first_argmax(jnp.where(is_group, lg, neg))
    g_sum = jnp.sum(jnp.where(is_group, jnp.exp(lg - g_max), 0.0), axis=0, keepdims=True)
    p_group = 1.0 / g_sum
    lo = N_GROUPS + EXPERTS_PER_GROUP * gi
    in_group = (row >= lo) & (row < lo + EXPERTS_PER_GROUP)
    cand = jnp.where(in_group, lg, neg)
    v1, i1 = first_argmax(cand)
    v2, i2 = first_argmax(jnp.where(row == i1, neg, cand))
    tt = jnp.exp(v2 - v1)
    w1 = p_group / (1.0 + tt)
    w2 = p_group * tt / (1.0 + tt)
    e1 = i1 - N_GROUPS
    e2 = i2 - N_GROUPS
    sel1 = row == e1
    sel2 = row == e2
    onehot = (sel1 | sel2).astype(F32)
    before = jnp.dot(onehot.astype(BF16), tri_ref[...], preferred_element_type=F32) + carry_ref[:, 0:1]
    r1 = jnp.sum(jnp.where(sel1, before, 0.0), axis=0, keepdims=True).astype(jnp.int32)
    r2 = jnp.sum(jnp.where(sel2, before, 0.0), axis=0, keepdims=True).astype(jnp.int32)
    total = before[:, tm - 1:tm] + onehot[:, tm - 1:tm]
    carry_ref[...] = jnp.broadcast_to(total, carry_ref.shape)
    cnt_ref[...] = jnp.broadcast_to(total, cnt_ref.shape)
    slot = lax.broadcasted_iota(jnp.int32, (SUBLANES, tm), 0)
    er_ref[...] = jnp.where(slot == 0, e1, jnp.where(slot == 1, e2, jnp.where(slot == 2, r1, r2)))
    wt_ref[...] = jnp.where(row == 0, w1, jnp.where(row == 1, w2, 0.0)).T


def _route(logits):
    n = logits.shape[0]
    tm = TOK_TILE
    tri = jnp.triu(jnp.ones((tm, tm), BF16), k=1)
    tok = pl.BlockSpec((tm, LANES), lambda i: (i, 0))
    return pl.pallas_call(
        _route_kernel,
        grid=(n // tm,),
        in_specs=[tok, pl.BlockSpec((tm, tm), lambda i: (0, 0))],
        out_specs=[pl.BlockSpec((SUBLANES, tm), lambda i: (0, i)), tok,
                   pl.BlockSpec((LANES, LANES), lambda i: (0, 0))],
        out_shape=[jax.ShapeDtypeStruct((SUBLANES, n), jnp.int32), jax.ShapeDtypeStruct((n, LANES), F32),
                   jax.ShapeDtypeStruct((LANES, LANES), F32)],
        scratch_shapes=[pltpu.VMEM((LANES, LANES), F32)],
        compiler_params=_params(("arbitrary",)),
        name="route",
    )(logits, tri)


ISSUE_UNROLL = 8


def _dispatch_kernel(dest_ref, h_ref, rows_in_ref, rows_ref, sem):
    del rows_in_ref
    tm = h_ref.shape[0] // ROW_SLABS

    def issue(g, c):
        for j in range(ISSUE_UNROLL):
            t = g * ISSUE_UNROLL + j
            _row_tile_copy(h_ref, t, rows_ref, dest_ref[0, 0, t], sem).start(priority=0)
            _row_tile_copy(h_ref, t, rows_ref, dest_ref[0, 0, tm + t], sem).start(priority=1)
        return c

    lax.fori_loop(0, tm // ISSUE_UNROLL, issue, 0)
    for _ in range(2):
        pltpu.make_async_copy(h_ref, rows_ref.at[pl.ds(0, tm * ROW_SLABS), :], sem).wait()


def _dispatch(dest3, h2_tiles, rows_zero):
    tm = MOVE_TILE
    n = h2_tiles.shape[0] // ROW_SLABS
    return pl.pallas_call(
        _dispatch_kernel,
        grid=(n // tm,),
        in_specs=[pl.BlockSpec((1, 1, 2 * tm), lambda i: (i, 0, 0), memory_space=pltpu.SMEM),
                  pl.BlockSpec((tm * ROW_SLABS, LANES), lambda i: (i, 0)),
                  pl.BlockSpec(memory_space=pl.ANY)],
        out_specs=pl.BlockSpec(memory_space=pl.ANY),
        out_shape=jax.ShapeDtypeStruct(rows_zero.shape, rows_zero.dtype),
        scratch_shapes=[pltpu.SemaphoreType.DMA(())],
        input_output_aliases={2: 0},
        compiler_params=_params(("arbitrary",)),
        name="dispatch",
    )(dest3, h2_tiles, rows_zero)


def _combine_kernel(n_steps, dest_ref, dnext_ref, x1_ref, wt_ref, mod_ref, gf_ref, yr_ref, o_ref,
                    buf_ref, sem):
    i = pl.program_id(0)
    tm = x1_ref.shape[0]

    def gather(idx_ref, which):
        def issue(g, c):
            for j in range(ISSUE_UNROLL):
                t = g * ISSUE_UNROLL + j
                _row_tile_copy(yr_ref, idx_ref[0, 0, t], buf_ref.at[which, 0], t,
                               sem.at[which]).start(priority=0)
                _row_tile_copy(yr_ref, idx_ref[0, 0, tm + t], buf_ref.at[which, 1], t,
                               sem.at[which]).start(priority=1)
            return c

        lax.fori_loop(0, tm // ISSUE_UNROLL, issue, 0)

    cur = i % 2

    @pl.when(i == 0)
    def _():
        gather(dest_ref, 0)

    @pl.when(i + 1 < n_steps)
    def _():
        gather(dnext_ref, 1 - cur)

    for slot in range(2):
        pltpu.make_async_copy(yr_ref.at[pl.ds(0, tm * ROW_SLABS), :], buf_ref.at[cur, slot],
                              sem.at[cur]).wait()
    wt = wt_ref[...]
    moe = (wt[:, 0:1] * _load_row_tiles(buf_ref.at[cur, 0], tm)
           + wt[:, 1:2] * _load_row_tiles(buf_ref.at[cur, 1], tm))
    x = x1_ref[...] + mod_ref[5:6, :] * moe
    ms = jnp.mean(x * x, axis=-1, keepdims=True)
    o_ref[...] = (x * lax.rsqrt(ms + RMS_EPS)) * gf_ref[...]


def _combine(dest3, x1, wts, mod3, final_g, y_rows, seq):
    n, d = x1.shape
    tm = MOVE_TILE
    tpb = seq // tm
    n_steps = n // tm
    idx_spec = lambda f: pl.BlockSpec((1, 1, 2 * tm), f, memory_space=pltpu.SMEM)
    return pl.pallas_call(
        functools.partial(_combine_kernel, n_steps),
        grid=(n_steps,),
        in_specs=[idx_spec(lambda i: (i, 0, 0)),
                  idx_spec(lambda i: (jnp.minimum(i + 1, n_steps - 1), 0, 0)),
                  pl.BlockSpec((tm, d), lambda i: (i, 0)),
                  pl.BlockSpec((tm, LANES), lambda i: (i, 0)),
                  pl.BlockSpec((None, N_MOD, d), lambda i: (i // tpb, 0, 0)),
                  pl.BlockSpec((1, d), lambda i: (0, 0)),
                  pl.BlockSpec(memory_space=pl.ANY)],
        out_specs=pl.BlockSpec((tm, d), lambda i: (i, 0)),
        out_shape=jax.ShapeDtypeStruct((n, d), F32),
        scratch_shapes=[pltpu.VMEM((2, 2, tm * ROW_SLABS, LANES), jnp.uint32),
                        pltpu.SemaphoreType.DMA((2,))],
        compiler_params=_params(("arbitrary",)),
        name="combine",
    )(dest3, dest3, x1, wts, mod3, final_g, y_rows)


def _expert_kernel(be_ref, seg_ref, nxt_ref, nv_ref, x_ref, wg_hbm, wu_hbm, wd_hbm, y_ref,
                   wg_buf, wu_buf, wd_buf, wgb_ref, wub_ref, wdb_ref, sem):
    i = pl.program_id(0)
    valid = i < nv_ref[0]
    first = (i == 0) | (be_ref[i] != be_ref[jnp.maximum(i - 1, 0)])
    slot = seg_ref[i] % 2

    def weight_copies(e, s):
        return [pltpu.make_async_copy(hbm.at[e], buf.at[s], sem.at[s])
                for hbm, buf in ((wg_hbm, wg_buf), (wu_hbm, wu_buf), (wd_hbm, wd_buf))]

    @pl.when(valid & (i == 0))
    def _():
        for c in weight_copies(be_ref[0], 0):
            c.start()

    @pl.when(valid & first)
    def _():
        for c in weight_copies(be_ref[i], slot):
            c.wait()

        @pl.when(nxt_ref[i] >= 0)
        def _():
            for c in weight_copies(nxt_ref[i], 1 - slot):
                c.start()

        wgb_ref[...] = wg_buf[slot].astype(BF16)
        wub_ref[...] = wu_buf[slot].astype(BF16)
        wdb_ref[...] = wd_buf[slot].astype(BF16)

    @pl.when(valid)
    def _():
        xb = _load_row_tiles(x_ref, x_ref.shape[0] // ROW_SLABS).astype(BF16)
        a = jnp.dot(xb, wgb_ref[...], preferred_element_type=F32)
        b = jnp.dot(xb, wub_ref[...], preferred_element_type=F32)
        hid = (a * _sigmoid(a)) * b
        _store_row_tiles(y_ref, jnp.dot(hid.astype(BF16), wdb_ref[...], preferred_element_type=F32))

    @pl.when(jnp.logical_not(valid))
    def _():
        y_ref[...] = jnp.zeros_like(y_ref)


def _experts(blk_e, n_valid, x_rows, w_gate, w_up, w_down):
    d = D_MODEL
    rows = x_rows.shape[0] // ROW_SLABS
    tb = ROW_BLOCK
    n_blocks = rows // tb
    idx = jnp.arange(n_blocks, dtype=jnp.int32)
    change = (idx == 0) | (blk_e != jnp.roll(blk_e, 1))
    seg = jnp.cumsum(change.astype(jnp.int32)) - 1
    later_start = (idx[None, :] > idx[:, None]) & change[None, :] & (idx[None, :] < n_valid[0])
    none = jnp.int32(N_EXPERTS)
    nxt = jnp.min(jnp.where(later_start, blk_e[None, :], none), axis=1)
    nxt = jnp.where(nxt == none, -1, nxt)
    row_spec = pl.BlockSpec((tb * ROW_SLABS, LANES), lambda i, *_: (i, 0))
    grid_spec = pltpu.PrefetchScalarGridSpec(
        num_scalar_prefetch=4,
        grid=(n_blocks,),
        in_specs=[row_spec, pl.BlockSpec(memory_space=pl.ANY), pl.BlockSpec(memory_space=pl.ANY),
                  pl.BlockSpec(memory_space=pl.ANY)],
        out_specs=row_spec,
        scratch_shapes=[pltpu.VMEM((2, d, D_EXPERT), F32), pltpu.VMEM((2, d, D_EXPERT), F32),
                        pltpu.VMEM((2, D_EXPERT, d), F32),
                        pltpu.VMEM((d, D_EXPERT), BF16), pltpu.VMEM((d, D_EXPERT), BF16),
                        pltpu.VMEM((D_EXPERT, d), BF16), pltpu.SemaphoreType.DMA((2,))],
    )
    return pl.pallas_call(
        _expert_kernel,
        grid_spec=grid_spec,
        out_shape=jax.ShapeDtypeStruct(x_rows.shape, x_rows.dtype),
        compiler_params=_params(("arbitrary",)),
        name="experts",
    )(blk_e, seg, nxt.astype(jnp.int32), n_valid, x_rows, w_gate, w_up, w_down)


def kernel(x, c, w_ada, b_ada, norm_mix_g, w_in, b_forget, w_out_fox, lambda_re, lambda_im, log_dt,
           ssm_b_re, ssm_b_im, ssm_c_re, ssm_c_im, d_skip, w_glu, w_out_ssm, w_o, norm_ffn_g,
           w_router_group, b_router_group, w_router_expert, b_router_expert, w_gate_e, w_up_e,
           w_down_e, final_g):
    bsz, seq, d = x.shape
    n = bsz * seq
    assert w_ada.shape[0] == 1, "the final RMSNorm is fused into the (single) layer's combine kernel"
    xc = x.reshape(n, d)
    for l in range(1):
        mod3 = _mod(c, w_ada[l], b_ada[l]).reshape(bsz, N_MOD, d)

        wi = w_in[l]
        s_q, s_k, s_v, s_f, s_u, s_ga = 512, 1024, 1536, 1544, 2056, 3080
        scale = FOX_HEAD_DIM ** -0.5
        w_all = jnp.concatenate(
            [wi[:, :s_q] * scale, wi[:, s_q:s_k], wi[:, s_f:s_u], wi[:, s_u:s_ga],
             wi[:, s_ga:], jnp.pad(wi[:, s_v:s_f], ((0, 0), (0, LANES - FOX_HEADS)))],
            axis=1).astype(BF16)
        w_vt = wi[:, s_k:s_v].T.astype(BF16)
        bf_pad = jnp.pad(b_forget[l], (0, LANES - FOX_HEADS)).reshape(1, LANES)
        q, k, v_t, u, u_flat, sga, sgb = _inproj(xc, mod3, norm_mix_g[l].reshape(1, d), w_all, w_vt,
                                                 bf_pad, seq)

        o_fox = _attention_static(q, k, v_t, bsz, seq)

        toep, b_state, b_swap, c_pow, a_step = _ssm_prep(
            lambda_re[l], lambda_im[l], log_dt[l], ssm_b_re[l], ssm_b_im[l], ssm_c_re[l], ssm_c_im[l])
        y_flat = _ssm(u_flat, toep, b_state, b_swap, c_pow, a_step, bsz)

        w_r = jnp.pad(jnp.concatenate([w_router_group[l], w_router_expert[l]], axis=1),
                      ((0, 0), (0, LANES - N_GROUPS - N_EXPERTS)))
        w_r1 = _top_bits(w_r)
        w_r2 = _top_bits(w_r - w_r1)
        w_r = jnp.concatenate([w_r1, w_r2, w_r1], axis=0).astype(BF16)
        b_r = jnp.pad(jnp.concatenate([b_router_group[l], b_router_expert[l]]),
                      (0, LANES - N_GROUPS - N_EXPERTS)).reshape(1, LANES)
        x1, h2, logits = _mix(xc, o_fox, y_flat, u, sga, sgb, mod3, d_skip[l].reshape(1, SSM_WIDTH),
                              w_glu[l].astype(BF16), w_out_fox[l].astype(BF16),
                              w_out_ssm[l].astype(BF16), w_o[l].astype(BF16),
                              norm_ffn_g[l].reshape(1, d), w_r, b_r, seq)

        er, wts, cnt = _route(logits)
        counts = cnt[:N_EXPERTS, 0].astype(jnp.int32)
        pcounts = ((counts + ROW_BLOCK - 1) // ROW_BLOCK) * ROW_BLOCK
        pends = jnp.cumsum(pcounts)
        pstarts = pends - pcounts
        hit = er[0:2, None, :] == jnp.arange(N_EXPERTS, dtype=jnp.int32)[None, :, None]
        dest = jnp.sum(jnp.where(hit, pstarts[None, :, None], 0), axis=1) + er[2:4]
        rows = 2 * n + N_EXPERTS * ROW_BLOCK
        n_blocks = rows // ROW_BLOCK
        blk_start = jnp.arange(n_blocks, dtype=jnp.int32) * ROW_BLOCK
        blk_e = jnp.minimum(jnp.sum((pends[None, :] <= blk_start[:, None]).astype(jnp.int32), axis=1),
                            N_EXPERTS - 1)
        n_valid = (pends[-1:] // ROW_BLOCK).astype(jnp.int32)
        dest3 = (dest.astype(jnp.int32).reshape(2, n // MOVE_TILE, MOVE_TILE).transpose(1, 0, 2)
                 .reshape(n // MOVE_TILE, 1, 2 * MOVE_TILE))

        x_rows = _dispatch(dest3, h2, jnp.zeros((rows * ROW_SLABS, LANES), jnp.uint32))
        y_rows = _experts(blk_e, n_valid, x_rows, w_gate_e[l], w_up_e[l], w_down_e[l])
        xc = _combine(dest3, x1, wts, mod3, final_g.reshape(1, d), y_rows, seq)
    return xc.reshape(bsz, seq, d)
```

```python
import functools
import math

import jax
import jax.numpy as jnp
import numpy as np
from jax import lax
from jax.experimental import pallas as pl
from jax.experimental.pallas import tpu as pltpu

F32 = jnp.float32
BF16 = jnp.bfloat16

D_MODEL = 1024
N_MOD = 6
RMS_EPS = 1e-6
FOX_HEADS = 8
FOX_HEAD_DIM = 64
FOX_WIDTH = FOX_HEADS * FOX_HEAD_DIM
HEAD_PAIRS = FOX_HEADS // 2
SSM_WIDTH = 512
SSM_GROUP = 16
SSM_GROUPS = SSM_WIDTH // SSM_GROUP
SSM_STATE = 64
LAMBDA_RE_MAX = -1e-4
N_GROUPS = 4
EXPERTS_PER_GROUP = 8
N_EXPERTS = N_GROUPS * EXPERTS_PER_GROUP
D_EXPERT = 512

LANES = 128
SUBLANES = 8
VMEM_LIMIT = 56 * 1024 * 1024

SSM_CHUNK = 16
TOK_TILE = 512
MIX_TILE = 256
ATT_Q_TILE = 512
ATT_K_TILE = 512
ROW_BLOCK = 512
MOVE_TILE = 512
NEG_BIG = -1e30

HIGHEST = lax.Precision.HIGHEST


def _params(sem):
    return pltpu.CompilerParams(dimension_semantics=sem, vmem_limit_bytes=VMEM_LIMIT)


def _sigmoid(x):
    return 0.5 * jnp.tanh(0.5 * x) + 0.5


def _rms_modulate(x, gain, shift, scale):
    ms = jnp.mean(x * x, axis=-1, keepdims=True)
    return (x * lax.rsqrt(ms + RMS_EPS)) * gain * (1.0 + scale) + shift


def _mod_kernel(c_ref, w_ref, b_ref, o_ref):
    c = c_ref[...]
    ca = (c * jax.nn.sigmoid(c)).astype(BF16)
    o_ref[...] = jnp.dot(ca, w_ref[...].astype(BF16), preferred_element_type=F32) + b_ref[...]


def _mod(c, w_ada, b_ada):
    bsz, d = c.shape
    cols = w_ada.shape[1]
    tn = 1536
    return pl.pallas_call(
        _mod_kernel,
        grid=(cols // tn,),
        in_specs=[pl.BlockSpec((bsz, d), lambda j: (0, 0)),
                  pl.BlockSpec((d, tn), lambda j: (0, j)),
                  pl.BlockSpec((1, tn), lambda j: (0, j))],
        out_specs=pl.BlockSpec((bsz, tn), lambda j: (0, j)),
        out_shape=jax.ShapeDtypeStruct((bsz, cols), F32),
        compiler_params=_params(("arbitrary",)),
        name="mod",
    )(c, w_ada, b_ada.reshape(1, cols))


_C_Q, _C_K, _C_U, _C_GA, _C_GB, _C_F, _C_END = 0, 512, 1024, 1536, 2560, 3584, 3712


def _lane_block():
    return lax.broadcasted_iota(jnp.int32, (1, LANES), 1) // SSM_GROUP


def _to_group_major(tok_ref, flat_ref, rows):
    blk = _lane_block()
    for half in range(2):
        for j in range(SSM_WIDTH // LANES):
            w = []
            for s8 in range(8):
                v = tok_ref[j, pl.ds(8 * half + s8, rows, stride=SSM_CHUNK), :]
                w.append(pltpu.roll(v, s8 * SSM_GROUP, axis=1) if s8 else v)
            for p in range(8):
                acc = w[0]
                for s8 in range(1, 8):
                    acc = jnp.where(blk == (p + s8) % 8, w[s8], acc)
                flat_ref[8 * j + p, :, half * LANES:(half + 1) * LANES] = acc.astype(flat_ref.dtype)


def _to_token_major(flat_ref, tok_ref, rows):
    blk = _lane_block()
    for half in range(2):
        for j in range(SSM_WIDTH // LANES):
            ys = [flat_ref[8 * j + p, :, half * LANES:(half + 1) * LANES] for p in range(8)]
            for s8 in range(8):
                acc = ys[0]
                for p in range(1, 8):
                    acc = jnp.where(blk == (p + s8) % 8, ys[p], acc)
                if s8:
                    acc = pltpu.roll(acc, LANES - s8 * SSM_GROUP, axis=1)
                tok_ref[j, pl.ds(8 * half + s8, rows, stride=SSM_CHUNK), :] = acc


def _bias_lane_placement():
    pq = np.zeros((3 * LANES, LANES), np.float32)
    pk = np.zeros((3 * LANES, LANES), np.float32)
    bq = np.zeros((1, LANES), np.float32)
    bk = np.zeros((1, LANES), np.float32)
    for head in range(FOX_HEADS):
        base = head * 8
        for term in range(3):
            pq[term * LANES + head, base + term] = 1.0
            pk[term * LANES + head, base + 3 + term] = -1.0
            bq[0, base + 3 + term] = 1.0
            bk[0, base + term] = 1.0
    return pq, pk, bq, bk


def _top_bits(a):
    bits = lax.bitcast_convert_type(a, jnp.uint32) & jnp.uint32(0xFFFF0000)
    return lax.bitcast_convert_type(bits, F32)


def _inproj_kernel(tiles_per_batch, x_ref, mod_ref, g_ref, w_ref, wvt_ref, bf_ref, tri_ref,
                   pq_ref, pk_ref, bq_ref, bk_ref,
                   q_ref, k_ref, vt_ref, u_ref, uflat_ref, ga_ref, gb_ref, carry_ref, uslab_ref):
    i = pl.program_id(0)
    h = _rms_modulate(x_ref[...], g_ref[...], mod_ref[0:1, :], mod_ref[1:2, :])
    hb = h.astype(BF16)

    def proj(a, b):
        return jnp.dot(hb, w_ref[:, a:b], preferred_element_type=F32)

    q = proj(_C_Q, _C_K).astype(BF16)
    k = proj(_C_K, _C_U).astype(BF16)
    vt_ref[...] = lax.dot_general(wvt_ref[...], hb, (((1,), (1,)), ((), ())),
                                  preferred_element_type=F32).astype(BF16)
    u = proj(_C_U, _C_GA)
    u_ref[...] = u
    for j in range(SSM_WIDTH // LANES):
        uslab_ref[j] = u[:, j * LANES:(j + 1) * LANES]
    _to_group_major(uslab_ref, uflat_ref, u.shape[0] // SSM_CHUNK)
    ga_ref[...] = _sigmoid(proj(_C_GA, _C_GB)).astype(BF16)
    gb_ref[...] = _sigmoid(proj(_C_GB, _C_F)).astype(BF16)

    f = proj(_C_F, _C_END) + bf_ref[...]
    logf = jnp.minimum(f, 0.0) - jnp.log(1.0 + jnp.exp(-jnp.abs(f)))

    @pl.when(i % tiles_per_batch == 0)
    def _():
        carry_ref[...] = jnp.zeros_like(carry_ref)

    def split3(a):
        hi = _top_bits(a)
        r1 = a - hi
        mid = _top_bits(r1)
        return jnp.concatenate([hi, mid, _top_bits(r1 - mid)], axis=1).astype(BF16)

    part = jnp.dot(tri_ref[...], split3(logf), preferred_element_type=F32)
    cs = (part[:, :LANES] + part[:, LANES:2 * LANES] + part[:, 2 * LANES:]) + carry_ref[0:1, :]
    carry_ref[...] = jnp.broadcast_to(cs[-1:, :], carry_ref.shape)

    terms = split3(cs)
    bias_q = (jnp.dot(terms, pq_ref[...], preferred_element_type=F32) + bq_ref[...]).astype(BF16)
    bias_k = (jnp.dot(terms, pk_ref[...], preferred_element_type=F32) + bk_ref[...]).astype(BF16)
    for p in range(HEAD_PAIRS):
        lanes = slice(p * LANES, (p + 1) * LANES)
        q_ref[:, 2 * p * LANES:(2 * p + 1) * LANES] = q[:, lanes]
        q_ref[:, (2 * p + 1) * LANES:(2 * p + 2) * LANES] = bias_q
        k_ref[:, 2 * p * LANES:(2 * p + 1) * LANES] = k[:, lanes]
        k_ref[:, (2 * p + 1) * LANES:(2 * p + 2) * LANES] = bias_k


def _inproj(x2, mod3, gain, w_all, w_vt, bf_pad, seq):
    n, d = x2.shape
    tm = TOK_TILE
    tpb = seq // tm
    tri = jnp.tril(jnp.ones((tm, tm), BF16))
    pq, pk, bq, bk = _bias_lane_placement()
    tok = lambda w: pl.BlockSpec((tm, w), lambda i: (i, 0))
    const = lambda shape: pl.BlockSpec(shape, lambda i: (0,) * len(shape))
    qk_width = 2 * FOX_WIDTH
    return pl.pallas_call(
        functools.partial(_inproj_kernel, tpb),
        grid=(n // tm,),
        in_specs=[tok(d),
                  pl.BlockSpec((None, N_MOD, d), lambda i: (i // tpb, 0, 0)),
                  const((1, d)), const((d, _C_END)), const((FOX_WIDTH, d)), const((1, LANES)),
                  const((tm, tm)), const(pq.shape), const(pk.shape), const(bq.shape), const(bk.shape)],
        out_specs=[tok(qk_width), tok(qk_width), pl.BlockSpec((FOX_WIDTH, tm), lambda i: (0, i)),
                   tok(SSM_WIDTH),
                   pl.BlockSpec((SSM_GROUPS, tm // SSM_CHUNK, SSM_CHUNK * SSM_GROUP), lambda i: (0, i, 0)),
                   tok(d), tok(d)],
        out_shape=[jax.ShapeDtypeStruct((n, qk_width), BF16)] * 2
        + [jax.ShapeDtypeStruct((FOX_WIDTH, n), BF16)]
        + [jax.ShapeDtypeStruct((n, SSM_WIDTH), F32)]
        + [jax.ShapeDtypeStruct((SSM_GROUPS, n // SSM_CHUNK, SSM_CHUNK * SSM_GROUP), BF16)]
        + [jax.ShapeDtypeStruct((n, d), BF16)] * 2,
        scratch_shapes=[pltpu.VMEM((SUBLANES, LANES), F32),
                        pltpu.VMEM((SSM_WIDTH // LANES, tm, LANES), F32)],
        compiler_params=_params(("arbitrary",)),
        name="inproj",
    )(x2, mod3, gain, w_all, w_vt, bf_pad, tri, jnp.asarray(pq, BF16), jnp.asarray(pk, BF16),
      jnp.asarray(bq), jnp.asarray(bk))


def _attn_kernel(q_ref, k_ref, vt_ref, o_ref, m_ref, acc_ref, sa_ref, sb_ref):
    tq, tk = ATT_Q_TILE, ATT_K_TILE
    seq = q_ref.shape[0]
    nq = seq // tq
    half = FOX_HEAD_DIM
    lane = lax.broadcasted_iota(jnp.int32, (1, 2 * LANES), 1)
    bias0 = LANES + 16 * pl.program_id(1)
    own0 = (lane < half) | ((lane >= bias0) & (lane < bias0 + 8))
    own1 = ((lane >= half) & (lane < LANES)) | ((lane >= bias0 + 8) & (lane < bias0 + 16))
    ones_rows = jnp.ones((2 * SUBLANES, tk), BF16)
    key_in_tile = lax.broadcasted_iota(jnp.int32, (tk, 2 * tq), 0)
    qry_in_tile = lax.broadcasted_iota(jnp.int32, (tk, 2 * tq), 1) & (tq - 1)
    bufs = (sa_ref, sb_ref)
    key_tiles = [-(-(i + 1) * tq // tk) for i in range(nq)]
    steps = [(i, j) for i in range(nq) for j in range(key_tiles[i])]
    q_cache = {}

    def q_both(i):
        if i not in q_cache:
            q = q_ref[i * tq:(i + 1) * tq, :]
            zq = jnp.zeros_like(q)
            q_cache[i] = jnp.concatenate([jnp.where(own0, q, zq), jnp.where(own1, q, zq)], axis=0)
        return q_cache[i]

    def scores(n):
        i, j = steps[n]
        s = lax.dot_general(k_ref[j * tk:(j + 1) * tk, :], q_both(i), (((1,), (1,)), ((), ())),
                            preferred_element_type=F32)
        if j * tk + tk - 1 > i * tq:
            s = jnp.where(key_in_tile + (j * tk - i * tq) <= qry_in_tile, s, NEG_BIG)
        bufs[n % 2][...] = s

    scores(0)
    for n, (i, j) in enumerate(steps):
        if n + 1 < len(steps):
            scores(n + 1)
        s_ref = bufs[n % 2]
        va = jnp.concatenate([vt_ref[:, j * tk:(j + 1) * tk], ones_rows], axis=0)
        if j == 0:
            m_new = jnp.max(s_ref[...], axis=0, keepdims=True)
            p = jnp.exp(s_ref[...] - m_new).astype(BF16)
            acc_ref[...] = jnp.dot(va, p, preferred_element_type=F32)
        else:
            m_old = m_ref[...]
            m_new = jnp.maximum(m_old, jnp.max(s_ref[...], axis=0, keepdims=True))
            alpha = jnp.exp(m_old - m_new)
            p = jnp.exp(s_ref[...] - m_new).astype(BF16)
            acc_ref[...] = alpha * acc_ref[...] + jnp.dot(va, p, preferred_element_type=F32)
        m_ref[...] = m_new
        if j == key_tiles[i] - 1:
            acc = acc_ref[...]
            o_t = jnp.concatenate([acc[0:half, 0:tq] / acc[LANES:LANES + 1, 0:tq],
                                   acc[half:LANES, tq:2 * tq] / acc[LANES:LANES + 1, tq:2 * tq]], axis=0)
            o_ref[i * tq:(i + 1) * tq, :] = o_t.T.astype(o_ref.dtype)


def _attention(q, k, v_t, bsz, seq):
    n = q.shape[0]
    t = ATT_Q_TILE
    return pl.pallas_call(
        _attn_kernel,
        grid=(bsz, HEAD_PAIRS),
        in_specs=[pl.BlockSpec((seq, 2 * LANES), lambda b, p: (b, p)),
                  pl.BlockSpec((seq, 2 * LANES), lambda b, p: (b, p)),
                  pl.BlockSpec((LANES, seq), lambda b, p: (p, b))],
        out_specs=pl.BlockSpec((seq, LANES), lambda b, p: (b, p)),
        out_shape=jax.ShapeDtypeStruct((n, FOX_WIDTH), BF16),
        scratch_shapes=[pltpu.VMEM((1, 2 * t), F32), pltpu.VMEM((LANES + 2 * SUBLANES, 2 * t), F32),
                        pltpu.VMEM((ATT_K_TILE, 2 * t), F32), pltpu.VMEM((ATT_K_TILE, 2 * t), F32)],
        compiler_params=_params(("arbitrary", "arbitrary")),
        name="attn",
    )(q, k, v_t)


def _ssm_prep_kernel(lrow_ref, lcol_ref, ldt_ref, btr_ref, bti_ref, ctr_ref, cti_ref,
                     toep_ref, bst_ref, bsw_ref, cpw_ref, a_ref):
    p8 = pl.program_id(0) % 8
    t_len, grp = SSM_CHUNK, SSM_GROUP
    dt = jnp.exp(ldt_ref[...])
    lr, li = jnp.minimum(lrow_ref[0:1, :], LAMBDA_RE_MAX), lrow_ref[1:2, :]

    def powers(steps, re, im):
        mag = jnp.exp(steps * (re * dt))
        return mag * jnp.cos(steps * (im * dt)), mag * jnp.sin(steps * (im * dt))

    a_re, a_im = powers(1.0, lr, li)
    den = lr * lr + li * li
    nr = a_re - 1.0
    co_re = (nr * lr + a_im * li) / den
    co_im = (a_im * lr - nr * li) / den
    bbt_re = co_re * btr_ref[...] - co_im * bti_ref[...]
    bbt_im = co_re * bti_ref[...] + co_im * btr_ref[...]

    lag = (lax.broadcasted_iota(jnp.int32, (1, t_len * grp), 1) // grp).astype(F32)
    lcr, lci = jnp.minimum(lcol_ref[:, 0:1], LAMBDA_RE_MAX), lcol_ref[:, 1:2]

    def c_times_power(steps):
        p_re, p_im = powers(steps, lcr, lci)
        return (ctr_ref[...] * p_re - cti_ref[...] * p_im, ctr_ref[...] * p_im + cti_ref[...] * p_re)

    wt_re, wt_im = c_times_power(lag)
    kern = (jnp.dot(bbt_re, wt_re, precision=HIGHEST, preferred_element_type=F32)
            - jnp.dot(bbt_im, wt_im, precision=HIGHEST, preferred_element_type=F32))

    lane = lax.broadcasted_iota(jnp.int32, (1, LANES), 1)
    col_shift = p8 * grp

    def store_cols(ref, rows, lo_half, hi_half):
        ref[rows, 0:LANES] = pltpu.roll(lo_half, col_shift, axis=1).astype(ref.dtype)
        ref[rows, LANES:2 * LANES] = pltpu.roll(hi_half, col_shift, axis=1).astype(ref.dtype)

    def slot_rows(s):
        half, s8 = divmod(s, 8)
        return pl.ds(pl.multiple_of((8 * half + (s8 + p8) % 8) * grp, grp), grp)

    back = (t_len - 1 - lax.broadcasted_iota(jnp.int32, (t_len, 1), 0)).astype(F32)
    e_re, e_im = powers(back, lr, li)
    zero = jnp.zeros((grp, LANES), F32)
    k_lo, k_hi = kern[:, 0:LANES], kern[:, LANES:2 * LANES]
    for s in range(t_len):
        half, s8 = divmod(s, 8)
        keep = lane >= s8 * grp
        r_lo = pltpu.roll(k_lo, s8 * grp, axis=1) if s8 else k_lo
        r_hi = pltpu.roll(k_hi, s8 * grp, axis=1) if s8 else k_hi
        if half == 0:
            lo, hi = jnp.where(keep, r_lo, 0.0), jnp.where(keep, r_hi, r_lo)
        else:
            lo, hi = zero, jnp.where(keep, r_lo, 0.0)
        store_cols(toep_ref, slot_rows(s), lo, hi)
        es_re, es_im = e_re[s:s + 1, :], e_im[s:s + 1, :]
        bs_re = es_re * bbt_re - es_im * bbt_im
        bs_im = es_re * bbt_im + es_im * bbt_re
        bst_ref[slot_rows(s), :] = jnp.concatenate([bs_re, bs_im], axis=1).astype(bst_ref.dtype)
        bsw_ref[slot_rows(s), :] = jnp.concatenate([bs_im, bs_re], axis=1).astype(bsw_ref.dtype)

    w1_re, w1_im = c_times_power(lag + 1.0)
    store_cols(cpw_ref, pl.ds(0, SSM_STATE), w1_re[:, 0:LANES], w1_re[:, LANES:2 * LANES])
    store_cols(cpw_ref, pl.ds(SSM_STATE, SSM_STATE), -w1_im[:, 0:LANES], -w1_im[:, LANES:2 * LANES])
    s_re, s_im = powers(float(t_len), lr, li)
    a_ref[0:1, :] = jnp.concatenate([s_re, s_re], axis=1)
    a_ref[1:2, :] = jnp.concatenate([-s_im, s_im], axis=1)


def _ssm_prep(lambda_re, lambda_im, log_dt, b_re, b_im, c_re, c_im):
    width = SSM_CHUNK * SSM_GROUP
    lam_row = jnp.stack([lambda_re, lambda_im], axis=1)
    tiled = lambda c: jnp.tile(c.transpose(0, 2, 1), (1, 1, SSM_CHUNK))
    per = lambda a, b: pl.BlockSpec((None, a, b), lambda g: (g, 0, 0))
    return pl.pallas_call(
        _ssm_prep_kernel,
        grid=(SSM_GROUPS,),
        in_specs=[per(2, SSM_STATE), per(SSM_STATE, 2), per(1, 1), per(SSM_GROUP, SSM_STATE),
                  per(SSM_GROUP, SSM_STATE), per(SSM_STATE, width), per(SSM_STATE, width)],
        out_specs=[per(width, width), per(width, 2 * SSM_STATE), per(width, 2 * SSM_STATE),
                   per(2 * SSM_STATE, width), per(2, 2 * SSM_STATE)],
        out_shape=[jax.ShapeDtypeStruct((SSM_GROUPS, width, width), BF16),
                   jax.ShapeDtypeStruct((SSM_GROUPS, width, 2 * SSM_STATE), BF16),
                   jax.ShapeDtypeStruct((SSM_GROUPS, width, 2 * SSM_STATE), BF16),
                   jax.ShapeDtypeStruct((SSM_GROUPS, 2 * SSM_STATE, width), BF16),
                   jax.ShapeDtypeStruct((SSM_GROUPS, 2, 2 * SSM_STATE), F32)],
        compiler_params=_params(("arbitrary",)),
        name="ssm_prep",
    )(lam_row, lam_row.transpose(0, 2, 1), log_dt.reshape(SSM_GROUPS, 1, 1),
      b_re.transpose(0, 2, 1), b_im.transpose(0, 2, 1), tiled(c_re), tiled(c_im))


SSM_GROUPS_PER_STEP = 4


def _ssm_kernel(n_chunks, bsz, u_ref, toep_ref, bst_ref, bsw_ref, cpw_ref, a_ref, y_ref,
                contrib_ref, cswap_ref, xprev_ref):
    groups = u_ref.shape[0]
    for k in range(groups):
        u = u_ref[k]
        contrib_ref[k] = jnp.dot(u, bst_ref[k], preferred_element_type=F32)
        cswap_ref[k] = jnp.dot(u, bsw_ref[k], preferred_element_type=F32)
    a1 = [a_ref[k, 0:1, :] for k in range(groups)]
    a2 = [a_ref[k, 1:2, :] for k in range(groups)]

    def step(n, carry):
        rows = pl.ds(n, bsz, stride=n_chunks)
        new = []
        for k in range(groups):
            x, xs = carry[2 * k], carry[2 * k + 1]
            xprev_ref[k, rows, :] = x
            new.append(a1[k] * x + a2[k] * xs + contrib_ref[k, rows, :])
            new.append(a1[k] * xs - a2[k] * x + cswap_ref[k, rows, :])
        return tuple(new)

    zero = jnp.zeros((bsz, 2 * SSM_STATE), F32)
    lax.fori_loop(0, n_chunks, step, (zero,) * (2 * groups), unroll=2)
    for k in range(groups):
        y_ref[k] = (jnp.dot(u_ref[k], toep_ref[k], preferred_element_type=F32)
                    + jnp.dot(xprev_ref[k].astype(BF16), cpw_ref[k], preferred_element_type=F32))


def _ssm(u_flat, toep, b_state, b_swap, c_pow, a_step, bsz):
    g, rows, w = u_flat.shape
    gb = SSM_GROUPS_PER_STEP
    per = lambda a, b: pl.BlockSpec((gb, a, b), lambda i: (i, 0, 0))
    state = pltpu.VMEM((gb, rows, 2 * SSM_STATE), F32)
    return pl.pallas_call(
        functools.partial(_ssm_kernel, rows // bsz, bsz),
        grid=(g // gb,),
        in_specs=[per(rows, w), per(w, w), per(w, 2 * SSM_STATE), per(w, 2 * SSM_STATE),
                  per(2 * SSM_STATE, w), per(2, 2 * SSM_STATE)],
        out_specs=per(rows, w),
        out_shape=jax.ShapeDtypeStruct((g, rows, w), F32),
        scratch_shapes=[state, state, state],
        compiler_params=_params(("arbitrary",)),
        name="ssm",
    )(u_flat, toep, b_state, b_swap, c_pow, a_step)


ROW_SLABS = D_MODEL // LANES // 2
_HIGH_HALF = 0xFFFF0000


def _store_row_tiles(ref, value):
    rows = value.shape[0]
    bits = lax.bitcast_convert_type(value.astype(BF16).astype(F32), jnp.uint32)
    for j in range(ROW_SLABS):
        low = bits[:, j * LANES:(j + 1) * LANES] >> 16
        high = bits[:, (j + ROW_SLABS) * LANES:(j + ROW_SLABS + 1) * LANES] & jnp.uint32(_HIGH_HALF)
        ref[pl.ds(j, rows, stride=ROW_SLABS), :] = high | low


def _load_row_tiles(ref, rows):
    words = [ref[pl.ds(j, rows, stride=ROW_SLABS), :] for j in range(ROW_SLABS)]
    low = [lax.bitcast_convert_type(w << 16, F32) for w in words]
    high = [lax.bitcast_convert_type(w & jnp.uint32(_HIGH_HALF), F32) for w in words]
    return jnp.concatenate(low + high, axis=1)


def _row_tile_copy(src_ref, src_row, dst_ref, dst_row, sem):
    src = src_ref.at[pl.ds(pl.multiple_of(src_row * ROW_SLABS, ROW_SLABS), ROW_SLABS), :]
    dst = dst_ref.at[pl.ds(pl.multiple_of(dst_row * ROW_SLABS, ROW_SLABS), ROW_SLABS), :]
    return pltpu.make_async_copy(src, dst, sem)


def _mix_kernel(x_ref, of_ref, yf_ref, u_ref, ga_ref, gb_ref, mod_ref, dsk_ref, wglu_ref, wfox_ref,
                wssm_ref, wo_ref, g2_ref, wr_ref, br_ref, x1_ref, h2_ref, lg_ref, ytok_ref):
    _to_token_major(yf_ref, ytok_ref, yf_ref.shape[1])
    y_ssm = jnp.concatenate([ytok_ref[j] for j in range(SSM_WIDTH // LANES)], axis=1)
    y = y_ssm + dsk_ref[...] * u_ref[...]
    y = 0.5 * y * (1.0 + jnp.tanh(math.sqrt(2.0 / math.pi) * (y + 0.044715 * (y * y * y))))
    gl = jnp.dot(y.astype(BF16), wglu_ref[...], preferred_element_type=F32)
    o_ssm = gl[:, :SSM_WIDTH] * _sigmoid(gl[:, SSM_WIDTH:])
    merged = (ga_ref[...].astype(F32) * jnp.dot(of_ref[...], wfox_ref[...], preferred_element_type=F32)
              + gb_ref[...].astype(F32) * jnp.dot(o_ssm.astype(BF16), wssm_ref[...],
                                                  preferred_element_type=F32))
    x1 = x_ref[...] + mod_ref[2:3, :] * jnp.dot(merged.astype(BF16), wo_ref[...],
                                                 preferred_element_type=F32)
    x1_ref[...] = x1
    h2 = _rms_modulate(x1, g2_ref[...], mod_ref[3:4, :], mod_ref[4:5, :])
    _store_row_tiles(h2_ref, h2)
    a1 = _top_bits(h2)
    a2 = _top_bits(h2 - a1)
    lhs = jnp.concatenate([a1, a1, a2], axis=1).astype(BF16)
    lg_ref[...] = jnp.dot(lhs, wr_ref[...], preferred_element_type=F32) + br_ref[...]


def _mix(x2, o_fox, y_flat, u, sga, sgb, mod3, d_skip, w_glu, w_fox, w_ssm, w_o, g2, w_r, b_r, seq):
    n, d = x2.shape
    tm = MIX_TILE
    tpb = seq // tm
    tok = lambda w: pl.BlockSpec((tm, w), lambda i: (i, 0))
    const = lambda a: pl.BlockSpec(a.shape, lambda i: (0,) * a.ndim)
    flat = pl.BlockSpec((SSM_GROUPS, tm // SSM_CHUNK, SSM_CHUNK * SSM_GROUP), lambda i: (0, i, 0))
    return pl.pallas_call(
        _mix_kernel,
        grid=(n // tm,),
        in_specs=[tok(d), tok(FOX_WIDTH), flat, tok(SSM_WIDTH), tok(d), tok(d),
                  pl.BlockSpec((None, N_MOD, d), lambda i: (i // tpb, 0, 0)),
                  const(d_skip), const(w_glu), const(w_fox), const(w_ssm), const(w_o), const(g2),
                  const(w_r), const(b_r)],
        out_specs=[tok(d), pl.BlockSpec((tm * ROW_SLABS, LANES), lambda i: (i, 0)), tok(LANES)],
        out_shape=[jax.ShapeDtypeStruct((n, d), F32), jax.ShapeDtypeStruct((n * ROW_SLABS, LANES), jnp.uint32),
                   jax.ShapeDtypeStruct((n, LANES), F32)],
        scratch_shapes=[pltpu.VMEM((SSM_WIDTH // LANES, tm, LANES), F32)],
        compiler_params=_params(("arbitrary",)),
        name="mix",
    )(x2, o_fox, y_flat, u, sga, sgb, mod3, d_skip, w_glu, w_fox, w_ssm, w_o, g2, w_r, b_r)


def _route_kernel(lg_ref, tri_ref, er_ref, wt_ref, cnt_ref, carry_ref):
    i = pl.program_id(0)

    @pl.when(i == 0)
    def _():
        carry_ref[...] = jnp.zeros_like(carry_ref)

    lg = lg_ref[...].T
    tm = lg.shape[1]
    row = lax.broadcasted_iota(jnp.int32, (LANES, tm), 0)
    neg = jnp.full_like(lg, -jnp.inf)

    def first_argmax(vals):
        mx = jnp.max(vals, axis=0, keepdims=True)
        ix = jnp.min(jnp.where(vals == mx, row, LANES), axis=0, keepdims=True)
        return mx, ix

    is_group = row < N_GROUPS
    g_max, gi = first_argmax(jnp.where(is_group, lg, neg))
    g_sum = jnp.sum(jnp.where(is_group, jnp.exp(lg - g_max), 0.0), axis=0, keepdims=True)
    p_group = 1.0 / g_sum
    lo = N_GROUPS + EXPERTS_PER_GROUP * gi
    in_group = (row >= lo) & (row < lo + EXPERTS_PER_GROUP)
    cand = jnp.where(in_group, lg, neg)
    v1, i1 = first_argmax(cand)
    v2, i2 = first_argmax(jnp.where(row == i1, neg, cand))
    tt = jnp.exp(v2 - v1)
    w1 = p_group / (1.0 + tt)
    w2 = p_group * tt / (1.0 + tt)
    e1 = i1 - N_GROUPS
    e2 = i2 - N_GROUPS
    sel1 = row == e1
    sel2 = row == e2
    onehot = (sel1 | sel2).astype(F32)
    before = jnp.dot(onehot.astype(BF16), tri_ref[...], preferred_element_type=F32) + carry_ref[:, 0:1]
    r1 = jnp.sum(jnp.where(sel1, before, 0.0), axis=0, keepdims=True).astype(jnp.int32)
    r2 = jnp.sum(jnp.where(sel2, before, 0.0), axis=0, keepdims=True).astype(jnp.int32)
    total = before[:, tm - 1:tm] + onehot[:, tm - 1:tm]
    carry_ref[...] = jnp.broadcast_to(total, carry_ref.shape)
    cnt_ref[...] = jnp.broadcast_to(total, cnt_ref.shape)
    slot = lax.broadcasted_iota(jnp.int32, (SUBLANES, tm), 0)
    er_ref[...] = jnp.where(slot == 0, e1, jnp.where(slot == 1, e2, jnp.where(slot == 2, r1, r2)))
    wt_ref[...] = jnp.where(row == 0, w1, jnp.where(row == 1, w2, 0.0)).T


def _route(logits):
    n = logits.shape[0]
    tm = TOK_TILE
    tri = jnp.triu(jnp.ones((tm, tm), BF16), k=1)
    tok = pl.BlockSpec((tm, LANES), lambda i: (i, 0))
    return pl.pallas_call(
        _route_kernel,
        grid=(n // tm,),
        in_specs=[tok, pl.BlockSpec((tm, tm), lambda i: (0, 0))],
        out_specs=[pl.BlockSpec((SUBLANES, tm), lambda i: (0, i)), tok,
                   pl.BlockSpec((LANES, LANES), lambda i: (0, 0))],
        out_shape=[jax.ShapeDtypeStruct((SUBLANES, n), jnp.int32), jax.ShapeDtypeStruct((n, LANES), F32),
                   jax.ShapeDtypeStruct((LANES, LANES), F32)],
        scratch_shapes=[pltpu.VMEM((LANES, LANES), F32)],
        compiler_params=_params(("arbitrary",)),
        name="route",
    )(logits, tri)


ISSUE_UNROLL = 8


def _dispatch_kernel(dest_ref, h_ref, rows_in_ref, rows_ref, sem):
    del rows_in_ref
    tm = h_ref.shape[0] // ROW_SLABS

    def issue(g, c):
        for j in range(ISSUE_UNROLL):
            t = g * ISSUE_UNROLL + j
            _row_tile_copy(h_ref, t, rows_ref, dest_ref[0, 0, t], sem).start(priority=0)
            _row_tile_copy(h_ref, t, rows_ref, dest_ref[0, 0, tm + t], sem).start(priority=1)
        return c

    lax.fori_loop(0, tm // ISSUE_UNROLL, issue, 0)
    for _ in range(2):
        pltpu.make_async_copy(h_ref, rows_ref.at[pl.ds(0, tm * ROW_SLABS), :], sem).wait()


def _dispatch(dest3, h2_tiles, rows_zero):
    tm = MOVE_TILE
    n = h2_tiles.shape[0] // ROW_SLABS
    return pl.pallas_call(
        _dispatch_kernel,
        grid=(n // tm,),
        in_specs=[pl.BlockSpec((1, 1, 2 * tm), lambda i: (i, 0, 0), memory_space=pltpu.SMEM),
                  pl.BlockSpec((tm * ROW_SLABS, LANES), lambda i: (i, 0)),
                  pl.BlockSpec(memory_space=pl.ANY)],
        out_specs=pl.BlockSpec(memory_space=pl.ANY),
        out_shape=jax.ShapeDtypeStruct(rows_zero.shape, rows_zero.dtype),
        scratch_shapes=[pltpu.SemaphoreType.DMA(())],
        input_output_aliases={2: 0},
        compiler_params=_params(("arbitrary",)),
        name="dispatch",
    )(dest3, h2_tiles, rows_zero)


def _combine_kernel(n_steps, dest_ref, dnext_ref, x1_ref, wt_ref, mod_ref, gf_ref, yr_ref, o_ref,
                    buf_ref, sem):
    i = pl.program_id(0)
    tm = x1_ref.shape[0]

    def gather(idx_ref, which):
        def issue(g, c):
            for j in range(ISSUE_UNROLL):
                t = g * ISSUE_UNROLL + j
                _row_tile_copy(yr_ref, idx_ref[0, 0, t], buf_ref.at[which, 0], t,
                               sem.at[which]).start(priority=0)
                _row_tile_copy(yr_ref, idx_ref[0, 0, tm + t], buf_ref.at[which, 1], t,
                               sem.at[which]).start(priority=1)
            return c

        lax.fori_loop(0, tm // ISSUE_UNROLL, issue, 0)

    cur = i % 2

    @pl.when(i == 0)
    def _():
        gather(dest_ref, 0)

    @pl.when(i + 1 < n_steps)
    def _():
        gather(dnext_ref, 1 - cur)

    for slot in range(2):
        pltpu.make_async_copy(yr_ref.at[pl.ds(0, tm * ROW_SLABS), :], buf_ref.at[cur, slot],
                              sem.at[cur]).wait()
    wt = wt_ref[...]
    moe = (wt[:, 0:1] * _load_row_tiles(buf_ref.at[cur, 0], tm)
           + wt[:, 1:2] * _load_row_tiles(buf_ref.at[cur, 1], tm))
    x = x1_ref[...] + mod_ref[5:6, :] * moe
    ms = jnp.mean(x * x, axis=-1, keepdims=True)
    o_ref[...] = (x * lax.rsqrt(ms + RMS_EPS)) * gf_ref[...]


def _combine(dest3, x1, wts, mod3, final_g, y_rows, seq):
    n, d = x1.shape
    tm = MOVE_TILE
    tpb = seq // tm
    n_steps = n // tm
    idx_spec = lambda f: pl.BlockSpec((1, 1, 2 * tm), f, memory_space=pltpu.SMEM)
    return pl.pallas_call(
        functools.partial(_combine_kernel, n_steps),
        grid=(n_steps,),
        in_specs=[idx_spec(lambda i: (i, 0, 0)),
                  idx_spec(lambda i: (jnp.minimum(i + 1, n_steps - 1), 0, 0)),
                  pl.BlockSpec((tm, d), lambda i: (i, 0)),
                  pl.BlockSpec((tm, LANES), lambda i: (i, 0)),
                  pl.BlockSpec((None, N_MOD, d), lambda i: (i // tpb, 0, 0)),
                  pl.BlockSpec((1, d), lambda i: (0, 0)),
                  pl.BlockSpec(memory_space=pl.ANY)],
        out_specs=pl.BlockSpec((tm, d), lambda i: (i, 0)),
        out_shape=jax.ShapeDtypeStruct((n, d), F32),
        scratch_shapes=[pltpu.VMEM((2, 2, tm * ROW_SLABS, LANES), jnp.uint32),
                        pltpu.SemaphoreType.DMA((2,))],
        compiler_params=_params(("arbitrary",)),
        name="combine",
    )(dest3, dest3, x1, wts, mod3, final_g, y_rows)


def _expert_kernel(be_ref, seg_ref, nxt_ref, nv_ref, x_ref, wg_hbm, wu_hbm, wd_hbm, y_ref,
                   wg_buf, wu_buf, wd_buf, wgb_ref, wub_ref, wdb_ref, sem):
    i = pl.program_id(0)
    valid = i < nv_ref[0]
    first = (i == 0) | (be_ref[i] != be_ref[jnp.maximum(i - 1, 0)])
    slot = seg_ref[i] % 2

    def weight_copies(e, s):
        return [pltpu.make_async_copy(hbm.at[e], buf.at[s], sem.at[s])
                for hbm, buf in ((wg_hbm, wg_buf), (wu_hbm, wu_buf), (wd_hbm, wd_buf))]

    @pl.when(valid & (i == 0))
    def _():
        for c in weight_copies(be_ref[0], 0):
            c.start()

    @pl.when(valid & first)
    def _():
        for c in weight_copies(be_ref[i], slot):
            c.wait()

        @pl.when(nxt_ref[i] >= 0)
        def _():
            for c in weight_copies(nxt_ref[i], 1 - slot):
                c.start()

        wgb_ref[...] = wg_buf[slot].astype(BF16)
        wub_ref[...] = wu_buf[slot].astype(BF16)
        wdb_ref[...] = wd_buf[slot].astype(BF16)

    @pl.when(valid)
    def _():
        xb = _load_row_tiles(x_ref, x_ref.shape[0] // ROW_SLABS).astype(BF16)
        a = jnp.dot(xb, wgb_ref[...], preferred_element_type=F32)
        b = jnp.dot(xb, wub_ref[...], preferred_element_type=F32)
        hid = (a * _sigmoid(a)) * b
        _store_row_tiles(y_ref, jnp.dot(hid.astype(BF16), wdb_ref[...], preferred_element_type=F32))

    @pl.when(jnp.logical_not(valid))
    def _():
        y_ref[...] = jnp.zeros_like(y_ref)


def _experts(blk_e, n_valid, x_rows, w_gate, w_up, w_down):
    d = D_MODEL
    rows = x_rows.shape[0] // ROW_SLABS
    tb = ROW_BLOCK
    n_blocks = rows // tb
    idx = jnp.arange(n_blocks, dtype=jnp.int32)
    change = (idx == 0) | (blk_e != jnp.roll(blk_e, 1))
    seg = jnp.cumsum(change.astype(jnp.int32)) - 1
    later_start = (idx[None, :] > idx[:, None]) & change[None, :] & (idx[None, :] < n_valid[0])
    none = jnp.int32(N_EXPERTS)
    nxt = jnp.min(jnp.where(later_start, blk_e[None, :], none), axis=1)
    nxt = jnp.where(nxt == none, -1, nxt)
    row_spec = pl.BlockSpec((tb * ROW_SLABS, LANES), lambda i, *_: (i, 0))
    grid_spec = pltpu.PrefetchScalarGridSpec(
        num_scalar_prefetch=4,
        grid=(n_blocks,),
        in_specs=[row_spec, pl.BlockSpec(memory_space=pl.ANY), pl.BlockSpec(memory_space=pl.ANY),
                  pl.BlockSpec(memory_space=pl.ANY)],
        out_specs=row_spec,
        scratch_shapes=[pltpu.VMEM((2, d, D_EXPERT), F32), pltpu.VMEM((2, d, D_EXPERT), F32),
                        pltpu.VMEM((2, D_EXPERT, d), F32),
                        pltpu.VMEM((d, D_EXPERT), BF16), pltpu.VMEM((d, D_EXPERT), BF16),
                        pltpu.VMEM((D_EXPERT, d), BF16), pltpu.SemaphoreType.DMA((2,))],
    )
    return pl.pallas_call(
        _expert_kernel,
        grid_spec=grid_spec,
        out_shape=jax.ShapeDtypeStruct(x_rows.shape, x_rows.dtype),
        compiler_params=_params(("arbitrary",)),
        name="experts",
    )(blk_e, seg, nxt.astype(jnp.int32), n_valid, x_rows, w_gate, w_up, w_down)


def kernel(x, c, w_ada, b_ada, norm_mix_g, w_in, b_forget, w_out_fox, lambda_re, lambda_im, log_dt,
           ssm_b_re, ssm_b_im, ssm_c_re, ssm_c_im, d_skip, w_glu, w_out_ssm, w_o, norm_ffn_g,
           w_router_group, b_router_group, w_router_expert, b_router_expert, w_gate_e, w_up_e,
           w_down_e, final_g):
    bsz, seq, d = x.shape
    n = bsz * seq
    assert w_ada.shape[0] == 1, "the final RMSNorm is fused into the (single) layer's combine kernel"
    xc = x.reshape(n, d)
    for l in range(1):
        mod3 = _mod(c, w_ada[l], b_ada[l]).reshape(bsz, N_MOD, d)

        wi = w_in[l]
        s_q, s_k, s_v, s_f, s_u, s_ga = 512, 1024, 1536, 1544, 2056, 3080
        scale = FOX_HEAD_DIM ** -0.5
        w_all = jnp.concatenate(
            [wi[:, :s_q] * scale, wi[:, s_q:s_k], wi[:, s_f:s_u], wi[:, s_u:s_ga],
             wi[:, s_ga:], jnp.pad(wi[:, s_v:s_f], ((0, 0), (0, LANES - FOX_HEADS)))],
            axis=1).astype(BF16)
        w_vt = wi[:, s_k:s_v].T.astype(BF16)
        bf_pad = jnp.pad(b_forget[l], (0, LANES - FOX_HEADS)).reshape(1, LANES)
        q, k, v_t, u, u_flat, sga, sgb = _inproj(xc, mod3, norm_mix_g[l].reshape(1, d), w_all, w_vt,
                                                 bf_pad, seq)

        o_fox = _attention(q, k, v_t, bsz, seq)

        toep, b_state, b_swap, c_pow, a_step = _ssm_prep(
            lambda_re[l], lambda_im[l], log_dt[l], ssm_b_re[l], ssm_b_im[l], ssm_c_re[l], ssm_c_im[l])
        y_flat = _ssm(u_flat, toep, b_state, b_swap, c_pow, a_step, bsz)

        w_r = jnp.pad(jnp.concatenate([w_router_group[l], w_router_expert[l]], axis=1),
                      ((0, 0), (0, LANES - N_GROUPS - N_EXPERTS)))
        w_r1 = _top_bits(w_r)
        w_r2 = _top_bits(w_r - w_r1)
        w_r = jnp.concatenate([w_r1, w_r2, w_r1], axis=0).astype(BF16)
        b_r = jnp.pad(jnp.concatenate([b_router_group[l], b_router_expert[l]]),
                      (0, LANES - N_GROUPS - N_EXPERTS)).reshape(1, LANES)
        x1, h2, logits = _mix(xc, o_fox, y_flat, u, sga, sgb, mod3, d_skip[l].reshape(1, SSM_WIDTH),
                              w_glu[l].astype(BF16), w_out_fox[l].astype(BF16),
                              w_out_ssm[l].astype(BF16), w_o[l].astype(BF16),
                              norm_ffn_g[l].reshape(1, d), w_r, b_r, seq)

        er, wts, cnt = _route(logits)
        counts = cnt[:N_EXPERTS, 0].astype(jnp.int32)
        pcounts = ((counts + ROW_BLOCK - 1) // ROW_BLOCK) * ROW_BLOCK
        pends = jnp.cumsum(pcounts)
        pstarts = pends - pcounts
        hit = er[0:2, None, :] == jnp.arange(N_EXPERTS, dtype=jnp.int32)[None, :, None]
        dest = jnp.sum(jnp.where(hit, pstarts[None, :, None], 0), axis=1) + er[2:4]
        rows = 2 * n + N_EXPERTS * ROW_BLOCK
        n_blocks = rows // ROW_BLOCK
        blk_start = jnp.arange(n_blocks, dtype=jnp.int32) * ROW_BLOCK
        blk_e = jnp.minimum(jnp.sum((pends[None, :] <= blk_start[:, None]).astype(jnp.int32), axis=1),
                            N_EXPERTS - 1)
        n_valid = (pends[-1:] // ROW_BLOCK).astype(jnp.int32)
        dest3 = (dest.astype(jnp.int32).reshape(2, n // MOVE_TILE, MOVE_TILE).transpose(1, 0, 2)
                 .reshape(n // MOVE_TILE, 1, 2 * MOVE_TILE))

        x_rows = _dispatch(dest3, h2, jnp.zeros((rows * ROW_SLABS, LANES), jnp.uint32))
        y_rows = _experts(blk_e, n_valid, x_rows, w_gate_e[l], w_up_e[l], w_down_e[l])
        xc = _combine(dest3, x1, wts, mod3, final_g.reshape(1, d), y_rows, seq)
    return xc.reshape(bsz, seq, d)
```

```python
import functools
import math

import jax
import jax.numpy as jnp
import numpy as np
from jax import lax
from jax.experimental import pallas as pl
from jax.experimental.pallas import tpu as pltpu

F32 = jnp.float32
BF16 = jnp.bfloat16

D_MODEL = 1024
N_MOD = 6
RMS_EPS = 1e-6
FOX_HEADS = 8
FOX_HEAD_DIM = 64
FOX_WIDTH = FOX_HEADS * FOX_HEAD_DIM
HEAD_PAIRS = FOX_HEADS // 2
SSM_WIDTH = 512
SSM_GROUP = 16
SSM_GROUPS = SSM_WIDTH // SSM_GROUP
SSM_STATE = 64
LAMBDA_RE_MAX = -1e-4
N_GROUPS = 4
EXPERTS_PER_GROUP = 8
N_EXPERTS = N_GROUPS * EXPERTS_PER_GROUP
D_EXPERT = 512

LANES = 128
SUBLANES = 8
VMEM_LIMIT = 56 * 1024 * 1024

SSM_CHUNK = 16
TOK_TILE = 512
MIX_TILE = 256
ATT_Q_TILE = 512
ATT_K_TILE = 512
ROW_BLOCK = 512
MOVE_TILE = 512
NEG_BIG = -1e30

HIGHEST = lax.Precision.HIGHEST


def _params(sem):
    return pltpu.CompilerParams(dimension_semantics=sem, vmem_limit_bytes=VMEM_LIMIT)


def _sigmoid(x):
    return 0.5 * jnp.tanh(0.5 * x) + 0.5


def _rms_modulate(x, gain, shift, scale):
    ms = jnp.mean(x * x, axis=-1, keepdims=True)
    return (x * lax.rsqrt(ms + RMS_EPS)) * gain * (1.0 + scale) + shift


def _mod_kernel(c_ref, w_ref, b_ref, o_ref):
    c = c_ref[...]
    ca = (c * jax.nn.sigmoid(c)).astype(BF16)
    o_ref[...] = jnp.dot(ca, w_ref[...].astype(BF16), preferred_element_type=F32) + b_ref[...]


def _mod(c, w_ada, b_ada):
    bsz, d = c.shape
    cols = w_ada.shape[1]
    tn = 1536
    return pl.pallas_call(
        _mod_kernel,
        grid=(cols // tn,),
        in_specs=[pl.BlockSpec((bsz, d), lambda j: (0, 0)),
                  pl.BlockSpec((d, tn), lambda j: (0, j)),
                  pl.BlockSpec((1, tn), lambda j: (0, j))],
        out_specs=pl.BlockSpec((bsz, tn), lambda j: (0, j)),
        out_shape=jax.ShapeDtypeStruct((bsz, cols), F32),
        compiler_params=_params(("arbitrary",)),
        name="mod",
    )(c, w_ada, b_ada.reshape(1, cols))


_C_Q, _C_K, _C_U, _C_GA, _C_GB, _C_F, _C_END = 0, 512, 1024, 1536, 2560, 3584, 3712


def _lane_block():
    return lax.broadcasted_iota(jnp.int32, (1, LANES), 1) // SSM_GROUP


def _to_group_major(tok_ref, flat_ref, rows):
    blk = _lane_block()
    for half in range(2):
        for j in range(SSM_WIDTH // LANES):
            w = []
            for s8 in range(8):
                v = tok_ref[j, pl.ds(8 * half + s8, rows, stride=SSM_CHUNK), :]
                w.append(pltpu.roll(v, s8 * SSM_GROUP, axis=1) if s8 else v)
            for p in range(8):
                acc = w[0]
                for s8 in range(1, 8):
                    acc = jnp.where(blk == (p + s8) % 8, w[s8], acc)
                flat_ref[8 * j + p, :, half * LANES:(half + 1) * LANES] = acc.astype(flat_ref.dtype)


def _to_token_major(flat_ref, tok_ref, rows):
    blk = _lane_block()
    for half in range(2):
        for j in range(SSM_WIDTH // LANES):
            ys = [flat_ref[8 * j + p, :, half * LANES:(half + 1) * LANES] for p in range(8)]
            for s8 in range(8):
                acc = ys[0]
                for p in range(1, 8):
                    acc = jnp.where(blk == (p + s8) % 8, ys[p], acc)
                if s8:
                    acc = pltpu.roll(acc, LANES - s8 * SSM_GROUP, axis=1)
                tok_ref[j, pl.ds(8 * half + s8, rows, stride=SSM_CHUNK), :] = acc


def _bias_lane_placement():
    pq = np.zeros((3 * LANES, LANES), np.float32)
    pk = np.zeros((3 * LANES, LANES), np.float32)
    bq = np.zeros((1, LANES), np.float32)
    bk = np.zeros((1, LANES), np.float32)
    for head in range(FOX_HEADS):
        base = head * 8
        for term in range(3):
            pq[term * LANES + head, base + term] = 1.0
            pk[term * LANES + head, base + 3 + term] = -1.0
            bq[0, base + 3 + term] = 1.0
            bk[0, base + term] = 1.0
    return pq, pk, bq, bk


def _top_bits(a):
    bits = lax.bitcast_convert_type(a, jnp.uint32) & jnp.uint32(0xFFFF0000)
    return lax.bitcast_convert_type(bits, F32)


def _inproj_kernel(tiles_per_batch, x_ref, mod_ref, g_ref, w_ref, wvt_ref, bf_ref, tri_ref,
                   pq_ref, pk_ref, bq_ref, bk_ref,
                   q_ref, k_ref, vt_ref, u_ref, uflat_ref, ga_ref, gb_ref, carry_ref, uslab_ref):
    i = pl.program_id(0)
    h = _rms_modulate(x_ref[...], g_ref[...], mod_ref[0:1, :], mod_ref[1:2, :])
    hb = h.astype(BF16)

    def proj(a, b):
        return jnp.dot(hb, w_ref[:, a:b], preferred_element_type=F32)

    q = proj(_C_Q, _C_K).astype(BF16)
    k = proj(_C_K, _C_U).astype(BF16)
    vt_ref[...] = lax.dot_general(wvt_ref[...], hb, (((1,), (1,)), ((), ())),
                                  preferred_element_type=F32).astype(BF16)
    u = proj(_C_U, _C_GA)
    u_ref[...] = u
    for j in range(SSM_WIDTH // LANES):
        uslab_ref[j] = u[:, j * LANES:(j + 1) * LANES]
    _to_group_major(uslab_ref, uflat_ref, u.shape[0] // SSM_CHUNK)
    ga_ref[...] = _sigmoid(proj(_C_GA, _C_GB)).astype(BF16)
    gb_ref[...] = _sigmoid(proj(_C_GB, _C_F)).astype(BF16)

    f = proj(_C_F, _C_END) + bf_ref[...]
    logf = jnp.minimum(f, 0.0) - jnp.log(1.0 + jnp.exp(-jnp.abs(f)))

    @pl.when(i % tiles_per_batch == 0)
    def _():
        carry_ref[...] = jnp.zeros_like(carry_ref)

    def split3(a):
        hi = _top_bits(a)
        r1 = a - hi
        mid = _top_bits(r1)
        return jnp.concatenate([hi, mid, _top_bits(r1 - mid)], axis=1).astype(BF16)

    part = jnp.dot(tri_ref[...], split3(logf), preferred_element_type=F32)
    cs = (part[:, :LANES] + part[:, LANES:2 * LANES] + part[:, 2 * LANES:]) + carry_ref[0:1, :]
    carry_ref[...] = jnp.broadcast_to(cs[-1:, :], carry_ref.shape)

    terms = split3(cs)
    bias_q = (jnp.dot(terms, pq_ref[...], preferred_element_type=F32) + bq_ref[...]).astype(BF16)
    bias_k = (jnp.dot(terms, pk_ref[...], preferred_element_type=F32) + bk_ref[...]).astype(BF16)
    for p in range(HEAD_PAIRS):
        lanes = slice(p * LANES, (p + 1) * LANES)
        q_ref[:, 2 * p * LANES:(2 * p + 1) * LANES] = q[:, lanes]
        q_ref[:, (2 * p + 1) * LANES:(2 * p + 2) * LANES] = bias_q
        k_ref[:, 2 * p * LANES:(2 * p + 1) * LANES] = k[:, lanes]
        k_ref[:, (2 * p + 1) * LANES:(2 * p + 2) * LANES] = bias_k


def _inproj(x2, mod3, gain, w_all, w_vt, bf_pad, seq):
    n, d = x2.shape
    tm = TOK_TILE
    tpb = seq // tm
    tri = jnp.tril(jnp.ones((tm, tm), BF16))
    pq, pk, bq, bk = _bias_lane_placement()
    tok = lambda w: pl.BlockSpec((tm, w), lambda i: (i, 0))
    const = lambda shape: pl.BlockSpec(shape, lambda i: (0,) * len(shape))
    qk_width = 2 * FOX_WIDTH
    return pl.pallas_call(
        functools.partial(_inproj_kernel, tpb),
        grid=(n // tm,),
        in_specs=[tok(d),
                  pl.BlockSpec((None, N_MOD, d), lambda i: (i // tpb, 0, 0)),
                  const((1, d)), const((d, _C_END)), const((FOX_WIDTH, d)), const((1, LANES)),
                  const((tm, tm)), const(pq.shape), const(pk.shape), const(bq.shape), const(bk.shape)],
        out_specs=[tok(qk_width), tok(qk_width), pl.BlockSpec((FOX_WIDTH, tm), lambda i: (0, i)),
                   tok(SSM_WIDTH),
                   pl.BlockSpec((SSM_GROUPS, tm // SSM_CHUNK, SSM_CHUNK * SSM_GROUP), lambda i: (0, i, 0)),
                   tok(d), tok(d)],
        out_shape=[jax.ShapeDtypeStruct((n, qk_width), BF16)] * 2
        + [jax.ShapeDtypeStruct((FOX_WIDTH, n), BF16)]
        + [jax.ShapeDtypeStruct((n, SSM_WIDTH), F32)]
        + [jax.ShapeDtypeStruct((SSM_GROUPS, n // SSM_CHUNK, SSM_CHUNK * SSM_GROUP), BF16)]
        + [jax.ShapeDtypeStruct((n, d), BF16)] * 2,
        scratch_shapes=[pltpu.VMEM((SUBLANES, LANES), F32),
                        pltpu.VMEM((SSM_WIDTH // LANES, tm, LANES), F32)],
        compiler_params=_params(("arbitrary",)),
        name="inproj",
    )(x2, mod3, gain, w_all, w_vt, bf_pad, tri, jnp.asarray(pq, BF16), jnp.asarray(pk, BF16),
      jnp.asarray(bq), jnp.asarray(bk))


def _attn_kernel(q_ref, k_ref, vt_ref, o_ref, m_ref, acc_ref, sa_ref, sb_ref):
    tq, tk = ATT_Q_TILE, ATT_K_TILE
    seq = q_ref.shape[0]
    nq = seq // tq
    half = FOX_HEAD_DIM
    lane = lax.broadcasted_iota(jnp.int32, (1, 2 * LANES), 1)
    bias0 = LANES + 16 * pl.program_id(1)
    own0 = (lane < half) | ((lane >= bias0) & (lane < bias0 + 8))
    own1 = ((lane >= half) & (lane < LANES)) | ((lane >= bias0 + 8) & (lane < bias0 + 16))
    ones_rows = jnp.ones((2 * SUBLANES, tk), BF16)
    key_in_tile = lax.broadcasted_iota(jnp.int32, (tk, 2 * tq), 0)
    qry_in_tile = lax.broadcasted_iota(jnp.int32, (tk, 2 * tq), 1) & (tq - 1)
    bufs = (sa_ref, sb_ref)
    key_tiles = [-(-(i + 1) * tq // tk) for i in range(nq)]
    steps = [(i, j) for i in range(nq) for j in range(key_tiles[i])]
    q_cache = {}

    def q_both(i):
        if i not in q_cache:
            q = q_ref[i * tq:(i + 1) * tq, :]
            zq = jnp.zeros_like(q)
            q_cache[i] = jnp.concatenate([jnp.where(own0, q, zq), jnp.where(own1, q, zq)], axis=0)
        return q_cache[i]

    def scores(n):
        i, j = steps[n]
        s = lax.dot_general(k_ref[j * tk:(j + 1) * tk, :], q_both(i), (((1,), (1,)), ((), ())),
                            preferred_element_type=F32)
        if j * tk + tk - 1 > i * tq:
            s = jnp.where(key_in_tile + (j * tk - i * tq) <= qry_in_tile, s, NEG_BIG)
        bufs[n % 2][...] = s

    scores(0)
    for n, (i, j) in enumerate(steps):
        if n + 1 < len(steps):
            scores(n + 1)
        s_ref = bufs[n % 2]
        va = jnp.concatenate([vt_ref[:, j * tk:(j + 1) * tk], ones_rows], axis=0)
        for h in range(2):
            cols = slice(h * tq, (h + 1) * tq)
            if j == 0:
                m_new = jnp.max(s_ref[:, cols], axis=0, keepdims=True)
                p = jnp.exp(s_ref[:, cols] - m_new).astype(BF16)
                acc_ref[:, cols] = jnp.dot(va, p, preferred_element_type=F32)
            else:
                m_old = m_ref[:, cols]
                m_new = jnp.maximum(m_old, jnp.max(s_ref[:, cols], axis=0, keepdims=True))
                alpha = jnp.exp(m_old - m_new)
                p = jnp.exp(s_ref[:, cols] - m_new).astype(BF16)
                acc_ref[:, cols] = alpha * acc_ref[:, cols] + jnp.dot(va, p, preferred_element_type=F32)
            m_ref[:, cols] = m_new
        if j == key_tiles[i] - 1:
            acc = acc_ref[...]
            o_t = jnp.concatenate([acc[0:half, 0:tq] / acc[LANES:LANES + 1, 0:tq],
                                   acc[half:LANES, tq:2 * tq] / acc[LANES:LANES + 1, tq:2 * tq]], axis=0)
            o_ref[i * tq:(i + 1) * tq, :] = o_t.T.astype(o_ref.dtype)


def _attention(q, k, v_t, bsz, seq):
    n = q.shape[0]
    t = ATT_Q_TILE
    return pl.pallas_call(
        _attn_kernel,
        grid=(bsz, HEAD_PAIRS),
        in_specs=[pl.BlockSpec((seq, 2 * LANES), lambda b, p: (b, p)),
                  pl.BlockSpec((seq, 2 * LANES), lambda b, p: (b, p)),
                  pl.BlockSpec((LANES, seq), lambda b, p: (p, b))],
        out_specs=pl.BlockSpec((seq, LANES), lambda b, p: (b, p)),
        out_shape=jax.ShapeDtypeStruct((n, FOX_WIDTH), BF16),
        scratch_shapes=[pltpu.VMEM((1, 2 * t), F32), pltpu.VMEM((LANES + 2 * SUBLANES, 2 * t), F32),
                        pltpu.VMEM((ATT_K_TILE, 2 * t), F32), pltpu.VMEM((ATT_K_TILE, 2 * t), F32)],
        compiler_params=_params(("arbitrary", "arbitrary")),
        name="attn",
    )(q, k, v_t)


def _ssm_prep_kernel(lrow_ref, lcol_ref, ldt_ref, btr_ref, bti_ref, ctr_ref, cti_ref,
                     toep_ref, bst_ref, bsw_ref, cpw_ref, a_ref):
    p8 = pl.program_id(0) % 8
    t_len, grp = SSM_CHUNK, SSM_GROUP
    dt = jnp.exp(ldt_ref[...])
    lr, li = jnp.minimum(lrow_ref[0:1, :], LAMBDA_RE_MAX), lrow_ref[1:2, :]

    def powers(steps, re, im):
        mag = jnp.exp(steps * (re * dt))
        return mag * jnp.cos(steps * (im * dt)), mag * jnp.sin(steps * (im * dt))

    a_re, a_im = powers(1.0, lr, li)
    den = lr * lr + li * li
    nr = a_re - 1.0
    co_re = (nr * lr + a_im * li) / den
    co_im = (a_im * lr - nr * li) / den
    bbt_re = co_re * btr_ref[...] - co_im * bti_ref[...]
    bbt_im = co_re * bti_ref[...] + co_im * btr_ref[...]

    lag = (lax.broadcasted_iota(jnp.int32, (1, t_len * grp), 1) // grp).astype(F32)
    lcr, lci = jnp.minimum(lcol_ref[:, 0:1], LAMBDA_RE_MAX), lcol_ref[:, 1:2]

    def c_times_power(steps):
        p_re, p_im = powers(steps, lcr, lci)
        return (ctr_ref[...] * p_re - cti_ref[...] * p_im, ctr_ref[...] * p_im + cti_ref[...] * p_re)

    wt_re, wt_im = c_times_power(lag)
    kern = (jnp.dot(bbt_re, wt_re, precision=HIGHEST, preferred_element_type=F32)
            - jnp.dot(bbt_im, wt_im, precision=HIGHEST, preferred_element_type=F32))

    lane = lax.broadcasted_iota(jnp.int32, (1, LANES), 1)
    col_shift = p8 * grp

    def store_cols(ref, rows, lo_half, hi_half):
        ref[rows, 0:LANES] = pltpu.roll(lo_half, col_shift, axis=1).astype(ref.dtype)
        ref[rows, LANES:2 * LANES] = pltpu.roll(hi_half, col_shift, axis=1).astype(ref.dtype)

    def slot_rows(s):
        half, s8 = divmod(s, 8)
        return pl.ds(pl.multiple_of((8 * half + (s8 + p8) % 8) * grp, grp), grp)

    back = (t_len - 1 - lax.broadcasted_iota(jnp.int32, (t_len, 1), 0)).astype(F32)
    e_re, e_im = powers(back, lr, li)
    zero = jnp.zeros((grp, LANES), F32)
    k_lo, k_hi = kern[:, 0:LANES], kern[:, LANES:2 * LANES]
    for s in range(t_len):
        half, s8 = divmod(s, 8)
        keep = lane >= s8 * grp
        r_lo = pltpu.roll(k_lo, s8 * grp, axis=1) if s8 else k_lo
        r_hi = pltpu.roll(k_hi, s8 * grp, axis=1) if s8 else k_hi
        if half == 0:
            lo, hi = jnp.where(keep, r_lo, 0.0), jnp.where(keep, r_hi, r_lo)
        else:
            lo, hi = zero, jnp.where(keep, r_lo, 0.0)
        store_cols(toep_ref, slot_rows(s), lo, hi)
        es_re, es_im = e_re[s:s + 1, :], e_im[s:s + 1, :]
        bs_re = es_re * bbt_re - es_im * bbt_im
        bs_im = es_re * bbt_im + es_im * bbt_re
        bst_ref[slot_rows(s), :] = jnp.concatenate([bs_re, bs_im], axis=1).astype(bst_ref.dtype)
        bsw_ref[slot_rows(s), :] = jnp.concatenate([bs_im, bs_re], axis=1).astype(bsw_ref.dtype)

    w1_re, w1_im = c_times_power(lag + 1.0)
    store_cols(cpw_ref, pl.ds(0, SSM_STATE), w1_re[:, 0:LANES], w1_re[:, LANES:2 * LANES])
    store_cols(cpw_ref, pl.ds(SSM_STATE, SSM_STATE), -w1_im[:, 0:LANES], -w1_im[:, LANES:2 * LANES])
    s_re, s_im = powers(float(t_len), lr, li)
    a_ref[0:1, :] = jnp.concatenate([s_re, s_re], axis=1)
    a_ref[1:2, :] = jnp.concatenate([-s_im, s_im], axis=1)


def _ssm_prep(lambda_re, lambda_im, log_dt, b_re, b_im, c_re, c_im):
    width = SSM_CHUNK * SSM_GROUP
    lam_row = jnp.stack([lambda_re, lambda_im], axis=1)
    tiled = lambda c: jnp.tile(c.transpose(0, 2, 1), (1, 1, SSM_CHUNK))
    per = lambda a, b: pl.BlockSpec((None, a, b), lambda g: (g, 0, 0))
    return pl.pallas_call(
        _ssm_prep_kernel,
        grid=(SSM_GROUPS,),
        in_specs=[per(2, SSM_STATE), per(SSM_STATE, 2), per(1, 1), per(SSM_GROUP, SSM_STATE),
                  per(SSM_GROUP, SSM_STATE), per(SSM_STATE, width), per(SSM_STATE, width)],
        out_specs=[per(width, width), per(width, 2 * SSM_STATE), per(width, 2 * SSM_STATE),
                   per(2 * SSM_STATE, width), per(2, 2 * SSM_STATE)],
        out_shape=[jax.ShapeDtypeStruct((SSM_GROUPS, width, width), BF16),
                   jax.ShapeDtypeStruct((SSM_GROUPS, width, 2 * SSM_STATE), BF16),
                   jax.ShapeDtypeStruct((SSM_GROUPS, width, 2 * SSM_STATE), BF16),
                   jax.ShapeDtypeStruct((SSM_GROUPS, 2 * SSM_STATE, width), BF16),
                   jax.ShapeDtypeStruct((SSM_GROUPS, 2, 2 * SSM_STATE), F32)],
        compiler_params=_params(("arbitrary",)),
        name="ssm_prep",
    )(lam_row, lam_row.transpose(0, 2, 1), log_dt.reshape(SSM_GROUPS, 1, 1),
      b_re.transpose(0, 2, 1), b_im.transpose(0, 2, 1), tiled(c_re), tiled(c_im))


SSM_GROUPS_PER_STEP = 4


def _ssm_kernel(n_chunks, bsz, u_ref, toep_ref, bst_ref, bsw_ref, cpw_ref, a_ref, y_ref,
                contrib_ref, cswap_ref, xprev_ref):
    groups = u_ref.shape[0]
    for k in range(groups):
        u = u_ref[k]
        contrib_ref[k] = jnp.dot(u, bst_ref[k], preferred_element_type=F32)
        cswap_ref[k] = jnp.dot(u, bsw_ref[k], preferred_element_type=F32)
    a1 = [a_ref[k, 0:1, :] for k in range(groups)]
    a2 = [a_ref[k, 1:2, :] for k in range(groups)]

    def step(n, carry):
        rows = pl.ds(n, bsz, stride=n_chunks)
        new = []
        for k in range(groups):
            x, xs = carry[2 * k], carry[2 * k + 1]
            xprev_ref[k, rows, :] = x
            new.append(a1[k] * x + a2[k] * xs + contrib_ref[k, rows, :])
            new.append(a1[k] * xs - a2[k] * x + cswap_ref[k, rows, :])
        return tuple(new)

    zero = jnp.zeros((bsz, 2 * SSM_STATE), F32)
    lax.fori_loop(0, n_chunks, step, (zero,) * (2 * groups), unroll=2)
    for k in range(groups):
        y_ref[k] = (jnp.dot(u_ref[k], toep_ref[k], preferred_element_type=F32)
                    + jnp.dot(xprev_ref[k].astype(BF16), cpw_ref[k], preferred_element_type=F32))


def _ssm(u_flat, toep, b_state, b_swap, c_pow, a_step, bsz):
    g, rows, w = u_flat.shape
    gb = SSM_GROUPS_PER_STEP
    per = lambda a, b: pl.BlockSpec((gb, a, b), lambda i: (i, 0, 0))
    state = pltpu.VMEM((gb, rows, 2 * SSM_STATE), F32)
    return pl.pallas_call(
        functools.partial(_ssm_kernel, rows // bsz, bsz),
        grid=(g // gb,),
        in_specs=[per(rows, w), per(w, w), per(w, 2 * SSM_STATE), per(w, 2 * SSM_STATE),
                  per(2 * SSM_STATE, w), per(2, 2 * SSM_STATE)],
        out_specs=per(rows, w),
        out_shape=jax.ShapeDtypeStruct((g, rows, w), F32),
        scratch_shapes=[state, state, state],
        compiler_params=_params(("arbitrary",)),
        name="ssm",
    )(u_flat, toep, b_state, b_swap, c_pow, a_step)


ROW_SLABS = D_MODEL // LANES // 2
_HIGH_HALF = 0xFFFF0000


def _store_row_tiles(ref, value):
    rows = value.shape[0]
    bits = lax.bitcast_convert_type(value.astype(BF16).astype(F32), jnp.uint32)
    for j in range(ROW_SLABS):
        low = bits[:, j * LANES:(j + 1) * LANES] >> 16
        high = bits[:, (j + ROW_SLABS) * LANES:(j + ROW_SLABS + 1) * LANES] & jnp.uint32(_HIGH_HALF)
        ref[pl.ds(j, rows, stride=ROW_SLABS), :] = high | low


def _load_row_tiles(ref, rows):
    words = [ref[pl.ds(j, rows, stride=ROW_SLABS), :] for j in range(ROW_SLABS)]
    low = [lax.bitcast_convert_type(w << 16, F32) for w in words]
    high = [lax.bitcast_convert_type(w & jnp.uint32(_HIGH_HALF), F32) for w in words]
    return jnp.concatenate(low + high, axis=1)


def _row_tile_copy(src_ref, src_row, dst_ref, dst_row, sem):
    src = src_ref.at[pl.ds(pl.multiple_of(src_row * ROW_SLABS, ROW_SLABS), ROW_SLABS), :]
    dst = dst_ref.at[pl.ds(pl.multiple_of(dst_row * ROW_SLABS, ROW_SLABS), ROW_SLABS), :]
    return pltpu.make_async_copy(src, dst, sem)


def _mix_kernel(x_ref, of_ref, yf_ref, u_ref, ga_ref, gb_ref, mod_ref, dsk_ref, wglu_ref, wfox_ref,
                wssm_ref, wo_ref, g2_ref, wr_ref, br_ref, x1_ref, h2_ref, lg_ref, ytok_ref):
    _to_token_major(yf_ref, ytok_ref, yf_ref.shape[1])
    y_ssm = jnp.concatenate([ytok_ref[j] for j in range(SSM_WIDTH // LANES)], axis=1)
    y = y_ssm + dsk_ref[...] * u_ref[...]
    y = 0.5 * y * (1.0 + jnp.tanh(math.sqrt(2.0 / math.pi) * (y + 0.044715 * (y * y * y))))
    gl = jnp.dot(y.astype(BF16), wglu_ref[...], preferred_element_type=F32)
    o_ssm = gl[:, :SSM_WIDTH] * _sigmoid(gl[:, SSM_WIDTH:])
    merged = (ga_ref[...].astype(F32) * jnp.dot(of_ref[...], wfox_ref[...], preferred_element_type=F32)
              + gb_ref[...].astype(F32) * jnp.dot(o_ssm.astype(BF16), wssm_ref[...],
                                                  preferred_element_type=F32))
    x1 = x_ref[...] + mod_ref[2:3, :] * jnp.dot(merged.astype(BF16), wo_ref[...],
                                                 preferred_element_type=F32)
    x1_ref[...] = x1
    h2 = _rms_modulate(x1, g2_ref[...], mod_ref[3:4, :], mod_ref[4:5, :])
    _store_row_tiles(h2_ref, h2)
    a1 = _top_bits(h2)
    a2 = _top_bits(h2 - a1)
    lhs = jnp.concatenate([a1, a1, a2], axis=1).astype(BF16)
    lg_ref[...] = jnp.dot(lhs, wr_ref[...], preferred_element_type=F32) + br_ref[...]


def _mix(x2, o_fox, y_flat, u, sga, sgb, mod3, d_skip, w_glu, w_fox, w_ssm, w_o, g2, w_r, b_r, seq):
    n, d = x2.shape
    tm = MIX_TILE
    tpb = seq // tm
    tok = lambda w: pl.BlockSpec((tm, w), lambda i: (i, 0))
    const = lambda a: pl.BlockSpec(a.shape, lambda i: (0,) * a.ndim)
    flat = pl.BlockSpec((SSM_GROUPS, tm // SSM_CHUNK, SSM_CHUNK * SSM_GROUP), lambda i: (0, i, 0))
    return pl.pallas_call(
        _mix_kernel,
        grid=(n // tm,),
        in_specs=[tok(d), tok(FOX_WIDTH), flat, tok(SSM_WIDTH), tok(d), tok(d),
                  pl.BlockSpec((None, N_MOD, d), lambda i: (i // tpb, 0, 0)),
                  const(d_skip), const(w_glu), const(w_fox), const(w_ssm), const(w_o), const(g2),
                  const(w_r), const(b_r)],
        out_specs=[tok(d), pl.BlockSpec((tm * ROW_SLABS, LANES), lambda i: (i, 0)), tok(LANES)],
        out_shape=[jax.ShapeDtypeStruct((n, d), F32), jax.ShapeDtypeStruct((n * ROW_SLABS, LANES), jnp.uint32),
                   jax.ShapeDtypeStruct((n, LANES), F32)],
        scratch_shapes=[pltpu.VMEM((SSM_WIDTH // LANES, tm, LANES), F32)],
        compiler_params=_params(("arbitrary",)),
        name="mix",
    )(x2, o_fox, y_flat, u, sga, sgb, mod3, d_skip, w_glu, w_fox, w_ssm, w_o, g2, w_r, b_r)


def _route_kernel(lg_ref, tri_ref, er_ref, wt_ref, cnt_ref, carry_ref):
    i = pl.program_id(0)

    @pl.when(i == 0)
    def _():
        carry_ref[...] = jnp.zeros_like(carry_ref)

    lg = lg_ref[...].T
    tm = lg.shape[1]
    row = lax.broadcasted_iota(jnp.int32, (LANES, tm), 0)
    neg = jnp.full_like(lg, -jnp.inf)

    def first_argmax(vals):
        mx = jnp.max(vals, axis=0, keepdims=True)
        ix = jnp.min(jnp.where(vals == mx, row, LANES), axis=0, keepdims=True)
        return mx, ix

    is_group = row < N_GROUPS
    g_max, gi = first_argmax(jnp.where(is_group, lg, neg))
    g_sum = jnp.sum(jnp.where(is_group, jnp.exp(lg - g_max), 0.0), axis=0, keepdims=True)
    p_group = 1.0 / g_sum
    lo = N_GROUPS + EXPERTS_PER_GROUP * gi
    in_group = (row >= lo) & (row < lo + EXPERTS_PER_GROUP)
    cand = jnp.where(in_group, lg, neg)
    v1, i1 = first_argmax(cand)
    v2, i2 = first_argmax(jnp.where(row == i1, neg, cand))
    tt = jnp.exp(v2 - v1)
    w1 = p_group / (1.0 + tt)
    w2 = p_group * tt / (1.0 + tt)
    e1 = i1 - N_GROUPS
    e2 = i2 - N_GROUPS
    sel1 = row == e1
    sel2 = row == e2
    onehot = (sel1 | sel2).astype(F32)
    before = jnp.dot(onehot.astype(BF16), tri_ref[...], preferred_element_type=F32) + carry_ref[:, 0:1]
    r1 = jnp.sum(jnp.where(sel1, before, 0.0), axis=0, keepdims=True).astype(jnp.int32)
    r2 = jnp.sum(jnp.where(sel2, before, 0.0), axis=0, keepdims=True).astype(jnp.int32)
    total = before[:, tm - 1:tm] + onehot[:, tm - 1:tm]
    carry_ref[...] = jnp.broadcast_to(total, carry_ref.shape)
    cnt_ref[...] = jnp.broadcast_to(total, cnt_ref.shape)
    slot = lax.broadcasted_iota(jnp.int32, (SUBLANES, tm), 0)
    er_ref[...] = jnp.where(slot == 0, e1, jnp.where(slot == 1, e2, jnp.where(slot == 2, r1, r2)))
    wt_ref[...] = jnp.where(row == 0, w1, jnp.where(row == 1, w2, 0.0)).T


def _route(logits):
    n = logits.shape[0]
    tm = TOK_TILE
    tri = jnp.triu(jnp.ones((tm, tm), BF16), k=1)
    tok = pl.BlockSpec((tm, LANES), lambda i: (i, 0))
    return pl.pallas_call(
        _route_kernel,
        grid=(n // tm,),
        in_specs=[tok, pl.BlockSpec((tm, tm), lambda i: (0, 0))],
        out_specs=[pl.BlockSpec((SUBLANES, tm), lambda i: (0, i)), tok,
                   pl.BlockSpec((LANES, LANES), lambda i: (0, 0))],
        out_shape=[jax.ShapeDtypeStruct((SUBLANES, n), jnp.int32), jax.ShapeDtypeStruct((n, LANES), F32),
                   jax.ShapeDtypeStruct((LANES, LANES), F32)],
        scratch_shapes=[pltpu.VMEM((LANES, LANES), F32)],
        compiler_params=_params(("arbitrary",)),
        name="route",
    )(logits, tri)


ISSUE_UNROLL = 8


def _dispatch_kernel(dest_ref, h_ref, rows_in_ref, rows_ref, sem):
    del rows_in_ref
    tm = h_ref.shape[0] // ROW_SLABS

    def issue(g, c):
        for j in range(ISSUE_UNROLL):
            t = g * ISSUE_UNROLL + j
            _row_tile_copy(h_ref, t, rows_ref, dest_ref[0, 0, t], sem).start(priority=0)
            _row_tile_copy(h_ref, t, rows_ref, dest_ref[0, 0, tm + t], sem).start(priority=1)
        return c

    lax.fori_loop(0, tm // ISSUE_UNROLL, issue, 0)
    for _ in range(2):
        pltpu.make_async_copy(h_ref, rows_ref.at[pl.ds(0, tm * ROW_SLABS), :], sem).wait()


def _dispatch(dest3, h2_tiles, rows_zero):
    tm = MOVE_TILE
    n = h2_tiles.shape[0] // ROW_SLABS
    return pl.pallas_call(
        _dispatch_kernel,
        grid=(n // tm,),
        in_specs=[pl.BlockSpec((1, 1, 2 * tm), lambda i: (i, 0, 0), memory_space=pltpu.SMEM),
                  pl.BlockSpec((tm * ROW_SLABS, LANES), lambda i: (i, 0)),
                  pl.BlockSpec(memory_space=pl.ANY)],
        out_specs=pl.BlockSpec(memory_space=pl.ANY),
        out_shape=jax.ShapeDtypeStruct(rows_zero.shape, rows_zero.dtype),
        scratch_shapes=[pltpu.SemaphoreType.DMA(())],
        input_output_aliases={2: 0},
        compiler_params=_params(("arbitrary",)),
        name="dispatch",
    )(dest3, h2_tiles, rows_zero)


def _combine_kernel(n_steps, dest_ref, dnext_ref, x1_ref, wt_ref, mod_ref, gf_ref, yr_ref, o_ref,
                    buf_ref, sem):
    i = pl.program_id(0)
    tm = x1_ref.shape[0]

    def gather(idx_ref, which):
        def issue(g, c):
            for j in range(ISSUE_UNROLL):
                t = g * ISSUE_UNROLL + j
                _row_tile_copy(yr_ref, idx_ref[0, 0, t], buf_ref.at[which, 0], t,
                               sem.at[which]).start(priority=0)
                _row_tile_copy(yr_ref, idx_ref[0, 0, tm + t], buf_ref.at[which, 1], t,
                               sem.at[which]).start(priority=1)
            return c

        lax.fori_loop(0, tm // ISSUE_UNROLL, issue, 0)

    cur = i % 2

    @pl.when(i == 0)
    def _():
        gather(dest_ref, 0)

    @pl.when(i + 1 < n_steps)
    def _():
        gather(dnext_ref, 1 - cur)

    for slot in range(2):
        pltpu.make_async_copy(yr_ref.at[pl.ds(0, tm * ROW_SLABS), :], buf_ref.at[cur, slot],
                              sem.at[cur]).wait()
    wt = wt_ref[...]
    moe = (wt[:, 0:1] * _load_row_tiles(buf_ref.at[cur, 0], tm)
           + wt[:, 1:2] * _load_row_tiles(buf_ref.at[cur, 1], tm))
    x = x1_ref[...] + mod_ref[5:6, :] * moe
    ms = jnp.mean(x * x, axis=-1, keepdims=True)
    o_ref[...] = (x * lax.rsqrt(ms + RMS_EPS)) * gf_ref[...]


def _combine(dest3, x1, wts, mod3, final_g, y_rows, seq):
    n, d = x1.shape
    tm = MOVE_TILE
    tpb = seq // tm
    n_steps = n // tm
    idx_spec = lambda f: pl.BlockSpec((1, 1, 2 * tm), f, memory_space=pltpu.SMEM)
    return pl.pallas_call(
        functools.partial(_combine_kernel, n_steps),
        grid=(n_steps,),
        in_specs=[idx_spec(lambda i: (i, 0, 0)),
                  idx_spec(lambda i: (jnp.minimum(i + 1, n_steps - 1), 0, 0)),
                  pl.BlockSpec((tm, d), lambda i: (i, 0)),
                  pl.BlockSpec((tm, LANES), lambda i: (i, 0)),
                  pl.BlockSpec((None, N_MOD, d), lambda i: (i // tpb, 0, 0)),
                  pl.BlockSpec((1, d), lambda i: (0, 0)),
                  pl.BlockSpec(memory_space=pl.ANY)],
        out_specs=pl.BlockSpec((tm, d), lambda i: (i, 0)),
        out_shape=jax.ShapeDtypeStruct((n, d), F32),
        scratch_shapes=[pltpu.VMEM((2, 2, tm * ROW_SLABS, LANES), jnp.uint32),
                        pltpu.SemaphoreType.DMA((2,))],
        compiler_params=_params(("arbitrary",)),
        name="combine",
    )(dest3, dest3, x1, wts, mod3, final_g, y_rows)


def _expert_kernel(be_ref, seg_ref, nxt_ref, nv_ref, x_ref, wg_hbm, wu_hbm, wd_hbm, y_ref,
                   wg_buf, wu_buf, wd_buf, wgb_ref, wub_ref, wdb_ref, sem):
    i = pl.program_id(0)
    valid = i < nv_ref[0]
    first = (i == 0) | (be_ref[i] != be_ref[jnp.maximum(i - 1, 0)])
    slot = seg_ref[i] % 2

    def weight_copies(e, s):
        return [pltpu.make_async_copy(hbm.at[e], buf.at[s], sem.at[s])
                for hbm, buf in ((wg_hbm, wg_buf), (wu_hbm, wu_buf), (wd_hbm, wd_buf))]

    @pl.when(valid & (i == 0))
    def _():
        for c in weight_copies(be_ref[0], 0):
            c.start()

    @pl.when(valid & first)
    def _():
        for c in weight_copies(be_ref[i], slot):
            c.wait()

        @pl.when(nxt_ref[i] >= 0)
        def _():
            for c in weight_copies(nxt_ref[i], 1 - slot):
                c.start()

        wgb_ref[...] = wg_buf[slot].astype(BF16)
        wub_ref[...] = wu_buf[slot].astype(BF16)
        wdb_ref[...] = wd_buf[slot].astype(BF16)

    @pl.when(valid)
    def _():
        xb = _load_row_tiles(x_ref, x_ref.shape[0] // ROW_SLABS).astype(BF16)
        a = jnp.dot(xb, wgb_ref[...], preferred_element_type=F32)
        b = jnp.dot(xb, wub_ref[...], preferred_element_type=F32)
        hid = (a * _sigmoid(a)) * b
        _store_row_tiles(y_ref, jnp.dot(hid.astype(BF16), wdb_ref[...], preferred_element_type=F32))

    @pl.when(jnp.logical_not(valid))
    def _():
        y_ref[...] = jnp.zeros_like(y_ref)


def _experts(blk_e, n_valid, x_rows, w_gate, w_up, w_down):
    d = D_MODEL
    rows = x_rows.shape[0] // ROW_SLABS
    tb = ROW_BLOCK
    n_blocks = rows // tb
    idx = jnp.arange(n_blocks, dtype=jnp.int32)
    change = (idx == 0) | (blk_e != jnp.roll(blk_e, 1))
    seg = jnp.cumsum(change.astype(jnp.int32)) - 1
    later_start = (idx[None, :] > idx[:, None]) & change[None, :] & (idx[None, :] < n_valid[0])
    none = jnp.int32(N_EXPERTS)
    nxt = jnp.min(jnp.where(later_start, blk_e[None, :], none), axis=1)
    nxt = jnp.where(nxt == none, -1, nxt)
    row_spec = pl.BlockSpec((tb * ROW_SLABS, LANES), lambda i, *_: (i, 0))
    grid_spec = pltpu.PrefetchScalarGridSpec(
        num_scalar_prefetch=4,
        grid=(n_blocks,),
        in_specs=[row_spec, pl.BlockSpec(memory_space=pl.ANY), pl.BlockSpec(memory_space=pl.ANY),
                  pl.BlockSpec(memory_space=pl.ANY)],
        out_specs=row_spec,
        scratch_shapes=[pltpu.VMEM((2, d, D_EXPERT), F32), pltpu.VMEM((2, d, D_EXPERT), F32),
                        pltpu.VMEM((2, D_EXPERT, d), F32),
                        pltpu.VMEM((d, D_EXPERT), BF16), pltpu.VMEM((d, D_EXPERT), BF16),
                        pltpu.VMEM((D_EXPERT, d), BF16), pltpu.SemaphoreType.DMA((2,))],
    )
    return pl.pallas_call(
        _expert_kernel,
        grid_spec=grid_spec,
        out_shape=jax.ShapeDtypeStruct(x_rows.shape, x_rows.dtype),
        compiler_params=_params(("arbitrary",)),
        name="experts",
    )(blk_e, seg, nxt.astype(jnp.int32), n_valid, x_rows, w_gate, w_up, w_down)


def kernel(x, c, w_ada, b_ada, norm_mix_g, w_in, b_forget, w_out_fox, lambda_re, lambda_im, log_dt,
           ssm_b_re, ssm_b_im, ssm_c_re, ssm_c_im, d_skip, w_glu, w_out_ssm, w_o, norm_ffn_g,
           w_router_group, b_router_group, w_router_expert, b_router_expert, w_gate_e, w_up_e,
           w_down_e, final_g):
    bsz, seq, d = x.shape
    n = bsz * seq
    assert w_ada.shape[0] == 1, "the final RMSNorm is fused into the (single) layer's combine kernel"
    xc = x.reshape(n, d)
    for l in range(1):
        mod3 = _mod(c, w_ada[l], b_ada[l]).reshape(bsz, N_MOD, d)

        wi = w_in[l]
        s_q, s_k, s_v, s_f, s_u, s_ga = 512, 1024, 1536, 1544, 2056, 3080
        scale = FOX_HEAD_DIM ** -0.5
        w_all = jnp.concatenate(
            [wi[:, :s_q] * scale, wi[:, s_q:s_k], wi[:, s_f:s_u], wi[:, s_u:s_ga],
             wi[:, s_ga:], jnp.pad(wi[:, s_v:s_f], ((0, 0), (0, LANES - FOX_HEADS)))],
            axis=1).astype(BF16)
        w_vt = wi[:, s_k:s_v].T.astype(BF16)
        bf_pad = jnp.pad(b_forget[l], (0, LANES - FOX_HEADS)).reshape(1, LANES)
        q, k, v_t, u, u_flat, sga, sgb = _inproj(xc, mod3, norm_mix_g[l].reshape(1, d), w_all, w_vt,
                                                 bf_pad, seq)

        o_fox = _attention(q, k, v_t, bsz, seq)

        toep, b_state, b_swap, c_pow, a_step = _ssm_prep(
            lambda_re[l], lambda_im[l], log_dt[l], ssm_b_re[l], ssm_b_im[l], ssm_c_re[l], ssm_c_im[l])
        y_flat = _ssm(u_flat, toep, b_state, b_swap, c_pow, a_step, bsz)

        w_r = jnp.pad(jnp.concatenate([w_router_group[l], w_router_expert[l]], axis=1),
                      ((0, 0), (0, LANES - N_GROUPS - N_EXPERTS)))
        w_r1 = _top_bits(w_r)
        w_r2 = _top_bits(w_r - w_r1)
        w_r = jnp.concatenate([w_r1, w_r2, w_r1], axis=0).astype(BF16)
        b_r = jnp.pad(jnp.concatenate([b_router_group[l], b_router_expert[l]]),
                      (0, LANES - N_GROUPS - N_EXPERTS)).reshape(1, LANES)
        x1, h2, logits = _mix(xc, o_fox, y_flat, u, sga, sgb, mod3, d_skip[l].reshape(1, SSM_WIDTH),
                              w_glu[l].astype(BF16), w_out_fox[l].astype(BF16),
                              w_out_ssm[l].astype(BF16), w_o[l].astype(BF16),
                              norm_ffn_g[l].reshape(1, d), w_r, b_r, seq)

        er, wts, cnt = _route(logits)
        counts = cnt[:N_EXPERTS, 0].astype(jnp.int32)
        pcounts = ((counts + ROW_BLOCK - 1) // ROW_BLOCK) * ROW_BLOCK
        pends = jnp.cumsum(pcounts)
        pstarts = pends - pcounts
        hit = er[0:2, None, :] == jnp.arange(N_EXPERTS, dtype=jnp.int32)[None, :, None]
        dest = jnp.sum(jnp.where(hit, pstarts[None, :, None], 0), axis=1) + er[2:4]
        rows = 2 * n + N_EXPERTS * ROW_BLOCK
        n_blocks = rows // ROW_BLOCK
        blk_start = jnp.arange(n_blocks, dtype=jnp.int32) * ROW_BLOCK
        blk_e = jnp.minimum(jnp.sum((pends[None, :] <= blk_start[:, None]).astype(jnp.int32), axis=1),
                            N_EXPERTS - 1)
        n_valid = (pends[-1:] // ROW_BLOCK).astype(jnp.int32)
        dest3 = (dest.astype(jnp.int32).reshape(2, n // MOVE_TILE, MOVE_TILE).transpose(1, 0, 2)
                 .reshape(n // MOVE_TILE, 1, 2 * MOVE_TILE))

        x_rows = _dispatch(dest3, h2, jnp.zeros((rows * ROW_SLABS, LANES), jnp.uint32))
        y_rows = _experts(blk_e, n_valid, x_rows, w_gate_e[l], w_up_e[l], w_down_e[l])
        xc = _combine(dest3, x1, wts, mod3, final_g.reshape(1, d), y_rows, seq)
    return xc.reshape(bsz, seq, d)
```

```python
import functools
import math

import jax
import jax.numpy as jnp
import numpy as np
from jax import lax
from jax.experimental import pallas as pl
from jax.experimental.pallas import tpu as pltpu

F32 = jnp.float32
BF16 = jnp.bfloat16

D_MODEL = 1024
N_MOD = 6
RMS_EPS = 1e-6
FOX_HEADS = 8
FOX_HEAD_DIM = 64
FOX_WIDTH = FOX_HEADS * FOX_HEAD_DIM
HEAD_PAIRS = FOX_HEADS // 2
SSM_WIDTH = 512
SSM_GROUP = 16
SSM_GROUPS = SSM_WIDTH // SSM_GROUP
SSM_STATE = 64
LAMBDA_RE_MAX = -1e-4
N_GROUPS = 4
EXPERTS_PER_GROUP = 8
N_EXPERTS = N_GROUPS * EXPERTS_PER_GROUP
D_EXPERT = 512

LANES = 128
SUBLANES = 8
VMEM_LIMIT = 56 * 1024 * 1024

SSM_CHUNK = 16
TOK_TILE = 512
MIX_TILE = 512
MIX_SUB_TILE = 256
ATT_Q_TILE = 512
ATT_K_TILE = 512
ROW_BLOCK = 512
MOVE_TILE = 512
NEG_BIG = -1e30

HIGHEST = lax.Precision.HIGHEST


def _params(sem):
    return pltpu.CompilerParams(dimension_semantics=sem, vmem_limit_bytes=VMEM_LIMIT)


def _sigmoid(x):
    return 0.5 * jnp.tanh(0.5 * x) + 0.5


def _rms_modulate(x, gain, shift, scale):
    ms = jnp.mean(x * x, axis=-1, keepdims=True)
    return (x * lax.rsqrt(ms + RMS_EPS)) * gain * (1.0 + scale) + shift


def _mod_kernel(c_ref, w_ref, b_ref, o_ref):
    c = c_ref[...]
    ca = (c * jax.nn.sigmoid(c)).astype(BF16)
    o_ref[...] = jnp.dot(ca, w_ref[...].astype(BF16), preferred_element_type=F32) + b_ref[...]


def _mod(c, w_ada, b_ada):
    bsz, d = c.shape
    cols = w_ada.shape[1]
    tn = 1536
    return pl.pallas_call(
        _mod_kernel,
        grid=(cols // tn,),
        in_specs=[pl.BlockSpec((bsz, d), lambda j: (0, 0)),
                  pl.BlockSpec((d, tn), lambda j: (0, j)),
                  pl.BlockSpec((1, tn), lambda j: (0, j))],
        out_specs=pl.BlockSpec((bsz, tn), lambda j: (0, j)),
        out_shape=jax.ShapeDtypeStruct((bsz, cols), F32),
        compiler_params=_params(("arbitrary",)),
        name="mod",
    )(c, w_ada, b_ada.reshape(1, cols))


_C_Q, _C_K, _C_U, _C_GA, _C_GB, _C_F, _C_END = 0, 512, 1024, 1536, 2560, 3584, 3712


def _lane_block():
    return lax.broadcasted_iota(jnp.int32, (1, LANES), 1) // SSM_GROUP


def _to_group_major(tok_ref, flat_ref, rows):
    blk = _lane_block()
    for half in range(2):
        for j in range(SSM_WIDTH // LANES):
            w = []
            for s8 in range(8):
                v = tok_ref[j, pl.ds(8 * half + s8, rows, stride=SSM_CHUNK), :]
                w.append(pltpu.roll(v, s8 * SSM_GROUP, axis=1) if s8 else v)
            for p in range(8):
                acc = w[0]
                for s8 in range(1, 8):
                    acc = jnp.where(blk == (p + s8) % 8, w[s8], acc)
                flat_ref[8 * j + p, :, half * LANES:(half + 1) * LANES] = acc.astype(flat_ref.dtype)


def _to_token_major(flat_ref, tok_ref, rows):
    blk = _lane_block()
    for half in range(2):
        for j in range(SSM_WIDTH // LANES):
            ys = [flat_ref[8 * j + p, :, half * LANES:(half + 1) * LANES] for p in range(8)]
            for s8 in range(8):
                acc = ys[0]
                for p in range(1, 8):
                    acc = jnp.where(blk == (p + s8) % 8, ys[p], acc)
                if s8:
                    acc = pltpu.roll(acc, LANES - s8 * SSM_GROUP, axis=1)
                tok_ref[j, pl.ds(8 * half + s8, rows, stride=SSM_CHUNK), :] = acc


def _bias_lane_placement():
    pq = np.zeros((3 * LANES, LANES), np.float32)
    pk = np.zeros((3 * LANES, LANES), np.float32)
    bq = np.zeros((1, LANES), np.float32)
    bk = np.zeros((1, LANES), np.float32)
    for head in range(FOX_HEADS):
        base = head * 8
        for term in range(3):
            pq[term * LANES + head, base + term] = 1.0
            pk[term * LANES + head, base + 3 + term] = -1.0
            bq[0, base + 3 + term] = 1.0
            bk[0, base + term] = 1.0
    return pq, pk, bq, bk


def _top_bits(a):
    bits = lax.bitcast_convert_type(a, jnp.uint32) & jnp.uint32(0xFFFF0000)
    return lax.bitcast_convert_type(bits, F32)


def _inproj_kernel(tiles_per_batch, x_ref, mod_ref, g_ref, w_ref, wvt_ref, bf_ref, tri_ref,
                   pq_ref, pk_ref, bq_ref, bk_ref,
                   q_ref, k_ref, vt_ref, u_ref, uflat_ref, ga_ref, gb_ref, carry_ref, uslab_ref):
    i = pl.program_id(0)
    h = _rms_modulate(x_ref[...], g_ref[...], mod_ref[0:1, :], mod_ref[1:2, :])
    hb = h.astype(BF16)

    def proj(a, b):
        return jnp.dot(hb, w_ref[:, a:b], preferred_element_type=F32)

    q = proj(_C_Q, _C_K).astype(BF16)
    k = proj(_C_K, _C_U).astype(BF16)
    vt_ref[...] = lax.dot_general(wvt_ref[...], hb, (((1,), (1,)), ((), ())),
                                  preferred_element_type=F32).astype(BF16)
    u = proj(_C_U, _C_GA)
    u_ref[...] = u
    for j in range(SSM_WIDTH // LANES):
        uslab_ref[j] = u[:, j * LANES:(j + 1) * LANES]
    _to_group_major(uslab_ref, uflat_ref, u.shape[0] // SSM_CHUNK)
    ga_ref[...] = _sigmoid(proj(_C_GA, _C_GB)).astype(BF16)
    gb_ref[...] = _sigmoid(proj(_C_GB, _C_F)).astype(BF16)

    f = proj(_C_F, _C_END) + bf_ref[...]
    logf = jnp.minimum(f, 0.0) - jnp.log(1.0 + jnp.exp(-jnp.abs(f)))

    @pl.when(i % tiles_per_batch == 0)
    def _():
        carry_ref[...] = jnp.zeros_like(carry_ref)

    def split3(a):
        hi = _top_bits(a)
        r1 = a - hi
        mid = _top_bits(r1)
        return jnp.concatenate([hi, mid, _top_bits(r1 - mid)], axis=1).astype(BF16)

    part = jnp.dot(tri_ref[...], split3(logf), preferred_element_type=F32)
    cs = (part[:, :LANES] + part[:, LANES:2 * LANES] + part[:, 2 * LANES:]) + carry_ref[0:1, :]
    carry_ref[...] = jnp.broadcast_to(cs[-1:, :], carry_ref.shape)

    terms = split3(cs)
    bias_q = (jnp.dot(terms, pq_ref[...], preferred_element_type=F32) + bq_ref[...]).astype(BF16)
    bias_k = (jnp.dot(terms, pk_ref[...], preferred_element_type=F32) + bk_ref[...]).astype(BF16)
    for p in range(HEAD_PAIRS):
        lanes = slice(p * LANES, (p + 1) * LANES)
        q_ref[:, 2 * p * LANES:(2 * p + 1) * LANES] = q[:, lanes]
        q_ref[:, (2 * p + 1) * LANES:(2 * p + 2) * LANES] = bias_q
        k_ref[:, 2 * p * LANES:(2 * p + 1) * LANES] = k[:, lanes]
        k_ref[:, (2 * p + 1) * LANES:(2 * p + 2) * LANES] = bias_k


def _inproj(x2, mod3, gain, w_all, w_vt, bf_pad, seq):
    n, d = x2.shape
    tm = TOK_TILE
    tpb = seq // tm
    tri = jnp.tril(jnp.ones((tm, tm), BF16))
    pq, pk, bq, bk = _bias_lane_placement()
    tok = lambda w: pl.BlockSpec((tm, w), lambda i: (i, 0))
    const = lambda shape: pl.BlockSpec(shape, lambda i: (0,) * len(shape))
    qk_width = 2 * FOX_WIDTH
    return pl.pallas_call(
        functools.partial(_inproj_kernel, tpb),
        grid=(n // tm,),
        in_specs=[tok(d),
                  pl.BlockSpec((None, N_MOD, d), lambda i: (i // tpb, 0, 0)),
                  const((1, d)), const((d, _C_END)), const((FOX_WIDTH, d)), const((1, LANES)),
                  const((tm, tm)), const(pq.shape), const(pk.shape), const(bq.shape), const(bk.shape)],
        out_specs=[tok(qk_width), tok(qk_width), pl.BlockSpec((FOX_WIDTH, tm), lambda i: (0, i)),
                   tok(SSM_WIDTH),
                   pl.BlockSpec((SSM_GROUPS, tm // SSM_CHUNK, SSM_CHUNK * SSM_GROUP), lambda i: (0, i, 0)),
                   tok(d), tok(d)],
        out_shape=[jax.ShapeDtypeStruct((n, qk_width), BF16)] * 2
        + [jax.ShapeDtypeStruct((FOX_WIDTH, n), BF16)]
        + [jax.ShapeDtypeStruct((n, SSM_WIDTH), F32)]
        + [jax.ShapeDtypeStruct((SSM_GROUPS, n // SSM_CHUNK, SSM_CHUNK * SSM_GROUP), BF16)]
        + [jax.ShapeDtypeStruct((n, d), BF16)] * 2,
        scratch_shapes=[pltpu.VMEM((SUBLANES, LANES), F32),
                        pltpu.VMEM((SSM_WIDTH // LANES, tm, LANES), F32)],
        compiler_params=_params(("arbitrary",)),
        name="inproj",
    )(x2, mod3, gain, w_all, w_vt, bf_pad, tri, jnp.asarray(pq, BF16), jnp.asarray(pk, BF16),
      jnp.asarray(bq), jnp.asarray(bk))


def _attn_kernel(q_ref, k_ref, vt_ref, o_ref, m_ref, acc_ref, sa_ref, sb_ref):
    tq, tk = ATT_Q_TILE, ATT_K_TILE
    seq = q_ref.shape[0]
    nq = seq // tq
    half = FOX_HEAD_DIM
    lane = lax.broadcasted_iota(jnp.int32, (1, 2 * LANES), 1)
    bias0 = LANES + 16 * pl.program_id(1)
    own0 = (lane < half) | ((lane >= bias0) & (lane < bias0 + 8))
    own1 = ((lane >= half) & (lane < LANES)) | ((lane >= bias0 + 8) & (lane < bias0 + 16))
    ones_rows = jnp.ones((2 * SUBLANES, tk), BF16)
    key_in_tile = lax.broadcasted_iota(jnp.int32, (tk, 2 * tq), 0)
    qry_in_tile = lax.broadcasted_iota(jnp.int32, (tk, 2 * tq), 1) & (tq - 1)
    bufs = (sa_ref, sb_ref)
    key_tiles = [-(-(i + 1) * tq // tk) for i in range(nq)]
    steps = [(i, j) for i in range(nq) for j in range(key_tiles[i])]
    q_cache = {}

    def q_both(i):
        if i not in q_cache:
            q = q_ref[i * tq:(i + 1) * tq, :]
            zq = jnp.zeros_like(q)
            q_cache[i] = jnp.concatenate([jnp.where(own0, q, zq), jnp.where(own1, q, zq)], axis=0)
        return q_cache[i]

    def scores(n):
        i, j = steps[n]
        s = lax.dot_general(k_ref[j * tk:(j + 1) * tk, :], q_both(i), (((1,), (1,)), ((), ())),
                            preferred_element_type=F32)
        if j * tk + tk - 1 > i * tq:
            s = jnp.where(key_in_tile + (j * tk - i * tq) <= qry_in_tile, s, NEG_BIG)
        bufs[n % 2][...] = s

    scores(0)
    for n, (i, j) in enumerate(steps):
        if n + 1 < len(steps):
            scores(n + 1)
        s_ref = bufs[n % 2]
        va = jnp.concatenate([vt_ref[:, j * tk:(j + 1) * tk], ones_rows], axis=0)
        for h in range(2):
            cols = slice(h * tq, (h + 1) * tq)
            if j == 0:
                m_new = jnp.max(s_ref[:, cols], axis=0, keepdims=True)
                p = jnp.exp(s_ref[:, cols] - m_new).astype(BF16)
                acc_ref[:, cols] = jnp.dot(va, p, preferred_element_type=F32)
            else:
                m_old = m_ref[:, cols]
                m_new = jnp.maximum(m_old, jnp.max(s_ref[:, cols], axis=0, keepdims=True))
                alpha = jnp.exp(m_old - m_new)
                p = jnp.exp(s_ref[:, cols] - m_new).astype(BF16)
                acc_ref[:, cols] = alpha * acc_ref[:, cols] + jnp.dot(va, p, preferred_element_type=F32)
            m_ref[:, cols] = m_new
        if j == key_tiles[i] - 1:
            acc = acc_ref[...]
            o_t = jnp.concatenate([acc[0:half, 0:tq] / acc[LANES:LANES + 1, 0:tq],
                                   acc[half:LANES, tq:2 * tq] / acc[LANES:LANES + 1, tq:2 * tq]], axis=0)
            o_ref[i * tq:(i + 1) * tq, :] = o_t.T.astype(o_ref.dtype)


def _attention(q, k, v_t, bsz, seq):
    n = q.shape[0]
    t = ATT_Q_TILE
    return pl.pallas_call(
        _attn_kernel,
        grid=(bsz, HEAD_PAIRS),
        in_specs=[pl.BlockSpec((seq, 2 * LANES), lambda b, p: (b, p)),
                  pl.BlockSpec((seq, 2 * LANES), lambda b, p: (b, p)),
                  pl.BlockSpec((LANES, seq), lambda b, p: (p, b))],
        out_specs=pl.BlockSpec((seq, LANES), lambda b, p: (b, p)),
        out_shape=jax.ShapeDtypeStruct((n, FOX_WIDTH), BF16),
        scratch_shapes=[pltpu.VMEM((1, 2 * t), F32), pltpu.VMEM((LANES + 2 * SUBLANES, 2 * t), F32),
                        pltpu.VMEM((ATT_K_TILE, 2 * t), F32), pltpu.VMEM((ATT_K_TILE, 2 * t), F32)],
        compiler_params=_params(("arbitrary", "arbitrary")),
        name="attn",
    )(q, k, v_t)


def _ssm_prep_kernel(lrow_ref, lcol_ref, ldt_ref, btr_ref, bti_ref, ctr_ref, cti_ref,
                     toep_ref, bst_ref, bsw_ref, cpw_ref, a_ref):
    p8 = pl.program_id(0) % 8
    t_len, grp = SSM_CHUNK, SSM_GROUP
    dt = jnp.exp(ldt_ref[...])
    lr, li = jnp.minimum(lrow_ref[0:1, :], LAMBDA_RE_MAX), lrow_ref[1:2, :]

    def powers(steps, re, im):
        mag = jnp.exp(steps * (re * dt))
        return mag * jnp.cos(steps * (im * dt)), mag * jnp.sin(steps * (im * dt))

    a_re, a_im = powers(1.0, lr, li)
    den = lr * lr + li * li
    nr = a_re - 1.0
    co_re = (nr * lr + a_im * li) / den
    co_im = (a_im * lr - nr * li) / den
    bbt_re = co_re * btr_ref[...] - co_im * bti_ref[...]
    bbt_im = co_re * bti_ref[...] + co_im * btr_ref[...]

    lag = (lax.broadcasted_iota(jnp.int32, (1, t_len * grp), 1) // grp).astype(F32)
    lcr, lci = jnp.minimum(lcol_ref[:, 0:1], LAMBDA_RE_MAX), lcol_ref[:, 1:2]

    def c_times_power(steps):
        p_re, p_im = powers(steps, lcr, lci)
        return (ctr_ref[...] * p_re - cti_ref[...] * p_im, ctr_ref[...] * p_im + cti_ref[...] * p_re)

    wt_re, wt_im = c_times_power(lag)
    kern = (jnp.dot(bbt_re, wt_re, precision=HIGHEST, preferred_element_type=F32)
            - jnp.dot(bbt_im, wt_im, precision=HIGHEST, preferred_element_type=F32))

    lane = lax.broadcasted_iota(jnp.int32, (1, LANES), 1)
    col_shift = p8 * grp

    def store_cols(ref, rows, lo_half, hi_half):
        ref[rows, 0:LANES] = pltpu.roll(lo_half, col_shift, axis=1).astype(ref.dtype)
        ref[rows, LANES:2 * LANES] = pltpu.roll(hi_half, col_shift, axis=1).astype(ref.dtype)

    def slot_rows(s):
        half, s8 = divmod(s, 8)
        return pl.ds(pl.multiple_of((8 * half + (s8 + p8) % 8) * grp, grp), grp)

    back = (t_len - 1 - lax.broadcasted_iota(jnp.int32, (t_len, 1), 0)).astype(F32)
    e_re, e_im = powers(back, lr, li)
    zero = jnp.zeros((grp, LANES), F32)
    k_lo, k_hi = kern[:, 0:LANES], kern[:, LANES:2 * LANES]
    for s in range(t_len):
        half, s8 = divmod(s, 8)
        keep = lane >= s8 * grp
        r_lo = pltpu.roll(k_lo, s8 * grp, axis=1) if s8 else k_lo
        r_hi = pltpu.roll(k_hi, s8 * grp, axis=1) if s8 else k_hi
        if half == 0:
            lo, hi = jnp.where(keep, r_lo, 0.0), jnp.where(keep, r_hi, r_lo)
        else:
            lo, hi = zero, jnp.where(keep, r_lo, 0.0)
        store_cols(toep_ref, slot_rows(s), lo, hi)
        es_re, es_im = e_re[s:s + 1, :], e_im[s:s + 1, :]
        bs_re = es_re * bbt_re - es_im * bbt_im
        bs_im = es_re * bbt_im + es_im * bbt_re
        bst_ref[slot_rows(s), :] = jnp.concatenate([bs_re, bs_im], axis=1).astype(bst_ref.dtype)
        bsw_ref[slot_rows(s), :] = jnp.concatenate([bs_im, bs_re], axis=1).astype(bsw_ref.dtype)

    w1_re, w1_im = c_times_power(lag + 1.0)
    store_cols(cpw_ref, pl.ds(0, SSM_STATE), w1_re[:, 0:LANES], w1_re[:, LANES:2 * LANES])
    store_cols(cpw_ref, pl.ds(SSM_STATE, SSM_STATE), -w1_im[:, 0:LANES], -w1_im[:, LANES:2 * LANES])
    s_re, s_im = powers(float(t_len), lr, li)
    a_ref[0:1, :] = jnp.concatenate([s_re, s_re], axis=1)
    a_ref[1:2, :] = jnp.concatenate([-s_im, s_im], axis=1)


def _ssm_prep(lambda_re, lambda_im, log_dt, b_re, b_im, c_re, c_im):
    width = SSM_CHUNK * SSM_GROUP
    lam_row = jnp.stack([lambda_re, lambda_im], axis=1)
    tiled = lambda c: jnp.tile(c.transpose(0, 2, 1), (1, 1, SSM_CHUNK))
    per = lambda a, b: pl.BlockSpec((None, a, b), lambda g: (g, 0, 0))
    return pl.pallas_call(
        _ssm_prep_kernel,
        grid=(SSM_GROUPS,),
        in_specs=[per(2, SSM_STATE), per(SSM_STATE, 2), per(1, 1), per(SSM_GROUP, SSM_STATE),
                  per(SSM_GROUP, SSM_STATE), per(SSM_STATE, width), per(SSM_STATE, width)],
        out_specs=[per(width, width), per(width, 2 * SSM_STATE), per(width, 2 * SSM_STATE),
                   per(2 * SSM_STATE, width), per(2, 2 * SSM_STATE)],
        out_shape=[jax.ShapeDtypeStruct((SSM_GROUPS, width, width), BF16),
                   jax.ShapeDtypeStruct((SSM_GROUPS, width, 2 * SSM_STATE), BF16),
                   jax.ShapeDtypeStruct((SSM_GROUPS, width, 2 * SSM_STATE), BF16),
                   jax.ShapeDtypeStruct((SSM_GROUPS, 2 * SSM_STATE, width), BF16),
                   jax.ShapeDtypeStruct((SSM_GROUPS, 2, 2 * SSM_STATE), F32)],
        compiler_params=_params(("arbitrary",)),
        name="ssm_prep",
    )(lam_row, lam_row.transpose(0, 2, 1), log_dt.reshape(SSM_GROUPS, 1, 1),
      b_re.transpose(0, 2, 1), b_im.transpose(0, 2, 1), tiled(c_re), tiled(c_im))


SSM_GROUPS_PER_STEP = 4


def _ssm_kernel(n_chunks, bsz, u_ref, toep_ref, bst_ref, bsw_ref, cpw_ref, a_ref, y_ref,
                contrib_ref, cswap_ref, xprev_ref):
    groups = u_ref.shape[0]
    for k in range(groups):
        u = u_ref[k]
        contrib_ref[k] = jnp.dot(u, bst_ref[k], preferred_element_type=F32)
        cswap_ref[k] = jnp.dot(u, bsw_ref[k], preferred_element_type=F32)
    a1 = [a_ref[k, 0:1, :] for k in range(groups)]
    a2 = [a_ref[k, 1:2, :] for k in range(groups)]

    def step(n, carry):
        rows = pl.ds(n, bsz, stride=n_chunks)
        new = []
        for k in range(groups):
            x, xs = carry[2 * k], carry[2 * k + 1]
            xprev_ref[k, rows, :] = x
            new.append(a1[k] * x + a2[k] * xs + contrib_ref[k, rows, :])
            new.append(a1[k] * xs - a2[k] * x + cswap_ref[k, rows, :])
        return tuple(new)

    zero = jnp.zeros((bsz, 2 * SSM_STATE), F32)
    lax.fori_loop(0, n_chunks, step, (zero,) * (2 * groups), unroll=2)
    for k in range(groups):
        y_ref[k] = (jnp.dot(u_ref[k], toep_ref[k], preferred_element_type=F32)
                    + jnp.dot(xprev_ref[k].astype(BF16), cpw_ref[k], preferred_element_type=F32))


def _ssm(u_flat, toep, b_state, b_swap, c_pow, a_step, bsz):
    g, rows, w = u_flat.shape
    gb = SSM_GROUPS_PER_STEP
    per = lambda a, b: pl.BlockSpec((gb, a, b), lambda i: (i, 0, 0))
    state = pltpu.VMEM((gb, rows, 2 * SSM_STATE), F32)
    return pl.pallas_call(
        functools.partial(_ssm_kernel, rows // bsz, bsz),
        grid=(g // gb,),
        in_specs=[per(rows, w), per(w, w), per(w, 2 * SSM_STATE), per(w, 2 * SSM_STATE),
                  per(2 * SSM_STATE, w), per(2, 2 * SSM_STATE)],
        out_specs=per(rows, w),
        out_shape=jax.ShapeDtypeStruct((g, rows, w), F32),
        scratch_shapes=[state, state, state],
        compiler_params=_params(("arbitrary",)),
        name="ssm",
    )(u_flat, toep, b_state, b_swap, c_pow, a_step)


ROW_SLABS = D_MODEL // LANES // 2
_HIGH_HALF = 0xFFFF0000


def _store_row_tiles(ref, value):
    rows = value.shape[0]
    bits = lax.bitcast_convert_type(value.astype(BF16).astype(F32), jnp.uint32)
    for j in range(ROW_SLABS):
        low = bits[:, j * LANES:(j + 1) * LANES] >> 16
        high = bits[:, (j + ROW_SLABS) * LANES:(j + ROW_SLABS + 1) * LANES] & jnp.uint32(_HIGH_HALF)
        ref[pl.ds(j, rows, stride=ROW_SLABS), :] = high | low


def _load_row_tiles(ref, rows):
    words = [ref[pl.ds(j, rows, stride=ROW_SLABS), :] for j in range(ROW_SLABS)]
    low = [lax.bitcast_convert_type(w << 16, F32) for w in words]
    high = [lax.bitcast_convert_type(w & jnp.uint32(_HIGH_HALF), F32) for w in words]
    return jnp.concatenate(low + high, axis=1)


def _row_tile_copy(src_ref, src_row, dst_ref, dst_row, sem):
    src = src_ref.at[pl.ds(pl.multiple_of(src_row * ROW_SLABS, ROW_SLABS), ROW_SLABS), :]
    dst = dst_ref.at[pl.ds(pl.multiple_of(dst_row * ROW_SLABS, ROW_SLABS), ROW_SLABS), :]
    return pltpu.make_async_copy(src, dst, sem)


def _mix_kernel(x_ref, of_ref, yf_ref, u_ref, ga_ref, gb_ref, mod_ref, dsk_ref, wglu_ref, wfox_ref,
                wssm_ref, wo_ref, g2_ref, wr_ref, br_ref, x1_ref, h2_ref, lg_ref, ytok_ref):
    sub = MIX_SUB_TILE
    for s in range(x_ref.shape[0] // sub):
        rows = pl.ds(s * sub, sub)
        chunks = pl.ds(s * sub // SSM_CHUNK, sub // SSM_CHUNK)
        _to_token_major(yf_ref.at[:, chunks, :], ytok_ref.at[s], sub // SSM_CHUNK)
        y_ssm = jnp.concatenate([ytok_ref[s, j] for j in range(SSM_WIDTH // LANES)], axis=1)
        y = y_ssm + dsk_ref[...] * u_ref[rows, :]
        y = 0.5 * y * (1.0 + jnp.tanh(math.sqrt(2.0 / math.pi) * (y + 0.044715 * (y * y * y))))
        gl = jnp.dot(y.astype(BF16), wglu_ref[...], preferred_element_type=F32)
        o_ssm = gl[:, :SSM_WIDTH] * _sigmoid(gl[:, SSM_WIDTH:])
        merged = (ga_ref[rows, :].astype(F32) * jnp.dot(of_ref[rows, :], wfox_ref[...],
                                                        preferred_element_type=F32)
                  + gb_ref[rows, :].astype(F32) * jnp.dot(o_ssm.astype(BF16), wssm_ref[...],
                                                          preferred_element_type=F32))
        x1 = x_ref[rows, :] + mod_ref[2:3, :] * jnp.dot(merged.astype(BF16), wo_ref[...],
                                                         preferred_element_type=F32)
        x1_ref[rows, :] = x1
        h2 = _rms_modulate(x1, g2_ref[...], mod_ref[3:4, :], mod_ref[4:5, :])
        _store_row_tiles(h2_ref.at[pl.ds(s * sub * ROW_SLABS, sub * ROW_SLABS), :], h2)
        a1 = _top_bits(h2)
        a2 = _top_bits(h2 - a1)
        lhs = jnp.concatenate([a1, a1, a2], axis=1).astype(BF16)
        lg_ref[rows, :] = jnp.dot(lhs, wr_ref[...], preferred_element_type=F32) + br_ref[...]


def _mix(x2, o_fox, y_flat, u, sga, sgb, mod3, d_skip, w_glu, w_fox, w_ssm, w_o, g2, w_r, b_r, seq):
    n, d = x2.shape
    tm = MIX_TILE
    tpb = seq // tm
    tok = lambda w: pl.BlockSpec((tm, w), lambda i: (i, 0))
    const = lambda a: pl.BlockSpec(a.shape, lambda i: (0,) * a.ndim)
    flat = pl.BlockSpec((SSM_GROUPS, tm // SSM_CHUNK, SSM_CHUNK * SSM_GROUP), lambda i: (0, i, 0))
    return pl.pallas_call(
        _mix_kernel,
        grid=(n // tm,),
        in_specs=[tok(d), tok(FOX_WIDTH), flat, tok(SSM_WIDTH), tok(d), tok(d),
                  pl.BlockSpec((None, N_MOD, d), lambda i: (i // tpb, 0, 0)),
                  const(d_skip), const(w_glu), const(w_fox), const(w_ssm), const(w_o), const(g2),
                  const(w_r), const(b_r)],
        out_specs=[tok(d), pl.BlockSpec((tm * ROW_SLABS, LANES), lambda i: (i, 0)), tok(LANES)],
        out_shape=[jax.ShapeDtypeStruct((n, d), F32), jax.ShapeDtypeStruct((n * ROW_SLABS, LANES), jnp.uint32),
                   jax.ShapeDtypeStruct((n, LANES), F32)],
        scratch_shapes=[pltpu.VMEM((tm // MIX_SUB_TILE, SSM_WIDTH // LANES, MIX_SUB_TILE, LANES), F32)],
        compiler_params=_params(("arbitrary",)),
        name="mix",
    )(x2, o_fox, y_flat, u, sga, sgb, mod3, d_skip, w_glu, w_fox, w_ssm, w_o, g2, w_r, b_r)


def _route_kernel(lg_ref, tri_ref, er_ref, wt_ref, cnt_ref, carry_ref):
    i = pl.program_id(0)

    @pl.when(i == 0)
    def _():
        carry_ref[...] = jnp.zeros_like(carry_ref)

    lg = lg_ref[...].T
    tm = lg.shape[1]
    row = lax.broadcasted_iota(jnp.int32, (LANES, tm), 0)
    neg = jnp.full_like(lg, -jnp.inf)

    def first_argmax(vals):
        mx = jnp.max(vals, axis=0, keepdims=True)
        ix = jnp.min(jnp.where(vals == mx, row, LANES), axis=0, keepdims=True)
        return mx, ix

    is_group = row < N_GROUPS
    g_max, gi = first_argmax(jnp.where(is_group, lg, neg))
    g_sum = jnp.sum(jnp.where(is_group, jnp.exp(lg - g_max), 0.0), axis=0, keepdims=True)
    p_group = 1.0 / g_sum
    lo = N_GROUPS + EXPERTS_PER_GROUP * gi
    in_group = (row >= lo) & (row < lo + EXPERTS_PER_GROUP)
    cand = jnp.where(in_group, lg, neg)
    v1, i1 = first_argmax(cand)
    v2, i2 = first_argmax(jnp.where(row == i1, neg, cand))
    tt = jnp.exp(v2 - v1)
    w1 = p_group / (1.0 + tt)
    w2 = p_group * tt / (1.0 + tt)
    e1 = i1 - N_GROUPS
    e2 = i2 - N_GROUPS
    sel1 = row == e1
    sel2 = row == e2
    onehot = (sel1 | sel2).astype(F32)
    before = jnp.dot(onehot.astype(BF16), tri_ref[...], preferred_element_type=F32) + carry_ref[:, 0:1]
    r1 = jnp.sum(jnp.where(sel1, before, 0.0), axis=0, keepdims=True).astype(jnp.int32)
    r2 = jnp.sum(jnp.where(sel2, before, 0.0), axis=0, keepdims=True).astype(jnp.int32)
    total = before[:, tm - 1:tm] + onehot[:, tm - 1:tm]
    carry_ref[...] = jnp.broadcast_to(total, carry_ref.shape)
    cnt_ref[...] = jnp.broadcast_to(total, cnt_ref.shape)
    slot = lax.broadcasted_iota(jnp.int32, (SUBLANES, tm), 0)
    er_ref[...] = jnp.where(slot == 0, e1, jnp.where(slot == 1, e2, jnp.where(slot == 2, r1, r2)))
    wt_ref[...] = jnp.where(row == 0, w1, jnp.where(row == 1, w2, 0.0)).T


def _route(logits):
    n = logits.shape[0]
    tm = TOK_TILE
    tri = jnp.triu(jnp.ones((tm, tm), BF16), k=1)
    tok = pl.BlockSpec((tm, LANES), lambda i: (i, 0))
    return pl.pallas_call(
        _route_kernel,
        grid=(n // tm,),
        in_specs=[tok, pl.BlockSpec((tm, tm), lambda i: (0, 0))],
        out_specs=[pl.BlockSpec((SUBLANES, tm), lambda i: (0, i)), tok,
                   pl.BlockSpec((LANES, LANES), lambda i: (0, 0))],
        out_shape=[jax.ShapeDtypeStruct((SUBLANES, n), jnp.int32), jax.ShapeDtypeStruct((n, LANES), F32),
                   jax.ShapeDtypeStruct((LANES, LANES), F32)],
        scratch_shapes=[pltpu.VMEM((LANES, LANES), F32)],
        compiler_params=_params(("arbitrary",)),
        name="route",
    )(logits, tri)


ISSUE_UNROLL = 8


def _dispatch_kernel(dest_ref, h_ref, rows_in_ref, rows_ref, sem):
    del rows_in_ref
    tm = h_ref.shape[0] // ROW_SLABS

    def issue(g, c):
        for j in range(ISSUE_UNROLL):
            t = g * ISSUE_UNROLL + j
            _row_tile_copy(h_ref, t, rows_ref, dest_ref[0, 0, t], sem).start(priority=0)
            _row_tile_copy(h_ref, t, rows_ref, dest_ref[0, 0, tm + t], sem).start(priority=1)
        return c

    lax.fori_loop(0, tm // ISSUE_UNROLL, issue, 0)
    for _ in range(2):
        pltpu.make_async_copy(h_ref, rows_ref.at[pl.ds(0, tm * ROW_SLABS), :], sem).wait()


def _dispatch(dest3, h2_tiles, rows_zero):
    tm = MOVE_TILE
    n = h2_tiles.shape[0] // ROW_SLABS
    return pl.pallas_call(
        _dispatch_kernel,
        grid=(n // tm,),
        in_specs=[pl.BlockSpec((1, 1, 2 * tm), lambda i: (i, 0, 0), memory_space=pltpu.SMEM),
                  pl.BlockSpec((tm * ROW_SLABS, LANES), lambda i: (i, 0)),
                  pl.BlockSpec(memory_space=pl.ANY)],
        out_specs=pl.BlockSpec(memory_space=pl.ANY),
        out_shape=jax.ShapeDtypeStruct(rows_zero.shape, rows_zero.dtype),
        scratch_shapes=[pltpu.SemaphoreType.DMA(())],
        input_output_aliases={2: 0},
        compiler_params=_params(("arbitrary",)),
        name="dispatch",
    )(dest3, h2_tiles, rows_zero)


def _combine_kernel(n_steps, dest_ref, dnext_ref, x1_ref, wt_ref, mod_ref, gf_ref, yr_ref, o_ref,
                    buf_ref, sem):
    i = pl.program_id(0)
    tm = x1_ref.shape[0]

    def gather(idx_ref, which):
        def issue(g, c):
            for j in range(ISSUE_UNROLL):
                t = g * ISSUE_UNROLL + j
                _row_tile_copy(yr_ref, idx_ref[0, 0, t], buf_ref.at[which, 0], t,
                               sem.at[which]).start(priority=0)
                _row_tile_copy(yr_ref, idx_ref[0, 0, tm + t], buf_ref.at[which, 1], t,
                               sem.at[which]).start(priority=1)
            return c

        lax.fori_loop(0, tm // ISSUE_UNROLL, issue, 0)

    cur = i % 2

    @pl.when(i == 0)
    def _():
        gather(dest_ref, 0)

    @pl.when(i + 1 < n_steps)
    def _():
        gather(dnext_ref, 1 - cur)

    for slot in range(2):
        pltpu.make_async_copy(yr_ref.at[pl.ds(0, tm * ROW_SLABS), :], buf_ref.at[cur, slot],
                              sem.at[cur]).wait()
    wt = wt_ref[...]
    moe = (wt[:, 0:1] * _load_row_tiles(buf_ref.at[cur, 0], tm)
           + wt[:, 1:2] * _load_row_tiles(buf_ref.at[cur, 1], tm))
    x = x1_ref[...] + mod_ref[5:6, :] * moe
    ms = jnp.mean(x * x, axis=-1, keepdims=True)
    o_ref[...] = (x * lax.rsqrt(ms + RMS_EPS)) * gf_ref[...]


def _combine(dest3, x1, wts, mod3, final_g, y_rows, seq):
    n, d = x1.shape
    tm = MOVE_TILE
    tpb = seq // tm
    n_steps = n // tm
    idx_spec = lambda f: pl.BlockSpec((1, 1, 2 * tm), f, memory_space=pltpu.SMEM)
    return pl.pallas_call(
        functools.partial(_combine_kernel, n_steps),
        grid=(n_steps,),
        in_specs=[idx_spec(lambda i: (i, 0, 0)),
                  idx_spec(lambda i: (jnp.minimum(i + 1, n_steps - 1), 0, 0)),
                  pl.BlockSpec((tm, d), lambda i: (i, 0)),
                  pl.BlockSpec((tm, LANES), lambda i: (i, 0)),
                  pl.BlockSpec((None, N_MOD, d), lambda i: (i // tpb, 0, 0)),
                  pl.BlockSpec((1, d), lambda i: (0, 0)),
                  pl.BlockSpec(memory_space=pl.ANY)],
        out_specs=pl.BlockSpec((tm, d), lambda i: (i, 0)),
        out_shape=jax.ShapeDtypeStruct((n, d), F32),
        scratch_shapes=[pltpu.VMEM((2, 2, tm * ROW_SLABS, LANES), jnp.uint32),
                        pltpu.SemaphoreType.DMA((2,))],
        compiler_params=_params(("arbitrary",)),
        name="combine",
    )(dest3, dest3, x1, wts, mod3, final_g, y_rows)


def _expert_kernel(be_ref, seg_ref, nxt_ref, nv_ref, x_ref, wg_hbm, wu_hbm, wd_hbm, y_ref,
                   wg_buf, wu_buf, wd_buf, wgb_ref, wub_ref, wdb_ref, sem):
    i = pl.program_id(0)
    valid = i < nv_ref[0]
    first = (i == 0) | (be_ref[i] != be_ref[jnp.maximum(i - 1, 0)])
    slot = seg_ref[i] % 2

    def weight_copies(e, s):
        return [pltpu.make_async_copy(hbm.at[e], buf.at[s], sem.at[s])
                for hbm, buf in ((wg_hbm, wg_buf), (wu_hbm, wu_buf), (wd_hbm, wd_buf))]

    @pl.when(valid & (i == 0))
    def _():
        for c in weight_copies(be_ref[0], 0):
            c.start()

    @pl.when(valid & first)
    def _():
        for c in weight_copies(be_ref[i], slot):
            c.wait()

        @pl.when(nxt_ref[i] >= 0)
        def _():
            for c in weight_copies(nxt_ref[i], 1 - slot):
                c.start()

        wgb_ref[...] = wg_buf[slot].astype(BF16)
        wub_ref[...] = wu_buf[slot].astype(BF16)
        wdb_ref[...] = wd_buf[slot].astype(BF16)

    @pl.when(valid)
    def _():
        xb = _load_row_tiles(x_ref, x_ref.shape[0] // ROW_SLABS).astype(BF16)
        a = jnp.dot(xb, wgb_ref[...], preferred_element_type=F32)
        b = jnp.dot(xb, wub_ref[...], preferred_element_type=F32)
        hid = (a * _sigmoid(a)) * b
        _store_row_tiles(y_ref, jnp.dot(hid.astype(BF16), wdb_ref[...], preferred_element_type=F32))

    @pl.when(jnp.logical_not(valid))
    def _():
        y_ref[...] = jnp.zeros_like(y_ref)


def _experts(blk_e, n_valid, x_rows, w_gate, w_up, w_down):
    d = D_MODEL
    rows = x_rows.shape[0] // ROW_SLABS
    tb = ROW_BLOCK
    n_blocks = rows // tb
    idx = jnp.arange(n_blocks, dtype=jnp.int32)
    change = (idx == 0) | (blk_e != jnp.roll(blk_e, 1))
    seg = jnp.cumsum(change.astype(jnp.int32)) - 1
    later_start = (idx[None, :] > idx[:, None]) & change[None, :] & (idx[None, :] < n_valid[0])
    none = jnp.int32(N_EXPERTS)
    nxt = jnp.min(jnp.where(later_start, blk_e[None, :], none), axis=1)
    nxt = jnp.where(nxt == none, -1, nxt)
    row_spec = pl.BlockSpec((tb * ROW_SLABS, LANES), lambda i, *_: (i, 0))
    grid_spec = pltpu.PrefetchScalarGridSpec(
        num_scalar_prefetch=4,
        grid=(n_blocks,),
        in_specs=[row_spec, pl.BlockSpec(memory_space=pl.ANY), pl.BlockSpec(memory_space=pl.ANY),
                  pl.BlockSpec(memory_space=pl.ANY)],
        out_specs=row_spec,
        scratch_shapes=[pltpu.VMEM((2, d, D_EXPERT), F32), pltpu.VMEM((2, d, D_EXPERT), F32),
                        pltpu.VMEM((2, D_EXPERT, d), F32),
                        pltpu.VMEM((d, D_EXPERT), BF16), pltpu.VMEM((d, D_EXPERT), BF16),
                        pltpu.VMEM((D_EXPERT, d), BF16), pltpu.SemaphoreType.DMA((2,))],
    )
    return pl.pallas_call(
        _expert_kernel,
        grid_spec=grid_spec,
        out_shape=jax.ShapeDtypeStruct(x_rows.shape, x_rows.dtype),
        compiler_params=_params(("arbitrary",)),
        name="experts",
    )(blk_e, seg, nxt.astype(jnp.int32), n_valid, x_rows, w_gate, w_up, w_down)


def kernel(x, c, w_ada, b_ada, norm_mix_g, w_in, b_forget, w_out_fox, lambda_re, lambda_im, log_dt,
           ssm_b_re, ssm_b_im, ssm_c_re, ssm_c_im, d_skip, w_glu, w_out_ssm, w_o, norm_ffn_g,
           w_router_group, b_router_group, w_router_expert, b_router_expert, w_gate_e, w_up_e,
           w_down_e, final_g):
    bsz, seq, d = x.shape
    n = bsz * seq
    assert w_ada.shape[0] == 1, "the final RMSNorm is fused into the (single) layer's combine kernel"
    xc = x.reshape(n, d)
    for l in range(1):
        mod3 = _mod(c, w_ada[l], b_ada[l]).reshape(bsz, N_MOD, d)

        wi = w_in[l]
        s_q, s_k, s_v, s_f, s_u, s_ga = 512, 1024, 1536, 1544, 2056, 3080
        scale = FOX_HEAD_DIM ** -0.5
        w_all = jnp.concatenate(
            [wi[:, :s_q] * scale, wi[:, s_q:s_k], wi[:, s_f:s_u], wi[:, s_u:s_ga],
             wi[:, s_ga:], jnp.pad(wi[:, s_v:s_f], ((0, 0), (0, LANES - FOX_HEADS)))],
            axis=1).astype(BF16)
        w_vt = wi[:, s_k:s_v].T.astype(BF16)
        bf_pad = jnp.pad(b_forget[l], (0, LANES - FOX_HEADS)).reshape(1, LANES)
        q, k, v_t, u, u_flat, sga, sgb = _inproj(xc, mod3, norm_mix_g[l].reshape(1, d), w_all, w_vt,
                                                 bf_pad, seq)

        o_fox = _attention(q, k, v_t, bsz, seq)

        toep, b_state, b_swap, c_pow, a_step = _ssm_prep(
            lambda_re[l], lambda_im[l], log_dt[l], ssm_b_re[l], ssm_b_im[l], ssm_c_re[l], ssm_c_im[l])
        y_flat = _ssm(u_flat, toep, b_state, b_swap, c_pow, a_step, bsz)

        w_r = jnp.pad(jnp.concatenate([w_router_group[l], w_router_expert[l]], axis=1),
                      ((0, 0), (0, LANES - N_GROUPS - N_EXPERTS)))
        w_r1 = _top_bits(w_r)
        w_r2 = _top_bits(w_r - w_r1)
        w_r = jnp.concatenate([w_r1, w_r2, w_r1], axis=0).astype(BF16)
        b_r = jnp.pad(jnp.concatenate([b_router_group[l], b_router_expert[l]]),
                      (0, LANES - N_GROUPS - N_EXPERTS)).reshape(1, LANES)
        x1, h2, logits = _mix(xc, o_fox, y_flat, u, sga, sgb, mod3, d_skip[l].reshape(1, SSM_WIDTH),
                              w_glu[l].astype(BF16), w_out_fox[l].astype(BF16),
                              w_out_ssm[l].astype(BF16), w_o[l].astype(BF16),
                              norm_ffn_g[l].reshape(1, d), w_r, b_r, seq)

        er, wts, cnt = _route(logits)
        counts = cnt[:N_EXPERTS, 0].astype(jnp.int32)
        pcounts = ((counts + ROW_BLOCK - 1) // ROW_BLOCK) * ROW_BLOCK
        pends = jnp.cumsum(pcounts)
        pstarts = pends - pcounts
        hit = er[0:2, None, :] == jnp.arange(N_EXPERTS, dtype=jnp.int32)[None, :, None]
        dest = jnp.sum(jnp.where(hit, pstarts[None, :, None], 0), axis=1) + er[2:4]
        rows = 2 * n + N_EXPERTS * ROW_BLOCK
        n_blocks = rows // ROW_BLOCK
        blk_start = jnp.arange(n_blocks, dtype=jnp.int32) * ROW_BLOCK
        blk_e = jnp.minimum(jnp.sum((pends[None, :] <= blk_start[:, None]).astype(jnp.int32), axis=1),
                            N_EXPERTS - 1)
        n_valid = (pends[-1:] // ROW_BLOCK).astype(jnp.int32)
        dest3 = (dest.astype(jnp.int32).reshape(2, n // MOVE_TILE, MOVE_TILE).transpose(1, 0, 2)
                 .reshape(n // MOVE_TILE, 1, 2 * MOVE_TILE))

        x_rows = _dispatch(dest3, h2, jnp.zeros((rows * ROW_SLABS, LANES), jnp.uint32))
        y_rows = _experts(blk_e, n_valid, x_rows, w_gate_e[l], w_up_e[l], w_down_e[l])
        xc = _combine(dest3, x1, wts, mod3, final_g.reshape(1, d), y_rows, seq)
    return xc.reshape(bsz, seq, d)
```

```python
import functools
import math

import jax
import jax.numpy as jnp
import numpy as np
from jax import lax
from jax.experimental import pallas as pl
from jax.experimental.pallas import tpu as pltpu

F32 = jnp.float32
BF16 = jnp.bfloat16

D_MODEL = 1024
N_MOD = 6
RMS_EPS = 1e-6
FOX_HEADS = 8
FOX_HEAD_DIM = 64
FOX_WIDTH = FOX_HEADS * FOX_HEAD_DIM
HEAD_PAIRS = FOX_HEADS // 2
SSM_WIDTH = 512
SSM_GROUP = 16
SSM_GROUPS = SSM_WIDTH // SSM_GROUP
SSM_STATE = 64
LAMBDA_RE_MAX = -1e-4
N_GROUPS = 4
EXPERTS_PER_GROUP = 8
N_EXPERTS = N_GROUPS * EXPERTS_PER_GROUP
D_EXPERT = 512

LANES = 128
SUBLANES = 8
VMEM_LIMIT = 56 * 1024 * 1024

SSM_CHUNK = 16
TOK_TILE = 512
MIX_TILE = 512
MIX_SUB_TILE = 256
ATT_Q_TILE = 512
ATT_K_TILE = 512
ROW_BLOCK = 512
MOVE_TILE = 512
NEG_BIG = -1e30

HIGHEST = lax.Precision.HIGHEST


def _params(sem):
    return pltpu.CompilerParams(dimension_semantics=sem, vmem_limit_bytes=VMEM_LIMIT)


def _sigmoid(x):
    return 0.5 * jnp.tanh(0.5 * x) + 0.5


def _rms_modulate(x, gain, shift, scale):
    ms = jnp.mean(x * x, axis=-1, keepdims=True)
    return (x * lax.rsqrt(ms + RMS_EPS)) * gain * (1.0 + scale) + shift


def _mod_kernel(c_ref, w_ref, b_ref, o_ref):
    c = c_ref[...]
    ca = (c * jax.nn.sigmoid(c)).astype(BF16)
    o_ref[...] = jnp.dot(ca, w_ref[...].astype(BF16), preferred_element_type=F32) + b_ref[...]


def _mod(c, w_ada, b_ada):
    bsz, d = c.shape
    cols = w_ada.shape[1]
    tn = 1536
    return pl.pallas_call(
        _mod_kernel,
        grid=(cols // tn,),
        in_specs=[pl.BlockSpec((bsz, d), lambda j: (0, 0)),
                  pl.BlockSpec((d, tn), lambda j: (0, j)),
                  pl.BlockSpec((1, tn), lambda j: (0, j))],
        out_specs=pl.BlockSpec((bsz, tn), lambda j: (0, j)),
        out_shape=jax.ShapeDtypeStruct((bsz, cols), F32),
        compiler_params=_params(("arbitrary",)),
        name="mod",
    )(c, w_ada, b_ada.reshape(1, cols))


_C_Q, _C_K, _C_U, _C_GA, _C_GB, _C_F, _C_END = 0, 512, 1024, 1536, 2560, 3584, 3712


def _lane_block():
    return lax.broadcasted_iota(jnp.int32, (1, LANES), 1) // SSM_GROUP


def _to_group_major(tok_ref, flat_ref, rows):
    blk = _lane_block()
    for half in range(2):
        for j in range(SSM_WIDTH // LANES):
            w = []
            for s8 in range(8):
                v = tok_ref[j, pl.ds(8 * half + s8, rows, stride=SSM_CHUNK), :]
                w.append(pltpu.roll(v, s8 * SSM_GROUP, axis=1) if s8 else v)
            for p in range(8):
                acc = w[0]
                for s8 in range(1, 8):
                    acc = jnp.where(blk == (p + s8) % 8, w[s8], acc)
                flat_ref[8 * j + p, :, half * LANES:(half + 1) * LANES] = acc.astype(flat_ref.dtype)


def _to_token_major(flat_ref, tok_ref, rows):
    blk = _lane_block()
    for half in range(2):
        for j in range(SSM_WIDTH // LANES):
            ys = [flat_ref[8 * j + p, :, half * LANES:(half + 1) * LANES] for p in range(8)]
            for s8 in range(8):
                acc = ys[0]
                for p in range(1, 8):
                    acc = jnp.where(blk == (p + s8) % 8, ys[p], acc)
                if s8:
                    acc = pltpu.roll(acc, LANES - s8 * SSM_GROUP, axis=1)
                tok_ref[j, pl.ds(8 * half + s8, rows, stride=SSM_CHUNK), :] = acc


def _bias_lane_placement():
    pq = np.zeros((3 * LANES, LANES), np.float32)
    pk = np.zeros((3 * LANES, LANES), np.float32)
    bq = np.zeros((1, LANES), np.float32)
    bk = np.zeros((1, LANES), np.float32)
    for head in range(FOX_HEADS):
        base = head * 8
        for term in range(3):
            pq[term * LANES + head, base + term] = 1.0
            pk[term * LANES + head, base + 3 + term] = -1.0
            bq[0, base + 3 + term] = 1.0
            bk[0, base + term] = 1.0
    return pq, pk, bq, bk


def _top_bits(a):
    bits = lax.bitcast_convert_type(a, jnp.uint32) & jnp.uint32(0xFFFF0000)
    return lax.bitcast_convert_type(bits, F32)


def _inproj_kernel(tiles_per_batch, x_ref, mod_ref, g_ref, w_ref, wvt_ref, bf_ref, tri_ref,
                   pq_ref, pk_ref, bq_ref, bk_ref,
                   q_ref, k_ref, vt_ref, u_ref, uflat_ref, ga_ref, gb_ref, carry_ref, uslab_ref):
    i = pl.program_id(0)
    h = _rms_modulate(x_ref[...], g_ref[...], mod_ref[0:1, :], mod_ref[1:2, :])
    hb = h.astype(BF16)

    def proj(a, b):
        return jnp.dot(hb, w_ref[:, a:b], preferred_element_type=F32)

    q = proj(_C_Q, _C_K).astype(BF16)
    k = proj(_C_K, _C_U).astype(BF16)
    vt_ref[...] = lax.dot_general(wvt_ref[...], hb, (((1,), (1,)), ((), ())),
                                  preferred_element_type=F32).astype(BF16)
    u = proj(_C_U, _C_GA)
    u_ref[...] = u
    for j in range(SSM_WIDTH // LANES):
        uslab_ref[j] = u[:, j * LANES:(j + 1) * LANES]
    _to_group_major(uslab_ref, uflat_ref, u.shape[0] // SSM_CHUNK)
    ga_ref[...] = _sigmoid(proj(_C_GA, _C_GB)).astype(BF16)
    gb_ref[...] = _sigmoid(proj(_C_GB, _C_F)).astype(BF16)

    f = proj(_C_F, _C_END) + bf_ref[...]
    logf = jnp.minimum(f, 0.0) - jnp.log(1.0 + jnp.exp(-jnp.abs(f)))

    @pl.when(i % tiles_per_batch == 0)
    def _():
        carry_ref[...] = jnp.zeros_like(carry_ref)

    def split3(a):
        hi = _top_bits(a)
        r1 = a - hi
        mid = _top_bits(r1)
        return jnp.concatenate([hi, mid, _top_bits(r1 - mid)], axis=1).astype(BF16)

    part = jnp.dot(tri_ref[...], split3(logf), preferred_element_type=F32)
    cs = (part[:, :LANES] + part[:, LANES:2 * LANES] + part[:, 2 * LANES:]) + carry_ref[0:1, :]
    carry_ref[...] = jnp.broadcast_to(cs[-1:, :], carry_ref.shape)

    terms = split3(cs)
    bias_q = (jnp.dot(terms, pq_ref[...], preferred_element_type=F32) + bq_ref[...]).astype(BF16)
    bias_k = (jnp.dot(terms, pk_ref[...], preferred_element_type=F32) + bk_ref[...]).astype(BF16)
    for p in range(HEAD_PAIRS):
        lanes = slice(p * LANES, (p + 1) * LANES)
        q_ref[:, 2 * p * LANES:(2 * p + 1) * LANES] = q[:, lanes]
        q_ref[:, (2 * p + 1) * LANES:(2 * p + 2) * LANES] = bias_q
        k_ref[:, 2 * p * LANES:(2 * p + 1) * LANES] = k[:, lanes]
        k_ref[:, (2 * p + 1) * LANES:(2 * p + 2) * LANES] = bias_k


def _inproj(x2, mod3, gain, w_all, w_vt, bf_pad, seq):
    n, d = x2.shape
    tm = TOK_TILE
    tpb = seq // tm
    tri = jnp.tril(jnp.ones((tm, tm), BF16))
    pq, pk, bq, bk = _bias_lane_placement()
    tok = lambda w: pl.BlockSpec((tm, w), lambda i: (i, 0))
    const = lambda shape: pl.BlockSpec(shape, lambda i: (0,) * len(shape))
    qk_width = 2 * FOX_WIDTH
    return pl.pallas_call(
        functools.partial(_inproj_kernel, tpb),
        grid=(n // tm,),
        in_specs=[tok(d),
                  pl.BlockSpec((None, N_MOD, d), lambda i: (i // tpb, 0, 0)),
                  const((1, d)), const((d, _C_END)), const((FOX_WIDTH, d)), const((1, LANES)),
                  const((tm, tm)), const(pq.shape), const(pk.shape), const(bq.shape), const(bk.shape)],
        out_specs=[tok(qk_width), tok(qk_width), pl.BlockSpec((FOX_WIDTH, tm), lambda i: (0, i)),
                   tok(SSM_WIDTH),
                   pl.BlockSpec((SSM_GROUPS, tm // SSM_CHUNK, SSM_CHUNK * SSM_GROUP), lambda i: (0, i, 0)),
                   tok(d), tok(d)],
        out_shape=[jax.ShapeDtypeStruct((n, qk_width), BF16)] * 2
        + [jax.ShapeDtypeStruct((FOX_WIDTH, n), BF16)]
        + [jax.ShapeDtypeStruct((n, SSM_WIDTH), F32)]
        + [jax.ShapeDtypeStruct((SSM_GROUPS, n // SSM_CHUNK, SSM_CHUNK * SSM_GROUP), BF16)]
        + [jax.ShapeDtypeStruct((n, d), BF16)] * 2,
        scratch_shapes=[pltpu.VMEM((SUBLANES, LANES), F32),
                        pltpu.VMEM((SSM_WIDTH // LANES, tm, LANES), F32)],
        compiler_params=_params(("arbitrary",)),
        name="inproj",
    )(x2, mod3, gain, w_all, w_vt, bf_pad, tri, jnp.asarray(pq, BF16), jnp.asarray(pk, BF16),
      jnp.asarray(bq), jnp.asarray(bk))


def _attn_kernel(q_ref, k_ref, vt_ref, o_ref, m_ref, acc_ref, sa_ref, sb_ref):
    tq, tk = ATT_Q_TILE, ATT_K_TILE
    seq = q_ref.shape[0]
    nq = seq // tq
    half = FOX_HEAD_DIM
    lane = lax.broadcasted_iota(jnp.int32, (1, 2 * LANES), 1)
    bias0 = LANES + 16 * pl.program_id(1)
    own0 = (lane < half) | ((lane >= bias0) & (lane < bias0 + 8))
    own1 = ((lane >= half) & (lane < LANES)) | ((lane >= bias0 + 8) & (lane < bias0 + 16))
    ones_rows = jnp.ones((2 * SUBLANES, tk), BF16)
    key_in_tile = lax.broadcasted_iota(jnp.int32, (tk, 2 * tq), 0)
    qry_in_tile = lax.broadcasted_iota(jnp.int32, (tk, 2 * tq), 1) & (tq - 1)
    bufs = (sa_ref, sb_ref)
    key_tiles = [-(-(i + 1) * tq // tk) for i in range(nq)]
    steps = [(i, j) for i in range(nq) for j in range(key_tiles[i])]
    q_cache = {}

    def q_both(i):
        if i not in q_cache:
            q = q_ref[i * tq:(i + 1) * tq, :]
            zq = jnp.zeros_like(q)
            q_cache[i] = jnp.concatenate([jnp.where(own0, q, zq), jnp.where(own1, q, zq)], axis=0)
        return q_cache[i]

    def scores(n):
        i, j = steps[n]
        s = lax.dot_general(k_ref[j * tk:(j + 1) * tk, :], q_both(i), (((1,), (1,)), ((), ())),
                            preferred_element_type=F32)
        if j * tk + tk - 1 > i * tq:
            s = jnp.where(key_in_tile + (j * tk - i * tq) <= qry_in_tile, s, NEG_BIG)
        bufs[n % 2][...] = s

    scores(0)
    for n, (i, j) in enumerate(steps):
        if n + 1 < len(steps):
            scores(n + 1)
        s_ref = bufs[n % 2]
        va = jnp.concatenate([vt_ref[:, j * tk:(j + 1) * tk], ones_rows], axis=0)
        for h in range(2):
            cols = slice(h * tq, (h + 1) * tq)
            if j == 0:
                m_new = jnp.max(s_ref[:, cols], axis=0, keepdims=True)
                p = jnp.exp(s_ref[:, cols] - m_new).astype(BF16)
                acc_ref[:, cols] = jnp.dot(va, p, preferred_element_type=F32)
            else:
                m_old = m_ref[:, cols]
                m_new = jnp.maximum(m_old, jnp.max(s_ref[:, cols], axis=0, keepdims=True))
                alpha = jnp.exp(m_old - m_new)
                p = jnp.exp(s_ref[:, cols] - m_new).astype(BF16)
                acc_ref[:, cols] = alpha * acc_ref[:, cols] + jnp.dot(va, p, preferred_element_type=F32)
            m_ref[:, cols] = m_new
        if j == key_tiles[i] - 1:
            acc = acc_ref[...]
            o_t = jnp.concatenate([acc[0:half, 0:tq] / acc[LANES:LANES + 1, 0:tq],
                                   acc[half:LANES, tq:2 * tq] / acc[LANES:LANES + 1, tq:2 * tq]], axis=0)
            o_ref[i * tq:(i + 1) * tq, :] = o_t.T.astype(o_ref.dtype)


def _attention(q, k, v_t, bsz, seq):
    n = q.shape[0]
    t = ATT_Q_TILE
    return pl.pallas_call(
        _attn_kernel,
        grid=(bsz, HEAD_PAIRS),
        in_specs=[pl.BlockSpec((seq, 2 * LANES), lambda b, p: (b, p)),
                  pl.BlockSpec((seq, 2 * LANES), lambda b, p: (b, p)),
                  pl.BlockSpec((LANES, seq), lambda b, p: (p, b))],
        out_specs=pl.BlockSpec((seq, LANES), lambda b, p: (b, p)),
        out_shape=jax.ShapeDtypeStruct((n, FOX_WIDTH), BF16),
        scratch_shapes=[pltpu.VMEM((1, 2 * t), F32), pltpu.VMEM((LANES + 2 * SUBLANES, 2 * t), F32),
                        pltpu.VMEM((ATT_K_TILE, 2 * t), F32), pltpu.VMEM((ATT_K_TILE, 2 * t), F32)],
        compiler_params=_params(("arbitrary", "arbitrary")),
        name="attn",
    )(q, k, v_t)


def _ssm_prep_kernel(lrow_ref, lcol_ref, ldt_ref, btr_ref, bti_ref, ctr_ref, cti_ref,
                     toep_ref, bst_ref, bsw_ref, cpw_ref, a_ref):
    p8 = pl.program_id(0) % 8
    t_len, grp = SSM_CHUNK, SSM_GROUP
    dt = jnp.exp(ldt_ref[...])
    lr, li = jnp.minimum(lrow_ref[0:1, :], LAMBDA_RE_MAX), lrow_ref[1:2, :]

    def powers(steps, re, im):
        mag = jnp.exp(steps * (re * dt))
        return mag * jnp.cos(steps * (im * dt)), mag * jnp.sin(steps * (im * dt))

    a_re, a_im = powers(1.0, lr, li)
    den = lr * lr + li * li
    nr = a_re - 1.0
    co_re = (nr * lr + a_im * li) / den
    co_im = (a_im * lr - nr * li) / den
    bbt_re = co_re * btr_ref[...] - co_im * bti_ref[...]
    bbt_im = co_re * bti_ref[...] + co_im * btr_ref[...]

    lag = (lax.broadcasted_iota(jnp.int32, (1, t_len * grp), 1) // grp).astype(F32)
    lcr, lci = jnp.minimum(lcol_ref[:, 0:1], LAMBDA_RE_MAX), lcol_ref[:, 1:2]

    def c_times_power(steps):
        p_re, p_im = powers(steps, lcr, lci)
        return (ctr_ref[...] * p_re - cti_ref[...] * p_im, ctr_ref[...] * p_im + cti_ref[...] * p_re)

    wt_re, wt_im = c_times_power(lag)
    kern = (jnp.dot(bbt_re, wt_re, precision=HIGHEST, preferred_element_type=F32)
            - jnp.dot(bbt_im, wt_im, precision=HIGHEST, preferred_element_type=F32))

    lane = lax.broadcasted_iota(jnp.int32, (1, LANES), 1)
    col_shift = p8 * grp

    def store_cols(ref, rows, lo_half, hi_half):
        ref[rows, 0:LANES] = pltpu.roll(lo_half, col_shift, axis=1).astype(ref.dtype)
        ref[rows, LANES:2 * LANES] = pltpu.roll(hi_half, col_shift, axis=1).astype(ref.dtype)

    def slot_rows(s):
        half, s8 = divmod(s, 8)
        return pl.ds(pl.multiple_of((8 * half + (s8 + p8) % 8) * grp, grp), grp)

    back = (t_len - 1 - lax.broadcasted_iota(jnp.int32, (t_len, 1), 0)).astype(F32)
    e_re, e_im = powers(back, lr, li)
    zero = jnp.zeros((grp, LANES), F32)
    k_lo, k_hi = kern[:, 0:LANES], kern[:, LANES:2 * LANES]
    for s in range(t_len):
        half, s8 = divmod(s, 8)
        keep = lane >= s8 * grp
        r_lo = pltpu.roll(k_lo, s8 * grp, axis=1) if s8 else k_lo
        r_hi = pltpu.roll(k_hi, s8 * grp, axis=1) if s8 else k_hi
        if half == 0:
            lo, hi = jnp.where(keep, r_lo, 0.0), jnp.where(keep, r_hi, r_lo)
        else:
            lo, hi = zero, jnp.where(keep, r_lo, 0.0)
        store_cols(toep_ref, slot_rows(s), lo, hi)
        es_re, es_im = e_re[s:s + 1, :], e_im[s:s + 1, :]
        bs_re = es_re * bbt_re - es_im * bbt_im
        bs_im = es_re * bbt_im + es_im * bbt_re
        bst_ref[slot_rows(s), :] = jnp.concatenate([bs_re, bs_im], axis=1).astype(bst_ref.dtype)
        bsw_ref[slot_rows(s), :] = jnp.concatenate([bs_im, bs_re], axis=1).astype(bsw_ref.dtype)

    w1_re, w1_im = c_times_power(lag + 1.0)
    store_cols(cpw_ref, pl.ds(0, SSM_STATE), w1_re[:, 0:LANES], w1_re[:, LANES:2 * LANES])
    store_cols(cpw_ref, pl.ds(SSM_STATE, SSM_STATE), -w1_im[:, 0:LANES], -w1_im[:, LANES:2 * LANES])
    s_re, s_im = powers(float(t_len), lr, li)
    a_ref[0:1, :] = jnp.concatenate([s_re, s_re], axis=1)
    a_ref[1:2, :] = jnp.concatenate([-s_im, s_im], axis=1)


def _ssm_prep(lambda_re, lambda_im, log_dt, b_re, b_im, c_re, c_im):
    width = SSM_CHUNK * SSM_GROUP
    lam_row = jnp.stack([lambda_re, lambda_im], axis=1)
    tiled = lambda c: jnp.tile(c.transpose(0, 2, 1), (1, 1, SSM_CHUNK))
    per = lambda a, b: pl.BlockSpec((None, a, b), lambda g: (g, 0, 0))
    return pl.pallas_call(
        _ssm_prep_kernel,
        grid=(SSM_GROUPS,),
        in_specs=[per(2, SSM_STATE), per(SSM_STATE, 2), per(1, 1), per(SSM_GROUP, SSM_STATE),
                  per(SSM_GROUP, SSM_STATE), per(SSM_STATE, width), per(SSM_STATE, width)],
        out_specs=[per(width, width), per(width, 2 * SSM_STATE), per(width, 2 * SSM_STATE),
                   per(2 * SSM_STATE, width), per(2, 2 * SSM_STATE)],
        out_shape=[jax.ShapeDtypeStruct((SSM_GROUPS, width, width), BF16),
                   jax.ShapeDtypeStruct((SSM_GROUPS, width, 2 * SSM_STATE), BF16),
                   jax.ShapeDtypeStruct((SSM_GROUPS, width, 2 * SSM_STATE), BF16),
                   jax.ShapeDtypeStruct((SSM_GROUPS, 2 * SSM_STATE, width), BF16),
                   jax.ShapeDtypeStruct((SSM_GROUPS, 2, 2 * SSM_STATE), F32)],
        compiler_params=_params(("arbitrary",)),
        name="ssm_prep",
    )(lam_row, lam_row.transpose(0, 2, 1), log_dt.reshape(SSM_GROUPS, 1, 1),
      b_re.transpose(0, 2, 1), b_im.transpose(0, 2, 1), tiled(c_re), tiled(c_im))


SSM_GROUPS_PER_STEP = 4


def _ssm_kernel(n_chunks, bsz, u_ref, toep_ref, bst_ref, bsw_ref, cpw_ref, a_ref, y_ref,
                contrib_ref, cswap_ref, xprev_ref):
    groups = u_ref.shape[0]
    for k in range(groups):
        u = u_ref[k]
        contrib_ref[k] = jnp.dot(u, bst_ref[k], preferred_element_type=F32)
        cswap_ref[k] = jnp.dot(u, bsw_ref[k], preferred_element_type=F32)
    a1 = [a_ref[k, 0:1, :] for k in range(groups)]
    a2 = [a_ref[k, 1:2, :] for k in range(groups)]

    def step(n, carry):
        rows = pl.ds(n, bsz, stride=n_chunks)
        new = []
        for k in range(groups):
            x, xs = carry[2 * k], carry[2 * k + 1]
            xprev_ref[k, rows, :] = x
            new.append(a1[k] * x + a2[k] * xs + contrib_ref[k, rows, :])
            new.append(a1[k] * xs - a2[k] * x + cswap_ref[k, rows, :])
        return tuple(new)

    zero = jnp.zeros((bsz, 2 * SSM_STATE), F32)
    lax.fori_loop(0, n_chunks, step, (zero,) * (2 * groups), unroll=2)
    for k in range(groups):
        y_ref[k] = (jnp.dot(u_ref[k], toep_ref[k], preferred_element_type=F32)
                    + jnp.dot(xprev_ref[k].astype(BF16), cpw_ref[k], preferred_element_type=F32))


def _ssm(u_flat, toep, b_state, b_swap, c_pow, a_step, bsz):
    g, rows, w = u_flat.shape
    gb = SSM_GROUPS_PER_STEP
    per = lambda a, b: pl.BlockSpec((gb, a, b), lambda i: (i, 0, 0))
    state = pltpu.VMEM((gb, rows, 2 * SSM_STATE), F32)
    return pl.pallas_call(
        functools.partial(_ssm_kernel, rows // bsz, bsz),
        grid=(g // gb,),
        in_specs=[per(rows, w), per(w, w), per(w, 2 * SSM_STATE), per(w, 2 * SSM_STATE),
                  per(2 * SSM_STATE, w), per(2, 2 * SSM_STATE)],
        out_specs=per(rows, w),
        out_shape=jax.ShapeDtypeStruct((g, rows, w), F32),
        scratch_shapes=[state, state, state],
        compiler_params=_params(("arbitrary",)),
        name="ssm",
    )(u_flat, toep, b_state, b_swap, c_pow, a_step)


ROW_SLABS = D_MODEL // LANES // 2
_HIGH_HALF = 0xFFFF0000


def _store_row_tiles(ref, value):
    rows = value.shape[0]
    bits = lax.bitcast_convert_type(value.astype(BF16).astype(F32), jnp.uint32)
    for j in range(ROW_SLABS):
        low = bits[:, j * LANES:(j + 1) * LANES] >> 16
        high = bits[:, (j + ROW_SLABS) * LANES:(j + ROW_SLABS + 1) * LANES] & jnp.uint32(_HIGH_HALF)
        ref[pl.ds(j, rows, stride=ROW_SLABS), :] = high | low


def _load_row_tiles(ref, rows):
    words = [ref[pl.ds(j, rows, stride=ROW_SLABS), :] for j in range(ROW_SLABS)]
    low = [lax.bitcast_convert_type(w << 16, F32) for w in words]
    high = [lax.bitcast_convert_type(w & jnp.uint32(_HIGH_HALF), F32) for w in words]
    return jnp.concatenate(low + high, axis=1)


def _row_tile_copy(src_ref, src_row, dst_ref, dst_row, sem):
    src = src_ref.at[pl.ds(pl.multiple_of(src_row * ROW_SLABS, ROW_SLABS), ROW_SLABS), :]
    dst = dst_ref.at[pl.ds(pl.multiple_of(dst_row * ROW_SLABS, ROW_SLABS), ROW_SLABS), :]
    return pltpu.make_async_copy(src, dst, sem)


def _mix_kernel(x_ref, of_ref, yf_ref, u_ref, ga_ref, gb_ref, mod_ref, dsk_ref, wglu_ref, wfox_ref,
                wssm_ref, wo_ref, g2_ref, wr_ref, br_ref, x1_ref, h2_ref, lg_ref, ytok_ref):
    sub = MIX_SUB_TILE
    for s in range(x_ref.shape[0] // sub):
        rows = pl.ds(s * sub, sub)
        chunks = pl.ds(s * sub // SSM_CHUNK, sub // SSM_CHUNK)
        _to_token_major(yf_ref.at[:, chunks, :], ytok_ref.at[s], sub // SSM_CHUNK)
        y_ssm = jnp.concatenate([ytok_ref[s, j] for j in range(SSM_WIDTH // LANES)], axis=1)
        y = y_ssm + dsk_ref[...] * u_ref[rows, :]
        y = 0.5 * y * (1.0 + jnp.tanh(math.sqrt(2.0 / math.pi) * (y + 0.044715 * (y * y * y))))
        gl = jnp.dot(y.astype(BF16), wglu_ref[...], preferred_element_type=F32)
        o_ssm = gl[:, :SSM_WIDTH] * _sigmoid(gl[:, SSM_WIDTH:])
        merged = (ga_ref[rows, :].astype(F32) * jnp.dot(of_ref[rows, :], wfox_ref[...],
                                                        preferred_element_type=F32)
                  + gb_ref[rows, :].astype(F32) * jnp.dot(o_ssm.astype(BF16), wssm_ref[...],
                                                          preferred_element_type=F32))
        x1 = x_ref[rows, :] + mod_ref[2:3, :] * jnp.dot(merged.astype(BF16), wo_ref[...],
                                                         preferred_element_type=F32)
        x1_ref[rows, :] = x1
        h2 = _rms_modulate(x1, g2_ref[...], mod_ref[3:4, :], mod_ref[4:5, :])
        _store_row_tiles(h2_ref.at[pl.ds(s * sub * ROW_SLABS, sub * ROW_SLABS), :], h2)
        a1 = _top_bits(h2)
        a2 = _top_bits(h2 - a1)
        lhs = jnp.concatenate([a1, a1, a2], axis=1).astype(BF16)
        lg_ref[rows, :] = jnp.dot(lhs, wr_ref[...], preferred_element_type=F32) + br_ref[...]


def _mix(x2, o_fox, y_flat, u, sga, sgb, mod3, d_skip, w_glu, w_fox, w_ssm, w_o, g2, w_r, b_r, seq):
    n, d = x2.shape
    tm = MIX_TILE
    tpb = seq // tm
    tok = lambda w: pl.BlockSpec((tm, w), lambda i: (i, 0))
    const = lambda a: pl.BlockSpec(a.shape, lambda i: (0,) * a.ndim)
    flat = pl.BlockSpec((SSM_GROUPS, tm // SSM_CHUNK, SSM_CHUNK * SSM_GROUP), lambda i: (0, i, 0))
    return pl.pallas_call(
        _mix_kernel,
        grid=(n // tm,),
        in_specs=[tok(d), tok(FOX_WIDTH), flat, tok(SSM_WIDTH), tok(d), tok(d),
                  pl.BlockSpec((None, N_MOD, d), lambda i: (i // tpb, 0, 0)),
                  const(d_skip), const(w_glu), const(w_fox), const(w_ssm), const(w_o), const(g2),
                  const(w_r), const(b_r)],
        out_specs=[tok(d), pl.BlockSpec((tm * ROW_SLABS, LANES), lambda i: (i, 0)), tok(LANES)],
        out_shape=[jax.ShapeDtypeStruct((n, d), F32), jax.ShapeDtypeStruct((n * ROW_SLABS, LANES), jnp.uint32),
                   jax.ShapeDtypeStruct((n, LANES), F32)],
        scratch_shapes=[pltpu.VMEM((tm // MIX_SUB_TILE, SSM_WIDTH // LANES, MIX_SUB_TILE, LANES), F32)],
        compiler_params=_params(("arbitrary",)),
        name="mix",
    )(x2, o_fox, y_flat, u, sga, sgb, mod3, d_skip, w_glu, w_fox, w_ssm, w_o, g2, w_r, b_r)


def _route_kernel(lg_ref, tri_ref, er_ref, wt_ref, cnt_ref, carry_ref):
    i = pl.program_id(0)

    @pl.when(i == 0)
    def _():
        carry_ref[...] = jnp.zeros_like(carry_ref)

    lg = lg_ref[...].T
    tm = lg.shape[1]
    row = lax.broadcasted_iota(jnp.int32, (LANES, tm), 0)
    neg = jnp.full_like(lg, -jnp.inf)

    def first_argmax(vals):
        mx = jnp.max(vals, axis=0, keepdims=True)
        ix = jnp.min(jnp.where(vals == mx, row, LANES), axis=0, keepdims=True)
        return mx, ix

    is_group = row < N_GROUPS
    g_max, gi = first_argmax(jnp.where(is_group, lg, neg))
    g_sum = jnp.sum(jnp.where(is_group, jnp.exp(lg - g_max), 0.0), axis=0, keepdims=True)
    p_group = 1.0 / g_sum
    lo = N_GROUPS + EXPERTS_PER_GROUP * gi
    in_group = (row >= lo) & (row < lo + EXPERTS_PER_GROUP)
    cand = jnp.where(in_group, lg, neg)
    v1, i1 = first_argmax(cand)
    v2, i2 = first_argmax(jnp.where(row == i1, neg, cand))
    tt = jnp.exp(v2 - v1)
    w1 = p_group / (1.0 + tt)
    w2 = p_group * tt / (1.0 + tt)
    e1 = i1 - N_GROUPS
    e2 = i2 - N_GROUPS
    sel1 = row == e1
    sel2 = row == e2
    onehot = (sel1 | sel2).astype(F32)
    before = jnp.dot(onehot.astype(BF16), tri_ref[...], preferred_element_type=F32) + carry_ref[:, 0:1]
    r1 = jnp.sum(jnp.where(sel1, before, 0.0), axis=0, keepdims=True).astype(jnp.int32)
    r2 = jnp.sum(jnp.where(sel2, before, 0.0), axis=0, keepdims=True).astype(jnp.int32)
    total = before[:, tm - 1:tm] + onehot[:, tm - 1:tm]
    carry_ref[...] = jnp.broadcast_to(total, carry_ref.shape)
    cnt_ref[...] = jnp.broadcast_to(total, cnt_ref.shape)
    slot = lax.broadcasted_iota(jnp.int32, (SUBLANES, tm), 0)
    er_ref[...] = jnp.where(slot == 0, e1, jnp.where(slot == 1, e2, jnp.where(slot == 2, r1, r2)))
    wt_ref[...] = jnp.where(row == 0, w1, jnp.where(row == 1, w2, 0.0)).T


def _route(logits):
    n = logits.shape[0]
    tm = TOK_TILE
    tri = jnp.triu(jnp.ones((tm, tm), BF16), k=1)
    tok = pl.BlockSpec((tm, LANES), lambda i: (i, 0))
    return pl.pallas_call(
        _route_kernel,
        grid=(n // tm,),
        in_specs=[tok, pl.BlockSpec((tm, tm), lambda i: (0, 0))],
        out_specs=[pl.BlockSpec((SUBLANES, tm), lambda i: (0, i)), tok,
                   pl.BlockSpec((LANES, LANES), lambda i: (0, 0))],
        out_shape=[jax.ShapeDtypeStruct((SUBLANES, n), jnp.int32), jax.ShapeDtypeStruct((n, LANES), F32),
                   jax.ShapeDtypeStruct((LANES, LANES), F32)],
        scratch_shapes=[pltpu.VMEM((LANES, LANES), F32)],
        compiler_params=_params(("arbitrary",)),
        name="route",
    )(logits, tri)


ISSUE_UNROLL = 8


def _dispatch_kernel(n_steps, dest_ref, h_ref, rows_in_ref, rows_ref, stage_ref, sem):
    del rows_in_ref
    i = pl.program_id(0)
    tm = h_ref.shape[0] // ROW_SLABS
    cur = i % 2
    stage_ref[cur] = h_ref[...]

    def issue(g, c):
        for j in range(ISSUE_UNROLL):
            t = g * ISSUE_UNROLL + j
            _row_tile_copy(stage_ref.at[cur], t, rows_ref, dest_ref[0, 0, t], sem.at[cur]).start(priority=0)
            _row_tile_copy(stage_ref.at[cur], t, rows_ref, dest_ref[0, 0, tm + t],
                           sem.at[cur]).start(priority=1)
        return c

    lax.fori_loop(0, tm // ISSUE_UNROLL, issue, 0)

    def drain(which):
        for _ in range(2):
            pltpu.make_async_copy(stage_ref.at[which], rows_ref.at[pl.ds(0, tm * ROW_SLABS), :],
                                  sem.at[which]).wait()

    @pl.when(i > 0)
    def _():
        drain(1 - cur)

    @pl.when(i == n_steps - 1)
    def _():
        drain(cur)


def _dispatch(dest3, h2_tiles, rows_zero):
    tm = MOVE_TILE
    n = h2_tiles.shape[0] // ROW_SLABS
    return pl.pallas_call(
        functools.partial(_dispatch_kernel, n // tm),
        grid=(n // tm,),
        in_specs=[pl.BlockSpec((1, 1, 2 * tm), lambda i: (i, 0, 0), memory_space=pltpu.SMEM),
                  pl.BlockSpec((tm * ROW_SLABS, LANES), lambda i: (i, 0)),
                  pl.BlockSpec(memory_space=pl.ANY)],
        out_specs=pl.BlockSpec(memory_space=pl.ANY),
        out_shape=jax.ShapeDtypeStruct(rows_zero.shape, rows_zero.dtype),
        scratch_shapes=[pltpu.VMEM((2, tm * ROW_SLABS, LANES), jnp.uint32), pltpu.SemaphoreType.DMA((2,))],
        input_output_aliases={2: 0},
        compiler_params=_params(("arbitrary",)),
        name="dispatch",
    )(dest3, h2_tiles, rows_zero)


def _combine_kernel(n_steps, dest_ref, dnext_ref, x1_ref, wt_ref, mod_ref, gf_ref, yr_ref, o_ref,
                    buf_ref, sem):
    i = pl.program_id(0)
    tm = x1_ref.shape[0]

    def gather(idx_ref, which):
        def issue(g, c):
            for j in range(ISSUE_UNROLL):
                t = g * ISSUE_UNROLL + j
                _row_tile_copy(yr_ref, idx_ref[0, 0, t], buf_ref.at[which, 0], t,
                               sem.at[which]).start(priority=0)
                _row_tile_copy(yr_ref, idx_ref[0, 0, tm + t], buf_ref.at[which, 1], t,
                               sem.at[which]).start(priority=1)
            return c

        lax.fori_loop(0, tm // ISSUE_UNROLL, issue, 0)

    cur = i % 2

    @pl.when(i == 0)
    def _():
        gather(dest_ref, 0)

    @pl.when(i + 1 < n_steps)
    def _():
        gather(dnext_ref, 1 - cur)

    for slot in range(2):
        pltpu.make_async_copy(yr_ref.at[pl.ds(0, tm * ROW_SLABS), :], buf_ref.at[cur, slot],
                              sem.at[cur]).wait()
    wt = wt_ref[...]
    moe = (wt[:, 0:1] * _load_row_tiles(buf_ref.at[cur, 0], tm)
           + wt[:, 1:2] * _load_row_tiles(buf_ref.at[cur, 1], tm))
    x = x1_ref[...] + mod_ref[5:6, :] * moe
    ms = jnp.mean(x * x, axis=-1, keepdims=True)
    o_ref[...] = (x * lax.rsqrt(ms + RMS_EPS)) * gf_ref[...]


def _combine(dest3, x1, wts, mod3, final_g, y_rows, seq):
    n, d = x1.shape
    tm = MOVE_TILE
    tpb = seq // tm
    n_steps = n // tm
    idx_spec = lambda f: pl.BlockSpec((1, 1, 2 * tm), f, memory_space=pltpu.SMEM)
    return pl.pallas_call(
        functools.partial(_combine_kernel, n_steps),
        grid=(n_steps,),
        in_specs=[idx_spec(lambda i: (i, 0, 0)),
                  idx_spec(lambda i: (jnp.minimum(i + 1, n_steps - 1), 0, 0)),
                  pl.BlockSpec((tm, d), lambda i: (i, 0)),
                  pl.BlockSpec((tm, LANES), lambda i: (i, 0)),
                  pl.BlockSpec((None, N_MOD, d), lambda i: (i // tpb, 0, 0)),
                  pl.BlockSpec((1, d), lambda i: (0, 0)),
                  pl.BlockSpec(memory_space=pl.ANY)],
        out_specs=pl.BlockSpec((tm, d), lambda i: (i, 0)),
        out_shape=jax.ShapeDtypeStruct((n, d), F32),
        scratch_shapes=[pltpu.VMEM((2, 2, tm * ROW_SLABS, LANES), jnp.uint32),
                        pltpu.SemaphoreType.DMA((2,))],
        compiler_params=_params(("arbitrary",)),
        name="combine",
    )(dest3, dest3, x1, wts, mod3, final_g, y_rows)


def _expert_kernel(be_ref, seg_ref, nxt_ref, nv_ref, x_ref, wg_hbm, wu_hbm, wd_hbm, y_ref,
                   wg_buf, wu_buf, wd_buf, wgb_ref, wub_ref, wdb_ref, sem):
    i = pl.program_id(0)
    valid = i < nv_ref[0]
    first = (i == 0) | (be_ref[i] != be_ref[jnp.maximum(i - 1, 0)])
    slot = seg_ref[i] % 2

    def weight_copies(e, s):
        return [pltpu.make_async_copy(hbm.at[e], buf.at[s], sem.at[s])
                for hbm, buf in ((wg_hbm, wg_buf), (wu_hbm, wu_buf), (wd_hbm, wd_buf))]

    @pl.when(valid & (i == 0))
    def _():
        for c in weight_copies(be_ref[0], 0):
            c.start()

    @pl.when(valid & first)
    def _():
        for c in weight_copies(be_ref[i], slot):
            c.wait()

        @pl.when(nxt_ref[i] >= 0)
        def _():
            for c in weight_copies(nxt_ref[i], 1 - slot):
                c.start()

        wgb_ref[...] = wg_buf[slot].astype(BF16)
        wub_ref[...] = wu_buf[slot].astype(BF16)
        wdb_ref[...] = wd_buf[slot].astype(BF16)

    @pl.when(valid)
    def _():
        xb = _load_row_tiles(x_ref, x_ref.shape[0] // ROW_SLABS).astype(BF16)
        a = jnp.dot(xb, wgb_ref[...], preferred_element_type=F32)
        b = jnp.dot(xb, wub_ref[...], preferred_element_type=F32)
        hid = (a * _sigmoid(a)) * b
        _store_row_tiles(y_ref, jnp.dot(hid.astype(BF16), wdb_ref[...], preferred_element_type=F32))

    @pl.when(jnp.logical_not(valid))
    def _():
        y_ref[...] = jnp.zeros_like(y_ref)


def _experts(blk_e, n_valid, x_rows, w_gate, w_up, w_down):
    d = D_MODEL
    rows = x_rows.shape[0] // ROW_SLABS
    tb = ROW_BLOCK
    n_blocks = rows // tb
    idx = jnp.arange(n_blocks, dtype=jnp.int32)
    change = (idx == 0) | (blk_e != jnp.roll(blk_e, 1))
    seg = jnp.cumsum(change.astype(jnp.int32)) - 1
    later_start = (idx[None, :] > idx[:, None]) & change[None, :] & (idx[None, :] < n_valid[0])
    none = jnp.int32(N_EXPERTS)
    nxt = jnp.min(jnp.where(later_start, blk_e[None, :], none), axis=1)
    nxt = jnp.where(nxt == none, -1, nxt)
    row_spec = pl.BlockSpec((tb * ROW_SLABS, LANES), lambda i, *_: (i, 0))
    grid_spec = pltpu.PrefetchScalarGridSpec(
        num_scalar_prefetch=4,
        grid=(n_blocks,),
        in_specs=[row_spec, pl.BlockSpec(memory_space=pl.ANY), pl.BlockSpec(memory_space=pl.ANY),
                  pl.BlockSpec(memory_space=pl.ANY)],
        out_specs=row_spec,
        scratch_shapes=[pltpu.VMEM((2, d, D_EXPERT), F32), pltpu.VMEM((2, d, D_EXPERT), F32),
                        pltpu.VMEM((2, D_EXPERT, d), F32),
                        pltpu.VMEM((d, D_EXPERT), BF16), pltpu.VMEM((d, D_EXPERT), BF16),
                        pltpu.VMEM((D_EXPERT, d), BF16), pltpu.SemaphoreType.DMA((2,))],
    )
    return pl.pallas_call(
        _expert_kernel,
        grid_spec=grid_spec,
        out_shape=jax.ShapeDtypeStruct(x_rows.shape, x_rows.dtype),
        compiler_params=_params(("arbitrary",)),
        name="experts",
    )(blk_e, seg, nxt.astype(jnp.int32), n_valid, x_rows, w_gate, w_up, w_down)


def kernel(x, c, w_ada, b_ada, norm_mix_g, w_in, b_forget, w_out_fox, lambda_re, lambda_im, log_dt,
           ssm_b_re, ssm_b_im, ssm_c_re, ssm_c_im, d_skip, w_glu, w_out_ssm, w_o, norm_ffn_g,
           w_router_group, b_router_group, w_router_expert, b_router_expert, w_gate_e, w_up_e,
           w_down_e, final_g):
    bsz, seq, d = x.shape
    n = bsz * seq
    assert w_ada.shape[0] == 1, "the final RMSNorm is fused into the (single) layer's combine kernel"
    xc = x.reshape(n, d)
    for l in range(1):
        mod3 = _mod(c, w_ada[l], b_ada[l]).reshape(bsz, N_MOD, d)

        wi = w_in[l]
        s_q, s_k, s_v, s_f, s_u, s_ga = 512, 1024, 1536, 1544, 2056, 3080
        scale = FOX_HEAD_DIM ** -0.5
        w_all = jnp.concatenate(
            [wi[:, :s_q] * scale, wi[:, s_q:s_k], wi[:, s_f:s_u], wi[:, s_u:s_ga],
             wi[:, s_ga:], jnp.pad(wi[:, s_v:s_f], ((0, 0), (0, LANES - FOX_HEADS)))],
            axis=1).astype(BF16)
        w_vt = wi[:, s_k:s_v].T.astype(BF16)
        bf_pad = jnp.pad(b_forget[l], (0, LANES - FOX_HEADS)).reshape(1, LANES)
        q, k, v_t, u, u_flat, sga, sgb = _inproj(xc, mod3, norm_mix_g[l].reshape(1, d), w_all, w_vt,
                                                 bf_pad, seq)

        o_fox = _attention(q, k, v_t, bsz, seq)

        toep, b_state, b_swap, c_pow, a_step = _ssm_prep(
            lambda_re[l], lambda_im[l], log_dt[l], ssm_b_re[l], ssm_b_im[l], ssm_c_re[l], ssm_c_im[l])
        y_flat = _ssm(u_flat, toep, b_state, b_swap, c_pow, a_step, bsz)

        w_r = jnp.pad(jnp.concatenate([w_router_group[l], w_router_expert[l]], axis=1),
                      ((0, 0), (0, LANES - N_GROUPS - N_EXPERTS)))
        w_r1 = _top_bits(w_r)
        w_r2 = _top_bits(w_r - w_r1)
        w_r = jnp.concatenate([w_r1, w_r2, w_r1], axis=0).astype(BF16)
        b_r = jnp.pad(jnp.concatenate([b_router_group[l], b_router_expert[l]]),
                      (0, LANES - N_GROUPS - N_EXPERTS)).reshape(1, LANES)
        x1, h2, logits = _mix(xc, o_fox, y_flat, u, sga, sgb, mod3, d_skip[l].reshape(1, SSM_WIDTH),
                              w_glu[l].astype(BF16), w_out_fox[l].astype(BF16),
                              w_out_ssm[l].astype(BF16), w_o[l].astype(BF16),
                              norm_ffn_g[l].reshape(1, d), w_r, b_r, seq)

        er, wts, cnt = _route(logits)
        counts = cnt[:N_EXPERTS, 0].astype(jnp.int32)
        pcounts = ((counts + ROW_BLOCK - 1) // ROW_BLOCK) * ROW_BLOCK
        pends = jnp.cumsum(pcounts)
        pstarts = pends - pcounts
        hit = er[0:2, None, :] == jnp.arange(N_EXPERTS, dtype=jnp.int32)[None, :, None]
        dest = jnp.sum(jnp.where(hit, pstarts[None, :, None], 0), axis=1) + er[2:4]
        rows = 2 * n + N_EXPERTS * ROW_BLOCK
        n_blocks = rows // ROW_BLOCK
        blk_start = jnp.arange(n_blocks, dtype=jnp.int32) * ROW_BLOCK
        blk_e = jnp.minimum(jnp.sum((pends[None, :] <= blk_start[:, None]).astype(jnp.int32), axis=1),
                            N_EXPERTS - 1)
        n_valid = (pends[-1:] // ROW_BLOCK).astype(jnp.int32)
        dest3 = (dest.astype(jnp.int32).reshape(2, n // MOVE_TILE, MOVE_TILE).transpose(1, 0, 2)
                 .reshape(n // MOVE_TILE, 1, 2 * MOVE_TILE))

        x_rows = _dispatch(dest3, h2, jnp.zeros((rows * ROW_SLABS, LANES), jnp.uint32))
        y_rows = _experts(blk_e, n_valid, x_rows, w_gate_e[l], w_up_e[l], w_down_e[l])
        xc = _combine(dest3, x1, wts, mod3, final_g.reshape(1, d), y_rows, seq)
    return xc.reshape(bsz, seq, d)
```

```python
import functools
import math

import jax
import jax.numpy as jnp
import numpy as np
from jax import lax
from jax.experimental import pallas as pl
from jax.experimental.pallas import tpu as pltpu

F32 = jnp.float32
BF16 = jnp.bfloat16

D_MODEL = 1024
N_MOD = 6
RMS_EPS = 1e-6
FOX_HEADS = 8
FOX_HEAD_DIM = 64
FOX_WIDTH = FOX_HEADS * FOX_HEAD_DIM
HEAD_PAIRS = FOX_HEADS // 2
SSM_WIDTH = 512
SSM_GROUP = 16
SSM_GROUPS = SSM_WIDTH // SSM_GROUP
SSM_STATE = 64
LAMBDA_RE_MAX = -1e-4
N_GROUPS = 4
EXPERTS_PER_GROUP = 8
N_EXPERTS = N_GROUPS * EXPERTS_PER_GROUP
D_EXPERT = 512

LANES = 128
SUBLANES = 8
VMEM_LIMIT = 56 * 1024 * 1024

SSM_CHUNK = 16
TOK_TILE = 512
MIX_TILE = 512
MIX_SUB_TILE = 256
ATT_Q_TILE = 512
ATT_K_TILE = 512
ROW_BLOCK = 512
MOVE_TILE = 512
NEG_BIG = -1e30

HIGHEST = lax.Precision.HIGHEST


def _params(sem):
    return pltpu.CompilerParams(dimension_semantics=sem, vmem_limit_bytes=VMEM_LIMIT)


def _sigmoid(x):
    return 0.5 * jnp.tanh(0.5 * x) + 0.5


def _rms_modulate(x, gain, shift, scale):
    ms = jnp.mean(x * x, axis=-1, keepdims=True)
    return (x * lax.rsqrt(ms + RMS_EPS)) * gain * (1.0 + scale) + shift


def _mod_kernel(c_ref, w_ref, b_ref, o_ref):
    c = c_ref[...]
    ca = (c * jax.nn.sigmoid(c)).astype(BF16)
    o_ref[...] = jnp.dot(ca, w_ref[...].astype(BF16), preferred_element_type=F32) + b_ref[...]


def _mod(c, w_ada, b_ada):
    bsz, d = c.shape
    cols = w_ada.shape[1]
    tn = 1536
    return pl.pallas_call(
        _mod_kernel,
        grid=(cols // tn,),
        in_specs=[pl.BlockSpec((bsz, d), lambda j: (0, 0)),
                  pl.BlockSpec((d, tn), lambda j: (0, j)),
                  pl.BlockSpec((1, tn), lambda j: (0, j))],
        out_specs=pl.BlockSpec((bsz, tn), lambda j: (0, j)),
        out_shape=jax.ShapeDtypeStruct((bsz, cols), F32),
        compiler_params=_params(("arbitrary",)),
        name="mod",
    )(c, w_ada, b_ada.reshape(1, cols))


_C_Q, _C_K, _C_U, _C_GA, _C_GB, _C_F, _C_END = 0, 512, 1024, 1536, 2560, 3584, 3712


def _lane_block():
    return lax.broadcasted_iota(jnp.int32, (1, LANES), 1) // SSM_GROUP


def _to_group_major(tok_ref, flat_ref, rows):
    blk = _lane_block()
    for half in range(2):
        for j in range(SSM_WIDTH // LANES):
            w = []
            for s8 in range(8):
                v = tok_ref[j, pl.ds(8 * half + s8, rows, stride=SSM_CHUNK), :]
                w.append(pltpu.roll(v, s8 * SSM_GROUP, axis=1) if s8 else v)
            for p in range(8):
                acc = w[0]
                for s8 in range(1, 8):
                    acc = jnp.where(blk == (p + s8) % 8, w[s8], acc)
                flat_ref[8 * j + p, :, half * LANES:(half + 1) * LANES] = acc.astype(flat_ref.dtype)


def _to_token_major(flat_ref, tok_ref, rows):
    blk = _lane_block()
    for half in range(2):
        for j in range(SSM_WIDTH // LANES):
            ys = [flat_ref[8 * j + p, :, half * LANES:(half + 1) * LANES] for p in range(8)]
            for s8 in range(8):
                acc = ys[0]
                for p in range(1, 8):
                    acc = jnp.where(blk == (p + s8) % 8, ys[p], acc)
                if s8:
                    acc = pltpu.roll(acc, LANES - s8 * SSM_GROUP, axis=1)
                tok_ref[j, pl.ds(8 * half + s8, rows, stride=SSM_CHUNK), :] = acc


def _bias_lane_placement():
    pq = np.zeros((3 * LANES, LANES), np.float32)
    pk = np.zeros((3 * LANES, LANES), np.float32)
    bq = np.zeros((1, LANES), np.float32)
    bk = np.zeros((1, LANES), np.float32)
    for head in range(FOX_HEADS):
        base = head * 8
        for term in range(3):
            pq[term * LANES + head, base + term] = 1.0
            pk[term * LANES + head, base + 3 + term] = -1.0
            bq[0, base + 3 + term] = 1.0
            bk[0, base + term] = 1.0
    return pq, pk, bq, bk


def _top_bits(a):
    bits = lax.bitcast_convert_type(a, jnp.uint32) & jnp.uint32(0xFFFF0000)
    return lax.bitcast_convert_type(bits, F32)


def _inproj_kernel(tiles_per_batch, x_ref, mod_ref, g_ref, w_ref, wvt_ref, bf_ref, tri_ref,
                   pq_ref, pk_ref, bq_ref, bk_ref,
                   q_ref, k_ref, vt_ref, u_ref, uflat_ref, ga_ref, gb_ref, carry_ref, uslab_ref):
    i = pl.program_id(0)
    h = _rms_modulate(x_ref[...], g_ref[...], mod_ref[0:1, :], mod_ref[1:2, :])
    hb = h.astype(BF16)

    def proj(a, b):
        return jnp.dot(hb, w_ref[:, a:b], preferred_element_type=F32)

    q = proj(_C_Q, _C_K).astype(BF16)
    k = proj(_C_K, _C_U).astype(BF16)
    vt_ref[...] = lax.dot_general(wvt_ref[...], hb, (((1,), (1,)), ((), ())),
                                  preferred_element_type=F32).astype(BF16)
    u = proj(_C_U, _C_GA)
    u_ref[...] = u
    for j in range(SSM_WIDTH // LANES):
        uslab_ref[j] = u[:, j * LANES:(j + 1) * LANES]
    _to_group_major(uslab_ref, uflat_ref, u.shape[0] // SSM_CHUNK)
    ga_ref[...] = _sigmoid(proj(_C_GA, _C_GB)).astype(BF16)
    gb_ref[...] = _sigmoid(proj(_C_GB, _C_F)).astype(BF16)

    f = proj(_C_F, _C_END) + bf_ref[...]
    logf = jnp.minimum(f, 0.0) - jnp.log(1.0 + jnp.exp(-jnp.abs(f)))

    @pl.when(i % tiles_per_batch == 0)
    def _():
        carry_ref[...] = jnp.zeros_like(carry_ref)

    def split3(a):
        hi = _top_bits(a)
        r1 = a - hi
        mid = _top_bits(r1)
        return jnp.concatenate([hi, mid, _top_bits(r1 - mid)], axis=1).astype(BF16)

    part = jnp.dot(tri_ref[...], split3(logf), preferred_element_type=F32)
    cs = (part[:, :LANES] + part[:, LANES:2 * LANES] + part[:, 2 * LANES:]) + carry_ref[0:1, :]
    carry_ref[...] = jnp.broadcast_to(cs[-1:, :], carry_ref.shape)

    terms = split3(cs)
    bias_q = (jnp.dot(terms, pq_ref[...], preferred_element_type=F32) + bq_ref[...]).astype(BF16)
    bias_k = (jnp.dot(terms, pk_ref[...], preferred_element_type=F32) + bk_ref[...]).astype(BF16)
    for p in range(HEAD_PAIRS):
        lanes = slice(p * LANES, (p + 1) * LANES)
        q_ref[:, 2 * p * LANES:(2 * p + 1) * LANES] = q[:, lanes]
        q_ref[:, (2 * p + 1) * LANES:(2 * p + 2) * LANES] = bias_q
        k_ref[:, 2 * p * LANES:(2 * p + 1) * LANES] = k[:, lanes]
        k_ref[:, (2 * p + 1) * LANES:(2 * p + 2) * LANES] = bias_k


def _inproj(x2, mod3, gain, w_all, w_vt, bf_pad, seq):
    n, d = x2.shape
    tm = TOK_TILE
    tpb = seq // tm
    tri = jnp.tril(jnp.ones((tm, tm), BF16))
    pq, pk, bq, bk = _bias_lane_placement()
    tok = lambda w: pl.BlockSpec((tm, w), lambda i: (i, 0))
    const = lambda shape: pl.BlockSpec(shape, lambda i: (0,) * len(shape))
    qk_width = 2 * FOX_WIDTH
    return pl.pallas_call(
        functools.partial(_inproj_kernel, tpb),
        grid=(n // tm,),
        in_specs=[tok(d),
                  pl.BlockSpec((None, N_MOD, d), lambda i: (i // tpb, 0, 0)),
                  const((1, d)), const((d, _C_END)), const((FOX_WIDTH, d)), const((1, LANES)),
                  const((tm, tm)), const(pq.shape), const(pk.shape), const(bq.shape), const(bk.shape)],
        out_specs=[tok(qk_width), tok(qk_width), pl.BlockSpec((FOX_WIDTH, tm), lambda i: (0, i)),
                   tok(SSM_WIDTH),
                   pl.BlockSpec((SSM_GROUPS, tm // SSM_CHUNK, SSM_CHUNK * SSM_GROUP), lambda i: (0, i, 0)),
                   tok(d), tok(d)],
        out_shape=[jax.ShapeDtypeStruct((n, qk_width), BF16)] * 2
        + [jax.ShapeDtypeStruct((FOX_WIDTH, n), BF16)]
        + [jax.ShapeDtypeStruct((n, SSM_WIDTH), F32)]
        + [jax.ShapeDtypeStruct((SSM_GROUPS, n // SSM_CHUNK, SSM_CHUNK * SSM_GROUP), BF16)]
        + [jax.ShapeDtypeStruct((n, d), BF16)] * 2,
        scratch_shapes=[pltpu.VMEM((SUBLANES, LANES), F32),
                        pltpu.VMEM((SSM_WIDTH // LANES, tm, LANES), F32)],
        compiler_params=_params(("arbitrary",)),
        name="inproj",
    )(x2, mod3, gain, w_all, w_vt, bf_pad, tri, jnp.asarray(pq, BF16), jnp.asarray(pk, BF16),
      jnp.asarray(bq), jnp.asarray(bk))


def _attn_kernel(q_ref, k_ref, vt_ref, o_ref, m_ref, acc_ref, sa_ref, sb_ref):
    tq, tk = ATT_Q_TILE, ATT_K_TILE
    seq = q_ref.shape[0]
    nq = seq // tq
    half = FOX_HEAD_DIM
    lane = lax.broadcasted_iota(jnp.int32, (1, 2 * LANES), 1)
    bias0 = LANES + 16 * pl.program_id(1)
    own0 = (lane < half) | ((lane >= bias0) & (lane < bias0 + 8))
    own1 = ((lane >= half) & (lane < LANES)) | ((lane >= bias0 + 8) & (lane < bias0 + 16))
    ones_rows = jnp.ones((2 * SUBLANES, tk), BF16)
    key_in_tile = lax.broadcasted_iota(jnp.int32, (tk, 2 * tq), 0)
    qry_in_tile = lax.broadcasted_iota(jnp.int32, (tk, 2 * tq), 1) & (tq - 1)
    bufs = (sa_ref, sb_ref)
    key_tiles = [-(-(i + 1) * tq // tk) for i in range(nq)]
    steps = [(i, j) for i in range(nq) for j in range(key_tiles[i])]
    q_cache = {}

    def q_both(i):
        if i not in q_cache:
            q = q_ref[i * tq:(i + 1) * tq, :]
            zq = jnp.zeros_like(q)
            q_cache[i] = jnp.concatenate([jnp.where(own0, q, zq), jnp.where(own1, q, zq)], axis=0)
        return q_cache[i]

    def scores(n):
        i, j = steps[n]
        s = lax.dot_general(k_ref[j * tk:(j + 1) * tk, :], q_both(i), (((1,), (1,)), ((), ())),
                            preferred_element_type=F32)
        if j * tk + tk - 1 > i * tq:
            s = jnp.where(key_in_tile + (j * tk - i * tq) <= qry_in_tile, s, NEG_BIG)
        bufs[n % 2][...] = s

    scores(0)
    for n, (i, j) in enumerate(steps):
        if n + 1 < len(steps):
            scores(n + 1)
        s_ref = bufs[n % 2]
        va = jnp.concatenate([vt_ref[:, j * tk:(j + 1) * tk], ones_rows], axis=0)
        for h in range(2):
            cols = slice(h * tq, (h + 1) * tq)
            if j == 0:
                m_new = jnp.max(s_ref[:, cols], axis=0, keepdims=True)
                p = jnp.exp(s_ref[:, cols] - m_new).astype(BF16)
                acc_ref[:, cols] = jnp.dot(va, p, preferred_element_type=F32)
            else:
                m_old = m_ref[:, cols]
                m_new = jnp.maximum(m_old, jnp.max(s_ref[:, cols], axis=0, keepdims=True))
                alpha = jnp.exp(m_old - m_new)
                p = jnp.exp(s_ref[:, cols] - m_new).astype(BF16)
                acc_ref[:, cols] = alpha * acc_ref[:, cols] + jnp.dot(va, p, preferred_element_type=F32)
            m_ref[:, cols] = m_new
        if j == key_tiles[i] - 1:
            acc = acc_ref[...]
            o_t = jnp.concatenate([acc[0:half, 0:tq] / acc[LANES:LANES + 1, 0:tq],
                                   acc[half:LANES, tq:2 * tq] / acc[LANES:LANES + 1, tq:2 * tq]], axis=0)
            o_ref[i * tq:(i + 1) * tq, :] = o_t.T.astype(o_ref.dtype)


def _attention(q, k, v_t, bsz, seq):
    n = q.shape[0]
    t = ATT_Q_TILE
    return pl.pallas_call(
        _attn_kernel,
        grid=(bsz, HEAD_PAIRS),
        in_specs=[pl.BlockSpec((seq, 2 * LANES), lambda b, p: (b, p)),
                  pl.BlockSpec((seq, 2 * LANES), lambda b, p: (b, p)),
                  pl.BlockSpec((LANES, seq), lambda b, p: (p, b))],
        out_specs=pl.BlockSpec((seq, LANES), lambda b, p: (b, p)),
        out_shape=jax.ShapeDtypeStruct((n, FOX_WIDTH), BF16),
        scratch_shapes=[pltpu.VMEM((1, 2 * t), F32), pltpu.VMEM((LANES + 2 * SUBLANES, 2 * t), F32),
                        pltpu.VMEM((ATT_K_TILE, 2 * t), F32), pltpu.VMEM((ATT_K_TILE, 2 * t), F32)],
        compiler_params=_params(("arbitrary", "arbitrary")),
        name="attn",
    )(q, k, v_t)


def _ssm_prep_kernel(lrow_ref, lcol_ref, ldt_ref, btr_ref, bti_ref, ctr_ref, cti_ref,
                     toep_ref, bst_ref, bsw_ref, cpw_ref, a_ref):
    p8 = pl.program_id(0) % 8
    t_len, grp = SSM_CHUNK, SSM_GROUP
    dt = jnp.exp(ldt_ref[...])
    lr, li = jnp.minimum(lrow_ref[0:1, :], LAMBDA_RE_MAX), lrow_ref[1:2, :]

    def powers(steps, re, im):
        mag = jnp.exp(steps * (re * dt))
        return mag * jnp.cos(steps * (im * dt)), mag * jnp.sin(steps * (im * dt))

    a_re, a_im = powers(1.0, lr, li)
    den = lr * lr + li * li
    nr = a_re - 1.0
    co_re = (nr * lr + a_im * li) / den
    co_im = (a_im * lr - nr * li) / den
    bbt_re = co_re * btr_ref[...] - co_im * bti_ref[...]
    bbt_im = co_re * bti_ref[...] + co_im * btr_ref[...]

    lag = (lax.broadcasted_iota(jnp.int32, (1, t_len * grp), 1) // grp).astype(F32)
    lcr, lci = jnp.minimum(lcol_ref[:, 0:1], LAMBDA_RE_MAX), lcol_ref[:, 1:2]

    def c_times_power(steps):
        p_re, p_im = powers(steps, lcr, lci)
        return (ctr_ref[...] * p_re - cti_ref[...] * p_im, ctr_ref[...] * p_im + cti_ref[...] * p_re)

    wt_re, wt_im = c_times_power(lag)
    kern = (jnp.dot(bbt_re, wt_re, precision=HIGHEST, preferred_element_type=F32)
            - jnp.dot(bbt_im, wt_im, precision=HIGHEST, preferred_element_type=F32))

    lane = lax.broadcasted_iota(jnp.int32, (1, LANES), 1)
    col_shift = p8 * grp

    def store_cols(ref, rows, lo_half, hi_half):
        ref[rows, 0:LANES] = pltpu.roll(lo_half, col_shift, axis=1).astype(ref.dtype)
        ref[rows, LANES:2 * LANES] = pltpu.roll(hi_half, col_shift, axis=1).astype(ref.dtype)

    def slot_rows(s):
        half, s8 = divmod(s, 8)
        return pl.ds(pl.multiple_of((8 * half + (s8 + p8) % 8) * grp, grp), grp)

    back = (t_len - 1 - lax.broadcasted_iota(jnp.int32, (t_len, 1), 0)).astype(F32)
    e_re, e_im = powers(back, lr, li)
    zero = jnp.zeros((grp, LANES), F32)
    k_lo, k_hi = kern[:, 0:LANES], kern[:, LANES:2 * LANES]
    for s in range(t_len):
        half, s8 = divmod(s, 8)
        keep = lane >= s8 * grp
        r_lo = pltpu.roll(k_lo, s8 * grp, axis=1) if s8 else k_lo
        r_hi = pltpu.roll(k_hi, s8 * grp, axis=1) if s8 else k_hi
        if half == 0:
            lo, hi = jnp.where(keep, r_lo, 0.0), jnp.where(keep, r_hi, r_lo)
        else:
            lo, hi = zero, jnp.where(keep, r_lo, 0.0)
        store_cols(toep_ref, slot_rows(s), lo, hi)
        es_re, es_im = e_re[s:s + 1, :], e_im[s:s + 1, :]
        bs_re = es_re * bbt_re - es_im * bbt_im
        bs_im = es_re * bbt_im + es_im * bbt_re
        bst_ref[slot_rows(s), :] = jnp.concatenate([bs_re, bs_im], axis=1).astype(bst_ref.dtype)
        bsw_ref[slot_rows(s), :] = jnp.concatenate([bs_im, bs_re], axis=1).astype(bsw_ref.dtype)

    w1_re, w1_im = c_times_power(lag + 1.0)
    store_cols(cpw_ref, pl.ds(0, SSM_STATE), w1_re[:, 0:LANES], w1_re[:, LANES:2 * LANES])
    store_cols(cpw_ref, pl.ds(SSM_STATE, SSM_STATE), -w1_im[:, 0:LANES], -w1_im[:, LANES:2 * LANES])
    s_re, s_im = powers(float(t_len), lr, li)
    a_ref[0:1, :] = jnp.concatenate([s_re, s_re], axis=1)
    a_ref[1:2, :] = jnp.concatenate([-s_im, s_im], axis=1)


def _ssm_prep(lambda_re, lambda_im, log_dt, b_re, b_im, c_re, c_im):
    width = SSM_CHUNK * SSM_GROUP
    lam_row = jnp.stack([lambda_re, lambda_im], axis=1)
    tiled = lambda c: jnp.tile(c.transpose(0, 2, 1), (1, 1, SSM_CHUNK))
    per = lambda a, b: pl.BlockSpec((None, a, b), lambda g: (g, 0, 0))
    return pl.pallas_call(
        _ssm_prep_kernel,
        grid=(SSM_GROUPS,),
        in_specs=[per(2, SSM_STATE), per(SSM_STATE, 2), per(1, 1), per(SSM_GROUP, SSM_STATE),
                  per(SSM_GROUP, SSM_STATE), per(SSM_STATE, width), per(SSM_STATE, width)],
        out_specs=[per(width, width), per(width, 2 * SSM_STATE), per(width, 2 * SSM_STATE),
                   per(2 * SSM_STATE, width), per(2, 2 * SSM_STATE)],
        out_shape=[jax.ShapeDtypeStruct((SSM_GROUPS, width, width), BF16),
                   jax.ShapeDtypeStruct((SSM_GROUPS, width, 2 * SSM_STATE), BF16),
                   jax.ShapeDtypeStruct((SSM_GROUPS, width, 2 * SSM_STATE), BF16),
                   jax.ShapeDtypeStruct((SSM_GROUPS, 2 * SSM_STATE, width), BF16),
                   jax.ShapeDtypeStruct((SSM_GROUPS, 2, 2 * SSM_STATE), F32)],
        compiler_params=_params(("arbitrary",)),
        name="ssm_prep",
    )(lam_row, lam_row.transpose(0, 2, 1), log_dt.reshape(SSM_GROUPS, 1, 1),
      b_re.transpose(0, 2, 1), b_im.transpose(0, 2, 1), tiled(c_re), tiled(c_im))


SSM_GROUPS_PER_STEP = 4


def _ssm_kernel(n_chunks, bsz, u_ref, toep_ref, bst_ref, bsw_ref, cpw_ref, a_ref, y_ref,
                contrib_ref, cswap_ref, xprev_ref):
    groups = u_ref.shape[0]
    for k in range(groups):
        u = u_ref[k]
        contrib_ref[k] = jnp.dot(u, bst_ref[k], preferred_element_type=F32)
        cswap_ref[k] = jnp.dot(u, bsw_ref[k], preferred_element_type=F32)
    a1 = [a_ref[k, 0:1, :] for k in range(groups)]
    a2 = [a_ref[k, 1:2, :] for k in range(groups)]

    def step(n, carry):
        rows = pl.ds(n, bsz, stride=n_chunks)
        new = []
        for k in range(groups):
            x, xs = carry[2 * k], carry[2 * k + 1]
            xprev_ref[k, rows, :] = x
            new.append(a1[k] * x + a2[k] * xs + contrib_ref[k, rows, :])
            new.append(a1[k] * xs - a2[k] * x + cswap_ref[k, rows, :])
        return tuple(new)

    zero = jnp.zeros((bsz, 2 * SSM_STATE), F32)
    lax.fori_loop(0, n_chunks, step, (zero,) * (2 * groups), unroll=2)
    for k in range(groups):
        y_ref[k] = (jnp.dot(u_ref[k], toep_ref[k], preferred_element_type=F32)
                    + jnp.dot(xprev_ref[k].astype(BF16), cpw_ref[k], preferred_element_type=F32))


def _ssm(u_flat, toep, b_state, b_swap, c_pow, a_step, bsz):
    g, rows, w = u_flat.shape
    gb = SSM_GROUPS_PER_STEP
    per = lambda a, b: pl.BlockSpec((gb, a, b), lambda i: (i, 0, 0))
    state = pltpu.VMEM((gb, rows, 2 * SSM_STATE), F32)
    return pl.pallas_call(
        functools.partial(_ssm_kernel, rows // bsz, bsz),
        grid=(g // gb,),
        in_specs=[per(rows, w), per(w, w), per(w, 2 * SSM_STATE), per(w, 2 * SSM_STATE),
                  per(2 * SSM_STATE, w), per(2, 2 * SSM_STATE)],
        out_specs=per(rows, w),
        out_shape=jax.ShapeDtypeStruct((g, rows, w), F32),
        scratch_shapes=[state, state, state],
        compiler_params=_params(("arbitrary",)),
        name="ssm",
    )(u_flat, toep, b_state, b_swap, c_pow, a_step)


ROW_SLABS = D_MODEL // LANES // 2
_HIGH_HALF = 0xFFFF0000


def _store_row_tiles(ref, value):
    rows = value.shape[0]
    bits = lax.bitcast_convert_type(value.astype(BF16).astype(F32), jnp.uint32)
    for j in range(ROW_SLABS):
        low = bits[:, j * LANES:(j + 1) * LANES] >> 16
        high = bits[:, (j + ROW_SLABS) * LANES:(j + ROW_SLABS + 1) * LANES] & jnp.uint32(_HIGH_HALF)
        ref[pl.ds(j, rows, stride=ROW_SLABS), :] = high | low


def _load_row_tiles(ref, rows):
    words = [ref[pl.ds(j, rows, stride=ROW_SLABS), :] for j in range(ROW_SLABS)]
    low = [lax.bitcast_convert_type(w << 16, F32) for w in words]
    high = [lax.bitcast_convert_type(w & jnp.uint32(_HIGH_HALF), F32) for w in words]
    return jnp.concatenate(low + high, axis=1)


def _row_tile_copy(src_ref, src_row, dst_ref, dst_row, sem):
    src = src_ref.at[pl.ds(pl.multiple_of(src_row * ROW_SLABS, ROW_SLABS), ROW_SLABS), :]
    dst = dst_ref.at[pl.ds(pl.multiple_of(dst_row * ROW_SLABS, ROW_SLABS), ROW_SLABS), :]
    return pltpu.make_async_copy(src, dst, sem)


def _mix_kernel(x_ref, of_ref, yf_ref, u_ref, ga_ref, gb_ref, mod_ref, dsk_ref, wglu_ref, wfox_ref,
                wssm_ref, wo_ref, g2_ref, wr_ref, br_ref, x1_ref, h2_ref, lg_ref, ytok_ref):
    sub = MIX_SUB_TILE
    for s in range(x_ref.shape[0] // sub):
        rows = pl.ds(s * sub, sub)
        chunks = pl.ds(s * sub // SSM_CHUNK, sub // SSM_CHUNK)
        _to_token_major(yf_ref.at[:, chunks, :], ytok_ref.at[s], sub // SSM_CHUNK)
        y_ssm = jnp.concatenate([ytok_ref[s, j] for j in range(SSM_WIDTH // LANES)], axis=1)
        y = y_ssm + dsk_ref[...] * u_ref[rows, :]
        y = 0.5 * y * (1.0 + jnp.tanh(math.sqrt(2.0 / math.pi) * (y + 0.044715 * (y * y * y))))
        gl = jnp.dot(y.astype(BF16), wglu_ref[...], preferred_element_type=F32)
        o_ssm = gl[:, :SSM_WIDTH] * _sigmoid(gl[:, SSM_WIDTH:])
        merged = (ga_ref[rows, :].astype(F32) * jnp.dot(of_ref[rows, :], wfox_ref[...],
                                                        preferred_element_type=F32)
                  + gb_ref[rows, :].astype(F32) * jnp.dot(o_ssm.astype(BF16), wssm_ref[...],
                                                          preferred_element_type=F32))
        x1 = x_ref[rows, :] + mod_ref[2:3, :] * jnp.dot(merged.astype(BF16), wo_ref[...],
                                                         preferred_element_type=F32)
        x1_ref[rows, :] = x1
        h2 = _rms_modulate(x1, g2_ref[...], mod_ref[3:4, :], mod_ref[4:5, :])
        _store_row_tiles(h2_ref.at[pl.ds(s * sub * ROW_SLABS, sub * ROW_SLABS), :], h2)
        a1 = _top_bits(h2)
        a2 = _top_bits(h2 - a1)
        lhs = jnp.concatenate([a1, a1, a2], axis=1).astype(BF16)
        lg_ref[rows, :] = jnp.dot(lhs, wr_ref[...], preferred_element_type=F32) + br_ref[...]


def _mix(x2, o_fox, y_flat, u, sga, sgb, mod3, d_skip, w_glu, w_fox, w_ssm, w_o, g2, w_r, b_r, seq):
    n, d = x2.shape
    tm = MIX_TILE
    tpb = seq // tm
    tok = lambda w: pl.BlockSpec((tm, w), lambda i: (i, 0))
    const = lambda a: pl.BlockSpec(a.shape, lambda i: (0,) * a.ndim)
    flat = pl.BlockSpec((SSM_GROUPS, tm // SSM_CHUNK, SSM_CHUNK * SSM_GROUP), lambda i: (0, i, 0))
    return pl.pallas_call(
        _mix_kernel,
        grid=(n // tm,),
        in_specs=[tok(d), tok(FOX_WIDTH), flat, tok(SSM_WIDTH), tok(d), tok(d),
                  pl.BlockSpec((None, N_MOD, d), lambda i: (i // tpb, 0, 0)),
                  const(d_skip), const(w_glu), const(w_fox), const(w_ssm), const(w_o), const(g2),
                  const(w_r), const(b_r)],
        out_specs=[tok(d), pl.BlockSpec((tm * ROW_SLABS, LANES), lambda i: (i, 0)), tok(LANES)],
        out_shape=[jax.ShapeDtypeStruct((n, d), F32), jax.ShapeDtypeStruct((n * ROW_SLABS, LANES), jnp.uint32),
                   jax.ShapeDtypeStruct((n, LANES), F32)],
        scratch_shapes=[pltpu.VMEM((tm // MIX_SUB_TILE, SSM_WIDTH // LANES, MIX_SUB_TILE, LANES), F32)],
        compiler_params=_params(("arbitrary",)),
        name="mix",
    )(x2, o_fox, y_flat, u, sga, sgb, mod3, d_skip, w_glu, w_fox, w_ssm, w_o, g2, w_r, b_r)


def _route_kernel(lg_ref, tri_ref, er_ref, wt_ref, cnt_ref, carry_ref):
    i = pl.program_id(0)

    @pl.when(i == 0)
    def _():
        carry_ref[...] = jnp.zeros_like(carry_ref)

    lg = lg_ref[...].T
    tm = lg.shape[1]
    row = lax.broadcasted_iota(jnp.int32, (LANES, tm), 0)
    neg = jnp.full_like(lg, -jnp.inf)

    def first_argmax(vals):
        mx = jnp.max(vals, axis=0, keepdims=True)
        ix = jnp.min(jnp.where(vals == mx, row, LANES), axis=0, keepdims=True)
        return mx, ix

    is_group = row < N_GROUPS
    g_max, gi = first_argmax(jnp.where(is_group, lg, neg))
    g_sum = jnp.sum(jnp.where(is_group, jnp.exp(lg - g_max), 0.0), axis=0, keepdims=True)
    p_group = 1.0 / g_sum
    lo = N_GROUPS + EXPERTS_PER_GROUP * gi
    in_group = (row >= lo) & (row < lo + EXPERTS_PER_GROUP)
    cand = jnp.where(in_group, lg, neg)
    v1, i1 = first_argmax(cand)
    v2, i2 = first_argmax(jnp.where(row == i1, neg, cand))
    tt = jnp.exp(v2 - v1)
    w1 = p_group / (1.0 + tt)
    w2 = p_group * tt / (1.0 + tt)
    e1 = i1 - N_GROUPS
    e2 = i2 - N_GROUPS
    sel1 = row == e1
    sel2 = row == e2
    onehot = (sel1 | sel2).astype(F32)
    before = jnp.dot(onehot.astype(BF16), tri_ref[...], preferred_element_type=F32) + carry_ref[:, 0:1]
    r1 = jnp.sum(jnp.where(sel1, before, 0.0), axis=0, keepdims=True).astype(jnp.int32)
    r2 = jnp.sum(jnp.where(sel2, before, 0.0), axis=0, keepdims=True).astype(jnp.int32)
    total = before[:, tm - 1:tm] + onehot[:, tm - 1:tm]
    carry_ref[...] = jnp.broadcast_to(total, carry_ref.shape)
    cnt_ref[...] = jnp.broadcast_to(total, cnt_ref.shape)
    slot = lax.broadcasted_iota(jnp.int32, (SUBLANES, tm), 0)
    er_ref[...] = jnp.where(slot == 0, e1, jnp.where(slot == 1, e2, jnp.where(slot == 2, r1, r2)))
    wt_ref[...] = jnp.where(row == 0, w1, jnp.where(row == 1, w2, 0.0)).T


def _route(logits):
    n = logits.shape[0]
    tm = TOK_TILE
    tri = jnp.triu(jnp.ones((tm, tm), BF16), k=1)
    tok = pl.BlockSpec((tm, LANES), lambda i: (i, 0))
    return pl.pallas_call(
        _route_kernel,
        grid=(n // tm,),
        in_specs=[tok, pl.BlockSpec((tm, tm), lambda i: (0, 0))],
        out_specs=[pl.BlockSpec((SUBLANES, tm), lambda i: (0, i)), tok,
                   pl.BlockSpec((LANES, LANES), lambda i: (0, 0))],
        out_shape=[jax.ShapeDtypeStruct((SUBLANES, n), jnp.int32), jax.ShapeDtypeStruct((n, LANES), F32),
                   jax.ShapeDtypeStruct((LANES, LANES), F32)],
        scratch_shapes=[pltpu.VMEM((LANES, LANES), F32)],
        compiler_params=_params(("arbitrary",)),
        name="route",
    )(logits, tri)


ISSUE_UNROLL = 8


def _dispatch_kernel(n_steps, dest_ref, h_ref, rows_in_ref, rows_ref, stage_ref, sem):
    del rows_in_ref
    i = pl.program_id(0)
    tm = h_ref.shape[0] // ROW_SLABS
    cur = i % 2
    stage_ref[cur] = h_ref[...]

    def issue(g, c):
        for j in range(ISSUE_UNROLL):
            t = g * ISSUE_UNROLL + j
            _row_tile_copy(stage_ref.at[cur], t, rows_ref, dest_ref[0, 0, t], sem.at[cur]).start(priority=0)
            _row_tile_copy(stage_ref.at[cur], t, rows_ref, dest_ref[0, 0, tm + t],
                           sem.at[cur]).start(priority=1)
        return c

    lax.fori_loop(0, tm // ISSUE_UNROLL, issue, 0)

    def drain(which):
        for _ in range(2):
            pltpu.make_async_copy(stage_ref.at[which], rows_ref.at[pl.ds(0, tm * ROW_SLABS), :],
                                  sem.at[which]).wait()

    @pl.when(i > 0)
    def _():
        drain(1 - cur)

    @pl.when(i == n_steps - 1)
    def _():
        drain(cur)


def _dispatch(dest3, h2_tiles, rows_zero):
    tm = MOVE_TILE
    n = h2_tiles.shape[0] // ROW_SLABS
    return pl.pallas_call(
        functools.partial(_dispatch_kernel, n // tm),
        grid=(n // tm,),
        in_specs=[pl.BlockSpec((1, 1, 2 * tm), lambda i: (i, 0, 0), memory_space=pltpu.SMEM),
                  pl.BlockSpec((tm * ROW_SLABS, LANES), lambda i: (i, 0)),
                  pl.BlockSpec(memory_space=pl.ANY)],
        out_specs=pl.BlockSpec(memory_space=pl.ANY),
        out_shape=jax.ShapeDtypeStruct(rows_zero.shape, rows_zero.dtype),
        scratch_shapes=[pltpu.VMEM((2, tm * ROW_SLABS, LANES), jnp.uint32), pltpu.SemaphoreType.DMA((2,))],
        input_output_aliases={2: 0},
        compiler_params=_params(("arbitrary",)),
        name="dispatch",
    )(dest3, h2_tiles, rows_zero)


def _combine_kernel(n_steps, dest_ref, dnext_ref, x1_ref, wt_ref, mod_ref, gf_ref, yr_ref, o_ref,
                    buf_ref, sem):
    i = pl.program_id(0)
    tm = x1_ref.shape[0]

    def gather(idx_ref, which):
        def issue(g, c):
            for j in range(ISSUE_UNROLL):
                t = g * ISSUE_UNROLL + j
                _row_tile_copy(yr_ref, idx_ref[0, 0, t], buf_ref.at[which, 0], t,
                               sem.at[which]).start(priority=0)
                _row_tile_copy(yr_ref, idx_ref[0, 0, tm + t], buf_ref.at[which, 1], t,
                               sem.at[which]).start(priority=1)
            return c

        lax.fori_loop(0, tm // ISSUE_UNROLL, issue, 0)

    cur = i % 2

    @pl.when(i == 0)
    def _():
        gather(dest_ref, 0)

    @pl.when(i + 1 < n_steps)
    def _():
        gather(dnext_ref, 1 - cur)

    for slot in range(2):
        pltpu.make_async_copy(yr_ref.at[pl.ds(0, tm * ROW_SLABS), :], buf_ref.at[cur, slot],
                              sem.at[cur]).wait()
    wt = wt_ref[...]
    moe = (wt[:, 0:1] * _load_row_tiles(buf_ref.at[cur, 0], tm)
           + wt[:, 1:2] * _load_row_tiles(buf_ref.at[cur, 1], tm))
    x = x1_ref[...] + mod_ref[5:6, :] * moe
    ms = jnp.mean(x * x, axis=-1, keepdims=True)
    o_ref[...] = (x * lax.rsqrt(ms + RMS_EPS)) * gf_ref[...]


def _combine(dest3, x1, wts, mod3, final_g, y_rows, seq):
    n, d = x1.shape
    tm = MOVE_TILE
    tpb = seq // tm
    n_steps = n // tm
    idx_spec = lambda f: pl.BlockSpec((1, 1, 2 * tm), f, memory_space=pltpu.SMEM)
    return pl.pallas_call(
        functools.partial(_combine_kernel, n_steps),
        grid=(n_steps,),
        in_specs=[idx_spec(lambda i: (i, 0, 0)),
                  idx_spec(lambda i: (jnp.minimum(i + 1, n_steps - 1), 0, 0)),
                  pl.BlockSpec((tm, d), lambda i: (i, 0)),
                  pl.BlockSpec((tm, LANES), lambda i: (i, 0)),
                  pl.BlockSpec((None, N_MOD, d), lambda i: (i // tpb, 0, 0)),
                  pl.BlockSpec((1, d), lambda i: (0, 0)),
                  pl.BlockSpec(memory_space=pl.ANY)],
        out_specs=pl.BlockSpec((tm, d), lambda i: (i, 0)),
        out_shape=jax.ShapeDtypeStruct((n, d), F32),
        scratch_shapes=[pltpu.VMEM((2, 2, tm * ROW_SLABS, LANES), jnp.uint32),
                        pltpu.SemaphoreType.DMA((2,))],
        compiler_params=_params(("arbitrary",)),
        name="combine",
    )(dest3, dest3, x1, wts, mod3, final_g, y_rows)


def _expert_kernel(be_ref, seg_ref, nxt_ref, nv_ref, x_ref, wg_hbm, wu_hbm, wd_hbm, y_ref,
                   wg_buf, wu_buf, wd_buf, wgb_ref, wub_ref, wdb_ref, sem):
    i = pl.program_id(0)
    valid = i < nv_ref[0]
    first = (i == 0) | (be_ref[i] != be_ref[jnp.maximum(i - 1, 0)])
    slot = seg_ref[i] % 2

    def weight_copies(e, s):
        return [pltpu.make_async_copy(hbm.at[e], buf.at[s], sem.at[s])
                for hbm, buf in ((wg_hbm, wg_buf), (wu_hbm, wu_buf), (wd_hbm, wd_buf))]

    @pl.when(valid & (i == 0))
    def _():
        for c in weight_copies(be_ref[0], 0):
            c.start()

    @pl.when(valid & first)
    def _():
        for c in weight_copies(be_ref[i], slot):
            c.wait()

        @pl.when(nxt_ref[i] >= 0)
        def _():
            for c in weight_copies(nxt_ref[i], 1 - slot):
                c.start(priority=1)

        wgb_ref[...] = wg_buf[slot].astype(BF16)
        wub_ref[...] = wu_buf[slot].astype(BF16)
        wdb_ref[...] = wd_buf[slot].astype(BF16)

    @pl.when(valid)
    def _():
        xb = _load_row_tiles(x_ref, x_ref.shape[0] // ROW_SLABS).astype(BF16)
        a = jnp.dot(xb, wgb_ref[...], preferred_element_type=F32)
        b = jnp.dot(xb, wub_ref[...], preferred_element_type=F32)
        hid = (a * _sigmoid(a)) * b
        _store_row_tiles(y_ref, jnp.dot(hid.astype(BF16), wdb_ref[...], preferred_element_type=F32))

    @pl.when(jnp.logical_not(valid))
    def _():
        y_ref[...] = jnp.zeros_like(y_ref)


def _experts(blk_e, n_valid, x_rows, w_gate, w_up, w_down):
    d = D_MODEL
    rows = x_rows.shape[0] // ROW_SLABS
    tb = ROW_BLOCK
    n_blocks = rows // tb
    idx = jnp.arange(n_blocks, dtype=jnp.int32)
    change = (idx == 0) | (blk_e != jnp.roll(blk_e, 1))
    seg = jnp.cumsum(change.astype(jnp.int32)) - 1
    later_start = (idx[None, :] > idx[:, None]) & change[None, :] & (idx[None, :] < n_valid[0])
    none = jnp.int32(N_EXPERTS)
    nxt = jnp.min(jnp.where(later_start, blk_e[None, :], none), axis=1)
    nxt = jnp.where(nxt == none, -1, nxt)
    row_spec = pl.BlockSpec((tb * ROW_SLABS, LANES), lambda i, *_: (i, 0))
    grid_spec = pltpu.PrefetchScalarGridSpec(
        num_scalar_prefetch=4,
        grid=(n_blocks,),
        in_specs=[row_spec, pl.BlockSpec(memory_space=pl.ANY), pl.BlockSpec(memory_space=pl.ANY),
                  pl.BlockSpec(memory_space=pl.ANY)],
        out_specs=row_spec,
        scratch_shapes=[pltpu.VMEM((2, d, D_EXPERT), F32), pltpu.VMEM((2, d, D_EXPERT), F32),
                        pltpu.VMEM((2, D_EXPERT, d), F32),
                        pltpu.VMEM((d, D_EXPERT), BF16), pltpu.VMEM((d, D_EXPERT), BF16),
                        pltpu.VMEM((D_EXPERT, d), BF16), pltpu.SemaphoreType.DMA((2,))],
    )
    return pl.pallas_call(
        _expert_kernel,
        grid_spec=grid_spec,
        out_shape=jax.ShapeDtypeStruct(x_rows.shape, x_rows.dtype),
        compiler_params=_params(("arbitrary",)),
        name="experts",
    )(blk_e, seg, nxt.astype(jnp.int32), n_valid, x_rows, w_gate, w_up, w_down)


def kernel(x, c, w_ada, b_ada, norm_mix_g, w_in, b_forget, w_out_fox, lambda_re, lambda_im, log_dt,
           ssm_b_re, ssm_b_im, ssm_c_re, ssm_c_im, d_skip, w_glu, w_out_ssm, w_o, norm_ffn_g,
           w_router_group, b_router_group, w_router_expert, b_router_expert, w_gate_e, w_up_e,
           w_down_e, final_g):
    bsz, seq, d = x.shape
    n = bsz * seq
    assert w_ada.shape[0] == 1, "the final RMSNorm is fused into the (single) layer's combine kernel"
    xc = x.reshape(n, d)
    for l in range(1):
        mod3 = _mod(c, w_ada[l], b_ada[l]).reshape(bsz, N_MOD, d)

        wi = w_in[l]
        s_q, s_k, s_v, s_f, s_u, s_ga = 512, 1024, 1536, 1544, 2056, 3080
        scale = FOX_HEAD_DIM ** -0.5
        w_all = jnp.concatenate(
            [wi[:, :s_q] * scale, wi[:, s_q:s_k], wi[:, s_f:s_u], wi[:, s_u:s_ga],
             wi[:, s_ga:], jnp.pad(wi[:, s_v:s_f], ((0, 0), (0, LANES - FOX_HEADS)))],
            axis=1).astype(BF16)
        w_vt = wi[:, s_k:s_v].T.astype(BF16)
        bf_pad = jnp.pad(b_forget[l], (0, LANES - FOX_HEADS)).reshape(1, LANES)
        q, k, v_t, u, u_flat, sga, sgb = _inproj(xc, mod3, norm_mix_g[l].reshape(1, d), w_all, w_vt,
                                                 bf_pad, seq)

        o_fox = _attention(q, k, v_t, bsz, seq)

        toep, b_state, b_swap, c_pow, a_step = _ssm_prep(
            lambda_re[l], lambda_im[l], log_dt[l], ssm_b_re[l], ssm_b_im[l], ssm_c_re[l], ssm_c_im[l])
        y_flat = _ssm(u_flat, toep, b_state, b_swap, c_pow, a_step, bsz)

        w_r = jnp.pad(jnp.concatenate([w_router_group[l], w_router_expert[l]], axis=1),
                      ((0, 0), (0, LANES - N_GROUPS - N_EXPERTS)))
        w_r1 = _top_bits(w_r)
        w_r2 = _top_bits(w_r - w_r1)
        w_r = jnp.concatenate([w_r1, w_r2, w_r1], axis=0).astype(BF16)
        b_r = jnp.pad(jnp.concatenate([b_router_group[l], b_router_expert[l]]),
                      (0, LANES - N_GROUPS - N_EXPERTS)).reshape(1, LANES)
        x1, h2, logits = _mix(xc, o_fox, y_flat, u, sga, sgb, mod3, d_skip[l].reshape(1, SSM_WIDTH),
                              w_glu[l].astype(BF16), w_out_fox[l].astype(BF16),
                              w_out_ssm[l].astype(BF16), w_o[l].astype(BF16),
                              norm_ffn_g[l].reshape(1, d), w_r, b_r, seq)

        er, wts, cnt = _route(logits)
        counts = cnt[:N_EXPERTS, 0].astype(jnp.int32)
        pcounts = ((counts + ROW_BLOCK - 1) // ROW_BLOCK) * ROW_BLOCK
        pends = jnp.cumsum(pcounts)
        pstarts = pends - pcounts
        hit = er[0:2, None, :] == jnp.arange(N_EXPERTS, dtype=jnp.int32)[None, :, None]
        dest = jnp.sum(jnp.where(hit, pstarts[None, :, None], 0), axis=1) + er[2:4]
        rows = 2 * n + N_EXPERTS * ROW_BLOCK
        n_blocks = rows // ROW_BLOCK
        blk_start = jnp.arange(n_blocks, dtype=jnp.int32) * ROW_BLOCK
        blk_e = jnp.minimum(jnp.sum((pends[None, :] <= blk_start[:, None]).astype(jnp.int32), axis=1),
                            N_EXPERTS - 1)
        n_valid = (pends[-1:] // ROW_BLOCK).astype(jnp.int32)
        dest3 = (dest.astype(jnp.int32).reshape(2, n // MOVE_TILE, MOVE_TILE).transpose(1, 0, 2)
                 .reshape(n // MOVE_TILE, 1, 2 * MOVE_TILE))

        x_rows = _dispatch(dest3, h2, jnp.zeros((rows * ROW_SLABS, LANES), jnp.uint32))
        y_rows = _experts(blk_e, n_valid, x_rows, w_gate_e[l], w_up_e[l], w_down_e[l])
        xc = _combine(dest3, x1, wts, mod3, final_g.reshape(1, d), y_rows, seq)
    return xc.reshape(bsz, seq, d)
```

```python
import functools
import math

import jax
import jax.numpy as jnp
import numpy as np
from jax import lax
from jax.experimental import pallas as pl
from jax.experimental.pallas import tpu as pltpu

F32 = jnp.float32
BF16 = jnp.bfloat16

D_MODEL = 1024
N_MOD = 6
RMS_EPS = 1e-6
FOX_HEADS = 8
FOX_HEAD_DIM = 64
FOX_WIDTH = FOX_HEADS * FOX_HEAD_DIM
HEAD_PAIRS = FOX_HEADS // 2
SSM_WIDTH = 512
SSM_GROUP = 16
SSM_GROUPS = SSM_WIDTH // SSM_GROUP
SSM_STATE = 64
LAMBDA_RE_MAX = -1e-4
N_GROUPS = 4
EXPERTS_PER_GROUP = 8
N_EXPERTS = N_GROUPS * EXPERTS_PER_GROUP
D_EXPERT = 512

LANES = 128
SUBLANES = 8
VMEM_LIMIT = 56 * 1024 * 1024

SSM_CHUNK = 16
TOK_TILE = 512
MIX_TILE = 512
MIX_SUB_TILE = 256
ATT_Q_TILE = 512
ATT_K_TILE = 512
ROW_BLOCK = 512
MOVE_TILE = 512
NEG_BIG = -1e30

HIGHEST = lax.Precision.HIGHEST


def _params(sem):
    return pltpu.CompilerParams(dimension_semantics=sem, vmem_limit_bytes=VMEM_LIMIT)


def _sigmoid(x):
    return 0.5 * jnp.tanh(0.5 * x) + 0.5


def _rms_modulate(x, gain, shift, scale):
    ms = jnp.mean(x * x, axis=-1, keepdims=True)
    return (x * lax.rsqrt(ms + RMS_EPS)) * gain * (1.0 + scale) + shift


def _mod_kernel(c_ref, w_ref, b_ref, o_ref):
    c = c_ref[...]
    ca = (c * jax.nn.sigmoid(c)).astype(BF16)
    o_ref[...] = jnp.dot(ca, w_ref[...].astype(BF16), preferred_element_type=F32) + b_ref[...]


def _mod(c, w_ada, b_ada):
    bsz, d = c.shape
    cols = w_ada.shape[1]
    tn = 1536
    return pl.pallas_call(
        _mod_kernel,
        grid=(cols // tn,),
        in_specs=[pl.BlockSpec((bsz, d), lambda j: (0, 0)),
                  pl.BlockSpec((d, tn), lambda j: (0, j)),
                  pl.BlockSpec((1, tn), lambda j: (0, j))],
        out_specs=pl.BlockSpec((bsz, tn), lambda j: (0, j)),
        out_shape=jax.ShapeDtypeStruct((bsz, cols), F32),
        compiler_params=_params(("arbitrary",)),
        name="mod",
    )(c, w_ada, b_ada.reshape(1, cols))


_C_Q, _C_K, _C_U, _C_GA, _C_GB, _C_F, _C_END = 0, 512, 1024, 1536, 2560, 3584, 3712


def _lane_block():
    return lax.broadcasted_iota(jnp.int32, (1, LANES), 1) // SSM_GROUP


def _to_group_major(tok_ref, flat_ref, rows):
    blk = _lane_block()
    for half in range(2):
        for j in range(SSM_WIDTH // LANES):
            w = []
            for s8 in range(8):
                v = tok_ref[j, pl.ds(8 * half + s8, rows, stride=SSM_CHUNK), :]
                w.append(pltpu.roll(v, s8 * SSM_GROUP, axis=1) if s8 else v)
            for p in range(8):
                acc = w[0]
                for s8 in range(1, 8):
                    acc = jnp.where(blk == (p + s8) % 8, w[s8], acc)
                flat_ref[8 * j + p, :, half * LANES:(half + 1) * LANES] = acc.astype(flat_ref.dtype)


def _to_token_major(flat_ref, tok_ref, rows):
    blk = _lane_block()
    for half in range(2):
        for j in range(SSM_WIDTH // LANES):
            ys = [flat_ref[8 * j + p, :, half * LANES:(half + 1) * LANES] for p in range(8)]
            for s8 in range(8):
                acc = ys[0]
                for p in range(1, 8):
                    acc = jnp.where(blk == (p + s8) % 8, ys[p], acc)
                if s8:
                    acc = pltpu.roll(acc, LANES - s8 * SSM_GROUP, axis=1)
                tok_ref[j, pl.ds(8 * half + s8, rows, stride=SSM_CHUNK), :] = acc


def _bias_lane_placement():
    pq = np.zeros((3 * LANES, LANES), np.float32)
    pk = np.zeros((3 * LANES, LANES), np.float32)
    bq = np.zeros((1, LANES), np.float32)
    bk = np.zeros((1, LANES), np.float32)
    for head in range(FOX_HEADS):
        base = head * 8
        for term in range(3):
            pq[term * LANES + head, base + term] = 1.0
            pk[term * LANES + head, base + 3 + term] = -1.0
            bq[0, base + 3 + term] = 1.0
            bk[0, base + term] = 1.0
    return pq, pk, bq, bk


def _top_bits(a):
    bits = lax.bitcast_convert_type(a, jnp.uint32) & jnp.uint32(0xFFFF0000)
    return lax.bitcast_convert_type(bits, F32)


def _inproj_kernel(tiles_per_batch, x_ref, mod_ref, g_ref, w_ref, wvt_ref, bf_ref, tri_ref,
                   pq_ref, pk_ref, bq_ref, bk_ref,
                   q_ref, k_ref, vt_ref, u_ref, uflat_ref, ga_ref, gb_ref, carry_ref, uslab_ref):
    i = pl.program_id(0)
    h = _rms_modulate(x_ref[...], g_ref[...], mod_ref[0:1, :], mod_ref[1:2, :])
    hb = h.astype(BF16)

    def proj(a, b):
        return jnp.dot(hb, w_ref[:, a:b], preferred_element_type=F32)

    q = proj(_C_Q, _C_K).astype(BF16)
    k = proj(_C_K, _C_U).astype(BF16)
    vt_ref[...] = lax.dot_general(wvt_ref[...], hb, (((1,), (1,)), ((), ())),
                                  preferred_element_type=F32).astype(BF16)
    u = proj(_C_U, _C_GA)
    u_ref[...] = u
    for j in range(SSM_WIDTH // LANES):
        uslab_ref[j] = u[:, j * LANES:(j + 1) * LANES]
    _to_group_major(uslab_ref, uflat_ref, u.shape[0] // SSM_CHUNK)
    ga_ref[...] = _sigmoid(proj(_C_GA, _C_GB)).astype(BF16)
    gb_ref[...] = _sigmoid(proj(_C_GB, _C_F)).astype(BF16)

    f = proj(_C_F, _C_END) + bf_ref[...]
    logf = jnp.minimum(f, 0.0) - jnp.log(1.0 + jnp.exp(-jnp.abs(f)))

    @pl.when(i % tiles_per_batch == 0)
    def _():
        carry_ref[...] = jnp.zeros_like(carry_ref)

    def split3(a):
        hi = _top_bits(a)
        r1 = a - hi
        mid = _top_bits(r1)
        return jnp.concatenate([hi, mid, _top_bits(r1 - mid)], axis=1).astype(BF16)

    part = jnp.dot(tri_ref[...], split3(logf), preferred_element_type=F32)
    cs = (part[:, :LANES] + part[:, LANES:2 * LANES] + part[:, 2 * LANES:]) + carry_ref[0:1, :]
    carry_ref[...] = jnp.broadcast_to(cs[-1:, :], carry_ref.shape)

    terms = split3(cs)
    bias_q = (jnp.dot(terms, pq_ref[...], preferred_element_type=F32) + bq_ref[...]).astype(BF16)
    bias_k = (jnp.dot(terms, pk_ref[...], preferred_element_type=F32) + bk_ref[...]).astype(BF16)
    for p in range(HEAD_PAIRS):
        lanes = slice(p * LANES, (p + 1) * LANES)
        q_ref[:, 2 * p * LANES:(2 * p + 1) * LANES] = q[:, lanes]
        q_ref[:, (2 * p + 1) * LANES:(2 * p + 2) * LANES] = bias_q
        k_ref[:, 2 * p * LANES:(2 * p + 1) * LANES] = k[:, lanes]
        k_ref[:, (2 * p + 1) * LANES:(2 * p + 2) * LANES] = bias_k


def _inproj(x2, mod3, gain, w_all, w_vt, bf_pad, seq):
    n, d = x2.shape
    tm = TOK_TILE
    tpb = seq // tm
    tri = jnp.tril(jnp.ones((tm, tm), BF16))
    pq, pk, bq, bk = _bias_lane_placement()
    tok = lambda w: pl.BlockSpec((tm, w), lambda i: (i, 0))
    const = lambda shape: pl.BlockSpec(shape, lambda i: (0,) * len(shape))
    qk_width = 2 * FOX_WIDTH
    return pl.pallas_call(
        functools.partial(_inproj_kernel, tpb),
        grid=(n // tm,),
        in_specs=[tok(d),
                  pl.BlockSpec((None, N_MOD, d), lambda i: (i // tpb, 0, 0)),
                  const((1, d)), const((d, _C_END)), const((FOX_WIDTH, d)), const((1, LANES)),
                  const((tm, tm)), const(pq.shape), const(pk.shape), const(bq.shape), const(bk.shape)],
        out_specs=[tok(qk_width), tok(qk_width), pl.BlockSpec((FOX_WIDTH, tm), lambda i: (0, i)),
                   tok(SSM_WIDTH),
                   pl.BlockSpec((SSM_GROUPS, tm // SSM_CHUNK, SSM_CHUNK * SSM_GROUP), lambda i: (0, i, 0)),
                   tok(d), tok(d)],
        out_shape=[jax.ShapeDtypeStruct((n, qk_width), BF16)] * 2
        + [jax.ShapeDtypeStruct((FOX_WIDTH, n), BF16)]
        + [jax.ShapeDtypeStruct((n, SSM_WIDTH), F32)]
        + [jax.ShapeDtypeStruct((SSM_GROUPS, n // SSM_CHUNK, SSM_CHUNK * SSM_GROUP), BF16)]
        + [jax.ShapeDtypeStruct((n, d), BF16)] * 2,
        scratch_shapes=[pltpu.VMEM((SUBLANES, LANES), F32),
                        pltpu.VMEM((SSM_WIDTH // LANES, tm, LANES), F32)],
        compiler_params=_params(("arbitrary",)),
        name="inproj",
    )(x2, mod3, gain, w_all, w_vt, bf_pad, tri, jnp.asarray(pq, BF16), jnp.asarray(pk, BF16),
      jnp.asarray(bq), jnp.asarray(bk))


def _attn_kernel(q_ref, k_ref, vt_ref, o_ref, m_ref, acc_ref, sa_ref, sb_ref):
    tq, tk = ATT_Q_TILE, ATT_K_TILE
    seq = q_ref.shape[0]
    nq = seq // tq
    half = FOX_HEAD_DIM
    lane = lax.broadcasted_iota(jnp.int32, (1, 2 * LANES), 1)
    bias0 = LANES + 16 * pl.program_id(1)
    own0 = (lane < half) | ((lane >= bias0) & (lane < bias0 + 8))
    own1 = ((lane >= half) & (lane < LANES)) | ((lane >= bias0 + 8) & (lane < bias0 + 16))
    ones_rows = jnp.ones((2 * SUBLANES, tk), BF16)
    key_in_tile = lax.broadcasted_iota(jnp.int32, (tk, 2 * tq), 0)
    qry_in_tile = lax.broadcasted_iota(jnp.int32, (tk, 2 * tq), 1) & (tq - 1)
    bufs = (sa_ref, sb_ref)
    key_tiles = [-(-(i + 1) * tq // tk) for i in range(nq)]
    steps = [(i, j) for i in range(nq) for j in range(key_tiles[i])]
    q_cache = {}

    def q_both(i):
        if i not in q_cache:
            q = q_ref[i * tq:(i + 1) * tq, :]
            zq = jnp.zeros_like(q)
            q_cache[i] = jnp.concatenate([jnp.where(own0, q, zq), jnp.where(own1, q, zq)], axis=0)
        return q_cache[i]

    def scores(n):
        i, j = steps[n]
        s = lax.dot_general(k_ref[j * tk:(j + 1) * tk, :], q_both(i), (((1,), (1,)), ((), ())),
                            preferred_element_type=F32)
        if j * tk + tk - 1 > i * tq:
            s = jnp.where(key_in_tile + (j * tk - i * tq) <= qry_in_tile, s, NEG_BIG)
        bufs[n % 2][...] = s

    scores(0)
    for n, (i, j) in enumerate(steps):
        if n + 1 < len(steps):
            scores(n + 1)
        s_ref = bufs[n % 2]
        va = jnp.concatenate([vt_ref[:, j * tk:(j + 1) * tk], ones_rows], axis=0)
        for h in range(2):
            cols = slice(h * tq, (h + 1) * tq)
            if j == 0:
                m_new = jnp.max(s_ref[:, cols], axis=0, keepdims=True)
                p = jnp.exp(s_ref[:, cols] - m_new).astype(BF16)
                acc_ref[:, cols] = jnp.dot(va, p, preferred_element_type=F32)
            else:
                m_old = m_ref[:, cols]
                m_new = jnp.maximum(m_old, jnp.max(s_ref[:, cols], axis=0, keepdims=True))
                alpha = jnp.exp(m_old - m_new)
                p = jnp.exp(s_ref[:, cols] - m_new).astype(BF16)
                acc_ref[:, cols] = alpha * acc_ref[:, cols] + jnp.dot(va, p, preferred_element_type=F32)
            m_ref[:, cols] = m_new
        if j == key_tiles[i] - 1:
            acc = acc_ref[...]
            o_t = jnp.concatenate([acc[0:half, 0:tq] / acc[LANES:LANES + 1, 0:tq],
                                   acc[half:LANES, tq:2 * tq] / acc[LANES:LANES + 1, tq:2 * tq]], axis=0)
            o_ref[i * tq:(i + 1) * tq, :] = o_t.T.astype(o_ref.dtype)


def _attention(q, k, v_t, bsz, seq):
    n = q.shape[0]
    t = ATT_Q_TILE
    return pl.pallas_call(
        _attn_kernel,
        grid=(bsz, HEAD_PAIRS),
        in_specs=[pl.BlockSpec((seq, 2 * LANES), lambda b, p: (b, p)),
                  pl.BlockSpec((seq, 2 * LANES), lambda b, p: (b, p)),
                  pl.BlockSpec((LANES, seq), lambda b, p: (p, b))],
        out_specs=pl.BlockSpec((seq, LANES), lambda b, p: (b, p)),
        out_shape=jax.ShapeDtypeStruct((n, FOX_WIDTH), BF16),
        scratch_shapes=[pltpu.VMEM((1, 2 * t), F32), pltpu.VMEM((LANES + 2 * SUBLANES, 2 * t), F32),
                        pltpu.VMEM((ATT_K_TILE, 2 * t), F32), pltpu.VMEM((ATT_K_TILE, 2 * t), F32)],
        compiler_params=_params(("arbitrary", "arbitrary")),
        name="attn",
    )(q, k, v_t)


def _ssm_prep_kernel(lrow_ref, lcol_ref, ldt_ref, btr_ref, bti_ref, ctr_ref, cti_ref,
                     toep_ref, bst_ref, bsw_ref, cpw_ref, a_ref):
    p8 = pl.program_id(0) % 8
    t_len, grp = SSM_CHUNK, SSM_GROUP
    dt = jnp.exp(ldt_ref[...])
    lr, li = jnp.minimum(lrow_ref[0:1, :], LAMBDA_RE_MAX), lrow_ref[1:2, :]

    def powers(steps, re, im):
        mag = jnp.exp(steps * (re * dt))
        return mag * jnp.cos(steps * (im * dt)), mag * jnp.sin(steps * (im * dt))

    a_re, a_im = powers(1.0, lr, li)
    den = lr * lr + li * li
    nr = a_re - 1.0
    co_re = (nr * lr + a_im * li) / den
    co_im = (a_im * lr - nr * li) / den
    bbt_re = co_re * btr_ref[...] - co_im * bti_ref[...]
    bbt_im = co_re * bti_ref[...] + co_im * btr_ref[...]

    lag = (lax.broadcasted_iota(jnp.int32, (1, t_len * grp), 1) // grp).astype(F32)
    lcr, lci = jnp.minimum(lcol_ref[:, 0:1], LAMBDA_RE_MAX), lcol_ref[:, 1:2]

    def c_times_power(steps):
        p_re, p_im = powers(steps, lcr, lci)
        return (ctr_ref[...] * p_re - cti_ref[...] * p_im, ctr_ref[...] * p_im + cti_ref[...] * p_re)

    wt_re, wt_im = c_times_power(lag)
    kern = (jnp.dot(bbt_re, wt_re, precision=HIGHEST, preferred_element_type=F32)
            - jnp.dot(bbt_im, wt_im, precision=HIGHEST, preferred_element_type=F32))

    lane = lax.broadcasted_iota(jnp.int32, (1, LANES), 1)
    col_shift = p8 * grp

    def store_cols(ref, rows, lo_half, hi_half):
        ref[rows, 0:LANES] = pltpu.roll(lo_half, col_shift, axis=1).astype(ref.dtype)
        ref[rows, LANES:2 * LANES] = pltpu.roll(hi_half, col_shift, axis=1).astype(ref.dtype)

    def slot_rows(s):
        half, s8 = divmod(s, 8)
        return pl.ds(pl.multiple_of((8 * half + (s8 + p8) % 8) * grp, grp), grp)

    back = (t_len - 1 - lax.broadcasted_iota(jnp.int32, (t_len, 1), 0)).astype(F32)
    e_re, e_im = powers(back, lr, li)
    zero = jnp.zeros((grp, LANES), F32)
    k_lo, k_hi = kern[:, 0:LANES], kern[:, LANES:2 * LANES]
    for s in range(t_len):
        half, s8 = divmod(s, 8)
        keep = lane >= s8 * grp
        r_lo = pltpu.roll(k_lo, s8 * grp, axis=1) if s8 else k_lo
        r_hi = pltpu.roll(k_hi, s8 * grp, axis=1) if s8 else k_hi
        if half == 0:
            lo, hi = jnp.where(keep, r_lo, 0.0), jnp.where(keep, r_hi, r_lo)
        else:
            lo, hi = zero, jnp.where(keep, r_lo, 0.0)
        store_cols(toep_ref, slot_rows(s), lo, hi)
        es_re, es_im = e_re[s:s + 1, :], e_im[s:s + 1, :]
        bs_re = es_re * bbt_re - es_im * bbt_im
        bs_im = es_re * bbt_im + es_im * bbt_re
        bst_ref[slot_rows(s), :] = jnp.concatenate([bs_re, bs_im], axis=1).astype(bst_ref.dtype)
        bsw_ref[slot_rows(s), :] = jnp.concatenate([bs_im, bs_re], axis=1).astype(bsw_ref.dtype)

    w1_re, w1_im = c_times_power(lag + 1.0)
    store_cols(cpw_ref, pl.ds(0, SSM_STATE), w1_re[:, 0:LANES], w1_re[:, LANES:2 * LANES])
    store_cols(cpw_ref, pl.ds(SSM_STATE, SSM_STATE), -w1_im[:, 0:LANES], -w1_im[:, LANES:2 * LANES])
    s_re, s_im = powers(float(t_len), lr, li)
    a_ref[0:1, :] = jnp.concatenate([s_re, s_re], axis=1)
    a_ref[1:2, :] = jnp.concatenate([-s_im, s_im], axis=1)


def _ssm_prep(lambda_re, lambda_im, log_dt, b_re, b_im, c_re, c_im):
    width = SSM_CHUNK * SSM_GROUP
    lam_row = jnp.stack([lambda_re, lambda_im], axis=1)
    tiled = lambda c: jnp.tile(c.transpose(0, 2, 1), (1, 1, SSM_CHUNK))
    per = lambda a, b: pl.BlockSpec((None, a, b), lambda g: (g, 0, 0))
    return pl.pallas_call(
        _ssm_prep_kernel,
        grid=(SSM_GROUPS,),
        in_specs=[per(2, SSM_STATE), per(SSM_STATE, 2), per(1, 1), per(SSM_GROUP, SSM_STATE),
                  per(SSM_GROUP, SSM_STATE), per(SSM_STATE, width), per(SSM_STATE, width)],
        out_specs=[per(width, width), per(width, 2 * SSM_STATE), per(width, 2 * SSM_STATE),
                   per(2 * SSM_STATE, width), per(2, 2 * SSM_STATE)],
        out_shape=[jax.ShapeDtypeStruct((SSM_GROUPS, width, width), BF16),
                   jax.ShapeDtypeStruct((SSM_GROUPS, width, 2 * SSM_STATE), BF16),
                   jax.ShapeDtypeStruct((SSM_GROUPS, width, 2 * SSM_STATE), BF16),
                   jax.ShapeDtypeStruct((SSM_GROUPS, 2 * SSM_STATE, width), BF16),
                   jax.ShapeDtypeStruct((SSM_GROUPS, 2, 2 * SSM_STATE), F32)],
        compiler_params=_params(("arbitrary",)),
        name="ssm_prep",
    )(lam_row, lam_row.transpose(0, 2, 1), log_dt.reshape(SSM_GROUPS, 1, 1),
      b_re.transpose(0, 2, 1), b_im.transpose(0, 2, 1), tiled(c_re), tiled(c_im))


SSM_GROUPS_PER_STEP = 4


def _ssm_kernel(n_chunks, bsz, u_ref, toep_ref, bst_ref, bsw_ref, cpw_ref, a_ref, y_ref,
                contrib_ref, cswap_ref, xprev_ref):
    groups = u_ref.shape[0]
    for k in range(groups):
        u = u_ref[k]
        contrib_ref[k] = jnp.dot(u, bst_ref[k], preferred_element_type=F32)
        cswap_ref[k] = jnp.dot(u, bsw_ref[k], preferred_element_type=F32)
    a1 = [a_ref[k, 0:1, :] for k in range(groups)]
    a2 = [a_ref[k, 1:2, :] for k in range(groups)]

    def step(n, carry):
        rows = pl.ds(n, bsz, stride=n_chunks)
        new = []
        for k in range(groups):
            x, xs = carry[2 * k], carry[2 * k + 1]
            xprev_ref[k, rows, :] = x
            new.append(a1[k] * x + a2[k] * xs + contrib_ref[k, rows, :])
            new.append(a1[k] * xs - a2[k] * x + cswap_ref[k, rows, :])
        return tuple(new)

    zero = jnp.zeros((bsz, 2 * SSM_STATE), F32)
    lax.fori_loop(0, n_chunks, step, (zero,) * (2 * groups), unroll=2)
    for k in range(groups):
        y_ref[k] = (jnp.dot(u_ref[k], toep_ref[k], preferred_element_type=F32)
                    + jnp.dot(xprev_ref[k].astype(BF16), cpw_ref[k], preferred_element_type=F32))


def _ssm(u_flat, toep, b_state, b_swap, c_pow, a_step, bsz):
    g, rows, w = u_flat.shape
    gb = SSM_GROUPS_PER_STEP
    per = lambda a, b: pl.BlockSpec((gb, a, b), lambda i: (i, 0, 0))
    state = pltpu.VMEM((gb, rows, 2 * SSM_STATE), F32)
    return pl.pallas_call(
        functools.partial(_ssm_kernel, rows // bsz, bsz),
        grid=(g // gb,),
        in_specs=[per(rows, w), per(w, w), per(w, 2 * SSM_STATE), per(w, 2 * SSM_STATE),
                  per(2 * SSM_STATE, w), per(2, 2 * SSM_STATE)],
        out_specs=per(rows, w),
        out_shape=jax.ShapeDtypeStruct((g, rows, w), F32),
        scratch_shapes=[state, state, state],
        compiler_params=_params(("arbitrary",)),
        name="ssm",
    )(u_flat, toep, b_state, b_swap, c_pow, a_step)


ROW_SLABS = D_MODEL // LANES // 2
_HIGH_HALF = 0xFFFF0000


def _store_row_tiles(ref, value):
    rows = value.shape[0]
    bits = lax.bitcast_convert_type(value.astype(BF16).astype(F32), jnp.uint32)
    for j in range(ROW_SLABS):
        low = bits[:, j * LANES:(j + 1) * LANES] >> 16
        high = bits[:, (j + ROW_SLABS) * LANES:(j + ROW_SLABS + 1) * LANES] & jnp.uint32(_HIGH_HALF)
        ref[pl.ds(j, rows, stride=ROW_SLABS), :] = high | low


def _load_row_tiles(ref, rows):
    words = [ref[pl.ds(j, rows, stride=ROW_SLABS), :] for j in range(ROW_SLABS)]
    low = [lax.bitcast_convert_type(w << 16, F32) for w in words]
    high = [lax.bitcast_convert_type(w & jnp.uint32(_HIGH_HALF), F32) for w in words]
    return jnp.concatenate(low + high, axis=1)


def _row_tile_copy(src_ref, src_row, dst_ref, dst_row, sem):
    src = src_ref.at[pl.ds(pl.multiple_of(src_row * ROW_SLABS, ROW_SLABS), ROW_SLABS), :]
    dst = dst_ref.at[pl.ds(pl.multiple_of(dst_row * ROW_SLABS, ROW_SLABS), ROW_SLABS), :]
    return pltpu.make_async_copy(src, dst, sem)


def _mix_kernel(x_ref, of_ref, yf_ref, u_ref, ga_ref, gb_ref, mod_ref, dsk_ref, wglu_ref, wfox_ref,
                wssm_ref, wo_ref, g2_ref, wr_ref, br_ref, x1_ref, h2_ref, lg_ref, ytok_ref):
    sub = MIX_SUB_TILE
    for s in range(x_ref.shape[0] // sub):
        rows = pl.ds(s * sub, sub)
        chunks = pl.ds(s * sub // SSM_CHUNK, sub // SSM_CHUNK)
        _to_token_major(yf_ref.at[:, chunks, :], ytok_ref.at[s], sub // SSM_CHUNK)
        y_ssm = jnp.concatenate([ytok_ref[s, j] for j in range(SSM_WIDTH // LANES)], axis=1)
        y = y_ssm + dsk_ref[...] * u_ref[rows, :]
        y = 0.5 * y * (1.0 + jnp.tanh(math.sqrt(2.0 / math.pi) * (y + 0.044715 * (y * y * y))))
        gl = jnp.dot(y.astype(BF16), wglu_ref[...], preferred_element_type=F32)
        o_ssm = gl[:, :SSM_WIDTH] * _sigmoid(gl[:, SSM_WIDTH:])
        merged = (ga_ref[rows, :].astype(F32) * jnp.dot(of_ref[rows, :], wfox_ref[...],
                                                        preferred_element_type=F32)
                  + gb_ref[rows, :].astype(F32) * jnp.dot(o_ssm.astype(BF16), wssm_ref[...],
                                                          preferred_element_type=F32))
        x1 = x_ref[rows, :] + mod_ref[2:3, :] * jnp.dot(merged.astype(BF16), wo_ref[...],
                                                         preferred_element_type=F32)
        x1_ref[rows, :] = x1
        h2 = _rms_modulate(x1, g2_ref[...], mod_ref[3:4, :], mod_ref[4:5, :])
        _store_row_tiles(h2_ref.at[pl.ds(s * sub * ROW_SLABS, sub * ROW_SLABS), :], h2)
        a1 = _top_bits(h2)
        a2 = _top_bits(h2 - a1)
        lhs = jnp.concatenate([a1, a1, a2], axis=1).astype(BF16)
        lg_ref[rows, :] = jnp.dot(lhs, wr_ref[...], preferred_element_type=F32) + br_ref[...]


def _mix(x2, o_fox, y_flat, u, sga, sgb, mod3, d_skip, w_glu, w_fox, w_ssm, w_o, g2, w_r, b_r, seq):
    n, d = x2.shape
    tm = MIX_TILE
    tpb = seq // tm
    tok = lambda w: pl.BlockSpec((tm, w), lambda i: (i, 0))
    const = lambda a: pl.BlockSpec(a.shape, lambda i: (0,) * a.ndim)
    flat = pl.BlockSpec((SSM_GROUPS, tm // SSM_CHUNK, SSM_CHUNK * SSM_GROUP), lambda i: (0, i, 0))
    return pl.pallas_call(
        _mix_kernel,
        grid=(n // tm,),
        in_specs=[tok(d), tok(FOX_WIDTH), flat, tok(SSM_WIDTH), tok(d), tok(d),
                  pl.BlockSpec((None, N_MOD, d), lambda i: (i // tpb, 0, 0)),
                  const(d_skip), const(w_glu), const(w_fox), const(w_ssm), const(w_o), const(g2),
                  const(w_r), const(b_r)],
        out_specs=[tok(d), pl.BlockSpec((tm * ROW_SLABS, LANES), lambda i: (i, 0)), tok(LANES)],
        out_shape=[jax.ShapeDtypeStruct((n, d), F32), jax.ShapeDtypeStruct((n * ROW_SLABS, LANES), jnp.uint32),
                   jax.ShapeDtypeStruct((n, LANES), F32)],
        scratch_shapes=[pltpu.VMEM((tm // MIX_SUB_TILE, SSM_WIDTH // LANES, MIX_SUB_TILE, LANES), F32)],
        compiler_params=_params(("arbitrary",)),
        name="mix",
    )(x2, o_fox, y_flat, u, sga, sgb, mod3, d_skip, w_glu, w_fox, w_ssm, w_o, g2, w_r, b_r)


def _route_kernel(lg_ref, tri_ref, er_ref, wt_ref, cnt_ref, carry_ref):
    i = pl.program_id(0)

    @pl.when(i == 0)
    def _():
        carry_ref[...] = jnp.zeros_like(carry_ref)

    lg = lg_ref[...].T
    tm = lg.shape[1]
    row = lax.broadcasted_iota(jnp.int32, (LANES, tm), 0)
    neg = jnp.full_like(lg, -jnp.inf)

    def first_argmax(vals):
        mx = jnp.max(vals, axis=0, keepdims=True)
        ix = jnp.min(jnp.where(vals == mx, row, LANES), axis=0, keepdims=True)
        return mx, ix

    is_group = row < N_GROUPS
    g_max, gi = first_argmax(jnp.where(is_group, lg, neg))
    g_sum = jnp.sum(jnp.where(is_group, jnp.exp(lg - g_max), 0.0), axis=0, keepdims=True)
    p_group = 1.0 / g_sum
    lo = N_GROUPS + EXPERTS_PER_GROUP * gi
    in_group = (row >= lo) & (row < lo + EXPERTS_PER_GROUP)
    cand = jnp.where(in_group, lg, neg)
    v1, i1 = first_argmax(cand)
    v2, i2 = first_argmax(jnp.where(row == i1, neg, cand))
    tt = jnp.exp(v2 - v1)
    w1 = p_group / (1.0 + tt)
    w2 = p_group * tt / (1.0 + tt)
    e1 = i1 - N_GROUPS
    e2 = i2 - N_GROUPS
    sel1 = row == e1
    sel2 = row == e2
    onehot = (sel1 | sel2).astype(F32)
    before = jnp.dot(onehot.astype(BF16), tri_ref[...], preferred_element_type=F32) + carry_ref[:, 0:1]
    r1 = jnp.sum(jnp.where(sel1, before, 0.0), axis=0, keepdims=True).astype(jnp.int32)
    r2 = jnp.sum(jnp.where(sel2, before, 0.0), axis=0, keepdims=True).astype(jnp.int32)
    total = before[:, tm - 1:tm] + onehot[:, tm - 1:tm]
    carry_ref[...] = jnp.broadcast_to(total, carry_ref.shape)
    cnt_ref[...] = jnp.broadcast_to(total, cnt_ref.shape)
    slot = lax.broadcasted_iota(jnp.int32, (SUBLANES, tm), 0)
    er_ref[...] = jnp.where(slot == 0, e1, jnp.where(slot == 1, e2, jnp.where(slot == 2, r1, r2)))
    wt_ref[...] = jnp.where(row == 0, w1, jnp.where(row == 1, w2, 0.0)).T


def _route(logits):
    n = logits.shape[0]
    tm = TOK_TILE
    tri = jnp.triu(jnp.ones((tm, tm), BF16), k=1)
    tok = pl.BlockSpec((tm, LANES), lambda i: (i, 0))
    return pl.pallas_call(
        _route_kernel,
        grid=(n // tm,),
        in_specs=[tok, pl.BlockSpec((tm, tm), lambda i: (0, 0))],
        out_specs=[pl.BlockSpec((SUBLANES, tm), lambda i: (0, i)), tok,
                   pl.BlockSpec((LANES, LANES), lambda i: (0, 0))],
        out_shape=[jax.ShapeDtypeStruct((SUBLANES, n), jnp.int32), jax.ShapeDtypeStruct((n, LANES), F32),
                   jax.ShapeDtypeStruct((LANES, LANES), F32)],
        scratch_shapes=[pltpu.VMEM((LANES, LANES), F32)],
        compiler_params=_params(("arbitrary",)),
        name="route",
    )(logits, tri)


ISSUE_UNROLL = 8


def _dispatch_kernel(n_steps, dest_ref, h_ref, rows_in_ref, rows_ref, stage_ref, sem):
    del rows_in_ref
    i = pl.program_id(0)
    tm = h_ref.shape[0] // ROW_SLABS
    cur = i % 2
    stage_ref[cur] = h_ref[...]

    def issue(g, c):
        for j in range(ISSUE_UNROLL):
            t = g * ISSUE_UNROLL + j
            _row_tile_copy(stage_ref.at[cur], t, rows_ref, dest_ref[0, 0, t], sem.at[cur]).start(priority=0)
            _row_tile_copy(stage_ref.at[cur], t, rows_ref, dest_ref[0, 0, tm + t],
                           sem.at[cur]).start(priority=1)
        return c

    lax.fori_loop(0, tm // ISSUE_UNROLL, issue, 0)

    def drain(which):
        for _ in range(2):
            pltpu.make_async_copy(stage_ref.at[which], rows_ref.at[pl.ds(0, tm * ROW_SLABS), :],
                                  sem.at[which]).wait()

    @pl.when(i > 0)
    def _():
        drain(1 - cur)

    @pl.when(i == n_steps - 1)
    def _():
        drain(cur)


def _dispatch(dest3, h2_tiles, rows_zero):
    tm = MOVE_TILE
    n = h2_tiles.shape[0] // ROW_SLABS
    return pl.pallas_call(
        functools.partial(_dispatch_kernel, n // tm),
        grid=(n // tm,),
        in_specs=[pl.BlockSpec((1, 1, 2 * tm), lambda i: (i, 0, 0), memory_space=pltpu.SMEM),
                  pl.BlockSpec((tm * ROW_SLABS, LANES), lambda i: (i, 0)),
                  pl.BlockSpec(memory_space=pl.ANY)],
        out_specs=pl.BlockSpec(memory_space=pl.ANY),
        out_shape=jax.ShapeDtypeStruct(rows_zero.shape, rows_zero.dtype),
        scratch_shapes=[pltpu.VMEM((2, tm * ROW_SLABS, LANES), jnp.uint32), pltpu.SemaphoreType.DMA((2,))],
        input_output_aliases={2: 0},
        compiler_params=_params(("arbitrary",)),
        name="dispatch",
    )(dest3, h2_tiles, rows_zero)


def _combine_kernel(n_steps, dest_ref, dnext_ref, x1_ref, wt_ref, mod_ref, gf_ref, yr_ref, o_ref,
                    buf_ref, sem):
    i = pl.program_id(0)
    tm = x1_ref.shape[0]

    def gather(idx_ref, which):
        def issue(g, c):
            for j in range(ISSUE_UNROLL):
                t = g * ISSUE_UNROLL + j
                _row_tile_copy(yr_ref, idx_ref[0, 0, t], buf_ref.at[which, 0], t,
                               sem.at[which]).start(priority=0)
                _row_tile_copy(yr_ref, idx_ref[0, 0, tm + t], buf_ref.at[which, 1], t,
                               sem.at[which]).start(priority=1)
            return c

        lax.fori_loop(0, tm // ISSUE_UNROLL, issue, 0)

    cur = i % 2

    @pl.when(i == 0)
    def _():
        gather(dest_ref, 0)

    @pl.when(i + 1 < n_steps)
    def _():
        gather(dnext_ref, 1 - cur)

    for slot in range(2):
        pltpu.make_async_copy(yr_ref.at[pl.ds(0, tm * ROW_SLABS), :], buf_ref.at[cur, slot],
                              sem.at[cur]).wait()
    wt = wt_ref[...]
    moe = (wt[:, 0:1] * _load_row_tiles(buf_ref.at[cur, 0], tm)
           + wt[:, 1:2] * _load_row_tiles(buf_ref.at[cur, 1], tm))
    x = x1_ref[...] + mod_ref[5:6, :] * moe
    ms = jnp.mean(x * x, axis=-1, keepdims=True)
    o_ref[...] = (x * lax.rsqrt(ms + RMS_EPS)) * gf_ref[...]


def _combine(dest3, x1, wts, mod3, final_g, y_rows, seq):
    n, d = x1.shape
    tm = MOVE_TILE
    tpb = seq // tm
    n_steps = n // tm
    idx_spec = lambda f: pl.BlockSpec((1, 1, 2 * tm), f, memory_space=pltpu.SMEM)
    return pl.pallas_call(
        functools.partial(_combine_kernel, n_steps),
        grid=(n_steps,),
        in_specs=[idx_spec(lambda i: (i, 0, 0)),
                  idx_spec(lambda i: (jnp.minimum(i + 1, n_steps - 1), 0, 0)),
                  pl.BlockSpec((tm, d), lambda i: (i, 0)),
                  pl.BlockSpec((tm, LANES), lambda i: (i, 0)),
                  pl.BlockSpec((None, N_MOD, d), lambda i: (i // tpb, 0, 0)),
                  pl.BlockSpec((1, d), lambda i: (0, 0)),
                  pl.BlockSpec(memory_space=pl.ANY)],
        out_specs=pl.BlockSpec((tm, d), lambda i: (i, 0)),
        out_shape=jax.ShapeDtypeStruct((n, d), F32),
        scratch_shapes=[pltpu.VMEM((2, 2, tm * ROW_SLABS, LANES), jnp.uint32),
                        pltpu.SemaphoreType.DMA((2,))],
        compiler_params=_params(("arbitrary",)),
        name="combine",
    )(dest3, dest3, x1, wts, mod3, final_g, y_rows)


def _expert_kernel(be_ref, seg_ref, nxt_ref, nv_ref, x_ref, wg_hbm, wu_hbm, wd_hbm, y_ref,
                   wg_buf, wu_buf, wd_buf, wgb_ref, wub_ref, wdb_ref, sem):
    i = pl.program_id(0)
    valid = i < nv_ref[0]
    first = (i == 0) | (be_ref[i] != be_ref[jnp.maximum(i - 1, 0)])
    slot = seg_ref[i] % 2

    def weight_copies(e, s):
        return [pltpu.make_async_copy(hbm.at[e], buf.at[s], sem.at[s])
                for hbm, buf in ((wg_hbm, wg_buf), (wu_hbm, wu_buf), (wd_hbm, wd_buf))]

    @pl.when(valid & (i == 0))
    def _():
        for c in weight_copies(be_ref[0], 0):
            c.start()

    @pl.when(valid & first)
    def _():
        for c in weight_copies(be_ref[i], slot):
            c.wait()

        @pl.when(nxt_ref[i] >= 0)
        def _():
            for c in weight_copies(nxt_ref[i], 1 - slot):
                c.start()

        wgb_ref[...] = wg_buf[slot].astype(BF16)
        wub_ref[...] = wu_buf[slot].astype(BF16)
        wdb_ref[...] = wd_buf[slot].astype(BF16)

    @pl.when(valid)
    def _():
        xb = _load_row_tiles(x_ref, x_ref.shape[0] // ROW_SLABS).astype(BF16)
        a = jnp.dot(xb, wgb_ref[...], preferred_element_type=F32)
        b = jnp.dot(xb, wub_ref[...], preferred_element_type=F32)
        hid = (a * _sigmoid(a)) * b
        _store_row_tiles(y_ref, jnp.dot(hid.astype(BF16), wdb_ref[...], preferred_element_type=F32))


def _experts(blk_e, n_valid, x_rows, w_gate, w_up, w_down):
    d = D_MODEL
    rows = x_rows.shape[0] // ROW_SLABS
    tb = ROW_BLOCK
    n_blocks = rows // tb
    idx = jnp.arange(n_blocks, dtype=jnp.int32)
    change = (idx == 0) | (blk_e != jnp.roll(blk_e, 1))
    seg = jnp.cumsum(change.astype(jnp.int32)) - 1
    later_start = (idx[None, :] > idx[:, None]) & change[None, :] & (idx[None, :] < n_valid[0])
    none = jnp.int32(N_EXPERTS)
    nxt = jnp.min(jnp.where(later_start, blk_e[None, :], none), axis=1)
    nxt = jnp.where(nxt == none, -1, nxt)
    row_spec = pl.BlockSpec((tb * ROW_SLABS, LANES),
                            lambda i, be, sg, nx, nv: (jnp.minimum(i, nv[0] - 1), 0))
    grid_spec = pltpu.PrefetchScalarGridSpec(
        num_scalar_prefetch=4,
        grid=(n_blocks,),
        in_specs=[row_spec, pl.BlockSpec(memory_space=pl.ANY), pl.BlockSpec(memory_space=pl.ANY),
                  pl.BlockSpec(memory_space=pl.ANY)],
        out_specs=row_spec,
        scratch_shapes=[pltpu.VMEM((2, d, D_EXPERT), F32), pltpu.VMEM((2, d, D_EXPERT), F32),
                        pltpu.VMEM((2, D_EXPERT, d), F32),
                        pltpu.VMEM((d, D_EXPERT), BF16), pltpu.VMEM((d, D_EXPERT), BF16),
                        pltpu.VMEM((D_EXPERT, d), BF16), pltpu.SemaphoreType.DMA((2,))],
    )
    return pl.pallas_call(
        _expert_kernel,
        grid_spec=grid_spec,
        out_shape=jax.ShapeDtypeStruct(x_rows.shape, x_rows.dtype),
        input_output_aliases={4: 0},
        compiler_params=_params(("arbitrary",)),
        name="experts",
    )(blk_e, seg, nxt.astype(jnp.int32), n_valid, x_rows, w_gate, w_up, w_down)


def kernel(x, c, w_ada, b_ada, norm_mix_g, w_in, b_forget, w_out_fox, lambda_re, lambda_im, log_dt,
           ssm_b_re, ssm_b_im, ssm_c_re, ssm_c_im, d_skip, w_glu, w_out_ssm, w_o, norm_ffn_g,
           w_router_group, b_router_group, w_router_expert, b_router_expert, w_gate_e, w_up_e,
           w_down_e, final_g):
    bsz, seq, d = x.shape
    n = bsz * seq
    assert w_ada.shape[0] == 1, "the final RMSNorm is fused into the (single) layer's combine kernel"
    xc = x.reshape(n, d)
    for l in range(1):
        mod3 = _mod(c, w_ada[l], b_ada[l]).reshape(bsz, N_MOD, d)

        wi = w_in[l]
        s_q, s_k, s_v, s_f, s_u, s_ga = 512, 1024, 1536, 1544, 2056, 3080
        scale = FOX_HEAD_DIM ** -0.5
        w_all = jnp.concatenate(
            [wi[:, :s_q] * scale, wi[:, s_q:s_k], wi[:, s_f:s_u], wi[:, s_u:s_ga],
             wi[:, s_ga:], jnp.pad(wi[:, s_v:s_f], ((0, 0), (0, LANES - FOX_HEADS)))],
            axis=1).astype(BF16)
        w_vt = wi[:, s_k:s_v].T.astype(BF16)
        bf_pad = jnp.pad(b_forget[l], (0, LANES - FOX_HEADS)).reshape(1, LANES)
        q, k, v_t, u, u_flat, sga, sgb = _inproj(xc, mod3, norm_mix_g[l].reshape(1, d), w_all, w_vt,
                                                 bf_pad, seq)

        o_fox = _attention(q, k, v_t, bsz, seq)

        toep, b_state, b_swap, c_pow, a_step = _ssm_prep(
            lambda_re[l], lambda_im[l], log_dt[l], ssm_b_re[l], ssm_b_im[l], ssm_c_re[l], ssm_c_im[l])
        y_flat = _ssm(u_flat, toep, b_state, b_swap, c_pow, a_step, bsz)

        w_r = jnp.pad(jnp.concatenate([w_router_group[l], w_router_expert[l]], axis=1),
                      ((0, 0), (0, LANES - N_GROUPS - N_EXPERTS)))
        w_r1 = _top_bits(w_r)
        w_r2 = _top_bits(w_r - w_r1)
        w_r = jnp.concatenate([w_r1, w_r2, w_r1], axis=0).astype(BF16)
        b_r = jnp.pad(jnp.concatenate([b_router_group[l], b_router_expert[l]]),
                      (0, LANES - N_GROUPS - N_EXPERTS)).reshape(1, LANES)
        x1, h2, logits = _mix(xc, o_fox, y_flat, u, sga, sgb, mod3, d_skip[l].reshape(1, SSM_WIDTH),
                              w_glu[l].astype(BF16), w_out_fox[l].astype(BF16),
                              w_out_ssm[l].astype(BF16), w_o[l].astype(BF16),
                              norm_ffn_g[l].reshape(1, d), w_r, b_r, seq)

        er, wts, cnt = _route(logits)
        counts = cnt[:N_EXPERTS, 0].astype(jnp.int32)
        pcounts = ((counts + ROW_BLOCK - 1) // ROW_BLOCK) * ROW_BLOCK
        pends = jnp.cumsum(pcounts)
        pstarts = pends - pcounts
        hit = er[0:2, None, :] == jnp.arange(N_EXPERTS, dtype=jnp.int32)[None, :, None]
        dest = jnp.sum(jnp.where(hit, pstarts[None, :, None], 0), axis=1) + er[2:4]
        rows = 2 * n + N_EXPERTS * ROW_BLOCK
        n_blocks = rows // ROW_BLOCK
        blk_start = jnp.arange(n_blocks, dtype=jnp.int32) * ROW_BLOCK
        blk_e = jnp.minimum(jnp.sum((pends[None, :] <= blk_start[:, None]).astype(jnp.int32), axis=1),
                            N_EXPERTS - 1)
        n_valid = (pends[-1:] // ROW_BLOCK).astype(jnp.int32)
        dest3 = (dest.astype(jnp.int32).reshape(2, n // MOVE_TILE, MOVE_TILE).transpose(1, 0, 2)
                 .reshape(n // MOVE_TILE, 1, 2 * MOVE_TILE))

        x_rows = _dispatch(dest3, h2, jnp.zeros((rows * ROW_SLABS, LANES), jnp.uint32))
        y_rows = _experts(blk_e, n_valid, x_rows, w_gate_e[l], w_up_e[l], w_down_e[l])
        xc = _combine(dest3, x1, wts, mod3, final_g.reshape(1, d), y_rows, seq)
    return xc.reshape(bsz, seq, d)
```

```python
import functools
import math

import jax
import jax.numpy as jnp
import numpy as np
from jax import lax
from jax.experimental import pallas as pl
from jax.experimental.pallas import tpu as pltpu

F32 = jnp.float32
BF16 = jnp.bfloat16

D_MODEL = 1024
N_MOD = 6
RMS_EPS = 1e-6
FOX_HEADS = 8
FOX_HEAD_DIM = 64
FOX_WIDTH = FOX_HEADS * FOX_HEAD_DIM
HEAD_PAIRS = FOX_HEADS // 2
SSM_WIDTH = 512
SSM_GROUP = 16
SSM_GROUPS = SSM_WIDTH // SSM_GROUP
SSM_STATE = 64
LAMBDA_RE_MAX = -1e-4
N_GROUPS = 4
EXPERTS_PER_GROUP = 8
N_EXPERTS = N_GROUPS * EXPERTS_PER_GROUP
D_EXPERT = 512

LANES = 128
SUBLANES = 8
VMEM_LIMIT = 56 * 1024 * 1024

SSM_CHUNK = 16
TOK_TILE = 512
MIX_TILE = 512
MIX_SUB_TILE = 256
ATT_Q_TILE = 512
ATT_K_TILE = 512
ROW_BLOCK = 512
MOVE_TILE = 512
NEG_BIG = -1e30

HIGHEST = lax.Precision.HIGHEST


def _params(sem):
    return pltpu.CompilerParams(dimension_semantics=sem, vmem_limit_bytes=VMEM_LIMIT)


def _sigmoid(x):
    return 0.5 * jnp.tanh(0.5 * x) + 0.5


def _rms_modulate(x, gain, shift, scale):
    ms = jnp.mean(x * x, axis=-1, keepdims=True)
    return (x * lax.rsqrt(ms + RMS_EPS)) * gain * (1.0 + scale) + shift


def _mod_kernel(c_ref, w_ref, b_ref, o_ref):
    c = c_ref[...]
    ca = (c * jax.nn.sigmoid(c)).astype(BF16)
    o_ref[...] = jnp.dot(ca, w_ref[...].astype(BF16), preferred_element_type=F32) + b_ref[...]


def _mod(c, w_ada, b_ada):
    bsz, d = c.shape
    cols = w_ada.shape[1]
    tn = 1536
    return pl.pallas_call(
        _mod_kernel,
        grid=(cols // tn,),
        in_specs=[pl.BlockSpec((bsz, d), lambda j: (0, 0)),
                  pl.BlockSpec((d, tn), lambda j: (0, j)),
                  pl.BlockSpec((1, tn), lambda j: (0, j))],
        out_specs=pl.BlockSpec((bsz, tn), lambda j: (0, j)),
        out_shape=jax.ShapeDtypeStruct((bsz, cols), F32),
        compiler_params=_params(("arbitrary",)),
        name="mod",
    )(c, w_ada, b_ada.reshape(1, cols))


_C_Q, _C_K, _C_U, _C_GA, _C_GB, _C_F, _C_END = 0, 512, 1024, 1536, 2560, 3584, 3712


def _lane_block():
    return lax.broadcasted_iota(jnp.int32, (1, LANES), 1) // SSM_GROUP


def _to_group_major(tok_ref, flat_ref, rows):
    blk = _lane_block()
    for half in range(2):
        for j in range(SSM_WIDTH // LANES):
            w = []
            for s8 in range(8):
                v = tok_ref[j, pl.ds(8 * half + s8, rows, stride=SSM_CHUNK), :]
                w.append(pltpu.roll(v, s8 * SSM_GROUP, axis=1) if s8 else v)
            for p in range(8):
                acc = w[0]
                for s8 in range(1, 8):
                    acc = jnp.where(blk == (p + s8) % 8, w[s8], acc)
                flat_ref[8 * j + p, :, half * LANES:(half + 1) * LANES] = acc.astype(flat_ref.dtype)


def _to_token_major(flat_ref, tok_ref, rows):
    blk = _lane_block()
    for half in range(2):
        for j in range(SSM_WIDTH // LANES):
            ys = [flat_ref[8 * j + p, :, half * LANES:(half + 1) * LANES] for p in range(8)]
            for s8 in range(8):
                acc = ys[0]
                for p in range(1, 8):
                    acc = jnp.where(blk == (p + s8) % 8, ys[p], acc)
                if s8:
                    acc = pltpu.roll(acc, LANES - s8 * SSM_GROUP, axis=1)
                tok_ref[j, pl.ds(8 * half + s8, rows, stride=SSM_CHUNK), :] = acc


def _bias_lane_placement():
    pq = np.zeros((3 * LANES, LANES), np.float32)
    pk = np.zeros((3 * LANES, LANES), np.float32)
    bq = np.zeros((1, LANES), np.float32)
    bk = np.zeros((1, LANES), np.float32)
    for head in range(FOX_HEADS):
        base = head * 8
        for term in range(3):
            pq[term * LANES + head, base + term] = 1.0
            pk[term * LANES + head, base + 3 + term] = -1.0
            bq[0, base + 3 + term] = 1.0
            bk[0, base + term] = 1.0
    return pq, pk, bq, bk


def _top_bits(a):
    bits = lax.bitcast_convert_type(a, jnp.uint32) & jnp.uint32(0xFFFF0000)
    return lax.bitcast_convert_type(bits, F32)


def _inproj_kernel(tiles_per_batch, x_ref, mod_ref, g_ref, w_ref, wvt_ref, bf_ref, tri_ref,
                   pq_ref, pk_ref, bq_ref, bk_ref,
                   q_ref, k_ref, vt_ref, u_ref, uflat_ref, ga_ref, gb_ref, carry_ref, uslab_ref):
    i = pl.program_id(0)
    h = _rms_modulate(x_ref[...], g_ref[...], mod_ref[0:1, :], mod_ref[1:2, :])
    hb = h.astype(BF16)

    def proj(a, b):
        return jnp.dot(hb, w_ref[:, a:b], preferred_element_type=F32)

    q = proj(_C_Q, _C_K).astype(BF16)
    k = proj(_C_K, _C_U).astype(BF16)
    vt_ref[...] = lax.dot_general(wvt_ref[...], hb, (((1,), (1,)), ((), ())),
                                  preferred_element_type=F32).astype(BF16)
    u = proj(_C_U, _C_GA)
    u_ref[...] = u
    for j in range(SSM_WIDTH // LANES):
        uslab_ref[j] = u[:, j * LANES:(j + 1) * LANES]
    _to_group_major(uslab_ref, uflat_ref, u.shape[0] // SSM_CHUNK)
    ga_ref[...] = _sigmoid(proj(_C_GA, _C_GB)).astype(BF16)
    gb_ref[...] = _sigmoid(proj(_C_GB, _C_F)).astype(BF16)

    f = proj(_C_F, _C_END) + bf_ref[...]
    logf = jnp.minimum(f, 0.0) - jnp.log(1.0 + jnp.exp(-jnp.abs(f)))

    @pl.when(i % tiles_per_batch == 0)
    def _():
        carry_ref[...] = jnp.zeros_like(carry_ref)

    def split3(a):
        hi = _top_bits(a)
        r1 = a - hi
        mid = _top_bits(r1)
        return jnp.concatenate([hi, mid, _top_bits(r1 - mid)], axis=1).astype(BF16)

    part = jnp.dot(tri_ref[...], split3(logf), preferred_element_type=F32)
    cs = (part[:, :LANES] + part[:, LANES:2 * LANES] + part[:, 2 * LANES:]) + carry_ref[0:1, :]
    carry_ref[...] = jnp.broadcast_to(cs[-1:, :], carry_ref.shape)

    terms = split3(cs)
    bias_q = (jnp.dot(terms, pq_ref[...], preferred_element_type=F32) + bq_ref[...]).astype(BF16)
    bias_k = (jnp.dot(terms, pk_ref[...], preferred_element_type=F32) + bk_ref[...]).astype(BF16)
    for p in range(HEAD_PAIRS):
        lanes = slice(p * LANES, (p + 1) * LANES)
        q_ref[:, 2 * p * LANES:(2 * p + 1) * LANES] = q[:, lanes]
        q_ref[:, (2 * p + 1) * LANES:(2 * p + 2) * LANES] = bias_q
        k_ref[:, 2 * p * LANES:(2 * p + 1) * LANES] = k[:, lanes]
        k_ref[:, (2 * p + 1) * LANES:(2 * p + 2) * LANES] = bias_k


def _inproj(x2, mod3, gain, w_all, w_vt, bf_pad, seq):
    n, d = x2.shape
    tm = TOK_TILE
    tpb = seq // tm
    tri = jnp.tril(jnp.ones((tm, tm), BF16))
    pq, pk, bq, bk = _bias_lane_placement()
    tok = lambda w: pl.BlockSpec((tm, w), lambda i: (i, 0))
    const = lambda shape: pl.BlockSpec(shape, lambda i: (0,) * len(shape))
    qk_width = 2 * FOX_WIDTH
    return pl.pallas_call(
        functools.partial(_inproj_kernel, tpb),
        grid=(n // tm,),
        in_specs=[tok(d),
                  pl.BlockSpec((None, N_MOD, d), lambda i: (i // tpb, 0, 0)),
                  const((1, d)), const((d, _C_END)), const((FOX_WIDTH, d)), const((1, LANES)),
                  const((tm, tm)), const(pq.shape), const(pk.shape), const(bq.shape), const(bk.shape)],
        out_specs=[tok(qk_width), tok(qk_width), pl.BlockSpec((FOX_WIDTH, tm), lambda i: (0, i)),
                   tok(SSM_WIDTH),
                   pl.BlockSpec((SSM_GROUPS, tm // SSM_CHUNK, SSM_CHUNK * SSM_GROUP), lambda i: (0, i, 0)),
                   tok(d), tok(d)],
        out_shape=[jax.ShapeDtypeStruct((n, qk_width), BF16)] * 2
        + [jax.ShapeDtypeStruct((FOX_WIDTH, n), BF16)]
        + [jax.ShapeDtypeStruct((n, SSM_WIDTH), F32)]
        + [jax.ShapeDtypeStruct((SSM_GROUPS, n // SSM_CHUNK, SSM_CHUNK * SSM_GROUP), BF16)]
        + [jax.ShapeDtypeStruct((n, d), BF16)] * 2,
        scratch_shapes=[pltpu.VMEM((SUBLANES, LANES), F32),
                        pltpu.VMEM((SSM_WIDTH // LANES, tm, LANES), F32)],
        compiler_params=_params(("arbitrary",)),
        name="inproj",
    )(x2, mod3, gain, w_all, w_vt, bf_pad, tri, jnp.asarray(pq, BF16), jnp.asarray(pk, BF16),
      jnp.asarray(bq), jnp.asarray(bk))


def _attn_kernel(q_ref, k_ref, vt_ref, o_ref, m_ref, acc_ref, sa_ref, sb_ref):
    tq, tk = ATT_Q_TILE, ATT_K_TILE
    seq = q_ref.shape[0]
    nq = seq // tq
    half = FOX_HEAD_DIM
    lane = lax.broadcasted_iota(jnp.int32, (1, 2 * LANES), 1)
    bias0 = LANES + 16 * pl.program_id(1)
    own0 = (lane < half) | ((lane >= bias0) & (lane < bias0 + 8))
    own1 = ((lane >= half) & (lane < LANES)) | ((lane >= bias0 + 8) & (lane < bias0 + 16))
    ones_rows = jnp.ones((2 * SUBLANES, tk), BF16)
    key_in_tile = lax.broadcasted_iota(jnp.int32, (tk, 2 * tq), 0)
    qry_in_tile = lax.broadcasted_iota(jnp.int32, (tk, 2 * tq), 1) & (tq - 1)
    bufs = (sa_ref, sb_ref)
    key_tiles = [-(-(i + 1) * tq // tk) for i in range(nq)]
    steps = [(i, j) for i in range(nq) for j in range(key_tiles[i])]
    q_cache = {}

    def q_both(i):
        if i not in q_cache:
            q = q_ref[i * tq:(i + 1) * tq, :]
            zq = jnp.zeros_like(q)
            q_cache[i] = jnp.concatenate([jnp.where(own0, q, zq), jnp.where(own1, q, zq)], axis=0)
        return q_cache[i]

    def scores(n):
        i, j = steps[n]
        s = lax.dot_general(k_ref[j * tk:(j + 1) * tk, :], q_both(i), (((1,), (1,)), ((), ())),
                            preferred_element_type=F32)
        if j * tk + tk - 1 > i * tq:
            s = jnp.where(key_in_tile + (j * tk - i * tq) <= qry_in_tile, s, NEG_BIG)
        bufs[n % 2][...] = s

    scores(0)
    for n, (i, j) in enumerate(steps):
        if n + 1 < len(steps):
            scores(n + 1)
        s_ref = bufs[n % 2]
        va = jnp.concatenate([vt_ref[:, j * tk:(j + 1) * tk], ones_rows], axis=0)
        for h in range(2):
            cols = slice(h * tq, (h + 1) * tq)
            if j == 0:
                m_new = jnp.max(s_ref[:, cols], axis=0, keepdims=True)
                p = jnp.exp(s_ref[:, cols] - m_new).astype(BF16)
                acc_ref[:, cols] = jnp.dot(va, p, preferred_element_type=F32)
            else:
                m_old = m_ref[:, cols]
                m_new = jnp.maximum(m_old, jnp.max(s_ref[:, cols], axis=0, keepdims=True))
                alpha = jnp.exp(m_old - m_new)
                p = jnp.exp(s_ref[:, cols] - m_new).astype(BF16)
                acc_ref[:, cols] = alpha * acc_ref[:, cols] + jnp.dot(va, p, preferred_element_type=F32)
            m_ref[:, cols] = m_new
        if j == key_tiles[i] - 1:
            acc = acc_ref[...]
            o_t = jnp.concatenate([acc[0:half, 0:tq] / acc[LANES:LANES + 1, 0:tq],
                                   acc[half:LANES, tq:2 * tq] / acc[LANES:LANES + 1, tq:2 * tq]], axis=0)
            o_ref[i * tq:(i + 1) * tq, :] = o_t.T.astype(o_ref.dtype)


def _attention(q, k, v_t, bsz, seq):
    n = q.shape[0]
    t = ATT_Q_TILE
    return pl.pallas_call(
        _attn_kernel,
        grid=(bsz, HEAD_PAIRS),
        in_specs=[pl.BlockSpec((seq, 2 * LANES), lambda b, p: (b, p)),
                  pl.BlockSpec((seq, 2 * LANES), lambda b, p: (b, p)),
                  pl.BlockSpec((LANES, seq), lambda b, p: (p, b))],
        out_specs=pl.BlockSpec((seq, LANES), lambda b, p: (b, p)),
        out_shape=jax.ShapeDtypeStruct((n, FOX_WIDTH), BF16),
        scratch_shapes=[pltpu.VMEM((1, 2 * t), F32), pltpu.VMEM((LANES + 2 * SUBLANES, 2 * t), F32),
                        pltpu.VMEM((ATT_K_TILE, 2 * t), F32), pltpu.VMEM((ATT_K_TILE, 2 * t), F32)],
        compiler_params=_params(("arbitrary", "arbitrary")),
        name="attn",
    )(q, k, v_t)


def _ssm_prep_kernel(lrow_ref, lcol_ref, ldt_ref, btr_ref, bti_ref, ctr_ref, cti_ref,
                     toep_ref, bst_ref, bsw_ref, cpw_ref, a_ref):
    p8 = pl.program_id(0) % 8
    t_len, grp = SSM_CHUNK, SSM_GROUP
    dt = jnp.exp(ldt_ref[...])
    lr, li = jnp.minimum(lrow_ref[0:1, :], LAMBDA_RE_MAX), lrow_ref[1:2, :]

    def powers(steps, re, im):
        mag = jnp.exp(steps * (re * dt))
        return mag * jnp.cos(steps * (im * dt)), mag * jnp.sin(steps * (im * dt))

    a_re, a_im = powers(1.0, lr, li)
    den = lr * lr + li * li
    nr = a_re - 1.0
    co_re = (nr * lr + a_im * li) / den
    co_im = (a_im * lr - nr * li) / den
    bbt_re = co_re * btr_ref[...] - co_im * bti_ref[...]
    bbt_im = co_re * bti_ref[...] + co_im * btr_ref[...]

    lag = lax.broadcasted_iota(jnp.int32, (1, t_len * grp), 1) // grp
    lcr, lci = jnp.minimum(lcol_ref[:, 0:1], LAMBDA_RE_MAX), lcol_ref[:, 1:2]

    def cmul(x, y):
        return x[0] * y[0] - x[1] * y[1], x[0] * y[1] + x[1] * y[0]

    a_col = powers(1.0, lcr, lci)
    a_lag = None
    a_bit = a_col
    for bit in range(4):
        use = ((lag >> bit) & 1) == 1
        factor = (jnp.where(use, a_bit[0], 1.0), jnp.where(use, a_bit[1], 0.0))
        a_lag = factor if a_lag is None else cmul(a_lag, factor)
        a_bit = cmul(a_bit, a_bit)

    def c_times(power):
        return cmul((ctr_ref[...], cti_ref[...]), power)

    wt_re, wt_im = c_times(a_lag)
    kern = (jnp.dot(bbt_re, wt_re, precision=HIGHEST, preferred_element_type=F32)
            - jnp.dot(bbt_im, wt_im, precision=HIGHEST, preferred_element_type=F32))

    lane = lax.broadcasted_iota(jnp.int32, (1, LANES), 1)
    col_shift = p8 * grp

    def store_cols(ref, rows, lo_half, hi_half):
        ref[rows, 0:LANES] = pltpu.roll(lo_half, col_shift, axis=1).astype(ref.dtype)
        ref[rows, LANES:2 * LANES] = pltpu.roll(hi_half, col_shift, axis=1).astype(ref.dtype)

    def slot_rows(s):
        half, s8 = divmod(s, 8)
        return pl.ds(pl.multiple_of((8 * half + (s8 + p8) % 8) * grp, grp), grp)

    back = (t_len - 1 - lax.broadcasted_iota(jnp.int32, (t_len, 1), 0)).astype(F32)
    e_re, e_im = powers(back, lr, li)
    zero = jnp.zeros((grp, LANES), F32)
    k_lo, k_hi = kern[:, 0:LANES], kern[:, LANES:2 * LANES]
    for s in range(t_len):
        half, s8 = divmod(s, 8)
        keep = lane >= s8 * grp
        r_lo = pltpu.roll(k_lo, s8 * grp, axis=1) if s8 else k_lo
        r_hi = pltpu.roll(k_hi, s8 * grp, axis=1) if s8 else k_hi
        if half == 0:
            lo, hi = jnp.where(keep, r_lo, 0.0), jnp.where(keep, r_hi, r_lo)
        else:
            lo, hi = zero, jnp.where(keep, r_lo, 0.0)
        store_cols(toep_ref, slot_rows(s), lo, hi)
        es_re, es_im = e_re[s:s + 1, :], e_im[s:s + 1, :]
        bs_re = es_re * bbt_re - es_im * bbt_im
        bs_im = es_re * bbt_im + es_im * bbt_re
        bst_ref[slot_rows(s), :] = jnp.concatenate([bs_re, bs_im], axis=1).astype(bst_ref.dtype)
        bsw_ref[slot_rows(s), :] = jnp.concatenate([bs_im, bs_re], axis=1).astype(bsw_ref.dtype)

    w1_re, w1_im = c_times(cmul(a_lag, a_col))
    store_cols(cpw_ref, pl.ds(0, SSM_STATE), w1_re[:, 0:LANES], w1_re[:, LANES:2 * LANES])
    store_cols(cpw_ref, pl.ds(SSM_STATE, SSM_STATE), -w1_im[:, 0:LANES], -w1_im[:, LANES:2 * LANES])
    s_re, s_im = powers(float(t_len), lr, li)
    a_ref[0:1, :] = jnp.concatenate([s_re, s_re], axis=1)
    a_ref[1:2, :] = jnp.concatenate([-s_im, s_im], axis=1)


def _ssm_prep(lambda_re, lambda_im, log_dt, b_re, b_im, c_re, c_im):
    width = SSM_CHUNK * SSM_GROUP
    lam_row = jnp.stack([lambda_re, lambda_im], axis=1)
    tiled = lambda c: jnp.tile(c.transpose(0, 2, 1), (1, 1, SSM_CHUNK))
    per = lambda a, b: pl.BlockSpec((None, a, b), lambda g: (g, 0, 0))
    return pl.pallas_call(
        _ssm_prep_kernel,
        grid=(SSM_GROUPS,),
        in_specs=[per(2, SSM_STATE), per(SSM_STATE, 2), per(1, 1), per(SSM_GROUP, SSM_STATE),
                  per(SSM_GROUP, SSM_STATE), per(SSM_STATE, width), per(SSM_STATE, width)],
        out_specs=[per(width, width), per(width, 2 * SSM_STATE), per(width, 2 * SSM_STATE),
                   per(2 * SSM_STATE, width), per(2, 2 * SSM_STATE)],
        out_shape=[jax.ShapeDtypeStruct((SSM_GROUPS, width, width), BF16),
                   jax.ShapeDtypeStruct((SSM_GROUPS, width, 2 * SSM_STATE), BF16),
                   jax.ShapeDtypeStruct((SSM_GROUPS, width, 2 * SSM_STATE), BF16),
                   jax.ShapeDtypeStruct((SSM_GROUPS, 2 * SSM_STATE, width), BF16),
                   jax.ShapeDtypeStruct((SSM_GROUPS, 2, 2 * SSM_STATE), F32)],
        compiler_params=_params(("arbitrary",)),
        name="ssm_prep",
    )(lam_row, lam_row.transpose(0, 2, 1), log_dt.reshape(SSM_GROUPS, 1, 1),
      b_re.transpose(0, 2, 1), b_im.transpose(0, 2, 1), tiled(c_re), tiled(c_im))


SSM_GROUPS_PER_STEP = 4


def _ssm_kernel(n_chunks, bsz, u_ref, toep_ref, bst_ref, bsw_ref, cpw_ref, a_ref, y_ref,
                contrib_ref, cswap_ref, xprev_ref):
    groups = u_ref.shape[0]
    for k in range(groups):
        u = u_ref[k]
        contrib_ref[k] = jnp.dot(u, bst_ref[k], preferred_element_type=F32)
        cswap_ref[k] = jnp.dot(u, bsw_ref[k], preferred_element_type=F32)
    a1 = [a_ref[k, 0:1, :] for k in range(groups)]
    a2 = [a_ref[k, 1:2, :] for k in range(groups)]

    def step(n, carry):
        rows = pl.ds(n, bsz, stride=n_chunks)
        new = []
        for k in range(groups):
            x, xs = carry[2 * k], carry[2 * k + 1]
            xprev_ref[k, rows, :] = x
            new.append(a1[k] * x + a2[k] * xs + contrib_ref[k, rows, :])
            new.append(a1[k] * xs - a2[k] * x + cswap_ref[k, rows, :])
        return tuple(new)

    zero = jnp.zeros((bsz, 2 * SSM_STATE), F32)
    lax.fori_loop(0, n_chunks, step, (zero,) * (2 * groups), unroll=2)
    for k in range(groups):
        y_ref[k] = (jnp.dot(u_ref[k], toep_ref[k], preferred_element_type=F32)
                    + jnp.dot(xprev_ref[k].astype(BF16), cpw_ref[k], preferred_element_type=F32))


def _ssm(u_flat, toep, b_state, b_swap, c_pow, a_step, bsz):
    g, rows, w = u_flat.shape
    gb = SSM_GROUPS_PER_STEP
    per = lambda a, b: pl.BlockSpec((gb, a, b), lambda i: (i, 0, 0))
    state = pltpu.VMEM((gb, rows, 2 * SSM_STATE), F32)
    return pl.pallas_call(
        functools.partial(_ssm_kernel, rows // bsz, bsz),
        grid=(g // gb,),
        in_specs=[per(rows, w), per(w, w), per(w, 2 * SSM_STATE), per(w, 2 * SSM_STATE),
                  per(2 * SSM_STATE, w), per(2, 2 * SSM_STATE)],
        out_specs=per(rows, w),
        out_shape=jax.ShapeDtypeStruct((g, rows, w), F32),
        scratch_shapes=[state, state, state],
        compiler_params=_params(("arbitrary",)),
        name="ssm",
    )(u_flat, toep, b_state, b_swap, c_pow, a_step)


ROW_SLABS = D_MODEL // LANES // 2
_HIGH_HALF = 0xFFFF0000


def _store_row_tiles(ref, value):
    rows = value.shape[0]
    bits = lax.bitcast_convert_type(value.astype(BF16).astype(F32), jnp.uint32)
    for j in range(ROW_SLABS):
        low = bits[:, j * LANES:(j + 1) * LANES] >> 16
        high = bits[:, (j + ROW_SLABS) * LANES:(j + ROW_SLABS + 1) * LANES] & jnp.uint32(_HIGH_HALF)
        ref[pl.ds(j, rows, stride=ROW_SLABS), :] = high | low


def _load_row_tiles(ref, rows):
    words = [ref[pl.ds(j, rows, stride=ROW_SLABS), :] for j in range(ROW_SLABS)]
    low = [lax.bitcast_convert_type(w << 16, F32) for w in words]
    high = [lax.bitcast_convert_type(w & jnp.uint32(_HIGH_HALF), F32) for w in words]
    return jnp.concatenate(low + high, axis=1)


def _row_tile_copy(src_ref, src_row, dst_ref, dst_row, sem):
    src = src_ref.at[pl.ds(pl.multiple_of(src_row * ROW_SLABS, ROW_SLABS), ROW_SLABS), :]
    dst = dst_ref.at[pl.ds(pl.multiple_of(dst_row * ROW_SLABS, ROW_SLABS), ROW_SLABS), :]
    return pltpu.make_async_copy(src, dst, sem)


def _mix_kernel(x_ref, of_ref, yf_ref, u_ref, ga_ref, gb_ref, mod_ref, dsk_ref, wglu_ref, wfox_ref,
                wssm_ref, wo_ref, g2_ref, wr_ref, br_ref, x1_ref, h2_ref, lg_ref, ytok_ref):
    sub = MIX_SUB_TILE
    for s in range(x_ref.shape[0] // sub):
        rows = pl.ds(s * sub, sub)
        chunks = pl.ds(s * sub // SSM_CHUNK, sub // SSM_CHUNK)
        _to_token_major(yf_ref.at[:, chunks, :], ytok_ref.at[s], sub // SSM_CHUNK)
        y_ssm = jnp.concatenate([ytok_ref[s, j] for j in range(SSM_WIDTH // LANES)], axis=1)
        y = y_ssm + dsk_ref[...] * u_ref[rows, :]
        y = 0.5 * y * (1.0 + jnp.tanh(math.sqrt(2.0 / math.pi) * (y + 0.044715 * (y * y * y))))
        gl = jnp.dot(y.astype(BF16), wglu_ref[...], preferred_element_type=F32)
        o_ssm = gl[:, :SSM_WIDTH] * _sigmoid(gl[:, SSM_WIDTH:])
        merged = (ga_ref[rows, :].astype(F32) * jnp.dot(of_ref[rows, :], wfox_ref[...],
                                                        preferred_element_type=F32)
                  + gb_ref[rows, :].astype(F32) * jnp.dot(o_ssm.astype(BF16), wssm_ref[...],
                                                          preferred_element_type=F32))
        x1 = x_ref[rows, :] + mod_ref[2:3, :] * jnp.dot(merged.astype(BF16), wo_ref[...],
                                                         preferred_element_type=F32)
        x1_ref[rows, :] = x1
        h2 = _rms_modulate(x1, g2_ref[...], mod_ref[3:4, :], mod_ref[4:5, :])
        _store_row_tiles(h2_ref.at[pl.ds(s * sub * ROW_SLABS, sub * ROW_SLABS), :], h2)
        a1 = _top_bits(h2)
        a2 = _top_bits(h2 - a1)
        lhs = jnp.concatenate([a1, a1, a2], axis=1).astype(BF16)
        lg_ref[rows, :] = jnp.dot(lhs, wr_ref[...], preferred_element_type=F32) + br_ref[...]


def _mix(x2, o_fox, y_flat, u, sga, sgb, mod3, d_skip, w_glu, w_fox, w_ssm, w_o, g2, w_r, b_r, seq):
    n, d = x2.shape
    tm = MIX_TILE
    tpb = seq // tm
    tok = lambda w: pl.BlockSpec((tm, w), lambda i: (i, 0))
    const = lambda a: pl.BlockSpec(a.shape, lambda i: (0,) * a.ndim)
    flat = pl.BlockSpec((SSM_GROUPS, tm // SSM_CHUNK, SSM_CHUNK * SSM_GROUP), lambda i: (0, i, 0))
    return pl.pallas_call(
        _mix_kernel,
        grid=(n // tm,),
        in_specs=[tok(d), tok(FOX_WIDTH), flat, tok(SSM_WIDTH), tok(d), tok(d),
                  pl.BlockSpec((None, N_MOD, d), lambda i: (i // tpb, 0, 0)),
                  const(d_skip), const(w_glu), const(w_fox), const(w_ssm), const(w_o), const(g2),
                  const(w_r), const(b_r)],
        out_specs=[tok(d), pl.BlockSpec((tm * ROW_SLABS, LANES), lambda i: (i, 0)), tok(LANES)],
        out_shape=[jax.ShapeDtypeStruct((n, d), F32), jax.ShapeDtypeStruct((n * ROW_SLABS, LANES), jnp.uint32),
                   jax.ShapeDtypeStruct((n, LANES), F32)],
        scratch_shapes=[pltpu.VMEM((tm // MIX_SUB_TILE, SSM_WIDTH // LANES, MIX_SUB_TILE, LANES), F32)],
        compiler_params=_params(("arbitrary",)),
        name="mix",
    )(x2, o_fox, y_flat, u, sga, sgb, mod3, d_skip, w_glu, w_fox, w_ssm, w_o, g2, w_r, b_r)


def _route_kernel(lg_ref, tri_ref, er_ref, wt_ref, cnt_ref, carry_ref):
    i = pl.program_id(0)

    @pl.when(i == 0)
    def _():
        carry_ref[...] = jnp.zeros_like(carry_ref)

    lg = lg_ref[...].T
    tm = lg.shape[1]
    row = lax.broadcasted_iota(jnp.int32, (LANES, tm), 0)
    neg = jnp.full_like(lg, -jnp.inf)

    def first_argmax(vals):
        mx = jnp.max(vals, axis=0, keepdims=True)
        ix = jnp.min(jnp.where(vals == mx, row, LANES), axis=0, keepdims=True)
        return mx, ix

    is_group = row < N_GROUPS
    g_max, gi = first_argmax(jnp.where(is_group, lg, neg))
    g_sum = jnp.sum(jnp.where(is_group, jnp.exp(lg - g_max), 0.0), axis=0, keepdims=True)
    p_group = 1.0 / g_sum
    lo = N_GROUPS + EXPERTS_PER_GROUP * gi
    in_group = (row >= lo) & (row < lo + EXPERTS_PER_GROUP)
    cand = jnp.where(in_group, lg, neg)
    v1, i1 = first_argmax(cand)
    v2, i2 = first_argmax(jnp.where(row == i1, neg, cand))
    tt = jnp.exp(v2 - v1)
    w1 = p_group / (1.0 + tt)
    w2 = p_group * tt / (1.0 + tt)
    e1 = i1 - N_GROUPS
    e2 = i2 - N_GROUPS
    sel1 = row == e1
    sel2 = row == e2
    onehot = (sel1 | sel2).astype(F32)
    before = jnp.dot(onehot.astype(BF16), tri_ref[...], preferred_element_type=F32) + carry_ref[:, 0:1]
    r1 = jnp.sum(jnp.where(sel1, before, 0.0), axis=0, keepdims=True).astype(jnp.int32)
    r2 = jnp.sum(jnp.where(sel2, before, 0.0), axis=0, keepdims=True).astype(jnp.int32)
    total = before[:, tm - 1:tm] + onehot[:, tm - 1:tm]
    carry_ref[...] = jnp.broadcast_to(total, carry_ref.shape)
    cnt_ref[...] = jnp.broadcast_to(total, cnt_ref.shape)
    slot = lax.broadcasted_iota(jnp.int32, (SUBLANES, tm), 0)
    er_ref[...] = jnp.where(slot == 0, e1, jnp.where(slot == 1, e2, jnp.where(slot == 2, r1, r2)))
    wt_ref[...] = jnp.where(row == 0, w1, jnp.where(row == 1, w2, 0.0)).T


def _route(logits):
    n = logits.shape[0]
    tm = TOK_TILE
    tri = jnp.triu(jnp.ones((tm, tm), BF16), k=1)
    tok = pl.BlockSpec((tm, LANES), lambda i: (i, 0))
    return pl.pallas_call(
        _route_kernel,
        grid=(n // tm,),
        in_specs=[tok, pl.BlockSpec((tm, tm), lambda i: (0, 0))],
        out_specs=[pl.BlockSpec((SUBLANES, tm), lambda i: (0, i)), tok,
                   pl.BlockSpec((LANES, LANES), lambda i: (0, 0))],
        out_shape=[jax.ShapeDtypeStruct((SUBLANES, n), jnp.int32), jax.ShapeDtypeStruct((n, LANES), F32),
                   jax.ShapeDtypeStruct((LANES, LANES), F32)],
        scratch_shapes=[pltpu.VMEM((LANES, LANES), F32)],
        compiler_params=_params(("arbitrary",)),
        name="route",
    )(logits, tri)


ISSUE_UNROLL = 32


def _dispatch_kernel(n_steps, dest_ref, h_ref, rows_in_ref, rows_ref, stage_ref, sem):
    del rows_in_ref
    i = pl.program_id(0)
    tm = h_ref.shape[0] // ROW_SLABS
    cur = i % 2
    stage_ref[cur] = h_ref[...]

    def issue(g, c):
        for j in range(ISSUE_UNROLL):
            t = g * ISSUE_UNROLL + j
            _row_tile_copy(stage_ref.at[cur], t, rows_ref, dest_ref[0, 0, t], sem.at[cur]).start(priority=0)
            _row_tile_copy(stage_ref.at[cur], t, rows_ref, dest_ref[0, 0, tm + t],
                           sem.at[cur]).start(priority=1)
        return c

    lax.fori_loop(0, tm // ISSUE_UNROLL, issue, 0)

    def drain(which):
        for _ in range(2):
            pltpu.make_async_copy(stage_ref.at[which], rows_ref.at[pl.ds(0, tm * ROW_SLABS), :],
                                  sem.at[which]).wait()

    @pl.when(i > 0)
    def _():
        drain(1 - cur)

    @pl.when(i == n_steps - 1)
    def _():
        drain(cur)


def _dispatch(dest3, h2_tiles, rows_zero):
    tm = MOVE_TILE
    n = h2_tiles.shape[0] // ROW_SLABS
    return pl.pallas_call(
        functools.partial(_dispatch_kernel, n // tm),
        grid=(n // tm,),
        in_specs=[pl.BlockSpec((1, 1, 2 * tm), lambda i: (i, 0, 0), memory_space=pltpu.SMEM),
                  pl.BlockSpec((tm * ROW_SLABS, LANES), lambda i: (i, 0)),
                  pl.BlockSpec(memory_space=pl.ANY)],
        out_specs=pl.BlockSpec(memory_space=pl.ANY),
        out_shape=jax.ShapeDtypeStruct(rows_zero.shape, rows_zero.dtype),
        scratch_shapes=[pltpu.VMEM((2, tm * ROW_SLABS, LANES), jnp.uint32), pltpu.SemaphoreType.DMA((2,))],
        input_output_aliases={2: 0},
        compiler_params=_params(("arbitrary",)),
        name="dispatch",
    )(dest3, h2_tiles, rows_zero)


def _combine_kernel(n_steps, dest_ref, dnext_ref, x1_ref, wt_ref, mod_ref, gf_ref, yr_ref, o_ref,
                    buf_ref, sem):
    i = pl.program_id(0)
    tm = x1_ref.shape[0]

    def gather(idx_ref, which):
        def issue(g, c):
            for j in range(ISSUE_UNROLL):
                t = g * ISSUE_UNROLL + j
                _row_tile_copy(yr_ref, idx_ref[0, 0, t], buf_ref.at[which, 0], t,
                               sem.at[which]).start(priority=0)
                _row_tile_copy(yr_ref, idx_ref[0, 0, tm + t], buf_ref.at[which, 1], t,
                               sem.at[which]).start(priority=1)
            return c

        lax.fori_loop(0, tm // ISSUE_UNROLL, issue, 0)

    cur = i % 2

    @pl.when(i == 0)
    def _():
        gather(dest_ref, 0)

    @pl.when(i + 1 < n_steps)
    def _():
        gather(dnext_ref, 1 - cur)

    for slot in range(2):
        pltpu.make_async_copy(yr_ref.at[pl.ds(0, tm * ROW_SLABS), :], buf_ref.at[cur, slot],
                              sem.at[cur]).wait()
    wt = wt_ref[...]
    moe = (wt[:, 0:1] * _load_row_tiles(buf_ref.at[cur, 0], tm)
           + wt[:, 1:2] * _load_row_tiles(buf_ref.at[cur, 1], tm))
    x = x1_ref[...] + mod_ref[5:6, :] * moe
    ms = jnp.mean(x * x, axis=-1, keepdims=True)
    o_ref[...] = (x * lax.rsqrt(ms + RMS_EPS)) * gf_ref[...]


def _combine(dest3, x1, wts, mod3, final_g, y_rows, seq):
    n, d = x1.shape
    tm = MOVE_TILE
    tpb = seq // tm
    n_steps = n // tm
    idx_spec = lambda f: pl.BlockSpec((1, 1, 2 * tm), f, memory_space=pltpu.SMEM)
    return pl.pallas_call(
        functools.partial(_combine_kernel, n_steps),
        grid=(n_steps,),
        in_specs=[idx_spec(lambda i: (i, 0, 0)),
                  idx_spec(lambda i: (jnp.minimum(i + 1, n_steps - 1), 0, 0)),
                  pl.BlockSpec((tm, d), lambda i: (i, 0)),
                  pl.BlockSpec((tm, LANES), lambda i: (i, 0)),
                  pl.BlockSpec((None, N_MOD, d), lambda i: (i // tpb, 0, 0)),
                  pl.BlockSpec((1, d), lambda i: (0, 0)),
                  pl.BlockSpec(memory_space=pl.ANY)],
        out_specs=pl.BlockSpec((tm, d), lambda i: (i, 0)),
        out_shape=jax.ShapeDtypeStruct((n, d), F32),
        scratch_shapes=[pltpu.VMEM((2, 2, tm * ROW_SLABS, LANES), jnp.uint32),
                        pltpu.SemaphoreType.DMA((2,))],
        compiler_params=_params(("arbitrary",)),
        name="combine",
    )(dest3, dest3, x1, wts, mod3, final_g, y_rows)


def _expert_kernel(be_ref, seg_ref, nxt_ref, nv_ref, x_ref, wg_hbm, wu_hbm, wd_hbm, y_ref,
                   wg_buf, wu_buf, wd_buf, wgb_ref, wub_ref, wdb_ref, sem):
    i = pl.program_id(0)
    valid = i < nv_ref[0]
    first = (i == 0) | (be_ref[i] != be_ref[jnp.maximum(i - 1, 0)])
    slot = seg_ref[i] % 2

    def weight_copies(e, s):
        return [pltpu.make_async_copy(hbm.at[e], buf.at[s], sem.at[s])
                for hbm, buf in ((wg_hbm, wg_buf), (wu_hbm, wu_buf), (wd_hbm, wd_buf))]

    @pl.when(valid & (i == 0))
    def _():
        for c in weight_copies(be_ref[0], 0):
            c.start()

    @pl.when(valid & first)
    def _():
        for c in weight_copies(be_ref[i], slot):
            c.wait()

        @pl.when(nxt_ref[i] >= 0)
        def _():
            for c in weight_copies(nxt_ref[i], 1 - slot):
                c.start()

        wgb_ref[...] = wg_buf[slot].astype(BF16)
        wub_ref[...] = wu_buf[slot].astype(BF16)
        wdb_ref[...] = wd_buf[slot].astype(BF16)

    @pl.when(valid)
    def _():
        xb = _load_row_tiles(x_ref, x_ref.shape[0] // ROW_SLABS).astype(BF16)
        a = jnp.dot(xb, wgb_ref[...], preferred_element_type=F32)
        b = jnp.dot(xb, wub_ref[...], preferred_element_type=F32)
        hid = (a * _sigmoid(a)) * b
        _store_row_tiles(y_ref, jnp.dot(hid.astype(BF16), wdb_ref[...], preferred_element_type=F32))


def _experts(blk_e, n_valid, x_rows, w_gate, w_up, w_down):
    d = D_MODEL
    rows = x_rows.shape[0] // ROW_SLABS
    tb = ROW_BLOCK
    n_blocks = rows // tb
    idx = jnp.arange(n_blocks, dtype=jnp.int32)
    change = (idx == 0) | (blk_e != jnp.roll(blk_e, 1))
    seg = jnp.cumsum(change.astype(jnp.int32)) - 1
    later_start = (idx[None, :] > idx[:, None]) & change[None, :] & (idx[None, :] < n_valid[0])
    none = jnp.int32(N_EXPERTS)
    nxt = jnp.min(jnp.where(later_start, blk_e[None, :], none), axis=1)
    nxt = jnp.where(nxt == none, -1, nxt)
    row_spec = pl.BlockSpec((tb * ROW_SLABS, LANES),
                            lambda i, be, sg, nx, nv: (jnp.minimum(i, nv[0] - 1), 0))
    grid_spec = pltpu.PrefetchScalarGridSpec(
        num_scalar_prefetch=4,
        grid=(n_blocks,),
        in_specs=[row_spec, pl.BlockSpec(memory_space=pl.ANY), pl.BlockSpec(memory_space=pl.ANY),
                  pl.BlockSpec(memory_space=pl.ANY)],
        out_specs=row_spec,
        scratch_shapes=[pltpu.VMEM((2, d, D_EXPERT), F32), pltpu.VMEM((2, d, D_EXPERT), F32),
                        pltpu.VMEM((2, D_EXPERT, d), F32),
                        pltpu.VMEM((d, D_EXPERT), BF16), pltpu.VMEM((d, D_EXPERT), BF16),
                        pltpu.VMEM((D_EXPERT, d), BF16), pltpu.SemaphoreType.DMA((2,))],
    )
    return pl.pallas_call(
        _expert_kernel,
        grid_spec=grid_spec,
        out_shape=jax.ShapeDtypeStruct(x_rows.shape, x_rows.dtype),
        input_output_aliases={4: 0},
        compiler_params=_params(("arbitrary",)),
        name="experts",
    )(blk_e, seg, nxt.astype(jnp.int32), n_valid, x_rows, w_gate, w_up, w_down)


def kernel(x, c, w_ada, b_ada, norm_mix_g, w_in, b_forget, w_out_fox, lambda_re, lambda_im, log_dt,
           ssm_b_re, ssm_b_im, ssm_c_re, ssm_c_im, d_skip, w_glu, w_out_ssm, w_o, norm_ffn_g,
           w_router_group, b_router_group, w_router_expert, b_router_expert, w_gate_e, w_up_e,
           w_down_e, final_g):
    bsz, seq, d = x.shape
    n = bsz * seq
    assert w_ada.shape[0] == 1, "the final RMSNorm is fused into the (single) layer's combine kernel"
    xc = x.reshape(n, d)
    for l in range(1):
        mod3 = _mod(c, w_ada[l], b_ada[l]).reshape(bsz, N_MOD, d)

        wi = w_in[l]
        s_q, s_k, s_v, s_f, s_u, s_ga = 512, 1024, 1536, 1544, 2056, 3080
        scale = FOX_HEAD_DIM ** -0.5
        w_all = jnp.concatenate(
            [wi[:, :s_q] * scale, wi[:, s_q:s_k], wi[:, s_f:s_u], wi[:, s_u:s_ga],
             wi[:, s_ga:], jnp.pad(wi[:, s_v:s_f], ((0, 0), (0, LANES - FOX_HEADS)))],
            axis=1).astype(BF16)
        w_vt = wi[:, s_k:s_v].T.astype(BF16)
        bf_pad = jnp.pad(b_forget[l], (0, LANES - FOX_HEADS)).reshape(1, LANES)
        q, k, v_t, u, u_flat, sga, sgb = _inproj(xc, mod3, norm_mix_g[l].reshape(1, d), w_all, w_vt,
                                                 bf_pad, seq)

        o_fox = _attention(q, k, v_t, bsz, seq)

        toep, b_state, b_swap, c_pow, a_step = _ssm_prep(
            lambda_re[l], lambda_im[l], log_dt[l], ssm_b_re[l], ssm_b_im[l], ssm_c_re[l], ssm_c_im[l])
        y_flat = _ssm(u_flat, toep, b_state, b_swap, c_pow, a_step, bsz)

        w_r = jnp.pad(jnp.concatenate([w_router_group[l], w_router_expert[l]], axis=1),
                      ((0, 0), (0, LANES - N_GROUPS - N_EXPERTS)))
        w_r1 = _top_bits(w_r)
        w_r2 = _top_bits(w_r - w_r1)
        w_r = jnp.concatenate([w_r1, w_r2, w_r1], axis=0).astype(BF16)
        b_r = jnp.pad(jnp.concatenate([b_router_group[l], b_router_expert[l]]),
                      (0, LANES - N_GROUPS - N_EXPERTS)).reshape(1, LANES)
        x1, h2, logits = _mix(xc, o_fox, y_flat, u, sga, sgb, mod3, d_skip[l].reshape(1, SSM_WIDTH),
                              w_glu[l].astype(BF16), w_out_fox[l].astype(BF16),
                              w_out_ssm[l].astype(BF16), w_o[l].astype(BF16),
                              norm_ffn_g[l].reshape(1, d), w_r, b_r, seq)

        er, wts, cnt = _route(logits)
        counts = cnt[:N_EXPERTS, 0].astype(jnp.int32)
        pcounts = ((counts + ROW_BLOCK - 1) // ROW_BLOCK) * ROW_BLOCK
        pends = jnp.cumsum(pcounts)
        pstarts = pends - pcounts
        hit = er[0:2, None, :] == jnp.arange(N_EXPERTS, dtype=jnp.int32)[None, :, None]
        dest = jnp.sum(jnp.where(hit, pstarts[None, :, None], 0), axis=1) + er[2:4]
        rows = 2 * n + N_EXPERTS * ROW_BLOCK
        n_blocks = rows // ROW_BLOCK
        blk_start = jnp.arange(n_blocks, dtype=jnp.int32) * ROW_BLOCK
        blk_e = jnp.minimum(jnp.sum((pends[None, :] <= blk_start[:, None]).astype(jnp.int32), axis=1),
                            N_EXPERTS - 1)
        n_valid = (pends[-1:] // ROW_BLOCK).astype(jnp.int32)
        dest3 = (dest.astype(jnp.int32).reshape(2, n // MOVE_TILE, MOVE_TILE).transpose(1, 0, 2)
                 .reshape(n // MOVE_TILE, 1, 2 * MOVE_TILE))

        x_rows = _dispatch(dest3, h2, jnp.zeros((rows * ROW_SLABS, LANES), jnp.uint32))
        y_rows = _experts(blk_e, n_valid, x_rows, w_gate_e[l], w_up_e[l], w_down_e[l])
        xc = _combine(dest3, x1, wts, mod3, final_g.reshape(1, d), y_rows, seq)
    return xc.reshape(bsz, seq, d)
```

```python
import functools
import math

import jax
import jax.numpy as jnp
import numpy as np
from jax import lax
from jax.experimental import pallas as pl
from jax.experimental.pallas import tpu as pltpu

F32 = jnp.float32
BF16 = jnp.bfloat16

D_MODEL = 1024
N_MOD = 6
RMS_EPS = 1e-6
FOX_HEADS = 8
FOX_HEAD_DIM = 64
FOX_WIDTH = FOX_HEADS * FOX_HEAD_DIM
HEAD_PAIRS = FOX_HEADS // 2
SSM_WIDTH = 512
SSM_GROUP = 16
SSM_GROUPS = SSM_WIDTH // SSM_GROUP
SSM_STATE = 64
LAMBDA_RE_MAX = -1e-4
N_GROUPS = 4
EXPERTS_PER_GROUP = 8
N_EXPERTS = N_GROUPS * EXPERTS_PER_GROUP
D_EXPERT = 512

LANES = 128
SUBLANES = 8
VMEM_LIMIT = 56 * 1024 * 1024

SSM_CHUNK = 16
TOK_TILE = 512
MIX_TILE = 512
MIX_SUB_TILE = 256
ATT_Q_TILE = 512
ATT_K_TILE = 512
ROW_BLOCK = 512
MOVE_TILE = 512
NEG_BIG = -1e30

HIGHEST = lax.Precision.HIGHEST


def _params(sem):
    return pltpu.CompilerParams(dimension_semantics=sem, vmem_limit_bytes=VMEM_LIMIT)


def _sigmoid(x):
    return 0.5 * jnp.tanh(0.5 * x) + 0.5


def _rms_modulate(x, gain, shift, scale):
    ms = jnp.mean(x * x, axis=-1, keepdims=True)
    return (x * lax.rsqrt(ms + RMS_EPS)) * gain * (1.0 + scale) + shift


def _mod_kernel(c_ref, w_ref, b_ref, o_ref):
    c = c_ref[...]
    ca = (c * jax.nn.sigmoid(c)).astype(BF16)
    o_ref[...] = jnp.dot(ca, w_ref[...].astype(BF16), preferred_element_type=F32) + b_ref[...]


def _mod(c, w_ada, b_ada):
    bsz, d = c.shape
    cols = w_ada.shape[1]
    tn = 1536
    return pl.pallas_call(
        _mod_kernel,
        grid=(cols // tn,),
        in_specs=[pl.BlockSpec((bsz, d), lambda j: (0, 0)),
                  pl.BlockSpec((d, tn), lambda j: (0, j)),
                  pl.BlockSpec((1, tn), lambda j: (0, j))],
        out_specs=pl.BlockSpec((bsz, tn), lambda j: (0, j)),
        out_shape=jax.ShapeDtypeStruct((bsz, cols), F32),
        compiler_params=_params(("arbitrary",)),
        name="mod",
    )(c, w_ada, b_ada.reshape(1, cols))


_C_Q, _C_K, _C_U, _C_GA, _C_GB, _C_F, _C_END = 0, 512, 1024, 1536, 2560, 3584, 3712


def _lane_block():
    return lax.broadcasted_iota(jnp.int32, (1, LANES), 1) // SSM_GROUP


def _to_group_major(tok_ref, flat_ref, rows):
    blk = _lane_block()
    for half in range(2):
        for j in range(SSM_WIDTH // LANES):
            w = []
            for s8 in range(8):
                v = tok_ref[j, pl.ds(8 * half + s8, rows, stride=SSM_CHUNK), :]
                w.append(pltpu.roll(v, s8 * SSM_GROUP, axis=1) if s8 else v)
            for p in range(8):
                acc = w[0]
                for s8 in range(1, 8):
                    acc = jnp.where(blk == (p + s8) % 8, w[s8], acc)
                flat_ref[8 * j + p, :, half * LANES:(half + 1) * LANES] = acc.astype(flat_ref.dtype)


def _to_token_major(flat_ref, tok_ref, rows):
    blk = _lane_block()
    for half in range(2):
        for j in range(SSM_WIDTH // LANES):
            ys = [flat_ref[8 * j + p, :, half * LANES:(half + 1) * LANES] for p in range(8)]
            for s8 in range(8):
                acc = ys[0]
                for p in range(1, 8):
                    acc = jnp.where(blk == (p + s8) % 8, ys[p], acc)
                if s8:
                    acc = pltpu.roll(acc, LANES - s8 * SSM_GROUP, axis=1)
                tok_ref[j, pl.ds(8 * half + s8, rows, stride=SSM_CHUNK), :] = acc


def _bias_lane_placement():
    pq = np.zeros((3 * LANES, LANES), np.float32)
    pk = np.zeros((3 * LANES, LANES), np.float32)
    bq = np.zeros((1, LANES), np.float32)
    bk = np.zeros((1, LANES), np.float32)
    for head in range(FOX_HEADS):
        base = head * 8
        for term in range(3):
            pq[term * LANES + head, base + term] = 1.0
            pk[term * LANES + head, base + 3 + term] = -1.0
            bq[0, base + 3 + term] = 1.0
            bk[0, base + term] = 1.0
    return pq, pk, bq, bk


def _top_bits(a):
    bits = lax.bitcast_convert_type(a, jnp.uint32) & jnp.uint32(0xFFFF0000)
    return lax.bitcast_convert_type(bits, F32)


def _inproj_kernel(tiles_per_batch, x_ref, mod_ref, g_ref, w_ref, wvt_ref, bf_ref, tri_ref,
                   pq_ref, pk_ref, bq_ref, bk_ref,
                   q_ref, k_ref, vt_ref, u_ref, uflat_ref, ga_ref, gb_ref, carry_ref, uslab_ref):
    i = pl.program_id(0)
    h = _rms_modulate(x_ref[...], g_ref[...], mod_ref[0:1, :], mod_ref[1:2, :])
    hb = h.astype(BF16)

    def proj(a, b):
        return jnp.dot(hb, w_ref[:, a:b], preferred_element_type=F32)

    q = proj(_C_Q, _C_K).astype(BF16)
    k = proj(_C_K, _C_U).astype(BF16)
    vt_ref[...] = lax.dot_general(wvt_ref[...], hb, (((1,), (1,)), ((), ())),
                                  preferred_element_type=F32).astype(BF16)
    u = proj(_C_U, _C_GA)
    u_ref[...] = u
    for j in range(SSM_WIDTH // LANES):
        uslab_ref[j] = u[:, j * LANES:(j + 1) * LANES]
    _to_group_major(uslab_ref, uflat_ref, u.shape[0] // SSM_CHUNK)
    ga_ref[...] = _sigmoid(proj(_C_GA, _C_GB)).astype(BF16)
    gb_ref[...] = _sigmoid(proj(_C_GB, _C_F)).astype(BF16)

    f = proj(_C_F, _C_END) + bf_ref[...]
    logf = jnp.minimum(f, 0.0) - jnp.log(1.0 + jnp.exp(-jnp.abs(f)))

    @pl.when(i % tiles_per_batch == 0)
    def _():
        carry_ref[...] = jnp.zeros_like(carry_ref)

    def split3(a):
        hi = _top_bits(a)
        r1 = a - hi
        mid = _top_bits(r1)
        return jnp.concatenate([hi, mid, _top_bits(r1 - mid)], axis=1).astype(BF16)

    part = jnp.dot(tri_ref[...], split3(logf), preferred_element_type=F32)
    cs = (part[:, :LANES] + part[:, LANES:2 * LANES] + part[:, 2 * LANES:]) + carry_ref[0:1, :]
    carry_ref[...] = jnp.broadcast_to(cs[-1:, :], carry_ref.shape)

    terms = split3(cs)
    bias_q = (jnp.dot(terms, pq_ref[...], preferred_element_type=F32) + bq_ref[...]).astype(BF16)
    bias_k = (jnp.dot(terms, pk_ref[...], preferred_element_type=F32) + bk_ref[...]).astype(BF16)
    for p in range(HEAD_PAIRS):
        lanes = slice(p * LANES, (p + 1) * LANES)
        q_ref[:, 2 * p * LANES:(2 * p + 1) * LANES] = q[:, lanes]
        q_ref[:, (2 * p + 1) * LANES:(2 * p + 2) * LANES] = bias_q
        k_ref[:, 2 * p * LANES:(2 * p + 1) * LANES] = k[:, lanes]
        k_ref[:, (2 * p + 1) * LANES:(2 * p + 2) * LANES] = bias_k


def _inproj(x2, mod3, gain, w_all, w_vt, bf_pad, seq):
    n, d = x2.shape
    tm = TOK_TILE
    tpb = seq // tm
    tri = jnp.tril(jnp.ones((tm, tm), BF16))
    pq, pk, bq, bk = _bias_lane_placement()
    tok = lambda w: pl.BlockSpec((tm, w), lambda i: (i, 0))
    const = lambda shape: pl.BlockSpec(shape, lambda i: (0,) * len(shape))
    qk_width = 2 * FOX_WIDTH
    return pl.pallas_call(
        functools.partial(_inproj_kernel, tpb),
        grid=(n // tm,),
        in_specs=[tok(d),
                  pl.BlockSpec((None, N_MOD, d), lambda i: (i // tpb, 0, 0)),
                  const((1, d)), const((d, _C_END)), const((FOX_WIDTH, d)), const((1, LANES)),
                  const((tm, tm)), const(pq.shape), const(pk.shape), const(bq.shape), const(bk.shape)],
        out_specs=[tok(qk_width), tok(qk_width), pl.BlockSpec((FOX_WIDTH, tm), lambda i: (0, i)),
                   tok(SSM_WIDTH),
                   pl.BlockSpec((SSM_GROUPS, tm // SSM_CHUNK, SSM_CHUNK * SSM_GROUP), lambda i: (0, i, 0)),
                   tok(d), tok(d)],
        out_shape=[jax.ShapeDtypeStruct((n, qk_width), BF16)] * 2
        + [jax.ShapeDtypeStruct((FOX_WIDTH, n), BF16)]
        + [jax.ShapeDtypeStruct((n, SSM_WIDTH), F32)]
        + [jax.ShapeDtypeStruct((SSM_GROUPS, n // SSM_CHUNK, SSM_CHUNK * SSM_GROUP), BF16)]
        + [jax.ShapeDtypeStruct((n, d), BF16)] * 2,
        scratch_shapes=[pltpu.VMEM((SUBLANES, LANES), F32),
                        pltpu.VMEM((SSM_WIDTH // LANES, tm, LANES), F32)],
        compiler_params=_params(("arbitrary",)),
        name="inproj",
    )(x2, mod3, gain, w_all, w_vt, bf_pad, tri, jnp.asarray(pq, BF16), jnp.asarray(pk, BF16),
      jnp.asarray(bq), jnp.asarray(bk))


def _attn_kernel(q_ref, k_ref, vt_ref, o_ref, m_ref, acc_ref, sa_ref, sb_ref):
    tq, tk = ATT_Q_TILE, ATT_K_TILE
    seq = q_ref.shape[0]
    nq = seq // tq
    half = FOX_HEAD_DIM
    lane = lax.broadcasted_iota(jnp.int32, (1, 2 * LANES), 1)
    bias0 = LANES + 16 * pl.program_id(1)
    own0 = (lane < half) | ((lane >= bias0) & (lane < bias0 + 8))
    own1 = ((lane >= half) & (lane < LANES)) | ((lane >= bias0 + 8) & (lane < bias0 + 16))
    ones_rows = jnp.ones((2 * SUBLANES, tk), BF16)
    key_in_tile = lax.broadcasted_iota(jnp.int32, (tk, 2 * tq), 0)
    qry_in_tile = lax.broadcasted_iota(jnp.int32, (tk, 2 * tq), 1) & (tq - 1)
    bufs = (sa_ref, sb_ref)
    key_tiles = [-(-(i + 1) * tq // tk) for i in range(nq)]
    steps = [(i, j) for i in range(nq) for j in range(key_tiles[i])]
    q_cache = {}

    def q_both(i):
        if i not in q_cache:
            q = q_ref[i * tq:(i + 1) * tq, :]
            zq = jnp.zeros_like(q)
            q_cache[i] = jnp.concatenate([jnp.where(own0, q, zq), jnp.where(own1, q, zq)], axis=0)
        return q_cache[i]

    def scores(n):
        i, j = steps[n]
        s = lax.dot_general(k_ref[j * tk:(j + 1) * tk, :], q_both(i), (((1,), (1,)), ((), ())),
                            preferred_element_type=F32)
        if j * tk + tk - 1 > i * tq:
            s = jnp.where(key_in_tile + (j * tk - i * tq) <= qry_in_tile, s, NEG_BIG)
        bufs[n % 2][...] = s

    scores(0)
    for n, (i, j) in enumerate(steps):
        if n + 1 < len(steps):
            scores(n + 1)
        s_ref = bufs[n % 2]
        va = jnp.concatenate([vt_ref[:, j * tk:(j + 1) * tk], ones_rows], axis=0)
        for h in range(2):
            cols = slice(h * tq, (h + 1) * tq)
            if j == 0:
                m_new = jnp.max(s_ref[:, cols], axis=0, keepdims=True)
                p = jnp.exp(s_ref[:, cols] - m_new).astype(BF16)
                acc_ref[:, cols] = jnp.dot(va, p, preferred_element_type=F32)
            else:
                m_old = m_ref[:, cols]
                m_new = jnp.maximum(m_old, jnp.max(s_ref[:, cols], axis=0, keepdims=True))
                alpha = jnp.exp(m_old - m_new)
                p = jnp.exp(s_ref[:, cols] - m_new).astype(BF16)
                acc_ref[:, cols] = alpha * acc_ref[:, cols] + jnp.dot(va, p, preferred_element_type=F32)
            m_ref[:, cols] = m_new
        if j == key_tiles[i] - 1:
            acc = acc_ref[...]
            o_t = jnp.concatenate([acc[0:half, 0:tq] / acc[LANES:LANES + 1, 0:tq],
                                   acc[half:LANES, tq:2 * tq] / acc[LANES:LANES + 1, tq:2 * tq]], axis=0)
            o_ref[i * tq:(i + 1) * tq, :] = o_t.T.astype(o_ref.dtype)


def _attention(q, k, v_t, bsz, seq):
    n = q.shape[0]
    t = ATT_Q_TILE
    return pl.pallas_call(
        _attn_kernel,
        grid=(bsz, HEAD_PAIRS),
        in_specs=[pl.BlockSpec((seq, 2 * LANES), lambda b, p: (b, p)),
                  pl.BlockSpec((seq, 2 * LANES), lambda b, p: (b, p)),
                  pl.BlockSpec((LANES, seq), lambda b, p: (p, b))],
        out_specs=pl.BlockSpec((seq, LANES), lambda b, p: (b, p)),
        out_shape=jax.ShapeDtypeStruct((n, FOX_WIDTH), BF16),
        scratch_shapes=[pltpu.VMEM((1, 2 * t), F32), pltpu.VMEM((LANES + 2 * SUBLANES, 2 * t), F32),
                        pltpu.VMEM((ATT_K_TILE, 2 * t), F32), pltpu.VMEM((ATT_K_TILE, 2 * t), F32)],
        compiler_params=_params(("arbitrary", "arbitrary")),
        name="attn",
    )(q, k, v_t)


def _ssm_prep_kernel(lrow_ref, lcol_ref, ldt_ref, btr_ref, bti_ref, ctr_ref, cti_ref,
                     toep_ref, bst_ref, bsw_ref, cpw_ref, a_ref):
    p8 = pl.program_id(0) % 8
    t_len, grp = SSM_CHUNK, SSM_GROUP
    dt = jnp.exp(ldt_ref[...])
    lr, li = jnp.minimum(lrow_ref[0:1, :], LAMBDA_RE_MAX), lrow_ref[1:2, :]

    def powers(steps, re, im):
        mag = jnp.exp(steps * (re * dt))
        return mag * jnp.cos(steps * (im * dt)), mag * jnp.sin(steps * (im * dt))

    a_re, a_im = powers(1.0, lr, li)
    den = lr * lr + li * li
    nr = a_re - 1.0
    co_re = (nr * lr + a_im * li) / den
    co_im = (a_im * lr - nr * li) / den
    bbt_re = co_re * btr_ref[...] - co_im * bti_ref[...]
    bbt_im = co_re * bti_ref[...] + co_im * btr_ref[...]

    lag = (lax.broadcasted_iota(jnp.int32, (1, t_len * grp), 1) // grp).astype(F32)
    lcr, lci = jnp.minimum(lcol_ref[:, 0:1], LAMBDA_RE_MAX), lcol_ref[:, 1:2]

    def c_times_power(steps):
        p_re, p_im = powers(steps, lcr, lci)
        return (ctr_ref[...] * p_re - cti_ref[...] * p_im, ctr_ref[...] * p_im + cti_ref[...] * p_re)

    wt_re, wt_im = c_times_power(lag)
    kern = (jnp.dot(bbt_re, wt_re, precision=HIGHEST, preferred_element_type=F32)
            - jnp.dot(bbt_im, wt_im, precision=HIGHEST, preferred_element_type=F32))

    lane = lax.broadcasted_iota(jnp.int32, (1, LANES), 1)
    col_shift = p8 * grp

    def store_cols(ref, rows, lo_half, hi_half):
        ref[rows, 0:LANES] = pltpu.roll(lo_half, col_shift, axis=1).astype(ref.dtype)
        ref[rows, LANES:2 * LANES] = pltpu.roll(hi_half, col_shift, axis=1).astype(ref.dtype)

    def slot_rows(s):
        half, s8 = divmod(s, 8)
        return pl.ds(pl.multiple_of((8 * half + (s8 + p8) % 8) * grp, grp), grp)

    back = (t_len - 1 - lax.broadcasted_iota(jnp.int32, (t_len, 1), 0)).astype(F32)
    e_re, e_im = powers(back, lr, li)
    zero = jnp.zeros((grp, LANES), F32)
    k_lo, k_hi = kern[:, 0:LANES], kern[:, LANES:2 * LANES]
    for s in range(t_len):
        half, s8 = divmod(s, 8)
        keep = lane >= s8 * grp
        r_lo = pltpu.roll(k_lo, s8 * grp, axis=1) if s8 else k_lo
        r_hi = pltpu.roll(k_hi, s8 * grp, axis=1) if s8 else k_hi
        if half == 0:
            lo, hi = jnp.where(keep, r_lo, 0.0), jnp.where(keep, r_hi, r_lo)
        else:
            lo, hi = zero, jnp.where(keep, r_lo, 0.0)
        store_cols(toep_ref, slot_rows(s), lo, hi)
        es_re, es_im = e_re[s:s + 1, :], e_im[s:s + 1, :]
        bs_re = es_re * bbt_re - es_im * bbt_im
        bs_im = es_re * bbt_im + es_im * bbt_re
        bst_ref[slot_rows(s), :] = jnp.concatenate([bs_re, bs_im], axis=1).astype(bst_ref.dtype)
        bsw_ref[slot_rows(s), :] = jnp.concatenate([bs_im, bs_re], axis=1).astype(bsw_ref.dtype)

    w1_re, w1_im = c_times_power(lag + 1.0)
    store_cols(cpw_ref, pl.ds(0, SSM_STATE), w1_re[:, 0:LANES], w1_re[:, LANES:2 * LANES])
    store_cols(cpw_ref, pl.ds(SSM_STATE, SSM_STATE), -w1_im[:, 0:LANES], -w1_im[:, LANES:2 * LANES])
    s_re, s_im = powers(float(t_len), lr, li)
    a_ref[0:1, :] = jnp.concatenate([s_re, s_re], axis=1)
    a_ref[1:2, :] = jnp.concatenate([-s_im, s_im], axis=1)


def _ssm_prep(lambda_re, lambda_im, log_dt, b_re, b_im, c_re, c_im):
    width = SSM_CHUNK * SSM_GROUP
    lam_row = jnp.stack([lambda_re, lambda_im], axis=1)
    tiled = lambda c: jnp.tile(c.transpose(0, 2, 1), (1, 1, SSM_CHUNK))
    per = lambda a, b: pl.BlockSpec((None, a, b), lambda g: (g, 0, 0))
    return pl.pallas_call(
        _ssm_prep_kernel,
        grid=(SSM_GROUPS,),
        in_specs=[per(2, SSM_STATE), per(SSM_STATE, 2), per(1, 1), per(SSM_GROUP, SSM_STATE),
                  per(SSM_GROUP, SSM_STATE), per(SSM_STATE, width), per(SSM_STATE, width)],
        out_specs=[per(width, width), per(width, 2 * SSM_STATE), per(width, 2 * SSM_STATE),
                   per(2 * SSM_STATE, width), per(2, 2 * SSM_STATE)],
        out_shape=[jax.ShapeDtypeStruct((SSM_GROUPS, width, width), BF16),
                   jax.ShapeDtypeStruct((SSM_GROUPS, width, 2 * SSM_STATE), BF16),
                   jax.ShapeDtypeStruct((SSM_GROUPS, width, 2 * SSM_STATE), BF16),
                   jax.ShapeDtypeStruct((SSM_GROUPS, 2 * SSM_STATE, width), BF16),
                   jax.ShapeDtypeStruct((SSM_GROUPS, 2, 2 * SSM_STATE), F32)],
        compiler_params=_params(("arbitrary",)),
        name="ssm_prep",
    )(lam_row, lam_row.transpose(0, 2, 1), log_dt.reshape(SSM_GROUPS, 1, 1),
      b_re.transpose(0, 2, 1), b_im.transpose(0, 2, 1), tiled(c_re), tiled(c_im))


SSM_GROUPS_PER_STEP = 4


def _ssm_kernel(n_chunks, bsz, u_ref, toep_ref, bst_ref, bsw_ref, cpw_ref, a_ref, y_ref,
                contrib_ref, cswap_ref, xprev_ref):
    groups = u_ref.shape[0]
    for k in range(groups):
        u = u_ref[k]
        contrib_ref[k] = jnp.dot(u, bst_ref[k], preferred_element_type=F32)
        cswap_ref[k] = jnp.dot(u, bsw_ref[k], preferred_element_type=F32)
    a1 = [a_ref[k, 0:1, :] for k in range(groups)]
    a2 = [a_ref[k, 1:2, :] for k in range(groups)]

    def step(n, carry):
        rows = pl.ds(n, bsz, stride=n_chunks)
        new = []
        for k in range(groups):
            x, xs = carry[2 * k], carry[2 * k + 1]
            xprev_ref[k, rows, :] = x
            new.append(a1[k] * x + a2[k] * xs + contrib_ref[k, rows, :])
            new.append(a1[k] * xs - a2[k] * x + cswap_ref[k, rows, :])
        return tuple(new)

    zero = jnp.zeros((bsz, 2 * SSM_STATE), F32)
    lax.fori_loop(0, n_chunks, step, (zero,) * (2 * groups), unroll=2)
    for k in range(groups):
        y_ref[k] = (jnp.dot(u_ref[k], toep_ref[k], preferred_element_type=F32)
                    + jnp.dot(xprev_ref[k].astype(BF16), cpw_ref[k], preferred_element_type=F32))


def _ssm(u_flat, toep, b_state, b_swap, c_pow, a_step, bsz):
    g, rows, w = u_flat.shape
    gb = SSM_GROUPS_PER_STEP
    per = lambda a, b: pl.BlockSpec((gb, a, b), lambda i: (i, 0, 0))
    state = pltpu.VMEM((gb, rows, 2 * SSM_STATE), F32)
    return pl.pallas_call(
        functools.partial(_ssm_kernel, rows // bsz, bsz),
        grid=(g // gb,),
        in_specs=[per(rows, w), per(w, w), per(w, 2 * SSM_STATE), per(w, 2 * SSM_STATE),
                  per(2 * SSM_STATE, w), per(2, 2 * SSM_STATE)],
        out_specs=per(rows, w),
        out_shape=jax.ShapeDtypeStruct((g, rows, w), F32),
        scratch_shapes=[state, state, state],
        compiler_params=_params(("arbitrary",)),
        name="ssm",
    )(u_flat, toep, b_state, b_swap, c_pow, a_step)


ROW_SLABS = D_MODEL // LANES // 2
_HIGH_HALF = 0xFFFF0000


def _store_row_tiles(ref, value):
    rows = value.shape[0]
    bits = lax.bitcast_convert_type(value.astype(BF16).astype(F32), jnp.uint32)
    for j in range(ROW_SLABS):
        low = bits[:, j * LANES:(j + 1) * LANES] >> 16
        high = bits[:, (j + ROW_SLABS) * LANES:(j + ROW_SLABS + 1) * LANES] & jnp.uint32(_HIGH_HALF)
        ref[pl.ds(j, rows, stride=ROW_SLABS), :] = high | low


def _load_row_tiles(ref, rows):
    words = [ref[pl.ds(j, rows, stride=ROW_SLABS), :] for j in range(ROW_SLABS)]
    low = [lax.bitcast_convert_type(w << 16, F32) for w in words]
    high = [lax.bitcast_convert_type(w & jnp.uint32(_HIGH_HALF), F32) for w in words]
    return jnp.concatenate(low + high, axis=1)


def _row_tile_copy(src_ref, src_row, dst_ref, dst_row, sem):
    src = src_ref.at[pl.ds(pl.multiple_of(src_row * ROW_SLABS, ROW_SLABS), ROW_SLABS), :]
    dst = dst_ref.at[pl.ds(pl.multiple_of(dst_row * ROW_SLABS, ROW_SLABS), ROW_SLABS), :]
    return pltpu.make_async_copy(src, dst, sem)


def _mix_kernel(x_ref, of_ref, yf_ref, u_ref, ga_ref, gb_ref, mod_ref, dsk_ref, wglu_ref, wfox_ref,
                wssm_ref, wo_ref, g2_ref, wr_ref, br_ref, x1_ref, h2_ref, lg_ref, ytok_ref):
    sub = MIX_SUB_TILE
    for s in range(x_ref.shape[0] // sub):
        rows = pl.ds(s * sub, sub)
        chunks = pl.ds(s * sub // SSM_CHUNK, sub // SSM_CHUNK)
        _to_token_major(yf_ref.at[:, chunks, :], ytok_ref.at[s], sub // SSM_CHUNK)
        y_ssm = jnp.concatenate([ytok_ref[s, j] for j in range(SSM_WIDTH // LANES)], axis=1)
        y = y_ssm + dsk_ref[...] * u_ref[rows, :]
        y = 0.5 * y * (1.0 + jnp.tanh(math.sqrt(2.0 / math.pi) * (y + 0.044715 * (y * y * y))))
        gl = jnp.dot(y.astype(BF16), wglu_ref[...], preferred_element_type=F32)
        o_ssm = gl[:, :SSM_WIDTH] * _sigmoid(gl[:, SSM_WIDTH:])
        merged = (ga_ref[rows, :].astype(F32) * jnp.dot(of_ref[rows, :], wfox_ref[...],
                                                        preferred_element_type=F32)
                  + gb_ref[rows, :].astype(F32) * jnp.dot(o_ssm.astype(BF16), wssm_ref[...],
                                                          preferred_element_type=F32))
        x1 = x_ref[rows, :] + mod_ref[2:3, :] * jnp.dot(merged.astype(BF16), wo_ref[...],
                                                         preferred_element_type=F32)
        x1_ref[rows, :] = x1
        h2 = _rms_modulate(x1, g2_ref[...], mod_ref[3:4, :], mod_ref[4:5, :])
        _store_row_tiles(h2_ref.at[pl.ds(s * sub * ROW_SLABS, sub * ROW_SLABS), :], h2)
        a1 = _top_bits(h2)
        a2 = _top_bits(h2 - a1)
        lhs = jnp.concatenate([a1, a1, a2], axis=1).astype(BF16)
        lg_ref[rows, :] = jnp.dot(lhs, wr_ref[...], preferred_element_type=F32) + br_ref[...]


def _mix(x2, o_fox, y_flat, u, sga, sgb, mod3, d_skip, w_glu, w_fox, w_ssm, w_o, g2, w_r, b_r, seq):
    n, d = x2.shape
    tm = MIX_TILE
    tpb = seq // tm
    tok = lambda w: pl.BlockSpec((tm, w), lambda i: (i, 0))
    const = lambda a: pl.BlockSpec(a.shape, lambda i: (0,) * a.ndim)
    flat = pl.BlockSpec((SSM_GROUPS, tm // SSM_CHUNK, SSM_CHUNK * SSM_GROUP), lambda i: (0, i, 0))
    return pl.pallas_call(
        _mix_kernel,
        grid=(n // tm,),
        in_specs=[tok(d), tok(FOX_WIDTH), flat, tok(SSM_WIDTH), tok(d), tok(d),
                  pl.BlockSpec((None, N_MOD, d), lambda i: (i // tpb, 0, 0)),
                  const(d_skip), const(w_glu), const(w_fox), const(w_ssm), const(w_o), const(g2),
                  const(w_r), const(b_r)],
        out_specs=[tok(d), pl.BlockSpec((tm * ROW_SLABS, LANES), lambda i: (i, 0)), tok(LANES)],
        out_shape=[jax.ShapeDtypeStruct((n, d), F32), jax.ShapeDtypeStruct((n * ROW_SLABS, LANES), jnp.uint32),
                   jax.ShapeDtypeStruct((n, LANES), F32)],
        scratch_shapes=[pltpu.VMEM((tm // MIX_SUB_TILE, SSM_WIDTH // LANES, MIX_SUB_TILE, LANES), F32)],
        compiler_params=_params(("arbitrary",)),
        name="mix",
    )(x2, o_fox, y_flat, u, sga, sgb, mod3, d_skip, w_glu, w_fox, w_ssm, w_o, g2, w_r, b_r)


def _route_kernel(lg_ref, tri_ref, er_ref, wt_ref, cnt_ref, zero_ref, carry_ref):
    i = pl.program_id(0)
    zero_ref[...] = jnp.zeros_like(zero_ref)

    @pl.when(i == 0)
    def _():
        carry_ref[...] = jnp.zeros_like(carry_ref)

    lg = lg_ref[...].T
    tm = lg.shape[1]
    row = lax.broadcasted_iota(jnp.int32, (LANES, tm), 0)
    neg = jnp.full_like(lg, -jnp.inf)

    def first_argmax(vals):
        mx = jnp.max(vals, axis=0, keepdims=True)
        ix = jnp.min(jnp.where(vals == mx, row, LANES), axis=0, keepdims=True)
        return mx, ix

    is_group = row < N_GROUPS
    g_max, gi = first_argmax(jnp.where(is_group, lg, neg))
    g_sum = jnp.sum(jnp.where(is_group, jnp.exp(lg - g_max), 0.0), axis=0, keepdims=True)
    p_group = 1.0 / g_sum
    lo = N_GROUPS + EXPERTS_PER_GROUP * gi
    in_group = (row >= lo) & (row < lo + EXPERTS_PER_GROUP)
    cand = jnp.where(in_group, lg, neg)
    v1, i1 = first_argmax(cand)
    v2, i2 = first_argmax(jnp.where(row == i1, neg, cand))
    tt = jnp.exp(v2 - v1)
    w1 = p_group / (1.0 + tt)
    w2 = p_group * tt / (1.0 + tt)
    e1 = i1 - N_GROUPS
    e2 = i2 - N_GROUPS
    sel1 = row == e1
    sel2 = row == e2
    onehot = (sel1 | sel2).astype(F32)
    before = jnp.dot(onehot.astype(BF16), tri_ref[...], preferred_element_type=F32) + carry_ref[:, 0:1]
    r1 = jnp.sum(jnp.where(sel1, before, 0.0), axis=0, keepdims=True).astype(jnp.int32)
    r2 = jnp.sum(jnp.where(sel2, before, 0.0), axis=0, keepdims=True).astype(jnp.int32)
    total = before[:, tm - 1:tm] + onehot[:, tm - 1:tm]
    carry_ref[...] = jnp.broadcast_to(total, carry_ref.shape)
    cnt_ref[...] = jnp.broadcast_to(total, cnt_ref.shape)
    slot = lax.broadcasted_iota(jnp.int32, (SUBLANES, tm), 0)
    er_ref[...] = jnp.where(slot == 0, e1, jnp.where(slot == 1, e2, jnp.where(slot == 2, r1, r2)))
    wt_ref[...] = jnp.where(row == 0, w1, jnp.where(row == 1, w2, 0.0)).T


def _route(logits, zero_rows):
    n = logits.shape[0]
    tm = TOK_TILE
    zero_block = zero_rows // (n // tm)
    assert zero_block * (n // tm) == zero_rows and zero_block % SUBLANES == 0
    tri = jnp.triu(jnp.ones((tm, tm), BF16), k=1)
    tok = pl.BlockSpec((tm, LANES), lambda i: (i, 0))
    return pl.pallas_call(
        _route_kernel,
        grid=(n // tm,),
        in_specs=[tok, pl.BlockSpec((tm, tm), lambda i: (0, 0))],
        out_specs=[pl.BlockSpec((SUBLANES, tm), lambda i: (0, i)), tok,
                   pl.BlockSpec((LANES, LANES), lambda i: (0, 0)),
                   pl.BlockSpec((zero_block, LANES), lambda i: (i, 0))],
        out_shape=[jax.ShapeDtypeStruct((SUBLANES, n), jnp.int32), jax.ShapeDtypeStruct((n, LANES), F32),
                   jax.ShapeDtypeStruct((LANES, LANES), F32),
                   jax.ShapeDtypeStruct((zero_rows, LANES), jnp.uint32)],
        scratch_shapes=[pltpu.VMEM((LANES, LANES), F32)],
        compiler_params=_params(("arbitrary",)),
        name="route",
    )(logits, tri)


ISSUE_UNROLL = 8


def _dispatch_kernel(n_steps, dest_ref, h_ref, rows_in_ref, rows_ref, stage_ref, sem):
    del rows_in_ref
    i = pl.program_id(0)
    tm = h_ref.shape[0] // ROW_SLABS
    cur = i % 2
    stage_ref[cur] = h_ref[...]

    def issue(g, c):
        for j in range(ISSUE_UNROLL):
            t = g * ISSUE_UNROLL + j
            _row_tile_copy(stage_ref.at[cur], t, rows_ref, dest_ref[0, 0, t], sem.at[cur]).start(priority=0)
            _row_tile_copy(stage_ref.at[cur], t, rows_ref, dest_ref[0, 0, tm + t],
                           sem.at[cur]).start(priority=1)
        return c

    lax.fori_loop(0, tm // ISSUE_UNROLL, issue, 0)

    def drain(which):
        for _ in range(2):
            pltpu.make_async_copy(stage_ref.at[which], rows_ref.at[pl.ds(0, tm * ROW_SLABS), :],
                                  sem.at[which]).wait()

    @pl.when(i > 0)
    def _():
        drain(1 - cur)

    @pl.when(i == n_steps - 1)
    def _():
        drain(cur)


def _dispatch(dest3, h2_tiles, rows_zero):
    tm = MOVE_TILE
    n = h2_tiles.shape[0] // ROW_SLABS
    return pl.pallas_call(
        functools.partial(_dispatch_kernel, n // tm),
        grid=(n // tm,),
        in_specs=[pl.BlockSpec((1, 1, 2 * tm), lambda i: (i, 0, 0), memory_space=pltpu.SMEM),
                  pl.BlockSpec((tm * ROW_SLABS, LANES), lambda i: (i, 0)),
                  pl.BlockSpec(memory_space=pl.ANY)],
        out_specs=pl.BlockSpec(memory_space=pl.ANY),
        out_shape=jax.ShapeDtypeStruct(rows_zero.shape, rows_zero.dtype),
        scratch_shapes=[pltpu.VMEM((2, tm * ROW_SLABS, LANES), jnp.uint32), pltpu.SemaphoreType.DMA((2,))],
        input_output_aliases={2: 0},
        compiler_params=_params(("arbitrary",)),
        name="dispatch",
    )(dest3, h2_tiles, rows_zero)


def _combine_kernel(n_steps, dest_ref, dnext_ref, x1_ref, wt_ref, mod_ref, gf_ref, yr_ref, o_ref,
                    buf_ref, sem):
    i = pl.program_id(0)
    tm = x1_ref.shape[0]

    def gather(idx_ref, which):
        def issue(g, c):
            for j in range(ISSUE_UNROLL):
                t = g * ISSUE_UNROLL + j
                _row_tile_copy(yr_ref, idx_ref[0, 0, t], buf_ref.at[which, 0], t,
                               sem.at[which]).start(priority=0)
                _row_tile_copy(yr_ref, idx_ref[0, 0, tm + t], buf_ref.at[which, 1], t,
                               sem.at[which]).start(priority=1)
            return c

        lax.fori_loop(0, tm // ISSUE_UNROLL, issue, 0)

    cur = i % 2

    @pl.when(i == 0)
    def _():
        gather(dest_ref, 0)

    @pl.when(i + 1 < n_steps)
    def _():
        gather(dnext_ref, 1 - cur)

    for slot in range(2):
        pltpu.make_async_copy(yr_ref.at[pl.ds(0, tm * ROW_SLABS), :], buf_ref.at[cur, slot],
                              sem.at[cur]).wait()
    wt = wt_ref[...]
    moe = (wt[:, 0:1] * _load_row_tiles(buf_ref.at[cur, 0], tm)
           + wt[:, 1:2] * _load_row_tiles(buf_ref.at[cur, 1], tm))
    x = x1_ref[...] + mod_ref[5:6, :] * moe
    ms = jnp.mean(x * x, axis=-1, keepdims=True)
    o_ref[...] = (x * lax.rsqrt(ms + RMS_EPS)) * gf_ref[...]


def _combine(dest3, x1, wts, mod3, final_g, y_rows, seq):
    n, d = x1.shape
    tm = MOVE_TILE
    tpb = seq // tm
    n_steps = n // tm
    idx_spec = lambda f: pl.BlockSpec((1, 1, 2 * tm), f, memory_space=pltpu.SMEM)
    return pl.pallas_call(
        functools.partial(_combine_kernel, n_steps),
        grid=(n_steps,),
        in_specs=[idx_spec(lambda i: (i, 0, 0)),
                  idx_spec(lambda i: (jnp.minimum(i + 1, n_steps - 1), 0, 0)),
                  pl.BlockSpec((tm, d), lambda i: (i, 0)),
                  pl.BlockSpec((tm, LANES), lambda i: (i, 0)),
                  pl.BlockSpec((None, N_MOD, d), lambda i: (i // tpb, 0, 0)),
                  pl.BlockSpec((1, d), lambda i: (0, 0)),
                  pl.BlockSpec(memory_space=pl.ANY)],
        out_specs=pl.BlockSpec((tm, d), lambda i: (i, 0)),
        out_shape=jax.ShapeDtypeStruct((n, d), F32),
        scratch_shapes=[pltpu.VMEM((2, 2, tm * ROW_SLABS, LANES), jnp.uint32),
                        pltpu.SemaphoreType.DMA((2,))],
        compiler_params=_params(("arbitrary",)),
        name="combine",
    )(dest3, dest3, x1, wts, mod3, final_g, y_rows)


def _expert_kernel(be_ref, seg_ref, nxt_ref, nv_ref, x_ref, wg_hbm, wu_hbm, wd_hbm, y_ref,
                   wg_buf, wu_buf, wd_buf, wgb_ref, wub_ref, wdb_ref, sem):
    i = pl.program_id(0)
    valid = i < nv_ref[0]
    first = (i == 0) | (be_ref[i] != be_ref[jnp.maximum(i - 1, 0)])
    slot = seg_ref[i] % 2

    def weight_copies(e, s):
        return [pltpu.make_async_copy(hbm.at[e], buf.at[s], sem.at[s])
                for hbm, buf in ((wg_hbm, wg_buf), (wu_hbm, wu_buf), (wd_hbm, wd_buf))]

    @pl.when(valid & (i == 0))
    def _():
        for c in weight_copies(be_ref[0], 0):
            c.start()

    @pl.when(valid & first)
    def _():
        for c in weight_copies(be_ref[i], slot):
            c.wait()

        @pl.when(nxt_ref[i] >= 0)
        def _():
            for c in weight_copies(nxt_ref[i], 1 - slot):
                c.start()

        wgb_ref[...] = wg_buf[slot].astype(BF16)
        wub_ref[...] = wu_buf[slot].astype(BF16)
        wdb_ref[...] = wd_buf[slot].astype(BF16)

    @pl.when(valid)
    def _():
        xb = _load_row_tiles(x_ref, x_ref.shape[0] // ROW_SLABS).astype(BF16)
        a = jnp.dot(xb, wgb_ref[...], preferred_element_type=F32)
        b = jnp.dot(xb, wub_ref[...], preferred_element_type=F32)
        hid = (a * _sigmoid(a)) * b
        _store_row_tiles(y_ref, jnp.dot(hid.astype(BF16), wdb_ref[...], preferred_element_type=F32))


def _experts(blk_e, n_valid, x_rows, w_gate, w_up, w_down):
    d = D_MODEL
    rows = x_rows.shape[0] // ROW_SLABS
    tb = ROW_BLOCK
    n_blocks = rows // tb
    idx = jnp.arange(n_blocks, dtype=jnp.int32)
    change = (idx == 0) | (blk_e != jnp.roll(blk_e, 1))
    seg = jnp.cumsum(change.astype(jnp.int32)) - 1
    later_start = (idx[None, :] > idx[:, None]) & change[None, :] & (idx[None, :] < n_valid[0])
    none = jnp.int32(N_EXPERTS)
    nxt = jnp.min(jnp.where(later_start, blk_e[None, :], none), axis=1)
    nxt = jnp.where(nxt == none, -1, nxt)
    row_spec = pl.BlockSpec((tb * ROW_SLABS, LANES),
                            lambda i, be, sg, nx, nv: (jnp.minimum(i, nv[0] - 1), 0))
    grid_spec = pltpu.PrefetchScalarGridSpec(
        num_scalar_prefetch=4,
        grid=(n_blocks,),
        in_specs=[row_spec, pl.BlockSpec(memory_space=pl.ANY), pl.BlockSpec(memory_space=pl.ANY),
                  pl.BlockSpec(memory_space=pl.ANY)],
        out_specs=row_spec,
        scratch_shapes=[pltpu.VMEM((2, d, D_EXPERT), F32), pltpu.VMEM((2, d, D_EXPERT), F32),
                        pltpu.VMEM((2, D_EXPERT, d), F32),
                        pltpu.VMEM((d, D_EXPERT), BF16), pltpu.VMEM((d, D_EXPERT), BF16),
                        pltpu.VMEM((D_EXPERT, d), BF16), pltpu.SemaphoreType.DMA((2,))],
    )
    return pl.pallas_call(
        _expert_kernel,
        grid_spec=grid_spec,
        out_shape=jax.ShapeDtypeStruct(x_rows.shape, x_rows.dtype),
        input_output_aliases={4: 0},
        compiler_params=_params(("arbitrary",)),
        name="experts",
    )(blk_e, seg, nxt.astype(jnp.int32), n_valid, x_rows, w_gate, w_up, w_down)


def kernel(x, c, w_ada, b_ada, norm_mix_g, w_in, b_forget, w_out_fox, lambda_re, lambda_im, log_dt,
           ssm_b_re, ssm_b_im, ssm_c_re, ssm_c_im, d_skip, w_glu, w_out_ssm, w_o, norm_ffn_g,
           w_router_group, b_router_group, w_router_expert, b_router_expert, w_gate_e, w_up_e,
           w_down_e, final_g):
    bsz, seq, d = x.shape
    n = bsz * seq
    assert w_ada.shape[0] == 1, "the final RMSNorm is fused into the (single) layer's combine kernel"
    xc = x.reshape(n, d)
    for l in range(1):
        mod3 = _mod(c, w_ada[l], b_ada[l]).reshape(bsz, N_MOD, d)

        wi = w_in[l]
        s_q, s_k, s_v, s_f, s_u, s_ga = 512, 1024, 1536, 1544, 2056, 3080
        scale = FOX_HEAD_DIM ** -0.5
        w_all = jnp.concatenate(
            [wi[:, :s_q] * scale, wi[:, s_q:s_k], wi[:, s_f:s_u], wi[:, s_u:s_ga],
             wi[:, s_ga:], jnp.pad(wi[:, s_v:s_f], ((0, 0), (0, LANES - FOX_HEADS)))],
            axis=1).astype(BF16)
        w_vt = wi[:, s_k:s_v].T.astype(BF16)
        bf_pad = jnp.pad(b_forget[l], (0, LANES - FOX_HEADS)).reshape(1, LANES)
        q, k, v_t, u, u_flat, sga, sgb = _inproj(xc, mod3, norm_mix_g[l].reshape(1, d), w_all, w_vt,
                                                 bf_pad, seq)

        o_fox = _attention(q, k, v_t, bsz, seq)

        toep, b_state, b_swap, c_pow, a_step = _ssm_prep(
            lambda_re[l], lambda_im[l], log_dt[l], ssm_b_re[l], ssm_b_im[l], ssm_c_re[l], ssm_c_im[l])
        y_flat = _ssm(u_flat, toep, b_state, b_swap, c_pow, a_step, bsz)

        w_r = jnp.pad(jnp.concatenate([w_router_group[l], w_router_expert[l]], axis=1),
                      ((0, 0), (0, LANES - N_GROUPS - N_EXPERTS)))
        w_r1 = _top_bits(w_r)
        w_r2 = _top_bits(w_r - w_r1)
        w_r = jnp.concatenate([w_r1, w_r2, w_r1], axis=0).astype(BF16)
        b_r = jnp.pad(jnp.concatenate([b_router_group[l], b_router_expert[l]]),
                      (0, LANES - N_GROUPS - N_EXPERTS)).reshape(1, LANES)
        x1, h2, logits = _mix(xc, o_fox, y_flat, u, sga, sgb, mod3, d_skip[l].reshape(1, SSM_WIDTH),
                              w_glu[l].astype(BF16), w_out_fox[l].astype(BF16),
                              w_out_ssm[l].astype(BF16), w_o[l].astype(BF16),
                              norm_ffn_g[l].reshape(1, d), w_r, b_r, seq)

        rows = 2 * n + N_EXPERTS * ROW_BLOCK
        er, wts, cnt, rows_zero = _route(logits, rows * ROW_SLABS)
        counts = cnt[:N_EXPERTS, 0].astype(jnp.int32)
        pcounts = ((counts + ROW_BLOCK - 1) // ROW_BLOCK) * ROW_BLOCK
        pends = jnp.cumsum(pcounts)
        pstarts = pends - pcounts
        hit = er[0:2, None, :] == jnp.arange(N_EXPERTS, dtype=jnp.int32)[None, :, None]
        dest = jnp.sum(jnp.where(hit, pstarts[None, :, None], 0), axis=1) + er[2:4]
        n_blocks = rows // ROW_BLOCK
        blk_start = jnp.arange(n_blocks, dtype=jnp.int32) * ROW_BLOCK
        blk_e = jnp.minimum(jnp.sum((pends[None, :] <= blk_start[:, None]).astype(jnp.int32), axis=1),
                            N_EXPERTS - 1)
        n_valid = (pends[-1:] // ROW_BLOCK).astype(jnp.int32)
        dest3 = (dest.astype(jnp.int32).reshape(2, n // MOVE_TILE, MOVE_TILE).transpose(1, 0, 2)
                 .reshape(n // MOVE_TILE, 1, 2 * MOVE_TILE))

        x_rows = _dispatch(dest3, h2, rows_zero)
        y_rows = _experts(blk_e, n_valid, x_rows, w_gate_e[l], w_up_e[l], w_down_e[l])
        xc = _combine(dest3, x1, wts, mod3, final_g.reshape(1, d), y_rows, seq)
    return xc.reshape(bsz, seq, d)
```
